```python
import jax, jax.numpy as jnp
from jax import lax
import numpy as np

D_MODEL = 1024
BATCH = 8
SEQ = 2048
DEPTH = 4
DEC_BATCH = 128
DEC_SEQ = 1
PAST_LEN = 16384
PAGE_SIZE = 128

N_EVEN = (DEPTH + 1) // 2
N_ODD = DEPTH // 2
D_MIX = D_MODEL
D_HALF = D_MIX // 2
CONV_W = 4
LRU_BLOCKS = 8
LRU_BLOCK = D_HALF // LRU_BLOCKS
LRU_C = 8.0
RET_HEADS = 4
RET_DH = D_HALF // RET_HEADS
CHUNK = 128
ROPE_BASE = 10000.0
ML_HEADS = 4
ML_DH = D_HALF // ML_HEADS
ML_QKV_BLOCK = 4
ML_NBLK = D_HALF // ML_QKV_BLOCK
RW_HEADS = 8
RW_DH = D_HALF // RW_HEADS
RW_DECAY_LORA = 32
RW_A_LORA = 32
RW_GATE_LORA = 96
RW_SHIFT_COLS = 3 * D_HALF + RW_DECAY_LORA + RW_A_LORA + RW_GATE_LORA
D_FF = 4 * D_MODEL
ALPHA = (2.0 * DEPTH) ** 0.25
BETA = (8.0 * DEPTH) ** -0.25
EVEN_IN = 6 * D_HALF
ODD_IN = 2 * D_HALF + RW_SHIFT_COLS
LN_EPS = 1e-5
GN_EPS = 1e-5
RW_GN_EPS = 64e-5

kernel_name = 'hybrid_rglru_retnet_mlstm_rwkv7_step'


def layer_norm(x, g, b):
    xf = x.astype(jnp.float32)
    mu = jnp.mean(xf, -1, keepdims=True)
    var = jnp.mean(jnp.square(xf - mu), -1, keepdims=True)
    return ((xf - mu) * lax.rsqrt(var + LN_EPS) * g + b).astype(x.dtype)


def head_norm(x, g, b, eps):
    xf = x.astype(jnp.float32)
    mu = jnp.mean(xf, -1, keepdims=True)
    var = jnp.mean(jnp.square(xf - mu), -1, keepdims=True)
    return (xf - mu) * lax.rsqrt(var + eps) * g + b


def chunk_len(T):
    return CHUNK if T % CHUNK == 0 else T


def causal_conv(x, buf, w, b):
    T = x.shape[1]
    xp = jnp.concatenate([buf.astype(x.dtype), x], axis=1)
    y = b + sum(w[i] * xp[:, i:i + T] for i in range(CONV_W))
    return y, xp[:, T:]


def block_diag(x, w):
    G, bi, bo = w.shape
    lead = x.shape[:-1]
    return jnp.einsum('...gi,gio->...go', x.reshape(*lead, G, bi), w).reshape(*lead, G * bo)


def rotary(x, pos):
    half = x.shape[-1] // 2
    inv = ROPE_BASE ** (-jnp.arange(half, dtype=jnp.float32) / half)
    ang = pos.astype(jnp.float32)[:, None] * inv[None, :]
    cos = jnp.cos(ang)[None, :, None, :]
    sin = jnp.sin(ang)[None, :, None, :]
    xf = x.astype(jnp.float32)
    x1, x2 = xf[..., :half], xf[..., half:]
    return jnp.concatenate([x1 * cos - x2 * sin, x1 * sin + x2 * cos], axis=-1)


def _lin_combine(left, right):
    a1, b1 = left
    a2, b2 = right
    return a1 * a2, a2 * b1 + b2


def rg_lru(x, h0, w_a, b_a, w_x, b_x, lam):
    xf = x.astype(jnp.float32)
    r = jax.nn.sigmoid(block_diag(xf, w_a) + b_a)
    i = jax.nn.sigmoid(block_diag(xf, w_x) + b_x)
    log_a = -LRU_C * r * jax.nn.softplus(-lam.astype(jnp.float32))
    a = jnp.exp(log_a)
    u = jnp.sqrt(-jnp.expm1(2.0 * log_a)) * (i * xf)
    u = u.at[:, 0].add(a[:, 0] * h0.astype(jnp.float32))
    _, h = lax.associative_scan(_lin_combine, (a, u), axis=1)
    return h, h[:, -1]


def retention_chunked(q, k, v, S0, log_gamma):
    B, T, H, _ = q.shape
    Dv = v.shape[-1]
    L = chunk_len(T)
    N = T // L
    to_chunks = lambda t: t.astype(jnp.float32).reshape(B, N, L, H, t.shape[-1]).transpose(1, 0, 3, 2, 4)
    idx = jnp.arange(L, dtype=jnp.float32)
    diff = idx[:, None] - idx[None, :]
    decay_mask = jnp.where(diff >= 0, jnp.exp(log_gamma[:, None, None] * jnp.maximum(diff, 0.0)), 0.0)
    q_decay = jnp.exp(log_gamma[:, None] * (idx + 1.0))[None, :, :, None]
    k_decay = jnp.exp(log_gamma[:, None] * (L - 1.0 - idx))[None, :, :, None]
    chunk_decay = jnp.exp(log_gamma * L)[None, :, None, None]

    def step(S, blk):
        qb, kb, vb = blk
        scores = jnp.einsum('bhld,bhmd->bhlm', qb, kb) * decay_mask
        out = jnp.einsum('bhlm,bhmv->bhlv', scores, vb) + jnp.einsum('bhld,bhdv->bhlv', qb, S) * q_decay
        S = S * chunk_decay + jnp.einsum('bhld,bhlv->bhdv', kb * k_decay, vb)
        return S, out

    S_fin, out = lax.scan(step, S0.astype(jnp.float32), (to_chunks(q), to_chunks(k), to_chunks(v)))
    return out.transpose(1, 0, 3, 2, 4).reshape(B, T, H, Dv), S_fin


def mlstm_chunked(q, k, v, log_i, log_f, C0, n0, m0):
    B, T, H, D = q.shape
    L = chunk_len(T)
    N = T // L
    f32 = jnp.float32
    to_chunks = lambda t: t.astype(f32).reshape(B, N, L, H, D).transpose(1, 0, 3, 2, 4)
    gate_chunks = lambda t: t.astype(f32).reshape(B, N, L, H).transpose(1, 0, 3, 2)
    causal = jnp.tril(jnp.ones((L, L), dtype=bool))

    def step(carry, blk):
        C, n, m = carry
        qb, kb, vb, li, lf = blk
        b = jnp.cumsum(lf, axis=-1)
        logD = jnp.where(causal, b[..., :, None] - b[..., None, :] + li[..., None, :], -jnp.inf)
        log_inter = b + m[..., None]
        m_t = jnp.maximum(jnp.max(logD, -1), log_inter)
        s = jnp.einsum('bhld,bhmd->bhlm', qb, kb) * jnp.exp(logD - m_t[..., None])
        w_inter = jnp.exp(log_inter - m_t)
        num = jnp.einsum('bhlm,bhmv->bhlv', s, vb) + w_inter[..., None] * jnp.einsum('bhld,bhdv->bhlv', qb, C)
        den = jnp.sum(s, -1) + w_inter * jnp.einsum('bhld,bhd->bhl', qb, n)
        h = num / jnp.maximum(jnp.abs(den), jnp.exp(-m_t))[..., None]
        m_new = m_t[..., -1]
        w_k = jnp.exp(b[..., -1:] - b + li - m_new[..., None])[..., None]
        w_C = jnp.exp(b[..., -1] + m - m_new)
        C = w_C[..., None, None] * C + jnp.einsum('bhld,bhlv->bhdv', kb * w_k, vb)
        n = w_C[..., None] * n + jnp.sum(kb * w_k, axis=2)
        return (C, n, m_new), h

    carry0 = (C0.astype(f32), n0.astype(f32), m0.astype(f32))
    (C, n, m), h = lax.scan(step, carry0, (to_chunks(q), to_chunks(k), to_chunks(v), gate_chunks(log_i), gate_chunks(log_f)))
    return h.transpose(1, 0, 3, 2, 4).reshape(B, T, H, D), C, n, m


def rwkv7_recurrence(r, w_log, k, v, a_vec, b_vec, S0):
    f32 = jnp.float32
    decay = jnp.exp(-jnp.exp(w_log.astype(f32)))
    seq = tuple(t.astype(f32).transpose(1, 0, 2, 3) for t in (r, decay, k, v, a_vec, b_vec))

    def step(S, inp):
        rt, dt, kt, vt, at, bt = inp
        sa = jnp.einsum('bhij,bhj->bhi', S, at)
        S = S * dt[:, :, None, :] + sa[..., :, None] * bt[..., None, :] + vt[..., :, None] * kt[..., None, :]
        return S, jnp.einsum('bhij,bhj->bhi', S, rt)

    S_fin, y = lax.scan(step, S0.astype(f32), seq)
    return y.transpose(1, 0, 2, 3), S_fin


def even_mixer(x, pos, j, st, W, log_gamma):
    B, T, _ = x.shape
    xa, ga, q, k, v, gb = jnp.split(x @ W['ev_w_in'][j], 6, axis=-1)
    xc, conv_new = causal_conv(xa, st['lru_conv'][j], W['lru_conv_w'][j], W['lru_conv_b'][j])
    h, h_new = rg_lru(xc, st['lru_h'][j], W['lru_wa'][j], W['lru_ba'][j], W['lru_wx'][j], W['lru_bx'][j], W['lru_lambda'][j])
    ya = jax.nn.gelu(ga) * h.astype(x.dtype)
    heads = lambda t: t.reshape(B, T, RET_HEADS, RET_DH)
    qh = rotary(heads(q), pos)
    kh = rotary(heads(k), pos) * RET_DH ** -0.5
    o, S_new = retention_chunked(qh, kh, heads(v), st['ret_S'][j], log_gamma)
    o = head_norm(o, W['ret_gn_g'][j], W['ret_gn_b'][j], GN_EPS).astype(x.dtype)
    yb = jax.nn.silu(gb) * o.reshape(B, T, D_HALF)
    y = jnp.concatenate([ya, yb], axis=-1) @ W['ev_w_out'][j]
    return y, (h_new, conv_new, S_new)


def odd_mixer(x, j, st, W):
    B, T, _ = x.shape
    f32 = jnp.float32
    p = x @ W['od_w_in'][j]
    xm, z, pr = p[..., :D_HALF], p[..., D_HALF:2 * D_HALF], p[..., 2 * D_HALF:]
    xc, ml_conv_new = causal_conv(xm, st['ml_conv'][j], W['ml_conv_w'][j], W['ml_conv_b'][j])
    xc = jax.nn.silu(xc)
    q = block_diag(xc, W['ml_wq'][j])
    k = block_diag(xc, W['ml_wk'][j])
    v = block_diag(xm, W['ml_wv'][j])
    gates = jnp.concatenate([q, k, v], axis=-1) @ W['ml_w_gate'][j] + W['ml_b_gate'][j]
    log_i = gates[..., :ML_HEADS].astype(f32)
    log_f = jax.nn.log_sigmoid(gates[..., ML_HEADS:].astype(f32))
    mh = lambda t: t.reshape(B, T, ML_HEADS, ML_DH)
    h, C_new, n_new, m_new = mlstm_chunked(mh(q), mh(k) * ML_DH ** -0.5, mh(v), log_i, log_f,
                                          st['ml_C'][j], st['ml_n'][j], st['ml_m'][j])
    h = head_norm(h, W['ml_gn_g'][j], W['ml_gn_b'][j], GN_EPS).astype(x.dtype).reshape(B, T, D_HALF)
    yc = (h + W['ml_skip'][j] * xc) * jax.nn.silu(z)
    pr_prev = jnp.concatenate([st['rw_shift'][j][:, None].astype(pr.dtype), pr[:, :-1]], axis=1)
    shift_new = pr[:, -1]
    pm = pr + (pr_prev - pr) * W['rw_mu'][j]
    o = 3 * D_HALF
    r, kr, vr = pm[..., :D_HALF], pm[..., D_HALF:2 * D_HALF], pm[..., 2 * D_HALF:o]
    wl = pm[..., o:o + RW_DECAY_LORA]
    al = pm[..., o + RW_DECAY_LORA:o + RW_DECAY_LORA + RW_A_LORA]
    gl = pm[..., o + RW_DECAY_LORA + RW_A_LORA:]
    w_log = -jax.nn.softplus(-(W['rw_w0'][j] + jnp.tanh(wl) @ W['rw_w2'][j])) - 0.5
    a = jax.nn.sigmoid(W['rw_a0'][j] + al @ W['rw_a2'][j])
    g = jax.nn.sigmoid(gl) @ W['rw_g2'][j]
    rh = lambda t: t.astype(f32).reshape(B, T, RW_HEADS, RW_DH)
    rr, kh, vh, ah = rh(r), rh(kr), rh(vr), rh(a)
    kk = kh * W['rw_kk'][j].reshape(RW_HEADS, RW_DH)
    kk = kk / jnp.maximum(jnp.sqrt(jnp.sum(jnp.square(kk), -1, keepdims=True)), 1e-12)
    kh = kh * (1.0 + (ah - 1.0) * W['rw_ka'][j].reshape(RW_HEADS, RW_DH))
    yd, S_new = rwkv7_recurrence(rr, rh(w_log), kh, vh, -kk, kk * ah, st['rw_S'][j])
    yd = head_norm(yd, W['rw_gn_g'][j], W['rw_gn_b'][j], RW_GN_EPS)
    yd = yd + jnp.sum(rr * kh * W['rw_rk'][j], -1, keepdims=True) * vh
    yd = (yd.reshape(B, T, D_HALF) * g).astype(x.dtype)
    y = jnp.concatenate([yc, yd], axis=-1) @ W['od_w_out'][j]
    return y, (C_new, n_new, m_new, ml_conv_new, S_new, shift_new)


def squared_relu_mlp(x, w1, w2):
    return jnp.square(jax.nn.relu(x @ w1)) @ w2


def zero_states(batch, dtype):
    z = lambda *s: jnp.zeros(s, dtype)
    return dict(lru_h=z(N_EVEN, batch, D_HALF), lru_conv=z(N_EVEN, batch, CONV_W - 1, D_HALF),
                ret_S=z(N_EVEN, batch, RET_HEADS, RET_DH, RET_DH),
                ml_C=z(N_ODD, batch, ML_HEADS, ML_DH, ML_DH), ml_n=z(N_ODD, batch, ML_HEADS, ML_DH),
                ml_m=z(N_ODD, batch, ML_HEADS), ml_conv=z(N_ODD, batch, CONV_W - 1, D_HALF),
                rw_S=z(N_ODD, batch, RW_HEADS, RW_DH, RW_DH), rw_shift=z(N_ODD, batch, RW_SHIFT_COLS))


def trunk(x, st, pos, W, log_gamma):
    new = {name: [] for name in st}
    for l in range(DEPTH):
        j = l // 2
        if l % 2 == 0:
            y, (h, cb, S) = even_mixer(x, pos, j, st, W, log_gamma)
            new['lru_h'].append(h)
            new['lru_conv'].append(cb)
            new['ret_S'].append(S)
        else:
            y, (C, n, m, cb, S, sh) = odd_mixer(x, j, st, W)
            new['ml_C'].append(C)
            new['ml_n'].append(n)
            new['ml_m'].append(m)
            new['ml_conv'].append(cb)
            new['rw_S'].append(S)
            new['rw_shift'].append(sh)
        x = layer_norm(ALPHA * x + y, W['ln1_g'][l], W['ln1_b'][l])
        x = layer_norm(ALPHA * x + squared_relu_mlp(x, W['mlp_w1'][l], W['mlp_w2'][l]), W['ln2_g'][l], W['ln2_b'][l])
    return x, {name: jnp.stack(v).astype(st[name].dtype) for name, v in new.items()}


def setup_inputs(seed: int = 0) -> dict:
    key = jax.random.key(seed)
    ks = iter(jax.random.split(key, 64))
    f32 = jnp.float32

    def nrm(shape, scale):
        return jax.random.normal(next(ks), shape, f32) * scale

    def unif(shape, lo, hi):
        return jax.random.uniform(next(ks), shape, f32, lo, hi)

    out = {}
    out['x_prompt'] = nrm((BATCH, SEQ, D_MODEL), 1.0)
    out['x_sample'] = nrm((DEC_BATCH, DEC_SEQ, D_MODEL), 1.0)
    out['state_lru_h'] = nrm((N_EVEN, DEC_BATCH, D_HALF), 0.5)
    out['state_lru_conv'] = nrm((N_EVEN, DEC_BATCH, CONV_W - 1, D_HALF), 1.0)
    out['state_ret'] = nrm((N_EVEN, DEC_BATCH, RET_HEADS, RET_DH, RET_DH), 0.5)
    out['state_mlstm_C'] = nrm((N_ODD, DEC_BATCH, ML_HEADS, ML_DH, ML_DH), 0.3)
    out['state_mlstm_n'] = nrm((N_ODD, DEC_BATCH, ML_HEADS, ML_DH), 0.3)
    out['state_mlstm_m'] = unif((N_ODD, DEC_BATCH, ML_HEADS), 0.0, 3.0)
    out['state_mlstm_conv'] = nrm((N_ODD, DEC_BATCH, CONV_W - 1, D_HALF), 1.0)
    out['state_rwkv_S'] = nrm((N_ODD, DEC_BATCH, RW_HEADS, RW_DH, RW_DH), 0.3)
    out['state_rwkv_shift'] = nrm((N_ODD, DEC_BATCH, RW_SHIFT_COLS), 1.0)
    out['ln1_g'] = 1.0 + nrm((DEPTH, D_MODEL), 0.02)
    out['ln1_b'] = nrm((DEPTH, D_MODEL), 0.02)
    out['ln2_g'] = 1.0 + nrm((DEPTH, D_MODEL), 0.02)
    out['ln2_b'] = nrm((DEPTH, D_MODEL), 0.02)
    out['mlp_w1'] = nrm((DEPTH, D_MODEL, D_FF), D_MODEL ** -0.5)
    out['mlp_w2'] = nrm((DEPTH, D_FF, D_MODEL), BETA * D_FF ** -0.5)
    out['ev_w_in'] = nrm((N_EVEN, D_MODEL, EVEN_IN), D_MODEL ** -0.5)
    out['ev_w_out'] = nrm((N_EVEN, D_MIX, D_MODEL), BETA * D_MIX ** -0.5)
    out['lru_conv_w'] = nrm((N_EVEN, CONV_W, D_HALF), CONV_W ** -0.5)
    out['lru_conv_b'] = nrm((N_EVEN, D_HALF), 0.02)
    out['lru_wa'] = nrm((N_EVEN, LRU_BLOCKS, LRU_BLOCK, LRU_BLOCK), LRU_BLOCK ** -0.5)
    out['lru_ba'] = nrm((N_EVEN, D_HALF), 0.02)
    out['lru_wx'] = nrm((N_EVEN, LRU_BLOCKS, LRU_BLOCK, LRU_BLOCK), LRU_BLOCK ** -0.5)
    out['lru_bx'] = nrm((N_EVEN, D_HALF), 0.02)
    a_lru = unif((N_EVEN, D_HALF), 0.9, 0.999) ** (1.0 / LRU_C)
    out['lru_lambda'] = jnp.log(a_lru) - jnp.log1p(-a_lru)
    out['ret_gn_g'] = 1.0 + nrm((N_EVEN, RET_HEADS, RET_DH), 0.02)
    out['ret_gn_b'] = nrm((N_EVEN, RET_HEADS, RET_DH), 0.02)
    out['od_w_in'] = nrm((N_ODD, D_MODEL, ODD_IN), D_MODEL ** -0.5)
    out['od_w_out'] = nrm((N_ODD, D_MIX, D_MODEL), BETA * D_MIX ** -0.5)
    out['ml_conv_w'] = nrm((N_ODD, CONV_W, D_HALF), CONV_W ** -0.5)
    out['ml_conv_b'] = nrm((N_ODD, D_HALF), 0.02)
    out['ml_wq'] = nrm((N_ODD, ML_NBLK, ML_QKV_BLOCK, ML_QKV_BLOCK), ML_QKV_BLOCK ** -0.5)
    out['ml_wk'] = nrm((N_ODD, ML_NBLK, ML_QKV_BLOCK, ML_QKV_BLOCK), ML_QKV_BLOCK ** -0.5)
    out['ml_wv'] = nrm((N_ODD, ML_NBLK, ML_QKV_BLOCK, ML_QKV_BLOCK), ML_QKV_BLOCK ** -0.5)
    out['ml_w_gate'] = nrm((N_ODD, 3 * D_HALF, 2 * ML_HEADS), 0.5 * (3 * D_HALF) ** -0.5)
    f_bias = jnp.broadcast_to(jnp.linspace(3.0, 6.0, ML_HEADS, dtype=f32), (N_ODD, ML_HEADS))
    out['ml_b_gate'] = jnp.concatenate([nrm((N_ODD, ML_HEADS), 0.1), f_bias + nrm((N_ODD, ML_HEADS), 0.1)], axis=-1)
    out['ml_gn_g'] = 1.0 + nrm((N_ODD, ML_HEADS, ML_DH), 0.02)
    out['ml_gn_b'] = nrm((N_ODD, ML_HEADS, ML_DH), 0.02)
    out['ml_skip'] = 1.0 + nrm((N_ODD, D_HALF), 0.02)
    out['rw_mu'] = unif((N_ODD, RW_SHIFT_COLS), 0.0, 1.0)
    out['rw_w0'] = jnp.broadcast_to(jnp.linspace(-6.5, -1.5, D_HALF, dtype=f32), (N_ODD, D_HALF)) + nrm((N_ODD, D_HALF), 0.1)
    out['rw_w2'] = nrm((N_ODD, RW_DECAY_LORA, D_HALF), 0.1)
    out['rw_a0'] = nrm((N_ODD, D_HALF), 0.1)
    out['rw_a2'] = nrm((N_ODD, RW_A_LORA, D_HALF), 0.1)
    out['rw_g2'] = nrm((N_ODD, RW_GATE_LORA, D_HALF), RW_GATE_LORA ** -0.5)
    out['rw_kk'] = 0.85 + nrm((N_ODD, D_HALF), 0.02)
    out['rw_ka'] = 1.0 + nrm((N_ODD, D_HALF), 0.02)
    out['rw_rk'] = nrm((N_ODD, RW_HEADS, RW_DH), 0.1)
    out['rw_gn_g'] = 1.0 + nrm((N_ODD, RW_HEADS, RW_DH), 0.02)
    out['rw_gn_b'] = nrm((N_ODD, RW_HEADS, RW_DH), 0.02)
    return out


def reference(x_prompt, x_sample, state_lru_h, state_lru_conv, state_ret, state_mlstm_C, state_mlstm_n,
              state_mlstm_m, state_mlstm_conv, state_rwkv_S, state_rwkv_shift,
              ln1_g, ln1_b, ln2_g, ln2_b, mlp_w1, mlp_w2, ev_w_in, ev_w_out,
              lru_conv_w, lru_conv_b, lru_wa, lru_ba, lru_wx, lru_bx, lru_lambda, ret_gn_g, ret_gn_b,
              od_w_in, od_w_out, ml_conv_w, ml_conv_b, ml_wq, ml_wk, ml_wv, ml_w_gate, ml_b_gate,
              ml_gn_g, ml_gn_b, ml_skip, rw_mu, rw_w0, rw_w2, rw_a0, rw_a2, rw_g2, rw_kk, rw_ka, rw_rk,
              rw_gn_g, rw_gn_b):
    W = dict(ln1_g=ln1_g, ln1_b=ln1_b, ln2_g=ln2_g, ln2_b=ln2_b, mlp_w1=mlp_w1, mlp_w2=mlp_w2,
             ev_w_in=ev_w_in, ev_w_out=ev_w_out, lru_conv_w=lru_conv_w, lru_conv_b=lru_conv_b,
             lru_wa=lru_wa, lru_ba=lru_ba, lru_wx=lru_wx, lru_bx=lru_bx, lru_lambda=lru_lambda,
             ret_gn_g=ret_gn_g, ret_gn_b=ret_gn_b, od_w_in=od_w_in, od_w_out=od_w_out,
             ml_conv_w=ml_conv_w, ml_conv_b=ml_conv_b, ml_wq=ml_wq, ml_wk=ml_wk, ml_wv=ml_wv,
             ml_w_gate=ml_w_gate, ml_b_gate=ml_b_gate, ml_gn_g=ml_gn_g, ml_gn_b=ml_gn_b, ml_skip=ml_skip,
             rw_mu=rw_mu, rw_w0=rw_w0, rw_w2=rw_w2, rw_a0=rw_a0, rw_a2=rw_a2, rw_g2=rw_g2,
             rw_kk=rw_kk, rw_ka=rw_ka, rw_rk=rw_rk, rw_gn_g=rw_gn_g, rw_gn_b=rw_gn_b)
    log_gamma = jnp.log1p(-jnp.exp2(-5.0 - jnp.arange(RET_HEADS, dtype=jnp.float32)))
    st_prompt = zero_states(x_prompt.shape[0], x_prompt.dtype)
    st_sample = dict(lru_h=state_lru_h, lru_conv=state_lru_conv, ret_S=state_ret,
                     ml_C=state_mlstm_C, ml_n=state_mlstm_n, ml_m=state_mlstm_m, ml_conv=state_mlstm_conv,
                     rw_S=state_rwkv_S, rw_shift=state_rwkv_shift)
    pos_prompt = jnp.arange(x_prompt.shape[1], dtype=jnp.int32)
    pos_sample = PAST_LEN + jnp.arange(x_sample.shape[1], dtype=jnp.int32)
    y_prompt, sp = trunk(x_prompt, st_prompt, pos_prompt, W, log_gamma)
    y_sample, ss = trunk(x_sample, st_sample, pos_sample, W, log_gamma)
    return (y_prompt, y_sample,
            sp['lru_h'], sp['lru_conv'], sp['ret_S'], sp['ml_C'], sp['ml_n'], sp['ml_m'], sp['ml_conv'], sp['rw_S'], sp['rw_shift'],
            ss['lru_h'], ss['lru_conv'], ss['ret_S'], ss['ml_C'], ss['ml_n'], ss['ml_m'], ss['ml_conv'], ss['rw_S'], ss['rw_shift'])
```

```python
import functools

import jax
import jax.numpy as jnp
from jax import lax
from jax.experimental import pallas as pl
from jax.experimental.pallas import tpu as pltpu

F32 = jnp.float32
BF16 = jnp.bfloat16

D_MODEL = 1024
DEPTH = 4
PAST_LEN = 16384
D_HALF = D_MODEL // 2
CONV_W = 4
LRU_BLOCKS = 8
LRU_BLOCK = D_HALF // LRU_BLOCKS
LRU_C = 8.0
RET_HEADS = 4
RET_DH = D_HALF // RET_HEADS
CHUNK = 128
ROPE_BASE = 10000.0
ML_HEADS = 4
ML_DH = D_HALF // ML_HEADS
ML_QKV_BLOCK = 4
ML_NBLK = D_HALF // ML_QKV_BLOCK
RW_HEADS = 8
RW_DH = D_HALF // RW_HEADS
RW_DECAY_LORA = 32
RW_A_LORA = 32
RW_GATE_LORA = 96
RW_LORA = RW_DECAY_LORA + RW_A_LORA + RW_GATE_LORA
RW_SHIFT_COLS = 3 * D_HALF + RW_LORA
D_FF = 4 * D_MODEL
ALPHA = (2.0 * DEPTH) ** 0.25
EVEN_IN = 6 * D_HALF
ODD_IN = 2 * D_HALF + RW_SHIFT_COLS
LN_EPS = 1e-5
GN_EPS = 1e-5
RW_GN_EPS = 64e-5

LANES = 128
SUBLANES = 8
RW_LORA_PAD = 2 * LANES
RW_COLS_PAD = 3 * D_HALF + RW_LORA_PAD
ODD_IN_PAD = 2 * D_HALF + RW_COLS_PAD
RW_BB = 8
VMEM_LIMIT = 56 * 1024 * 1024


def _params(sem):
    return pltpu.CompilerParams(dimension_semantics=sem, vmem_limit_bytes=VMEM_LIMIT)


def _dot(a, b):
    return jnp.dot(a.astype(BF16), b.astype(BF16), preferred_element_type=F32)


def _dot_nt(a, b):
    return lax.dot_general(a.astype(BF16), b.astype(BF16), (((1,), (1,)), ((), ())),
                           preferred_element_type=F32)


def _dot_tn(a, b):
    return lax.dot_general(a.astype(BF16), b.astype(BF16), (((0,), (0,)), ((), ())),
                           preferred_element_type=F32)


def _split3(a):
    hi = a.astype(BF16)
    r1 = a - hi.astype(F32)
    mid = r1.astype(BF16)
    lo = (r1 - mid.astype(F32)).astype(BF16)
    return hi, mid, lo


def _xdot(a, b01):
    hi, mid, lo = _split3(a)
    f = lambda t: jnp.dot(t, b01, preferred_element_type=F32)
    return f(hi) + f(mid) + f(lo)


def _xdot_l(b01, a):
    hi, mid, lo = _split3(a)
    f = lambda t: jnp.dot(b01, t, preferred_element_type=F32)
    return f(hi) + f(mid) + f(lo)


def _sigmoid(x):
    return 1.0 / (1.0 + jnp.exp(-x))


def _silu(x):
    return x * _sigmoid(x)


def _softplus(x):
    return jnp.maximum(x, 0.0) + jnp.log1p(jnp.exp(-jnp.abs(x)))


def _gelu_tanh(x):
    return 0.5 * x * (1.0 + jnp.tanh(0.7978845608028654 * (x + 0.044715 * (x * x * x))))


def _layer_norm(x, g, b, eps):
    mu = jnp.mean(x, -1, keepdims=True)
    xc = x - mu
    var = jnp.mean(xc * xc, -1, keepdims=True)
    return xc * lax.rsqrt(var + eps) * g + b


def _group_ones(n, group):
    r = lax.broadcasted_iota(jnp.int32, (n, n), 0) // group
    c = lax.broadcasted_iota(jnp.int32, (n, n), 1) // group
    return jnp.where(r == c, 1.0, 0.0).astype(BF16)


def _group_sum(x, ones_bd):
    parts = [_xdot(x[:, s * LANES:(s + 1) * LANES], ones_bd) for s in range(x.shape[1] // LANES)]
    return jnp.concatenate(parts, axis=-1)


def _rotate(x, cosf, sinf):
    return x * cosf + pltpu.roll(x, RET_DH // 2, 1) * sinf


def _proj_kernel(x_ref, w_ref, *o_refs, splits):
    xb = x_ref[...].astype(BF16)
    off = 0
    for o_ref, n in zip(o_refs, splits):
        o_ref[...] = jnp.dot(xb, w_ref[:, off:off + n], preferred_element_type=F32)
        off += n


def _proj(x2d, w_all, j, tm, splits):
    M = x2d.shape[0]
    N = w_all.shape[2]
    return pl.pallas_call(
        functools.partial(_proj_kernel, splits=splits),
        grid=(M // tm,),
        in_specs=[pl.BlockSpec((tm, D_MODEL), lambda i: (i, 0)),
                  pl.BlockSpec((None, D_MODEL, N), lambda i: (j, 0, 0))],
        out_specs=[pl.BlockSpec((tm, n), lambda i: (i, 0)) for n in splits],
        out_shape=[jax.ShapeDtypeStruct((M, n), F32) for n in splits],
        compiler_params=_params(("parallel",)),
        name="proj_in",
    )(x2d, w_all)


FF_CHUNK = 1024


def _post_kernel(ya_ref, yb_ref, x_ref, wo_ref, g1_ref, b1_ref, w1_ref, w2_ref, g2_ref, b2_ref, o_ref):
    y = (jnp.dot(ya_ref[...].astype(BF16), wo_ref[0:D_HALF, :], preferred_element_type=F32)
         + jnp.dot(yb_ref[...].astype(BF16), wo_ref[D_HALF:D_MODEL, :], preferred_element_type=F32))
    x1 = _layer_norm(ALPHA * x_ref[...] + y, g1_ref[...], b1_ref[...], LN_EPS)
    x1b = x1.astype(BF16)
    acc = jnp.zeros(x1.shape, F32)
    for c in range(D_FF // FF_CHUNK):
        h = jnp.dot(x1b, w1_ref[:, c * FF_CHUNK:(c + 1) * FF_CHUNK], preferred_element_type=F32)
        h = jnp.square(jnp.maximum(h, 0.0))
        acc = acc + jnp.dot(h.astype(BF16), w2_ref[c * FF_CHUNK:(c + 1) * FF_CHUNK, :],
                            preferred_element_type=F32)
    o_ref[...] = _layer_norm(ALPHA * x1 + acc, g2_ref[...], b2_ref[...], LN_EPS)


def _post(ya, yb, x2d, wout_all, j, l, ln1_g, ln1_b, w1_all, w2_all, ln2_g, ln2_b, tm):
    M = x2d.shape[0]
    row = lambda i: (i, 0)
    vec = pl.BlockSpec((None, 1, D_MODEL), lambda i: (l, 0, 0))
    r3 = lambda a: a.reshape(DEPTH, 1, D_MODEL)
    return pl.pallas_call(
        _post_kernel,
        grid=(M // tm,),
        in_specs=[pl.BlockSpec((tm, D_HALF), row), pl.BlockSpec((tm, D_HALF), row),
                  pl.BlockSpec((tm, D_MODEL), row),
                  pl.BlockSpec((None, D_MODEL, D_MODEL), lambda i: (j, 0, 0)),
                  vec, vec,
                  pl.BlockSpec((None, D_MODEL, D_FF), lambda i: (l, 0, 0)),
                  pl.BlockSpec((None, D_FF, D_MODEL), lambda i: (l, 0, 0)),
                  vec, vec],
        out_specs=pl.BlockSpec((tm, D_MODEL), row),
        out_shape=jax.ShapeDtypeStruct((M, D_MODEL), F32),
        compiler_params=_params(("parallel",)),
        name="post_mlp",
    )(ya, yb, x2d, wout_all, r3(ln1_g), r3(ln1_b), w1_all, w2_all, r3(ln2_g), r3(ln2_b))


def _conv_prefill(x, xbuf, cw_ref, cb_ref, L):
    xbuf[SUBLANES:SUBLANES + L, :] = x
    y = cb_ref[...]
    for i in range(CONV_W):
        y = y + cw_ref[i:i + 1, :] * xbuf[SUBLANES - (CONV_W - 1) + i:SUBLANES - (CONV_W - 1) + i + L, :]
    tail = xbuf[L + SUBLANES - (CONV_W - 1):L + SUBLANES, :]
    xbuf[SUBLANES - (CONV_W - 1):SUBLANES, :] = tail
    return y, tail


def _lru_gates(xc, wa_ref, ba_ref, wx_ref, bx_ref, lam_ref):
    xcb = xc.astype(BF16)
    nslab = D_HALF // LANES
    ra = jnp.concatenate([jnp.dot(xcb[:, s * LANES:(s + 1) * LANES], wa_ref[s], preferred_element_type=F32)
                          for s in range(nslab)], axis=-1)
    rx = jnp.concatenate([jnp.dot(xcb[:, s * LANES:(s + 1) * LANES], wx_ref[s], preferred_element_type=F32)
                          for s in range(nslab)], axis=-1)
    r = _sigmoid(ra + ba_ref[...])
    i = _sigmoid(rx + bx_ref[...])
    log_a = -LRU_C * r * _softplus(-lam_ref[...])
    a = jnp.exp(log_a)
    t = jnp.tanh(log_a)
    u = jnp.sqrt(-2.0 * t / (1.0 - t)) * (i * xc)
    return a, u


def _head_norm(o, g, b, eps):
    mu = jnp.mean(o, -1, keepdims=True)
    oc = o - mu
    var = jnp.mean(oc * oc, -1, keepdims=True)
    return oc * lax.rsqrt(var + eps) * g + b


def _even_prefill_kernel(xa_ref, ga_ref, q_ref, k_ref, v_ref, gb_ref,
                         cw_ref, cb_ref, wa_ref, ba_ref, wx_ref, bx_ref, lam_ref,
                         cos_ref, sin_ref, dmask_ref, qd_ref, kd_ref, cd_ref, gng_ref, gnb_ref,
                         h0_ref, conv0_ref, S0_ref,
                         ya_ref, yb_ref, h_ref, conv_ref, S_ref, xbuf, *, L):
    c = pl.program_id(1)

    @pl.when(c == 0)
    def _():
        h_ref[...] = h0_ref[...]
        S_ref[...] = S0_ref[...]
        xbuf[SUBLANES - (CONV_W - 1):SUBLANES, :] = conv0_ref[...]

    xc, tail = _conv_prefill(xa_ref[...], xbuf, cw_ref, cb_ref, L)
    conv_ref[...] = tail
    a, u = _lru_gates(xc, wa_ref, ba_ref, wx_ref, bx_ref, lam_ref)
    row = lax.broadcasted_iota(jnp.int32, (L, D_HALF), 0)
    s = 1
    while s < L:
        keep = row >= s
        a_sh = jnp.where(keep, pltpu.roll(a, s, 0), 1.0)
        u_sh = jnp.where(keep, pltpu.roll(u, s, 0), 0.0)
        u = a * u_sh + u
        a = a * a_sh
        s *= 2
    h = a * h_ref[...] + u
    h_ref[...] = h[L - 1:L, :]
    ya_ref[...] = _gelu_tanh(ga_ref[...]) * h

    cosf = cos_ref[...]
    sinf = sin_ref[...]
    for hh in range(RET_HEADS):
        sl = slice(hh * RET_DH, (hh + 1) * RET_DH)
        qh = _rotate(q_ref[:, sl], cosf, sinf)
        kh = _rotate(k_ref[:, sl], cosf, sinf) * (RET_DH ** -0.5)
        vh = v_ref[:, sl]
        S = S_ref[hh]
        scores = _dot_nt(qh, kh) * dmask_ref[hh]
        o = _dot(scores, vh) + _dot(qh * qd_ref[:, sl], S)
        S_ref[hh] = S * cd_ref[hh] + _dot_tn(kh * kd_ref[:, sl], vh)
        on = _head_norm(o, gng_ref[:, sl], gnb_ref[:, sl], GN_EPS)
        yb_ref[:, sl] = _silu(gb_ref[:, sl]) * on


def _ret_tables(L):
    log_gamma = jnp.log1p(-jnp.exp2(-5.0 - jnp.arange(RET_HEADS, dtype=F32)))
    idx = jnp.arange(L, dtype=F32)
    diff = idx[:, None] - idx[None, :]
    dmask = jnp.where(diff >= 0, jnp.exp(log_gamma[:, None, None] * jnp.maximum(diff, 0.0)), 0.0)
    qd = jnp.exp(log_gamma[:, None] * (idx + 1.0))
    kd = jnp.exp(log_gamma[:, None] * (L - 1.0 - idx))
    cd = jnp.exp(log_gamma * L)
    qd_full = jnp.repeat(qd.T, RET_DH, axis=1)
    kd_full = jnp.repeat(kd.T, RET_DH, axis=1)
    cd_full = jnp.broadcast_to(cd[:, None, None], (RET_HEADS, 1, RET_DH))
    return dmask, qd_full, kd_full, cd_full


def _rope_tables(pos):
    half = RET_DH // 2
    inv = ROPE_BASE ** (-jnp.arange(half, dtype=F32) / half)
    ang = pos.astype(F32)[:, None] * inv[None, :]
    cos, sin = jnp.cos(ang), jnp.sin(ang)
    return jnp.concatenate([cos, cos], -1), jnp.concatenate([-sin, sin], -1)


def _lru_dense(w):
    pairs = LANES // LRU_BLOCK
    w4 = w.reshape(LRU_BLOCKS // pairs, pairs, LRU_BLOCK, LRU_BLOCK)
    eye = jnp.eye(pairs, dtype=w.dtype)
    d = w4[:, :, :, None, :] * eye[None, :, None, :, None]
    return d.reshape(LRU_BLOCKS // pairs, LANES, LANES).astype(BF16)


def _full(shape):
    n = len(shape)
    return pl.BlockSpec(shape, lambda *_: (0,) * n)


def _even_prefill(p3, pos, st, Wl):
    B, T, _ = p3.shape
    L = CHUNK
    nc = T // L
    dmask, qd, kd, cd = _ret_tables(L)
    cosf, sinf = _rope_tables(pos)
    col = lambda i: pl.BlockSpec((None, L, D_HALF), lambda b, c: (b, c, i))
    perb = lambda *s: pl.BlockSpec((None,) + s, lambda b, c: (b,) + (0,) * len(s))
    row2 = lambda a: a.reshape(1, D_HALF)
    ins = [p3] * 6 + [Wl['lru_conv_w'], row2(Wl['lru_conv_b']), Wl['lru_wa_d'], row2(Wl['lru_ba']),
                      Wl['lru_wx_d'], row2(Wl['lru_bx']), row2(Wl['lru_lambda']),
                      cosf, sinf, dmask, qd, kd, cd, row2(Wl['ret_gn_g']), row2(Wl['ret_gn_b']),
                      st['lru_h'].reshape(B, 1, D_HALF), st['lru_conv'], st['ret_S']]
    in_specs = [col(i) for i in range(6)] + [
        _full((CONV_W, D_HALF)), _full((1, D_HALF)), _full((4, LANES, LANES)), _full((1, D_HALF)),
        _full((4, LANES, LANES)), _full((1, D_HALF)), _full((1, D_HALF)),
        pl.BlockSpec((L, RET_DH), lambda b, c: (c, 0)), pl.BlockSpec((L, RET_DH), lambda b, c: (c, 0)),
        _full((RET_HEADS, L, L)), _full((L, D_HALF)), _full((L, D_HALF)), _full((RET_HEADS, 1, RET_DH)),
        _full((1, D_HALF)), _full((1, D_HALF)),
        perb(1, D_HALF), perb(CONV_W - 1, D_HALF), perb(RET_HEADS, RET_DH, RET_DH)]
    seq = pl.BlockSpec((None, L, D_HALF), lambda b, c: (b, c, 0))
    ya, yb, h, conv, S = pl.pallas_call(
        functools.partial(_even_prefill_kernel, L=L),
        grid=(B, nc),
        in_specs=in_specs,
        out_specs=[seq, seq, perb(1, D_HALF), perb(CONV_W - 1, D_HALF), perb(RET_HEADS, RET_DH, RET_DH)],
        out_shape=[jax.ShapeDtypeStruct((B, T, D_HALF), F32), jax.ShapeDtypeStruct((B, T, D_HALF), F32),
                   jax.ShapeDtypeStruct((B, 1, D_HALF), F32),
                   jax.ShapeDtypeStruct((B, CONV_W - 1, D_HALF), F32),
                   jax.ShapeDtypeStruct((B, RET_HEADS, RET_DH, RET_DH), F32)],
        scratch_shapes=[pltpu.VMEM((L + SUBLANES, D_HALF), F32)],
        compiler_params=_params(("parallel", "arbitrary")),
        name="even_prefill",
    )(*ins)
    return ya, yb, (h.reshape(B, D_HALF), conv, S)


DEC_BB = 8


def _even_decode_kernel(p_ref, cw_ref, cb_ref, wa_ref, ba_ref, wx_ref, bx_ref, lam_ref,
                        cos_ref, sin_ref, dm_ref, qd_ref, kd_ref, cd_ref, gng_ref, gnb_ref,
                        h0_ref, conv0_ref, S0_ref,
                        ya_ref, yb_ref, h_ref, conv_ref, S_ref):
    col = lambda i: p_ref[:, i * D_HALF:(i + 1) * D_HALF]
    xa = col(0)
    xc = cb_ref[...] + cw_ref[CONV_W - 1:CONV_W, :] * xa
    for i in range(CONV_W - 1):
        xc = xc + cw_ref[i:i + 1, :] * conv0_ref[i]
    for i in range(CONV_W - 2):
        conv_ref[i] = conv0_ref[i + 1]
    conv_ref[CONV_W - 2] = xa
    a, u = _lru_gates(xc, wa_ref, ba_ref, wx_ref, bx_ref, lam_ref)
    h = a * h0_ref[...] + u
    h_ref[...] = h
    ya_ref[...] = _gelu_tanh(col(1)) * h

    cosf = cos_ref[...]
    sinf = sin_ref[...]
    row8 = lax.broadcasted_iota(jnp.int32, (SUBLANES, RET_DH), 0)
    q, k, v, gb = col(2), col(3), col(4), col(5)
    for hh in range(RET_HEADS):
        sl = slice(hh * RET_DH, (hh + 1) * RET_DH)
        qh = _rotate(q[:, sl], cosf, sinf)
        kh = _rotate(k[:, sl], cosf, sinf) * (RET_DH ** -0.5)
        vh = v[:, sl]
        qk = jnp.sum(qh * kh, -1, keepdims=True) * dm_ref[:, sl]
        qq = qh * qd_ref[:, sl]
        kk = kh * kd_ref[:, sl]
        rows = []
        for bi in range(DEC_BB):
            S = S0_ref[bi, hh]
            q8 = jnp.broadcast_to(qq[bi:bi + 1, :], (SUBLANES, RET_DH))
            rows.append(_dot_f32(q8, S)[0:1, :])
            k8 = jnp.where(row8 == 0, jnp.broadcast_to(kk[bi:bi + 1, :], (SUBLANES, RET_DH)), 0.0)
            v8 = jnp.broadcast_to(vh[bi:bi + 1, :], (SUBLANES, RET_DH))
            S_ref[bi, hh] = S * cd_ref[hh] + _dot_tn_f32(k8, v8)
        o = qk * vh + jnp.concatenate(rows, axis=0)
        on = _head_norm(o, gng_ref[:, sl], gnb_ref[:, sl], GN_EPS)
        yb_ref[:, sl] = _silu(gb[:, sl]) * on


def _dot_f32(a, b):
    return jnp.dot(a, b, preferred_element_type=F32)


def _dot_tn_f32(a, b):
    return lax.dot_general(a, b, (((0,), (0,)), ((), ())), preferred_element_type=F32)


def _even_decode(p2, pos, st, Wl):
    B = p2.shape[0]
    bb = DEC_BB
    dmask, qd, kd, cd = _ret_tables(1)
    dm = jnp.repeat(dmask[:, 0, :].T, RET_DH, axis=1)
    cosf, sinf = _rope_tables(pos)
    row2 = lambda a: a.reshape(1, D_HALF)
    rows = lambda n: pl.BlockSpec((bb, n), lambda i: (i, 0))
    convs = pl.BlockSpec((CONV_W - 1, bb, D_HALF), lambda i: (0, i, 0))
    Ss = pl.BlockSpec((bb, RET_HEADS, RET_DH, RET_DH), lambda i: (i, 0, 0, 0))
    ins = [p2, Wl['lru_conv_w'], row2(Wl['lru_conv_b']), Wl['lru_wa_d'], row2(Wl['lru_ba']),
           Wl['lru_wx_d'], row2(Wl['lru_bx']), row2(Wl['lru_lambda']),
           cosf, sinf, dm, qd, kd, cd, row2(Wl['ret_gn_g']), row2(Wl['ret_gn_b']),
           st['lru_h'], jnp.swapaxes(st['lru_conv'], 0, 1), st['ret_S']]
    in_specs = [rows(EVEN_IN), _full((CONV_W, D_HALF)), _full((1, D_HALF)), _full((4, LANES, LANES)),
                _full((1, D_HALF)), _full((4, LANES, LANES)), _full((1, D_HALF)), _full((1, D_HALF)),
                _full((1, RET_DH)), _full((1, RET_DH)), _full((1, D_HALF)), _full((1, D_HALF)),
                _full((1, D_HALF)), _full((RET_HEADS, 1, RET_DH)), _full((1, D_HALF)), _full((1, D_HALF)),
                rows(D_HALF), convs, Ss]
    ya, yb, h, conv, S = pl.pallas_call(
        _even_decode_kernel,
        grid=(B // bb,),
        in_specs=in_specs,
        out_specs=[rows(D_HALF), rows(D_HALF), rows(D_HALF), convs, Ss],
        out_shape=[jax.ShapeDtypeStruct((B, D_HALF), F32), jax.ShapeDtypeStruct((B, D_HALF), F32),
                   jax.ShapeDtypeStruct((B, D_HALF), F32),
                   jax.ShapeDtypeStruct((CONV_W - 1, B, D_HALF), F32),
                   jax.ShapeDtypeStruct((B, RET_HEADS, RET_DH, RET_DH), F32)],
        compiler_params=_params(("parallel",)),
        name="even_decode",
    )(*ins)
    return ya, yb, (h, jnp.swapaxes(conv, 0, 1), S)


def _mlstm_qkv_gates(xm, xc, wq_ref, wk_ref, wv_ref, wg_ref, bg_ref):
    q = _dot(xc, wq_ref[...])
    k = _dot(xc, wk_ref[...])
    v = _dot(xm, wv_ref[...])
    g_col = (_dot(q, wg_ref[0:D_HALF, :]) + _dot(k, wg_ref[D_HALF:2 * D_HALF, :])
             + _dot(v, wg_ref[2 * D_HALF:3 * D_HALF, :]) + bg_ref[...])
    return q, k, v, g_col


def _mlstm_prefill_kernel(xm_ref, z_ref, cw_ref, cb_ref, wq_ref, wk_ref, wv_ref, wg_ref, bg_ref,
                          wgt_ref, bgt_ref, gng_ref, gnb_ref, skip_ref,
                          conv0_ref, C0_ref, n0_ref, m0_ref,
                          yc_ref, conv_ref, C_ref, n_ref, m_ref, xbuf, *, L):
    c = pl.program_id(1)

    @pl.when(c == 0)
    def _():
        C_ref[...] = C0_ref[...]
        n_ref[...] = n0_ref[...]
        m_ref[...] = m0_ref[...]
        xbuf[SUBLANES - (CONV_W - 1):SUBLANES, :] = conv0_ref[...]

    xm = xm_ref[...]
    xc, tail = _conv_prefill(xm, xbuf, cw_ref, cb_ref, L)
    conv_ref[...] = tail
    xc = _silu(xc)
    q, k, v, g_col = _mlstm_qkv_gates(xm, xc, wq_ref, wk_ref, wv_ref, wg_ref, bg_ref)
    g_row = (_dot_nt(wgt_ref[:, 0:D_HALF], q) + _dot_nt(wgt_ref[:, D_HALF:2 * D_HALF], k)
             + _dot_nt(wgt_ref[:, 2 * D_HALF:3 * D_HALF], v) + bgt_ref[...])
    ri = lax.broadcasted_iota(jnp.int32, (L, L), 0)
    ci = lax.broadcasted_iota(jnp.int32, (L, L), 1)
    causal = ri >= ci
    tril = jnp.where(causal, 1.0, 0.0).astype(BF16)
    triu = jnp.where(ci >= ri, 1.0, 0.0).astype(BF16)
    b_col = _xdot_l(tril, -_softplus(-g_col))
    b_row = _xdot(-_softplus(-g_row), triu)
    lane = lax.broadcasted_iota(jnp.int32, (1, LANES), 1)
    m_all = m_ref[...]
    for hh in range(ML_HEADS):
        sl = slice(hh * ML_DH, (hh + 1) * ML_DH)
        qh, vh = q[:, sl], v[:, sl]
        kh = k[:, sl] * (ML_DH ** -0.5)
        bc = b_col[:, ML_HEADS + hh:ML_HEADS + hh + 1]
        br = b_row[ML_HEADS + hh:ML_HEADS + hh + 1, :]
        lic = g_col[:, hh:hh + 1]
        lir = g_row[hh:hh + 1, :]
        m_prev = m_all[:, hh:hh + 1]
        C = C_ref[hh]
        n = n_ref[:, sl]
        logD = jnp.where(causal, bc - br + lir, -jnp.inf)
        log_inter = bc + m_prev
        m_t = jnp.maximum(jnp.max(logD, -1, keepdims=True), log_inter)
        s = _dot_nt(qh, kh) * jnp.exp(logD - m_t)
        w_inter = jnp.exp(log_inter - m_t)
        num = _dot(s, vh) + w_inter * _dot(qh, C)
        den = jnp.sum(s, -1, keepdims=True) + w_inter * jnp.sum(qh * n, -1, keepdims=True)
        hcell = num / jnp.maximum(jnp.abs(den), jnp.exp(-m_t))
        m_new = m_t[L - 1:L, :]
        b_last = bc[L - 1:L, :]
        w_k = jnp.exp(b_last - bc + lic - m_new)
        w_C = jnp.exp(b_last + m_prev - m_new)
        kw = kh * w_k
        C_ref[hh] = w_C * C + _dot_tn(kw, vh)
        n_ref[:, sl] = w_C * n + jnp.sum(kw, axis=0, keepdims=True)
        m_all = jnp.where(lane == hh, m_new, m_all)
        hn = _head_norm(hcell, gng_ref[:, sl], gnb_ref[:, sl], GN_EPS)
        yc_ref[:, sl] = (hn + skip_ref[:, sl] * xc[:, sl]) * _silu(z_ref[:, sl])
    m_ref[...] = m_all


def _ml_dense(w):
    eye = jnp.eye(ML_NBLK, dtype=w.dtype)
    d = w[:, :, None, :] * eye[:, None, :, None]
    return d.reshape(D_HALF, D_HALF).astype(BF16)


def _pad_lanes(a, n=LANES):
    return jnp.pad(a, [(0, 0)] * (a.ndim - 1) + [(0, n - a.shape[-1])])


def _mlstm_weights(Wl):
    wg = _pad_lanes(Wl['ml_w_gate']).astype(BF16)
    bg = _pad_lanes(Wl['ml_b_gate'].reshape(1, 2 * ML_HEADS))
    wgt = Wl['ml_w_gate'].T.astype(BF16)
    bgt = jnp.broadcast_to(Wl['ml_b_gate'].reshape(2 * ML_HEADS, 1), (2 * ML_HEADS, LANES))
    return wg, bg, wgt, bgt


def _mlstm_prefill(p3, st, Wl):
    B, T, _ = p3.shape
    L = CHUNK
    nc = T // L
    wg, bg, wgt, bgt = _mlstm_weights(Wl)
    col = lambda i: pl.BlockSpec((None, L, D_HALF), lambda b, c: (b, c, i))
    perb = lambda *s: pl.BlockSpec((None,) + s, lambda b, c: (b,) + (0,) * len(s))
    row2 = lambda a: a.reshape(1, D_HALF)
    ins = [p3, p3, Wl['ml_conv_w'], row2(Wl['ml_conv_b']), Wl['ml_wq_d'], Wl['ml_wk_d'], Wl['ml_wv_d'],
           wg, bg, wgt, bgt, row2(Wl['ml_gn_g']), row2(Wl['ml_gn_b']), row2(Wl['ml_skip']),
           st['ml_conv'], st['ml_C'], st['ml_n'].reshape(B, 1, D_HALF),
           _pad_lanes(st['ml_m']).reshape(B, 1, LANES)]
    in_specs = [col(0), col(1), _full((CONV_W, D_HALF)), _full((1, D_HALF)),
                _full((D_HALF, D_HALF)), _full((D_HALF, D_HALF)), _full((D_HALF, D_HALF)),
                _full((3 * D_HALF, LANES)), _full((1, LANES)), _full((2 * ML_HEADS, 3 * D_HALF)),
                _full((2 * ML_HEADS, LANES)), _full((1, D_HALF)), _full((1, D_HALF)), _full((1, D_HALF)),
                perb(CONV_W - 1, D_HALF), perb(ML_HEADS, ML_DH, ML_DH), perb(1, D_HALF), perb(1, LANES)]
    seq = pl.BlockSpec((None, L, D_HALF), lambda b, c: (b, c, 0))
    yc, conv, C, n, m = pl.pallas_call(
        functools.partial(_mlstm_prefill_kernel, L=L),
        grid=(B, nc),
        in_specs=in_specs,
        out_specs=[seq, perb(CONV_W - 1, D_HALF), perb(ML_HEADS, ML_DH, ML_DH), perb(1, D_HALF), perb(1, LANES)],
        out_shape=[jax.ShapeDtypeStruct((B, T, D_HALF), F32),
                   jax.ShapeDtypeStruct((B, CONV_W - 1, D_HALF), F32),
                   jax.ShapeDtypeStruct((B, ML_HEADS, ML_DH, ML_DH), F32),
                   jax.ShapeDtypeStruct((B, 1, D_HALF), F32),
                   jax.ShapeDtypeStruct((B, 1, LANES), F32)],
        scratch_shapes=[pltpu.VMEM((L + SUBLANES, D_HALF), F32)],
        compiler_params=_params(("parallel", "arbitrary")),
        name="mlstm_prefill",
    )(*ins)
    return yc, (C, n.reshape(B, ML_HEADS, ML_DH), m[:, 0, :ML_HEADS], conv)


def _mlstm_decode_kernel(p_ref, cw_ref, cb_ref, wq_ref, wk_ref, wv_ref, wg_ref, bg_ref,
                         gng_ref, gnb_ref, skip_ref, conv0_ref, C0_ref, n0_ref, m0_ref,
                         yc_ref, conv_ref, C_ref, n_ref, m_ref):
    xm = p_ref[:, 0:D_HALF]
    z = p_ref[:, D_HALF:2 * D_HALF]
    xc = cb_ref[...] + cw_ref[CONV_W - 1:CONV_W, :] * xm
    for i in range(CONV_W - 1):
        xc = xc + cw_ref[i:i + 1, :] * conv0_ref[i]
    for i in range(CONV_W - 2):
        conv_ref[i] = conv0_ref[i + 1]
    conv_ref[CONV_W - 2] = xm
    xc = _silu(xc)
    q, k, v, g = _mlstm_qkv_gates(xm, xc, wq_ref, wk_ref, wv_ref, wg_ref, bg_ref)
    lf_all = -_softplus(-g)
    lane = lax.broadcasted_iota(jnp.int32, (1, LANES), 1)
    row8 = lax.broadcasted_iota(jnp.int32, (SUBLANES, ML_DH), 0)
    m_all = m0_ref[...]
    m_out = m_all
    for hh in range(ML_HEADS):
        sl = slice(hh * ML_DH, (hh + 1) * ML_DH)
        qh, vh = q[:, sl], v[:, sl]
        kh = k[:, sl] * (ML_DH ** -0.5)
        li = g[:, hh:hh + 1]
        lf = lf_all[:, ML_HEADS + hh:ML_HEADS + hh + 1]
        m_prev = m_all[:, hh:hh + 1]
        n = n0_ref[:, sl]
        log_inter = lf + m_prev
        m_t = jnp.maximum(li, log_inter)
        s = jnp.sum(qh * kh, -1, keepdims=True) * jnp.exp(li - m_t)
        w_inter = jnp.exp(log_inter - m_t)
        w_k = jnp.exp(li - m_t)
        w_C = jnp.exp(log_inter - m_t)
        kw = kh * w_k
        rows = []
        for bi in range(DEC_BB):
            C = C0_ref[bi, hh]
            q8 = jnp.broadcast_to(qh[bi:bi + 1, :], (SUBLANES, ML_DH))
            rows.append(_dot_f32(q8, C)[0:1, :])
            k8 = jnp.where(row8 == 0, jnp.broadcast_to(kw[bi:bi + 1, :], (SUBLANES, ML_DH)), 0.0)
            v8 = jnp.broadcast_to(vh[bi:bi + 1, :], (SUBLANES, ML_DH))
            C_ref[bi, hh] = w_C[bi:bi + 1, :] * C + _dot_tn_f32(k8, v8)
        qC = jnp.concatenate(rows, axis=0)
        num = s * vh + w_inter * qC
        den = s + w_inter * jnp.sum(qh * n, -1, keepdims=True)
        hcell = num / jnp.maximum(jnp.abs(den), jnp.exp(-m_t))
        n_ref[:, sl] = w_C * n + kw
        m_out = jnp.where(lane == hh, m_t, m_out)
        hn = _head_norm(hcell, gng_ref[:, sl], gnb_ref[:, sl], GN_EPS)
        yc_ref[:, sl] = (hn + skip_ref[:, sl] * xc[:, sl]) * _silu(z[:, sl])
    m_ref[...] = m_out


def _mlstm_decode(p2, st, Wl):
    B = p2.shape[0]
    bb = DEC_BB
    wg, bg, _, _ = _mlstm_weights(Wl)
    row2 = lambda a: a.reshape(1, D_HALF)
    rows = lambda n: pl.BlockSpec((bb, n), lambda i: (i, 0))
    convs = pl.BlockSpec((CONV_W - 1, bb, D_HALF), lambda i: (0, i, 0))
    Cs = pl.BlockSpec((bb, ML_HEADS, ML_DH, ML_DH), lambda i: (i, 0, 0, 0))
    ins = [p2, Wl['ml_conv_w'], row2(Wl['ml_conv_b']), Wl['ml_wq_d'], Wl['ml_wk_d'], Wl['ml_wv_d'], wg, bg,
           row2(Wl['ml_gn_g']), row2(Wl['ml_gn_b']), row2(Wl['ml_skip']),
           jnp.swapaxes(st['ml_conv'], 0, 1), st['ml_C'], st['ml_n'].reshape(B, D_HALF), _pad_lanes(st['ml_m'])]
    in_specs = [pl.BlockSpec((bb, 2 * D_HALF), lambda i: (i, 0)), _full((CONV_W, D_HALF)), _full((1, D_HALF)),
                _full((D_HALF, D_HALF)), _full((D_HALF, D_HALF)), _full((D_HALF, D_HALF)),
                _full((3 * D_HALF, LANES)), _full((1, LANES)),
                _full((1, D_HALF)), _full((1, D_HALF)), _full((1, D_HALF)),
                convs, Cs, rows(D_HALF), rows(LANES)]
    yc, conv, C, n, m = pl.pallas_call(
        _mlstm_decode_kernel,
        grid=(B // bb,),
        in_specs=in_specs,
        out_specs=[rows(D_HALF), convs, Cs, rows(D_HALF), rows(LANES)],
        out_shape=[jax.ShapeDtypeStruct((B, D_HALF), F32),
                   jax.ShapeDtypeStruct((CONV_W - 1, B, D_HALF), F32),
                   jax.ShapeDtypeStruct((B, ML_HEADS, ML_DH, ML_DH), F32),
                   jax.ShapeDtypeStruct((B, D_HALF), F32),
                   jax.ShapeDtypeStruct((B, LANES), F32)],
        compiler_params=_params(("parallel",)),
        name="mlstm_decode",
    )(*ins)
    return yc, (C, n.reshape(B, ML_HEADS, ML_DH), m[:, :ML_HEADS], jnp.swapaxes(conv, 0, 1))


def _rwkv_pre_body(pr, pr_prev, mu_ref, w0_ref, a0_ref, w2_ref, a2_ref, g2_ref, kkw_ref, kaw_ref, rk_ref,
                   r_ref, d_ref, k_ref, v_ref, a_ref, b_ref, g_ref, bonus_ref):
    pm = pr + (pr_prev - pr) * mu_ref[...]
    r = pm[:, 0:D_HALF]
    kr = pm[:, D_HALF:2 * D_HALF]
    vr = pm[:, 2 * D_HALF:3 * D_HALF]
    lo = pm[:, 3 * D_HALF:RW_COLS_PAD]
    w_log = -_softplus(-(w0_ref[...] + _dot(jnp.tanh(lo), w2_ref[...]))) - 0.5
    a = _sigmoid(a0_ref[...] + _dot(lo, a2_ref[...]))
    g = _dot(_sigmoid(lo), g2_ref[...])
    ones_bd = _group_ones(LANES, RW_DH)
    kk = kr * kkw_ref[...]
    kk = kk / jnp.maximum(jnp.sqrt(_group_sum(kk * kk, ones_bd)), 1e-12)
    kh = kr * (1.0 + (a - 1.0) * kaw_ref[...])
    r_ref[...] = r
    d_ref[...] = jnp.exp(-jnp.exp(w_log))
    k_ref[...] = kh
    v_ref[...] = vr
    a_ref[...] = -kk
    b_ref[...] = kk * a
    g_ref[...] = g
    bonus_ref[...] = _group_sum(r * kh * rk_ref[...], ones_bd) * vr


def _rwkv_pre_prefill_kernel(pr_ref, shift0_ref, *rest, L):
    wrefs, outs, xbuf = rest[:9], rest[9:17], rest[17]
    c = pl.program_id(1)

    @pl.when(c == 0)
    def _():
        xbuf[SUBLANES - 1:SUBLANES, :] = shift0_ref[...]

    pr = pr_ref[...]
    xbuf[SUBLANES:SUBLANES + L, :] = pr
    pr_prev = xbuf[SUBLANES - 1:SUBLANES - 1 + L, :]
    xbuf[SUBLANES - 1:SUBLANES, :] = pr[L - 1:L, :]
    _rwkv_pre_body(pr, pr_prev, *wrefs, *outs)


def _rwkv_pre_decode_kernel(pr_ref, prev_ref, *rest):
    _rwkv_pre_body(pr_ref[...], prev_ref[...], *rest[:9], *rest[9:17])


def _rwkv_pre_weights(Wl):
    row2 = lambda a: a.reshape(1, D_HALF)
    padr = lambda w, o: jnp.pad(w, ((o, RW_LORA_PAD - o - w.shape[0]), (0, 0))).astype(BF16)
    mu = _pad_lanes(Wl['rw_mu'].reshape(1, RW_SHIFT_COLS), RW_COLS_PAD)
    ws = [mu, row2(Wl['rw_w0']), row2(Wl['rw_a0']),
          padr(Wl['rw_w2'], 0), padr(Wl['rw_a2'], RW_DECAY_LORA), padr(Wl['rw_g2'], RW_DECAY_LORA + RW_A_LORA),
          row2(Wl['rw_kk']), row2(Wl['rw_ka']), row2(Wl['rw_rk'])]
    specs = [_full((1, RW_COLS_PAD)), _full((1, D_HALF)), _full((1, D_HALF)),
             _full((RW_LORA_PAD, D_HALF)), _full((RW_LORA_PAD, D_HALF)), _full((RW_LORA_PAD, D_HALF)),
             _full((1, D_HALF)), _full((1, D_HALF)), _full((1, D_HALF))]
    return ws, specs


def _rwkv_pre_prefill(pr3, shift0, Wl):
    B, T, _ = pr3.shape
    L = CHUNK
    ws, wspecs = _rwkv_pre_weights(Wl)
    seq = pl.BlockSpec((None, L, D_HALF), lambda b, c: (b, c, 0))
    outs = pl.pallas_call(
        functools.partial(_rwkv_pre_prefill_kernel, L=L),
        grid=(B, T // L),
        in_specs=[pl.BlockSpec((None, L, RW_COLS_PAD), lambda b, c: (b, c, 0)),
                  pl.BlockSpec((None, 1, RW_COLS_PAD), lambda b, c: (b, 0, 0))] + wspecs,
        out_specs=[seq] * 8,
        out_shape=[jax.ShapeDtypeStruct((B, T, D_HALF), F32)] * 8,
        scratch_shapes=[pltpu.VMEM((L + SUBLANES, RW_COLS_PAD), F32)],
        compiler_params=_params(("parallel", "arbitrary")),
        name="rwkv_pre_prefill",
    )(pr3, _pad_lanes(shift0, RW_COLS_PAD).reshape(B, 1, RW_COLS_PAD), *ws)
    return outs


def _rwkv_pre_decode(pr, shift0, Wl):
    B = pr.shape[0]
    ws, wspecs = _rwkv_pre_weights(Wl)
    full2 = lambda n: pl.BlockSpec((B, n), lambda i: (0, 0))
    outs = pl.pallas_call(
        _rwkv_pre_decode_kernel,
        grid=(1,),
        in_specs=[full2(RW_COLS_PAD), full2(RW_COLS_PAD)] + wspecs,
        out_specs=[full2(D_HALF)] * 8,
        out_shape=[jax.ShapeDtypeStruct((B, D_HALF), F32)] * 8,
        compiler_params=_params(("arbitrary",)),
        name="rwkv_pre_decode",
    )(pr, _pad_lanes(shift0, RW_COLS_PAD), *ws)
    return [o.reshape(B, 1, D_HALF) for o in outs]


RW_IP = RW_DH // 2


def _rwkv_rec_kernel(r_ref, d_ref, k_ref, a_ref, b_ref, v_ref, S0_ref, y_ref, S_ref, *, Tc):
    @pl.when(pl.program_id(1) == 0)
    def _():
        S_ref[...] = S0_ref[...]

    def step(t, carry):
        a, d, b, k, r = a_ref[t], d_ref[t], b_ref[t], k_ref[t], r_ref[t]
        vt = v_ref[t]
        rows = []
        for ip in range(RW_IP):
            S = S_ref[ip]
            sa = jnp.sum(S * a, axis=0, keepdims=True)
            Sn = S * d + sa * b + vt[ip:ip + 1, :] * k
            S_ref[ip] = Sn
            rows.append(jnp.sum(Sn * r, axis=0, keepdims=True))
        y_ref[t] = jnp.concatenate(rows, axis=0)
        return carry

    lax.fori_loop(0, Tc, step, 0)


def _to_rec(x, dup):
    B, T, _ = x.shape
    nbb = B // RW_BB
    y = x.reshape(nbb, RW_BB, T, RW_HEADS, RW_DH).transpose(0, 2, 4, 1, 3).reshape(nbb, T, RW_DH, RW_BB * RW_HEADS)
    return jnp.concatenate([y, y], axis=-1) if dup else y


def _rwkv_recurrence(r, d, k, v, a, b, S0):
    B, T, _ = r.shape
    nbb = B // RW_BB
    Tc = min(T, 32)
    half = RW_BB * RW_HEADS
    vv = _to_rec(v, False).reshape(nbb, T, RW_IP, 2 * half)
    Sr = (S0.reshape(nbb, RW_BB, RW_HEADS, RW_IP, 2, RW_DH).transpose(0, 3, 5, 4, 1, 2)
          .reshape(nbb, RW_IP, RW_DH, 2 * half))
    vec = pl.BlockSpec((None, Tc, RW_DH, 2 * half), lambda i, c: (i, c, 0, 0))
    vsp = pl.BlockSpec((None, Tc, RW_IP, 2 * half), lambda i, c: (i, c, 0, 0))
    ssp = pl.BlockSpec((None, RW_IP, RW_DH, 2 * half), lambda i, c: (i, 0, 0, 0))
    y, S = pl.pallas_call(
        functools.partial(_rwkv_rec_kernel, Tc=Tc),
        grid=(nbb, T // Tc),
        in_specs=[vec] * 5 + [vsp, ssp],
        out_specs=[vsp, ssp],
        out_shape=[jax.ShapeDtypeStruct((nbb, T, RW_IP, 2 * half), F32),
                   jax.ShapeDtypeStruct((nbb, RW_IP, RW_DH, 2 * half), F32)],
        compiler_params=_params(("parallel", "arbitrary")),
        name="rwkv_recurrence",
    )(_to_rec(r, True), _to_rec(d, True), _to_rec(k, True), _to_rec(a, True), _to_rec(b, True), vv, Sr)
    y = (y.reshape(nbb, T, RW_DH, RW_BB, RW_HEADS).transpose(0, 3, 1, 4, 2).reshape(B, T, D_HALF))
    S = (S.reshape(nbb, RW_IP, RW_DH, 2, RW_BB, RW_HEADS).transpose(0, 4, 5, 1, 3, 2)
         .reshape(B, RW_HEADS, RW_DH, RW_DH))
    return y, S


def _rwkv_post_kernel(y_ref, g_ref, bonus_ref, gng_ref, gnb_ref, o_ref):
    ones_bd = _group_ones(LANES, RW_DH)
    y = y_ref[...]
    mu = _group_sum(y, ones_bd) * (1.0 / RW_DH)
    yc = y - mu
    var = _group_sum(yc * yc, ones_bd) * (1.0 / RW_DH)
    hn = yc * lax.rsqrt(var + RW_GN_EPS) * gng_ref[...] + gnb_ref[...]
    o_ref[...] = (hn + bonus_ref[...]) * g_ref[...]


def _rwkv_post(y2, g2, bonus2, Wl, tm):
    M = y2.shape[0]
    row = pl.BlockSpec((tm, D_HALF), lambda i: (i, 0))
    return pl.pallas_call(
        _rwkv_post_kernel,
        grid=(M // tm,),
        in_specs=[row, row, row, _full((1, D_HALF)), _full((1, D_HALF))],
        out_specs=row,
        out_shape=jax.ShapeDtypeStruct((M, D_HALF), F32),
        compiler_params=_params(("parallel",)),
        name="rwkv_post",
    )(y2, g2, bonus2, Wl['rw_gn_g'].reshape(1, D_HALF), Wl['rw_gn_b'].reshape(1, D_HALF))


def _trunk(x, st, pos, W):
    B, T, _ = x.shape
    M = B * T
    decode = T == 1
    tm_proj = min(M, 512)
    tm_post = min(M, 256)
    tm_rw = min(M, 512)
    x2 = x.reshape(M, D_MODEL)
    new = {name: [] for name in st}
    for l in range(DEPTH):
        j = l // 2
        Wl = {name: v[j] for name, v in W['per_pair'][l % 2].items()}
        stl = {name: v[j] for name, v in st.items()}
        if l % 2 == 0:
            p, = _proj(x2, W['ev_w_in'], j, tm_proj, (EVEN_IN,))
            if decode:
                ya, yb, (h, cb, S) = _even_decode(p, pos, stl, Wl)
            else:
                ya, yb, (h, cb, S) = _even_prefill(p.reshape(B, T, EVEN_IN), pos, stl, Wl)
            new['lru_h'].append(h)
            new['lru_conv'].append(cb)
            new['ret_S'].append(S)
            wout = W['ev_w_out']
        else:
            p_ml, p_rw = _proj(x2, W['od_w_in'], j, tm_proj, (2 * D_HALF, RW_COLS_PAD))
            p_rw3 = p_rw.reshape(B, T, RW_COLS_PAD)
            if decode:
                ya, (C, n, m, cb) = _mlstm_decode(p_ml, stl, Wl)
                outs = _rwkv_pre_decode(p_rw, stl['rw_shift'], Wl)
            else:
                ya, (C, n, m, cb) = _mlstm_prefill(p_ml.reshape(B, T, 2 * D_HALF), stl, Wl)
                outs = _rwkv_pre_prefill(p_rw3, stl['rw_shift'], Wl)
            r, d, k, v, a, b, g, bonus = outs
            y, S = _rwkv_recurrence(r, d, k, v, a, b, stl['rw_S'])
            yb = _rwkv_post(y.reshape(M, D_HALF), g.reshape(M, D_HALF), bonus.reshape(M, D_HALF), Wl, tm_rw)
            new['ml_C'].append(C)
            new['ml_n'].append(n)
            new['ml_m'].append(m)
            new['ml_conv'].append(cb)
            new['rw_S'].append(S)
            new['rw_shift'].append(p_rw3[:, T - 1, :RW_SHIFT_COLS])
            wout = W['od_w_out']
        x2 = _post(ya.reshape(M, D_HALF), yb.reshape(M, D_HALF), x2, wout, j, l,
                   W['ln1_g'], W['ln1_b'], W['mlp_w1'], W['mlp_w2'], W['ln2_g'], W['ln2_b'], tm_post)
    return x2.reshape(B, T, D_MODEL), {name: jnp.stack(v) for name, v in new.items()}


def _zero_states(batch):
    z = lambda *s: jnp.zeros(s, F32)
    n_even, n_odd = (DEPTH + 1) // 2, DEPTH // 2
    return dict(lru_h=z(n_even, batch, D_HALF), lru_conv=z(n_even, batch, CONV_W - 1, D_HALF),
                ret_S=z(n_even, batch, RET_HEADS, RET_DH, RET_DH),
                ml_C=z(n_odd, batch, ML_HEADS, ML_DH, ML_DH), ml_n=z(n_odd, batch, ML_HEADS, ML_DH),
                ml_m=z(n_odd, batch, ML_HEADS), ml_conv=z(n_odd, batch, CONV_W - 1, D_HALF),
                rw_S=z(n_odd, batch, RW_HEADS, RW_DH, RW_DH), rw_shift=z(n_odd, batch, RW_SHIFT_COLS))


def kernel(x_prompt, x_sample, state_lru_h, state_lru_conv, state_ret, state_mlstm_C, state_mlstm_n, state_mlstm_m, state_mlstm_conv, state_rwkv_S, state_rwkv_shift, ln1_g, ln1_b, ln2_g, ln2_b, mlp_w1, mlp_w2, ev_w_in, ev_w_out, lru_conv_w, lru_conv_b, lru_wa, lru_ba, lru_wx, lru_bx, lru_lambda, ret_gn_g, ret_gn_b, od_w_in, od_w_out, ml_conv_w, ml_conv_b, ml_wq, ml_wk, ml_wv, ml_w_gate, ml_b_gate, ml_gn_g, ml_gn_b, ml_skip, rw_mu, rw_w0, rw_w2, rw_a0, rw_a2, rw_g2, rw_kk, rw_ka, rw_rk, rw_gn_g, rw_gn_b):
    even = dict(lru_conv_w=lru_conv_w, lru_conv_b=lru_conv_b, lru_wa_d=jax.vmap(_lru_dense)(lru_wa), lru_ba=lru_ba,
                lru_wx_d=jax.vmap(_lru_dense)(lru_wx), lru_bx=lru_bx, lru_lambda=lru_lambda,
                ret_gn_g=ret_gn_g, ret_gn_b=ret_gn_b)
    odd = dict(ml_conv_w=ml_conv_w, ml_conv_b=ml_conv_b, ml_wq_d=jax.vmap(_ml_dense)(ml_wq),
               ml_wk_d=jax.vmap(_ml_dense)(ml_wk), ml_wv_d=jax.vmap(_ml_dense)(ml_wv),
               ml_w_gate=ml_w_gate, ml_b_gate=ml_b_gate, ml_gn_g=ml_gn_g, ml_gn_b=ml_gn_b, ml_skip=ml_skip,
               rw_mu=rw_mu, rw_w0=rw_w0, rw_w2=rw_w2, rw_a0=rw_a0, rw_a2=rw_a2, rw_g2=rw_g2,
               rw_kk=rw_kk, rw_ka=rw_ka, rw_rk=rw_rk, rw_gn_g=rw_gn_g, rw_gn_b=rw_gn_b)
    W = dict(per_pair=(even, odd),
             ev_w_in=ev_w_in.astype(BF16), ev_w_out=ev_w_out.astype(BF16),
             od_w_in=_pad_lanes(od_w_in, ODD_IN_PAD).astype(BF16), od_w_out=od_w_out.astype(BF16),
             mlp_w1=mlp_w1.astype(BF16), mlp_w2=mlp_w2.astype(BF16),
             ln1_g=ln1_g, ln1_b=ln1_b, ln2_g=ln2_g, ln2_b=ln2_b)
    st_sample = dict(lru_h=state_lru_h, lru_conv=state_lru_conv, ret_S=state_ret,
                     ml_C=state_mlstm_C, ml_n=state_mlstm_n, ml_m=state_mlstm_m, ml_conv=state_mlstm_conv,
                     rw_S=state_rwkv_S, rw_shift=state_rwkv_shift)
    pos_prompt = jnp.arange(x_prompt.shape[1], dtype=jnp.int32)
    pos_sample = PAST_LEN + jnp.arange(x_sample.shape[1], dtype=jnp.int32)
    y_prompt, sp = _trunk(x_prompt, _zero_states(x_prompt.shape[0]), pos_prompt, W)
    y_sample, ss = _trunk(x_sample, st_sample, pos_sample, W)
    names = ('lru_h', 'lru_conv', 'ret_S', 'ml_C', 'ml_n', 'ml_m', 'ml_conv', 'rw_S', 'rw_shift')
    return (y_prompt, y_sample) + tuple(sp[n] for n in names) + tuple(ss[n] for n in names)
```

```python
import functools

import jax
import jax.numpy as jnp
from jax import lax
from jax.experimental import pallas as pl
from jax.experimental.pallas import tpu as pltpu

F32 = jnp.float32
BF16 = jnp.bfloat16

D_MODEL = 1024
DEPTH = 4
PAST_LEN = 16384
D_HALF = D_MODEL // 2
CONV_W = 4
LRU_BLOCKS = 8
LRU_BLOCK = D_HALF // LRU_BLOCKS
LRU_C = 8.0
RET_HEADS = 4
RET_DH = D_HALF // RET_HEADS
CHUNK = 128
ROPE_BASE = 10000.0
ML_HEADS = 4
ML_DH = D_HALF // ML_HEADS
ML_QKV_BLOCK = 4
ML_NBLK = D_HALF // ML_QKV_BLOCK
RW_HEADS = 8
RW_DH = D_HALF // RW_HEADS
RW_DECAY_LORA = 32
RW_A_LORA = 32
RW_GATE_LORA = 96
RW_LORA = RW_DECAY_LORA + RW_A_LORA + RW_GATE_LORA
RW_SHIFT_COLS = 3 * D_HALF + RW_LORA
D_FF = 4 * D_MODEL
ALPHA = (2.0 * DEPTH) ** 0.25
EVEN_IN = 6 * D_HALF
ODD_IN = 2 * D_HALF + RW_SHIFT_COLS
LN_EPS = 1e-5
GN_EPS = 1e-5
RW_GN_EPS = 64e-5

LANES = 128
SUBLANES = 8
RW_LORA_PAD = 2 * LANES
RW_COLS_PAD = 3 * D_HALF + RW_LORA_PAD
ODD_IN_PAD = 2 * D_HALF + RW_COLS_PAD
RW_BB = 8
VMEM_LIMIT = 56 * 1024 * 1024


def _params(sem):
    return pltpu.CompilerParams(dimension_semantics=sem, vmem_limit_bytes=VMEM_LIMIT)


def _dot(a, b):
    return jnp.dot(a.astype(BF16), b.astype(BF16), preferred_element_type=F32)


def _dot_nt(a, b):
    return lax.dot_general(a.astype(BF16), b.astype(BF16), (((1,), (1,)), ((), ())),
                           preferred_element_type=F32)


def _dot_tn(a, b):
    return lax.dot_general(a.astype(BF16), b.astype(BF16), (((0,), (0,)), ((), ())),
                           preferred_element_type=F32)


def _split3(a):
    hi = a.astype(BF16)
    r1 = a - hi.astype(F32)
    mid = r1.astype(BF16)
    lo = (r1 - mid.astype(F32)).astype(BF16)
    return hi, mid, lo


def _xdot(a, b01):
    hi, mid, lo = _split3(a)
    f = lambda t: jnp.dot(t, b01, preferred_element_type=F32)
    return f(hi) + f(mid) + f(lo)


def _xdot_l(b01, a):
    hi, mid, lo = _split3(a)
    f = lambda t: jnp.dot(b01, t, preferred_element_type=F32)
    return f(hi) + f(mid) + f(lo)


def _sigmoid(x):
    return 1.0 / (1.0 + jnp.exp(-x))


def _silu(x):
    return x * _sigmoid(x)


def _softplus(x):
    return jnp.maximum(x, 0.0) + jnp.log1p(jnp.exp(-jnp.abs(x)))


def _gelu_tanh(x):
    return 0.5 * x * (1.0 + jnp.tanh(0.7978845608028654 * (x + 0.044715 * (x * x * x))))


def _layer_norm(x, g, b, eps):
    mu = jnp.mean(x, -1, keepdims=True)
    xc = x - mu
    var = jnp.mean(xc * xc, -1, keepdims=True)
    return xc * lax.rsqrt(var + eps) * g + b


def _group_ones(n, group):
    r = lax.broadcasted_iota(jnp.int32, (n, n), 0) // group
    c = lax.broadcasted_iota(jnp.int32, (n, n), 1) // group
    return jnp.where(r == c, 1.0, 0.0).astype(BF16)


def _group_sum(x, ones_bd):
    parts = [_xdot(x[:, s * LANES:(s + 1) * LANES], ones_bd) for s in range(x.shape[1] // LANES)]
    return jnp.concatenate(parts, axis=-1)


def _rotate(x, cosf, sinf):
    return x * cosf + pltpu.roll(x, RET_DH // 2, 1) * sinf


def _proj_kernel(x_ref, w_ref, *o_refs, splits):
    xb = x_ref[...].astype(BF16)
    off = 0
    for o_ref, n in zip(o_refs, splits):
        o_ref[...] = jnp.dot(xb, w_ref[:, off:off + n], preferred_element_type=F32)
        off += n


def _proj(x2d, w_all, j, tm, splits):
    M = x2d.shape[0]
    N = w_all.shape[2]
    return pl.pallas_call(
        functools.partial(_proj_kernel, splits=splits),
        grid=(M // tm,),
        in_specs=[pl.BlockSpec((tm, D_MODEL), lambda i: (i, 0)),
                  pl.BlockSpec((None, D_MODEL, N), lambda i: (j, 0, 0))],
        out_specs=[pl.BlockSpec((tm, n), lambda i: (i, 0)) for n in splits],
        out_shape=[jax.ShapeDtypeStruct((M, n), F32) for n in splits],
        compiler_params=_params(("parallel",)),
        name="proj_in",
    )(x2d, w_all)


FF_CHUNK = 1024


def _post_kernel(ya_ref, yb_ref, x_ref, wo_ref, g1_ref, b1_ref, w1_ref, w2_ref, g2_ref, b2_ref, o_ref):
    y = (jnp.dot(ya_ref[...].astype(BF16), wo_ref[0:D_HALF, :], preferred_element_type=F32)
         + jnp.dot(yb_ref[...].astype(BF16), wo_ref[D_HALF:D_MODEL, :], preferred_element_type=F32))
    x1 = _layer_norm(ALPHA * x_ref[...] + y, g1_ref[...], b1_ref[...], LN_EPS)
    x1b = x1.astype(BF16)
    acc = jnp.zeros(x1.shape, F32)
    for c in range(D_FF // FF_CHUNK):
        h = jnp.dot(x1b, w1_ref[:, c * FF_CHUNK:(c + 1) * FF_CHUNK], preferred_element_type=F32)
        h = jnp.square(jnp.maximum(h, 0.0))
        acc = acc + jnp.dot(h.astype(BF16), w2_ref[c * FF_CHUNK:(c + 1) * FF_CHUNK, :],
                            preferred_element_type=F32)
    o_ref[...] = _layer_norm(ALPHA * x1 + acc, g2_ref[...], b2_ref[...], LN_EPS)


def _post(ya, yb, x2d, wout_all, j, l, ln1_g, ln1_b, w1_all, w2_all, ln2_g, ln2_b, tm):
    M = x2d.shape[0]
    row = lambda i: (i, 0)
    vec = pl.BlockSpec((None, 1, D_MODEL), lambda i: (l, 0, 0))
    r3 = lambda a: a.reshape(DEPTH, 1, D_MODEL)
    return pl.pallas_call(
        _post_kernel,
        grid=(M // tm,),
        in_specs=[pl.BlockSpec((tm, D_HALF), row), pl.BlockSpec((tm, D_HALF), row),
                  pl.BlockSpec((tm, D_MODEL), row),
                  pl.BlockSpec((None, D_MODEL, D_MODEL), lambda i: (j, 0, 0)),
                  vec, vec,
                  pl.BlockSpec((None, D_MODEL, D_FF), lambda i: (l, 0, 0)),
                  pl.BlockSpec((None, D_FF, D_MODEL), lambda i: (l, 0, 0)),
                  vec, vec],
        out_specs=pl.BlockSpec((tm, D_MODEL), row),
        out_shape=jax.ShapeDtypeStruct((M, D_MODEL), F32),
        compiler_params=_params(("parallel",)),
        name="post_mlp",
    )(ya, yb, x2d, wout_all, r3(ln1_g), r3(ln1_b), w1_all, w2_all, r3(ln2_g), r3(ln2_b))


def _conv_prefill(x, xbuf, cw_ref, cb_ref, L):
    xbuf[SUBLANES:SUBLANES + L, :] = x
    y = cb_ref[...]
    for i in range(CONV_W):
        y = y + cw_ref[i:i + 1, :] * xbuf[SUBLANES - (CONV_W - 1) + i:SUBLANES - (CONV_W - 1) + i + L, :]
    tail = xbuf[L + SUBLANES - (CONV_W - 1):L + SUBLANES, :]
    xbuf[SUBLANES - (CONV_W - 1):SUBLANES, :] = tail
    return y, tail


def _lru_gates(xc, wa_ref, ba_ref, wx_ref, bx_ref, lam_ref):
    xcb = xc.astype(BF16)
    nslab = D_HALF // LANES
    ra = jnp.concatenate([jnp.dot(xcb[:, s * LANES:(s + 1) * LANES], wa_ref[s], preferred_element_type=F32)
                          for s in range(nslab)], axis=-1)
    rx = jnp.concatenate([jnp.dot(xcb[:, s * LANES:(s + 1) * LANES], wx_ref[s], preferred_element_type=F32)
                          for s in range(nslab)], axis=-1)
    r = _sigmoid(ra + ba_ref[...])
    i = _sigmoid(rx + bx_ref[...])
    log_a = -LRU_C * r * _softplus(-lam_ref[...])
    a = jnp.exp(log_a)
    t = jnp.tanh(log_a)
    u = jnp.sqrt(-2.0 * t / (1.0 - t)) * (i * xc)
    return a, u


def _head_norm(o, g, b, eps):
    mu = jnp.mean(o, -1, keepdims=True)
    oc = o - mu
    var = jnp.mean(oc * oc, -1, keepdims=True)
    return oc * lax.rsqrt(var + eps) * g + b


def _even_prefill_kernel(xa_ref, ga_ref, q_ref, k_ref, v_ref, gb_ref,
                         cw_ref, cb_ref, wa_ref, ba_ref, wx_ref, bx_ref, lam_ref,
                         cos_ref, sin_ref, dmask_ref, qd_ref, kd_ref, cd_ref, gng_ref, gnb_ref,
                         h0_ref, conv0_ref, S0_ref,
                         ya_ref, yb_ref, h_ref, conv_ref, S_ref, xbuf, *, L):
    c = pl.program_id(1)

    @pl.when(c == 0)
    def _():
        h_ref[...] = h0_ref[...]
        S_ref[...] = S0_ref[...]
        xbuf[SUBLANES - (CONV_W - 1):SUBLANES, :] = conv0_ref[...]

    xc, tail = _conv_prefill(xa_ref[...], xbuf, cw_ref, cb_ref, L)
    conv_ref[...] = tail
    a, u = _lru_gates(xc, wa_ref, ba_ref, wx_ref, bx_ref, lam_ref)
    row = lax.broadcasted_iota(jnp.int32, (L, D_HALF), 0)
    s = 1
    while s < L:
        keep = row >= s
        a_sh = jnp.where(keep, pltpu.roll(a, s, 0), 1.0)
        u_sh = jnp.where(keep, pltpu.roll(u, s, 0), 0.0)
        u = a * u_sh + u
        a = a * a_sh
        s *= 2
    h = a * h_ref[...] + u
    h_ref[...] = h[L - 1:L, :]
    ya_ref[...] = _gelu_tanh(ga_ref[...]) * h

    cosf = cos_ref[...]
    sinf = sin_ref[...]
    for hh in range(RET_HEADS):
        sl = slice(hh * RET_DH, (hh + 1) * RET_DH)
        qh = _rotate(q_ref[:, sl], cosf, sinf)
        kh = _rotate(k_ref[:, sl], cosf, sinf) * (RET_DH ** -0.5)
        vh = v_ref[:, sl]
        S = S_ref[hh]
        scores = _dot_nt(qh, kh) * dmask_ref[hh]
        o = _dot(scores, vh) + _dot(qh * qd_ref[:, sl], S)
        S_ref[hh] = S * cd_ref[hh] + _dot_tn(kh * kd_ref[:, sl], vh)
        on = _head_norm(o, gng_ref[:, sl], gnb_ref[:, sl], GN_EPS)
        yb_ref[:, sl] = _silu(gb_ref[:, sl]) * on


def _ret_tables(L):
    log_gamma = jnp.log1p(-jnp.exp2(-5.0 - jnp.arange(RET_HEADS, dtype=F32)))
    idx = jnp.arange(L, dtype=F32)
    diff = idx[:, None] - idx[None, :]
    dmask = jnp.where(diff >= 0, jnp.exp(log_gamma[:, None, None] * jnp.maximum(diff, 0.0)), 0.0)
    qd = jnp.exp(log_gamma[:, None] * (idx + 1.0))
    kd = jnp.exp(log_gamma[:, None] * (L - 1.0 - idx))
    cd = jnp.exp(log_gamma * L)
    qd_full = jnp.repeat(qd.T, RET_DH, axis=1)
    kd_full = jnp.repeat(kd.T, RET_DH, axis=1)
    cd_full = jnp.broadcast_to(cd[:, None, None], (RET_HEADS, 1, RET_DH))
    return dmask, qd_full, kd_full, cd_full


def _rope_tables(pos):
    half = RET_DH // 2
    inv = ROPE_BASE ** (-jnp.arange(half, dtype=F32) / half)
    ang = pos.astype(F32)[:, None] * inv[None, :]
    cos, sin = jnp.cos(ang), jnp.sin(ang)
    return jnp.concatenate([cos, cos], -1), jnp.concatenate([-sin, sin], -1)


def _lru_dense(w):
    pairs = LANES // LRU_BLOCK
    w4 = w.reshape(LRU_BLOCKS // pairs, pairs, LRU_BLOCK, LRU_BLOCK)
    eye = jnp.eye(pairs, dtype=w.dtype)
    d = w4[:, :, :, None, :] * eye[None, :, None, :, None]
    return d.reshape(LRU_BLOCKS // pairs, LANES, LANES).astype(BF16)


def _full(shape):
    n = len(shape)
    return pl.BlockSpec(shape, lambda *_: (0,) * n)


def _even_prefill(p3, pos, st, Wl):
    B, T, _ = p3.shape
    L = CHUNK
    nc = T // L
    dmask, qd, kd, cd = _ret_tables(L)
    cosf, sinf = _rope_tables(pos)
    col = lambda i: pl.BlockSpec((None, L, D_HALF), lambda b, c: (b, c, i))
    perb = lambda *s: pl.BlockSpec((None,) + s, lambda b, c: (b,) + (0,) * len(s))
    row2 = lambda a: a.reshape(1, D_HALF)
    ins = [p3] * 6 + [Wl['lru_conv_w'], row2(Wl['lru_conv_b']), Wl['lru_wa_d'], row2(Wl['lru_ba']),
                      Wl['lru_wx_d'], row2(Wl['lru_bx']), row2(Wl['lru_lambda']),
                      cosf, sinf, dmask, qd, kd, cd, row2(Wl['ret_gn_g']), row2(Wl['ret_gn_b']),
                      st['lru_h'].reshape(B, 1, D_HALF), st['lru_conv'], st['ret_S']]
    in_specs = [col(i) for i in range(6)] + [
        _full((CONV_W, D_HALF)), _full((1, D_HALF)), _full((4, LANES, LANES)), _full((1, D_HALF)),
        _full((4, LANES, LANES)), _full((1, D_HALF)), _full((1, D_HALF)),
        pl.BlockSpec((L, RET_DH), lambda b, c: (c, 0)), pl.BlockSpec((L, RET_DH), lambda b, c: (c, 0)),
        _full((RET_HEADS, L, L)), _full((L, D_HALF)), _full((L, D_HALF)), _full((RET_HEADS, 1, RET_DH)),
        _full((1, D_HALF)), _full((1, D_HALF)),
        perb(1, D_HALF), perb(CONV_W - 1, D_HALF), perb(RET_HEADS, RET_DH, RET_DH)]
    seq = pl.BlockSpec((None, L, D_HALF), lambda b, c: (b, c, 0))
    ya, yb, h, conv, S = pl.pallas_call(
        functools.partial(_even_prefill_kernel, L=L),
        grid=(B, nc),
        in_specs=in_specs,
        out_specs=[seq, seq, perb(1, D_HALF), perb(CONV_W - 1, D_HALF), perb(RET_HEADS, RET_DH, RET_DH)],
        out_shape=[jax.ShapeDtypeStruct((B, T, D_HALF), F32), jax.ShapeDtypeStruct((B, T, D_HALF), F32),
                   jax.ShapeDtypeStruct((B, 1, D_HALF), F32),
                   jax.ShapeDtypeStruct((B, CONV_W - 1, D_HALF), F32),
                   jax.ShapeDtypeStruct((B, RET_HEADS, RET_DH, RET_DH), F32)],
        scratch_shapes=[pltpu.VMEM((L + SUBLANES, D_HALF), F32)],
        compiler_params=_params(("parallel", "arbitrary")),
        name="even_prefill",
    )(*ins)
    return ya, yb, (h.reshape(B, D_HALF), conv, S)


DEC_BB = 8


def _even_decode_kernel(p_ref, cw_ref, cb_ref, wa_ref, ba_ref, wx_ref, bx_ref, lam_ref,
                        cos_ref, sin_ref, dm_ref, qd_ref, kd_ref, cd_ref, gng_ref, gnb_ref,
                        h0_ref, conv0_ref, S0_ref,
                        ya_ref, yb_ref, h_ref, conv_ref, S_ref):
    col = lambda i: p_ref[:, i * D_HALF:(i + 1) * D_HALF]
    xa = col(0)
    xc = cb_ref[...] + cw_ref[CONV_W - 1:CONV_W, :] * xa
    for i in range(CONV_W - 1):
        xc = xc + cw_ref[i:i + 1, :] * conv0_ref[i]
    for i in range(CONV_W - 2):
        conv_ref[i] = conv0_ref[i + 1]
    conv_ref[CONV_W - 2] = xa
    a, u = _lru_gates(xc, wa_ref, ba_ref, wx_ref, bx_ref, lam_ref)
    h = a * h0_ref[...] + u
    h_ref[...] = h
    ya_ref[...] = _gelu_tanh(col(1)) * h

    cosf = cos_ref[...]
    sinf = sin_ref[...]
    row8 = lax.broadcasted_iota(jnp.int32, (SUBLANES, RET_DH), 0)
    q, k, v, gb = col(2), col(3), col(4), col(5)
    for hh in range(RET_HEADS):
        sl = slice(hh * RET_DH, (hh + 1) * RET_DH)
        qh = _rotate(q[:, sl], cosf, sinf)
        kh = _rotate(k[:, sl], cosf, sinf) * (RET_DH ** -0.5)
        vh = v[:, sl]
        qk = jnp.sum(qh * kh, -1, keepdims=True) * dm_ref[:, sl]
        qq = qh * qd_ref[:, sl]
        kk = kh * kd_ref[:, sl]
        rows = []
        for bi in range(DEC_BB):
            S = S0_ref[bi, hh]
            q8 = jnp.broadcast_to(qq[bi:bi + 1, :], (SUBLANES, RET_DH))
            rows.append(_dot_f32(q8, S)[0:1, :])
            k8 = jnp.where(row8 == 0, jnp.broadcast_to(kk[bi:bi + 1, :], (SUBLANES, RET_DH)), 0.0)
            v8 = jnp.broadcast_to(vh[bi:bi + 1, :], (SUBLANES, RET_DH))
            S_ref[bi, hh] = S * cd_ref[hh] + _dot_tn_f32(k8, v8)
        o = qk * vh + jnp.concatenate(rows, axis=0)
        on = _head_norm(o, gng_ref[:, sl], gnb_ref[:, sl], GN_EPS)
        yb_ref[:, sl] = _silu(gb[:, sl]) * on


def _dot_f32(a, b):
    return jnp.dot(a, b, preferred_element_type=F32)


def _dot_tn_f32(a, b):
    return lax.dot_general(a, b, (((0,), (0,)), ((), ())), preferred_element_type=F32)


def _even_decode(p2, pos, st, Wl):
    B = p2.shape[0]
    bb = DEC_BB
    dmask, qd, kd, cd = _ret_tables(1)
    dm = jnp.repeat(dmask[:, 0, :].T, RET_DH, axis=1)
    cosf, sinf = _rope_tables(pos)
    row2 = lambda a: a.reshape(1, D_HALF)
    rows = lambda n: pl.BlockSpec((bb, n), lambda i: (i, 0))
    convs = pl.BlockSpec((CONV_W - 1, bb, D_HALF), lambda i: (0, i, 0))
    Ss = pl.BlockSpec((bb, RET_HEADS, RET_DH, RET_DH), lambda i: (i, 0, 0, 0))
    ins = [p2, Wl['lru_conv_w'], row2(Wl['lru_conv_b']), Wl['lru_wa_d'], row2(Wl['lru_ba']),
           Wl['lru_wx_d'], row2(Wl['lru_bx']), row2(Wl['lru_lambda']),
           cosf, sinf, dm, qd, kd, cd, row2(Wl['ret_gn_g']), row2(Wl['ret_gn_b']),
           st['lru_h'], jnp.swapaxes(st['lru_conv'], 0, 1), st['ret_S']]
    in_specs = [rows(EVEN_IN), _full((CONV_W, D_HALF)), _full((1, D_HALF)), _full((4, LANES, LANES)),
                _full((1, D_HALF)), _full((4, LANES, LANES)), _full((1, D_HALF)), _full((1, D_HALF)),
                _full((1, RET_DH)), _full((1, RET_DH)), _full((1, D_HALF)), _full((1, D_HALF)),
                _full((1, D_HALF)), _full((RET_HEADS, 1, RET_DH)), _full((1, D_HALF)), _full((1, D_HALF)),
                rows(D_HALF), convs, Ss]
    ya, yb, h, conv, S = pl.pallas_call(
        _even_decode_kernel,
        grid=(B // bb,),
        in_specs=in_specs,
        out_specs=[rows(D_HALF), rows(D_HALF), rows(D_HALF), convs, Ss],
        out_shape=[jax.ShapeDtypeStruct((B, D_HALF), F32), jax.ShapeDtypeStruct((B, D_HALF), F32),
                   jax.ShapeDtypeStruct((B, D_HALF), F32),
                   jax.ShapeDtypeStruct((CONV_W - 1, B, D_HALF), F32),
                   jax.ShapeDtypeStruct((B, RET_HEADS, RET_DH, RET_DH), F32)],
        compiler_params=_params(("parallel",)),
        name="even_decode",
    )(*ins)
    return ya, yb, (h, jnp.swapaxes(conv, 0, 1), S)


def _mlstm_qkv_gates(xm, xc, wq_ref, wk_ref, wv_ref, wg_ref, bg_ref):
    q = _dot(xc, wq_ref[...])
    k = _dot(xc, wk_ref[...])
    v = _dot(xm, wv_ref[...])
    g_col = (_dot(q, wg_ref[0:D_HALF, :]) + _dot(k, wg_ref[D_HALF:2 * D_HALF, :])
             + _dot(v, wg_ref[2 * D_HALF:3 * D_HALF, :]) + bg_ref[...])
    return q, k, v, g_col


def _mlstm_prefill_kernel(xm_ref, z_ref, cw_ref, cb_ref, wq_ref, wk_ref, wv_ref, wg_ref, bg_ref,
                          wgt_ref, bgt_ref, gng_ref, gnb_ref, skip_ref,
                          conv0_ref, C0_ref, n0_ref, m0_ref,
                          yc_ref, conv_ref, C_ref, n_ref, m_ref, xbuf, *, L):
    c = pl.program_id(1)

    @pl.when(c == 0)
    def _():
        C_ref[...] = C0_ref[...]
        n_ref[...] = n0_ref[...]
        m_ref[...] = m0_ref[...]
        xbuf[SUBLANES - (CONV_W - 1):SUBLANES, :] = conv0_ref[...]

    xm = xm_ref[...]
    xc, tail = _conv_prefill(xm, xbuf, cw_ref, cb_ref, L)
    conv_ref[...] = tail
    xc = _silu(xc)
    q, k, v, g_col = _mlstm_qkv_gates(xm, xc, wq_ref, wk_ref, wv_ref, wg_ref, bg_ref)
    g_row = (_dot_nt(wgt_ref[:, 0:D_HALF], q) + _dot_nt(wgt_ref[:, D_HALF:2 * D_HALF], k)
             + _dot_nt(wgt_ref[:, 2 * D_HALF:3 * D_HALF], v) + bgt_ref[...])
    ri = lax.broadcasted_iota(jnp.int32, (L, L), 0)
    ci = lax.broadcasted_iota(jnp.int32, (L, L), 1)
    causal = ri >= ci
    tril = jnp.where(causal, 1.0, 0.0).astype(BF16)
    triu = jnp.where(ci >= ri, 1.0, 0.0).astype(BF16)
    b_col = _xdot_l(tril, -_softplus(-g_col))
    b_row = _xdot(-_softplus(-g_row), triu)
    lane = lax.broadcasted_iota(jnp.int32, (1, LANES), 1)
    m_all = m_ref[...]
    for hh in range(ML_HEADS):
        sl = slice(hh * ML_DH, (hh + 1) * ML_DH)
        qh, vh = q[:, sl], v[:, sl]
        kh = k[:, sl] * (ML_DH ** -0.5)
        bc = b_col[:, ML_HEADS + hh:ML_HEADS + hh + 1]
        br = b_row[ML_HEADS + hh:ML_HEADS + hh + 1, :]
        lic = g_col[:, hh:hh + 1]
        lir = g_row[hh:hh + 1, :]
        m_prev = m_all[:, hh:hh + 1]
        C = C_ref[hh]
        n = n_ref[:, sl]
        logD = jnp.where(causal, bc - br + lir, -jnp.inf)
        log_inter = bc + m_prev
        m_t = jnp.maximum(jnp.max(logD, -1, keepdims=True), log_inter)
        s = _dot_nt(qh, kh) * jnp.exp(logD - m_t)
        w_inter = jnp.exp(log_inter - m_t)
        num = _dot(s, vh) + w_inter * _dot(qh, C)
        den = jnp.sum(s, -1, keepdims=True) + w_inter * jnp.sum(qh * n, -1, keepdims=True)
        hcell = num / jnp.maximum(jnp.abs(den), jnp.exp(-m_t))
        m_new = m_t[L - 1:L, :]
        b_last = bc[L - 1:L, :]
        w_k = jnp.exp(b_last - bc + lic - m_new)
        w_C = jnp.exp(b_last + m_prev - m_new)
        kw = kh * w_k
        C_ref[hh] = w_C * C + _dot_tn(kw, vh)
        n_ref[:, sl] = w_C * n + jnp.sum(kw, axis=0, keepdims=True)
        m_all = jnp.where(lane == hh, m_new, m_all)
        hn = _head_norm(hcell, gng_ref[:, sl], gnb_ref[:, sl], GN_EPS)
        yc_ref[:, sl] = (hn + skip_ref[:, sl] * xc[:, sl]) * _silu(z_ref[:, sl])
    m_ref[...] = m_all


def _ml_dense(w):
    eye = jnp.eye(ML_NBLK, dtype=w.dtype)
    d = w[:, :, None, :] * eye[:, None, :, None]
    return d.reshape(D_HALF, D_HALF).astype(BF16)


def _pad_lanes(a, n=LANES):
    return jnp.pad(a, [(0, 0)] * (a.ndim - 1) + [(0, n - a.shape[-1])])


def _mlstm_weights(Wl):
    wg = _pad_lanes(Wl['ml_w_gate']).astype(BF16)
    bg = _pad_lanes(Wl['ml_b_gate'].reshape(1, 2 * ML_HEADS))
    wgt = Wl['ml_w_gate'].T.astype(BF16)
    bgt = jnp.broadcast_to(Wl['ml_b_gate'].reshape(2 * ML_HEADS, 1), (2 * ML_HEADS, LANES))
    return wg, bg, wgt, bgt


def _mlstm_prefill(p3, st, Wl):
    B, T, _ = p3.shape
    L = CHUNK
    nc = T // L
    wg, bg, wgt, bgt = _mlstm_weights(Wl)
    col = lambda i: pl.BlockSpec((None, L, D_HALF), lambda b, c: (b, c, i))
    perb = lambda *s: pl.BlockSpec((None,) + s, lambda b, c: (b,) + (0,) * len(s))
    row2 = lambda a: a.reshape(1, D_HALF)
    ins = [p3, p3, Wl['ml_conv_w'], row2(Wl['ml_conv_b']), Wl['ml_wq_d'], Wl['ml_wk_d'], Wl['ml_wv_d'],
           wg, bg, wgt, bgt, row2(Wl['ml_gn_g']), row2(Wl['ml_gn_b']), row2(Wl['ml_skip']),
           st['ml_conv'], st['ml_C'], st['ml_n'].reshape(B, 1, D_HALF),
           _pad_lanes(st['ml_m']).reshape(B, 1, LANES)]
    in_specs = [col(0), col(1), _full((CONV_W, D_HALF)), _full((1, D_HALF)),
                _full((D_HALF, D_HALF)), _full((D_HALF, D_HALF)), _full((D_HALF, D_HALF)),
                _full((3 * D_HALF, LANES)), _full((1, LANES)), _full((2 * ML_HEADS, 3 * D_HALF)),
                _full((2 * ML_HEADS, LANES)), _full((1, D_HALF)), _full((1, D_HALF)), _full((1, D_HALF)),
                perb(CONV_W - 1, D_HALF), perb(ML_HEADS, ML_DH, ML_DH), perb(1, D_HALF), perb(1, LANES)]
    seq = pl.BlockSpec((None, L, D_HALF), lambda b, c: (b, c, 0))
    yc, conv, C, n, m = pl.pallas_call(
        functools.partial(_mlstm_prefill_kernel, L=L),
        grid=(B, nc),
        in_specs=in_specs,
        out_specs=[seq, perb(CONV_W - 1, D_HALF), perb(ML_HEADS, ML_DH, ML_DH), perb(1, D_HALF), perb(1, LANES)],
        out_shape=[jax.ShapeDtypeStruct((B, T, D_HALF), F32),
                   jax.ShapeDtypeStruct((B, CONV_W - 1, D_HALF), F32),
                   jax.ShapeDtypeStruct((B, ML_HEADS, ML_DH, ML_DH), F32),
                   jax.ShapeDtypeStruct((B, 1, D_HALF), F32),
                   jax.ShapeDtypeStruct((B, 1, LANES), F32)],
        scratch_shapes=[pltpu.VMEM((L + SUBLANES, D_HALF), F32)],
        compiler_params=_params(("parallel", "arbitrary")),
        name="mlstm_prefill",
    )(*ins)
    return yc, (C, n.reshape(B, ML_HEADS, ML_DH), m[:, 0, :ML_HEADS], conv)


def _mlstm_decode_kernel(p_ref, cw_ref, cb_ref, wq_ref, wk_ref, wv_ref, wg_ref, bg_ref,
                         gng_ref, gnb_ref, skip_ref, conv0_ref, C0_ref, n0_ref, m0_ref,
                         yc_ref, conv_ref, C_ref, n_ref, m_ref):
    xm = p_ref[:, 0:D_HALF]
    z = p_ref[:, D_HALF:2 * D_HALF]
    xc = cb_ref[...] + cw_ref[CONV_W - 1:CONV_W, :] * xm
    for i in range(CONV_W - 1):
        xc = xc + cw_ref[i:i + 1, :] * conv0_ref[i]
    for i in range(CONV_W - 2):
        conv_ref[i] = conv0_ref[i + 1]
    conv_ref[CONV_W - 2] = xm
    xc = _silu(xc)
    q, k, v, g = _mlstm_qkv_gates(xm, xc, wq_ref, wk_ref, wv_ref, wg_ref, bg_ref)
    lf_all = -_softplus(-g)
    lane = lax.broadcasted_iota(jnp.int32, (1, LANES), 1)
    row8 = lax.broadcasted_iota(jnp.int32, (SUBLANES, ML_DH), 0)
    m_all = m0_ref[...]
    m_out = m_all
    for hh in range(ML_HEADS):
        sl = slice(hh * ML_DH, (hh + 1) * ML_DH)
        qh, vh = q[:, sl], v[:, sl]
        kh = k[:, sl] * (ML_DH ** -0.5)
        li = g[:, hh:hh + 1]
        lf = lf_all[:, ML_HEADS + hh:ML_HEADS + hh + 1]
        m_prev = m_all[:, hh:hh + 1]
        n = n0_ref[:, sl]
        log_inter = lf + m_prev
        m_t = jnp.maximum(li, log_inter)
        s = jnp.sum(qh * kh, -1, keepdims=True) * jnp.exp(li - m_t)
        w_inter = jnp.exp(log_inter - m_t)
        w_k = jnp.exp(li - m_t)
        w_C = jnp.exp(log_inter - m_t)
        kw = kh * w_k
        rows = []
        for bi in range(DEC_BB):
            C = C0_ref[bi, hh]
            q8 = jnp.broadcast_to(qh[bi:bi + 1, :], (SUBLANES, ML_DH))
            rows.append(_dot_f32(q8, C)[0:1, :])
            k8 = jnp.where(row8 == 0, jnp.broadcast_to(kw[bi:bi + 1, :], (SUBLANES, ML_DH)), 0.0)
            v8 = jnp.broadcast_to(vh[bi:bi + 1, :], (SUBLANES, ML_DH))
            C_ref[bi, hh] = w_C[bi:bi + 1, :] * C + _dot_tn_f32(k8, v8)
        qC = jnp.concatenate(rows, axis=0)
        num = s * vh + w_inter * qC
        den = s + w_inter * jnp.sum(qh * n, -1, keepdims=True)
        hcell = num / jnp.maximum(jnp.abs(den), jnp.exp(-m_t))
        n_ref[:, sl] = w_C * n + kw
        m_out = jnp.where(lane == hh, m_t, m_out)
        hn = _head_norm(hcell, gng_ref[:, sl], gnb_ref[:, sl], GN_EPS)
        yc_ref[:, sl] = (hn + skip_ref[:, sl] * xc[:, sl]) * _silu(z[:, sl])
    m_ref[...] = m_out


def _mlstm_decode(p2, st, Wl):
    B = p2.shape[0]
    bb = DEC_BB
    wg, bg, _, _ = _mlstm_weights(Wl)
    row2 = lambda a: a.reshape(1, D_HALF)
    rows = lambda n: pl.BlockSpec((bb, n), lambda i: (i, 0))
    convs = pl.BlockSpec((CONV_W - 1, bb, D_HALF), lambda i: (0, i, 0))
    Cs = pl.BlockSpec((bb, ML_HEADS, ML_DH, ML_DH), lambda i: (i, 0, 0, 0))
    ins = [p2, Wl['ml_conv_w'], row2(Wl['ml_conv_b']), Wl['ml_wq_d'], Wl['ml_wk_d'], Wl['ml_wv_d'], wg, bg,
           row2(Wl['ml_gn_g']), row2(Wl['ml_gn_b']), row2(Wl['ml_skip']),
           jnp.swapaxes(st['ml_conv'], 0, 1), st['ml_C'], st['ml_n'].reshape(B, D_HALF), _pad_lanes(st['ml_m'])]
    in_specs = [pl.BlockSpec((bb, 2 * D_HALF), lambda i: (i, 0)), _full((CONV_W, D_HALF)), _full((1, D_HALF)),
                _full((D_HALF, D_HALF)), _full((D_HALF, D_HALF)), _full((D_HALF, D_HALF)),
                _full((3 * D_HALF, LANES)), _full((1, LANES)),
                _full((1, D_HALF)), _full((1, D_HALF)), _full((1, D_HALF)),
                convs, Cs, rows(D_HALF), rows(LANES)]
    yc, conv, C, n, m = pl.pallas_call(
        _mlstm_decode_kernel,
        grid=(B // bb,),
        in_specs=in_specs,
        out_specs=[rows(D_HALF), convs, Cs, rows(D_HALF), rows(LANES)],
        out_shape=[jax.ShapeDtypeStruct((B, D_HALF), F32),
                   jax.ShapeDtypeStruct((CONV_W - 1, B, D_HALF), F32),
                   jax.ShapeDtypeStruct((B, ML_HEADS, ML_DH, ML_DH), F32),
                   jax.ShapeDtypeStruct((B, D_HALF), F32),
                   jax.ShapeDtypeStruct((B, LANES), F32)],
        compiler_params=_params(("parallel",)),
        name="mlstm_decode",
    )(*ins)
    return yc, (C, n.reshape(B, ML_HEADS, ML_DH), m[:, :ML_HEADS], jnp.swapaxes(conv, 0, 1))


def _rwkv_pre_body(pr, pr_prev, mu_ref, w0_ref, a0_ref, w2_ref, a2_ref, g2_ref, kkw_ref, kaw_ref, rk_ref,
                   r_ref, d_ref, k_ref, v_ref, a_ref, b_ref, g_ref, bonus_ref):
    pm = pr + (pr_prev - pr) * mu_ref[...]
    r = pm[:, 0:D_HALF]
    kr = pm[:, D_HALF:2 * D_HALF]
    vr = pm[:, 2 * D_HALF:3 * D_HALF]
    lo = pm[:, 3 * D_HALF:RW_COLS_PAD]
    w_log = -_softplus(-(w0_ref[...] + _dot(jnp.tanh(lo), w2_ref[...]))) - 0.5
    a = _sigmoid(a0_ref[...] + _dot(lo, a2_ref[...]))
    g = _dot(_sigmoid(lo), g2_ref[...])
    ones_bd = _group_ones(LANES, RW_DH)
    kk = kr * kkw_ref[...]
    kk = kk / jnp.maximum(jnp.sqrt(_group_sum(kk * kk, ones_bd)), 1e-12)
    kh = kr * (1.0 + (a - 1.0) * kaw_ref[...])
    r_ref[...] = r
    d_ref[...] = jnp.exp(-jnp.exp(w_log))
    k_ref[...] = kh
    v_ref[...] = vr
    a_ref[...] = -kk
    b_ref[...] = kk * a
    g_ref[...] = g
    bonus_ref[...] = _group_sum(r * kh * rk_ref[...], ones_bd) * vr


def _rwkv_pre_prefill_kernel(pr_ref, shift0_ref, *rest, L):
    wrefs, outs, xbuf = rest[:9], rest[9:17], rest[17]
    c = pl.program_id(1)

    @pl.when(c == 0)
    def _():
        xbuf[SUBLANES - 1:SUBLANES, :] = shift0_ref[...]

    pr = pr_ref[...]
    xbuf[SUBLANES:SUBLANES + L, :] = pr
    pr_prev = xbuf[SUBLANES - 1:SUBLANES - 1 + L, :]
    xbuf[SUBLANES - 1:SUBLANES, :] = pr[L - 1:L, :]
    _rwkv_pre_body(pr, pr_prev, *wrefs, *outs)


def _rwkv_pre_decode_kernel(pr_ref, prev_ref, *rest):
    _rwkv_pre_body(pr_ref[...], prev_ref[...], *rest[:9], *rest[9:17])


def _rwkv_pre_weights(Wl):
    row2 = lambda a: a.reshape(1, D_HALF)
    padr = lambda w, o: jnp.pad(w, ((o, RW_LORA_PAD - o - w.shape[0]), (0, 0))).astype(BF16)
    mu = _pad_lanes(Wl['rw_mu'].reshape(1, RW_SHIFT_COLS), RW_COLS_PAD)
    ws = [mu, row2(Wl['rw_w0']), row2(Wl['rw_a0']),
          padr(Wl['rw_w2'], 0), padr(Wl['rw_a2'], RW_DECAY_LORA), padr(Wl['rw_g2'], RW_DECAY_LORA + RW_A_LORA),
          row2(Wl['rw_kk']), row2(Wl['rw_ka']), row2(Wl['rw_rk'])]
    specs = [_full((1, RW_COLS_PAD)), _full((1, D_HALF)), _full((1, D_HALF)),
             _full((RW_LORA_PAD, D_HALF)), _full((RW_LORA_PAD, D_HALF)), _full((RW_LORA_PAD, D_HALF)),
             _full((1, D_HALF)), _full((1, D_HALF)), _full((1, D_HALF))]
    return ws, specs


def _rwkv_pre_prefill(pr3, shift0, Wl):
    B, T, _ = pr3.shape
    L = CHUNK
    ws, wspecs = _rwkv_pre_weights(Wl)
    seq = pl.BlockSpec((None, L, D_HALF), lambda b, c: (b, c, 0))
    outs = pl.pallas_call(
        functools.partial(_rwkv_pre_prefill_kernel, L=L),
        grid=(B, T // L),
        in_specs=[pl.BlockSpec((None, L, RW_COLS_PAD), lambda b, c: (b, c, 0)),
                  pl.BlockSpec((None, 1, RW_COLS_PAD), lambda b, c: (b, 0, 0))] + wspecs,
        out_specs=[seq] * 8,
        out_shape=[jax.ShapeDtypeStruct((B, T, D_HALF), F32)] * 8,
        scratch_shapes=[pltpu.VMEM((L + SUBLANES, RW_COLS_PAD), F32)],
        compiler_params=_params(("parallel", "arbitrary")),
        name="rwkv_pre_prefill",
    )(pr3, _pad_lanes(shift0, RW_COLS_PAD).reshape(B, 1, RW_COLS_PAD), *ws)
    return outs


def _rwkv_pre_decode(pr, shift0, Wl):
    B = pr.shape[0]
    ws, wspecs = _rwkv_pre_weights(Wl)
    full2 = lambda n: pl.BlockSpec((B, n), lambda i: (0, 0))
    outs = pl.pallas_call(
        _rwkv_pre_decode_kernel,
        grid=(1,),
        in_specs=[full2(RW_COLS_PAD), full2(RW_COLS_PAD)] + wspecs,
        out_specs=[full2(D_HALF)] * 8,
        out_shape=[jax.ShapeDtypeStruct((B, D_HALF), F32)] * 8,
        compiler_params=_params(("arbitrary",)),
        name="rwkv_pre_decode",
    )(pr, _pad_lanes(shift0, RW_COLS_PAD), *ws)
    return [o.reshape(B, 1, D_HALF) for o in outs]


RW_IP = RW_DH // 2


def _rwkv_rec_kernel(r_ref, d_ref, k_ref, a_ref, b_ref, v_ref, S0_ref, y_ref, S_ref, *, Tc):
    @pl.when(pl.program_id(1) == 0)
    def _():
        S_ref[...] = S0_ref[...]

    def step(t, carry):
        a, d, b, k, r = a_ref[t], d_ref[t], b_ref[t], k_ref[t], r_ref[t]
        vt = v_ref[t]
        rows = []
        for ip in range(RW_IP):
            S = S_ref[ip]
            sa = jnp.sum(S * a, axis=0, keepdims=True)
            Sn = S * d + sa * b + vt[ip:ip + 1, :] * k
            S_ref[ip] = Sn
            rows.append(jnp.sum(Sn * r, axis=0, keepdims=True))
        y_ref[t] = jnp.concatenate(rows, axis=0)
        return carry

    lax.fori_loop(0, Tc, step, 0)


def _rwkv_rec_call(r, d, k, a, b, v, S0):
    nbb, T = r.shape[:2]
    Tc = min(T, 32)
    vec = pl.BlockSpec((None, Tc, RW_DH, LANES), lambda i, c: (i, c, 0, 0))
    vsp = pl.BlockSpec((None, Tc, RW_IP, LANES), lambda i, c: (i, c, 0, 0))
    ssp = pl.BlockSpec((None, RW_IP, RW_DH, LANES), lambda i, c: (i, 0, 0, 0))
    return pl.pallas_call(
        functools.partial(_rwkv_rec_kernel, Tc=Tc),
        grid=(nbb, T // Tc),
        in_specs=[vec] * 5 + [vsp, ssp],
        out_specs=[vsp, ssp],
        out_shape=[jax.ShapeDtypeStruct((nbb, T, RW_IP, LANES), F32),
                   jax.ShapeDtypeStruct((nbb, RW_IP, RW_DH, LANES), F32)],
        compiler_params=_params(("parallel", "arbitrary")),
        name="rwkv_recurrence",
    )(r, d, k, a, b, v, S0)


def _to_rec(x, dup):
    B, T, _ = x.shape
    nbb = B // RW_BB
    y = x.reshape(nbb, RW_BB, T, RW_HEADS, RW_DH).transpose(0, 2, 4, 1, 3).reshape(nbb, T, RW_DH, RW_BB * RW_HEADS)
    return jnp.concatenate([y, y], axis=-1) if dup else y


def _rwkv_recurrence(r, d, k, v, a, b, S0):
    B, T, _ = r.shape
    nbb = B // RW_BB
    Tc = min(T, 32)
    half = RW_BB * RW_HEADS
    vv = _to_rec(v, False).reshape(nbb, T, RW_IP, 2 * half)
    Sr = (S0.reshape(nbb, RW_BB, RW_HEADS, RW_IP, 2, RW_DH).transpose(0, 3, 5, 4, 1, 2)
          .reshape(nbb, RW_IP, RW_DH, 2 * half))
    y, S = _rwkv_rec_call(_to_rec(r, True), _to_rec(d, True), _to_rec(k, True), _to_rec(a, True),
                          _to_rec(b, True), vv, Sr)
    y = (y.reshape(nbb, T, RW_DH, RW_BB, RW_HEADS).transpose(0, 3, 1, 4, 2).reshape(B, T, D_HALF))
    S = (S.reshape(nbb, RW_IP, RW_DH, 2, RW_BB, RW_HEADS).transpose(0, 4, 5, 1, 3, 2)
         .reshape(B, RW_HEADS, RW_DH, RW_DH))
    return y, S


RW_TB = LANES // RW_BB


def _head_sum_rows(x):
    x3 = x.reshape(RW_HEADS, RW_DH, x.shape[-1])
    s = jnp.sum(x3, axis=1, keepdims=True)
    return jnp.broadcast_to(s, x3.shape).reshape(x.shape)


def _rwkv_pre_t_kernel(x_ref, shift0_ref, w_ref, mu_ref, w0_ref, a0_ref, w2_ref, a2_ref, g2_ref,
                       kkw_ref, kaw_ref, rk_ref,
                       r_ref, d_ref, k_ref, a_ref, b_ref, v_ref, g_ref, bonus_ref, last_ref, prev_scr):
    @pl.when(pl.program_id(0) == 0)
    def _():
        prev_scr[...] = shift0_ref[...]

    xg = jnp.concatenate([x_ref[:, t, :] for t in range(RW_TB)], axis=0).astype(BF16)
    pr = lax.dot_general(w_ref[...], xg, (((1,), (1,)), ((), ())), preferred_element_type=F32)
    lane = lax.broadcasted_iota(jnp.int32, (1, LANES), 1)
    prev = jnp.where(lane < RW_BB, pltpu.roll(prev_scr[...], RW_BB, 1), pltpu.roll(pr, RW_BB, 1))
    prev_scr[...] = pr
    last_ref[...] = pr
    pm = pr + (prev - pr) * mu_ref[...]
    r = pm[0:D_HALF]
    kr = pm[D_HALF:2 * D_HALF]
    vr = pm[2 * D_HALF:3 * D_HALF]
    lo = pm[3 * D_HALF:RW_COLS_PAD]
    w_log = -_softplus(-(w0_ref[...] + _dot(w2_ref[...], jnp.tanh(lo)))) - 0.5
    a = _sigmoid(a0_ref[...] + _dot(a2_ref[...], lo))
    g = _dot(g2_ref[...], _sigmoid(lo))
    kk = kr * kkw_ref[...]
    kk = kk / jnp.maximum(jnp.sqrt(_head_sum_rows(kk * kk)), 1e-12)
    kh = kr * (1.0 + (a - 1.0) * kaw_ref[...])
    g_ref[...] = g
    bonus_ref[...] = _head_sum_rows(r * kh * rk_ref[...]) * vr

    grp = lane // RW_BB
    ngrp = LANES // RW_BB

    def scatter(x, o_ref, nrow, npiece):
        rot = [x[q * nrow:(q + 1) * nrow, :] if q == 0 else pltpu.roll(x[q * nrow:(q + 1) * nrow, :], q * RW_BB, 1)
               for q in range(npiece)]
        for t in range(RW_TB):
            m = rot[0]
            for q in range(1, npiece):
                m = jnp.where(grp == (t + q) % ngrp, rot[q], m)
            if t:
                m = pltpu.roll(m, LANES - t * RW_BB, 1)
            if npiece < ngrp:
                m = jnp.where(lane < LANES // 2, m, pltpu.roll(m, LANES // 2, 1))
            o_ref[t] = m

    scatter(r, r_ref, RW_DH, RW_HEADS)
    scatter(jnp.exp(-jnp.exp(w_log)), d_ref, RW_DH, RW_HEADS)
    scatter(kh, k_ref, RW_DH, RW_HEADS)
    scatter(-kk, a_ref, RW_DH, RW_HEADS)
    scatter(kk * a, b_ref, RW_DH, RW_HEADS)
    vv = jnp.concatenate([vr[h * RW_DH + half * RW_IP:h * RW_DH + (half + 1) * RW_IP, :]
                          for half in range(2) for h in range(RW_HEADS)], axis=0)
    scatter(vv, v_ref, RW_IP, ngrp)


def _lane_bcast(a, n):
    return jnp.broadcast_to(a.reshape(n, 1), (n, LANES))


def _rwkv_pre_t(x3, shift0, w_rwt, j, Wl):
    B, T, _ = x3.shape
    nblk = T // RW_TB
    padr = lambda w, o: jnp.pad(w, ((o, RW_LORA_PAD - o - w.shape[0]), (0, 0))).astype(BF16).T
    sh = jnp.pad(shift0.T, ((0, RW_COLS_PAD - RW_SHIFT_COLS), (LANES - RW_BB, 0)))
    col = lambda a: _lane_bcast(a, D_HALF)
    ins = [x3, sh, w_rwt, _lane_bcast(_pad_lanes(Wl['rw_mu'].reshape(1, -1), RW_COLS_PAD), RW_COLS_PAD),
           col(Wl['rw_w0']), col(Wl['rw_a0']),
           padr(Wl['rw_w2'], 0), padr(Wl['rw_a2'], RW_DECAY_LORA), padr(Wl['rw_g2'], RW_DECAY_LORA + RW_A_LORA),
           col(Wl['rw_kk']), col(Wl['rw_ka']), col(Wl['rw_rk'])]
    in_specs = [pl.BlockSpec((B, RW_TB, D_MODEL), lambda c: (0, c, 0)), _full((RW_COLS_PAD, LANES)),
                pl.BlockSpec((None, RW_COLS_PAD, D_MODEL), lambda c: (j, 0, 0)), _full((RW_COLS_PAD, LANES)),
                _full((D_HALF, LANES)), _full((D_HALF, LANES)),
                _full((D_HALF, RW_LORA_PAD)), _full((D_HALF, RW_LORA_PAD)), _full((D_HALF, RW_LORA_PAD)),
                _full((D_HALF, LANES)), _full((D_HALF, LANES)), _full((D_HALF, LANES))]
    blk = pl.BlockSpec((None, D_HALF, LANES), lambda c: (c, 0, 0))
    ktile = pl.BlockSpec((RW_TB, RW_DH, LANES), lambda c: (c, 0, 0))
    vtile = pl.BlockSpec((RW_TB, RW_IP, LANES), lambda c: (c, 0, 0))
    outs = pl.pallas_call(
        _rwkv_pre_t_kernel,
        grid=(nblk,),
        in_specs=in_specs,
        out_specs=[ktile] * 5 + [vtile, blk, blk, _full((RW_COLS_PAD, LANES))],
        out_shape=[jax.ShapeDtypeStruct((T, RW_DH, LANES), F32)] * 5
                  + [jax.ShapeDtypeStruct((T, RW_IP, LANES), F32)]
                  + [jax.ShapeDtypeStruct((nblk, D_HALF, LANES), F32)] * 2
                  + [jax.ShapeDtypeStruct((RW_COLS_PAD, LANES), F32)],
        scratch_shapes=[pltpu.VMEM((RW_COLS_PAD, LANES), F32)],
        compiler_params=_params(("arbitrary",)),
        name="rwkv_pre_t",
    )(*ins)
    shift_new = outs[8][:RW_SHIFT_COLS, LANES - RW_BB:].T
    return outs[:8], shift_new


def _rwkv_post_t_kernel(y_ref, g_ref, bonus_ref, gng_ref, gnb_ref, yd_ref):
    lane = lax.broadcasted_iota(jnp.int32, (1, LANES), 1)
    grp = lane // RW_BB
    ngrp = LANES // RW_BB
    ys = [y_ref[t] for t in range(RW_TB)]
    rolled = []
    for s in range(ngrp):
        m = ys[s % RW_TB]
        for q in range(1, ngrp):
            m = jnp.where(grp == q, ys[(q + s) % RW_TB], m)
        rolled.append(pltpu.roll(m, s * RW_BB, 1) if s else m)
    pieces = {}
    for q in range(ngrp):
        m = rolled[(-q) % ngrp]
        for t in range(1, RW_TB):
            m = jnp.where(grp == t, rolled[(t - q) % ngrp], m)
        pieces[divmod(q, RW_HEADS)] = m
    y = jnp.concatenate([pieces[(half, h)] for h in range(RW_HEADS) for half in range(2)], axis=0)
    mu = _head_sum_rows(y) * (1.0 / RW_DH)
    yc = y - mu
    var = _head_sum_rows(yc * yc) * (1.0 / RW_DH)
    hn = yc * lax.rsqrt(var + RW_GN_EPS) * gng_ref[...] + gnb_ref[...]
    yd = ((hn + bonus_ref[...]) * g_ref[...]).T
    for t in range(RW_TB):
        yd_ref[:, t, :] = yd[t * RW_BB:(t + 1) * RW_BB, :]


def _rwkv_rec_t(vecs, S0, Wl, T):
    r, d, k, a, b, v, g, bonus = vecs
    nblk = T // RW_TB
    B = RW_BB
    Sr = (S0.reshape(B, RW_HEADS, 2, RW_IP, RW_DH).transpose(3, 4, 2, 1, 0).reshape(1, RW_IP, RW_DH, LANES))
    y, S = _rwkv_rec_call(*(t[None] for t in (r, d, k, a, b, v)), Sr)
    blk = pl.BlockSpec((None, D_HALF, LANES), lambda c: (c, 0, 0))
    col = lambda a_: _lane_bcast(a_, D_HALF)
    yd = pl.pallas_call(
        _rwkv_post_t_kernel,
        grid=(nblk,),
        in_specs=[pl.BlockSpec((RW_TB, RW_IP, LANES), lambda c: (c, 0, 0)), blk, blk,
                  _full((D_HALF, LANES)), _full((D_HALF, LANES))],
        out_specs=pl.BlockSpec((B, RW_TB, D_HALF), lambda c: (0, c, 0)),
        out_shape=jax.ShapeDtypeStruct((B, T, D_HALF), F32),
        compiler_params=_params(("parallel",)),
        name="rwkv_post_t",
    )(y[0], g, bonus, col(Wl['rw_gn_g']), col(Wl['rw_gn_b']))
    S = S[0].reshape(RW_IP, RW_DH, 2, RW_HEADS, B).transpose(4, 3, 2, 0, 1).reshape(B, RW_HEADS, RW_DH, RW_DH)
    return yd, S


def _rwkv_post_kernel(y_ref, g_ref, bonus_ref, gng_ref, gnb_ref, o_ref):
    ones_bd = _group_ones(LANES, RW_DH)
    y = y_ref[...]
    mu = _group_sum(y, ones_bd) * (1.0 / RW_DH)
    yc = y - mu
    var = _group_sum(yc * yc, ones_bd) * (1.0 / RW_DH)
    hn = yc * lax.rsqrt(var + RW_GN_EPS) * gng_ref[...] + gnb_ref[...]
    o_ref[...] = (hn + bonus_ref[...]) * g_ref[...]


def _rwkv_post(y2, g2, bonus2, Wl, tm):
    M = y2.shape[0]
    row = pl.BlockSpec((tm, D_HALF), lambda i: (i, 0))
    return pl.pallas_call(
        _rwkv_post_kernel,
        grid=(M // tm,),
        in_specs=[row, row, row, _full((1, D_HALF)), _full((1, D_HALF))],
        out_specs=row,
        out_shape=jax.ShapeDtypeStruct((M, D_HALF), F32),
        compiler_params=_params(("parallel",)),
        name="rwkv_post",
    )(y2, g2, bonus2, Wl['rw_gn_g'].reshape(1, D_HALF), Wl['rw_gn_b'].reshape(1, D_HALF))


def _trunk(x, st, pos, W):
    B, T, _ = x.shape
    M = B * T
    decode = T == 1
    tm_proj = min(M, 512)
    tm_post = min(M, 256)
    tm_rw = min(M, 512)
    x2 = x.reshape(M, D_MODEL)
    new = {name: [] for name in st}
    for l in range(DEPTH):
        j = l // 2
        Wl = {name: v[j] for name, v in W['per_pair'][l % 2].items()}
        stl = {name: v[j] for name, v in st.items()}
        if l % 2 == 0:
            p, = _proj(x2, W['ev_w_in'], j, tm_proj, (EVEN_IN,))
            if decode:
                ya, yb, (h, cb, S) = _even_decode(p, pos, stl, Wl)
            else:
                ya, yb, (h, cb, S) = _even_prefill(p.reshape(B, T, EVEN_IN), pos, stl, Wl)
            new['lru_h'].append(h)
            new['lru_conv'].append(cb)
            new['ret_S'].append(S)
            wout = W['ev_w_out']
        else:
            if decode:
                p_ml, p_rw = _proj(x2, W['od_w_in'], j, tm_proj, (2 * D_HALF, RW_COLS_PAD))
                ya, (C, n, m, cb) = _mlstm_decode(p_ml, stl, Wl)
                r, d, k, v, a, b, g, bonus = _rwkv_pre_decode(p_rw, stl['rw_shift'], Wl)
                y, S = _rwkv_recurrence(r, d, k, v, a, b, stl['rw_S'])
                yb = _rwkv_post(y.reshape(M, D_HALF), g.reshape(M, D_HALF), bonus.reshape(M, D_HALF), Wl, tm_rw)
                shift_new = p_rw[:, :RW_SHIFT_COLS]
            else:
                p_ml, = _proj(x2, W['od_w_in'], j, tm_proj, (2 * D_HALF,))
                ya, (C, n, m, cb) = _mlstm_prefill(p_ml.reshape(B, T, 2 * D_HALF), stl, Wl)
                vecs, shift_new = _rwkv_pre_t(x2.reshape(B, T, D_MODEL), stl['rw_shift'], W['od_w_rwt'], j, Wl)
                yb, S = _rwkv_rec_t(vecs, stl['rw_S'], Wl, T)
            new['ml_C'].append(C)
            new['ml_n'].append(n)
            new['ml_m'].append(m)
            new['ml_conv'].append(cb)
            new['rw_S'].append(S)
            new['rw_shift'].append(shift_new)
            wout = W['od_w_out']
        x2 = _post(ya.reshape(M, D_HALF), yb.reshape(M, D_HALF), x2, wout, j, l,
                   W['ln1_g'], W['ln1_b'], W['mlp_w1'], W['mlp_w2'], W['ln2_g'], W['ln2_b'], tm_post)
    return x2.reshape(B, T, D_MODEL), {name: jnp.stack(v) for name, v in new.items()}


def _prepare_weights(w):
    even_names = ('lru_conv_w', 'lru_conv_b', 'lru_ba', 'lru_bx', 'lru_lambda', 'ret_gn_g', 'ret_gn_b')
    odd_names = ('ml_conv_w', 'ml_conv_b', 'ml_w_gate', 'ml_b_gate', 'ml_gn_g', 'ml_gn_b', 'ml_skip',
                 'rw_mu', 'rw_w0', 'rw_w2', 'rw_a0', 'rw_a2', 'rw_g2', 'rw_kk', 'rw_ka', 'rw_rk', 'rw_gn_g', 'rw_gn_b')
    even = {n: w[n] for n in even_names}
    even.update(lru_wa_d=jax.vmap(_lru_dense)(w['lru_wa']), lru_wx_d=jax.vmap(_lru_dense)(w['lru_wx']))
    odd = {n: w[n] for n in odd_names}
    odd.update(ml_wq_d=jax.vmap(_ml_dense)(w['ml_wq']), ml_wk_d=jax.vmap(_ml_dense)(w['ml_wk']),
               ml_wv_d=jax.vmap(_ml_dense)(w['ml_wv']))
    od_w_in = _pad_lanes(w['od_w_in'], ODD_IN_PAD).astype(BF16)
    return dict(per_pair=(even, odd),
                ev_w_in=w['ev_w_in'].astype(BF16), ev_w_out=w['ev_w_out'].astype(BF16),
                od_w_in=od_w_in, od_w_rwt=jnp.swapaxes(od_w_in[:, :, 2 * D_HALF:], 1, 2),
                od_w_out=w['od_w_out'].astype(BF16),
                mlp_w1=w['mlp_w1'].astype(BF16), mlp_w2=w['mlp_w2'].astype(BF16),
                ln1_g=w['ln1_g'], ln1_b=w['ln1_b'], ln2_g=w['ln2_g'], ln2_b=w['ln2_b'])


def _zero_states(batch):
    z = lambda *s: jnp.zeros(s, F32)
    n_even, n_odd = (DEPTH + 1) // 2, DEPTH // 2
    return dict(lru_h=z(n_even, batch, D_HALF), lru_conv=z(n_even, batch, CONV_W - 1, D_HALF),
                ret_S=z(n_even, batch, RET_HEADS, RET_DH, RET_DH),
                ml_C=z(n_odd, batch, ML_HEADS, ML_DH, ML_DH), ml_n=z(n_odd, batch, ML_HEADS, ML_DH),
                ml_m=z(n_odd, batch, ML_HEADS), ml_conv=z(n_odd, batch, CONV_W - 1, D_HALF),
                rw_S=z(n_odd, batch, RW_HEADS, RW_DH, RW_DH), rw_shift=z(n_odd, batch, RW_SHIFT_COLS))


def kernel(x_prompt, x_sample, state_lru_h, state_lru_conv, state_ret, state_mlstm_C, state_mlstm_n, state_mlstm_m, state_mlstm_conv, state_rwkv_S, state_rwkv_shift, ln1_g, ln1_b, ln2_g, ln2_b, mlp_w1, mlp_w2, ev_w_in, ev_w_out, lru_conv_w, lru_conv_b, lru_wa, lru_ba, lru_wx, lru_bx, lru_lambda, ret_gn_g, ret_gn_b, od_w_in, od_w_out, ml_conv_w, ml_conv_b, ml_wq, ml_wk, ml_wv, ml_w_gate, ml_b_gate, ml_gn_g, ml_gn_b, ml_skip, rw_mu, rw_w0, rw_w2, rw_a0, rw_a2, rw_g2, rw_kk, rw_ka, rw_rk, rw_gn_g, rw_gn_b):
    W = _prepare_weights(dict(
        ln1_g=ln1_g, ln1_b=ln1_b, ln2_g=ln2_g, ln2_b=ln2_b, mlp_w1=mlp_w1, mlp_w2=mlp_w2,
        ev_w_in=ev_w_in, ev_w_out=ev_w_out, lru_conv_w=lru_conv_w, lru_conv_b=lru_conv_b,
        lru_wa=lru_wa, lru_ba=lru_ba, lru_wx=lru_wx, lru_bx=lru_bx, lru_lambda=lru_lambda,
        ret_gn_g=ret_gn_g, ret_gn_b=ret_gn_b, od_w_in=od_w_in, od_w_out=od_w_out,
        ml_conv_w=ml_conv_w, ml_conv_b=ml_conv_b, ml_wq=ml_wq, ml_wk=ml_wk, ml_wv=ml_wv,
        ml_w_gate=ml_w_gate, ml_b_gate=ml_b_gate, ml_gn_g=ml_gn_g, ml_gn_b=ml_gn_b, ml_skip=ml_skip,
        rw_mu=rw_mu, rw_w0=rw_w0, rw_w2=rw_w2, rw_a0=rw_a0, rw_a2=rw_a2, rw_g2=rw_g2,
        rw_kk=rw_kk, rw_ka=rw_ka, rw_rk=rw_rk, rw_gn_g=rw_gn_g, rw_gn_b=rw_gn_b))
    st_sample = dict(lru_h=state_lru_h, lru_conv=state_lru_conv, ret_S=state_ret,
                     ml_C=state_mlstm_C, ml_n=state_mlstm_n, ml_m=state_mlstm_m, ml_conv=state_mlstm_conv,
                     rw_S=state_rwkv_S, rw_shift=state_rwkv_shift)
    pos_prompt = jnp.arange(x_prompt.shape[1], dtype=jnp.int32)
    pos_sample = PAST_LEN + jnp.arange(x_sample.shape[1], dtype=jnp.int32)
    y_prompt, sp = _trunk(x_prompt, _zero_states(x_prompt.shape[0]), pos_prompt, W)
    y_sample, ss = _trunk(x_sample, st_sample, pos_sample, W)
    names = ('lru_h', 'lru_conv', 'ret_S', 'ml_C', 'ml_n', 'ml_m', 'ml_conv', 'rw_S', 'rw_shift')
    return (y_prompt, y_sample) + tuple(sp[n] for n in names) + tuple(ss[n] for n in names)
```

```python
import functools

import jax
import jax.numpy as jnp
from jax import lax
from jax.experimental import pallas as pl
from jax.experimental.pallas import tpu as pltpu

F32 = jnp.float32
BF16 = jnp.bfloat16

D_MODEL = 1024
DEPTH = 4
PAST_LEN = 16384
D_HALF = D_MODEL // 2
CONV_W = 4
LRU_BLOCKS = 8
LRU_BLOCK = D_HALF // LRU_BLOCKS
LRU_C = 8.0
RET_HEADS = 4
RET_DH = D_HALF // RET_HEADS
CHUNK = 128
ROPE_BASE = 10000.0
ML_HEADS = 4
ML_DH = D_HALF // ML_HEADS
ML_QKV_BLOCK = 4
ML_NBLK = D_HALF // ML_QKV_BLOCK
RW_HEADS = 8
RW_DH = D_HALF // RW_HEADS
RW_DECAY_LORA = 32
RW_A_LORA = 32
RW_GATE_LORA = 96
RW_LORA = RW_DECAY_LORA + RW_A_LORA + RW_GATE_LORA
RW_SHIFT_COLS = 3 * D_HALF + RW_LORA
D_FF = 4 * D_MODEL
ALPHA = (2.0 * DEPTH) ** 0.25
EVEN_IN = 6 * D_HALF
ODD_IN = 2 * D_HALF + RW_SHIFT_COLS
LN_EPS = 1e-5
GN_EPS = 1e-5
RW_GN_EPS = 64e-5

LANES = 128
SUBLANES = 8
RW_LORA_PAD = 2 * LANES
RW_COLS_PAD = 3 * D_HALF + RW_LORA_PAD
ODD_IN_PAD = 2 * D_HALF + RW_COLS_PAD
RW_BB = 8
VMEM_LIMIT = 56 * 1024 * 1024


def _params(sem):
    return pltpu.CompilerParams(dimension_semantics=sem, vmem_limit_bytes=VMEM_LIMIT)


def _dot(a, b):
    return jnp.dot(a.astype(BF16), b.astype(BF16), preferred_element_type=F32)


def _dot_nt(a, b):
    return lax.dot_general(a.astype(BF16), b.astype(BF16), (((1,), (1,)), ((), ())),
                           preferred_element_type=F32)


def _dot_tn(a, b):
    return lax.dot_general(a.astype(BF16), b.astype(BF16), (((0,), (0,)), ((), ())),
                           preferred_element_type=F32)


def _split3(a):
    hi = a.astype(BF16)
    r1 = a - hi.astype(F32)
    mid = r1.astype(BF16)
    lo = (r1 - mid.astype(F32)).astype(BF16)
    return hi, mid, lo


def _xdot(a, b01):
    hi, mid, lo = _split3(a)
    f = lambda t: jnp.dot(t, b01, preferred_element_type=F32)
    return f(hi) + f(mid) + f(lo)


def _xdot_l(b01, a):
    hi, mid, lo = _split3(a)
    f = lambda t: jnp.dot(b01, t, preferred_element_type=F32)
    return f(hi) + f(mid) + f(lo)


def _sigmoid(x):
    return 1.0 / (1.0 + jnp.exp(-x))


def _silu(x):
    return x * _sigmoid(x)


def _softplus(x):
    return jnp.maximum(x, 0.0) + jnp.log1p(jnp.exp(-jnp.abs(x)))


def _gelu_tanh(x):
    return 0.5 * x * (1.0 + jnp.tanh(0.7978845608028654 * (x + 0.044715 * (x * x * x))))


def _layer_norm(x, g, b, eps):
    mu = jnp.mean(x, -1, keepdims=True)
    xc = x - mu
    var = jnp.mean(xc * xc, -1, keepdims=True)
    return xc * lax.rsqrt(var + eps) * g + b


def _group_ones(n, group):
    r = lax.broadcasted_iota(jnp.int32, (n, n), 0) // group
    c = lax.broadcasted_iota(jnp.int32, (n, n), 1) // group
    return jnp.where(r == c, 1.0, 0.0).astype(BF16)


def _group_sum(x, ones_bd):
    parts = [_xdot(x[:, s * LANES:(s + 1) * LANES], ones_bd) for s in range(x.shape[1] // LANES)]
    return jnp.concatenate(parts, axis=-1)


def _rotate(x, cosf, sinf):
    return x * cosf + pltpu.roll(x, RET_DH // 2, 1) * sinf


def _proj_kernel(x_ref, w_ref, *o_refs, splits):
    xb = x_ref[...].astype(BF16)
    off = 0
    for o_ref, n in zip(o_refs, splits):
        o_ref[...] = jnp.dot(xb, w_ref[:, off:off + n], preferred_element_type=F32)
        off += n


def _proj(x2d, w_all, j, tm, splits):
    M = x2d.shape[0]
    N = w_all.shape[2]
    return pl.pallas_call(
        functools.partial(_proj_kernel, splits=splits),
        grid=(M // tm,),
        in_specs=[pl.BlockSpec((tm, D_MODEL), lambda i: (i, 0)),
                  pl.BlockSpec((None, D_MODEL, N), lambda i: (j, 0, 0))],
        out_specs=[pl.BlockSpec((tm, n), lambda i: (i, 0)) for n in splits],
        out_shape=[jax.ShapeDtypeStruct((M, n), F32) for n in splits],
        compiler_params=_params(("parallel",)),
        name="proj_in",
    )(x2d, w_all)


FF_CHUNK = 1024


def _post_kernel(ya_ref, yb_ref, x_ref, wo_ref, g1_ref, b1_ref, w1_ref, w2_ref, g2_ref, b2_ref, o_ref):
    y = (jnp.dot(ya_ref[...].astype(BF16), wo_ref[0:D_HALF, :], preferred_element_type=F32)
         + jnp.dot(yb_ref[...].astype(BF16), wo_ref[D_HALF:D_MODEL, :], preferred_element_type=F32))
    x1 = _layer_norm(ALPHA * x_ref[...] + y, g1_ref[...], b1_ref[...], LN_EPS)
    x1b = x1.astype(BF16)
    acc = jnp.zeros(x1.shape, F32)
    for c in range(D_FF // FF_CHUNK):
        h = jnp.dot(x1b, w1_ref[:, c * FF_CHUNK:(c + 1) * FF_CHUNK], preferred_element_type=F32)
        h = jnp.square(jnp.maximum(h, 0.0))
        acc = acc + jnp.dot(h.astype(BF16), w2_ref[c * FF_CHUNK:(c + 1) * FF_CHUNK, :],
                            preferred_element_type=F32)
    o_ref[...] = _layer_norm(ALPHA * x1 + acc, g2_ref[...], b2_ref[...], LN_EPS)


def _post(ya, yb, x2d, wout_all, j, l, ln1_g, ln1_b, w1_all, w2_all, ln2_g, ln2_b, tm):
    M = x2d.shape[0]
    row = lambda i: (i, 0)
    vec = pl.BlockSpec((None, 1, D_MODEL), lambda i: (l, 0, 0))
    r3 = lambda a: a.reshape(DEPTH, 1, D_MODEL)
    return pl.pallas_call(
        _post_kernel,
        grid=(M // tm,),
        in_specs=[pl.BlockSpec((tm, D_HALF), row), pl.BlockSpec((tm, D_HALF), row),
                  pl.BlockSpec((tm, D_MODEL), row),
                  pl.BlockSpec((None, D_MODEL, D_MODEL), lambda i: (j, 0, 0)),
                  vec, vec,
                  pl.BlockSpec((None, D_MODEL, D_FF), lambda i: (l, 0, 0)),
                  pl.BlockSpec((None, D_FF, D_MODEL), lambda i: (l, 0, 0)),
                  vec, vec],
        out_specs=pl.BlockSpec((tm, D_MODEL), row),
        out_shape=jax.ShapeDtypeStruct((M, D_MODEL), F32),
        compiler_params=_params(("parallel",)),
        name="post_mlp",
    )(ya, yb, x2d, wout_all, r3(ln1_g), r3(ln1_b), w1_all, w2_all, r3(ln2_g), r3(ln2_b))


def _conv_prefill(x, xbuf, cw_ref, cb_ref, L):
    xbuf[SUBLANES:SUBLANES + L, :] = x
    y = cb_ref[...]
    for i in range(CONV_W):
        y = y + cw_ref[i:i + 1, :] * xbuf[SUBLANES - (CONV_W - 1) + i:SUBLANES - (CONV_W - 1) + i + L, :]
    tail = xbuf[L + SUBLANES - (CONV_W - 1):L + SUBLANES, :]
    xbuf[SUBLANES - (CONV_W - 1):SUBLANES, :] = tail
    return y, tail


def _lru_gates(xc, wa_ref, ba_ref, wx_ref, bx_ref, lam_ref):
    xcb = xc.astype(BF16)
    nslab = D_HALF // LANES
    ra = jnp.concatenate([jnp.dot(xcb[:, s * LANES:(s + 1) * LANES], wa_ref[s], preferred_element_type=F32)
                          for s in range(nslab)], axis=-1)
    rx = jnp.concatenate([jnp.dot(xcb[:, s * LANES:(s + 1) * LANES], wx_ref[s], preferred_element_type=F32)
                          for s in range(nslab)], axis=-1)
    r = _sigmoid(ra + ba_ref[...])
    i = _sigmoid(rx + bx_ref[...])
    log_a = -LRU_C * r * _softplus(-lam_ref[...])
    a = jnp.exp(log_a)
    t = jnp.tanh(log_a)
    u = jnp.sqrt(-2.0 * t / (1.0 - t)) * (i * xc)
    return a, u


def _head_norm(o, g, b, eps):
    mu = jnp.mean(o, -1, keepdims=True)
    oc = o - mu
    var = jnp.mean(oc * oc, -1, keepdims=True)
    return oc * lax.rsqrt(var + eps) * g + b


def _even_prefill_kernel(xa_ref, ga_ref, q_ref, k_ref, v_ref, gb_ref,
                         cw_ref, cb_ref, wa_ref, ba_ref, wx_ref, bx_ref, lam_ref,
                         cos_ref, sin_ref, dmask_ref, qd_ref, kd_ref, cd_ref, gng_ref, gnb_ref,
                         h0_ref, conv0_ref, S0_ref,
                         ya_ref, yb_ref, h_ref, conv_ref, S_ref, xbuf, *, L):
    c = pl.program_id(1)

    @pl.when(c == 0)
    def _():
        h_ref[...] = h0_ref[...]
        S_ref[...] = S0_ref[...]
        xbuf[SUBLANES - (CONV_W - 1):SUBLANES, :] = conv0_ref[...]

    xc, tail = _conv_prefill(xa_ref[...], xbuf, cw_ref, cb_ref, L)
    conv_ref[...] = tail
    a, u = _lru_gates(xc, wa_ref, ba_ref, wx_ref, bx_ref, lam_ref)
    row = lax.broadcasted_iota(jnp.int32, (L, D_HALF), 0)
    s = 1
    while s < L:
        keep = row >= s
        a_sh = jnp.where(keep, pltpu.roll(a, s, 0), 1.0)
        u_sh = jnp.where(keep, pltpu.roll(u, s, 0), 0.0)
        u = a * u_sh + u
        a = a * a_sh
        s *= 2
    h = a * h_ref[...] + u
    h_ref[...] = h[L - 1:L, :]
    ya_ref[...] = _gelu_tanh(ga_ref[...]) * h

    cosf = cos_ref[...]
    sinf = sin_ref[...]
    for hh in range(RET_HEADS):
        sl = slice(hh * RET_DH, (hh + 1) * RET_DH)
        qh = _rotate(q_ref[:, sl], cosf, sinf)
        kh = _rotate(k_ref[:, sl], cosf, sinf) * (RET_DH ** -0.5)
        vh = v_ref[:, sl]
        S = S_ref[hh]
        scores = _dot_nt(qh, kh) * dmask_ref[hh]
        o = _dot(scores, vh) + _dot(qh * qd_ref[:, sl], S)
        S_ref[hh] = S * cd_ref[hh] + _dot_tn(kh * kd_ref[:, sl], vh)
        on = _head_norm(o, gng_ref[:, sl], gnb_ref[:, sl], GN_EPS)
        yb_ref[:, sl] = _silu(gb_ref[:, sl]) * on


def _ret_tables(L):
    log_gamma = jnp.log1p(-jnp.exp2(-5.0 - jnp.arange(RET_HEADS, dtype=F32)))
    idx = jnp.arange(L, dtype=F32)
    diff = idx[:, None] - idx[None, :]
    dmask = jnp.where(diff >= 0, jnp.exp(log_gamma[:, None, None] * jnp.maximum(diff, 0.0)), 0.0)
    qd = jnp.exp(log_gamma[:, None] * (idx + 1.0))
    kd = jnp.exp(log_gamma[:, None] * (L - 1.0 - idx))
    cd = jnp.exp(log_gamma * L)
    qd_full = jnp.repeat(qd.T, RET_DH, axis=1)
    kd_full = jnp.repeat(kd.T, RET_DH, axis=1)
    cd_full = jnp.broadcast_to(cd[:, None, None], (RET_HEADS, 1, RET_DH))
    return dmask, qd_full, kd_full, cd_full


def _rope_tables(pos):
    half = RET_DH // 2
    inv = ROPE_BASE ** (-jnp.arange(half, dtype=F32) / half)
    ang = pos.astype(F32)[:, None] * inv[None, :]
    cos, sin = jnp.cos(ang), jnp.sin(ang)
    return jnp.concatenate([cos, cos], -1), jnp.concatenate([-sin, sin], -1)


def _lru_dense(w):
    pairs = LANES // LRU_BLOCK
    w4 = w.reshape(LRU_BLOCKS // pairs, pairs, LRU_BLOCK, LRU_BLOCK)
    eye = jnp.eye(pairs, dtype=w.dtype)
    d = w4[:, :, :, None, :] * eye[None, :, None, :, None]
    return d.reshape(LRU_BLOCKS // pairs, LANES, LANES).astype(BF16)


def _full(shape):
    n = len(shape)
    return pl.BlockSpec(shape, lambda *_: (0,) * n)


def _even_prefill(p3, pos, st, Wl):
    B, T, _ = p3.shape
    L = CHUNK
    nc = T // L
    dmask, qd, kd, cd = _ret_tables(L)
    cosf, sinf = _rope_tables(pos)
    col = lambda i: pl.BlockSpec((None, L, D_HALF), lambda b, c: (b, c, i))
    perb = lambda *s: pl.BlockSpec((None,) + s, lambda b, c: (b,) + (0,) * len(s))
    row2 = lambda a: a.reshape(1, D_HALF)
    ins = [p3] * 6 + [Wl['lru_conv_w'], row2(Wl['lru_conv_b']), Wl['lru_wa_d'], row2(Wl['lru_ba']),
                      Wl['lru_wx_d'], row2(Wl['lru_bx']), row2(Wl['lru_lambda']),
                      cosf, sinf, dmask, qd, kd, cd, row2(Wl['ret_gn_g']), row2(Wl['ret_gn_b']),
                      st['lru_h'].reshape(B, 1, D_HALF), st['lru_conv'], st['ret_S']]
    in_specs = [col(i) for i in range(6)] + [
        _full((CONV_W, D_HALF)), _full((1, D_HALF)), _full((4, LANES, LANES)), _full((1, D_HALF)),
        _full((4, LANES, LANES)), _full((1, D_HALF)), _full((1, D_HALF)),
        pl.BlockSpec((L, RET_DH), lambda b, c: (c, 0)), pl.BlockSpec((L, RET_DH), lambda b, c: (c, 0)),
        _full((RET_HEADS, L, L)), _full((L, D_HALF)), _full((L, D_HALF)), _full((RET_HEADS, 1, RET_DH)),
        _full((1, D_HALF)), _full((1, D_HALF)),
        perb(1, D_HALF), perb(CONV_W - 1, D_HALF), perb(RET_HEADS, RET_DH, RET_DH)]
    seq = pl.BlockSpec((None, L, D_HALF), lambda b, c: (b, c, 0))
    ya, yb, h, conv, S = pl.pallas_call(
        functools.partial(_even_prefill_kernel, L=L),
        grid=(B, nc),
        in_specs=in_specs,
        out_specs=[seq, seq, perb(1, D_HALF), perb(CONV_W - 1, D_HALF), perb(RET_HEADS, RET_DH, RET_DH)],
        out_shape=[jax.ShapeDtypeStruct((B, T, D_HALF), F32), jax.ShapeDtypeStruct((B, T, D_HALF), F32),
                   jax.ShapeDtypeStruct((B, 1, D_HALF), F32),
                   jax.ShapeDtypeStruct((B, CONV_W - 1, D_HALF), F32),
                   jax.ShapeDtypeStruct((B, RET_HEADS, RET_DH, RET_DH), F32)],
        scratch_shapes=[pltpu.VMEM((L + SUBLANES, D_HALF), F32)],
        compiler_params=_params(("parallel", "arbitrary")),
        name="even_prefill",
    )(*ins)
    return ya, yb, (h.reshape(B, D_HALF), conv, S)


DEC_BB = 8


def _even_decode_kernel(p_ref, cw_ref, cb_ref, wa_ref, ba_ref, wx_ref, bx_ref, lam_ref,
                        cos_ref, sin_ref, dm_ref, qd_ref, kd_ref, cd_ref, gng_ref, gnb_ref,
                        h0_ref, conv0_ref, S0_ref,
                        ya_ref, yb_ref, h_ref, conv_ref, S_ref):
    col = lambda i: p_ref[:, i * D_HALF:(i + 1) * D_HALF]
    xa = col(0)
    xc = cb_ref[...] + cw_ref[CONV_W - 1:CONV_W, :] * xa
    for i in range(CONV_W - 1):
        xc = xc + cw_ref[i:i + 1, :] * conv0_ref[i]
    for i in range(CONV_W - 2):
        conv_ref[i] = conv0_ref[i + 1]
    conv_ref[CONV_W - 2] = xa
    a, u = _lru_gates(xc, wa_ref, ba_ref, wx_ref, bx_ref, lam_ref)
    h = a * h0_ref[...] + u
    h_ref[...] = h
    ya_ref[...] = _gelu_tanh(col(1)) * h

    cosf = cos_ref[...]
    sinf = sin_ref[...]
    row8 = lax.broadcasted_iota(jnp.int32, (SUBLANES, RET_DH), 0)
    q, k, v, gb = col(2), col(3), col(4), col(5)
    for hh in range(RET_HEADS):
        sl = slice(hh * RET_DH, (hh + 1) * RET_DH)
        qh = _rotate(q[:, sl], cosf, sinf)
        kh = _rotate(k[:, sl], cosf, sinf) * (RET_DH ** -0.5)
        vh = v[:, sl]
        qk = jnp.sum(qh * kh, -1, keepdims=True) * dm_ref[:, sl]
        qq = qh * qd_ref[:, sl]
        kk = kh * kd_ref[:, sl]
        rows = []
        for bi in range(DEC_BB):
            S = S0_ref[bi, hh]
            q8 = jnp.broadcast_to(qq[bi:bi + 1, :], (SUBLANES, RET_DH))
            rows.append(_dot_f32(q8, S)[0:1, :])
            k8 = jnp.where(row8 == 0, jnp.broadcast_to(kk[bi:bi + 1, :], (SUBLANES, RET_DH)), 0.0)
            v8 = jnp.broadcast_to(vh[bi:bi + 1, :], (SUBLANES, RET_DH))
            S_ref[bi, hh] = S * cd_ref[hh] + _dot_tn_f32(k8, v8)
        o = qk * vh + jnp.concatenate(rows, axis=0)
        on = _head_norm(o, gng_ref[:, sl], gnb_ref[:, sl], GN_EPS)
        yb_ref[:, sl] = _silu(gb[:, sl]) * on


def _dot_f32(a, b):
    return jnp.dot(a, b, preferred_element_type=F32)


def _dot_tn_f32(a, b):
    return lax.dot_general(a, b, (((0,), (0,)), ((), ())), preferred_element_type=F32)


def _even_decode(p2, pos, st, Wl):
    B = p2.shape[0]
    bb = DEC_BB
    dmask, qd, kd, cd = _ret_tables(1)
    dm = jnp.repeat(dmask[:, 0, :].T, RET_DH, axis=1)
    cosf, sinf = _rope_tables(pos)
    row2 = lambda a: a.reshape(1, D_HALF)
    rows = lambda n: pl.BlockSpec((bb, n), lambda i: (i, 0))
    convs = pl.BlockSpec((CONV_W - 1, bb, D_HALF), lambda i: (0, i, 0))
    Ss = pl.BlockSpec((bb, RET_HEADS, RET_DH, RET_DH), lambda i: (i, 0, 0, 0))
    ins = [p2, Wl['lru_conv_w'], row2(Wl['lru_conv_b']), Wl['lru_wa_d'], row2(Wl['lru_ba']),
           Wl['lru_wx_d'], row2(Wl['lru_bx']), row2(Wl['lru_lambda']),
           cosf, sinf, dm, qd, kd, cd, row2(Wl['ret_gn_g']), row2(Wl['ret_gn_b']),
           st['lru_h'], jnp.swapaxes(st['lru_conv'], 0, 1), st['ret_S']]
    in_specs = [rows(EVEN_IN), _full((CONV_W, D_HALF)), _full((1, D_HALF)), _full((4, LANES, LANES)),
                _full((1, D_HALF)), _full((4, LANES, LANES)), _full((1, D_HALF)), _full((1, D_HALF)),
                _full((1, RET_DH)), _full((1, RET_DH)), _full((1, D_HALF)), _full((1, D_HALF)),
                _full((1, D_HALF)), _full((RET_HEADS, 1, RET_DH)), _full((1, D_HALF)), _full((1, D_HALF)),
                rows(D_HALF), convs, Ss]
    ya, yb, h, conv, S = pl.pallas_call(
        _even_decode_kernel,
        grid=(B // bb,),
        in_specs=in_specs,
        out_specs=[rows(D_HALF), rows(D_HALF), rows(D_HALF), convs, Ss],
        out_shape=[jax.ShapeDtypeStruct((B, D_HALF), F32), jax.ShapeDtypeStruct((B, D_HALF), F32),
                   jax.ShapeDtypeStruct((B, D_HALF), F32),
                   jax.ShapeDtypeStruct((CONV_W - 1, B, D_HALF), F32),
                   jax.ShapeDtypeStruct((B, RET_HEADS, RET_DH, RET_DH), F32)],
        compiler_params=_params(("parallel",)),
        name="even_decode",
    )(*ins)
    return ya, yb, (h, jnp.swapaxes(conv, 0, 1), S)


def _mlstm_qkv_gates(xm, xc, wq_ref, wk_ref, wv_ref, wg_ref, bg_ref):
    q = _dot(xc, wq_ref[...])
    k = _dot(xc, wk_ref[...])
    v = _dot(xm, wv_ref[...])
    g_col = (_dot(q, wg_ref[0:D_HALF, :]) + _dot(k, wg_ref[D_HALF:2 * D_HALF, :])
             + _dot(v, wg_ref[2 * D_HALF:3 * D_HALF, :]) + bg_ref[...])
    return q, k, v, g_col


def _mlstm_prefill_kernel(xm_ref, z_ref, cw_ref, cb_ref, wq_ref, wk_ref, wv_ref, wg_ref, bg_ref,
                          wgt_ref, bgt_ref, gng_ref, gnb_ref, skip_ref,
                          conv0_ref, C0_ref, n0_ref, m0_ref,
                          yc_ref, conv_ref, C_ref, n_ref, m_ref, xbuf, *, L):
    c = pl.program_id(1)

    @pl.when(c == 0)
    def _():
        C_ref[...] = C0_ref[...]
        n_ref[...] = n0_ref[...]
        m_ref[...] = m0_ref[...]
        xbuf[SUBLANES - (CONV_W - 1):SUBLANES, :] = conv0_ref[...]

    xm = xm_ref[...]
    xc, tail = _conv_prefill(xm, xbuf, cw_ref, cb_ref, L)
    conv_ref[...] = tail
    xc = _silu(xc)
    q, k, v, g_col = _mlstm_qkv_gates(xm, xc, wq_ref, wk_ref, wv_ref, wg_ref, bg_ref)
    g_row = (_dot_nt(wgt_ref[:, 0:D_HALF], q) + _dot_nt(wgt_ref[:, D_HALF:2 * D_HALF], k)
             + _dot_nt(wgt_ref[:, 2 * D_HALF:3 * D_HALF], v) + bgt_ref[...])
    ri = lax.broadcasted_iota(jnp.int32, (L, L), 0)
    ci = lax.broadcasted_iota(jnp.int32, (L, L), 1)
    causal = ri >= ci
    tril = jnp.where(causal, 1.0, 0.0).astype(BF16)
    triu = jnp.where(ci >= ri, 1.0, 0.0).astype(BF16)
    b_col = _xdot_l(tril, -_softplus(-g_col))
    b_row = _xdot(-_softplus(-g_row), triu)
    lane = lax.broadcasted_iota(jnp.int32, (1, LANES), 1)
    m_all = m_ref[...]
    for hh in range(ML_HEADS):
        sl = slice(hh * ML_DH, (hh + 1) * ML_DH)
        qh, vh = q[:, sl], v[:, sl]
        kh = k[:, sl] * (ML_DH ** -0.5)
        bc = b_col[:, ML_HEADS + hh:ML_HEADS + hh + 1]
        br = b_row[ML_HEADS + hh:ML_HEADS + hh + 1, :]
        lic = g_col[:, hh:hh + 1]
        lir = g_row[hh:hh + 1, :]
        m_prev = m_all[:, hh:hh + 1]
        C = C_ref[hh]
        n = n_ref[:, sl]
        logD = jnp.where(causal, bc - br + lir, -jnp.inf)
        log_inter = bc + m_prev
        m_t = jnp.maximum(jnp.max(logD, -1, keepdims=True), log_inter)
        s = _dot_nt(qh, kh) * jnp.exp(logD - m_t)
        w_inter = jnp.exp(log_inter - m_t)
        num = _dot(s, vh) + w_inter * _dot(qh, C)
        den = jnp.sum(s, -1, keepdims=True) + w_inter * jnp.sum(qh * n, -1, keepdims=True)
        hcell = num / jnp.maximum(jnp.abs(den), jnp.exp(-m_t))
        m_new = m_t[L - 1:L, :]
        b_last = bc[L - 1:L, :]
        w_k = jnp.exp(b_last - bc + lic - m_new)
        w_C = jnp.exp(b_last + m_prev - m_new)
        kw = kh * w_k
        C_ref[hh] = w_C * C + _dot_tn(kw, vh)
        n_ref[:, sl] = w_C * n + jnp.sum(kw, axis=0, keepdims=True)
        m_all = jnp.where(lane == hh, m_new, m_all)
        hn = _head_norm(hcell, gng_ref[:, sl], gnb_ref[:, sl], GN_EPS)
        yc_ref[:, sl] = (hn + skip_ref[:, sl] * xc[:, sl]) * _silu(z_ref[:, sl])
    m_ref[...] = m_all


def _ml_dense(w):
    eye = jnp.eye(ML_NBLK, dtype=w.dtype)
    d = w[:, :, None, :] * eye[:, None, :, None]
    return d.reshape(D_HALF, D_HALF).astype(BF16)


def _pad_lanes(a, n=LANES):
    return jnp.pad(a, [(0, 0)] * (a.ndim - 1) + [(0, n - a.shape[-1])])


def _mlstm_weights(Wl):
    wg = _pad_lanes(Wl['ml_w_gate']).astype(BF16)
    bg = _pad_lanes(Wl['ml_b_gate'].reshape(1, 2 * ML_HEADS))
    wgt = Wl['ml_w_gate'].T.astype(BF16)
    bgt = jnp.broadcast_to(Wl['ml_b_gate'].reshape(2 * ML_HEADS, 1), (2 * ML_HEADS, LANES))
    return wg, bg, wgt, bgt


def _mlstm_prefill(p3, st, Wl):
    B, T, _ = p3.shape
    L = CHUNK
    nc = T // L
    wg, bg, wgt, bgt = _mlstm_weights(Wl)
    col = lambda i: pl.BlockSpec((None, L, D_HALF), lambda b, c: (b, c, i))
    perb = lambda *s: pl.BlockSpec((None,) + s, lambda b, c: (b,) + (0,) * len(s))
    row2 = lambda a: a.reshape(1, D_HALF)
    ins = [p3, p3, Wl['ml_conv_w'], row2(Wl['ml_conv_b']), Wl['ml_wq_d'], Wl['ml_wk_d'], Wl['ml_wv_d'],
           wg, bg, wgt, bgt, row2(Wl['ml_gn_g']), row2(Wl['ml_gn_b']), row2(Wl['ml_skip']),
           st['ml_conv'], st['ml_C'], st['ml_n'].reshape(B, 1, D_HALF),
           _pad_lanes(st['ml_m']).reshape(B, 1, LANES)]
    in_specs = [col(0), col(1), _full((CONV_W, D_HALF)), _full((1, D_HALF)),
                _full((D_HALF, D_HALF)), _full((D_HALF, D_HALF)), _full((D_HALF, D_HALF)),
                _full((3 * D_HALF, LANES)), _full((1, LANES)), _full((2 * ML_HEADS, 3 * D_HALF)),
                _full((2 * ML_HEADS, LANES)), _full((1, D_HALF)), _full((1, D_HALF)), _full((1, D_HALF)),
                perb(CONV_W - 1, D_HALF), perb(ML_HEADS, ML_DH, ML_DH), perb(1, D_HALF), perb(1, LANES)]
    seq = pl.BlockSpec((None, L, D_HALF), lambda b, c: (b, c, 0))
    yc, conv, C, n, m = pl.pallas_call(
        functools.partial(_mlstm_prefill_kernel, L=L),
        grid=(B, nc),
        in_specs=in_specs,
        out_specs=[seq, perb(CONV_W - 1, D_HALF), perb(ML_HEADS, ML_DH, ML_DH), perb(1, D_HALF), perb(1, LANES)],
        out_shape=[jax.ShapeDtypeStruct((B, T, D_HALF), F32),
                   jax.ShapeDtypeStruct((B, CONV_W - 1, D_HALF), F32),
                   jax.ShapeDtypeStruct((B, ML_HEADS, ML_DH, ML_DH), F32),
                   jax.ShapeDtypeStruct((B, 1, D_HALF), F32),
                   jax.ShapeDtypeStruct((B, 1, LANES), F32)],
        scratch_shapes=[pltpu.VMEM((L + SUBLANES, D_HALF), F32)],
        compiler_params=_params(("parallel", "arbitrary")),
        name="mlstm_prefill",
    )(*ins)
    return yc, (C, n.reshape(B, ML_HEADS, ML_DH), m[:, 0, :ML_HEADS], conv)


def _mlstm_decode_kernel(p_ref, cw_ref, cb_ref, wq_ref, wk_ref, wv_ref, wg_ref, bg_ref,
                         gng_ref, gnb_ref, skip_ref, conv0_ref, C0_ref, n0_ref, m0_ref,
                         yc_ref, conv_ref, C_ref, n_ref, m_ref):
    xm = p_ref[:, 0:D_HALF]
    z = p_ref[:, D_HALF:2 * D_HALF]
    xc = cb_ref[...] + cw_ref[CONV_W - 1:CONV_W, :] * xm
    for i in range(CONV_W - 1):
        xc = xc + cw_ref[i:i + 1, :] * conv0_ref[i]
    for i in range(CONV_W - 2):
        conv_ref[i] = conv0_ref[i + 1]
    conv_ref[CONV_W - 2] = xm
    xc = _silu(xc)
    q, k, v, g = _mlstm_qkv_gates(xm, xc, wq_ref, wk_ref, wv_ref, wg_ref, bg_ref)
    lf_all = -_softplus(-g)
    lane = lax.broadcasted_iota(jnp.int32, (1, LANES), 1)
    row8 = lax.broadcasted_iota(jnp.int32, (SUBLANES, ML_DH), 0)
    m_all = m0_ref[...]
    m_out = m_all
    for hh in range(ML_HEADS):
        sl = slice(hh * ML_DH, (hh + 1) * ML_DH)
        qh, vh = q[:, sl], v[:, sl]
        kh = k[:, sl] * (ML_DH ** -0.5)
        li = g[:, hh:hh + 1]
        lf = lf_all[:, ML_HEADS + hh:ML_HEADS + hh + 1]
        m_prev = m_all[:, hh:hh + 1]
        n = n0_ref[:, sl]
        log_inter = lf + m_prev
        m_t = jnp.maximum(li, log_inter)
        s = jnp.sum(qh * kh, -1, keepdims=True) * jnp.exp(li - m_t)
        w_inter = jnp.exp(log_inter - m_t)
        w_k = jnp.exp(li - m_t)
        w_C = jnp.exp(log_inter - m_t)
        kw = kh * w_k
        rows = []
        for bi in range(DEC_BB):
            C = C0_ref[bi, hh]
            q8 = jnp.broadcast_to(qh[bi:bi + 1, :], (SUBLANES, ML_DH))
            rows.append(_dot_f32(q8, C)[0:1, :])
            k8 = jnp.where(row8 == 0, jnp.broadcast_to(kw[bi:bi + 1, :], (SUBLANES, ML_DH)), 0.0)
            v8 = jnp.broadcast_to(vh[bi:bi + 1, :], (SUBLANES, ML_DH))
            C_ref[bi, hh] = w_C[bi:bi + 1, :] * C + _dot_tn_f32(k8, v8)
        qC = jnp.concatenate(rows, axis=0)
        num = s * vh + w_inter * qC
        den = s + w_inter * jnp.sum(qh * n, -1, keepdims=True)
        hcell = num / jnp.maximum(jnp.abs(den), jnp.exp(-m_t))
        n_ref[:, sl] = w_C * n + kw
        m_out = jnp.where(lane == hh, m_t, m_out)
        hn = _head_norm(hcell, gng_ref[:, sl], gnb_ref[:, sl], GN_EPS)
        yc_ref[:, sl] = (hn + skip_ref[:, sl] * xc[:, sl]) * _silu(z[:, sl])
    m_ref[...] = m_out


def _mlstm_decode(p2, st, Wl):
    B = p2.shape[0]
    bb = DEC_BB
    wg, bg, _, _ = _mlstm_weights(Wl)
    row2 = lambda a: a.reshape(1, D_HALF)
    rows = lambda n: pl.BlockSpec((bb, n), lambda i: (i, 0))
    convs = pl.BlockSpec((CONV_W - 1, bb, D_HALF), lambda i: (0, i, 0))
    Cs = pl.BlockSpec((bb, ML_HEADS, ML_DH, ML_DH), lambda i: (i, 0, 0, 0))
    ins = [p2, Wl['ml_conv_w'], row2(Wl['ml_conv_b']), Wl['ml_wq_d'], Wl['ml_wk_d'], Wl['ml_wv_d'], wg, bg,
           row2(Wl['ml_gn_g']), row2(Wl['ml_gn_b']), row2(Wl['ml_skip']),
           jnp.swapaxes(st['ml_conv'], 0, 1), st['ml_C'], st['ml_n'].reshape(B, D_HALF), _pad_lanes(st['ml_m'])]
    in_specs = [pl.BlockSpec((bb, 2 * D_HALF), lambda i: (i, 0)), _full((CONV_W, D_HALF)), _full((1, D_HALF)),
                _full((D_HALF, D_HALF)), _full((D_HALF, D_HALF)), _full((D_HALF, D_HALF)),
                _full((3 * D_HALF, LANES)), _full((1, LANES)),
                _full((1, D_HALF)), _full((1, D_HALF)), _full((1, D_HALF)),
                convs, Cs, rows(D_HALF), rows(LANES)]
    yc, conv, C, n, m = pl.pallas_call(
        _mlstm_decode_kernel,
        grid=(B // bb,),
        in_specs=in_specs,
        out_specs=[rows(D_HALF), convs, Cs, rows(D_HALF), rows(LANES)],
        out_shape=[jax.ShapeDtypeStruct((B, D_HALF), F32),
                   jax.ShapeDtypeStruct((CONV_W - 1, B, D_HALF), F32),
                   jax.ShapeDtypeStruct((B, ML_HEADS, ML_DH, ML_DH), F32),
                   jax.ShapeDtypeStruct((B, D_HALF), F32),
                   jax.ShapeDtypeStruct((B, LANES), F32)],
        compiler_params=_params(("parallel",)),
        name="mlstm_decode",
    )(*ins)
    return yc, (C, n.reshape(B, ML_HEADS, ML_DH), m[:, :ML_HEADS], jnp.swapaxes(conv, 0, 1))


def _rwkv_pre_body(pr, pr_prev, mu_ref, w0_ref, a0_ref, w2_ref, a2_ref, g2_ref, kkw_ref, kaw_ref, rk_ref,
                   r_ref, d_ref, k_ref, v_ref, a_ref, b_ref, g_ref, bonus_ref):
    pm = pr + (pr_prev - pr) * mu_ref[...]
    r = pm[:, 0:D_HALF]
    kr = pm[:, D_HALF:2 * D_HALF]
    vr = pm[:, 2 * D_HALF:3 * D_HALF]
    lo = pm[:, 3 * D_HALF:RW_COLS_PAD]
    w_log = -_softplus(-(w0_ref[...] + _dot(jnp.tanh(lo), w2_ref[...]))) - 0.5
    a = _sigmoid(a0_ref[...] + _dot(lo, a2_ref[...]))
    g = _dot(_sigmoid(lo), g2_ref[...])
    ones_bd = _group_ones(LANES, RW_DH)
    kk = kr * kkw_ref[...]
    kk = kk / jnp.maximum(jnp.sqrt(_group_sum(kk * kk, ones_bd)), 1e-12)
    kh = kr * (1.0 + (a - 1.0) * kaw_ref[...])
    r_ref[...] = r
    d_ref[...] = jnp.exp(-jnp.exp(w_log))
    k_ref[...] = kh
    v_ref[...] = vr
    a_ref[...] = -kk
    b_ref[...] = kk * a
    g_ref[...] = g
    bonus_ref[...] = _group_sum(r * kh * rk_ref[...], ones_bd) * vr


def _rwkv_pre_prefill_kernel(pr_ref, shift0_ref, *rest, L):
    wrefs, outs, xbuf = rest[:9], rest[9:17], rest[17]
    c = pl.program_id(1)

    @pl.when(c == 0)
    def _():
        xbuf[SUBLANES - 1:SUBLANES, :] = shift0_ref[...]

    pr = pr_ref[...]
    xbuf[SUBLANES:SUBLANES + L, :] = pr
    pr_prev = xbuf[SUBLANES - 1:SUBLANES - 1 + L, :]
    xbuf[SUBLANES - 1:SUBLANES, :] = pr[L - 1:L, :]
    _rwkv_pre_body(pr, pr_prev, *wrefs, *outs)


def _rwkv_pre_decode_kernel(pr_ref, prev_ref, *rest):
    _rwkv_pre_body(pr_ref[...], prev_ref[...], *rest[:9], *rest[9:17])


def _rwkv_pre_weights(Wl):
    row2 = lambda a: a.reshape(1, D_HALF)
    padr = lambda w, o: jnp.pad(w, ((o, RW_LORA_PAD - o - w.shape[0]), (0, 0))).astype(BF16)
    mu = _pad_lanes(Wl['rw_mu'].reshape(1, RW_SHIFT_COLS), RW_COLS_PAD)
    ws = [mu, row2(Wl['rw_w0']), row2(Wl['rw_a0']),
          padr(Wl['rw_w2'], 0), padr(Wl['rw_a2'], RW_DECAY_LORA), padr(Wl['rw_g2'], RW_DECAY_LORA + RW_A_LORA),
          row2(Wl['rw_kk']), row2(Wl['rw_ka']), row2(Wl['rw_rk'])]
    specs = [_full((1, RW_COLS_PAD)), _full((1, D_HALF)), _full((1, D_HALF)),
             _full((RW_LORA_PAD, D_HALF)), _full((RW_LORA_PAD, D_HALF)), _full((RW_LORA_PAD, D_HALF)),
             _full((1, D_HALF)), _full((1, D_HALF)), _full((1, D_HALF))]
    return ws, specs


def _rwkv_pre_prefill(pr3, shift0, Wl):
    B, T, _ = pr3.shape
    L = CHUNK
    ws, wspecs = _rwkv_pre_weights(Wl)
    seq = pl.BlockSpec((None, L, D_HALF), lambda b, c: (b, c, 0))
    outs = pl.pallas_call(
        functools.partial(_rwkv_pre_prefill_kernel, L=L),
        grid=(B, T // L),
        in_specs=[pl.BlockSpec((None, L, RW_COLS_PAD), lambda b, c: (b, c, 0)),
                  pl.BlockSpec((None, 1, RW_COLS_PAD), lambda b, c: (b, 0, 0))] + wspecs,
        out_specs=[seq] * 8,
        out_shape=[jax.ShapeDtypeStruct((B, T, D_HALF), F32)] * 8,
        scratch_shapes=[pltpu.VMEM((L + SUBLANES, RW_COLS_PAD), F32)],
        compiler_params=_params(("parallel", "arbitrary")),
        name="rwkv_pre_prefill",
    )(pr3, _pad_lanes(shift0, RW_COLS_PAD).reshape(B, 1, RW_COLS_PAD), *ws)
    return outs


def _rwkv_pre_decode(pr, shift0, Wl):
    B = pr.shape[0]
    ws, wspecs = _rwkv_pre_weights(Wl)
    full2 = lambda n: pl.BlockSpec((B, n), lambda i: (0, 0))
    outs = pl.pallas_call(
        _rwkv_pre_decode_kernel,
        grid=(1,),
        in_specs=[full2(RW_COLS_PAD), full2(RW_COLS_PAD)] + wspecs,
        out_specs=[full2(D_HALF)] * 8,
        out_shape=[jax.ShapeDtypeStruct((B, D_HALF), F32)] * 8,
        compiler_params=_params(("arbitrary",)),
        name="rwkv_pre_decode",
    )(pr, _pad_lanes(shift0, RW_COLS_PAD), *ws)
    return outs


RW_IP = RW_DH // 2


def _rwkv_rec_kernel(r_ref, d_ref, k_ref, a_ref, b_ref, v_ref, S0_ref, y_ref, S_ref, *, Tc):
    @pl.when(pl.program_id(1) == 0)
    def _():
        S_ref[...] = S0_ref[...]

    def step(t, carry):
        a, d, b, k, r = a_ref[t], d_ref[t], b_ref[t], k_ref[t], r_ref[t]
        vt = v_ref[t]
        rows = []
        for ip in range(RW_IP):
            S = S_ref[ip]
            sa = jnp.sum(S * a, axis=0, keepdims=True)
            Sn = S * d + sa * b + vt[ip:ip + 1, :] * k
            S_ref[ip] = Sn
            rows.append(jnp.sum(Sn * r, axis=0, keepdims=True))
        y_ref[t] = jnp.concatenate(rows, axis=0)
        return carry

    lax.fori_loop(0, Tc, step, 0)


def _rwkv_rec_call(r, d, k, a, b, v, S0):
    nbb, T = r.shape[:2]
    Tc = min(T, 32)
    vec = pl.BlockSpec((None, Tc, RW_DH, LANES), lambda i, c: (i, c, 0, 0))
    vsp = pl.BlockSpec((None, Tc, RW_IP, LANES), lambda i, c: (i, c, 0, 0))
    ssp = pl.BlockSpec((None, RW_IP, RW_DH, LANES), lambda i, c: (i, 0, 0, 0))
    return pl.pallas_call(
        functools.partial(_rwkv_rec_kernel, Tc=Tc),
        grid=(nbb, T // Tc),
        in_specs=[vec] * 5 + [vsp, ssp],
        out_specs=[vsp, ssp],
        out_shape=[jax.ShapeDtypeStruct((nbb, T, RW_IP, LANES), F32),
                   jax.ShapeDtypeStruct((nbb, RW_IP, RW_DH, LANES), F32)],
        compiler_params=_params(("parallel", "arbitrary")),
        name="rwkv_recurrence",
    )(r, d, k, a, b, v, S0)


def _rwkv_dec_kernel(r_ref, d_ref, k_ref, a_ref, b_ref, v_ref, S0_ref, y_ref, S_ref):
    a, d, b, k, r = a_ref[...], d_ref[...], b_ref[...], k_ref[...], r_ref[...]
    v = v_ref[...]
    rows = []
    for i in range(RW_DH):
        S = S0_ref[i]
        sa = jnp.sum(S * a, axis=0, keepdims=True)
        Sn = S * d + sa * b + v[i:i + 1, :] * k
        S_ref[i] = Sn
        rows.append(jnp.sum(Sn * r, axis=0, keepdims=True))
    y_ref[...] = jnp.concatenate(rows, axis=0)


def _rwkv_decode_step(r, d, k, v, a, b, S0):
    B = r.shape[0]
    tr = lambda x: x.T.reshape(RW_HEADS, RW_DH, B)
    St = S0.reshape(B, RW_HEADS * RW_DH * RW_DH).T.reshape(RW_HEADS, RW_DH, RW_DH, B)
    vec = pl.BlockSpec((None, RW_DH, B), lambda h: (h, 0, 0))
    ssp = pl.BlockSpec((None, RW_DH, RW_DH, B), lambda h: (h, 0, 0, 0))
    y, S = pl.pallas_call(
        _rwkv_dec_kernel,
        grid=(RW_HEADS,),
        in_specs=[vec] * 6 + [ssp],
        out_specs=[vec, ssp],
        out_shape=[jax.ShapeDtypeStruct((RW_HEADS, RW_DH, B), F32),
                   jax.ShapeDtypeStruct((RW_HEADS, RW_DH, RW_DH, B), F32)],
        compiler_params=_params(("parallel",)),
        name="rwkv_decode_step",
    )(tr(r), tr(d), tr(k), tr(a), tr(b), tr(v), St)
    y = y.reshape(D_HALF, B).T
    S = S.reshape(RW_HEADS * RW_DH * RW_DH, B).T.reshape(B, RW_HEADS, RW_DH, RW_DH)
    return y, S


RW_TB = LANES // RW_BB


def _head_sum_rows(x):
    x3 = x.reshape(RW_HEADS, RW_DH, x.shape[-1])
    s = jnp.sum(x3, axis=1, keepdims=True)
    return jnp.broadcast_to(s, x3.shape).reshape(x.shape)


def _rwkv_pre_t_kernel(x_ref, shift0_ref, w_ref, mu_ref, w0_ref, a0_ref, w2_ref, a2_ref, g2_ref,
                       kkw_ref, kaw_ref, rk_ref,
                       r_ref, d_ref, k_ref, a_ref, b_ref, v_ref, g_ref, bonus_ref, last_ref, prev_scr):
    @pl.when(pl.program_id(0) == 0)
    def _():
        prev_scr[...] = shift0_ref[...]

    ro = lax.broadcasted_iota(jnp.int32, (LANES, LANES), 0)
    ci = lax.broadcasted_iota(jnp.int32, (LANES, LANES), 1)
    perm = jnp.where(ci == (ro % RW_BB) * RW_TB + ro // RW_BB, 1.0, 0.0).astype(BF16)
    xn = x_ref[...].reshape(RW_BB * RW_TB, D_MODEL).astype(BF16)
    xg = jnp.dot(perm, xn, preferred_element_type=F32).astype(BF16)
    pr = lax.dot_general(w_ref[...], xg, (((1,), (1,)), ((), ())), preferred_element_type=F32)
    lane = lax.broadcasted_iota(jnp.int32, (1, LANES), 1)
    prev = jnp.where(lane < RW_BB, pltpu.roll(prev_scr[...], RW_BB, 1), pltpu.roll(pr, RW_BB, 1))
    prev_scr[...] = pr
    last_ref[...] = pr
    pm = pr + (prev - pr) * mu_ref[...]
    r = pm[0:D_HALF]
    kr = pm[D_HALF:2 * D_HALF]
    vr = pm[2 * D_HALF:3 * D_HALF]
    lo = pm[3 * D_HALF:RW_COLS_PAD]
    w_log = -_softplus(-(w0_ref[...] + _dot(w2_ref[...], jnp.tanh(lo)))) - 0.5
    a = _sigmoid(a0_ref[...] + _dot(a2_ref[...], lo))
    g = _dot(g2_ref[...], _sigmoid(lo))
    kk = kr * kkw_ref[...]
    kk = kk / jnp.maximum(jnp.sqrt(_head_sum_rows(kk * kk)), 1e-12)
    kh = kr * (1.0 + (a - 1.0) * kaw_ref[...])
    g_ref[...] = g
    bonus_ref[...] = _head_sum_rows(r * kh * rk_ref[...]) * vr

    grp = lane // RW_BB
    ngrp = LANES // RW_BB

    def scatter(x, o_ref, nrow, npiece):
        rot = [x[q * nrow:(q + 1) * nrow, :] if q == 0 else pltpu.roll(x[q * nrow:(q + 1) * nrow, :], q * RW_BB, 1)
               for q in range(npiece)]
        for t in range(RW_TB):
            m = rot[0]
            for q in range(1, npiece):
                m = jnp.where(grp == (t + q) % ngrp, rot[q], m)
            if t:
                m = pltpu.roll(m, LANES - t * RW_BB, 1)
            if npiece < ngrp:
                m = jnp.where(lane < LANES // 2, m, pltpu.roll(m, LANES // 2, 1))
            o_ref[t] = m

    scatter(r, r_ref, RW_DH, RW_HEADS)
    scatter(jnp.exp(-jnp.exp(w_log)), d_ref, RW_DH, RW_HEADS)
    scatter(kh, k_ref, RW_DH, RW_HEADS)
    scatter(-kk, a_ref, RW_DH, RW_HEADS)
    scatter(kk * a, b_ref, RW_DH, RW_HEADS)
    vv = jnp.concatenate([vr[h * RW_DH + half * RW_IP:h * RW_DH + (half + 1) * RW_IP, :]
                          for half in range(2) for h in range(RW_HEADS)], axis=0)
    scatter(vv, v_ref, RW_IP, ngrp)


def _lane_bcast(a, n):
    return jnp.broadcast_to(a.reshape(n, 1), (n, LANES))


def _rwkv_pre_t(x3, shift0, w_rwt, j, Wl):
    B, T, _ = x3.shape
    nblk = T // RW_TB
    padr = lambda w, o: jnp.pad(w, ((o, RW_LORA_PAD - o - w.shape[0]), (0, 0))).astype(BF16).T
    sh = jnp.pad(shift0.T, ((0, RW_COLS_PAD - RW_SHIFT_COLS), (LANES - RW_BB, 0)))
    col = lambda a: _lane_bcast(a, D_HALF)
    ins = [x3, sh, w_rwt, _lane_bcast(_pad_lanes(Wl['rw_mu'].reshape(1, -1), RW_COLS_PAD), RW_COLS_PAD),
           col(Wl['rw_w0']), col(Wl['rw_a0']),
           padr(Wl['rw_w2'], 0), padr(Wl['rw_a2'], RW_DECAY_LORA), padr(Wl['rw_g2'], RW_DECAY_LORA + RW_A_LORA),
           col(Wl['rw_kk']), col(Wl['rw_ka']), col(Wl['rw_rk'])]
    in_specs = [pl.BlockSpec((B, RW_TB, D_MODEL), lambda c: (0, c, 0)), _full((RW_COLS_PAD, LANES)),
                pl.BlockSpec((None, RW_COLS_PAD, D_MODEL), lambda c: (j, 0, 0)), _full((RW_COLS_PAD, LANES)),
                _full((D_HALF, LANES)), _full((D_HALF, LANES)),
                _full((D_HALF, RW_LORA_PAD)), _full((D_HALF, RW_LORA_PAD)), _full((D_HALF, RW_LORA_PAD)),
                _full((D_HALF, LANES)), _full((D_HALF, LANES)), _full((D_HALF, LANES))]
    blk = pl.BlockSpec((None, D_HALF, LANES), lambda c: (c, 0, 0))
    ktile = pl.BlockSpec((RW_TB, RW_DH, LANES), lambda c: (c, 0, 0))
    vtile = pl.BlockSpec((RW_TB, RW_IP, LANES), lambda c: (c, 0, 0))
    outs = pl.pallas_call(
        _rwkv_pre_t_kernel,
        grid=(nblk,),
        in_specs=in_specs,
        out_specs=[ktile] * 5 + [vtile, blk, blk, _full((RW_COLS_PAD, LANES))],
        out_shape=[jax.ShapeDtypeStruct((T, RW_DH, LANES), F32)] * 5
                  + [jax.ShapeDtypeStruct((T, RW_IP, LANES), F32)]
                  + [jax.ShapeDtypeStruct((nblk, D_HALF, LANES), F32)] * 2
                  + [jax.ShapeDtypeStruct((RW_COLS_PAD, LANES), F32)],
        scratch_shapes=[pltpu.VMEM((RW_COLS_PAD, LANES), F32)],
        compiler_params=_params(("arbitrary",)),
        name="rwkv_pre_t",
    )(*ins)
    shift_new = outs[8][:RW_SHIFT_COLS, LANES - RW_BB:].T
    return outs[:8], shift_new


def _rwkv_post_t_kernel(y_ref, g_ref, bonus_ref, gng_ref, gnb_ref, yd_ref):
    lane = lax.broadcasted_iota(jnp.int32, (1, LANES), 1)
    grp = lane // RW_BB
    ngrp = LANES // RW_BB
    ys = [y_ref[t] for t in range(RW_TB)]
    rolled = []
    for s in range(ngrp):
        m = ys[s % RW_TB]
        for q in range(1, ngrp):
            m = jnp.where(grp == q, ys[(q + s) % RW_TB], m)
        rolled.append(pltpu.roll(m, s * RW_BB, 1) if s else m)
    pieces = {}
    for q in range(ngrp):
        m = rolled[(-q) % ngrp]
        for t in range(1, RW_TB):
            m = jnp.where(grp == t, rolled[(t - q) % ngrp], m)
        pieces[divmod(q, RW_HEADS)] = m
    y = jnp.concatenate([pieces[(half, h)] for h in range(RW_HEADS) for half in range(2)], axis=0)
    mu = _head_sum_rows(y) * (1.0 / RW_DH)
    yc = y - mu
    var = _head_sum_rows(yc * yc) * (1.0 / RW_DH)
    hn = yc * lax.rsqrt(var + RW_GN_EPS) * gng_ref[...] + gnb_ref[...]
    yd = ((hn + bonus_ref[...]) * g_ref[...]).T
    for t in range(RW_TB):
        yd_ref[:, t, :] = yd[t * RW_BB:(t + 1) * RW_BB, :]


def _rwkv_rec_t(vecs, S0, Wl, T):
    r, d, k, a, b, v, g, bonus = vecs
    nblk = T // RW_TB
    B = RW_BB
    Sr = (S0.reshape(B, RW_HEADS, 2, RW_IP, RW_DH).transpose(3, 4, 2, 1, 0).reshape(1, RW_IP, RW_DH, LANES))
    y, S = _rwkv_rec_call(*(t[None] for t in (r, d, k, a, b, v)), Sr)
    blk = pl.BlockSpec((None, D_HALF, LANES), lambda c: (c, 0, 0))
    col = lambda a_: _lane_bcast(a_, D_HALF)
    yd = pl.pallas_call(
        _rwkv_post_t_kernel,
        grid=(nblk,),
        in_specs=[pl.BlockSpec((RW_TB, RW_IP, LANES), lambda c: (c, 0, 0)), blk, blk,
                  _full((D_HALF, LANES)), _full((D_HALF, LANES))],
        out_specs=pl.BlockSpec((B, RW_TB, D_HALF), lambda c: (0, c, 0)),
        out_shape=jax.ShapeDtypeStruct((B, T, D_HALF), F32),
        compiler_params=_params(("parallel",)),
        name="rwkv_post_t",
    )(y[0], g, bonus, col(Wl['rw_gn_g']), col(Wl['rw_gn_b']))
    S = S[0].reshape(RW_IP, RW_DH, 2, RW_HEADS, B).transpose(4, 3, 2, 0, 1).reshape(B, RW_HEADS, RW_DH, RW_DH)
    return yd, S


def _rwkv_post_kernel(y_ref, g_ref, bonus_ref, gng_ref, gnb_ref, o_ref):
    ones_bd = _group_ones(LANES, RW_DH)
    y = y_ref[...]
    mu = _group_sum(y, ones_bd) * (1.0 / RW_DH)
    yc = y - mu
    var = _group_sum(yc * yc, ones_bd) * (1.0 / RW_DH)
    hn = yc * lax.rsqrt(var + RW_GN_EPS) * gng_ref[...] + gnb_ref[...]
    o_ref[...] = (hn + bonus_ref[...]) * g_ref[...]


def _rwkv_post(y2, g2, bonus2, Wl, tm):
    M = y2.shape[0]
    row = pl.BlockSpec((tm, D_HALF), lambda i: (i, 0))
    return pl.pallas_call(
        _rwkv_post_kernel,
        grid=(M // tm,),
        in_specs=[row, row, row, _full((1, D_HALF)), _full((1, D_HALF))],
        out_specs=row,
        out_shape=jax.ShapeDtypeStruct((M, D_HALF), F32),
        compiler_params=_params(("parallel",)),
        name="rwkv_post",
    )(y2, g2, bonus2, Wl['rw_gn_g'].reshape(1, D_HALF), Wl['rw_gn_b'].reshape(1, D_HALF))


def _trunk(x, st, pos, W):
    B, T, _ = x.shape
    M = B * T
    decode = T == 1
    tm_proj = min(M, 512)
    tm_post = min(M, 256)
    tm_rw = min(M, 512)
    x2 = x.reshape(M, D_MODEL)
    new = {name: [] for name in st}
    for l in range(DEPTH):
        j = l // 2
        Wl = {name: v[j] for name, v in W['per_pair'][l % 2].items()}
        stl = {name: v[j] for name, v in st.items()}
        if l % 2 == 0:
            p, = _proj(x2, W['ev_w_in'], j, tm_proj, (EVEN_IN,))
            if decode:
                ya, yb, (h, cb, S) = _even_decode(p, pos, stl, Wl)
            else:
                ya, yb, (h, cb, S) = _even_prefill(p.reshape(B, T, EVEN_IN), pos, stl, Wl)
            new['lru_h'].append(h)
            new['lru_conv'].append(cb)
            new['ret_S'].append(S)
            wout = W['ev_w_out']
        else:
            if decode:
                p_ml, p_rw = _proj(x2, W['od_w_in'], j, tm_proj, (2 * D_HALF, RW_COLS_PAD))
                ya, (C, n, m, cb) = _mlstm_decode(p_ml, stl, Wl)
                r, d, k, v, a, b, g, bonus = _rwkv_pre_decode(p_rw, stl['rw_shift'], Wl)
                y, S = _rwkv_decode_step(r, d, k, v, a, b, stl['rw_S'])
                yb = _rwkv_post(y, g, bonus, Wl, tm_rw)
                shift_new = p_rw[:, :RW_SHIFT_COLS]
            else:
                p_ml, = _proj(x2, W['od_w_in'], j, tm_proj, (2 * D_HALF,))
                ya, (C, n, m, cb) = _mlstm_prefill(p_ml.reshape(B, T, 2 * D_HALF), stl, Wl)
                vecs, shift_new = _rwkv_pre_t(x2.reshape(B, T, D_MODEL), stl['rw_shift'], W['od_w_rwt'], j, Wl)
                yb, S = _rwkv_rec_t(vecs, stl['rw_S'], Wl, T)
            new['ml_C'].append(C)
            new['ml_n'].append(n)
            new['ml_m'].append(m)
            new['ml_conv'].append(cb)
            new['rw_S'].append(S)
            new['rw_shift'].append(shift_new)
            wout = W['od_w_out']
        x2 = _post(ya.reshape(M, D_HALF), yb.reshape(M, D_HALF), x2, wout, j, l,
                   W['ln1_g'], W['ln1_b'], W['mlp_w1'], W['mlp_w2'], W['ln2_g'], W['ln2_b'], tm_post)
    return x2.reshape(B, T, D_MODEL), {name: jnp.stack(v) for name, v in new.items()}


def _prepare_weights(w):
    even_names = ('lru_conv_w', 'lru_conv_b', 'lru_ba', 'lru_bx', 'lru_lambda', 'ret_gn_g', 'ret_gn_b')
    odd_names = ('ml_conv_w', 'ml_conv_b', 'ml_w_gate', 'ml_b_gate', 'ml_gn_g', 'ml_gn_b', 'ml_skip',
                 'rw_mu', 'rw_w0', 'rw_w2', 'rw_a0', 'rw_a2', 'rw_g2', 'rw_kk', 'rw_ka', 'rw_rk', 'rw_gn_g', 'rw_gn_b')
    even = {n: w[n] for n in even_names}
    even.update(lru_wa_d=jax.vmap(_lru_dense)(w['lru_wa']), lru_wx_d=jax.vmap(_lru_dense)(w['lru_wx']))
    odd = {n: w[n] for n in odd_names}
    odd.update(ml_wq_d=jax.vmap(_ml_dense)(w['ml_wq']), ml_wk_d=jax.vmap(_ml_dense)(w['ml_wk']),
               ml_wv_d=jax.vmap(_ml_dense)(w['ml_wv']))
    od_w_in = _pad_lanes(w['od_w_in'], ODD_IN_PAD).astype(BF16)
    return dict(per_pair=(even, odd),
                ev_w_in=w['ev_w_in'].astype(BF16), ev_w_out=w['ev_w_out'].astype(BF16),
                od_w_in=od_w_in, od_w_rwt=jnp.swapaxes(od_w_in[:, :, 2 * D_HALF:], 1, 2),
                od_w_out=w['od_w_out'].astype(BF16),
                mlp_w1=w['mlp_w1'].astype(BF16), mlp_w2=w['mlp_w2'].astype(BF16),
                ln1_g=w['ln1_g'], ln1_b=w['ln1_b'], ln2_g=w['ln2_g'], ln2_b=w['ln2_b'])


def _zero_states(batch):
    z = lambda *s: jnp.zeros(s, F32)
    n_even, n_odd = (DEPTH + 1) // 2, DEPTH // 2
    return dict(lru_h=z(n_even, batch, D_HALF), lru_conv=z(n_even, batch, CONV_W - 1, D_HALF),
                ret_S=z(n_even, batch, RET_HEADS, RET_DH, RET_DH),
                ml_C=z(n_odd, batch, ML_HEADS, ML_DH, ML_DH), ml_n=z(n_odd, batch, ML_HEADS, ML_DH),
                ml_m=z(n_odd, batch, ML_HEADS), ml_conv=z(n_odd, batch, CONV_W - 1, D_HALF),
                rw_S=z(n_odd, batch, RW_HEADS, RW_DH, RW_DH), rw_shift=z(n_odd, batch, RW_SHIFT_COLS))


def kernel(x_prompt, x_sample, state_lru_h, state_lru_conv, state_ret, state_mlstm_C, state_mlstm_n, state_mlstm_m, state_mlstm_conv, state_rwkv_S, state_rwkv_shift, ln1_g, ln1_b, ln2_g, ln2_b, mlp_w1, mlp_w2, ev_w_in, ev_w_out, lru_conv_w, lru_conv_b, lru_wa, lru_ba, lru_wx, lru_bx, lru_lambda, ret_gn_g, ret_gn_b, od_w_in, od_w_out, ml_conv_w, ml_conv_b, ml_wq, ml_wk, ml_wv, ml_w_gate, ml_b_gate, ml_gn_g, ml_gn_b, ml_skip, rw_mu, rw_w0, rw_w2, rw_a0, rw_a2, rw_g2, rw_kk, rw_ka, rw_rk, rw_gn_g, rw_gn_b):
    W = _prepare_weights(dict(
        ln1_g=ln1_g, ln1_b=ln1_b, ln2_g=ln2_g, ln2_b=ln2_b, mlp_w1=mlp_w1, mlp_w2=mlp_w2,
        ev_w_in=ev_w_in, ev_w_out=ev_w_out, lru_conv_w=lru_conv_w, lru_conv_b=lru_conv_b,
        lru_wa=lru_wa, lru_ba=lru_ba, lru_wx=lru_wx, lru_bx=lru_bx, lru_lambda=lru_lambda,
        ret_gn_g=ret_gn_g, ret_gn_b=ret_gn_b, od_w_in=od_w_in, od_w_out=od_w_out,
        ml_conv_w=ml_conv_w, ml_conv_b=ml_conv_b, ml_wq=ml_wq, ml_wk=ml_wk, ml_wv=ml_wv,
        ml_w_gate=ml_w_gate, ml_b_gate=ml_b_gate, ml_gn_g=ml_gn_g, ml_gn_b=ml_gn_b, ml_skip=ml_skip,
        rw_mu=rw_mu, rw_w0=rw_w0, rw_w2=rw_w2, rw_a0=rw_a0, rw_a2=rw_a2, rw_g2=rw_g2,
        rw_kk=rw_kk, rw_ka=rw_ka, rw_rk=rw_rk, rw_gn_g=rw_gn_g, rw_gn_b=rw_gn_b))
    st_sample = dict(lru_h=state_lru_h, lru_conv=state_lru_conv, ret_S=state_ret,
                     ml_C=state_mlstm_C, ml_n=state_mlstm_n, ml_m=state_mlstm_m, ml_conv=state_mlstm_conv,
                     rw_S=state_rwkv_S, rw_shift=state_rwkv_shift)
    pos_prompt = jnp.arange(x_prompt.shape[1], dtype=jnp.int32)
    pos_sample = PAST_LEN + jnp.arange(x_sample.shape[1], dtype=jnp.int32)
    y_prompt, sp = _trunk(x_prompt, _zero_states(x_prompt.shape[0]), pos_prompt, W)
    y_sample, ss = _trunk(x_sample, st_sample, pos_sample, W)
    names = ('lru_h', 'lru_conv', 'ret_S', 'ml_C', 'ml_n', 'ml_m', 'ml_conv', 'rw_S', 'rw_shift')
    return (y_prompt, y_sample) + tuple(sp[n] for n in names) + tuple(ss[n] for n in names)
```

```python
import functools

import jax
import jax.numpy as jnp
from jax import lax
from jax.experimental import pallas as pl
from jax.experimental.pallas import tpu as pltpu

F32 = jnp.float32
BF16 = jnp.bfloat16

D_MODEL = 1024
DEPTH = 4
PAST_LEN = 16384
D_HALF = D_MODEL // 2
CONV_W = 4
LRU_BLOCKS = 8
LRU_BLOCK = D_HALF // LRU_BLOCKS
LRU_C = 8.0
RET_HEADS = 4
RET_DH = D_HALF // RET_HEADS
CHUNK = 128
ROPE_BASE = 10000.0
ML_HEADS = 4
ML_DH = D_HALF // ML_HEADS
ML_QKV_BLOCK = 4
ML_NBLK = D_HALF // ML_QKV_BLOCK
RW_HEADS = 8
RW_DH = D_HALF // RW_HEADS
RW_DECAY_LORA = 32
RW_A_LORA = 32
RW_GATE_LORA = 96
RW_LORA = RW_DECAY_LORA + RW_A_LORA + RW_GATE_LORA
RW_SHIFT_COLS = 3 * D_HALF + RW_LORA
D_FF = 4 * D_MODEL
ALPHA = (2.0 * DEPTH) ** 0.25
EVEN_IN = 6 * D_HALF
ODD_IN = 2 * D_HALF + RW_SHIFT_COLS
LN_EPS = 1e-5
GN_EPS = 1e-5
RW_GN_EPS = 64e-5

LANES = 128
SUBLANES = 8
RW_LORA_PAD = 2 * LANES
RW_COLS_PAD = 3 * D_HALF + RW_LORA_PAD
ODD_IN_PAD = 2 * D_HALF + RW_COLS_PAD
RW_BB = 8
VMEM_LIMIT = 56 * 1024 * 1024


def _params(sem):
    return pltpu.CompilerParams(dimension_semantics=sem, vmem_limit_bytes=VMEM_LIMIT)


def _dot(a, b):
    return jnp.dot(a.astype(BF16), b.astype(BF16), preferred_element_type=F32)


def _dot_nt(a, b):
    return lax.dot_general(a.astype(BF16), b.astype(BF16), (((1,), (1,)), ((), ())),
                           preferred_element_type=F32)


def _dot_tn(a, b):
    return lax.dot_general(a.astype(BF16), b.astype(BF16), (((0,), (0,)), ((), ())),
                           preferred_element_type=F32)


def _split3(a):
    hi = a.astype(BF16)
    r1 = a - hi.astype(F32)
    mid = r1.astype(BF16)
    lo = (r1 - mid.astype(F32)).astype(BF16)
    return hi, mid, lo


def _xdot(a, b01):
    hi, mid, lo = _split3(a)
    f = lambda t: jnp.dot(t, b01, preferred_element_type=F32)
    return f(hi) + f(mid) + f(lo)


def _xdot_l(b01, a):
    hi, mid, lo = _split3(a)
    f = lambda t: jnp.dot(b01, t, preferred_element_type=F32)
    return f(hi) + f(mid) + f(lo)


def _sigmoid(x):
    return 1.0 / (1.0 + jnp.exp(-x))


def _silu(x):
    return x * _sigmoid(x)


def _softplus(x):
    return jnp.maximum(x, 0.0) + jnp.log1p(jnp.exp(-jnp.abs(x)))


def _gelu_tanh(x):
    return 0.5 * x * (1.0 + jnp.tanh(0.7978845608028654 * (x + 0.044715 * (x * x * x))))


def _layer_norm(x, g, b, eps):
    mu = jnp.mean(x, -1, keepdims=True)
    xc = x - mu
    var = jnp.mean(xc * xc, -1, keepdims=True)
    return xc * lax.rsqrt(var + eps) * g + b


def _group_ones(n, group):
    r = lax.broadcasted_iota(jnp.int32, (n, n), 0) // group
    c = lax.broadcasted_iota(jnp.int32, (n, n), 1) // group
    return jnp.where(r == c, 1.0, 0.0).astype(BF16)


def _group_sum(x, ones_bd):
    parts = [_xdot(x[:, s * LANES:(s + 1) * LANES], ones_bd) for s in range(x.shape[1] // LANES)]
    return jnp.concatenate(parts, axis=-1)


def _rotate(x, cosf, sinf):
    return x * cosf + pltpu.roll(x, RET_DH // 2, 1) * sinf


def _proj_kernel(x_ref, w_ref, *o_refs, splits):
    xb = x_ref[...].astype(BF16)
    off = 0
    for o_ref, n in zip(o_refs, splits):
        o_ref[...] = jnp.dot(xb, w_ref[:, off:off + n], preferred_element_type=F32)
        off += n


def _proj(x2d, w_all, j, tm, splits):
    M = x2d.shape[0]
    N = w_all.shape[2]
    return pl.pallas_call(
        functools.partial(_proj_kernel, splits=splits),
        grid=(M // tm,),
        in_specs=[pl.BlockSpec((tm, D_MODEL), lambda i: (i, 0)),
                  pl.BlockSpec((None, D_MODEL, N), lambda i: (j, 0, 0))],
        out_specs=[pl.BlockSpec((tm, n), lambda i: (i, 0)) for n in splits],
        out_shape=[jax.ShapeDtypeStruct((M, n), F32) for n in splits],
        compiler_params=_params(("parallel",)),
        name="proj_in",
    )(x2d, w_all)


FF_CHUNK = 1024


def _post_kernel(ya_ref, yb_ref, x_ref, wo_ref, g1_ref, b1_ref, w1_ref, w2_ref, g2_ref, b2_ref, o_ref):
    y = (jnp.dot(ya_ref[...].astype(BF16), wo_ref[0:D_HALF, :], preferred_element_type=F32)
         + jnp.dot(yb_ref[...].astype(BF16), wo_ref[D_HALF:D_MODEL, :], preferred_element_type=F32))
    x1 = _layer_norm(ALPHA * x_ref[...] + y, g1_ref[...], b1_ref[...], LN_EPS)
    x1b = x1.astype(BF16)
    acc = jnp.zeros(x1.shape, F32)
    for c in range(D_FF // FF_CHUNK):
        h = jnp.dot(x1b, w1_ref[:, c * FF_CHUNK:(c + 1) * FF_CHUNK], preferred_element_type=F32)
        h = jnp.square(jnp.maximum(h, 0.0))
        acc = acc + jnp.dot(h.astype(BF16), w2_ref[c * FF_CHUNK:(c + 1) * FF_CHUNK, :],
                            preferred_element_type=F32)
    o_ref[...] = _layer_norm(ALPHA * x1 + acc, g2_ref[...], b2_ref[...], LN_EPS)


def _post(ya, yb, x2d, wout_all, j, l, ln1_g, ln1_b, w1_all, w2_all, ln2_g, ln2_b, tm):
    M = x2d.shape[0]
    row = lambda i: (i, 0)
    vec = pl.BlockSpec((None, 1, D_MODEL), lambda i: (l, 0, 0))
    r3 = lambda a: a.reshape(DEPTH, 1, D_MODEL)
    return pl.pallas_call(
        _post_kernel,
        grid=(M // tm,),
        in_specs=[pl.BlockSpec((tm, D_HALF), row), pl.BlockSpec((tm, D_HALF), row),
                  pl.BlockSpec((tm, D_MODEL), row),
                  pl.BlockSpec((None, D_MODEL, D_MODEL), lambda i: (j, 0, 0)),
                  vec, vec,
                  pl.BlockSpec((None, D_MODEL, D_FF), lambda i: (l, 0, 0)),
                  pl.BlockSpec((None, D_FF, D_MODEL), lambda i: (l, 0, 0)),
                  vec, vec],
        out_specs=pl.BlockSpec((tm, D_MODEL), row),
        out_shape=jax.ShapeDtypeStruct((M, D_MODEL), F32),
        compiler_params=_params(("parallel",)),
        name="post_mlp",
    )(ya, yb, x2d, wout_all, r3(ln1_g), r3(ln1_b), w1_all, w2_all, r3(ln2_g), r3(ln2_b))


def _conv_prefill(x, xbuf, cw_ref, cb_ref, L):
    xbuf[SUBLANES:SUBLANES + L, :] = x
    y = cb_ref[...]
    for i in range(CONV_W):
        y = y + cw_ref[i:i + 1, :] * xbuf[SUBLANES - (CONV_W - 1) + i:SUBLANES - (CONV_W - 1) + i + L, :]
    tail = xbuf[L + SUBLANES - (CONV_W - 1):L + SUBLANES, :]
    xbuf[SUBLANES - (CONV_W - 1):SUBLANES, :] = tail
    return y, tail


def _lru_gates(xc, wa_ref, ba_ref, wx_ref, bx_ref, lam_ref):
    xcb = xc.astype(BF16)
    nslab = D_HALF // LANES
    ra = jnp.concatenate([jnp.dot(xcb[:, s * LANES:(s + 1) * LANES], wa_ref[s], preferred_element_type=F32)
                          for s in range(nslab)], axis=-1)
    rx = jnp.concatenate([jnp.dot(xcb[:, s * LANES:(s + 1) * LANES], wx_ref[s], preferred_element_type=F32)
                          for s in range(nslab)], axis=-1)
    r = _sigmoid(ra + ba_ref[...])
    i = _sigmoid(rx + bx_ref[...])
    log_a = -LRU_C * r * _softplus(-lam_ref[...])
    a = jnp.exp(log_a)
    t = jnp.tanh(log_a)
    u = jnp.sqrt(-2.0 * t / (1.0 - t)) * (i * xc)
    return a, u


def _head_norm(o, g, b, eps):
    mu = jnp.mean(o, -1, keepdims=True)
    oc = o - mu
    var = jnp.mean(oc * oc, -1, keepdims=True)
    return oc * lax.rsqrt(var + eps) * g + b


EV_NB = 4


def _even_prefill_kernel(xa_ref, ga_ref, q_ref, k_ref, v_ref, gb_ref,
                         cw_ref, cb_ref, wa_ref, ba_ref, wx_ref, bx_ref, lam_ref,
                         cos_ref, sin_ref, dmask_ref, qd_ref, kd_ref, cd_ref, gng_ref, gnb_ref,
                         h0_ref, conv0_ref, S0_ref,
                         ya_ref, yb_ref, h_ref, conv_ref, S_ref, xbuf, *, L):
    @pl.when(pl.program_id(1) == 0)
    def _():
        h_ref[...] = h0_ref[...]
        S_ref[...] = S0_ref[...]
        xbuf[:, SUBLANES - (CONV_W - 1):SUBLANES, :] = conv0_ref[...]

    rows = [_even_prefill_one(xa_ref.at[bi], ga_ref.at[bi], q_ref.at[bi], k_ref.at[bi], v_ref.at[bi], gb_ref.at[bi],
                              cw_ref, cb_ref, wa_ref, ba_ref, wx_ref, bx_ref, lam_ref,
                              cos_ref, sin_ref, dmask_ref, qd_ref, kd_ref, cd_ref, gng_ref, gnb_ref,
                              ya_ref.at[bi], yb_ref.at[bi], h_ref.at[bi], conv_ref.at[bi], S_ref.at[bi],
                              xbuf.at[bi], L=L) for bi in range(EV_NB)]
    _round_robin(rows)


def _even_prefill_one(xa_ref, ga_ref, q_ref, k_ref, v_ref, gb_ref,
                      cw_ref, cb_ref, wa_ref, ba_ref, wx_ref, bx_ref, lam_ref,
                      cos_ref, sin_ref, dmask_ref, qd_ref, kd_ref, cd_ref, gng_ref, gnb_ref,
                      ya_ref, yb_ref, h_ref, conv_ref, S_ref, xbuf, *, L):
    H = range(RET_HEADS)
    sls = [slice(hh * RET_DH, (hh + 1) * RET_DH) for hh in H]
    cosf = cos_ref[...]
    sinf = sin_ref[...]
    qh = [_rotate(q_ref[:, sl], cosf, sinf) for sl in sls]
    kh = [_rotate(k_ref[:, sl], cosf, sinf) * (RET_DH ** -0.5) for sl in sls]
    vh = [v_ref[:, sl].astype(BF16) for sl in sls]
    qk = [_dot_nt(qh[hh], kh[hh]) for hh in H]
    S = [S_ref[hh] for hh in H]
    qS = [_dot(qh[hh] * qd_ref[:, sls[hh]], S[hh]) for hh in H]
    kv = [_dot_tn(kh[hh] * kd_ref[:, sls[hh]], vh[hh]) for hh in H]
    yield
    xc, tail = _conv_prefill(xa_ref[...], xbuf, cw_ref, cb_ref, L)
    conv_ref[...] = tail
    a, u = _lru_gates(xc, wa_ref, ba_ref, wx_ref, bx_ref, lam_ref)
    yield
    for hh in H:
        S_ref[hh] = S[hh] * cd_ref[hh] + kv[hh]
    sc = [(qk[hh] * dmask_ref[hh]).astype(BF16) for hh in H]
    o = [jnp.dot(sc[hh], vh[hh], preferred_element_type=F32) + qS[hh] for hh in H]
    yield
    row = lax.broadcasted_iota(jnp.int32, (L, D_HALF), 0)
    s = 1
    while s < L:
        keep = row >= s
        a_sh = jnp.where(keep, pltpu.roll(a, s, 0), 1.0)
        u_sh = jnp.where(keep, pltpu.roll(u, s, 0), 0.0)
        u = a * u_sh + u
        a = a * a_sh
        s *= 2
        yield
    h = a * h_ref[...] + u
    h_ref[...] = h[L - 1:L, :]
    ya_ref[...] = _gelu_tanh(ga_ref[...]) * h
    ones = jnp.ones((RET_DH, LANES), BF16)
    mu = [jnp.dot(o[hh].astype(BF16), ones, preferred_element_type=F32) * (1.0 / RET_DH) for hh in H]
    yield
    oc = [o[hh] - mu[hh] for hh in H]
    var = [jnp.dot((oc[hh] * oc[hh]).astype(BF16), ones, preferred_element_type=F32) * (1.0 / RET_DH) for hh in H]
    yield
    for hh in H:
        sl = sls[hh]
        on = oc[hh] * lax.rsqrt(var[hh] + GN_EPS) * gng_ref[:, sl] + gnb_ref[:, sl]
        yb_ref[:, sl] = _silu(gb_ref[:, sl]) * on


def _ret_tables(L):
    log_gamma = jnp.log1p(-jnp.exp2(-5.0 - jnp.arange(RET_HEADS, dtype=F32)))
    idx = jnp.arange(L, dtype=F32)
    diff = idx[:, None] - idx[None, :]
    dmask = jnp.where(diff >= 0, jnp.exp(log_gamma[:, None, None] * jnp.maximum(diff, 0.0)), 0.0)
    qd = jnp.exp(log_gamma[:, None] * (idx + 1.0))
    kd = jnp.exp(log_gamma[:, None] * (L - 1.0 - idx))
    cd = jnp.exp(log_gamma * L)
    qd_full = jnp.repeat(qd.T, RET_DH, axis=1)
    kd_full = jnp.repeat(kd.T, RET_DH, axis=1)
    cd_full = jnp.broadcast_to(cd[:, None, None], (RET_HEADS, 1, RET_DH))
    return dmask, qd_full, kd_full, cd_full


def _rope_tables(pos):
    half = RET_DH // 2
    inv = ROPE_BASE ** (-jnp.arange(half, dtype=F32) / half)
    ang = pos.astype(F32)[:, None] * inv[None, :]
    cos, sin = jnp.cos(ang), jnp.sin(ang)
    return jnp.concatenate([cos, cos], -1), jnp.concatenate([-sin, sin], -1)


def _lru_dense(w):
    pairs = LANES // LRU_BLOCK
    w4 = w.reshape(LRU_BLOCKS // pairs, pairs, LRU_BLOCK, LRU_BLOCK)
    eye = jnp.eye(pairs, dtype=w.dtype)
    d = w4[:, :, :, None, :] * eye[None, :, None, :, None]
    return d.reshape(LRU_BLOCKS // pairs, LANES, LANES).astype(BF16)


def _full(shape):
    n = len(shape)
    return pl.BlockSpec(shape, lambda *_: (0,) * n)


def _even_prefill(p3, pos, st, Wl):
    B, T, _ = p3.shape
    L = CHUNK
    nc = T // L
    dmask, qd, kd, cd = _ret_tables(L)
    cosf, sinf = _rope_tables(pos)
    nb = EV_NB
    col = lambda i: pl.BlockSpec((nb, L, D_HALF), lambda b, c: (b, c, i))
    perb = lambda *s: pl.BlockSpec((nb,) + s, lambda b, c: (b,) + (0,) * len(s))
    row2 = lambda a: a.reshape(1, D_HALF)
    ins = [p3] * 6 + [Wl['lru_conv_w'], row2(Wl['lru_conv_b']), Wl['lru_wa_d'], row2(Wl['lru_ba']),
                      Wl['lru_wx_d'], row2(Wl['lru_bx']), row2(Wl['lru_lambda']),
                      cosf, sinf, dmask, qd, kd, cd, row2(Wl['ret_gn_g']), row2(Wl['ret_gn_b']),
                      st['lru_h'].reshape(B, 1, D_HALF), st['lru_conv'], st['ret_S']]
    in_specs = [col(i) for i in range(6)] + [
        _full((CONV_W, D_HALF)), _full((1, D_HALF)), _full((4, LANES, LANES)), _full((1, D_HALF)),
        _full((4, LANES, LANES)), _full((1, D_HALF)), _full((1, D_HALF)),
        pl.BlockSpec((L, RET_DH), lambda b, c: (c, 0)), pl.BlockSpec((L, RET_DH), lambda b, c: (c, 0)),
        _full((RET_HEADS, L, L)), _full((L, D_HALF)), _full((L, D_HALF)), _full((RET_HEADS, 1, RET_DH)),
        _full((1, D_HALF)), _full((1, D_HALF)),
        perb(1, D_HALF), perb(CONV_W - 1, D_HALF), perb(RET_HEADS, RET_DH, RET_DH)]
    seq = pl.BlockSpec((nb, L, D_HALF), lambda b, c: (b, c, 0))
    ya, yb, h, conv, S = pl.pallas_call(
        functools.partial(_even_prefill_kernel, L=L),
        grid=(B // nb, nc),
        in_specs=in_specs,
        out_specs=[seq, seq, perb(1, D_HALF), perb(CONV_W - 1, D_HALF), perb(RET_HEADS, RET_DH, RET_DH)],
        out_shape=[jax.ShapeDtypeStruct((B, T, D_HALF), F32), jax.ShapeDtypeStruct((B, T, D_HALF), F32),
                   jax.ShapeDtypeStruct((B, 1, D_HALF), F32),
                   jax.ShapeDtypeStruct((B, CONV_W - 1, D_HALF), F32),
                   jax.ShapeDtypeStruct((B, RET_HEADS, RET_DH, RET_DH), F32)],
        scratch_shapes=[pltpu.VMEM((nb, L + SUBLANES, D_HALF), F32)],
        compiler_params=_params(("parallel", "arbitrary")),
        name="even_prefill",
    )(*ins)
    return ya, yb, (h.reshape(B, D_HALF), conv, S)


DEC_BB = 8


def _even_decode_kernel(p_ref, cw_ref, cb_ref, wa_ref, ba_ref, wx_ref, bx_ref, lam_ref,
                        cos_ref, sin_ref, dm_ref, qd_ref, kd_ref, cd_ref, gng_ref, gnb_ref,
                        h0_ref, conv0_ref, S0_ref,
                        ya_ref, yb_ref, h_ref, conv_ref, S_ref):
    col = lambda i: p_ref[:, i * D_HALF:(i + 1) * D_HALF]
    xa = col(0)
    xc = cb_ref[...] + cw_ref[CONV_W - 1:CONV_W, :] * xa
    for i in range(CONV_W - 1):
        xc = xc + cw_ref[i:i + 1, :] * conv0_ref[i]
    for i in range(CONV_W - 2):
        conv_ref[i] = conv0_ref[i + 1]
    conv_ref[CONV_W - 2] = xa
    a, u = _lru_gates(xc, wa_ref, ba_ref, wx_ref, bx_ref, lam_ref)
    h = a * h0_ref[...] + u
    h_ref[...] = h
    ya_ref[...] = _gelu_tanh(col(1)) * h

    cosf = cos_ref[...]
    sinf = sin_ref[...]
    row8 = lax.broadcasted_iota(jnp.int32, (SUBLANES, RET_DH), 0)
    q, k, v, gb = col(2), col(3), col(4), col(5)
    for hh in range(RET_HEADS):
        sl = slice(hh * RET_DH, (hh + 1) * RET_DH)
        qh = _rotate(q[:, sl], cosf, sinf)
        kh = _rotate(k[:, sl], cosf, sinf) * (RET_DH ** -0.5)
        vh = v[:, sl]
        qk = jnp.sum(qh * kh, -1, keepdims=True) * dm_ref[:, sl]
        qq = qh * qd_ref[:, sl]
        kk = kh * kd_ref[:, sl]
        rows = []
        for bi in range(DEC_BB):
            S = S0_ref[bi, hh]
            q8 = jnp.broadcast_to(qq[bi:bi + 1, :], (SUBLANES, RET_DH))
            rows.append(_dot_f32(q8, S)[0:1, :])
            k8 = jnp.where(row8 == 0, jnp.broadcast_to(kk[bi:bi + 1, :], (SUBLANES, RET_DH)), 0.0)
            v8 = jnp.broadcast_to(vh[bi:bi + 1, :], (SUBLANES, RET_DH))
            S_ref[bi, hh] = S * cd_ref[hh] + _dot_tn_f32(k8, v8)
        o = qk * vh + jnp.concatenate(rows, axis=0)
        on = _head_norm(o, gng_ref[:, sl], gnb_ref[:, sl], GN_EPS)
        yb_ref[:, sl] = _silu(gb[:, sl]) * on


def _dot_f32(a, b):
    return jnp.dot(a, b, preferred_element_type=F32)


def _dot_tn_f32(a, b):
    return lax.dot_general(a, b, (((0,), (0,)), ((), ())), preferred_element_type=F32)


def _skip_ref(kernel, pos):
    def wrapped(*refs):
        kernel(*refs[:pos], *refs[pos + 1:])
    return wrapped


def _stacked_state_io(S_all, S_prev, j, bb):
    tail = S_all.shape[2:]
    spec = pl.BlockSpec((None, bb) + tail, lambda i: (j, i) + (0,) * len(tail))
    extra_in = [] if S_prev is None else [S_prev]
    extra_specs = [] if S_prev is None else [pl.BlockSpec(memory_space=pl.ANY)]
    return spec, extra_in, extra_specs, jax.ShapeDtypeStruct(S_all.shape, F32)


def _even_decode(p2, pos, st, Wl, S_all, j, S_prev):
    B = p2.shape[0]
    bb = DEC_BB
    dmask, qd, kd, cd = _ret_tables(1)
    dm = jnp.repeat(dmask[:, 0, :].T, RET_DH, axis=1)
    cosf, sinf = _rope_tables(pos)
    row2 = lambda a: a.reshape(1, D_HALF)
    rows = lambda n: pl.BlockSpec((bb, n), lambda i: (i, 0))
    convs = pl.BlockSpec((CONV_W - 1, bb, D_HALF), lambda i: (0, i, 0))
    ins = [p2, Wl['lru_conv_w'], row2(Wl['lru_conv_b']), Wl['lru_wa_d'], row2(Wl['lru_ba']),
           Wl['lru_wx_d'], row2(Wl['lru_bx']), row2(Wl['lru_lambda']),
           cosf, sinf, dm, qd, kd, cd, row2(Wl['ret_gn_g']), row2(Wl['ret_gn_b']),
           st['lru_h'], jnp.swapaxes(st['lru_conv'], 0, 1), S_all]
    Ss, extra_in, extra_specs, S_shape = _stacked_state_io(S_all, S_prev, j, bb)
    kern = _even_decode_kernel if S_prev is None else _skip_ref(_even_decode_kernel, len(ins))
    aliases = {} if S_prev is None else {len(ins): 4}
    in_specs = [rows(EVEN_IN), _full((CONV_W, D_HALF)), _full((1, D_HALF)), _full((4, LANES, LANES)),
                _full((1, D_HALF)), _full((4, LANES, LANES)), _full((1, D_HALF)), _full((1, D_HALF)),
                _full((1, RET_DH)), _full((1, RET_DH)), _full((1, D_HALF)), _full((1, D_HALF)),
                _full((1, D_HALF)), _full((RET_HEADS, 1, RET_DH)), _full((1, D_HALF)), _full((1, D_HALF)),
                rows(D_HALF), convs, Ss] + extra_specs
    ya, yb, h, conv, S = pl.pallas_call(
        kern,
        grid=(B // bb,),
        in_specs=in_specs,
        out_specs=[rows(D_HALF), rows(D_HALF), rows(D_HALF), convs, Ss],
        out_shape=[jax.ShapeDtypeStruct((B, D_HALF), F32), jax.ShapeDtypeStruct((B, D_HALF), F32),
                   jax.ShapeDtypeStruct((B, D_HALF), F32),
                   jax.ShapeDtypeStruct((CONV_W - 1, B, D_HALF), F32), S_shape],
        input_output_aliases=aliases,
        compiler_params=_params(("parallel",)),
        name="even_decode",
    )(*ins, *extra_in)
    return ya, yb, (h, jnp.swapaxes(conv, 0, 1), S)


def _mlstm_qkv_gates(xm, xc, wq_ref, wk_ref, wv_ref, wg_ref, bg_ref):
    q = _dot(xc, wq_ref[...])
    k = _dot(xc, wk_ref[...])
    v = _dot(xm, wv_ref[...])
    g_col = (_dot(q, wg_ref[0:D_HALF, :]) + _dot(k, wg_ref[D_HALF:2 * D_HALF, :])
             + _dot(v, wg_ref[2 * D_HALF:3 * D_HALF, :]) + bg_ref[...])
    return q, k, v, g_col


ML_NB = 4


def _round_robin(gens):
    gens = list(gens)
    while gens:
        alive = []
        for g in gens:
            try:
                next(g)
                alive.append(g)
            except StopIteration:
                pass
        gens = alive


def _mlstm_prefill_kernel(xm_ref, z_ref, cw_ref, cb_ref, wq_ref, wk_ref, wv_ref, wg_ref, bg_ref,
                          wgt_ref, bgt_ref, gng_ref, gnb_ref, skip_ref,
                          conv0_ref, C0_ref, n0_ref, m0_ref,
                          yc_ref, conv_ref, C_ref, n_ref, m_ref, xbuf, ncol, *, L):
    @pl.when(pl.program_id(1) == 0)
    def _():
        ones = jnp.ones((ML_DH, LANES), BF16)
        eye = jnp.where(lax.broadcasted_iota(jnp.int32, (ML_DH, LANES), 0)
                        == lax.broadcasted_iota(jnp.int32, (ML_DH, LANES), 1), 1.0, 0.0)
        C_ref[...] = C0_ref[...]
        m_ref[...] = m0_ref[...]
        for bi in range(ML_NB):
            xbuf[bi, SUBLANES - (CONV_W - 1):SUBLANES, :] = conv0_ref[bi]
            for hh in range(ML_HEADS):
                ncol[bi, hh] = _xdot(eye * n0_ref[bi, :, hh * ML_DH:(hh + 1) * ML_DH], ones)

    rows = [_mlstm_prefill_one(xm_ref.at[bi], z_ref.at[bi], cw_ref, cb_ref, wq_ref, wk_ref, wv_ref, wg_ref, bg_ref,
                               wgt_ref, bgt_ref, gng_ref, gnb_ref, skip_ref,
                               yc_ref.at[bi], conv_ref.at[bi], C_ref.at[bi], n_ref.at[bi], m_ref.at[bi],
                               xbuf.at[bi], ncol.at[bi], L=L) for bi in range(ML_NB)]
    _round_robin(rows)


def _mlstm_prefill_one(xm_ref, z_ref, cw_ref, cb_ref, wq_ref, wk_ref, wv_ref, wg_ref, bg_ref,
                       wgt_ref, bgt_ref, gng_ref, gnb_ref, skip_ref,
                       yc_ref, conv_ref, C_ref, n_ref, m_ref, xbuf, ncol, *, L):
    xm = xm_ref[...]
    xc, tail = _conv_prefill(xm, xbuf, cw_ref, cb_ref, L)
    conv_ref[...] = tail
    xc = _silu(xc)
    yield
    q, k, v, g_col = _mlstm_qkv_gates(xm, xc, wq_ref, wk_ref, wv_ref, wg_ref, bg_ref)
    g_row = (_dot_nt(wgt_ref[:, 0:D_HALF], q) + _dot_nt(wgt_ref[:, D_HALF:2 * D_HALF], k)
             + _dot_nt(wgt_ref[:, 2 * D_HALF:3 * D_HALF], v) + bgt_ref[...])
    yield
    ri = lax.broadcasted_iota(jnp.int32, (L, L), 0)
    ci = lax.broadcasted_iota(jnp.int32, (L, L), 1)
    causal = ri >= ci
    tril = jnp.where(causal, 1.0, 0.0).astype(BF16)
    triu = jnp.where(ci >= ri, 1.0, 0.0).astype(BF16)
    ones = jnp.ones((L, LANES), BF16)
    eye = jnp.where(ri == ci, 1.0, 0.0)

    li_col = g_col
    lf_col = -_softplus(-pltpu.roll(g_col, LANES - ML_HEADS, 1))
    b_col = _xdot_l(tril, lf_col)
    lf_row = -_softplus(-g_row)
    b_row = _xdot(lf_row, triu)
    yield
    c_row = g_row[0:ML_HEADS, :] - b_row[ML_HEADS:2 * ML_HEADS, :]
    row = lax.broadcasted_iota(jnp.int32, (L, LANES), 0)
    pm = li_col - b_col
    sft = 1
    while sft < L:
        pm = jnp.maximum(pm, jnp.where(row >= sft, pltpu.roll(pm, sft, 0), -jnp.inf))
        sft *= 2
    m_prev = m_ref[...]
    u_col = -jnp.maximum(pm, m_prev)
    m_t_col = b_col - u_col
    e_col = jnp.exp(-m_t_col)
    m_new = m_t_col[L - 1:L, :]
    b_last = b_col[L - 1:L, :]
    wk_col = jnp.exp(b_last - b_col + li_col - m_new)
    wC_row = jnp.exp(b_last + m_prev - m_new)
    m_ref[...] = m_new
    yield
    vones = ones
    rep = lambda col, hh: jnp.broadcast_to(col[:, hh:hh + 1], (L, LANES))
    H = range(ML_HEADS)
    sls = [slice(hh * ML_DH, (hh + 1) * ML_DH) for hh in H]
    qh = [q[:, sl].astype(BF16) for sl in sls]
    kh = [k[:, sl] * (ML_DH ** -0.5) for sl in sls]
    vh1 = [jnp.concatenate([v[:, sl].astype(BF16), vones], axis=-1) for sl in sls]
    qk = [_dot_nt(qh[hh], kh[hh]) for hh in H]
    CN = [jnp.concatenate([C_ref[hh], ncol[hh]], axis=-1) for hh in H]
    qc = [jnp.dot(qh[hh], CN[hh].astype(BF16), preferred_element_type=F32) for hh in H]
    kw = [kh[hh] * rep(wk_col, hh) for hh in H]
    upd = [lax.dot_general(kw[hh].astype(BF16), vh1[hh], (((0,), (0,)), ((), ())), preferred_element_type=F32)
           for hh in H]
    yield
    for hh in H:
        w_C =jnp.broadcast_to(wC_row[:, hh:hh + 1], (ML_DH, 2 * ML_DH))
        CNn = w_C * CN[hh] + upd[hh]
        C_ref[hh] = CNn[:, 0:ML_DH]
        ncol[hh] = CNn[:, ML_DH:2 * ML_DH]
        n_ref[:, sls[hh]] = jnp.sum(CNn[:, ML_DH:2 * ML_DH] * eye, axis=0, keepdims=True)
    yield
    u = [rep(u_col, hh) for hh in H]
    s = [(qk[hh] * jnp.exp(jnp.where(causal, u[hh] + c_row[hh:hh + 1, :], -jnp.inf))).astype(BF16) for hh in H]
    sv = [jnp.dot(s[hh], vh1[hh], preferred_element_type=F32) for hh in H]
    yield
    hcell = []
    for hh in H:
        w_inter = jnp.exp(rep(m_prev, hh) + u[hh])
        num = sv[hh][:, 0:ML_DH] + w_inter * qc[hh][:, 0:ML_DH]
        den = sv[hh][:, ML_DH:2 * ML_DH] + w_inter * qc[hh][:, ML_DH:2 * ML_DH]
        hcell.append(num / jnp.maximum(jnp.abs(den), rep(e_col, hh)))
    mu = [jnp.dot(hcell[hh].astype(BF16), ones, preferred_element_type=F32) * (1.0 / ML_DH) for hh in H]
    yield
    oc = [hcell[hh] - mu[hh] for hh in H]
    var = [jnp.dot((oc[hh] * oc[hh]).astype(BF16), ones, preferred_element_type=F32) * (1.0 / ML_DH) for hh in H]
    for hh in H:
        sl = sls[hh]
        hn = oc[hh] * lax.rsqrt(var[hh] + GN_EPS) * gng_ref[:, sl] + gnb_ref[:, sl]
        yc_ref[:, sl] = (hn + skip_ref[:, sl] * xc[:, sl]) * _silu(z_ref[:, sl])


def _ml_dense(w):
    eye = jnp.eye(ML_NBLK, dtype=w.dtype)
    d = w[:, :, None, :] * eye[:, None, :, None]
    return d.reshape(D_HALF, D_HALF).astype(BF16)


def _pad_lanes(a, n=LANES):
    return jnp.pad(a, [(0, 0)] * (a.ndim - 1) + [(0, n - a.shape[-1])])


def _mlstm_weights(Wl):
    wg = _pad_lanes(Wl['ml_w_gate']).astype(BF16)
    bg = _pad_lanes(Wl['ml_b_gate'].reshape(1, 2 * ML_HEADS))
    wgt = Wl['ml_w_gate'].T.astype(BF16)
    bgt = jnp.broadcast_to(Wl['ml_b_gate'].reshape(2 * ML_HEADS, 1), (2 * ML_HEADS, LANES))
    return wg, bg, wgt, bgt


def _mlstm_prefill(p3, st, Wl):
    B, T, _ = p3.shape
    L = CHUNK
    nc = T // L
    wg, bg, wgt, bgt = _mlstm_weights(Wl)
    nb = ML_NB
    col = lambda i: pl.BlockSpec((nb, L, D_HALF), lambda b, c: (b, c, i))
    perb = lambda *s: pl.BlockSpec((nb,) + s, lambda b, c: (b,) + (0,) * len(s))
    row2 = lambda a: a.reshape(1, D_HALF)
    ins = [p3, p3, Wl['ml_conv_w'], row2(Wl['ml_conv_b']), Wl['ml_wq_d'], Wl['ml_wk_d'], Wl['ml_wv_d'],
           wg, bg, wgt, bgt, row2(Wl['ml_gn_g']), row2(Wl['ml_gn_b']), row2(Wl['ml_skip']),
           st['ml_conv'], st['ml_C'], st['ml_n'].reshape(B, 1, D_HALF),
           _pad_lanes(st['ml_m']).reshape(B, 1, LANES)]
    in_specs = [col(0), col(1), _full((CONV_W, D_HALF)), _full((1, D_HALF)),
                _full((D_HALF, D_HALF)), _full((D_HALF, D_HALF)), _full((D_HALF, D_HALF)),
                _full((3 * D_HALF, LANES)), _full((1, LANES)), _full((2 * ML_HEADS, 3 * D_HALF)),
                _full((2 * ML_HEADS, LANES)), _full((1, D_HALF)), _full((1, D_HALF)), _full((1, D_HALF)),
                perb(CONV_W - 1, D_HALF), perb(ML_HEADS, ML_DH, ML_DH), perb(1, D_HALF), perb(1, LANES)]
    seq = pl.BlockSpec((nb, L, D_HALF), lambda b, c: (b, c, 0))
    yc, conv, C, n, m = pl.pallas_call(
        functools.partial(_mlstm_prefill_kernel, L=L),
        grid=(B // nb, nc),
        in_specs=in_specs,
        out_specs=[seq, perb(CONV_W - 1, D_HALF), perb(ML_HEADS, ML_DH, ML_DH), perb(1, D_HALF), perb(1, LANES)],
        out_shape=[jax.ShapeDtypeStruct((B, T, D_HALF), F32),
                   jax.ShapeDtypeStruct((B, CONV_W - 1, D_HALF), F32),
                   jax.ShapeDtypeStruct((B, ML_HEADS, ML_DH, ML_DH), F32),
                   jax.ShapeDtypeStruct((B, 1, D_HALF), F32),
                   jax.ShapeDtypeStruct((B, 1, LANES), F32)],
        scratch_shapes=[pltpu.VMEM((nb, L + SUBLANES, D_HALF), F32),
                        pltpu.VMEM((nb, ML_HEADS, ML_DH, LANES), F32)],
        compiler_params=_params(("parallel", "arbitrary")),
        name="mlstm_prefill",
    )(*ins)
    return yc, (C, n.reshape(B, ML_HEADS, ML_DH), m[:, 0, :ML_HEADS], conv)


def _mlstm_decode_kernel(p_ref, cw_ref, cb_ref, wq_ref, wk_ref, wv_ref, wg_ref, bg_ref,
                         gng_ref, gnb_ref, skip_ref, conv0_ref, C0_ref, n0_ref, m0_ref,
                         yc_ref, conv_ref, C_ref, n_ref, m_ref):
    xm = p_ref[:, 0:D_HALF]
    z = p_ref[:, D_HALF:2 * D_HALF]
    xc = cb_ref[...] + cw_ref[CONV_W - 1:CONV_W, :] * xm
    for i in range(CONV_W - 1):
        xc = xc + cw_ref[i:i + 1, :] * conv0_ref[i]
    for i in range(CONV_W - 2):
        conv_ref[i] = conv0_ref[i + 1]
    conv_ref[CONV_W - 2] = xm
    xc = _silu(xc)
    q, k, v, g = _mlstm_qkv_gates(xm, xc, wq_ref, wk_ref, wv_ref, wg_ref, bg_ref)
    lf_all = -_softplus(-g)
    lane = lax.broadcasted_iota(jnp.int32, (1, LANES), 1)
    row8 = lax.broadcasted_iota(jnp.int32, (SUBLANES, ML_DH), 0)
    m_all = m0_ref[...]
    m_out = m_all
    for hh in range(ML_HEADS):
        sl = slice(hh * ML_DH, (hh + 1) * ML_DH)
        qh, vh = q[:, sl], v[:, sl]
        kh = k[:, sl] * (ML_DH ** -0.5)
        li = g[:, hh:hh + 1]
        lf = lf_all[:, ML_HEADS + hh:ML_HEADS + hh + 1]
        m_prev = m_all[:, hh:hh + 1]
        n = n0_ref[:, sl]
        log_inter = lf + m_prev
        m_t = jnp.maximum(li, log_inter)
        s = jnp.sum(qh * kh, -1, keepdims=True) * jnp.exp(li - m_t)
        w_inter = jnp.exp(log_inter - m_t)
        w_k = jnp.exp(li - m_t)
        w_C = jnp.exp(log_inter - m_t)
        kw = kh * w_k
        rows = []
        for bi in range(DEC_BB):
            C = C0_ref[bi, hh]
            q8 = jnp.broadcast_to(qh[bi:bi + 1, :], (SUBLANES, ML_DH))
            rows.append(_dot_f32(q8, C)[0:1, :])
            k8 = jnp.where(row8 == 0, jnp.broadcast_to(kw[bi:bi + 1, :], (SUBLANES, ML_DH)), 0.0)
            v8 = jnp.broadcast_to(vh[bi:bi + 1, :], (SUBLANES, ML_DH))
            C_ref[bi, hh] = w_C[bi:bi + 1, :] * C + _dot_tn_f32(k8, v8)
        qC = jnp.concatenate(rows, axis=0)
        num = s * vh + w_inter * qC
        den = s + w_inter * jnp.sum(qh * n, -1, keepdims=True)
        hcell = num / jnp.maximum(jnp.abs(den), jnp.exp(-m_t))
        n_ref[:, sl] = w_C * n + kw
        m_out = jnp.where(lane == hh, m_t, m_out)
        hn = _head_norm(hcell, gng_ref[:, sl], gnb_ref[:, sl], GN_EPS)
        yc_ref[:, sl] = (hn + skip_ref[:, sl] * xc[:, sl]) * _silu(z[:, sl])
    m_ref[...] = m_out


def _mlstm_decode(p2, st, Wl, C_all, j, C_prev):
    B = p2.shape[0]
    bb = DEC_BB
    wg, bg, _, _ = _mlstm_weights(Wl)
    row2 = lambda a: a.reshape(1, D_HALF)
    rows = lambda n: pl.BlockSpec((bb, n), lambda i: (i, 0))
    convs = pl.BlockSpec((CONV_W - 1, bb, D_HALF), lambda i: (0, i, 0))
    ins = [p2, Wl['ml_conv_w'], row2(Wl['ml_conv_b']), Wl['ml_wq_d'], Wl['ml_wk_d'], Wl['ml_wv_d'], wg, bg,
           row2(Wl['ml_gn_g']), row2(Wl['ml_gn_b']), row2(Wl['ml_skip']),
           jnp.swapaxes(st['ml_conv'], 0, 1), C_all, st['ml_n'].reshape(B, D_HALF), _pad_lanes(st['ml_m'])]
    Cs, extra_in, extra_specs, C_shape = _stacked_state_io(C_all, C_prev, j, bb)
    kern = _mlstm_decode_kernel if C_prev is None else _skip_ref(_mlstm_decode_kernel, len(ins))
    aliases = {} if C_prev is None else {len(ins): 2}
    in_specs = [pl.BlockSpec((bb, 2 * D_HALF), lambda i: (i, 0)), _full((CONV_W, D_HALF)), _full((1, D_HALF)),
                _full((D_HALF, D_HALF)), _full((D_HALF, D_HALF)), _full((D_HALF, D_HALF)),
                _full((3 * D_HALF, LANES)), _full((1, LANES)),
                _full((1, D_HALF)), _full((1, D_HALF)), _full((1, D_HALF)),
                convs, Cs, rows(D_HALF), rows(LANES)] + extra_specs
    yc, conv, C, n, m = pl.pallas_call(
        kern,
        grid=(B // bb,),
        in_specs=in_specs,
        out_specs=[rows(D_HALF), convs, Cs, rows(D_HALF), rows(LANES)],
        out_shape=[jax.ShapeDtypeStruct((B, D_HALF), F32),
                   jax.ShapeDtypeStruct((CONV_W - 1, B, D_HALF), F32),
                   C_shape,
                   jax.ShapeDtypeStruct((B, D_HALF), F32),
                   jax.ShapeDtypeStruct((B, LANES), F32)],
        input_output_aliases=aliases,
        compiler_params=_params(("parallel",)),
        name="mlstm_decode",
    )(*ins, *extra_in)
    return yc, (C, n.reshape(B, ML_HEADS, ML_DH), m[:, :ML_HEADS], jnp.swapaxes(conv, 0, 1))


def _rwkv_pre_body(pr, pr_prev, mu_ref, w0_ref, a0_ref, w2_ref, a2_ref, g2_ref, kkw_ref, kaw_ref, rk_ref,
                   r_ref, d_ref, k_ref, v_ref, a_ref, b_ref, g_ref, bonus_ref):
    pm = pr + (pr_prev - pr) * mu_ref[...]
    r = pm[:, 0:D_HALF]
    kr = pm[:, D_HALF:2 * D_HALF]
    vr = pm[:, 2 * D_HALF:3 * D_HALF]
    lo = pm[:, 3 * D_HALF:RW_COLS_PAD]
    w_log = -_softplus(-(w0_ref[...] + _dot(jnp.tanh(lo), w2_ref[...]))) - 0.5
    a = _sigmoid(a0_ref[...] + _dot(lo, a2_ref[...]))
    g = _dot(_sigmoid(lo), g2_ref[...])
    ones_bd = _group_ones(LANES, RW_DH)
    kk = kr * kkw_ref[...]
    kk = kk / jnp.maximum(jnp.sqrt(_group_sum(kk * kk, ones_bd)), 1e-12)
    kh = kr * (1.0 + (a - 1.0) * kaw_ref[...])
    r_ref[...] = r
    d_ref[...] = jnp.exp(-jnp.exp(w_log))
    k_ref[...] = kh
    v_ref[...] = vr
    a_ref[...] = -kk
    b_ref[...] = kk * a
    g_ref[...] = g
    bonus_ref[...] = _group_sum(r * kh * rk_ref[...], ones_bd) * vr


def _rwkv_pre_prefill_kernel(pr_ref, shift0_ref, *rest, L):
    wrefs, outs, xbuf = rest[:9], rest[9:17], rest[17]
    c = pl.program_id(1)

    @pl.when(c == 0)
    def _():
        xbuf[SUBLANES - 1:SUBLANES, :] = shift0_ref[...]

    pr = pr_ref[...]
    xbuf[SUBLANES:SUBLANES + L, :] = pr
    pr_prev = xbuf[SUBLANES - 1:SUBLANES - 1 + L, :]
    xbuf[SUBLANES - 1:SUBLANES, :] = pr[L - 1:L, :]
    _rwkv_pre_body(pr, pr_prev, *wrefs, *outs)


def _rwkv_pre_decode_kernel(pr_ref, prev_ref, *rest):
    _rwkv_pre_body(pr_ref[...], prev_ref[...], *rest[:9], *rest[9:17])


def _rwkv_pre_weights(Wl):
    row2 = lambda a: a.reshape(1, D_HALF)
    padr = lambda w, o: jnp.pad(w, ((o, RW_LORA_PAD - o - w.shape[0]), (0, 0))).astype(BF16)
    mu = _pad_lanes(Wl['rw_mu'].reshape(1, RW_SHIFT_COLS), RW_COLS_PAD)
    ws = [mu, row2(Wl['rw_w0']), row2(Wl['rw_a0']),
          padr(Wl['rw_w2'], 0), padr(Wl['rw_a2'], RW_DECAY_LORA), padr(Wl['rw_g2'], RW_DECAY_LORA + RW_A_LORA),
          row2(Wl['rw_kk']), row2(Wl['rw_ka']), row2(Wl['rw_rk'])]
    specs = [_full((1, RW_COLS_PAD)), _full((1, D_HALF)), _full((1, D_HALF)),
             _full((RW_LORA_PAD, D_HALF)), _full((RW_LORA_PAD, D_HALF)), _full((RW_LORA_PAD, D_HALF)),
             _full((1, D_HALF)), _full((1, D_HALF)), _full((1, D_HALF))]
    return ws, specs


def _rwkv_pre_prefill(pr3, shift0, Wl):
    B, T, _ = pr3.shape
    L = CHUNK
    ws, wspecs = _rwkv_pre_weights(Wl)
    seq = pl.BlockSpec((None, L, D_HALF), lambda b, c: (b, c, 0))
    outs = pl.pallas_call(
        functools.partial(_rwkv_pre_prefill_kernel, L=L),
        grid=(B, T // L),
        in_specs=[pl.BlockSpec((None, L, RW_COLS_PAD), lambda b, c: (b, c, 0)),
                  pl.BlockSpec((None, 1, RW_COLS_PAD), lambda b, c: (b, 0, 0))] + wspecs,
        out_specs=[seq] * 8,
        out_shape=[jax.ShapeDtypeStruct((B, T, D_HALF), F32)] * 8,
        scratch_shapes=[pltpu.VMEM((L + SUBLANES, RW_COLS_PAD), F32)],
        compiler_params=_params(("parallel", "arbitrary")),
        name="rwkv_pre_prefill",
    )(pr3, _pad_lanes(shift0, RW_COLS_PAD).reshape(B, 1, RW_COLS_PAD), *ws)
    return outs


def _rwkv_pre_decode(pr, shift0, Wl):
    B = pr.shape[0]
    ws, wspecs = _rwkv_pre_weights(Wl)
    full2 = lambda n: pl.BlockSpec((B, n), lambda i: (0, 0))
    outs = pl.pallas_call(
        _rwkv_pre_decode_kernel,
        grid=(1,),
        in_specs=[full2(RW_COLS_PAD), full2(RW_COLS_PAD)] + wspecs,
        out_specs=[full2(D_HALF)] * 8,
        out_shape=[jax.ShapeDtypeStruct((B, D_HALF), F32)] * 8,
        compiler_params=_params(("arbitrary",)),
        name="rwkv_pre_decode",
    )(pr, _pad_lanes(shift0, RW_COLS_PAD), *ws)
    return outs


RW_IP = RW_DH // 2


def _rwkv_rec_kernel(r_ref, d_ref, k_ref, a_ref, b_ref, v_ref, S0_ref, y_ref, S_ref, *, Tc):
    @pl.when(pl.program_id(1) == 0)
    def _():
        S_ref[...] = S0_ref[...]

    def step(t, carry):
        a, d, b, k, r = a_ref[t], d_ref[t], b_ref[t], k_ref[t], r_ref[t]
        vt = v_ref[t]
        rows = []
        for ip in range(RW_IP):
            S = S_ref[ip]
            sa = jnp.sum(S * a, axis=0, keepdims=True)
            Sn = S * d + sa * b + vt[ip:ip + 1, :] * k
            S_ref[ip] = Sn
            rows.append(jnp.sum(Sn * r, axis=0, keepdims=True))
        y_ref[t] = jnp.concatenate(rows, axis=0)
        return carry

    lax.fori_loop(0, Tc, step, 0)


def _rwkv_rec_call(r, d, k, a, b, v, S0):
    nbb, T = r.shape[:2]
    Tc = min(T, 32)
    vec = pl.BlockSpec((None, Tc, RW_DH, LANES), lambda i, c: (i, c, 0, 0))
    vsp = pl.BlockSpec((None, Tc, RW_IP, LANES), lambda i, c: (i, c, 0, 0))
    ssp = pl.BlockSpec((None, RW_IP, RW_DH, LANES), lambda i, c: (i, 0, 0, 0))
    return pl.pallas_call(
        functools.partial(_rwkv_rec_kernel, Tc=Tc),
        grid=(nbb, T // Tc),
        in_specs=[vec] * 5 + [vsp, ssp],
        out_specs=[vsp, ssp],
        out_shape=[jax.ShapeDtypeStruct((nbb, T, RW_IP, LANES), F32),
                   jax.ShapeDtypeStruct((nbb, RW_IP, RW_DH, LANES), F32)],
        compiler_params=_params(("parallel", "arbitrary")),
        name="rwkv_recurrence",
    )(r, d, k, a, b, v, S0)


def _rwkv_dec_kernel(r_ref, d_ref, k_ref, a_ref, b_ref, v_ref, S0_ref, y_ref, S_ref):
    a, d, b, k, r = a_ref[...], d_ref[...], b_ref[...], k_ref[...], r_ref[...]
    v = v_ref[...]
    rows = []
    for i in range(RW_DH):
        S = S0_ref[i]
        sa = jnp.sum(S * a, axis=0, keepdims=True)
        Sn = S * d + sa * b + v[i:i + 1, :] * k
        S_ref[i] = Sn
        rows.append(jnp.sum(Sn * r, axis=0, keepdims=True))
    y_ref[...] = jnp.concatenate(rows, axis=0)


def _rwkv_decode_step(r, d, k, v, a, b, S0):
    B = r.shape[0]
    tr = lambda x: x.T.reshape(RW_HEADS, RW_DH, B)
    St = S0.reshape(B, RW_HEADS * RW_DH * RW_DH).T.reshape(RW_HEADS, RW_DH, RW_DH, B)
    vec = pl.BlockSpec((None, RW_DH, B), lambda h: (h, 0, 0))
    ssp = pl.BlockSpec((None, RW_DH, RW_DH, B), lambda h: (h, 0, 0, 0))
    y, S = pl.pallas_call(
        _rwkv_dec_kernel,
        grid=(RW_HEADS,),
        in_specs=[vec] * 6 + [ssp],
        out_specs=[vec, ssp],
        out_shape=[jax.ShapeDtypeStruct((RW_HEADS, RW_DH, B), F32),
                   jax.ShapeDtypeStruct((RW_HEADS, RW_DH, RW_DH, B), F32)],
        compiler_params=_params(("parallel",)),
        name="rwkv_decode_step",
    )(tr(r), tr(d), tr(k), tr(a), tr(b), tr(v), St)
    y = y.reshape(D_HALF, B).T
    S = S.reshape(RW_HEADS * RW_DH * RW_DH, B).T.reshape(B, RW_HEADS, RW_DH, RW_DH)
    return y, S


RW_TB = LANES // RW_BB


def _head_sum_rows(x):
    x3 = x.reshape(RW_HEADS, RW_DH, x.shape[-1])
    s = jnp.sum(x3, axis=1, keepdims=True)
    return jnp.broadcast_to(s, x3.shape).reshape(x.shape)


def _rwkv_pre_t_kernel(x_ref, shift0_ref, w_ref, mu_ref, w0_ref, a0_ref, w2_ref, a2_ref, g2_ref,
                       kkw_ref, kaw_ref, rk_ref,
                       r_ref, d_ref, k_ref, a_ref, b_ref, v_ref, g_ref, bonus_ref, last_ref, prev_scr):
    @pl.when(pl.program_id(0) == 0)
    def _():
        prev_scr[...] = shift0_ref[...]

    ro = lax.broadcasted_iota(jnp.int32, (LANES, LANES), 0)
    ci = lax.broadcasted_iota(jnp.int32, (LANES, LANES), 1)
    perm = jnp.where(ci == (ro % RW_BB) * RW_TB + ro // RW_BB, 1.0, 0.0).astype(BF16)
    xn = x_ref[...].reshape(RW_BB * RW_TB, D_MODEL).astype(BF16)
    xg = jnp.dot(perm, xn, preferred_element_type=F32).astype(BF16)
    pr = lax.dot_general(w_ref[...], xg, (((1,), (1,)), ((), ())), preferred_element_type=F32)
    lane = lax.broadcasted_iota(jnp.int32, (1, LANES), 1)
    prev = jnp.where(lane < RW_BB, pltpu.roll(prev_scr[...], RW_BB, 1), pltpu.roll(pr, RW_BB, 1))
    prev_scr[...] = pr
    last_ref[...] = pr
    pm = pr + (prev - pr) * mu_ref[...]
    r = pm[0:D_HALF]
    kr = pm[D_HALF:2 * D_HALF]
    vr = pm[2 * D_HALF:3 * D_HALF]
    lo = pm[3 * D_HALF:RW_COLS_PAD]
    w_log = -_softplus(-(w0_ref[...] + _dot(w2_ref[...], jnp.tanh(lo)))) - 0.5
    a = _sigmoid(a0_ref[...] + _dot(a2_ref[...], lo))
    g = _dot(g2_ref[...], _sigmoid(lo))
    kk = kr * kkw_ref[...]
    kk = kk / jnp.maximum(jnp.sqrt(_head_sum_rows(kk * kk)), 1e-12)
    kh = kr * (1.0 + (a - 1.0) * kaw_ref[...])
    g_ref[...] = g
    bonus_ref[...] = _head_sum_rows(r * kh * rk_ref[...]) * vr

    grp = lane // RW_BB
    ngrp = LANES // RW_BB

    def scatter(x, o_ref, nrow, npiece):
        rot = [x[q * nrow:(q + 1) * nrow, :] if q == 0 else pltpu.roll(x[q * nrow:(q + 1) * nrow, :], q * RW_BB, 1)
               for q in range(npiece)]
        for t in range(RW_TB):
            m = rot[0]
            for q in range(1, npiece):
                m = jnp.where(grp == (t + q) % ngrp, rot[q], m)
            if t:
                m = pltpu.roll(m, LANES - t * RW_BB, 1)
            if npiece < ngrp:
                m = jnp.where(lane < LANES // 2, m, pltpu.roll(m, LANES // 2, 1))
            o_ref[t] = m

    scatter(r, r_ref, RW_DH, RW_HEADS)
    scatter(jnp.exp(-jnp.exp(w_log)), d_ref, RW_DH, RW_HEADS)
    scatter(kh, k_ref, RW_DH, RW_HEADS)
    scatter(-kk, a_ref, RW_DH, RW_HEADS)
    scatter(kk * a, b_ref, RW_DH, RW_HEADS)
    vv = jnp.concatenate([vr[h * RW_DH + half * RW_IP:h * RW_DH + (half + 1) * RW_IP, :]
                          for half in range(2) for h in range(RW_HEADS)], axis=0)
    scatter(vv, v_ref, RW_IP, ngrp)


def _lane_bcast(a, n):
    return jnp.broadcast_to(a.reshape(n, 1), (n, LANES))


def _rwkv_pre_t(x3, shift0, w_rwt, j, Wl):
    B, T, _ = x3.shape
    nblk = T // RW_TB
    padr = lambda w, o: jnp.pad(w, ((o, RW_LORA_PAD - o - w.shape[0]), (0, 0))).astype(BF16).T
    sh = jnp.pad(shift0.T, ((0, RW_COLS_PAD - RW_SHIFT_COLS), (LANES - RW_BB, 0)))
    col = lambda a: _lane_bcast(a, D_HALF)
    ins = [x3, sh, w_rwt, _lane_bcast(_pad_lanes(Wl['rw_mu'].reshape(1, -1), RW_COLS_PAD), RW_COLS_PAD),
           col(Wl['rw_w0']), col(Wl['rw_a0']),
           padr(Wl['rw_w2'], 0), padr(Wl['rw_a2'], RW_DECAY_LORA), padr(Wl['rw_g2'], RW_DECAY_LORA + RW_A_LORA),
           col(Wl['rw_kk']), col(Wl['rw_ka']), col(Wl['rw_rk'])]
    in_specs = [pl.BlockSpec((B, RW_TB, D_MODEL), lambda c: (0, c, 0)), _full((RW_COLS_PAD, LANES)),
                pl.BlockSpec((None, RW_COLS_PAD, D_MODEL), lambda c: (j, 0, 0)), _full((RW_COLS_PAD, LANES)),
                _full((D_HALF, LANES)), _full((D_HALF, LANES)),
                _full((D_HALF, RW_LORA_PAD)), _full((D_HALF, RW_LORA_PAD)), _full((D_HALF, RW_LORA_PAD)),
                _full((D_HALF, LANES)), _full((D_HALF, LANES)), _full((D_HALF, LANES))]
    blk = pl.BlockSpec((None, D_HALF, LANES), lambda c: (c, 0, 0))
    ktile = pl.BlockSpec((RW_TB, RW_DH, LANES), lambda c: (c, 0, 0))
    vtile = pl.BlockSpec((RW_TB, RW_IP, LANES), lambda c: (c, 0, 0))
    outs = pl.pallas_call(
        _rwkv_pre_t_kernel,
        grid=(nblk,),
        in_specs=in_specs,
        out_specs=[ktile] * 5 + [vtile, blk, blk, _full((RW_COLS_PAD, LANES))],
        out_shape=[jax.ShapeDtypeStruct((T, RW_DH, LANES), F32)] * 5
                  + [jax.ShapeDtypeStruct((T, RW_IP, LANES), F32)]
                  + [jax.ShapeDtypeStruct((nblk, D_HALF, LANES), F32)] * 2
                  + [jax.ShapeDtypeStruct((RW_COLS_PAD, LANES), F32)],
        scratch_shapes=[pltpu.VMEM((RW_COLS_PAD, LANES), F32)],
        compiler_params=_params(("arbitrary",)),
        name="rwkv_pre_t",
    )(*ins)
    shift_new = outs[8][:RW_SHIFT_COLS, LANES - RW_BB:].T
    return outs[:8], shift_new


def _rwkv_post_t_kernel(y_ref, g_ref, bonus_ref, gng_ref, gnb_ref, yd_ref):
    lane = lax.broadcasted_iota(jnp.int32, (1, LANES), 1)
    grp = lane // RW_BB
    ngrp = LANES // RW_BB
    ys = [y_ref[t] for t in range(RW_TB)]
    rolled = []
    for s in range(ngrp):
        m = ys[s % RW_TB]
        for q in range(1, ngrp):
            m = jnp.where(grp == q, ys[(q + s) % RW_TB], m)
        rolled.append(pltpu.roll(m, s * RW_BB, 1) if s else m)
    pieces = {}
    for q in range(ngrp):
        m = rolled[(-q) % ngrp]
        for t in range(1, RW_TB):
            m = jnp.where(grp == t, rolled[(t - q) % ngrp], m)
        pieces[divmod(q, RW_HEADS)] = m
    y = jnp.concatenate([pieces[(half, h)] for h in range(RW_HEADS) for half in range(2)], axis=0)
    mu = _head_sum_rows(y) * (1.0 / RW_DH)
    yc = y - mu
    var = _head_sum_rows(yc * yc) * (1.0 / RW_DH)
    hn = yc * lax.rsqrt(var + RW_GN_EPS) * gng_ref[...] + gnb_ref[...]
    yd = ((hn + bonus_ref[...]) * g_ref[...]).T
    for t in range(RW_TB):
        yd_ref[:, t, :] = yd[t * RW_BB:(t + 1) * RW_BB, :]


def _rwkv_rec_t(vecs, S0, Wl, T):
    r, d, k, a, b, v, g, bonus = vecs
    nblk = T // RW_TB
    B = RW_BB
    Sr = (S0.reshape(B, RW_HEADS, 2, RW_IP, RW_DH).transpose(3, 4, 2, 1, 0).reshape(1, RW_IP, RW_DH, LANES))
    y, S = _rwkv_rec_call(*(t[None] for t in (r, d, k, a, b, v)), Sr)
    blk = pl.BlockSpec((None, D_HALF, LANES), lambda c: (c, 0, 0))
    col = lambda a_: _lane_bcast(a_, D_HALF)
    yd = pl.pallas_call(
        _rwkv_post_t_kernel,
        grid=(nblk,),
        in_specs=[pl.BlockSpec((RW_TB, RW_IP, LANES), lambda c: (c, 0, 0)), blk, blk,
                  _full((D_HALF, LANES)), _full((D_HALF, LANES))],
        out_specs=pl.BlockSpec((B, RW_TB, D_HALF), lambda c: (0, c, 0)),
        out_shape=jax.ShapeDtypeStruct((B, T, D_HALF), F32),
        compiler_params=_params(("parallel",)),
        name="rwkv_post_t",
    )(y[0], g, bonus, col(Wl['rw_gn_g']), col(Wl['rw_gn_b']))
    S = S[0].reshape(RW_IP, RW_DH, 2, RW_HEADS, B).transpose(4, 3, 2, 0, 1).reshape(B, RW_HEADS, RW_DH, RW_DH)
    return yd, S


def _rwkv_post_kernel(y_ref, g_ref, bonus_ref, gng_ref, gnb_ref, o_ref):
    ones_bd = _group_ones(LANES, RW_DH)
    y = y_ref[...]
    mu = _group_sum(y, ones_bd) * (1.0 / RW_DH)
    yc = y - mu
    var = _group_sum(yc * yc, ones_bd) * (1.0 / RW_DH)
    hn = yc * lax.rsqrt(var + RW_GN_EPS) * gng_ref[...] + gnb_ref[...]
    o_ref[...] = (hn + bonus_ref[...]) * g_ref[...]


def _rwkv_post(y2, g2, bonus2, Wl, tm):
    M = y2.shape[0]
    row = pl.BlockSpec((tm, D_HALF), lambda i: (i, 0))
    return pl.pallas_call(
        _rwkv_post_kernel,
        grid=(M // tm,),
        in_specs=[row, row, row, _full((1, D_HALF)), _full((1, D_HALF))],
        out_specs=row,
        out_shape=jax.ShapeDtypeStruct((M, D_HALF), F32),
        compiler_params=_params(("parallel",)),
        name="rwkv_post",
    )(y2, g2, bonus2, Wl['rw_gn_g'].reshape(1, D_HALF), Wl['rw_gn_b'].reshape(1, D_HALF))


def _trunk(x, st, pos, W):
    B, T, _ = x.shape
    M = B * T
    decode = T == 1
    tm_proj = min(M, 512)
    tm_post = min(M, 256)
    tm_rw = min(M, 512)
    x2 = x.reshape(M, D_MODEL)
    new = {name: [] for name in st}
    stacked = {}
    for l in range(DEPTH):
        j = l // 2
        Wl = {name: v[j] for name, v in W['per_pair'][l % 2].items()}
        stl = {name: v[j] for name, v in st.items()}
        if l % 2 == 0:
            p, = _proj(x2, W['ev_w_in'], j, tm_proj, (EVEN_IN,))
            if decode:
                ya, yb, (h, cb, S) = _even_decode(p, pos, stl, Wl, st['ret_S'], j, stacked.get('ret_S'))
                stacked['ret_S'] = S
            else:
                ya, yb, (h, cb, S) = _even_prefill(p.reshape(B, T, EVEN_IN), pos, stl, Wl)
                new['ret_S'].append(S)
            new['lru_h'].append(h)
            new['lru_conv'].append(cb)
            wout = W['ev_w_out']
        else:
            if decode:
                p_ml, p_rw = _proj(x2, W['od_w_in'], j, tm_proj, (2 * D_HALF, RW_COLS_PAD))
                ya, (C, n, m, cb) = _mlstm_decode(p_ml, stl, Wl, st['ml_C'], j, stacked.get('ml_C'))
                stacked['ml_C'] = C
                r, d, k, v, a, b, g, bonus = _rwkv_pre_decode(p_rw, stl['rw_shift'], Wl)
                y, S = _rwkv_decode_step(r, d, k, v, a, b, stl['rw_S'])
                yb = _rwkv_post(y, g, bonus, Wl, tm_rw)
                shift_new = p_rw[:, :RW_SHIFT_COLS]
            else:
                p_ml, = _proj(x2, W['od_w_in'], j, tm_proj, (2 * D_HALF,))
                ya, (C, n, m, cb) = _mlstm_prefill(p_ml.reshape(B, T, 2 * D_HALF), stl, Wl)
                vecs, shift_new = _rwkv_pre_t(x2.reshape(B, T, D_MODEL), stl['rw_shift'], W['od_w_rwt'], j, Wl)
                yb, S = _rwkv_rec_t(vecs, stl['rw_S'], Wl, T)
                new['ml_C'].append(C)
            new['ml_n'].append(n)
            new['ml_m'].append(m)
            new['ml_conv'].append(cb)
            new['rw_S'].append(S)
            new['rw_shift'].append(shift_new)
            wout = W['od_w_out']
        x2 = _post(ya.reshape(M, D_HALF), yb.reshape(M, D_HALF), x2, wout, j, l,
                   W['ln1_g'], W['ln1_b'], W['mlp_w1'], W['mlp_w2'], W['ln2_g'], W['ln2_b'], tm_post)
    out = {name: stacked[name] if name in stacked else jnp.stack(v) for name, v in new.items()}
    return x2.reshape(B, T, D_MODEL), out


def _prepare_weights(w):
    even_names = ('lru_conv_w', 'lru_conv_b', 'lru_ba', 'lru_bx', 'lru_lambda', 'ret_gn_g', 'ret_gn_b')
    odd_names = ('ml_conv_w', 'ml_conv_b', 'ml_w_gate', 'ml_b_gate', 'ml_gn_g', 'ml_gn_b', 'ml_skip',
                 'rw_mu', 'rw_w0', 'rw_w2', 'rw_a0', 'rw_a2', 'rw_g2', 'rw_kk', 'rw_ka', 'rw_rk', 'rw_gn_g', 'rw_gn_b')
    even = {n: w[n] for n in even_names}
    even.update(lru_wa_d=jax.vmap(_lru_dense)(w['lru_wa']), lru_wx_d=jax.vmap(_lru_dense)(w['lru_wx']))
    odd = {n: w[n] for n in odd_names}
    odd.update(ml_wq_d=jax.vmap(_ml_dense)(w['ml_wq']), ml_wk_d=jax.vmap(_ml_dense)(w['ml_wk']),
               ml_wv_d=jax.vmap(_ml_dense)(w['ml_wv']))
    od_w_in = _pad_lanes(w['od_w_in'], ODD_IN_PAD).astype(BF16)
    return dict(per_pair=(even, odd),
                ev_w_in=w['ev_w_in'].astype(BF16), ev_w_out=w['ev_w_out'].astype(BF16),
                od_w_in=od_w_in, od_w_rwt=jnp.swapaxes(od_w_in[:, :, 2 * D_HALF:], 1, 2),
                od_w_out=w['od_w_out'].astype(BF16),
                mlp_w1=w['mlp_w1'].astype(BF16), mlp_w2=w['mlp_w2'].astype(BF16),
                ln1_g=w['ln1_g'], ln1_b=w['ln1_b'], ln2_g=w['ln2_g'], ln2_b=w['ln2_b'])


def _zero_states(batch):
    z = lambda *s: jnp.zeros(s, F32)
    n_even, n_odd = (DEPTH + 1) // 2, DEPTH // 2
    return dict(lru_h=z(n_even, batch, D_HALF), lru_conv=z(n_even, batch, CONV_W - 1, D_HALF),
                ret_S=z(n_even, batch, RET_HEADS, RET_DH, RET_DH),
                ml_C=z(n_odd, batch, ML_HEADS, ML_DH, ML_DH), ml_n=z(n_odd, batch, ML_HEADS, ML_DH),
                ml_m=z(n_odd, batch, ML_HEADS), ml_conv=z(n_odd, batch, CONV_W - 1, D_HALF),
                rw_S=z(n_odd, batch, RW_HEADS, RW_DH, RW_DH), rw_shift=z(n_odd, batch, RW_SHIFT_COLS))


def kernel(x_prompt, x_sample, state_lru_h, state_lru_conv, state_ret, state_mlstm_C, state_mlstm_n, state_mlstm_m, state_mlstm_conv, state_rwkv_S, state_rwkv_shift, ln1_g, ln1_b, ln2_g, ln2_b, mlp_w1, mlp_w2, ev_w_in, ev_w_out, lru_conv_w, lru_conv_b, lru_wa, lru_ba, lru_wx, lru_bx, lru_lambda, ret_gn_g, ret_gn_b, od_w_in, od_w_out, ml_conv_w, ml_conv_b, ml_wq, ml_wk, ml_wv, ml_w_gate, ml_b_gate, ml_gn_g, ml_gn_b, ml_skip, rw_mu, rw_w0, rw_w2, rw_a0, rw_a2, rw_g2, rw_kk, rw_ka, rw_rk, rw_gn_g, rw_gn_b):
    W = _prepare_weights(dict(
        ln1_g=ln1_g, ln1_b=ln1_b, ln2_g=ln2_g, ln2_b=ln2_b, mlp_w1=mlp_w1, mlp_w2=mlp_w2,
        ev_w_in=ev_w_in, ev_w_out=ev_w_out, lru_conv_w=lru_conv_w, lru_conv_b=lru_conv_b,
        lru_wa=lru_wa, lru_ba=lru_ba, lru_wx=lru_wx, lru_bx=lru_bx, lru_lambda=lru_lambda,
        ret_gn_g=ret_gn_g, ret_gn_b=ret_gn_b, od_w_in=od_w_in, od_w_out=od_w_out,
        ml_conv_w=ml_conv_w, ml_conv_b=ml_conv_b, ml_wq=ml_wq, ml_wk=ml_wk, ml_wv=ml_wv,
        ml_w_gate=ml_w_gate, ml_b_gate=ml_b_gate, ml_gn_g=ml_gn_g, ml_gn_b=ml_gn_b, ml_skip=ml_skip,
        rw_mu=rw_mu, rw_w0=rw_w0, rw_w2=rw_w2, rw_a0=rw_a0, rw_a2=rw_a2, rw_g2=rw_g2,
        rw_kk=rw_kk, rw_ka=rw_ka, rw_rk=rw_rk, rw_gn_g=rw_gn_g, rw_gn_b=rw_gn_b))
    st_sample = dict(lru_h=state_lru_h, lru_conv=state_lru_conv, ret_S=state_ret,
                     ml_C=state_mlstm_C, ml_n=state_mlstm_n, ml_m=state_mlstm_m, ml_conv=state_mlstm_conv,
                     rw_S=state_rwkv_S, rw_shift=state_rwkv_shift)
    pos_prompt = jnp.arange(x_prompt.shape[1], dtype=jnp.int32)
    pos_sample = PAST_LEN + jnp.arange(x_sample.shape[1], dtype=jnp.int32)
    y_prompt, sp = _trunk(x_prompt, _zero_states(x_prompt.shape[0]), pos_prompt, W)
    y_sample, ss = _trunk(x_sample, st_sample, pos_sample, W)
    names = ('lru_h', 'lru_conv', 'ret_S', 'ml_C', 'ml_n', 'ml_m', 'ml_conv', 'rw_S', 'rw_shift')
    return (y_prompt, y_sample) + tuple(sp[n] for n in names) + tuple(ss[n] for n in names)
```

```python
import functools

import jax
import jax.numpy as jnp
from jax import lax
from jax.experimental import pallas as pl
from jax.experimental.pallas import tpu as pltpu

F32 = jnp.float32
BF16 = jnp.bfloat16

D_MODEL = 1024
DEPTH = 4
PAST_LEN = 16384
D_HALF = D_MODEL // 2
CONV_W = 4
LRU_BLOCKS = 8
LRU_BLOCK = D_HALF // LRU_BLOCKS
LRU_C = 8.0
RET_HEADS = 4
RET_DH = D_HALF // RET_HEADS
CHUNK = 128
ROPE_BASE = 10000.0
ML_HEADS = 4
ML_DH = D_HALF // ML_HEADS
ML_QKV_BLOCK = 4
ML_NBLK = D_HALF // ML_QKV_BLOCK
RW_HEADS = 8
RW_DH = D_HALF // RW_HEADS
RW_DECAY_LORA = 32
RW_A_LORA = 32
RW_GATE_LORA = 96
RW_LORA = RW_DECAY_LORA + RW_A_LORA + RW_GATE_LORA
RW_SHIFT_COLS = 3 * D_HALF + RW_LORA
D_FF = 4 * D_MODEL
ALPHA = (2.0 * DEPTH) ** 0.25
EVEN_IN = 6 * D_HALF
ODD_IN = 2 * D_HALF + RW_SHIFT_COLS
LN_EPS = 1e-5
GN_EPS = 1e-5
RW_GN_EPS = 64e-5

LANES = 128
SUBLANES = 8
RW_LORA_PAD = 2 * LANES
RW_COLS_PAD = 3 * D_HALF + RW_LORA_PAD
ODD_IN_PAD = 2 * D_HALF + RW_COLS_PAD
RW_BB = 8
VMEM_LIMIT = 56 * 1024 * 1024


def _params(sem):
    return pltpu.CompilerParams(dimension_semantics=sem, vmem_limit_bytes=VMEM_LIMIT)


def _dot(a, b):
    return jnp.dot(a.astype(BF16), b.astype(BF16), preferred_element_type=F32)


def _dot_nt(a, b):
    return lax.dot_general(a.astype(BF16), b.astype(BF16), (((1,), (1,)), ((), ())),
                           preferred_element_type=F32)


def _dot_tn(a, b):
    return lax.dot_general(a.astype(BF16), b.astype(BF16), (((0,), (0,)), ((), ())),
                           preferred_element_type=F32)


def _split3(a):
    hi = a.astype(BF16)
    r1 = a - hi.astype(F32)
    mid = r1.astype(BF16)
    lo = (r1 - mid.astype(F32)).astype(BF16)
    return hi, mid, lo


def _xdot(a, b01):
    hi, mid, lo = _split3(a)
    f = lambda t: jnp.dot(t, b01, preferred_element_type=F32)
    return f(hi) + f(mid) + f(lo)


def _xdot_l(b01, a):
    hi, mid, lo = _split3(a)
    f = lambda t: jnp.dot(b01, t, preferred_element_type=F32)
    return f(hi) + f(mid) + f(lo)


def _sigmoid(x):
    return 1.0 / (1.0 + jnp.exp(-x))


def _silu(x):
    return x * _sigmoid(x)


def _softplus(x):
    return jnp.maximum(x, 0.0) + jnp.log1p(jnp.exp(-jnp.abs(x)))


def _gelu_tanh(x):
    return 0.5 * x * (1.0 + jnp.tanh(0.7978845608028654 * (x + 0.044715 * (x * x * x))))


def _layer_norm(x, g, b, eps):
    mu = jnp.mean(x, -1, keepdims=True)
    xc = x - mu
    var = jnp.mean(xc * xc, -1, keepdims=True)
    return xc * lax.rsqrt(var + eps) * g + b


def _group_ones(n, group):
    r = lax.broadcasted_iota(jnp.int32, (n, n), 0) // group
    c = lax.broadcasted_iota(jnp.int32, (n, n), 1) // group
    return jnp.where(r == c, 1.0, 0.0).astype(BF16)


def _group_sum(x, ones_bd):
    parts = [_xdot(x[:, s * LANES:(s + 1) * LANES], ones_bd) for s in range(x.shape[1] // LANES)]
    return jnp.concatenate(parts, axis=-1)


def _rotate(x, cosf, sinf):
    return x * cosf + pltpu.roll(x, RET_DH // 2, 1) * sinf


def _proj_kernel(x_ref, w_ref, *o_refs, splits):
    xb = x_ref[...].astype(BF16)
    off = 0
    for o_ref, n in zip(o_refs, splits):
        o_ref[...] = jnp.dot(xb, w_ref[:, off:off + n], preferred_element_type=F32)
        off += n


def _proj(x2d, w_all, j, tm, splits):
    M = x2d.shape[0]
    N = w_all.shape[2]
    return pl.pallas_call(
        functools.partial(_proj_kernel, splits=splits),
        grid=(M // tm,),
        in_specs=[pl.BlockSpec((tm, D_MODEL), lambda i: (i, 0)),
                  pl.BlockSpec((None, D_MODEL, N), lambda i: (j, 0, 0))],
        out_specs=[pl.BlockSpec((tm, n), lambda i: (i, 0)) for n in splits],
        out_shape=[jax.ShapeDtypeStruct((M, n), F32) for n in splits],
        compiler_params=_params(("parallel",)),
        name="proj_in",
    )(x2d, w_all)


FF_CHUNK = 1024


def _post_kernel(ya_ref, yb_ref, x_ref, wo_ref, g1_ref, b1_ref, w1_ref, w2_ref, g2_ref, b2_ref, o_ref):
    y = (jnp.dot(ya_ref[...].astype(BF16), wo_ref[0:D_HALF, :], preferred_element_type=F32)
         + jnp.dot(yb_ref[...].astype(BF16), wo_ref[D_HALF:D_MODEL, :], preferred_element_type=F32))
    x1 = _layer_norm(ALPHA * x_ref[...] + y, g1_ref[...], b1_ref[...], LN_EPS)
    x1b = x1.astype(BF16)
    acc = jnp.zeros(x1.shape, F32)
    for c in range(D_FF // FF_CHUNK):
        h = jnp.dot(x1b, w1_ref[:, c * FF_CHUNK:(c + 1) * FF_CHUNK], preferred_element_type=F32)
        h = jnp.square(jnp.maximum(h, 0.0))
        acc = acc + jnp.dot(h.astype(BF16), w2_ref[c * FF_CHUNK:(c + 1) * FF_CHUNK, :],
                            preferred_element_type=F32)
    o_ref[...] = _layer_norm(ALPHA * x1 + acc, g2_ref[...], b2_ref[...], LN_EPS)


def _post(ya, yb, x2d, wout_all, j, l, ln1_g, ln1_b, w1_all, w2_all, ln2_g, ln2_b, tm):
    M = x2d.shape[0]
    row = lambda i: (i, 0)
    vec = pl.BlockSpec((None, 1, D_MODEL), lambda i: (l, 0, 0))
    r3 = lambda a: a.reshape(DEPTH, 1, D_MODEL)
    return pl.pallas_call(
        _post_kernel,
        grid=(M // tm,),
        in_specs=[pl.BlockSpec((tm, D_HALF), row), pl.BlockSpec((tm, D_HALF), row),
                  pl.BlockSpec((tm, D_MODEL), row),
                  pl.BlockSpec((None, D_MODEL, D_MODEL), lambda i: (j, 0, 0)),
                  vec, vec,
                  pl.BlockSpec((None, D_MODEL, D_FF), lambda i: (l, 0, 0)),
                  pl.BlockSpec((None, D_FF, D_MODEL), lambda i: (l, 0, 0)),
                  vec, vec],
        out_specs=pl.BlockSpec((tm, D_MODEL), row),
        out_shape=jax.ShapeDtypeStruct((M, D_MODEL), F32),
        compiler_params=_params(("parallel",)),
        name="post_mlp",
    )(ya, yb, x2d, wout_all, r3(ln1_g), r3(ln1_b), w1_all, w2_all, r3(ln2_g), r3(ln2_b))


def _conv_prefill(x, xbuf, cw_ref, cb_ref, L):
    xbuf[SUBLANES:SUBLANES + L, :] = x
    y = cb_ref[...]
    for i in range(CONV_W):
        y = y + cw_ref[i:i + 1, :] * xbuf[SUBLANES - (CONV_W - 1) + i:SUBLANES - (CONV_W - 1) + i + L, :]
    tail = xbuf[L + SUBLANES - (CONV_W - 1):L + SUBLANES, :]
    xbuf[SUBLANES - (CONV_W - 1):SUBLANES, :] = tail
    return y, tail


def _lru_gates(xc, wa_ref, ba_ref, wx_ref, bx_ref, lam_ref):
    xcb = xc.astype(BF16)
    nslab = D_HALF // LANES
    ra = jnp.concatenate([jnp.dot(xcb[:, s * LANES:(s + 1) * LANES], wa_ref[s], preferred_element_type=F32)
                          for s in range(nslab)], axis=-1)
    rx = jnp.concatenate([jnp.dot(xcb[:, s * LANES:(s + 1) * LANES], wx_ref[s], preferred_element_type=F32)
                          for s in range(nslab)], axis=-1)
    r = _sigmoid(ra + ba_ref[...])
    i = _sigmoid(rx + bx_ref[...])
    log_a = -LRU_C * r * _softplus(-lam_ref[...])
    a = jnp.exp(log_a)
    t = jnp.tanh(log_a)
    u = jnp.sqrt(-2.0 * t / (1.0 - t)) * (i * xc)
    return a, u


def _head_norm(o, g, b, eps):
    mu = jnp.mean(o, -1, keepdims=True)
    oc = o - mu
    var = jnp.mean(oc * oc, -1, keepdims=True)
    return oc * lax.rsqrt(var + eps) * g + b


EV_NB = 4


def _even_prefill_kernel(xa_ref, ga_ref, q_ref, k_ref, v_ref, gb_ref,
                         cw_ref, cb_ref, wa_ref, ba_ref, wx_ref, bx_ref, lam_ref,
                         cos_ref, sin_ref, dmask_ref, qd_ref, kd_ref, cd_ref, gng_ref, gnb_ref,
                         h0_ref, conv0_ref, S0_ref,
                         ya_ref, yb_ref, h_ref, conv_ref, S_ref, xbuf, *, L):
    @pl.when(pl.program_id(1) == 0)
    def _():
        h_ref[...] = h0_ref[...]
        S_ref[...] = S0_ref[...]
        xbuf[:, SUBLANES - (CONV_W - 1):SUBLANES, :] = conv0_ref[...]

    rows = [_even_prefill_one(xa_ref.at[bi], ga_ref.at[bi], q_ref.at[bi], k_ref.at[bi], v_ref.at[bi], gb_ref.at[bi],
                              cw_ref, cb_ref, wa_ref, ba_ref, wx_ref, bx_ref, lam_ref,
                              cos_ref, sin_ref, dmask_ref, qd_ref, kd_ref, cd_ref, gng_ref, gnb_ref,
                              ya_ref.at[bi], yb_ref.at[bi], h_ref.at[bi], conv_ref.at[bi], S_ref.at[bi],
                              xbuf.at[bi], L=L) for bi in range(EV_NB)]
    _round_robin(rows)


def _even_prefill_one(xa_ref, ga_ref, q_ref, k_ref, v_ref, gb_ref,
                      cw_ref, cb_ref, wa_ref, ba_ref, wx_ref, bx_ref, lam_ref,
                      cos_ref, sin_ref, dmask_ref, qd_ref, kd_ref, cd_ref, gng_ref, gnb_ref,
                      ya_ref, yb_ref, h_ref, conv_ref, S_ref, xbuf, *, L):
    H = range(RET_HEADS)
    sls = [slice(hh * RET_DH, (hh + 1) * RET_DH) for hh in H]
    cosf = cos_ref[...]
    sinf = sin_ref[...]
    qh = [_rotate(q_ref[:, sl], cosf, sinf) for sl in sls]
    kh = [_rotate(k_ref[:, sl], cosf, sinf) * (RET_DH ** -0.5) for sl in sls]
    vh = [v_ref[:, sl].astype(BF16) for sl in sls]
    qk = [_dot_nt(qh[hh], kh[hh]) for hh in H]
    S = [S_ref[hh] for hh in H]
    qS = [_dot(qh[hh] * qd_ref[:, sls[hh]], S[hh]) for hh in H]
    kv = [_dot_tn(kh[hh] * kd_ref[:, sls[hh]], vh[hh]) for hh in H]
    yield
    xc, tail = _conv_prefill(xa_ref[...], xbuf, cw_ref, cb_ref, L)
    conv_ref[...] = tail
    a, u = _lru_gates(xc, wa_ref, ba_ref, wx_ref, bx_ref, lam_ref)
    yield
    for hh in H:
        S_ref[hh] = S[hh] * cd_ref[hh] + kv[hh]
    sc = [(qk[hh] * dmask_ref[hh]).astype(BF16) for hh in H]
    o = [jnp.dot(sc[hh], vh[hh], preferred_element_type=F32) + qS[hh] for hh in H]
    yield
    row = lax.broadcasted_iota(jnp.int32, (L, D_HALF), 0)
    s = 1
    while s < L:
        keep = row >= s
        a_sh = jnp.where(keep, pltpu.roll(a, s, 0), 1.0)
        u_sh = jnp.where(keep, pltpu.roll(u, s, 0), 0.0)
        u = a * u_sh + u
        a = a * a_sh
        s *= 2
        yield
    h = a * h_ref[...] + u
    h_ref[...] = h[L - 1:L, :]
    ya_ref[...] = _gelu_tanh(ga_ref[...]) * h
    ones = jnp.ones((RET_DH, LANES), BF16)
    mu = [jnp.dot(o[hh].astype(BF16), ones, preferred_element_type=F32) * (1.0 / RET_DH) for hh in H]
    yield
    oc = [o[hh] - mu[hh] for hh in H]
    var = [jnp.dot((oc[hh] * oc[hh]).astype(BF16), ones, preferred_element_type=F32) * (1.0 / RET_DH) for hh in H]
    yield
    for hh in H:
        sl = sls[hh]
        on = oc[hh] * lax.rsqrt(var[hh] + GN_EPS) * gng_ref[:, sl] + gnb_ref[:, sl]
        yb_ref[:, sl] = _silu(gb_ref[:, sl]) * on


def _ret_tables(L):
    log_gamma = jnp.log1p(-jnp.exp2(-5.0 - jnp.arange(RET_HEADS, dtype=F32)))
    idx = jnp.arange(L, dtype=F32)
    diff = idx[:, None] - idx[None, :]
    dmask = jnp.where(diff >= 0, jnp.exp(log_gamma[:, None, None] * jnp.maximum(diff, 0.0)), 0.0)
    qd = jnp.exp(log_gamma[:, None] * (idx + 1.0))
    kd = jnp.exp(log_gamma[:, None] * (L - 1.0 - idx))
    cd = jnp.exp(log_gamma * L)
    qd_full = jnp.repeat(qd.T, RET_DH, axis=1)
    kd_full = jnp.repeat(kd.T, RET_DH, axis=1)
    cd_full = jnp.broadcast_to(cd[:, None, None], (RET_HEADS, 1, RET_DH))
    return dmask, qd_full, kd_full, cd_full


def _rope_tables(pos):
    half = RET_DH // 2
    inv = ROPE_BASE ** (-jnp.arange(half, dtype=F32) / half)
    ang = pos.astype(F32)[:, None] * inv[None, :]
    cos, sin = jnp.cos(ang), jnp.sin(ang)
    return jnp.concatenate([cos, cos], -1), jnp.concatenate([-sin, sin], -1)


def _lru_dense(w):
    pairs = LANES // LRU_BLOCK
    w4 = w.reshape(LRU_BLOCKS // pairs, pairs, LRU_BLOCK, LRU_BLOCK)
    eye = jnp.eye(pairs, dtype=w.dtype)
    d = w4[:, :, :, None, :] * eye[None, :, None, :, None]
    return d.reshape(LRU_BLOCKS // pairs, LANES, LANES).astype(BF16)


def _full(shape):
    n = len(shape)
    return pl.BlockSpec(shape, lambda *_: (0,) * n)


def _even_prefill(p3, pos, st, Wl):
    B, T, _ = p3.shape
    L = CHUNK
    nc = T // L
    dmask, qd, kd, cd = _ret_tables(L)
    cosf, sinf = _rope_tables(pos)
    nb = EV_NB
    col = lambda i: pl.BlockSpec((nb, L, D_HALF), lambda b, c: (b, c, i))
    perb = lambda *s: pl.BlockSpec((nb,) + s, lambda b, c: (b,) + (0,) * len(s))
    row2 = lambda a: a.reshape(1, D_HALF)
    ins = [p3] * 6 + [Wl['lru_conv_w'], row2(Wl['lru_conv_b']), Wl['lru_wa_d'], row2(Wl['lru_ba']),
                      Wl['lru_wx_d'], row2(Wl['lru_bx']), row2(Wl['lru_lambda']),
                      cosf, sinf, dmask, qd, kd, cd, row2(Wl['ret_gn_g']), row2(Wl['ret_gn_b']),
                      st['lru_h'].reshape(B, 1, D_HALF), st['lru_conv'], st['ret_S']]
    in_specs = [col(i) for i in range(6)] + [
        _full((CONV_W, D_HALF)), _full((1, D_HALF)), _full((4, LANES, LANES)), _full((1, D_HALF)),
        _full((4, LANES, LANES)), _full((1, D_HALF)), _full((1, D_HALF)),
        pl.BlockSpec((L, RET_DH), lambda b, c: (c, 0)), pl.BlockSpec((L, RET_DH), lambda b, c: (c, 0)),
        _full((RET_HEADS, L, L)), _full((L, D_HALF)), _full((L, D_HALF)), _full((RET_HEADS, 1, RET_DH)),
        _full((1, D_HALF)), _full((1, D_HALF)),
        perb(1, D_HALF), perb(CONV_W - 1, D_HALF), perb(RET_HEADS, RET_DH, RET_DH)]
    seq = pl.BlockSpec((nb, L, D_HALF), lambda b, c: (b, c, 0))
    ya, yb, h, conv, S = pl.pallas_call(
        functools.partial(_even_prefill_kernel, L=L),
        grid=(B // nb, nc),
        in_specs=in_specs,
        out_specs=[seq, seq, perb(1, D_HALF), perb(CONV_W - 1, D_HALF), perb(RET_HEADS, RET_DH, RET_DH)],
        out_shape=[jax.ShapeDtypeStruct((B, T, D_HALF), F32), jax.ShapeDtypeStruct((B, T, D_HALF), F32),
                   jax.ShapeDtypeStruct((B, 1, D_HALF), F32),
                   jax.ShapeDtypeStruct((B, CONV_W - 1, D_HALF), F32),
                   jax.ShapeDtypeStruct((B, RET_HEADS, RET_DH, RET_DH), F32)],
        scratch_shapes=[pltpu.VMEM((nb, L + SUBLANES, D_HALF), F32)],
        compiler_params=_params(("parallel", "arbitrary")),
        name="even_prefill",
    )(*ins)
    return ya, yb, (h.reshape(B, D_HALF), conv, S)


DEC_BB = 8


def _even_decode_kernel(p_ref, cw_ref, cb_ref, wa_ref, ba_ref, wx_ref, bx_ref, lam_ref,
                        cos_ref, sin_ref, dm_ref, qd_ref, kd_ref, cd_ref, gng_ref, gnb_ref,
                        h0_ref, conv0_ref, S0_ref,
                        ya_ref, yb_ref, h_ref, conv_ref, S_ref):
    col = lambda i: p_ref[:, i * D_HALF:(i + 1) * D_HALF]
    xa = col(0)
    xc = cb_ref[...] + cw_ref[CONV_W - 1:CONV_W, :] * xa
    for i in range(CONV_W - 1):
        xc = xc + cw_ref[i:i + 1, :] * conv0_ref[i]
    for i in range(CONV_W - 2):
        conv_ref[i] = conv0_ref[i + 1]
    conv_ref[CONV_W - 2] = xa
    a, u = _lru_gates(xc, wa_ref, ba_ref, wx_ref, bx_ref, lam_ref)
    h = a * h0_ref[...] + u
    h_ref[...] = h
    ya_ref[...] = _gelu_tanh(col(1)) * h

    cosf = cos_ref[...]
    sinf = sin_ref[...]
    row8 = lax.broadcasted_iota(jnp.int32, (SUBLANES, RET_DH), 0)
    q, k, v, gb = col(2), col(3), col(4), col(5)
    for hh in range(RET_HEADS):
        sl = slice(hh * RET_DH, (hh + 1) * RET_DH)
        qh = _rotate(q[:, sl], cosf, sinf)
        kh = _rotate(k[:, sl], cosf, sinf) * (RET_DH ** -0.5)
        vh = v[:, sl]
        qk = jnp.sum(qh * kh, -1, keepdims=True) * dm_ref[:, sl]
        qq = qh * qd_ref[:, sl]
        kk = kh * kd_ref[:, sl]
        rows = []
        for bi in range(DEC_BB):
            S = S0_ref[bi, hh]
            q8 = jnp.broadcast_to(qq[bi:bi + 1, :], (SUBLANES, RET_DH))
            rows.append(_dot_f32(q8, S)[0:1, :])
            k8 = jnp.where(row8 == 0, jnp.broadcast_to(kk[bi:bi + 1, :], (SUBLANES, RET_DH)), 0.0)
            v8 = jnp.broadcast_to(vh[bi:bi + 1, :], (SUBLANES, RET_DH))
            S_ref[bi, hh] = S * cd_ref[hh] + _dot_tn_f32(k8, v8)
        o = qk * vh + jnp.concatenate(rows, axis=0)
        on = _head_norm(o, gng_ref[:, sl], gnb_ref[:, sl], GN_EPS)
        yb_ref[:, sl] = _silu(gb[:, sl]) * on


def _dot_f32(a, b):
    return jnp.dot(a, b, preferred_element_type=F32)


def _dot_tn_f32(a, b):
    return lax.dot_general(a, b, (((0,), (0,)), ((), ())), preferred_element_type=F32)


def _skip_ref(kernel, pos):
    def wrapped(*refs):
        kernel(*refs[:pos], *refs[pos + 1:])
    return wrapped


def _stacked_state_io(S_all, S_prev, j, bb):
    tail = S_all.shape[2:]
    spec = pl.BlockSpec((None, bb) + tail, lambda i: (j, i) + (0,) * len(tail))
    extra_in = [] if S_prev is None else [S_prev]
    extra_specs = [] if S_prev is None else [pl.BlockSpec(memory_space=pl.ANY)]
    return spec, extra_in, extra_specs, jax.ShapeDtypeStruct(S_all.shape, F32)


def _even_decode(p2, pos, st, Wl, S_all, j, S_prev):
    B = p2.shape[0]
    bb = DEC_BB
    dmask, qd, kd, cd = _ret_tables(1)
    dm = jnp.repeat(dmask[:, 0, :].T, RET_DH, axis=1)
    cosf, sinf = _rope_tables(pos)
    row2 = lambda a: a.reshape(1, D_HALF)
    rows = lambda n: pl.BlockSpec((bb, n), lambda i: (i, 0))
    convs = pl.BlockSpec((CONV_W - 1, bb, D_HALF), lambda i: (0, i, 0))
    ins = [p2, Wl['lru_conv_w'], row2(Wl['lru_conv_b']), Wl['lru_wa_d'], row2(Wl['lru_ba']),
           Wl['lru_wx_d'], row2(Wl['lru_bx']), row2(Wl['lru_lambda']),
           cosf, sinf, dm, qd, kd, cd, row2(Wl['ret_gn_g']), row2(Wl['ret_gn_b']),
           st['lru_h'], jnp.swapaxes(st['lru_conv'], 0, 1), S_all]
    Ss, extra_in, extra_specs, S_shape = _stacked_state_io(S_all, S_prev, j, bb)
    kern = _even_decode_kernel if S_prev is None else _skip_ref(_even_decode_kernel, len(ins))
    aliases = {} if S_prev is None else {len(ins): 4}
    in_specs = [rows(EVEN_IN), _full((CONV_W, D_HALF)), _full((1, D_HALF)), _full((4, LANES, LANES)),
                _full((1, D_HALF)), _full((4, LANES, LANES)), _full((1, D_HALF)), _full((1, D_HALF)),
                _full((1, RET_DH)), _full((1, RET_DH)), _full((1, D_HALF)), _full((1, D_HALF)),
                _full((1, D_HALF)), _full((RET_HEADS, 1, RET_DH)), _full((1, D_HALF)), _full((1, D_HALF)),
                rows(D_HALF), convs, Ss] + extra_specs
    ya, yb, h, conv, S = pl.pallas_call(
        kern,
        grid=(B // bb,),
        in_specs=in_specs,
        out_specs=[rows(D_HALF), rows(D_HALF), rows(D_HALF), convs, Ss],
        out_shape=[jax.ShapeDtypeStruct((B, D_HALF), F32), jax.ShapeDtypeStruct((B, D_HALF), F32),
                   jax.ShapeDtypeStruct((B, D_HALF), F32),
                   jax.ShapeDtypeStruct((CONV_W - 1, B, D_HALF), F32), S_shape],
        input_output_aliases=aliases,
        compiler_params=_params(("parallel",)),
        name="even_decode",
    )(*ins, *extra_in)
    return ya, yb, (h, jnp.swapaxes(conv, 0, 1), S)


def _mlstm_qkv_gates(xm, xc, wq_ref, wk_ref, wv_ref, wg_ref, bg_ref):
    q = _dot(xc, wq_ref[...])
    k = _dot(xc, wk_ref[...])
    v = _dot(xm, wv_ref[...])
    g_col = (_dot(q, wg_ref[0:D_HALF, :]) + _dot(k, wg_ref[D_HALF:2 * D_HALF, :])
             + _dot(v, wg_ref[2 * D_HALF:3 * D_HALF, :]) + bg_ref[...])
    return q, k, v, g_col


ML_NB = 4


def _skewed(gens):
    live = []
    pending = list(gens)
    while pending or live:
        if pending:
            live.append(pending.pop(0))
        nxt = []
        for g in live:
            try:
                next(g)
                nxt.append(g)
            except StopIteration:
                pass
        live = nxt


def _round_robin(gens):
    gens = list(gens)
    while gens:
        alive = []
        for g in gens:
            try:
                next(g)
                alive.append(g)
            except StopIteration:
                pass
        gens = alive


def _mlstm_prefill_kernel(xm_ref, z_ref, cw_ref, cb_ref, wq_ref, wk_ref, wv_ref, wg_ref, bg_ref,
                          wgt_ref, bgt_ref, gng_ref, gnb_ref, skip_ref,
                          conv0_ref, C0_ref, n0_ref, m0_ref,
                          yc_ref, conv_ref, C_ref, n_ref, m_ref, xbuf, ncol, *, L):
    @pl.when(pl.program_id(1) == 0)
    def _():
        ones = jnp.ones((ML_DH, LANES), BF16)
        eye = jnp.where(lax.broadcasted_iota(jnp.int32, (ML_DH, LANES), 0)
                        == lax.broadcasted_iota(jnp.int32, (ML_DH, LANES), 1), 1.0, 0.0)
        C_ref[...] = C0_ref[...]
        m_ref[...] = m0_ref[...]
        for bi in range(ML_NB):
            xbuf[bi, SUBLANES - (CONV_W - 1):SUBLANES, :] = conv0_ref[bi]
            for hh in range(ML_HEADS):
                ncol[bi, hh] = _xdot(eye * n0_ref[bi, :, hh * ML_DH:(hh + 1) * ML_DH], ones)

    rows = [_mlstm_prefill_one(xm_ref.at[bi], z_ref.at[bi], cw_ref, cb_ref, wq_ref, wk_ref, wv_ref, wg_ref, bg_ref,
                               wgt_ref, bgt_ref, gng_ref, gnb_ref, skip_ref,
                               yc_ref.at[bi], conv_ref.at[bi], C_ref.at[bi], n_ref.at[bi], m_ref.at[bi],
                               xbuf.at[bi], ncol.at[bi], L=L) for bi in range(ML_NB)]
    _round_robin(rows)


def _mlstm_prefill_one(xm_ref, z_ref, cw_ref, cb_ref, wq_ref, wk_ref, wv_ref, wg_ref, bg_ref,
                       wgt_ref, bgt_ref, gng_ref, gnb_ref, skip_ref,
                       yc_ref, conv_ref, C_ref, n_ref, m_ref, xbuf, ncol, *, L):
    xm = xm_ref[...]
    xc, tail = _conv_prefill(xm, xbuf, cw_ref, cb_ref, L)
    conv_ref[...] = tail
    xc = _silu(xc)
    yield
    q, k, v, g_col = _mlstm_qkv_gates(xm, xc, wq_ref, wk_ref, wv_ref, wg_ref, bg_ref)
    g_row = (_dot_nt(wgt_ref[:, 0:D_HALF], q) + _dot_nt(wgt_ref[:, D_HALF:2 * D_HALF], k)
             + _dot_nt(wgt_ref[:, 2 * D_HALF:3 * D_HALF], v) + bgt_ref[...])
    yield
    ri = lax.broadcasted_iota(jnp.int32, (L, L), 0)
    ci = lax.broadcasted_iota(jnp.int32, (L, L), 1)
    causal = ri >= ci
    tril = jnp.where(causal, 1.0, 0.0).astype(BF16)
    triu = jnp.where(ci >= ri, 1.0, 0.0).astype(BF16)
    ones = jnp.ones((L, LANES), BF16)
    eye = jnp.where(ri == ci, 1.0, 0.0)

    li_col = g_col
    lf_col = -_softplus(-pltpu.roll(g_col, LANES - ML_HEADS, 1))
    b_col = _xdot_l(tril, lf_col)
    lf_row = -_softplus(-g_row)
    b_row = _xdot(lf_row, triu)
    yield
    c_row = g_row[0:ML_HEADS, :] - b_row[ML_HEADS:2 * ML_HEADS, :]
    row = lax.broadcasted_iota(jnp.int32, (L, LANES), 0)
    pm = li_col - b_col
    sft = 1
    while sft < L:
        pm = jnp.maximum(pm, jnp.where(row >= sft, pltpu.roll(pm, sft, 0), -jnp.inf))
        sft *= 2
    m_prev = m_ref[...]
    u_col = -jnp.maximum(pm, m_prev)
    m_t_col = b_col - u_col
    e_col = jnp.exp(-m_t_col)
    m_new = m_t_col[L - 1:L, :]
    b_last = b_col[L - 1:L, :]
    wk_col = jnp.exp(b_last - b_col + li_col - m_new)
    wC_row = jnp.exp(b_last + m_prev - m_new)
    m_ref[...] = m_new
    yield
    vones = ones
    rep = lambda col, hh: jnp.broadcast_to(col[:, hh:hh + 1], (L, LANES))
    H = range(ML_HEADS)
    sls = [slice(hh * ML_DH, (hh + 1) * ML_DH) for hh in H]
    qh = [q[:, sl].astype(BF16) for sl in sls]
    kh = [k[:, sl] * (ML_DH ** -0.5) for sl in sls]
    vh1 = [jnp.concatenate([v[:, sl].astype(BF16), vones], axis=-1) for sl in sls]
    qk = [_dot_nt(qh[hh], kh[hh]) for hh in H]
    CN = [jnp.concatenate([C_ref[hh], ncol[hh]], axis=-1) for hh in H]
    qc = [jnp.dot(qh[hh], CN[hh].astype(BF16), preferred_element_type=F32) for hh in H]
    kw = [kh[hh] * rep(wk_col, hh) for hh in H]
    upd = [lax.dot_general(kw[hh].astype(BF16), vh1[hh], (((0,), (0,)), ((), ())), preferred_element_type=F32)
           for hh in H]
    yield
    for hh in H:
        w_C =jnp.broadcast_to(wC_row[:, hh:hh + 1], (ML_DH, 2 * ML_DH))
        CNn = w_C * CN[hh] + upd[hh]
        C_ref[hh] = CNn[:, 0:ML_DH]
        ncol[hh] = CNn[:, ML_DH:2 * ML_DH]
        n_ref[:, sls[hh]] = jnp.sum(CNn[:, ML_DH:2 * ML_DH] * eye, axis=0, keepdims=True)
    yield
    u = [rep(u_col, hh) for hh in H]
    s = [(qk[hh] * jnp.exp(jnp.where(causal, u[hh] + c_row[hh:hh + 1, :], -jnp.inf))).astype(BF16) for hh in H]
    sv = [jnp.dot(s[hh], vh1[hh], preferred_element_type=F32) for hh in H]
    yield
    hcell = []
    for hh in H:
        w_inter = jnp.exp(rep(m_prev, hh) + u[hh])
        num = sv[hh][:, 0:ML_DH] + w_inter * qc[hh][:, 0:ML_DH]
        den = sv[hh][:, ML_DH:2 * ML_DH] + w_inter * qc[hh][:, ML_DH:2 * ML_DH]
        hcell.append(num / jnp.maximum(jnp.abs(den), rep(e_col, hh)))
    mu = [jnp.dot(hcell[hh].astype(BF16), ones, preferred_element_type=F32) * (1.0 / ML_DH) for hh in H]
    yield
    oc = [hcell[hh] - mu[hh] for hh in H]
    var = [jnp.dot((oc[hh] * oc[hh]).astype(BF16), ones, preferred_element_type=F32) * (1.0 / ML_DH) for hh in H]
    for hh in H:
        sl = sls[hh]
        hn = oc[hh] * lax.rsqrt(var[hh] + GN_EPS) * gng_ref[:, sl] + gnb_ref[:, sl]
        yc_ref[:, sl] = (hn + skip_ref[:, sl] * xc[:, sl]) * _silu(z_ref[:, sl])


def _ml_dense(w):
    w2 = w.reshape(D_HALF, ML_QKV_BLOCK)
    c = jnp.arange(D_HALF)
    spread = (c[None, :] % ML_QKV_BLOCK == jnp.arange(ML_QKV_BLOCK)[:, None]).astype(w.dtype)
    full = jnp.dot(w2, spread, precision=lax.Precision.HIGHEST)
    same_block = c[:, None] // ML_QKV_BLOCK == c[None, :] // ML_QKV_BLOCK
    return jnp.where(same_block, full, 0.0).astype(BF16)


def _pad_lanes(a, n=LANES):
    return jnp.pad(a, [(0, 0)] * (a.ndim - 1) + [(0, n - a.shape[-1])])


def _mlstm_weights(Wl):
    wg = _pad_lanes(Wl['ml_w_gate']).astype(BF16)
    bg = _pad_lanes(Wl['ml_b_gate'].reshape(1, 2 * ML_HEADS))
    wgt = Wl['ml_w_gate'].T.astype(BF16)
    bgt = jnp.broadcast_to(Wl['ml_b_gate'].reshape(2 * ML_HEADS, 1), (2 * ML_HEADS, LANES))
    return wg, bg, wgt, bgt


def _mlstm_prefill(p3, st, Wl):
    B, T, _ = p3.shape
    L = CHUNK
    nc = T // L
    wg, bg, wgt, bgt = _mlstm_weights(Wl)
    nb = ML_NB
    col = lambda i: pl.BlockSpec((nb, L, D_HALF), lambda b, c: (b, c, i))
    perb = lambda *s: pl.BlockSpec((nb,) + s, lambda b, c: (b,) + (0,) * len(s))
    row2 = lambda a: a.reshape(1, D_HALF)
    ins = [p3, p3, Wl['ml_conv_w'], row2(Wl['ml_conv_b']), Wl['ml_wq_d'], Wl['ml_wk_d'], Wl['ml_wv_d'],
           wg, bg, wgt, bgt, row2(Wl['ml_gn_g']), row2(Wl['ml_gn_b']), row2(Wl['ml_skip']),
           st['ml_conv'], st['ml_C'], st['ml_n'].reshape(B, 1, D_HALF),
           _pad_lanes(st['ml_m']).reshape(B, 1, LANES)]
    in_specs = [col(0), col(1), _full((CONV_W, D_HALF)), _full((1, D_HALF)),
                _full((D_HALF, D_HALF)), _full((D_HALF, D_HALF)), _full((D_HALF, D_HALF)),
                _full((3 * D_HALF, LANES)), _full((1, LANES)), _full((2 * ML_HEADS, 3 * D_HALF)),
                _full((2 * ML_HEADS, LANES)), _full((1, D_HALF)), _full((1, D_HALF)), _full((1, D_HALF)),
                perb(CONV_W - 1, D_HALF), perb(ML_HEADS, ML_DH, ML_DH), perb(1, D_HALF), perb(1, LANES)]
    seq = pl.BlockSpec((nb, L, D_HALF), lambda b, c: (b, c, 0))
    yc, conv, C, n, m = pl.pallas_call(
        functools.partial(_mlstm_prefill_kernel, L=L),
        grid=(B // nb, nc),
        in_specs=in_specs,
        out_specs=[seq, perb(CONV_W - 1, D_HALF), perb(ML_HEADS, ML_DH, ML_DH), perb(1, D_HALF), perb(1, LANES)],
        out_shape=[jax.ShapeDtypeStruct((B, T, D_HALF), F32),
                   jax.ShapeDtypeStruct((B, CONV_W - 1, D_HALF), F32),
                   jax.ShapeDtypeStruct((B, ML_HEADS, ML_DH, ML_DH), F32),
                   jax.ShapeDtypeStruct((B, 1, D_HALF), F32),
                   jax.ShapeDtypeStruct((B, 1, LANES), F32)],
        scratch_shapes=[pltpu.VMEM((nb, L + SUBLANES, D_HALF), F32),
                        pltpu.VMEM((nb, ML_HEADS, ML_DH, LANES), F32)],
        compiler_params=_params(("parallel", "arbitrary")),
        name="mlstm_prefill",
    )(*ins)
    return yc, (C, n.reshape(B, ML_HEADS, ML_DH), m[:, 0, :ML_HEADS], conv)


def _mlstm_decode_kernel(p_ref, cw_ref, cb_ref, wq_ref, wk_ref, wv_ref, wg_ref, bg_ref,
                         gng_ref, gnb_ref, skip_ref, conv0_ref, C0_ref, n0_ref, m0_ref,
                         yc_ref, conv_ref, C_ref, n_ref, m_ref):
    xm = p_ref[:, 0:D_HALF]
    z = p_ref[:, D_HALF:2 * D_HALF]
    xc = cb_ref[...] + cw_ref[CONV_W - 1:CONV_W, :] * xm
    for i in range(CONV_W - 1):
        xc = xc + cw_ref[i:i + 1, :] * conv0_ref[i]
    for i in range(CONV_W - 2):
        conv_ref[i] = conv0_ref[i + 1]
    conv_ref[CONV_W - 2] = xm
    xc = _silu(xc)
    q, k, v, g = _mlstm_qkv_gates(xm, xc, wq_ref, wk_ref, wv_ref, wg_ref, bg_ref)
    lf_all = -_softplus(-g)
    lane = lax.broadcasted_iota(jnp.int32, (1, LANES), 1)
    row8 = lax.broadcasted_iota(jnp.int32, (SUBLANES, ML_DH), 0)
    m_all = m0_ref[...]
    m_out = m_all
    for hh in range(ML_HEADS):
        sl = slice(hh * ML_DH, (hh + 1) * ML_DH)
        qh, vh = q[:, sl], v[:, sl]
        kh = k[:, sl] * (ML_DH ** -0.5)
        li = g[:, hh:hh + 1]
        lf = lf_all[:, ML_HEADS + hh:ML_HEADS + hh + 1]
        m_prev = m_all[:, hh:hh + 1]
        n = n0_ref[:, sl]
        log_inter = lf + m_prev
        m_t = jnp.maximum(li, log_inter)
        s = jnp.sum(qh * kh, -1, keepdims=True) * jnp.exp(li - m_t)
        w_inter = jnp.exp(log_inter - m_t)
        w_k = jnp.exp(li - m_t)
        w_C = jnp.exp(log_inter - m_t)
        kw = kh * w_k
        rows = []
        for bi in range(DEC_BB):
            C = C0_ref[bi, hh]
            q8 = jnp.broadcast_to(qh[bi:bi + 1, :], (SUBLANES, ML_DH))
            rows.append(_dot_f32(q8, C)[0:1, :])
            k8 = jnp.where(row8 == 0, jnp.broadcast_to(kw[bi:bi + 1, :], (SUBLANES, ML_DH)), 0.0)
            v8 = jnp.broadcast_to(vh[bi:bi + 1, :], (SUBLANES, ML_DH))
            C_ref[bi, hh] = w_C[bi:bi + 1, :] * C + _dot_tn_f32(k8, v8)
        qC = jnp.concatenate(rows, axis=0)
        num = s * vh + w_inter * qC
        den = s + w_inter * jnp.sum(qh * n, -1, keepdims=True)
        hcell = num / jnp.maximum(jnp.abs(den), jnp.exp(-m_t))
        n_ref[:, sl] = w_C * n + kw
        m_out = jnp.where(lane == hh, m_t, m_out)
        hn = _head_norm(hcell, gng_ref[:, sl], gnb_ref[:, sl], GN_EPS)
        yc_ref[:, sl] = (hn + skip_ref[:, sl] * xc[:, sl]) * _silu(z[:, sl])
    m_ref[...] = m_out


def _mlstm_decode(p2, st, Wl, C_all, j, C_prev):
    B = p2.shape[0]
    bb = DEC_BB
    wg, bg, _, _ = _mlstm_weights(Wl)
    row2 = lambda a: a.reshape(1, D_HALF)
    rows = lambda n: pl.BlockSpec((bb, n), lambda i: (i, 0))
    convs = pl.BlockSpec((CONV_W - 1, bb, D_HALF), lambda i: (0, i, 0))
    ins = [p2, Wl['ml_conv_w'], row2(Wl['ml_conv_b']), Wl['ml_wq_d'], Wl['ml_wk_d'], Wl['ml_wv_d'], wg, bg,
           row2(Wl['ml_gn_g']), row2(Wl['ml_gn_b']), row2(Wl['ml_skip']),
           jnp.swapaxes(st['ml_conv'], 0, 1), C_all, st['ml_n'].reshape(B, D_HALF), _pad_lanes(st['ml_m'])]
    Cs, extra_in, extra_specs, C_shape = _stacked_state_io(C_all, C_prev, j, bb)
    kern = _mlstm_decode_kernel if C_prev is None else _skip_ref(_mlstm_decode_kernel, len(ins))
    aliases = {} if C_prev is None else {len(ins): 2}
    in_specs = [pl.BlockSpec((bb, 2 * D_HALF), lambda i: (i, 0)), _full((CONV_W, D_HALF)), _full((1, D_HALF)),
                _full((D_HALF, D_HALF)), _full((D_HALF, D_HALF)), _full((D_HALF, D_HALF)),
                _full((3 * D_HALF, LANES)), _full((1, LANES)),
                _full((1, D_HALF)), _full((1, D_HALF)), _full((1, D_HALF)),
                convs, Cs, rows(D_HALF), rows(LANES)] + extra_specs
    yc, conv, C, n, m = pl.pallas_call(
        kern,
        grid=(B // bb,),
        in_specs=in_specs,
        out_specs=[rows(D_HALF), convs, Cs, rows(D_HALF), rows(LANES)],
        out_shape=[jax.ShapeDtypeStruct((B, D_HALF), F32),
                   jax.ShapeDtypeStruct((CONV_W - 1, B, D_HALF), F32),
                   C_shape,
                   jax.ShapeDtypeStruct((B, D_HALF), F32),
                   jax.ShapeDtypeStruct((B, LANES), F32)],
        input_output_aliases=aliases,
        compiler_params=_params(("parallel",)),
        name="mlstm_decode",
    )(*ins, *extra_in)
    return yc, (C, n.reshape(B, ML_HEADS, ML_DH), m[:, :ML_HEADS], jnp.swapaxes(conv, 0, 1))


def _rwkv_pre_body(pr, pr_prev, mu_ref, w0_ref, a0_ref, w2_ref, a2_ref, g2_ref, kkw_ref, kaw_ref, rk_ref,
                   r_ref, d_ref, k_ref, v_ref, a_ref, b_ref, g_ref, bonus_ref):
    pm = pr + (pr_prev - pr) * mu_ref[...]
    r = pm[:, 0:D_HALF]
    kr = pm[:, D_HALF:2 * D_HALF]
    vr = pm[:, 2 * D_HALF:3 * D_HALF]
    lo = pm[:, 3 * D_HALF:RW_COLS_PAD]
    w_log = -_softplus(-(w0_ref[...] + _dot(jnp.tanh(lo), w2_ref[...]))) - 0.5
    a = _sigmoid(a0_ref[...] + _dot(lo, a2_ref[...]))
    g = _dot(_sigmoid(lo), g2_ref[...])
    ones_bd = _group_ones(LANES, RW_DH)
    kk = kr * kkw_ref[...]
    kk = kk / jnp.maximum(jnp.sqrt(_group_sum(kk * kk, ones_bd)), 1e-12)
    kh = kr * (1.0 + (a - 1.0) * kaw_ref[...])
    r_ref[...] = r
    d_ref[...] = jnp.exp(-jnp.exp(w_log))
    k_ref[...] = kh
    v_ref[...] = vr
    a_ref[...] = -kk
    b_ref[...] = kk * a
    g_ref[...] = g
    bonus_ref[...] = _group_sum(r * kh * rk_ref[...], ones_bd) * vr


def _rwkv_pre_prefill_kernel(pr_ref, shift0_ref, *rest, L):
    wrefs, outs, xbuf = rest[:9], rest[9:17], rest[17]
    c = pl.program_id(1)

    @pl.when(c == 0)
    def _():
        xbuf[SUBLANES - 1:SUBLANES, :] = shift0_ref[...]

    pr = pr_ref[...]
    xbuf[SUBLANES:SUBLANES + L, :] = pr
    pr_prev = xbuf[SUBLANES - 1:SUBLANES - 1 + L, :]
    xbuf[SUBLANES - 1:SUBLANES, :] = pr[L - 1:L, :]
    _rwkv_pre_body(pr, pr_prev, *wrefs, *outs)


def _rwkv_pre_decode_kernel(pr_ref, prev_ref, *rest):
    _rwkv_pre_body(pr_ref[...], prev_ref[...], *rest[:9], *rest[9:17])


def _rwkv_pre_weights(Wl):
    row2 = lambda a: a.reshape(1, D_HALF)
    padr = lambda w, o: jnp.pad(w, ((o, RW_LORA_PAD - o - w.shape[0]), (0, 0))).astype(BF16)
    mu = _pad_lanes(Wl['rw_mu'].reshape(1, RW_SHIFT_COLS), RW_COLS_PAD)
    ws = [mu, row2(Wl['rw_w0']), row2(Wl['rw_a0']),
          padr(Wl['rw_w2'], 0), padr(Wl['rw_a2'], RW_DECAY_LORA), padr(Wl['rw_g2'], RW_DECAY_LORA + RW_A_LORA),
          row2(Wl['rw_kk']), row2(Wl['rw_ka']), row2(Wl['rw_rk'])]
    specs = [_full((1, RW_COLS_PAD)), _full((1, D_HALF)), _full((1, D_HALF)),
             _full((RW_LORA_PAD, D_HALF)), _full((RW_LORA_PAD, D_HALF)), _full((RW_LORA_PAD, D_HALF)),
             _full((1, D_HALF)), _full((1, D_HALF)), _full((1, D_HALF))]
    return ws, specs


def _rwkv_pre_prefill(pr3, shift0, Wl):
    B, T, _ = pr3.shape
    L = CHUNK
    ws, wspecs = _rwkv_pre_weights(Wl)
    seq = pl.BlockSpec((None, L, D_HALF), lambda b, c: (b, c, 0))
    outs = pl.pallas_call(
        functools.partial(_rwkv_pre_prefill_kernel, L=L),
        grid=(B, T // L),
        in_specs=[pl.BlockSpec((None, L, RW_COLS_PAD), lambda b, c: (b, c, 0)),
                  pl.BlockSpec((None, 1, RW_COLS_PAD), lambda b, c: (b, 0, 0))] + wspecs,
        out_specs=[seq] * 8,
        out_shape=[jax.ShapeDtypeStruct((B, T, D_HALF), F32)] * 8,
        scratch_shapes=[pltpu.VMEM((L + SUBLANES, RW_COLS_PAD), F32)],
        compiler_params=_params(("parallel", "arbitrary")),
        name="rwkv_pre_prefill",
    )(pr3, _pad_lanes(shift0, RW_COLS_PAD).reshape(B, 1, RW_COLS_PAD), *ws)
    return outs


def _rwkv_pre_decode(pr, shift0, Wl):
    B = pr.shape[0]
    ws, wspecs = _rwkv_pre_weights(Wl)
    full2 = lambda n: pl.BlockSpec((B, n), lambda i: (0, 0))
    outs = pl.pallas_call(
        _rwkv_pre_decode_kernel,
        grid=(1,),
        in_specs=[full2(RW_COLS_PAD), full2(RW_COLS_PAD)] + wspecs,
        out_specs=[full2(D_HALF)] * 8,
        out_shape=[jax.ShapeDtypeStruct((B, D_HALF), F32)] * 8,
        compiler_params=_params(("arbitrary",)),
        name="rwkv_pre_decode",
    )(pr, _pad_lanes(shift0, RW_COLS_PAD), *ws)
    return outs


RW_IP = RW_DH // 2


def _rwkv_rec_kernel(r_ref, d_ref, k_ref, a_ref, b_ref, v_ref, S0_ref, y_ref, S_ref, *, Tc):
    @pl.when(pl.program_id(0) == 0)
    def _():
        S_ref[...] = S0_ref[...]

    lane = lax.broadcasted_iota(jnp.int32, (1, LANES), 1)

    def tiles(t):
        return tuple(jnp.where(lane < LANES // 2, ref[t], pltpu.roll(ref[t], LANES // 2, 1))
                     for ref in (a_ref, d_ref, b_ref, k_ref, r_ref))

    def step(t, carry):
        a, d, b, k, r = carry
        nxt = tiles(jnp.minimum(t + 1, Tc - 1))
        vt = v_ref[t]
        rows = []
        for ip in range(RW_IP):
            S = S_ref[ip]
            sa = jnp.sum(S * a, axis=0, keepdims=True)
            Sn = S * d + sa * b + vt[ip:ip + 1, :] * k
            S_ref[ip] = Sn
            rows.append(jnp.sum(Sn * r, axis=0, keepdims=True))
        y_ref[t] = jnp.concatenate(rows, axis=0)
        return nxt

    lax.fori_loop(0, Tc, step, tiles(0))


def _rwkv_rec_call(r, d, k, a, b, v, S0):
    T = r.shape[0]
    Tc = min(T, 32)
    vec = pl.BlockSpec((Tc, RW_DH, LANES), lambda c: (c, 0, 0))
    vsp = pl.BlockSpec((Tc, RW_IP, LANES), lambda c: (c, 0, 0))
    ssp = _full((RW_IP, RW_DH, LANES))
    return pl.pallas_call(
        functools.partial(_rwkv_rec_kernel, Tc=Tc),
        grid=(T // Tc,),
        in_specs=[vec] * 5 + [vsp, ssp],
        out_specs=[vsp, ssp],
        out_shape=[jax.ShapeDtypeStruct((T, RW_IP, LANES), F32),
                   jax.ShapeDtypeStruct((RW_IP, RW_DH, LANES), F32)],
        compiler_params=_params(("arbitrary",)),
        name="rwkv_recurrence",
    )(r, d, k, a, b, v, S0)


def _rwkv_dec_kernel(r_ref, d_ref, k_ref, a_ref, b_ref, v_ref, S0_ref, y_ref, S_ref):
    a, d, b, k, r = a_ref[...], d_ref[...], b_ref[...], k_ref[...], r_ref[...]
    v = v_ref[...]
    rows = []
    for i in range(RW_DH):
        S = S0_ref[i]
        sa = jnp.sum(S * a, axis=0, keepdims=True)
        Sn = S * d + sa * b + v[i:i + 1, :] * k
        S_ref[i] = Sn
        rows.append(jnp.sum(Sn * r, axis=0, keepdims=True))
    y_ref[...] = jnp.concatenate(rows, axis=0)


def _rwkv_decode_step(r, d, k, v, a, b, S0):
    B = r.shape[0]
    tr = lambda x: x.T.reshape(RW_HEADS, RW_DH, B)
    St = S0.reshape(B, RW_HEADS * RW_DH * RW_DH).T.reshape(RW_HEADS, RW_DH, RW_DH, B)
    vec = pl.BlockSpec((None, RW_DH, B), lambda h: (h, 0, 0))
    ssp = pl.BlockSpec((None, RW_DH, RW_DH, B), lambda h: (h, 0, 0, 0))
    y, S = pl.pallas_call(
        _rwkv_dec_kernel,
        grid=(RW_HEADS,),
        in_specs=[vec] * 6 + [ssp],
        out_specs=[vec, ssp],
        out_shape=[jax.ShapeDtypeStruct((RW_HEADS, RW_DH, B), F32),
                   jax.ShapeDtypeStruct((RW_HEADS, RW_DH, RW_DH, B), F32)],
        compiler_params=_params(("parallel",)),
        name="rwkv_decode_step",
    )(tr(r), tr(d), tr(k), tr(a), tr(b), tr(v), St)
    y = y.reshape(D_HALF, B).T
    S = S.reshape(RW_HEADS * RW_DH * RW_DH, B).T.reshape(B, RW_HEADS, RW_DH, RW_DH)
    return y, S


RW_TB = LANES // RW_BB
RW_NSB = 1


def _head_sum_rows(x):
    x3 = x.reshape(RW_HEADS, RW_DH, x.shape[-1])
    s = jnp.sum(x3, axis=1, keepdims=True)
    return jnp.broadcast_to(s, x3.shape).reshape(x.shape)


def _rwkv_pre_t_kernel(x_ref, shift0_ref, w_ref, mu_ref, w0_ref, a0_ref, w2_ref, a2_ref, g2_ref,
                       kkw_ref, kaw_ref, rk_ref,
                       r_ref, d_ref, k_ref, a_ref, b_ref, v_ref, g_ref, bonus_ref, last_ref, prev_scr):
    @pl.when(pl.program_id(0) == 0)
    def _():
        prev_scr[...] = shift0_ref[...]

    ro = lax.broadcasted_iota(jnp.int32, (LANES, LANES), 0)
    ci = lax.broadcasted_iota(jnp.int32, (LANES, LANES), 1)
    perm = jnp.where(ci == (ro % RW_BB) * RW_TB + ro // RW_BB, 1.0, 0.0).astype(BF16)
    lane = lax.broadcasted_iota(jnp.int32, (1, LANES), 1)
    grp = lane // RW_BB
    ngrp = LANES // RW_BB
    prs = {}

    def scatter(x, o_ref, t0, nrow, npiece):
        rot = [x[q * nrow:(q + 1) * nrow, :] if q == 0 else pltpu.roll(x[q * nrow:(q + 1) * nrow, :], q * RW_BB, 1)
               for q in range(npiece)]
        for t in range(RW_TB):
            m = rot[0]
            for q in range(1, npiece):
                m = jnp.where(grp == (t + q) % ngrp, rot[q], m)
            if t:
                m = pltpu.roll(m, LANES - t * RW_BB, 1)
            o_ref[t0 + t] = m

    def block(sb):
        t0 = sb * RW_TB
        xn = x_ref[:, t0:t0 + RW_TB, :].reshape(RW_BB * RW_TB, D_MODEL).astype(BF16)
        xg = jnp.dot(perm, xn, preferred_element_type=F32).astype(BF16)
        pr = lax.dot_general(w_ref[...], xg, (((1,), (1,)), ((), ())), preferred_element_type=F32)
        prs[sb] = pr
        yield
        rolled = pltpu.roll(pr, RW_BB, 1)
        before = prev_scr[...] if sb == 0 else prs[sb - 1]
        prev = jnp.where(lane < RW_BB, before, rolled)
        prs[sb] = rolled
        if sb == RW_NSB - 1:
            prev_scr[...] = rolled
            last_ref[...] = pr
        pm = pr + (prev - pr) * mu_ref[...]
        r = pm[0:D_HALF]
        kr = pm[D_HALF:2 * D_HALF]
        vr = pm[2 * D_HALF:3 * D_HALF]
        lo = pm[3 * D_HALF:RW_COLS_PAD]
        yield
        w_log = -_softplus(-(w0_ref[...] + _dot(w2_ref[...], jnp.tanh(lo)))) - 0.5
        a = _sigmoid(a0_ref[...] + _dot(a2_ref[...], lo))
        g = _dot(g2_ref[...], _sigmoid(lo))
        kk = kr * kkw_ref[...]
        kk = kk / jnp.maximum(jnp.sqrt(_head_sum_rows(kk * kk)), 1e-12)
        kh = kr * (1.0 + (a - 1.0) * kaw_ref[...])
        g_ref[sb] = g
        bonus_ref[sb] = _head_sum_rows(r * kh * rk_ref[...]) * vr
        yield
        scatter(r, r_ref, t0, RW_DH, RW_HEADS)
        yield
        scatter(jnp.exp(-jnp.exp(w_log)), d_ref, t0, RW_DH, RW_HEADS)
        yield
        scatter(kh, k_ref, t0, RW_DH, RW_HEADS)
        yield
        scatter(-kk, a_ref, t0, RW_DH, RW_HEADS)
        yield
        scatter(kk * a, b_ref, t0, RW_DH, RW_HEADS)
        yield
        vv = jnp.concatenate([vr[h * RW_DH + half * RW_IP:h * RW_DH + (half + 1) * RW_IP, :]
                              for half in range(2) for h in range(RW_HEADS)], axis=0)
        scatter(vv, v_ref, t0, RW_IP, ngrp)

    _skewed([block(sb) for sb in range(RW_NSB)])


def _lane_bcast(a, n):
    return jnp.broadcast_to(a.reshape(n, 1), (n, LANES))


def _rwkv_pre_t(x3, shift0, w_rwt, j, Wl):
    B, T, _ = x3.shape
    nblk = T // RW_TB
    padr = lambda w, o: jnp.pad(w, ((o, RW_LORA_PAD - o - w.shape[0]), (0, 0))).astype(BF16).T
    sh = jnp.pad(shift0.T, ((0, RW_COLS_PAD - RW_SHIFT_COLS), (0, LANES - RW_BB)))
    col = lambda a: _lane_bcast(a, D_HALF)
    ins = [x3, sh, w_rwt, _lane_bcast(_pad_lanes(Wl['rw_mu'].reshape(1, -1), RW_COLS_PAD), RW_COLS_PAD),
           col(Wl['rw_w0']), col(Wl['rw_a0']),
           padr(Wl['rw_w2'], 0), padr(Wl['rw_a2'], RW_DECAY_LORA), padr(Wl['rw_g2'], RW_DECAY_LORA + RW_A_LORA),
           col(Wl['rw_kk']), col(Wl['rw_ka']), col(Wl['rw_rk'])]
    tb = RW_TB * RW_NSB
    in_specs = [pl.BlockSpec((B, tb, D_MODEL), lambda c: (0, c, 0)), _full((RW_COLS_PAD, LANES)),
                pl.BlockSpec((None, RW_COLS_PAD, D_MODEL), lambda c: (j, 0, 0)), _full((RW_COLS_PAD, LANES)),
                _full((D_HALF, LANES)), _full((D_HALF, LANES)),
                _full((D_HALF, RW_LORA_PAD)), _full((D_HALF, RW_LORA_PAD)), _full((D_HALF, RW_LORA_PAD)),
                _full((D_HALF, LANES)), _full((D_HALF, LANES)), _full((D_HALF, LANES))]
    blk = pl.BlockSpec((RW_NSB, D_HALF, LANES), lambda c: (c, 0, 0))
    ktile = pl.BlockSpec((tb, RW_DH, LANES), lambda c: (c, 0, 0))
    vtile = pl.BlockSpec((tb, RW_IP, LANES), lambda c: (c, 0, 0))
    outs = pl.pallas_call(
        _rwkv_pre_t_kernel,
        grid=(T // tb,),
        in_specs=in_specs,
        out_specs=[ktile] * 5 + [vtile, blk, blk, _full((RW_COLS_PAD, LANES))],
        out_shape=[jax.ShapeDtypeStruct((T, RW_DH, LANES), F32)] * 5
                  + [jax.ShapeDtypeStruct((T, RW_IP, LANES), F32)]
                  + [jax.ShapeDtypeStruct((nblk, D_HALF, LANES), F32)] * 2
                  + [jax.ShapeDtypeStruct((RW_COLS_PAD, LANES), F32)],
        scratch_shapes=[pltpu.VMEM((RW_COLS_PAD, LANES), F32)],
        compiler_params=_params(("arbitrary",)),
        name="rwkv_pre_t",
    )(*ins)
    shift_new = outs[8][:RW_SHIFT_COLS, LANES - RW_BB:].T
    return outs[:8], shift_new


def _rwkv_post_t_kernel(y_ref, g_ref, bonus_ref, gng_ref, gnb_ref, yd_ref):
    lane = lax.broadcasted_iota(jnp.int32, (1, LANES), 1)
    grp = lane // RW_BB
    ngrp = LANES // RW_BB
    ys = [y_ref[t] for t in range(RW_TB)]
    rolled = []
    for s in range(ngrp):
        m = ys[s % RW_TB]
        for q in range(1, ngrp):
            m = jnp.where(grp == q, ys[(q + s) % RW_TB], m)
        rolled.append(pltpu.roll(m, s * RW_BB, 1) if s else m)
    pieces = {}
    for q in range(ngrp):
        m = rolled[(-q) % ngrp]
        for t in range(1, RW_TB):
            m = jnp.where(grp == t, rolled[(t - q) % ngrp], m)
        pieces[divmod(q, RW_HEADS)] = m
    y = jnp.concatenate([pieces[(half, h)] for h in range(RW_HEADS) for half in range(2)], axis=0)
    mu = _head_sum_rows(y) * (1.0 / RW_DH)
    yc = y - mu
    var = _head_sum_rows(yc * yc) * (1.0 / RW_DH)
    hn = yc * lax.rsqrt(var + RW_GN_EPS) * gng_ref[...] + gnb_ref[...]
    yd = ((hn + bonus_ref[...]) * g_ref[...]).T
    for t in range(RW_TB):
        yd_ref[:, t, :] = yd[t * RW_BB:(t + 1) * RW_BB, :]


def _rwkv_rec_t(vecs, S0, Wl, T):
    r, d, k, a, b, v, g, bonus = vecs
    nblk = T // RW_TB
    B = RW_BB
    Sr = S0.reshape(B, RW_HEADS, 2, RW_IP, RW_DH).transpose(3, 4, 2, 1, 0).reshape(RW_IP, RW_DH, LANES)
    y, S = _rwkv_rec_call(r, d, k, a, b, v, Sr)
    blk = pl.BlockSpec((None, D_HALF, LANES), lambda c: (c, 0, 0))
    col = lambda a_: _lane_bcast(a_, D_HALF)
    yd = pl.pallas_call(
        _rwkv_post_t_kernel,
        grid=(nblk,),
        in_specs=[pl.BlockSpec((RW_TB, RW_IP, LANES), lambda c: (c, 0, 0)), blk, blk,
                  _full((D_HALF, LANES)), _full((D_HALF, LANES))],
        out_specs=pl.BlockSpec((B, RW_TB, D_HALF), lambda c: (0, c, 0)),
        out_shape=jax.ShapeDtypeStruct((B, T, D_HALF), F32),
        compiler_params=_params(("parallel",)),
        name="rwkv_post_t",
    )(y, g, bonus, col(Wl['rw_gn_g']), col(Wl['rw_gn_b']))
    S = S.reshape(RW_IP, RW_DH, 2, RW_HEADS, B).transpose(4, 3, 2, 0, 1).reshape(B, RW_HEADS, RW_DH, RW_DH)
    return yd, S


def _rwkv_post_kernel(y_ref, g_ref, bonus_ref, gng_ref, gnb_ref, o_ref):
    ones_bd = _group_ones(LANES, RW_DH)
    y = y_ref[...]
    mu = _group_sum(y, ones_bd) * (1.0 / RW_DH)
    yc = y - mu
    var = _group_sum(yc * yc, ones_bd) * (1.0 / RW_DH)
    hn = yc * lax.rsqrt(var + RW_GN_EPS) * gng_ref[...] + gnb_ref[...]
    o_ref[...] = (hn + bonus_ref[...]) * g_ref[...]


def _rwkv_post(y2, g2, bonus2, Wl, tm):
    M = y2.shape[0]
    row = pl.BlockSpec((tm, D_HALF), lambda i: (i, 0))
    return pl.pallas_call(
        _rwkv_post_kernel,
        grid=(M // tm,),
        in_specs=[row, row, row, _full((1, D_HALF)), _full((1, D_HALF))],
        out_specs=row,
        out_shape=jax.ShapeDtypeStruct((M, D_HALF), F32),
        compiler_params=_params(("parallel",)),
        name="rwkv_post",
    )(y2, g2, bonus2, Wl['rw_gn_g'].reshape(1, D_HALF), Wl['rw_gn_b'].reshape(1, D_HALF))


def _trunk(x, st, pos, W):
    B, T, _ = x.shape
    M = B * T
    decode = T == 1
    tm_proj = min(M, 512)
    tm_post = min(M, 256)
    tm_rw = min(M, 512)
    x2 = x.reshape(M, D_MODEL)
    new = {name: [] for name in st}
    stacked = {}
    for l in range(DEPTH):
        j = l // 2
        Wl = {name: v[j] for name, v in W['per_pair'][l % 2].items()}
        stl = {name: v[j] for name, v in st.items()}
        if l % 2 == 0:
            p, = _proj(x2, W['ev_w_in'], j, tm_proj, (EVEN_IN,))
            if decode:
                ya, yb, (h, cb, S) = _even_decode(p, pos, stl, Wl, st['ret_S'], j, stacked.get('ret_S'))
                stacked['ret_S'] = S
            else:
                ya, yb, (h, cb, S) = _even_prefill(p.reshape(B, T, EVEN_IN), pos, stl, Wl)
                new['ret_S'].append(S)
            new['lru_h'].append(h)
            new['lru_conv'].append(cb)
            wout = W['ev_w_out']
        else:
            if decode:
                p_ml, p_rw = _proj(x2, W['od_w_in'], j, tm_proj, (2 * D_HALF, RW_COLS_PAD))
                ya, (C, n, m, cb) = _mlstm_decode(p_ml, stl, Wl, st['ml_C'], j, stacked.get('ml_C'))
                stacked['ml_C'] = C
                r, d, k, v, a, b, g, bonus = _rwkv_pre_decode(p_rw, stl['rw_shift'], Wl)
                y, S = _rwkv_decode_step(r, d, k, v, a, b, stl['rw_S'])
                yb = _rwkv_post(y, g, bonus, Wl, tm_rw)
                shift_new = p_rw[:, :RW_SHIFT_COLS]
            else:
                p_ml, = _proj(x2, W['od_w_in'], j, tm_proj, (2 * D_HALF,))
                ya, (C, n, m, cb) = _mlstm_prefill(p_ml.reshape(B, T, 2 * D_HALF), stl, Wl)
                vecs, shift_new = _rwkv_pre_t(x2.reshape(B, T, D_MODEL), stl['rw_shift'], W['od_w_rwt'], j, Wl)
                yb, S = _rwkv_rec_t(vecs, stl['rw_S'], Wl, T)
                new['ml_C'].append(C)
            new['ml_n'].append(n)
            new['ml_m'].append(m)
            new['ml_conv'].append(cb)
            new['rw_S'].append(S)
            new['rw_shift'].append(shift_new)
            wout = W['od_w_out']
        x2 = _post(ya.reshape(M, D_HALF), yb.reshape(M, D_HALF), x2, wout, j, l,
                   W['ln1_g'], W['ln1_b'], W['mlp_w1'], W['mlp_w2'], W['ln2_g'], W['ln2_b'], tm_post)
    out = {name: stacked[name] if name in stacked else jnp.stack(v) for name, v in new.items()}
    return x2.reshape(B, T, D_MODEL), out


def _prepare_weights(w):
    even_names = ('lru_conv_w', 'lru_conv_b', 'lru_ba', 'lru_bx', 'lru_lambda', 'ret_gn_g', 'ret_gn_b')
    odd_names = ('ml_conv_w', 'ml_conv_b', 'ml_w_gate', 'ml_b_gate', 'ml_gn_g', 'ml_gn_b', 'ml_skip',
                 'rw_mu', 'rw_w0', 'rw_w2', 'rw_a0', 'rw_a2', 'rw_g2', 'rw_kk', 'rw_ka', 'rw_rk', 'rw_gn_g', 'rw_gn_b')
    even = {n: w[n] for n in even_names}
    even.update(lru_wa_d=jax.vmap(_lru_dense)(w['lru_wa']), lru_wx_d=jax.vmap(_lru_dense)(w['lru_wx']))
    odd = {n: w[n] for n in odd_names}
    odd.update(ml_wq_d=jax.vmap(_ml_dense)(w['ml_wq']), ml_wk_d=jax.vmap(_ml_dense)(w['ml_wk']),
               ml_wv_d=jax.vmap(_ml_dense)(w['ml_wv']))
    od_w_in = _pad_lanes(w['od_w_in'], ODD_IN_PAD).astype(BF16)
    return dict(per_pair=(even, odd),
                ev_w_in=w['ev_w_in'].astype(BF16), ev_w_out=w['ev_w_out'].astype(BF16),
                od_w_in=od_w_in, od_w_rwt=jnp.swapaxes(od_w_in[:, :, 2 * D_HALF:], 1, 2),
                od_w_out=w['od_w_out'].astype(BF16),
                mlp_w1=w['mlp_w1'].astype(BF16), mlp_w2=w['mlp_w2'].astype(BF16),
                ln1_g=w['ln1_g'], ln1_b=w['ln1_b'], ln2_g=w['ln2_g'], ln2_b=w['ln2_b'])


def _zero_states(batch):
    z = lambda *s: jnp.zeros(s, F32)
    n_even, n_odd = (DEPTH + 1) // 2, DEPTH // 2
    return dict(lru_h=z(n_even, batch, D_HALF), lru_conv=z(n_even, batch, CONV_W - 1, D_HALF),
                ret_S=z(n_even, batch, RET_HEADS, RET_DH, RET_DH),
                ml_C=z(n_odd, batch, ML_HEADS, ML_DH, ML_DH), ml_n=z(n_odd, batch, ML_HEADS, ML_DH),
                ml_m=z(n_odd, batch, ML_HEADS), ml_conv=z(n_odd, batch, CONV_W - 1, D_HALF),
                rw_S=z(n_odd, batch, RW_HEADS, RW_DH, RW_DH), rw_shift=z(n_odd, batch, RW_SHIFT_COLS))


def kernel(x_prompt, x_sample, state_lru_h, state_lru_conv, state_ret, state_mlstm_C, state_mlstm_n, state_mlstm_m, state_mlstm_conv, state_rwkv_S, state_rwkv_shift, ln1_g, ln1_b, ln2_g, ln2_b, mlp_w1, mlp_w2, ev_w_in, ev_w_out, lru_conv_w, lru_conv_b, lru_wa, lru_ba, lru_wx, lru_bx, lru_lambda, ret_gn_g, ret_gn_b, od_w_in, od_w_out, ml_conv_w, ml_conv_b, ml_wq, ml_wk, ml_wv, ml_w_gate, ml_b_gate, ml_gn_g, ml_gn_b, ml_skip, rw_mu, rw_w0, rw_w2, rw_a0, rw_a2, rw_g2, rw_kk, rw_ka, rw_rk, rw_gn_g, rw_gn_b):
    W = _prepare_weights(dict(
        ln1_g=ln1_g, ln1_b=ln1_b, ln2_g=ln2_g, ln2_b=ln2_b, mlp_w1=mlp_w1, mlp_w2=mlp_w2,
        ev_w_in=ev_w_in, ev_w_out=ev_w_out, lru_conv_w=lru_conv_w, lru_conv_b=lru_conv_b,
        lru_wa=lru_wa, lru_ba=lru_ba, lru_wx=lru_wx, lru_bx=lru_bx, lru_lambda=lru_lambda,
        ret_gn_g=ret_gn_g, ret_gn_b=ret_gn_b, od_w_in=od_w_in, od_w_out=od_w_out,
        ml_conv_w=ml_conv_w, ml_conv_b=ml_conv_b, ml_wq=ml_wq, ml_wk=ml_wk, ml_wv=ml_wv,
        ml_w_gate=ml_w_gate, ml_b_gate=ml_b_gate, ml_gn_g=ml_gn_g, ml_gn_b=ml_gn_b, ml_skip=ml_skip,
        rw_mu=rw_mu, rw_w0=rw_w0, rw_w2=rw_w2, rw_a0=rw_a0, rw_a2=rw_a2, rw_g2=rw_g2,
        rw_kk=rw_kk, rw_ka=rw_ka, rw_rk=rw_rk, rw_gn_g=rw_gn_g, rw_gn_b=rw_gn_b))
    st_sample = dict(lru_h=state_lru_h, lru_conv=state_lru_conv, ret_S=state_ret,
                     ml_C=state_mlstm_C, ml_n=state_mlstm_n, ml_m=state_mlstm_m, ml_conv=state_mlstm_conv,
                     rw_S=state_rwkv_S, rw_shift=state_rwkv_shift)
    pos_prompt = jnp.arange(x_prompt.shape[1], dtype=jnp.int32)
    pos_sample = PAST_LEN + jnp.arange(x_sample.shape[1], dtype=jnp.int32)
    y_prompt, sp = _trunk(x_prompt, _zero_states(x_prompt.shape[0]), pos_prompt, W)
    y_sample, ss = _trunk(x_sample, st_sample, pos_sample, W)
    names = ('lru_h', 'lru_conv', 'ret_S', 'ml_C', 'ml_n', 'ml_m', 'ml_conv', 'rw_S', 'rw_shift')
    return (y_prompt, y_sample) + tuple(sp[n] for n in names) + tuple(ss[n] for n in names)
```

```python
import functools

import jax
import jax.numpy as jnp
from jax import lax
from jax.experimental import pallas as pl
from jax.experimental.pallas import tpu as pltpu

F32 = jnp.float32
BF16 = jnp.bfloat16

D_MODEL = 1024
DEPTH = 4
PAST_LEN = 16384
D_HALF = D_MODEL // 2
CONV_W = 4
LRU_BLOCKS = 8
LRU_BLOCK = D_HALF // LRU_BLOCKS
LRU_C = 8.0
RET_HEADS = 4
RET_DH = D_HALF // RET_HEADS
CHUNK = 128
ROPE_BASE = 10000.0
ML_HEADS = 4
ML_DH = D_HALF // ML_HEADS
ML_QKV_BLOCK = 4
ML_NBLK = D_HALF // ML_QKV_BLOCK
RW_HEADS = 8
RW_DH = D_HALF // RW_HEADS
RW_DECAY_LORA = 32
RW_A_LORA = 32
RW_GATE_LORA = 96
RW_LORA = RW_DECAY_LORA + RW_A_LORA + RW_GATE_LORA
RW_SHIFT_COLS = 3 * D_HALF + RW_LORA
D_FF = 4 * D_MODEL
ALPHA = (2.0 * DEPTH) ** 0.25
EVEN_IN = 6 * D_HALF
ODD_IN = 2 * D_HALF + RW_SHIFT_COLS
LN_EPS = 1e-5
GN_EPS = 1e-5
RW_GN_EPS = 64e-5

LANES = 128
SUBLANES = 8
RW_LORA_PAD = 2 * LANES
RW_COLS_PAD = 3 * D_HALF + RW_LORA_PAD
ODD_IN_PAD = 2 * D_HALF + RW_COLS_PAD
RW_BB = 8
VMEM_LIMIT = 56 * 1024 * 1024


def _params(sem):
    return pltpu.CompilerParams(dimension_semantics=sem, vmem_limit_bytes=VMEM_LIMIT)


def _dot(a, b):
    return jnp.dot(a.astype(BF16), b.astype(BF16), preferred_element_type=F32)


def _dot_nt(a, b):
    return lax.dot_general(a.astype(BF16), b.astype(BF16), (((1,), (1,)), ((), ())),
                           preferred_element_type=F32)


def _dot_tn(a, b):
    return lax.dot_general(a.astype(BF16), b.astype(BF16), (((0,), (0,)), ((), ())),
                           preferred_element_type=F32)


def _split3(a):
    hi = a.astype(BF16)
    r1 = a - hi.astype(F32)
    mid = r1.astype(BF16)
    lo = (r1 - mid.astype(F32)).astype(BF16)
    return hi, mid, lo


def _xdot(a, b01):
    hi, mid, lo = _split3(a)
    f = lambda t: jnp.dot(t, b01, preferred_element_type=F32)
    return f(hi) + f(mid) + f(lo)


def _xdot_l(b01, a):
    hi, mid, lo = _split3(a)
    f = lambda t: jnp.dot(b01, t, preferred_element_type=F32)
    return f(hi) + f(mid) + f(lo)


def _sigmoid(x):
    return 1.0 / (1.0 + jnp.exp(-x))


def _silu(x):
    return x * _sigmoid(x)


def _softplus(x):
    return jnp.maximum(x, 0.0) + jnp.log1p(jnp.exp(-jnp.abs(x)))


def _gelu_tanh(x):
    return 0.5 * x * (1.0 + jnp.tanh(0.7978845608028654 * (x + 0.044715 * (x * x * x))))


def _layer_norm(x, g, b, eps):
    mu = jnp.mean(x, -1, keepdims=True)
    xc = x - mu
    var = jnp.mean(xc * xc, -1, keepdims=True)
    return xc * lax.rsqrt(var + eps) * g + b


def _group_ones(n, group):
    r = lax.broadcasted_iota(jnp.int32, (n, n), 0) // group
    c = lax.broadcasted_iota(jnp.int32, (n, n), 1) // group
    return jnp.where(r == c, 1.0, 0.0).astype(BF16)


def _group_sum(x, ones_bd):
    parts = [_xdot(x[:, s * LANES:(s + 1) * LANES], ones_bd) for s in range(x.shape[1] // LANES)]
    return jnp.concatenate(parts, axis=-1)


def _rotate(x, cosf, sinf):
    return x * cosf + pltpu.roll(x, RET_DH // 2, 1) * sinf


def _proj_kernel(x_ref, w_ref, *o_refs, splits):
    xb = x_ref[...].astype(BF16)
    off = 0
    for o_ref, n in zip(o_refs, splits):
        o_ref[...] = jnp.dot(xb, w_ref[:, off:off + n], preferred_element_type=F32)
        off += n


def _proj(x2d, w_all, j, tm, splits):
    M = x2d.shape[0]
    N = w_all.shape[2]
    return pl.pallas_call(
        functools.partial(_proj_kernel, splits=splits),
        grid=(M // tm,),
        in_specs=[pl.BlockSpec((tm, D_MODEL), lambda i: (i, 0)),
                  pl.BlockSpec((None, D_MODEL, N), lambda i: (j, 0, 0))],
        out_specs=[pl.BlockSpec((tm, n), lambda i: (i, 0)) for n in splits],
        out_shape=[jax.ShapeDtypeStruct((M, n), F32) for n in splits],
        compiler_params=_params(("parallel",)),
        name="proj_in",
    )(x2d, w_all)


FF_CHUNK = 1024


def _post_kernel(ya_ref, yb_ref, x_ref, wo_ref, g1_ref, b1_ref, w1_ref, w2_ref, g2_ref, b2_ref, o_ref):
    y = (jnp.dot(ya_ref[...].astype(BF16), wo_ref[0:D_HALF, :], preferred_element_type=F32)
         + jnp.dot(yb_ref[...].astype(BF16), wo_ref[D_HALF:D_MODEL, :], preferred_element_type=F32))
    x1 = _layer_norm(ALPHA * x_ref[...] + y, g1_ref[...], b1_ref[...], LN_EPS)
    x1b = x1.astype(BF16)
    acc = jnp.zeros(x1.shape, F32)
    for c in range(D_FF // FF_CHUNK):
        h = jnp.dot(x1b, w1_ref[:, c * FF_CHUNK:(c + 1) * FF_CHUNK], preferred_element_type=F32)
        h = jnp.square(jnp.maximum(h, 0.0))
        acc = acc + jnp.dot(h.astype(BF16), w2_ref[c * FF_CHUNK:(c + 1) * FF_CHUNK, :],
                            preferred_element_type=F32)
    o_ref[...] = _layer_norm(ALPHA * x1 + acc, g2_ref[...], b2_ref[...], LN_EPS)


def _post(ya, yb, x2d, wout_all, j, l, ln1_g, ln1_b, w1_all, w2_all, ln2_g, ln2_b, tm):
    M = x2d.shape[0]
    row = lambda i: (i, 0)
    vec = pl.BlockSpec((None, 1, D_MODEL), lambda i: (l, 0, 0))
    r3 = lambda a: a.reshape(DEPTH, 1, D_MODEL)
    return pl.pallas_call(
        _post_kernel,
        grid=(M // tm,),
        in_specs=[pl.BlockSpec((tm, D_HALF), row), pl.BlockSpec((tm, D_HALF), row),
                  pl.BlockSpec((tm, D_MODEL), row),
                  pl.BlockSpec((None, D_MODEL, D_MODEL), lambda i: (j, 0, 0)),
                  vec, vec,
                  pl.BlockSpec((None, D_MODEL, D_FF), lambda i: (l, 0, 0)),
                  pl.BlockSpec((None, D_FF, D_MODEL), lambda i: (l, 0, 0)),
                  vec, vec],
        out_specs=pl.BlockSpec((tm, D_MODEL), row),
        out_shape=jax.ShapeDtypeStruct((M, D_MODEL), F32),
        compiler_params=_params(("parallel",)),
        name="post_mlp",
    )(ya, yb, x2d, wout_all, r3(ln1_g), r3(ln1_b), w1_all, w2_all, r3(ln2_g), r3(ln2_b))


def _conv_prefill(x, xbuf, cw_ref, cb_ref, L):
    xbuf[SUBLANES:SUBLANES + L, :] = x
    y = cb_ref[...]
    for i in range(CONV_W):
        y = y + cw_ref[i:i + 1, :] * xbuf[SUBLANES - (CONV_W - 1) + i:SUBLANES - (CONV_W - 1) + i + L, :]
    tail = xbuf[L + SUBLANES - (CONV_W - 1):L + SUBLANES, :]
    xbuf[SUBLANES - (CONV_W - 1):SUBLANES, :] = tail
    return y, tail


def _lru_gates(xc, wa_ref, ba_ref, wx_ref, bx_ref, lam_ref):
    xcb = xc.astype(BF16)
    nslab = D_HALF // LANES
    ra = jnp.concatenate([jnp.dot(xcb[:, s * LANES:(s + 1) * LANES], wa_ref[s], preferred_element_type=F32)
                          for s in range(nslab)], axis=-1)
    rx = jnp.concatenate([jnp.dot(xcb[:, s * LANES:(s + 1) * LANES], wx_ref[s], preferred_element_type=F32)
                          for s in range(nslab)], axis=-1)
    r = _sigmoid(ra + ba_ref[...])
    i = _sigmoid(rx + bx_ref[...])
    log_a = -LRU_C * r * _softplus(-lam_ref[...])
    a = jnp.exp(log_a)
    t = jnp.tanh(log_a)
    u = jnp.sqrt(-2.0 * t / (1.0 - t)) * (i * xc)
    return a, u


def _head_norm(o, g, b, eps):
    mu = jnp.mean(o, -1, keepdims=True)
    oc = o - mu
    var = jnp.mean(oc * oc, -1, keepdims=True)
    return oc * lax.rsqrt(var + eps) * g + b


EV_NB = 4


class _Cols:
    def __init__(self, ref, off):
        self.ref, self.off = ref, off

    def __getitem__(self, idx):
        if idx is Ellipsis:
            return self.ref[:, self.off:self.off + D_HALF]
        rows, cols = idx
        return self.ref[rows, self.off + cols.start:self.off + cols.stop]


def _even_prefill_kernel(x_ref, w_ref,
                         cw_ref, cb_ref, wa_ref, ba_ref, wx_ref, bx_ref, lam_ref,
                         cos_ref, sin_ref, dmask_ref, qd_ref, kd_ref, cd_ref, gng_ref, gnb_ref,
                         h0_ref, conv0_ref, S0_ref,
                         ya_ref, yb_ref, h_ref, conv_ref, S_ref, xbuf, pbuf, *, L):
    @pl.when(pl.program_id(1) == 0)
    def _():
        h_ref[...] = h0_ref[...]
        S_ref[...] = S0_ref[...]
        xbuf[:, SUBLANES - (CONV_W - 1):SUBLANES, :] = conv0_ref[...]

    rows = [_even_prefill_one(x_ref.at[bi], w_ref, pbuf.at[bi],
                              cw_ref, cb_ref, wa_ref, ba_ref, wx_ref, bx_ref, lam_ref,
                              cos_ref, sin_ref, dmask_ref, qd_ref, kd_ref, cd_ref, gng_ref, gnb_ref,
                              ya_ref.at[bi], yb_ref.at[bi], h_ref.at[bi], conv_ref.at[bi], S_ref.at[bi],
                              xbuf.at[bi], L=L) for bi in range(EV_NB)]
    _round_robin(rows)


def _even_prefill_one(x_ref, w_ref, p_ref,
                      cw_ref, cb_ref, wa_ref, ba_ref, wx_ref, bx_ref, lam_ref,
                      cos_ref, sin_ref, dmask_ref, qd_ref, kd_ref, cd_ref, gng_ref, gnb_ref,
                      ya_ref, yb_ref, h_ref, conv_ref, S_ref, xbuf, *, L):
    p_ref[...] = jnp.dot(x_ref[...].astype(BF16), w_ref[...], preferred_element_type=F32)
    xa_ref, ga_ref, q_ref, k_ref, v_ref, gb_ref = (_Cols(p_ref, i * D_HALF) for i in range(6))
    yield
    H = range(RET_HEADS)
    sls = [slice(hh * RET_DH, (hh + 1) * RET_DH) for hh in H]
    cosf = cos_ref[...]
    sinf = sin_ref[...]
    qh = [_rotate(q_ref[:, sl], cosf, sinf) for sl in sls]
    kh = [_rotate(k_ref[:, sl], cosf, sinf) * (RET_DH ** -0.5) for sl in sls]
    vh = [v_ref[:, sl].astype(BF16) for sl in sls]
    qk = [_dot_nt(qh[hh], kh[hh]) for hh in H]
    S = [S_ref[hh] for hh in H]
    qS = [_dot(qh[hh] * qd_ref[:, sls[hh]], S[hh]) for hh in H]
    kv = [_dot_tn(kh[hh] * kd_ref[:, sls[hh]], vh[hh]) for hh in H]
    yield
    xc, tail = _conv_prefill(xa_ref[...], xbuf, cw_ref, cb_ref, L)
    conv_ref[...] = tail
    a, u = _lru_gates(xc, wa_ref, ba_ref, wx_ref, bx_ref, lam_ref)
    yield
    for hh in H:
        S_ref[hh] = S[hh] * cd_ref[hh] + kv[hh]
    sc = [(qk[hh] * dmask_ref[hh]).astype(BF16) for hh in H]
    o = [jnp.dot(sc[hh], vh[hh], preferred_element_type=F32) + qS[hh] for hh in H]
    yield
    row = lax.broadcasted_iota(jnp.int32, (L, D_HALF), 0)
    s = 1
    while s < L:
        keep = row >= s
        a_sh = jnp.where(keep, pltpu.roll(a, s, 0), 1.0)
        u_sh = jnp.where(keep, pltpu.roll(u, s, 0), 0.0)
        u = a * u_sh + u
        a = a * a_sh
        s *= 2
        yield
    h = a * h_ref[...] + u
    h_ref[...] = h[L - 1:L, :]
    ya_ref[...] = _gelu_tanh(ga_ref[...]) * h
    ones = jnp.ones((RET_DH, LANES), BF16)
    mu = [jnp.dot(o[hh].astype(BF16), ones, preferred_element_type=F32) * (1.0 / RET_DH) for hh in H]
    yield
    oc = [o[hh] - mu[hh] for hh in H]
    var = [jnp.dot((oc[hh] * oc[hh]).astype(BF16), ones, preferred_element_type=F32) * (1.0 / RET_DH) for hh in H]
    yield
    for hh in H:
        sl = sls[hh]
        on = oc[hh] * lax.rsqrt(var[hh] + GN_EPS) * gng_ref[:, sl] + gnb_ref[:, sl]
        yb_ref[:, sl] = _silu(gb_ref[:, sl]) * on


def _ret_tables(L):
    log_gamma = jnp.log1p(-jnp.exp2(-5.0 - jnp.arange(RET_HEADS, dtype=F32)))
    idx = jnp.arange(L, dtype=F32)
    diff = idx[:, None] - idx[None, :]
    dmask = jnp.where(diff >= 0, jnp.exp(log_gamma[:, None, None] * jnp.maximum(diff, 0.0)), 0.0)
    qd = jnp.exp(log_gamma[:, None] * (idx + 1.0))
    kd = jnp.exp(log_gamma[:, None] * (L - 1.0 - idx))
    cd = jnp.exp(log_gamma * L)
    qd_full = jnp.repeat(qd.T, RET_DH, axis=1)
    kd_full = jnp.repeat(kd.T, RET_DH, axis=1)
    cd_full = jnp.broadcast_to(cd[:, None, None], (RET_HEADS, 1, RET_DH))
    return dmask, qd_full, kd_full, cd_full


def _rope_tables(pos):
    half = RET_DH // 2
    inv = ROPE_BASE ** (-jnp.arange(half, dtype=F32) / half)
    ang = pos.astype(F32)[:, None] * inv[None, :]
    cos, sin = jnp.cos(ang), jnp.sin(ang)
    return jnp.concatenate([cos, cos], -1), jnp.concatenate([-sin, sin], -1)


def _lru_dense(w):
    pairs = LANES // LRU_BLOCK
    w4 = w.reshape(LRU_BLOCKS // pairs, pairs, LRU_BLOCK, LRU_BLOCK)
    eye = jnp.eye(pairs, dtype=w.dtype)
    d = w4[:, :, :, None, :] * eye[None, :, None, :, None]
    return d.reshape(LRU_BLOCKS // pairs, LANES, LANES).astype(BF16)


def _full(shape):
    n = len(shape)
    return pl.BlockSpec(shape, lambda *_: (0,) * n)


def _even_prefill(x3, w_in_all, j, pos, st, Wl):
    B, T, _ = x3.shape
    L = CHUNK
    nc = T // L
    dmask, qd, kd, cd = _ret_tables(L)
    cosf, sinf = _rope_tables(pos)
    nb = EV_NB
    perb = lambda *s: pl.BlockSpec((nb,) + s, lambda b, c: (b,) + (0,) * len(s))
    row2 = lambda a: a.reshape(1, D_HALF)
    ins = [x3, w_in_all, Wl['lru_conv_w'], row2(Wl['lru_conv_b']), Wl['lru_wa_d'], row2(Wl['lru_ba']),
           Wl['lru_wx_d'], row2(Wl['lru_bx']), row2(Wl['lru_lambda']),
           cosf, sinf, dmask, qd, kd, cd, row2(Wl['ret_gn_g']), row2(Wl['ret_gn_b']),
           st['lru_h'].reshape(B, 1, D_HALF), st['lru_conv'], st['ret_S']]
    in_specs = [pl.BlockSpec((nb, L, D_MODEL), lambda b, c: (b, c, 0)),
                pl.BlockSpec((None, D_MODEL, EVEN_IN), lambda b, c: (j, 0, 0))] + [
        _full((CONV_W, D_HALF)), _full((1, D_HALF)), _full((4, LANES, LANES)), _full((1, D_HALF)),
        _full((4, LANES, LANES)), _full((1, D_HALF)), _full((1, D_HALF)),
        pl.BlockSpec((L, RET_DH), lambda b, c: (c, 0)), pl.BlockSpec((L, RET_DH), lambda b, c: (c, 0)),
        _full((RET_HEADS, L, L)), _full((L, D_HALF)), _full((L, D_HALF)), _full((RET_HEADS, 1, RET_DH)),
        _full((1, D_HALF)), _full((1, D_HALF)),
        perb(1, D_HALF), perb(CONV_W - 1, D_HALF), perb(RET_HEADS, RET_DH, RET_DH)]
    seq = pl.BlockSpec((nb, L, D_HALF), lambda b, c: (b, c, 0))
    ya, yb, h, conv, S = pl.pallas_call(
        functools.partial(_even_prefill_kernel, L=L),
        grid=(B // nb, nc),
        in_specs=in_specs,
        out_specs=[seq, seq, perb(1, D_HALF), perb(CONV_W - 1, D_HALF), perb(RET_HEADS, RET_DH, RET_DH)],
        out_shape=[jax.ShapeDtypeStruct((B, T, D_HALF), F32), jax.ShapeDtypeStruct((B, T, D_HALF), F32),
                   jax.ShapeDtypeStruct((B, 1, D_HALF), F32),
                   jax.ShapeDtypeStruct((B, CONV_W - 1, D_HALF), F32),
                   jax.ShapeDtypeStruct((B, RET_HEADS, RET_DH, RET_DH), F32)],
        scratch_shapes=[pltpu.VMEM((nb, L + SUBLANES, D_HALF), F32), pltpu.VMEM((nb, L, EVEN_IN), F32)],
        compiler_params=_params(("parallel", "arbitrary")),
        name="even_prefill",
    )(*ins)
    return ya, yb, (h.reshape(B, D_HALF), conv, S)


DEC_BB = 8


def _even_decode_kernel(p_ref, cw_ref, cb_ref, wa_ref, ba_ref, wx_ref, bx_ref, lam_ref,
                        cos_ref, sin_ref, dm_ref, qd_ref, kd_ref, cd_ref, gng_ref, gnb_ref,
                        h0_ref, conv0_ref, S0_ref,
                        ya_ref, yb_ref, h_ref, conv_ref, S_ref):
    col = lambda i: p_ref[:, i * D_HALF:(i + 1) * D_HALF]
    xa = col(0)
    xc = cb_ref[...] + cw_ref[CONV_W - 1:CONV_W, :] * xa
    for i in range(CONV_W - 1):
        xc = xc + cw_ref[i:i + 1, :] * conv0_ref[i]
    for i in range(CONV_W - 2):
        conv_ref[i] = conv0_ref[i + 1]
    conv_ref[CONV_W - 2] = xa
    a, u = _lru_gates(xc, wa_ref, ba_ref, wx_ref, bx_ref, lam_ref)
    h = a * h0_ref[...] + u
    h_ref[...] = h
    ya_ref[...] = _gelu_tanh(col(1)) * h

    cosf = cos_ref[...]
    sinf = sin_ref[...]
    row8 = lax.broadcasted_iota(jnp.int32, (SUBLANES, RET_DH), 0)
    q, k, v, gb = col(2), col(3), col(4), col(5)
    for hh in range(RET_HEADS):
        sl = slice(hh * RET_DH, (hh + 1) * RET_DH)
        qh = _rotate(q[:, sl], cosf, sinf)
        kh = _rotate(k[:, sl], cosf, sinf) * (RET_DH ** -0.5)
        vh = v[:, sl]
        qk = jnp.sum(qh * kh, -1, keepdims=True) * dm_ref[:, sl]
        qq = qh * qd_ref[:, sl]
        kk = kh * kd_ref[:, sl]
        rows = []
        for bi in range(DEC_BB):
            S = S0_ref[bi, hh]
            q8 = jnp.broadcast_to(qq[bi:bi + 1, :], (SUBLANES, RET_DH))
            rows.append(_dot_f32(q8, S)[0:1, :])
            k8 = jnp.where(row8 == 0, jnp.broadcast_to(kk[bi:bi + 1, :], (SUBLANES, RET_DH)), 0.0)
            v8 = jnp.broadcast_to(vh[bi:bi + 1, :], (SUBLANES, RET_DH))
            S_ref[bi, hh] = S * cd_ref[hh] + _dot_tn_f32(k8, v8)
        o = qk * vh + jnp.concatenate(rows, axis=0)
        on = _head_norm(o, gng_ref[:, sl], gnb_ref[:, sl], GN_EPS)
        yb_ref[:, sl] = _silu(gb[:, sl]) * on


def _dot_f32(a, b):
    return jnp.dot(a, b, preferred_element_type=F32)


def _dot_tn_f32(a, b):
    return lax.dot_general(a, b, (((0,), (0,)), ((), ())), preferred_element_type=F32)


def _skip_ref(kernel, pos):
    def wrapped(*refs):
        kernel(*refs[:pos], *refs[pos + 1:])
    return wrapped


def _all_layers(kernel, pos_in, pos_out, j, n_layers):
    def wrapped(*refs):
        refs = list(refs)
        s_in, s_out = refs[pos_in], refs[pos_out]
        for other in range(n_layers):
            if other != j:
                s_out[other] = s_in[other]
        refs[pos_in], refs[pos_out] = s_in.at[j], s_out.at[j]
        kernel(*refs)
    return wrapped


def _stacked_state_io(kernel, S_all, S_prev, j, bb, pos_in, n_in, out_idx):
    n_layers, tail = S_all.shape[0], S_all.shape[2:]
    zeros = (0,) * len(tail)
    shape = jax.ShapeDtypeStruct(S_all.shape, F32)
    if S_prev is None:
        spec = pl.BlockSpec((n_layers, bb) + tail, lambda i: (0, i) + zeros)
        return _all_layers(kernel, pos_in, n_in + out_idx, j, n_layers), spec, [], [], {}, shape
    spec = pl.BlockSpec((None, bb) + tail, lambda i: (j, i) + zeros)
    return _skip_ref(kernel, n_in), spec, [S_prev], [pl.BlockSpec(memory_space=pl.ANY)], {n_in: out_idx}, shape


def _even_decode(p2, pos, st, Wl, S_all, j, S_prev):
    B = p2.shape[0]
    bb = DEC_BB
    dmask, qd, kd, cd = _ret_tables(1)
    dm = jnp.repeat(dmask[:, 0, :].T, RET_DH, axis=1)
    cosf, sinf = _rope_tables(pos)
    row2 = lambda a: a.reshape(1, D_HALF)
    rows = lambda n: pl.BlockSpec((bb, n), lambda i: (i, 0))
    convs = pl.BlockSpec((CONV_W - 1, bb, D_HALF), lambda i: (0, i, 0))
    ins = [p2, Wl['lru_conv_w'], row2(Wl['lru_conv_b']), Wl['lru_wa_d'], row2(Wl['lru_ba']),
           Wl['lru_wx_d'], row2(Wl['lru_bx']), row2(Wl['lru_lambda']),
           cosf, sinf, dm, qd, kd, cd, row2(Wl['ret_gn_g']), row2(Wl['ret_gn_b']),
           st['lru_h'], jnp.swapaxes(st['lru_conv'], 0, 1), S_all]
    kern, Ss, extra_in, extra_specs, aliases, S_shape = _stacked_state_io(
        _even_decode_kernel, S_all, S_prev, j, bb, len(ins) - 1, len(ins), 4)
    in_specs = [rows(EVEN_IN), _full((CONV_W, D_HALF)), _full((1, D_HALF)), _full((4, LANES, LANES)),
                _full((1, D_HALF)), _full((4, LANES, LANES)), _full((1, D_HALF)), _full((1, D_HALF)),
                _full((1, RET_DH)), _full((1, RET_DH)), _full((1, D_HALF)), _full((1, D_HALF)),
                _full((1, D_HALF)), _full((RET_HEADS, 1, RET_DH)), _full((1, D_HALF)), _full((1, D_HALF)),
                rows(D_HALF), convs, Ss] + extra_specs
    ya, yb, h, conv, S = pl.pallas_call(
        kern,
        grid=(B // bb,),
        in_specs=in_specs,
        out_specs=[rows(D_HALF), rows(D_HALF), rows(D_HALF), convs, Ss],
        out_shape=[jax.ShapeDtypeStruct((B, D_HALF), F32), jax.ShapeDtypeStruct((B, D_HALF), F32),
                   jax.ShapeDtypeStruct((B, D_HALF), F32),
                   jax.ShapeDtypeStruct((CONV_W - 1, B, D_HALF), F32), S_shape],
        input_output_aliases=aliases,
        compiler_params=_params(("parallel",)),
        name="even_decode",
    )(*ins, *extra_in)
    return ya, yb, (h, jnp.swapaxes(conv, 0, 1), S)


def _mlstm_qkv_gates(xm, xc, wq_ref, wk_ref, wv_ref, wg_ref, bg_ref):
    q = _dot(xc, wq_ref[...])
    k = _dot(xc, wk_ref[...])
    v = _dot(xm, wv_ref[...])
    g_col = (_dot(q, wg_ref[0:D_HALF, :]) + _dot(k, wg_ref[D_HALF:2 * D_HALF, :])
             + _dot(v, wg_ref[2 * D_HALF:3 * D_HALF, :]) + bg_ref[...])
    return q, k, v, g_col


ML_NB = 4


def _skewed(gens):
    live = []
    pending = list(gens)
    while pending or live:
        if pending:
            live.append(pending.pop(0))
        nxt = []
        for g in live:
            try:
                next(g)
                nxt.append(g)
            except StopIteration:
                pass
        live = nxt


def _round_robin(gens):
    gens = list(gens)
    while gens:
        alive = []
        for g in gens:
            try:
                next(g)
                alive.append(g)
            except StopIteration:
                pass
        gens = alive


def _mlstm_prefill_kernel(x_ref, w_ref, cw_ref, cb_ref, wq_ref, wk_ref, wv_ref, wg_ref, bg_ref,
                          wgt_ref, bgt_ref, gng_ref, gnb_ref, skip_ref,
                          conv0_ref, C0_ref, n0_ref, m0_ref,
                          yc_ref, conv_ref, C_ref, n_ref, m_ref, xbuf, ncol, pbuf, *, L):
    @pl.when(pl.program_id(1) == 0)
    def _():
        ones = jnp.ones((ML_DH, LANES), BF16)
        eye = jnp.where(lax.broadcasted_iota(jnp.int32, (ML_DH, LANES), 0)
                        == lax.broadcasted_iota(jnp.int32, (ML_DH, LANES), 1), 1.0, 0.0)
        C_ref[...] = C0_ref[...]
        m_ref[...] = m0_ref[...]
        for bi in range(ML_NB):
            xbuf[bi, SUBLANES - (CONV_W - 1):SUBLANES, :] = conv0_ref[bi]
            for hh in range(ML_HEADS):
                ncol[bi, hh] = _xdot(eye * n0_ref[bi, :, hh * ML_DH:(hh + 1) * ML_DH], ones)

    rows = [_mlstm_prefill_one(x_ref.at[bi], w_ref, pbuf.at[bi], cw_ref, cb_ref, wq_ref, wk_ref, wv_ref, wg_ref,
                               bg_ref, wgt_ref, bgt_ref, gng_ref, gnb_ref, skip_ref,
                               yc_ref.at[bi], conv_ref.at[bi], C_ref.at[bi], n_ref.at[bi], m_ref.at[bi],
                               xbuf.at[bi], ncol.at[bi], L=L) for bi in range(ML_NB)]
    _round_robin(rows)


def _mlstm_prefill_one(x_ref, w_ref, p_ref, cw_ref, cb_ref, wq_ref, wk_ref, wv_ref, wg_ref, bg_ref,
                       wgt_ref, bgt_ref, gng_ref, gnb_ref, skip_ref,
                       yc_ref, conv_ref, C_ref, n_ref, m_ref, xbuf, ncol, *, L):
    p_ref[...] = jnp.dot(x_ref[...].astype(BF16), w_ref[...], preferred_element_type=F32)
    xm_ref, z_ref = _Cols(p_ref, 0), _Cols(p_ref, D_HALF)
    yield
    xm = xm_ref[...]
    xc, tail = _conv_prefill(xm, xbuf, cw_ref, cb_ref, L)
    conv_ref[...] = tail
    xc = _silu(xc)
    yield
    q, k, v, g_col = _mlstm_qkv_gates(xm, xc, wq_ref, wk_ref, wv_ref, wg_ref, bg_ref)
    g_row = (_dot_nt(wgt_ref[:, 0:D_HALF], q) + _dot_nt(wgt_ref[:, D_HALF:2 * D_HALF], k)
             + _dot_nt(wgt_ref[:, 2 * D_HALF:3 * D_HALF], v) + bgt_ref[...])
    yield
    ri = lax.broadcasted_iota(jnp.int32, (L, L), 0)
    ci = lax.broadcasted_iota(jnp.int32, (L, L), 1)
    causal = ri >= ci
    tril = jnp.where(causal, 1.0, 0.0).astype(BF16)
    triu = jnp.where(ci >= ri, 1.0, 0.0).astype(BF16)
    ones = jnp.ones((L, LANES), BF16)
    eye = jnp.where(ri == ci, 1.0, 0.0)

    li_col = g_col
    lf_col = -_softplus(-pltpu.roll(g_col, LANES - ML_HEADS, 1))
    b_col = _xdot_l(tril, lf_col)
    lf_row = -_softplus(-g_row)
    b_row = _xdot(lf_row, triu)
    yield
    c_row = g_row[0:ML_HEADS, :] - b_row[ML_HEADS:2 * ML_HEADS, :]
    row = lax.broadcasted_iota(jnp.int32, (L, LANES), 0)
    pm = li_col - b_col
    sft = 1
    while sft < L:
        pm = jnp.maximum(pm, jnp.where(row >= sft, pltpu.roll(pm, sft, 0), -jnp.inf))
        sft *= 2
    m_prev = m_ref[...]
    u_col = -jnp.maximum(pm, m_prev)
    m_t_col = b_col - u_col
    e_col = jnp.exp(-m_t_col)
    m_new = m_t_col[L - 1:L, :]
    b_last = b_col[L - 1:L, :]
    wk_col = jnp.exp(b_last - b_col + li_col - m_new)
    wC_row = jnp.exp(b_last + m_prev - m_new)
    m_ref[...] = m_new
    yield
    vones = ones
    rep = lambda col, hh: jnp.broadcast_to(col[:, hh:hh + 1], (L, LANES))
    H = range(ML_HEADS)
    sls = [slice(hh * ML_DH, (hh + 1) * ML_DH) for hh in H]
    qh = [q[:, sl].astype(BF16) for sl in sls]
    kh = [k[:, sl] * (ML_DH ** -0.5) for sl in sls]
    vh1 = [jnp.concatenate([v[:, sl].astype(BF16), vones], axis=-1) for sl in sls]
    qk = [_dot_nt(qh[hh], kh[hh]) for hh in H]
    CN = [jnp.concatenate([C_ref[hh], ncol[hh]], axis=-1) for hh in H]
    qc = [jnp.dot(qh[hh], CN[hh].astype(BF16), preferred_element_type=F32) for hh in H]
    kw = [kh[hh] * rep(wk_col, hh) for hh in H]
    upd = [lax.dot_general(kw[hh].astype(BF16), vh1[hh], (((0,), (0,)), ((), ())), preferred_element_type=F32)
           for hh in H]
    yield
    for hh in H:
        w_C =jnp.broadcast_to(wC_row[:, hh:hh + 1], (ML_DH, 2 * ML_DH))
        CNn = w_C * CN[hh] + upd[hh]
        C_ref[hh] = CNn[:, 0:ML_DH]
        ncol[hh] = CNn[:, ML_DH:2 * ML_DH]
        n_ref[:, sls[hh]] = jnp.sum(CNn[:, ML_DH:2 * ML_DH] * eye, axis=0, keepdims=True)
    yield
    u = [rep(u_col, hh) for hh in H]
    s = [(qk[hh] * jnp.exp(jnp.where(causal, u[hh] + c_row[hh:hh + 1, :], -jnp.inf))).astype(BF16) for hh in H]
    sv = [jnp.dot(s[hh], vh1[hh], preferred_element_type=F32) for hh in H]
    yield
    hcell = []
    for hh in H:
        w_inter = jnp.exp(rep(m_prev, hh) + u[hh])
        num = sv[hh][:, 0:ML_DH] + w_inter * qc[hh][:, 0:ML_DH]
        den = sv[hh][:, ML_DH:2 * ML_DH] + w_inter * qc[hh][:, ML_DH:2 * ML_DH]
        hcell.append(num / jnp.maximum(jnp.abs(den), rep(e_col, hh)))
    mu = [jnp.dot(hcell[hh].astype(BF16), ones, preferred_element_type=F32) * (1.0 / ML_DH) for hh in H]
    yield
    oc = [hcell[hh] - mu[hh] for hh in H]
    var = [jnp.dot((oc[hh] * oc[hh]).astype(BF16), ones, preferred_element_type=F32) * (1.0 / ML_DH) for hh in H]
    for hh in H:
        sl = sls[hh]
        hn = oc[hh] * lax.rsqrt(var[hh] + GN_EPS) * gng_ref[:, sl] + gnb_ref[:, sl]
        yc_ref[:, sl] = (hn + skip_ref[:, sl] * xc[:, sl]) * _silu(z_ref[:, sl])


def _ml_dense(w):
    w2 = w.reshape(D_HALF, ML_QKV_BLOCK)
    c = jnp.arange(D_HALF)
    spread = (c[None, :] % ML_QKV_BLOCK == jnp.arange(ML_QKV_BLOCK)[:, None]).astype(w.dtype)
    full = jnp.dot(w2, spread, precision=lax.Precision.HIGHEST)
    same_block = c[:, None] // ML_QKV_BLOCK == c[None, :] // ML_QKV_BLOCK
    return jnp.where(same_block, full, 0.0).astype(BF16)


def _pad_lanes(a, n=LANES):
    return jnp.pad(a, [(0, 0)] * (a.ndim - 1) + [(0, n - a.shape[-1])])


def _mlstm_weights(Wl):
    wg = _pad_lanes(Wl['ml_w_gate']).astype(BF16)
    bg = _pad_lanes(Wl['ml_b_gate'].reshape(1, 2 * ML_HEADS))
    wgt = Wl['ml_w_gate'].T.astype(BF16)
    bgt = jnp.broadcast_to(Wl['ml_b_gate'].reshape(2 * ML_HEADS, 1), (2 * ML_HEADS, LANES))
    return wg, bg, wgt, bgt


def _mlstm_prefill(x3, w_in_all, j, st, Wl):
    B, T, _ = x3.shape
    L = CHUNK
    nc = T // L
    wg, bg, wgt, bgt = _mlstm_weights(Wl)
    nb = ML_NB
    perb = lambda *s: pl.BlockSpec((nb,) + s, lambda b, c: (b,) + (0,) * len(s))
    row2 = lambda a: a.reshape(1, D_HALF)
    ins = [x3, w_in_all, Wl['ml_conv_w'], row2(Wl['ml_conv_b']), Wl['ml_wq_d'], Wl['ml_wk_d'], Wl['ml_wv_d'],
           wg, bg, wgt, bgt, row2(Wl['ml_gn_g']), row2(Wl['ml_gn_b']), row2(Wl['ml_skip']),
           st['ml_conv'], st['ml_C'], st['ml_n'].reshape(B, 1, D_HALF),
           _pad_lanes(st['ml_m']).reshape(B, 1, LANES)]
    in_specs = [pl.BlockSpec((nb, L, D_MODEL), lambda b, c: (b, c, 0)),
                pl.BlockSpec((None, D_MODEL, 2 * D_HALF), lambda b, c: (j, 0, 0)),
                _full((CONV_W, D_HALF)), _full((1, D_HALF)),
                _full((D_HALF, D_HALF)), _full((D_HALF, D_HALF)), _full((D_HALF, D_HALF)),
                _full((3 * D_HALF, LANES)), _full((1, LANES)), _full((2 * ML_HEADS, 3 * D_HALF)),
                _full((2 * ML_HEADS, LANES)), _full((1, D_HALF)), _full((1, D_HALF)), _full((1, D_HALF)),
                perb(CONV_W - 1, D_HALF), perb(ML_HEADS, ML_DH, ML_DH), perb(1, D_HALF), perb(1, LANES)]
    seq = pl.BlockSpec((nb, L, D_HALF), lambda b, c: (b, c, 0))
    yc, conv, C, n, m = pl.pallas_call(
        functools.partial(_mlstm_prefill_kernel, L=L),
        grid=(B // nb, nc),
        in_specs=in_specs,
        out_specs=[seq, perb(CONV_W - 1, D_HALF), perb(ML_HEADS, ML_DH, ML_DH), perb(1, D_HALF), perb(1, LANES)],
        out_shape=[jax.ShapeDtypeStruct((B, T, D_HALF), F32),
                   jax.ShapeDtypeStruct((B, CONV_W - 1, D_HALF), F32),
                   jax.ShapeDtypeStruct((B, ML_HEADS, ML_DH, ML_DH), F32),
                   jax.ShapeDtypeStruct((B, 1, D_HALF), F32),
                   jax.ShapeDtypeStruct((B, 1, LANES), F32)],
        scratch_shapes=[pltpu.VMEM((nb, L + SUBLANES, D_HALF), F32),
                        pltpu.VMEM((nb, ML_HEADS, ML_DH, LANES), F32),
                        pltpu.VMEM((nb, L, 2 * D_HALF), F32)],
        compiler_params=_params(("parallel", "arbitrary")),
        name="mlstm_prefill",
    )(*ins)
    return yc, (C, n.reshape(B, ML_HEADS, ML_DH), m[:, 0, :ML_HEADS], conv)


def _mlstm_decode_kernel(p_ref, cw_ref, cb_ref, wq_ref, wk_ref, wv_ref, wg_ref, bg_ref,
                         gng_ref, gnb_ref, skip_ref, conv0_ref, C0_ref, n0_ref, m0_ref,
                         yc_ref, conv_ref, C_ref, n_ref, m_ref):
    xm = p_ref[:, 0:D_HALF]
    z = p_ref[:, D_HALF:2 * D_HALF]
    xc = cb_ref[...] + cw_ref[CONV_W - 1:CONV_W, :] * xm
    for i in range(CONV_W - 1):
        xc = xc + cw_ref[i:i + 1, :] * conv0_ref[i]
    for i in range(CONV_W - 2):
        conv_ref[i] = conv0_ref[i + 1]
    conv_ref[CONV_W - 2] = xm
    xc = _silu(xc)
    q, k, v, g = _mlstm_qkv_gates(xm, xc, wq_ref, wk_ref, wv_ref, wg_ref, bg_ref)
    lf_all = -_softplus(-g)
    lane = lax.broadcasted_iota(jnp.int32, (1, LANES), 1)
    row8 = lax.broadcasted_iota(jnp.int32, (SUBLANES, ML_DH), 0)
    m_all = m0_ref[...]
    m_out = m_all
    for hh in range(ML_HEADS):
        sl = slice(hh * ML_DH, (hh + 1) * ML_DH)
        qh, vh = q[:, sl], v[:, sl]
        kh = k[:, sl] * (ML_DH ** -0.5)
        li = g[:, hh:hh + 1]
        lf = lf_all[:, ML_HEADS + hh:ML_HEADS + hh + 1]
        m_prev = m_all[:, hh:hh + 1]
        n = n0_ref[:, sl]
        log_inter = lf + m_prev
        m_t = jnp.maximum(li, log_inter)
        s = jnp.sum(qh * kh, -1, keepdims=True) * jnp.exp(li - m_t)
        w_inter = jnp.exp(log_inter - m_t)
        w_k = jnp.exp(li - m_t)
        w_C = jnp.exp(log_inter - m_t)
        kw = kh * w_k
        rows = []
        for bi in range(DEC_BB):
            C = C0_ref[bi, hh]
            q8 = jnp.broadcast_to(qh[bi:bi + 1, :], (SUBLANES, ML_DH))
            rows.append(_dot_f32(q8, C)[0:1, :])
            k8 = jnp.where(row8 == 0, jnp.broadcast_to(kw[bi:bi + 1, :], (SUBLANES, ML_DH)), 0.0)
            v8 = jnp.broadcast_to(vh[bi:bi + 1, :], (SUBLANES, ML_DH))
            C_ref[bi, hh] = w_C[bi:bi + 1, :] * C + _dot_tn_f32(k8, v8)
        qC = jnp.concatenate(rows, axis=0)
        num = s * vh + w_inter * qC
        den = s + w_inter * jnp.sum(qh * n, -1, keepdims=True)
        hcell = num / jnp.maximum(jnp.abs(den), jnp.exp(-m_t))
        n_ref[:, sl] = w_C * n + kw
        m_out = jnp.where(lane == hh, m_t, m_out)
        hn = _head_norm(hcell, gng_ref[:, sl], gnb_ref[:, sl], GN_EPS)
        yc_ref[:, sl] = (hn + skip_ref[:, sl] * xc[:, sl]) * _silu(z[:, sl])
    m_ref[...] = m_out


def _mlstm_decode(p2, st, Wl, C_all, j, C_prev):
    B = p2.shape[0]
    bb = DEC_BB
    wg, bg, _, _ = _mlstm_weights(Wl)
    row2 = lambda a: a.reshape(1, D_HALF)
    rows = lambda n: pl.BlockSpec((bb, n), lambda i: (i, 0))
    convs = pl.BlockSpec((CONV_W - 1, bb, D_HALF), lambda i: (0, i, 0))
    ins = [p2, Wl['ml_conv_w'], row2(Wl['ml_conv_b']), Wl['ml_wq_d'], Wl['ml_wk_d'], Wl['ml_wv_d'], wg, bg,
           row2(Wl['ml_gn_g']), row2(Wl['ml_gn_b']), row2(Wl['ml_skip']),
           jnp.swapaxes(st['ml_conv'], 0, 1), C_all, st['ml_n'].reshape(B, D_HALF), _pad_lanes(st['ml_m'])]
    kern, Cs, extra_in, extra_specs, aliases, C_shape = _stacked_state_io(
        _mlstm_decode_kernel, C_all, C_prev, j, bb, len(ins) - 3, len(ins), 2)
    in_specs = [pl.BlockSpec((bb, 2 * D_HALF), lambda i: (i, 0)), _full((CONV_W, D_HALF)), _full((1, D_HALF)),
                _full((D_HALF, D_HALF)), _full((D_HALF, D_HALF)), _full((D_HALF, D_HALF)),
                _full((3 * D_HALF, LANES)), _full((1, LANES)),
                _full((1, D_HALF)), _full((1, D_HALF)), _full((1, D_HALF)),
                convs, Cs, rows(D_HALF), rows(LANES)] + extra_specs
    yc, conv, C, n, m = pl.pallas_call(
        kern,
        grid=(B // bb,),
        in_specs=in_specs,
        out_specs=[rows(D_HALF), convs, Cs, rows(D_HALF), rows(LANES)],
        out_shape=[jax.ShapeDtypeStruct((B, D_HALF), F32),
                   jax.ShapeDtypeStruct((CONV_W - 1, B, D_HALF), F32),
                   C_shape,
                   jax.ShapeDtypeStruct((B, D_HALF), F32),
                   jax.ShapeDtypeStruct((B, LANES), F32)],
        input_output_aliases=aliases,
        compiler_params=_params(("parallel",)),
        name="mlstm_decode",
    )(*ins, *extra_in)
    return yc, (C, n.reshape(B, ML_HEADS, ML_DH), m[:, :ML_HEADS], jnp.swapaxes(conv, 0, 1))


def _rwkv_pre_body(pr, pr_prev, mu_ref, w0_ref, a0_ref, w2_ref, a2_ref, g2_ref, kkw_ref, kaw_ref, rk_ref,
                   r_ref, d_ref, k_ref, v_ref, a_ref, b_ref, g_ref, bonus_ref):
    pm = pr + (pr_prev - pr) * mu_ref[...]
    r = pm[:, 0:D_HALF]
    kr = pm[:, D_HALF:2 * D_HALF]
    vr = pm[:, 2 * D_HALF:3 * D_HALF]
    lo = pm[:, 3 * D_HALF:RW_COLS_PAD]
    w_log = -_softplus(-(w0_ref[...] + _dot(jnp.tanh(lo), w2_ref[...]))) - 0.5
    a = _sigmoid(a0_ref[...] + _dot(lo, a2_ref[...]))
    g = _dot(_sigmoid(lo), g2_ref[...])
    ones_bd = _group_ones(LANES, RW_DH)
    kk = kr * kkw_ref[...]
    kk = kk / jnp.maximum(jnp.sqrt(_group_sum(kk * kk, ones_bd)), 1e-12)
    kh = kr * (1.0 + (a - 1.0) * kaw_ref[...])
    r_ref[...] = r
    d_ref[...] = jnp.exp(-jnp.exp(w_log))
    k_ref[...] = kh
    v_ref[...] = vr
    a_ref[...] = -kk
    b_ref[...] = kk * a
    g_ref[...] = g
    bonus_ref[...] = _group_sum(r * kh * rk_ref[...], ones_bd) * vr


def _rwkv_pre_prefill_kernel(pr_ref, shift0_ref, *rest, L):
    wrefs, outs, xbuf = rest[:9], rest[9:17], rest[17]
    c = pl.program_id(1)

    @pl.when(c == 0)
    def _():
        xbuf[SUBLANES - 1:SUBLANES, :] = shift0_ref[...]

    pr = pr_ref[...]
    xbuf[SUBLANES:SUBLANES + L, :] = pr
    pr_prev = xbuf[SUBLANES - 1:SUBLANES - 1 + L, :]
    xbuf[SUBLANES - 1:SUBLANES, :] = pr[L - 1:L, :]
    _rwkv_pre_body(pr, pr_prev, *wrefs, *outs)


def _rwkv_pre_decode_kernel(pr_ref, prev_ref, *rest):
    _rwkv_pre_body(pr_ref[...], prev_ref[...], *rest[:9], *rest[9:17])


def _rwkv_pre_weights(Wl):
    row2 = lambda a: a.reshape(1, D_HALF)
    padr = lambda w, o: jnp.pad(w, ((o, RW_LORA_PAD - o - w.shape[0]), (0, 0))).astype(BF16)
    mu = _pad_lanes(Wl['rw_mu'].reshape(1, RW_SHIFT_COLS), RW_COLS_PAD)
    ws = [mu, row2(Wl['rw_w0']), row2(Wl['rw_a0']),
          padr(Wl['rw_w2'], 0), padr(Wl['rw_a2'], RW_DECAY_LORA), padr(Wl['rw_g2'], RW_DECAY_LORA + RW_A_LORA),
          row2(Wl['rw_kk']), row2(Wl['rw_ka']), row2(Wl['rw_rk'])]
    specs = [_full((1, RW_COLS_PAD)), _full((1, D_HALF)), _full((1, D_HALF)),
             _full((RW_LORA_PAD, D_HALF)), _full((RW_LORA_PAD, D_HALF)), _full((RW_LORA_PAD, D_HALF)),
             _full((1, D_HALF)), _full((1, D_HALF)), _full((1, D_HALF))]
    return ws, specs


def _rwkv_pre_prefill(pr3, shift0, Wl):
    B, T, _ = pr3.shape
    L = CHUNK
    ws, wspecs = _rwkv_pre_weights(Wl)
    seq = pl.BlockSpec((None, L, D_HALF), lambda b, c: (b, c, 0))
    outs = pl.pallas_call(
        functools.partial(_rwkv_pre_prefill_kernel, L=L),
        grid=(B, T // L),
        in_specs=[pl.BlockSpec((None, L, RW_COLS_PAD), lambda b, c: (b, c, 0)),
                  pl.BlockSpec((None, 1, RW_COLS_PAD), lambda b, c: (b, 0, 0))] + wspecs,
        out_specs=[seq] * 8,
        out_shape=[jax.ShapeDtypeStruct((B, T, D_HALF), F32)] * 8,
        scratch_shapes=[pltpu.VMEM((L + SUBLANES, RW_COLS_PAD), F32)],
        compiler_params=_params(("parallel", "arbitrary")),
        name="rwkv_pre_prefill",
    )(pr3, _pad_lanes(shift0, RW_COLS_PAD).reshape(B, 1, RW_COLS_PAD), *ws)
    return outs


def _rwkv_pre_decode(pr, shift0, Wl):
    B = pr.shape[0]
    ws, wspecs = _rwkv_pre_weights(Wl)
    full2 = lambda n: pl.BlockSpec((B, n), lambda i: (0, 0))
    outs = pl.pallas_call(
        _rwkv_pre_decode_kernel,
        grid=(1,),
        in_specs=[full2(RW_COLS_PAD), full2(RW_COLS_PAD)] + wspecs,
        out_specs=[full2(D_HALF)] * 8,
        out_shape=[jax.ShapeDtypeStruct((B, D_HALF), F32)] * 8,
        compiler_params=_params(("arbitrary",)),
        name="rwkv_pre_decode",
    )(pr, _pad_lanes(shift0, RW_COLS_PAD), *ws)
    return outs


RW_IP = RW_DH // 2


def _rwkv_rec_kernel(r_ref, d_ref, k_ref, a_ref, b_ref, v_ref, S0_ref, y_ref, S_ref, *, Tc):
    @pl.when(pl.program_id(0) == 0)
    def _():
        S_ref[...] = S0_ref[...]

    lane = lax.broadcasted_iota(jnp.int32, (1, LANES), 1)

    def tiles(t):
        return tuple(jnp.where(lane < LANES // 2, ref[t], pltpu.roll(ref[t], LANES // 2, 1))
                     for ref in (a_ref, d_ref, b_ref, k_ref, r_ref))

    def step(t, carry):
        a, d, b, k, r = carry
        nxt = tiles(jnp.minimum(t + 1, Tc - 1))
        vt = v_ref[t]
        rows = []
        for ip in range(RW_IP):
            S = S_ref[ip]
            sa = jnp.sum(S * a, axis=0, keepdims=True)
            Sn = S * d + sa * b + vt[ip:ip + 1, :] * k
            S_ref[ip] = Sn
            rows.append(jnp.sum(Sn * r, axis=0, keepdims=True))
        y_ref[t] = jnp.concatenate(rows, axis=0)
        return nxt

    lax.fori_loop(0, Tc, step, tiles(0))


def _rwkv_rec_call(r, d, k, a, b, v, S0):
    T = r.shape[0]
    Tc = min(T, 32)
    vec = pl.BlockSpec((Tc, RW_DH, LANES), lambda c: (c, 0, 0))
    vsp = pl.BlockSpec((Tc, RW_IP, LANES), lambda c: (c, 0, 0))
    ssp = _full((RW_IP, RW_DH, LANES))
    return pl.pallas_call(
        functools.partial(_rwkv_rec_kernel, Tc=Tc),
        grid=(T // Tc,),
        in_specs=[vec] * 5 + [vsp, ssp],
        out_specs=[vsp, ssp],
        out_shape=[jax.ShapeDtypeStruct((T, RW_IP, LANES), F32),
                   jax.ShapeDtypeStruct((RW_IP, RW_DH, LANES), F32)],
        compiler_params=_params(("arbitrary",)),
        name="rwkv_recurrence",
    )(r, d, k, a, b, v, S0)


def _rwkv_dec_kernel(r_ref, d_ref, k_ref, a_ref, b_ref, v_ref, S0_ref, y_ref, S_ref):
    a, d, b, k, r = a_ref[...], d_ref[...], b_ref[...], k_ref[...], r_ref[...]
    v = v_ref[...]
    rows = []
    for i in range(RW_DH):
        S = S0_ref[i]
        sa = jnp.sum(S * a, axis=0, keepdims=True)
        Sn = S * d + sa * b + v[i:i + 1, :] * k
        S_ref[i] = Sn
        rows.append(jnp.sum(Sn * r, axis=0, keepdims=True))
    y_ref[...] = jnp.concatenate(rows, axis=0)


def _rwkv_decode_step(r, d, k, v, a, b, S0):
    B = r.shape[0]
    tr = lambda x: x.T.reshape(RW_HEADS, RW_DH, B)
    St = S0.reshape(B, RW_HEADS * RW_DH * RW_DH).T.reshape(RW_HEADS, RW_DH, RW_DH, B)
    vec = pl.BlockSpec((None, RW_DH, B), lambda h: (h, 0, 0))
    ssp = pl.BlockSpec((None, RW_DH, RW_DH, B), lambda h: (h, 0, 0, 0))
    y, S = pl.pallas_call(
        _rwkv_dec_kernel,
        grid=(RW_HEADS,),
        in_specs=[vec] * 6 + [ssp],
        out_specs=[vec, ssp],
        out_shape=[jax.ShapeDtypeStruct((RW_HEADS, RW_DH, B), F32),
                   jax.ShapeDtypeStruct((RW_HEADS, RW_DH, RW_DH, B), F32)],
        compiler_params=_params(("parallel",)),
        name="rwkv_decode_step",
    )(tr(r), tr(d), tr(k), tr(a), tr(b), tr(v), St)
    y = y.reshape(D_HALF, B).T
    S = S.reshape(RW_HEADS * RW_DH * RW_DH, B).T.reshape(B, RW_HEADS, RW_DH, RW_DH)
    return y, S


RW_TB = LANES // RW_BB
RW_NSB = 1


def _head_sum_rows(x):
    x3 = x.reshape(RW_HEADS, RW_DH, x.shape[-1])
    s = jnp.sum(x3, axis=1, keepdims=True)
    return jnp.broadcast_to(s, x3.shape).reshape(x.shape)


def _rwkv_pre_t_kernel(x_ref, shift0_ref, w_ref, mu_ref, w0_ref, a0_ref, w2_ref, a2_ref, g2_ref,
                       kkw_ref, kaw_ref, rk_ref,
                       r_ref, d_ref, k_ref, a_ref, b_ref, v_ref, g_ref, bonus_ref, last_ref, prev_scr):
    @pl.when(pl.program_id(0) == 0)
    def _():
        prev_scr[...] = shift0_ref[...]

    ro = lax.broadcasted_iota(jnp.int32, (LANES, LANES), 0)
    ci = lax.broadcasted_iota(jnp.int32, (LANES, LANES), 1)
    perm = jnp.where(ci == (ro % RW_BB) * RW_TB + ro // RW_BB, 1.0, 0.0).astype(BF16)
    lane = lax.broadcasted_iota(jnp.int32, (1, LANES), 1)
    grp = lane // RW_BB
    ngrp = LANES // RW_BB
    prs = {}

    def scatter(x, o_ref, t0, nrow, npiece):
        rot = [x[q * nrow:(q + 1) * nrow, :] if q == 0 else pltpu.roll(x[q * nrow:(q + 1) * nrow, :], q * RW_BB, 1)
               for q in range(npiece)]
        for t in range(RW_TB):
            m = rot[0]
            for q in range(1, npiece):
                m = jnp.where(grp == (t + q) % ngrp, rot[q], m)
            if t:
                m = pltpu.roll(m, LANES - t * RW_BB, 1)
            o_ref[t0 + t] = m

    def block(sb):
        t0 = sb * RW_TB
        xn = x_ref[:, t0:t0 + RW_TB, :].reshape(RW_BB * RW_TB, D_MODEL).astype(BF16)
        xg = jnp.dot(perm, xn, preferred_element_type=F32).astype(BF16)
        pr = lax.dot_general(w_ref[...], xg, (((1,), (1,)), ((), ())), preferred_element_type=F32)
        prs[sb] = pr
        yield
        rolled = pltpu.roll(pr, RW_BB, 1)
        before = prev_scr[...] if sb == 0 else prs[sb - 1]
        prev = jnp.where(lane < RW_BB, before, rolled)
        prs[sb] = rolled
        if sb == RW_NSB - 1:
            prev_scr[...] = rolled
            last_ref[...] = pr
        pm = pr + (prev - pr) * mu_ref[...]
        r = pm[0:D_HALF]
        kr = pm[D_HALF:2 * D_HALF]
        vr = pm[2 * D_HALF:3 * D_HALF]
        lo = pm[3 * D_HALF:RW_COLS_PAD]
        yield
        w_log = -_softplus(-(w0_ref[...] + _dot(w2_ref[...], jnp.tanh(lo)))) - 0.5
        a = _sigmoid(a0_ref[...] + _dot(a2_ref[...], lo))
        g = _dot(g2_ref[...], _sigmoid(lo))
        kk = kr * kkw_ref[...]
        kk = kk / jnp.maximum(jnp.sqrt(_head_sum_rows(kk * kk)), 1e-12)
        kh = kr * (1.0 + (a - 1.0) * kaw_ref[...])
        g_ref[sb] = g
        bonus_ref[sb] = _head_sum_rows(r * kh * rk_ref[...]) * vr
        yield
        scatter(r, r_ref, t0, RW_DH, RW_HEADS)
        yield
        scatter(jnp.exp(-jnp.exp(w_log)), d_ref, t0, RW_DH, RW_HEADS)
        yield
        scatter(kh, k_ref, t0, RW_DH, RW_HEADS)
        yield
        scatter(-kk, a_ref, t0, RW_DH, RW_HEADS)
        yield
        scatter(kk * a, b_ref, t0, RW_DH, RW_HEADS)
        yield
        vv = jnp.concatenate([vr[h * RW_DH + half * RW_IP:h * RW_DH + (half + 1) * RW_IP, :]
                              for half in range(2) for h in range(RW_HEADS)], axis=0)
        scatter(vv, v_ref, t0, RW_IP, ngrp)

    _skewed([block(sb) for sb in range(RW_NSB)])


def _lane_bcast(a, n):
    return jnp.broadcast_to(a.reshape(n, 1), (n, LANES))


def _rwkv_pre_t(x3, shift0, w_rwt, j, Wl):
    B, T, _ = x3.shape
    nblk = T // RW_TB
    padr = lambda w, o: jnp.pad(w, ((o, RW_LORA_PAD - o - w.shape[0]), (0, 0))).astype(BF16).T
    sh = jnp.pad(shift0.T, ((0, RW_COLS_PAD - RW_SHIFT_COLS), (0, LANES - RW_BB)))
    col = lambda a: _lane_bcast(a, D_HALF)
    ins = [x3, sh, w_rwt, _lane_bcast(_pad_lanes(Wl['rw_mu'].reshape(1, -1), RW_COLS_PAD), RW_COLS_PAD),
           col(Wl['rw_w0']), col(Wl['rw_a0']),
           padr(Wl['rw_w2'], 0), padr(Wl['rw_a2'], RW_DECAY_LORA), padr(Wl['rw_g2'], RW_DECAY_LORA + RW_A_LORA),
           col(Wl['rw_kk']), col(Wl['rw_ka']), col(Wl['rw_rk'])]
    tb = RW_TB * RW_NSB
    in_specs = [pl.BlockSpec((B, tb, D_MODEL), lambda c: (0, c, 0)), _full((RW_COLS_PAD, LANES)),
                pl.BlockSpec((None, RW_COLS_PAD, D_MODEL), lambda c: (j, 0, 0)), _full((RW_COLS_PAD, LANES)),
                _full((D_HALF, LANES)), _full((D_HALF, LANES)),
                _full((D_HALF, RW_LORA_PAD)), _full((D_HALF, RW_LORA_PAD)), _full((D_HALF, RW_LORA_PAD)),
                _full((D_HALF, LANES)), _full((D_HALF, LANES)), _full((D_HALF, LANES))]
    blk = pl.BlockSpec((RW_NSB, D_HALF, LANES), lambda c: (c, 0, 0))
    ktile = pl.BlockSpec((tb, RW_DH, LANES), lambda c: (c, 0, 0))
    vtile = pl.BlockSpec((tb, RW_IP, LANES), lambda c: (c, 0, 0))
    outs = pl.pallas_call(
        _rwkv_pre_t_kernel,
        grid=(T // tb,),
        in_specs=in_specs,
        out_specs=[ktile] * 5 + [vtile, blk, blk, _full((RW_COLS_PAD, LANES))],
        out_shape=[jax.ShapeDtypeStruct((T, RW_DH, LANES), F32)] * 5
                  + [jax.ShapeDtypeStruct((T, RW_IP, LANES), F32)]
                  + [jax.ShapeDtypeStruct((nblk, D_HALF, LANES), F32)] * 2
                  + [jax.ShapeDtypeStruct((RW_COLS_PAD, LANES), F32)],
        scratch_shapes=[pltpu.VMEM((RW_COLS_PAD, LANES), F32)],
        compiler_params=_params(("arbitrary",)),
        name="rwkv_pre_t",
    )(*ins)
    shift_new = outs[8][:RW_SHIFT_COLS, LANES - RW_BB:].T
    return outs[:8], shift_new


RW_NPB = 4


def _rwkv_post_t_kernel(y_ref, g_ref, bonus_ref, gng_ref, gnb_ref, yd_ref):
    _round_robin([_rwkv_post_t_block(y_ref, g_ref, bonus_ref, gng_ref, gnb_ref, yd_ref, pb)
                  for pb in range(RW_NPB)])


def _rwkv_post_t_block(y_ref, g_ref, bonus_ref, gng_ref, gnb_ref, yd_ref, pb):
    t0 = pb * RW_TB
    lane = lax.broadcasted_iota(jnp.int32, (1, LANES), 1)
    grp = lane // RW_BB
    ngrp = LANES // RW_BB
    ys = [y_ref[t0 + t] for t in range(RW_TB)]
    rolled = []
    for s in range(ngrp):
        m = ys[s % RW_TB]
        for q in range(1, ngrp):
            m = jnp.where(grp == q, ys[(q + s) % RW_TB], m)
        rolled.append(pltpu.roll(m, s * RW_BB, 1) if s else m)
    yield
    pieces = {}
    for q in range(ngrp):
        m = rolled[(-q) % ngrp]
        for t in range(1, RW_TB):
            m = jnp.where(grp == t, rolled[(t - q) % ngrp], m)
        pieces[divmod(q, RW_HEADS)] = m
    y = jnp.concatenate([pieces[(half, h)] for h in range(RW_HEADS) for half in range(2)], axis=0)
    yield
    mu = _head_sum_rows(y) * (1.0 / RW_DH)
    yc = y - mu
    var = _head_sum_rows(yc * yc) * (1.0 / RW_DH)
    hn = yc * lax.rsqrt(var + RW_GN_EPS) * gng_ref[...] + gnb_ref[...]
    yd = ((hn + bonus_ref[pb]) * g_ref[pb]).T
    yield
    for t in range(RW_TB):
        yd_ref[:, t0 + t, :] = yd[t * RW_BB:(t + 1) * RW_BB, :]


def _rwkv_rec_t(vecs, S0, Wl, T):
    r, d, k, a, b, v, g, bonus = vecs
    nblk = T // RW_TB
    B = RW_BB
    Sr = S0.reshape(B, RW_HEADS, 2, RW_IP, RW_DH).transpose(3, 4, 2, 1, 0).reshape(RW_IP, RW_DH, LANES)
    y, S = _rwkv_rec_call(r, d, k, a, b, v, Sr)
    tb = RW_TB * RW_NPB
    blk = pl.BlockSpec((RW_NPB, D_HALF, LANES), lambda c: (c, 0, 0))
    col = lambda a_: _lane_bcast(a_, D_HALF)
    yd = pl.pallas_call(
        _rwkv_post_t_kernel,
        grid=(nblk // RW_NPB,),
        in_specs=[pl.BlockSpec((tb, RW_IP, LANES), lambda c: (c, 0, 0)), blk, blk,
                  _full((D_HALF, LANES)), _full((D_HALF, LANES))],
        out_specs=pl.BlockSpec((B, tb, D_HALF), lambda c: (0, c, 0)),
        out_shape=jax.ShapeDtypeStruct((B, T, D_HALF), F32),
        compiler_params=_params(("parallel",)),
        name="rwkv_post_t",
    )(y, g, bonus, col(Wl['rw_gn_g']), col(Wl['rw_gn_b']))
    S = S.reshape(RW_IP, RW_DH, 2, RW_HEADS, B).transpose(4, 3, 2, 0, 1).reshape(B, RW_HEADS, RW_DH, RW_DH)
    return yd, S


def _rwkv_post_kernel(y_ref, g_ref, bonus_ref, gng_ref, gnb_ref, o_ref):
    ones_bd = _group_ones(LANES, RW_DH)
    y = y_ref[...]
    mu = _group_sum(y, ones_bd) * (1.0 / RW_DH)
    yc = y - mu
    var = _group_sum(yc * yc, ones_bd) * (1.0 / RW_DH)
    hn = yc * lax.rsqrt(var + RW_GN_EPS) * gng_ref[...] + gnb_ref[...]
    o_ref[...] = (hn + bonus_ref[...]) * g_ref[...]


def _rwkv_post(y2, g2, bonus2, Wl, tm):
    M = y2.shape[0]
    row = pl.BlockSpec((tm, D_HALF), lambda i: (i, 0))
    return pl.pallas_call(
        _rwkv_post_kernel,
        grid=(M // tm,),
        in_specs=[row, row, row, _full((1, D_HALF)), _full((1, D_HALF))],
        out_specs=row,
        out_shape=jax.ShapeDtypeStruct((M, D_HALF), F32),
        compiler_params=_params(("parallel",)),
        name="rwkv_post",
    )(y2, g2, bonus2, Wl['rw_gn_g'].reshape(1, D_HALF), Wl['rw_gn_b'].reshape(1, D_HALF))


def _trunk(x, st, pos, W):
    B, T, _ = x.shape
    M = B * T
    decode = T == 1
    tm_proj = min(M, 512)
    tm_post = min(M, 256)
    tm_rw = min(M, 512)
    x2 = x.reshape(M, D_MODEL)
    new = {name: [] for name in st}
    stacked = {}
    for l in range(DEPTH):
        j = l // 2
        Wl = {name: v[j] for name, v in W['per_pair'][l % 2].items()}
        stl = {name: v[j] for name, v in st.items()}
        if l % 2 == 0:
            if decode:
                p, = _proj(x2, W['ev_w_in'], j, tm_proj, (EVEN_IN,))
                ya, yb, (h, cb, S) = _even_decode(p, pos, stl, Wl, st['ret_S'], j, stacked.get('ret_S'))
                stacked['ret_S'] = S
            else:
                ya, yb, (h, cb, S) = _even_prefill(x2.reshape(B, T, D_MODEL), W['ev_w_in'], j, pos, stl, Wl)
                new['ret_S'].append(S)
            new['lru_h'].append(h)
            new['lru_conv'].append(cb)
            wout = W['ev_w_out']
        else:
            if decode:
                p_ml, p_rw = _proj(x2, W['od_w_in'], j, tm_proj, (2 * D_HALF, RW_COLS_PAD))
                ya, (C, n, m, cb) = _mlstm_decode(p_ml, stl, Wl, st['ml_C'], j, stacked.get('ml_C'))
                stacked['ml_C'] = C
                r, d, k, v, a, b, g, bonus = _rwkv_pre_decode(p_rw, stl['rw_shift'], Wl)
                y, S = _rwkv_decode_step(r, d, k, v, a, b, stl['rw_S'])
                yb = _rwkv_post(y, g, bonus, Wl, tm_rw)
                shift_new = p_rw[:, :RW_SHIFT_COLS]
            else:
                x3 = x2.reshape(B, T, D_MODEL)
                ya, (C, n, m, cb) = _mlstm_prefill(x3, W['od_w_in'], j, stl, Wl)
                vecs, shift_new = _rwkv_pre_t(x3, stl['rw_shift'], W['od_w_rwt'], j, Wl)
                yb, S = _rwkv_rec_t(vecs, stl['rw_S'], Wl, T)
                new['ml_C'].append(C)
            new['ml_n'].append(n)
            new['ml_m'].append(m)
            new['ml_conv'].append(cb)
            new['rw_S'].append(S)
            new['rw_shift'].append(shift_new)
            wout = W['od_w_out']
        x2 = _post(ya.reshape(M, D_HALF), yb.reshape(M, D_HALF), x2, wout, j, l,
                   W['ln1_g'], W['ln1_b'], W['mlp_w1'], W['mlp_w2'], W['ln2_g'], W['ln2_b'], tm_post)
    out = {name: stacked[name] if name in stacked else jnp.stack(v) for name, v in new.items()}
    return x2.reshape(B, T, D_MODEL), out


def _prepare_weights(w):
    even_names = ('lru_conv_w', 'lru_conv_b', 'lru_ba', 'lru_bx', 'lru_lambda', 'ret_gn_g', 'ret_gn_b')
    odd_names = ('ml_conv_w', 'ml_conv_b', 'ml_w_gate', 'ml_b_gate', 'ml_gn_g', 'ml_gn_b', 'ml_skip',
                 'rw_mu', 'rw_w0', 'rw_w2', 'rw_a0', 'rw_a2', 'rw_g2', 'rw_kk', 'rw_ka', 'rw_rk', 'rw_gn_g', 'rw_gn_b')
    even = {n: w[n] for n in even_names}
    even.update(lru_wa_d=jax.vmap(_lru_dense)(w['lru_wa']), lru_wx_d=jax.vmap(_lru_dense)(w['lru_wx']))
    odd = {n: w[n] for n in odd_names}
    odd.update(ml_wq_d=jax.vmap(_ml_dense)(w['ml_wq']), ml_wk_d=jax.vmap(_ml_dense)(w['ml_wk']),
               ml_wv_d=jax.vmap(_ml_dense)(w['ml_wv']))
    od_w_in = _pad_lanes(w['od_w_in'], ODD_IN_PAD).astype(BF16)
    return dict(per_pair=(even, odd),
                ev_w_in=w['ev_w_in'].astype(BF16), ev_w_out=w['ev_w_out'].astype(BF16),
                od_w_in=od_w_in, od_w_rwt=jnp.swapaxes(od_w_in[:, :, 2 * D_HALF:], 1, 2),
                od_w_out=w['od_w_out'].astype(BF16),
                mlp_w1=w['mlp_w1'].astype(BF16), mlp_w2=w['mlp_w2'].astype(BF16),
                ln1_g=w['ln1_g'], ln1_b=w['ln1_b'], ln2_g=w['ln2_g'], ln2_b=w['ln2_b'])


def _zero_states(batch):
    z = lambda *s: jnp.zeros(s, F32)
    n_even, n_odd = (DEPTH + 1) // 2, DEPTH // 2
    return dict(lru_h=z(n_even, batch, D_HALF), lru_conv=z(n_even, batch, CONV_W - 1, D_HALF),
                ret_S=z(n_even, batch, RET_HEADS, RET_DH, RET_DH),
                ml_C=z(n_odd, batch, ML_HEADS, ML_DH, ML_DH), ml_n=z(n_odd, batch, ML_HEADS, ML_DH),
                ml_m=z(n_odd, batch, ML_HEADS), ml_conv=z(n_odd, batch, CONV_W - 1, D_HALF),
                rw_S=z(n_odd, batch, RW_HEADS, RW_DH, RW_DH), rw_shift=z(n_odd, batch, RW_SHIFT_COLS))


def kernel(x_prompt, x_sample, state_lru_h, state_lru_conv, state_ret, state_mlstm_C, state_mlstm_n, state_mlstm_m, state_mlstm_conv, state_rwkv_S, state_rwkv_shift, ln1_g, ln1_b, ln2_g, ln2_b, mlp_w1, mlp_w2, ev_w_in, ev_w_out, lru_conv_w, lru_conv_b, lru_wa, lru_ba, lru_wx, lru_bx, lru_lambda, ret_gn_g, ret_gn_b, od_w_in, od_w_out, ml_conv_w, ml_conv_b, ml_wq, ml_wk, ml_wv, ml_w_gate, ml_b_gate, ml_gn_g, ml_gn_b, ml_skip, rw_mu, rw_w0, rw_w2, rw_a0, rw_a2, rw_g2, rw_kk, rw_ka, rw_rk, rw_gn_g, rw_gn_b):
    W = _prepare_weights(dict(
        ln1_g=ln1_g, ln1_b=ln1_b, ln2_g=ln2_g, ln2_b=ln2_b, mlp_w1=mlp_w1, mlp_w2=mlp_w2,
        ev_w_in=ev_w_in, ev_w_out=ev_w_out, lru_conv_w=lru_conv_w, lru_conv_b=lru_conv_b,
        lru_wa=lru_wa, lru_ba=lru_ba, lru_wx=lru_wx, lru_bx=lru_bx, lru_lambda=lru_lambda,
        ret_gn_g=ret_gn_g, ret_gn_b=ret_gn_b, od_w_in=od_w_in, od_w_out=od_w_out,
        ml_conv_w=ml_conv_w, ml_conv_b=ml_conv_b, ml_wq=ml_wq, ml_wk=ml_wk, ml_wv=ml_wv,
        ml_w_gate=ml_w_gate, ml_b_gate=ml_b_gate, ml_gn_g=ml_gn_g, ml_gn_b=ml_gn_b, ml_skip=ml_skip,
        rw_mu=rw_mu, rw_w0=rw_w0, rw_w2=rw_w2, rw_a0=rw_a0, rw_a2=rw_a2, rw_g2=rw_g2,
        rw_kk=rw_kk, rw_ka=rw_ka, rw_rk=rw_rk, rw_gn_g=rw_gn_g, rw_gn_b=rw_gn_b))
    st_sample = dict(lru_h=state_lru_h, lru_conv=state_lru_conv, ret_S=state_ret,
                     ml_C=state_mlstm_C, ml_n=state_mlstm_n, ml_m=state_mlstm_m, ml_conv=state_mlstm_conv,
                     rw_S=state_rwkv_S, rw_shift=state_rwkv_shift)
    pos_prompt = jnp.arange(x_prompt.shape[1], dtype=jnp.int32)
    pos_sample = PAST_LEN + jnp.arange(x_sample.shape[1], dtype=jnp.int32)
    y_prompt, sp = _trunk(x_prompt, _zero_states(x_prompt.shape[0]), pos_prompt, W)
    y_sample, ss = _trunk(x_sample, st_sample, pos_sample, W)
    names = ('lru_h', 'lru_conv', 'ret_S', 'ml_C', 'ml_n', 'ml_m', 'ml_conv', 'rw_S', 'rw_shift')
    return (y_prompt, y_sample) + tuple(sp[n] for n in names) + tuple(ss[n] for n in names)
```

```python
import functools

import jax
import jax.numpy as jnp
from jax import lax
from jax.experimental import pallas as pl
from jax.experimental.pallas import tpu as pltpu

F32 = jnp.float32
BF16 = jnp.bfloat16

D_MODEL = 1024
DEPTH = 4
PAST_LEN = 16384
D_HALF = D_MODEL // 2
CONV_W = 4
LRU_BLOCKS = 8
LRU_BLOCK = D_HALF // LRU_BLOCKS
LRU_C = 8.0
RET_HEADS = 4
RET_DH = D_HALF // RET_HEADS
CHUNK = 128
ROPE_BASE = 10000.0
ML_HEADS = 4
ML_DH = D_HALF // ML_HEADS
ML_QKV_BLOCK = 4
ML_NBLK = D_HALF // ML_QKV_BLOCK
RW_HEADS = 8
RW_DH = D_HALF // RW_HEADS
RW_DECAY_LORA = 32
RW_A_LORA = 32
RW_GATE_LORA = 96
RW_LORA = RW_DECAY_LORA + RW_A_LORA + RW_GATE_LORA
RW_SHIFT_COLS = 3 * D_HALF + RW_LORA
D_FF = 4 * D_MODEL
ALPHA = (2.0 * DEPTH) ** 0.25
EVEN_IN = 6 * D_HALF
ODD_IN = 2 * D_HALF + RW_SHIFT_COLS
LN_EPS = 1e-5
GN_EPS = 1e-5
RW_GN_EPS = 64e-5

LANES = 128
SUBLANES = 8
RW_LORA_PAD = 2 * LANES
RW_COLS_PAD = 3 * D_HALF + RW_LORA_PAD
ODD_IN_PAD = 2 * D_HALF + RW_COLS_PAD
RW_BB = 8
VMEM_LIMIT = 56 * 1024 * 1024


def _params(sem):
    return pltpu.CompilerParams(dimension_semantics=sem, vmem_limit_bytes=VMEM_LIMIT)


def _dot(a, b):
    return jnp.dot(a.astype(BF16), b.astype(BF16), preferred_element_type=F32)


def _dot_nt(a, b):
    return lax.dot_general(a.astype(BF16), b.astype(BF16), (((1,), (1,)), ((), ())),
                           preferred_element_type=F32)


def _dot_tn(a, b):
    return lax.dot_general(a.astype(BF16), b.astype(BF16), (((0,), (0,)), ((), ())),
                           preferred_element_type=F32)


def _split3(a):
    hi = a.astype(BF16)
    r1 = a - hi.astype(F32)
    mid = r1.astype(BF16)
    lo = (r1 - mid.astype(F32)).astype(BF16)
    return hi, mid, lo


def _xdot(a, b01):
    hi, mid, lo = _split3(a)
    f = lambda t: jnp.dot(t, b01, preferred_element_type=F32)
    return f(hi) + f(mid) + f(lo)


def _xdot_l(b01, a):
    hi, mid, lo = _split3(a)
    f = lambda t: jnp.dot(b01, t, preferred_element_type=F32)
    return f(hi) + f(mid) + f(lo)


def _sigmoid(x):
    return 1.0 / (1.0 + jnp.exp(-x))


def _silu(x):
    return x * _sigmoid(x)


def _softplus(x):
    return jnp.maximum(x, 0.0) + jnp.log1p(jnp.exp(-jnp.abs(x)))


def _gelu_tanh(x):
    return 0.5 * x * (1.0 + jnp.tanh(0.7978845608028654 * (x + 0.044715 * (x * x * x))))


def _layer_norm(x, g, b, eps):
    mu = jnp.mean(x, -1, keepdims=True)
    xc = x - mu
    var = jnp.mean(xc * xc, -1, keepdims=True)
    return xc * lax.rsqrt(var + eps) * g + b


def _group_ones(n, group):
    r = lax.broadcasted_iota(jnp.int32, (n, n), 0) // group
    c = lax.broadcasted_iota(jnp.int32, (n, n), 1) // group
    return jnp.where(r == c, 1.0, 0.0).astype(BF16)


def _group_sum(x, ones_bd):
    parts = [_xdot(x[:, s * LANES:(s + 1) * LANES], ones_bd) for s in range(x.shape[1] // LANES)]
    return jnp.concatenate(parts, axis=-1)


def _rotate(x, cosf, sinf):
    return x * cosf + pltpu.roll(x, RET_DH // 2, 1) * sinf


def _proj_kernel(x_ref, w_ref, *o_refs, splits):
    xb = x_ref[...].astype(BF16)
    off = 0
    for o_ref, n in zip(o_refs, splits):
        o_ref[...] = jnp.dot(xb, w_ref[:, off:off + n], preferred_element_type=F32)
        off += n


def _proj(x2d, w_all, j, tm, splits):
    M = x2d.shape[0]
    N = w_all.shape[2]
    return pl.pallas_call(
        functools.partial(_proj_kernel, splits=splits),
        grid=(M // tm,),
        in_specs=[pl.BlockSpec((tm, D_MODEL), lambda i: (i, 0)),
                  pl.BlockSpec((None, D_MODEL, N), lambda i: (j, 0, 0))],
        out_specs=[pl.BlockSpec((tm, n), lambda i: (i, 0)) for n in splits],
        out_shape=[jax.ShapeDtypeStruct((M, n), F32) for n in splits],
        compiler_params=_params(("parallel",)),
        name="proj_in",
    )(x2d, w_all)


FF_CHUNK = 1024


POST_SUB = 256


def _post_kernel(ya_ref, yb_ref, x_ref, wo_ref, g1_ref, b1_ref, w1_ref, w2_ref, g2_ref, b2_ref, o_ref):
    tm = x_ref.shape[0]
    sub = min(tm, POST_SUB)

    def tile(r0):
        rows = slice(r0, r0 + sub)
        y = (jnp.dot(ya_ref[rows, :].astype(BF16), wo_ref[0:D_HALF, :], preferred_element_type=F32)
             + jnp.dot(yb_ref[rows, :].astype(BF16), wo_ref[D_HALF:D_MODEL, :], preferred_element_type=F32))
        yield
        x1 = _layer_norm(ALPHA * x_ref[rows, :] + y, g1_ref[...], b1_ref[...], LN_EPS)
        x1b = x1.astype(BF16)
        acc = jnp.zeros(x1.shape, F32)
        for c in range(D_FF // FF_CHUNK):
            yield
            h = jnp.dot(x1b, w1_ref[:, c * FF_CHUNK:(c + 1) * FF_CHUNK], preferred_element_type=F32)
            yield
            h = jnp.square(jnp.maximum(h, 0.0))
            acc = acc + jnp.dot(h.astype(BF16), w2_ref[c * FF_CHUNK:(c + 1) * FF_CHUNK, :],
                                preferred_element_type=F32)
        yield
        o_ref[rows, :] = _layer_norm(ALPHA * x1 + acc, g2_ref[...], b2_ref[...], LN_EPS)

    _skewed([tile(r0) for r0 in range(0, tm, sub)])


def _post(ya, yb, x2d, wout_all, j, l, ln1_g, ln1_b, w1_all, w2_all, ln2_g, ln2_b, tm):
    M = x2d.shape[0]
    row = lambda i: (i, 0)
    vec = pl.BlockSpec((None, 1, D_MODEL), lambda i: (l, 0, 0))
    r3 = lambda a: a.reshape(DEPTH, 1, D_MODEL)
    return pl.pallas_call(
        _post_kernel,
        grid=(M // tm,),
        in_specs=[pl.BlockSpec((tm, D_HALF), row), pl.BlockSpec((tm, D_HALF), row),
                  pl.BlockSpec((tm, D_MODEL), row),
                  pl.BlockSpec((None, D_MODEL, D_MODEL), lambda i: (j, 0, 0)),
                  vec, vec,
                  pl.BlockSpec((None, D_MODEL, D_FF), lambda i: (l, 0, 0)),
                  pl.BlockSpec((None, D_FF, D_MODEL), lambda i: (l, 0, 0)),
                  vec, vec],
        out_specs=pl.BlockSpec((tm, D_MODEL), row),
        out_shape=jax.ShapeDtypeStruct((M, D_MODEL), F32),
        compiler_params=_params(("parallel",)),
        name="post_mlp",
    )(ya, yb, x2d, wout_all, r3(ln1_g), r3(ln1_b), w1_all, w2_all, r3(ln2_g), r3(ln2_b))


def _conv_prefill(x, xbuf, cw_ref, cb_ref, L):
    xbuf[SUBLANES:SUBLANES + L, :] = x
    y = cb_ref[...]
    for i in range(CONV_W):
        y = y + cw_ref[i:i + 1, :] * xbuf[SUBLANES - (CONV_W - 1) + i:SUBLANES - (CONV_W - 1) + i + L, :]
    tail = xbuf[L + SUBLANES - (CONV_W - 1):L + SUBLANES, :]
    xbuf[SUBLANES - (CONV_W - 1):SUBLANES, :] = tail
    return y, tail


def _lru_gates(xc, wa_ref, ba_ref, wx_ref, bx_ref, lam_ref):
    xcb = xc.astype(BF16)
    nslab = D_HALF // LANES
    ra = jnp.concatenate([jnp.dot(xcb[:, s * LANES:(s + 1) * LANES], wa_ref[s], preferred_element_type=F32)
                          for s in range(nslab)], axis=-1)
    rx = jnp.concatenate([jnp.dot(xcb[:, s * LANES:(s + 1) * LANES], wx_ref[s], preferred_element_type=F32)
                          for s in range(nslab)], axis=-1)
    r = _sigmoid(ra + ba_ref[...])
    i = _sigmoid(rx + bx_ref[...])
    log_a = -LRU_C * r * _softplus(-lam_ref[...])
    a = jnp.exp(log_a)
    t = jnp.tanh(log_a)
    u = jnp.sqrt(-2.0 * t / (1.0 - t)) * (i * xc)
    return a, u


def _head_norm(o, g, b, eps):
    mu = jnp.mean(o, -1, keepdims=True)
    oc = o - mu
    var = jnp.mean(oc * oc, -1, keepdims=True)
    return oc * lax.rsqrt(var + eps) * g + b


EV_NB = 4


class _Cols:
    def __init__(self, ref, off):
        self.ref, self.off = ref, off

    def __getitem__(self, idx):
        if idx is Ellipsis:
            return self.ref[:, self.off:self.off + D_HALF]
        rows, cols = idx
        return self.ref[rows, self.off + cols.start:self.off + cols.stop]


def _even_prefill_kernel(x_ref, w_ref,
                         cw_ref, cb_ref, wa_ref, ba_ref, wx_ref, bx_ref, lam_ref,
                         cos_ref, sin_ref, dmask_ref, qd_ref, kd_ref, cd_ref, gng_ref, gnb_ref,
                         h0_ref, conv0_ref, S0_ref,
                         ya_ref, yb_ref, h_ref, conv_ref, S_ref, xbuf, pbuf, *, L):
    @pl.when(pl.program_id(1) == 0)
    def _():
        h_ref[...] = h0_ref[...]
        S_ref[...] = S0_ref[...]
        xbuf[:, SUBLANES - (CONV_W - 1):SUBLANES, :] = conv0_ref[...]

    rows = [_even_prefill_one(x_ref.at[bi], w_ref, pbuf.at[bi],
                              cw_ref, cb_ref, wa_ref, ba_ref, wx_ref, bx_ref, lam_ref,
                              cos_ref, sin_ref, dmask_ref, qd_ref, kd_ref, cd_ref, gng_ref, gnb_ref,
                              ya_ref.at[bi], yb_ref.at[bi], h_ref.at[bi], conv_ref.at[bi], S_ref.at[bi],
                              xbuf.at[bi], L=L) for bi in range(EV_NB)]
    _round_robin(rows)


def _even_prefill_one(x_ref, w_ref, p_ref,
                      cw_ref, cb_ref, wa_ref, ba_ref, wx_ref, bx_ref, lam_ref,
                      cos_ref, sin_ref, dmask_ref, qd_ref, kd_ref, cd_ref, gng_ref, gnb_ref,
                      ya_ref, yb_ref, h_ref, conv_ref, S_ref, xbuf, *, L):
    p_ref[...] = jnp.dot(x_ref[...].astype(BF16), w_ref[...], preferred_element_type=F32)
    xa_ref, ga_ref, q_ref, k_ref, v_ref, gb_ref = (_Cols(p_ref, i * D_HALF) for i in range(6))
    yield
    H = range(RET_HEADS)
    sls = [slice(hh * RET_DH, (hh + 1) * RET_DH) for hh in H]
    cosf = cos_ref[...]
    sinf = sin_ref[...]
    qh = [_rotate(q_ref[:, sl], cosf, sinf) for sl in sls]
    kh = [_rotate(k_ref[:, sl], cosf, sinf) * (RET_DH ** -0.5) for sl in sls]
    vh = [v_ref[:, sl].astype(BF16) for sl in sls]
    qk = [_dot_nt(qh[hh], kh[hh]) for hh in H]
    S = [S_ref[hh] for hh in H]
    qS = [_dot(qh[hh] * qd_ref[:, sls[hh]], S[hh]) for hh in H]
    kv = [_dot_tn(kh[hh] * kd_ref[:, sls[hh]], vh[hh]) for hh in H]
    yield
    xc, tail = _conv_prefill(xa_ref[...], xbuf, cw_ref, cb_ref, L)
    conv_ref[...] = tail
    a, u = _lru_gates(xc, wa_ref, ba_ref, wx_ref, bx_ref, lam_ref)
    yield
    for hh in H:
        S_ref[hh] = S[hh] * cd_ref[hh] + kv[hh]
    sc = [(qk[hh] * dmask_ref[hh]).astype(BF16) for hh in H]
    o = [jnp.dot(sc[hh], vh[hh], preferred_element_type=F32) + qS[hh] for hh in H]
    yield
    row = lax.broadcasted_iota(jnp.int32, (L, D_HALF), 0) % SUBLANES
    s = 1
    while s < SUBLANES:
        keep = row >= s
        a_sh = jnp.where(keep, pltpu.roll(a, s, 0), 1.0)
        u_sh = jnp.where(keep, pltpu.roll(u, s, 0), 0.0)
        u = a * u_sh + u
        a = a * a_sh
        s *= 2
        yield
    carry = h_ref[...]
    groups = []
    for g in range(L // SUBLANES):
        rows8 = slice(g * SUBLANES, (g + 1) * SUBLANES)
        hg = a[rows8, :] * carry + u[rows8, :]
        carry = hg[SUBLANES - 1:SUBLANES, :]
        groups.append(hg)
        if g % 4 == 3:
            yield
    h = jnp.concatenate(groups, axis=0)
    h_ref[...] = carry
    ya_ref[...] = _gelu_tanh(ga_ref[...]) * h
    ones = jnp.ones((RET_DH, LANES), BF16)
    mu = [jnp.dot(o[hh].astype(BF16), ones, preferred_element_type=F32) * (1.0 / RET_DH) for hh in H]
    yield
    oc = [o[hh] - mu[hh] for hh in H]
    var = [jnp.dot((oc[hh] * oc[hh]).astype(BF16), ones, preferred_element_type=F32) * (1.0 / RET_DH) for hh in H]
    yield
    for hh in H:
        sl = sls[hh]
        on = oc[hh] * lax.rsqrt(var[hh] + GN_EPS) * gng_ref[:, sl] + gnb_ref[:, sl]
        yb_ref[:, sl] = _silu(gb_ref[:, sl]) * on


def _ret_tables(L):
    log_gamma = jnp.log1p(-jnp.exp2(-5.0 - jnp.arange(RET_HEADS, dtype=F32)))
    idx = jnp.arange(L, dtype=F32)
    diff = idx[:, None] - idx[None, :]
    dmask = jnp.where(diff >= 0, jnp.exp(log_gamma[:, None, None] * jnp.maximum(diff, 0.0)), 0.0)
    qd = jnp.exp(log_gamma[:, None] * (idx + 1.0))
    kd = jnp.exp(log_gamma[:, None] * (L - 1.0 - idx))
    cd = jnp.exp(log_gamma * L)
    qd_full = jnp.repeat(qd.T, RET_DH, axis=1)
    kd_full = jnp.repeat(kd.T, RET_DH, axis=1)
    cd_full = jnp.broadcast_to(cd[:, None, None], (RET_HEADS, 1, RET_DH))
    return dmask, qd_full, kd_full, cd_full


def _rope_tables(pos):
    half = RET_DH // 2
    inv = ROPE_BASE ** (-jnp.arange(half, dtype=F32) / half)
    ang = pos.astype(F32)[:, None] * inv[None, :]
    cos, sin = jnp.cos(ang), jnp.sin(ang)
    return jnp.concatenate([cos, cos], -1), jnp.concatenate([-sin, sin], -1)


def _lru_dense(w):
    pairs = LANES // LRU_BLOCK
    w4 = w.reshape(LRU_BLOCKS // pairs, pairs, LRU_BLOCK, LRU_BLOCK)
    eye = jnp.eye(pairs, dtype=w.dtype)
    d = w4[:, :, :, None, :] * eye[None, :, None, :, None]
    return d.reshape(LRU_BLOCKS // pairs, LANES, LANES).astype(BF16)


def _full(shape):
    n = len(shape)
    return pl.BlockSpec(shape, lambda *_: (0,) * n)


def _even_prefill(x3, w_in_all, j, pos, st, Wl):
    B, T, _ = x3.shape
    L = CHUNK
    nc = T // L
    dmask, qd, kd, cd = _ret_tables(L)
    cosf, sinf = _rope_tables(pos)
    nb = EV_NB
    perb = lambda *s: pl.BlockSpec((nb,) + s, lambda b, c: (b,) + (0,) * len(s))
    row2 = lambda a: a.reshape(1, D_HALF)
    ins = [x3, w_in_all, Wl['lru_conv_w'], row2(Wl['lru_conv_b']), Wl['lru_wa_d'], row2(Wl['lru_ba']),
           Wl['lru_wx_d'], row2(Wl['lru_bx']), row2(Wl['lru_lambda']),
           cosf, sinf, dmask, qd, kd, cd, row2(Wl['ret_gn_g']), row2(Wl['ret_gn_b']),
           st['lru_h'].reshape(B, 1, D_HALF), st['lru_conv'], st['ret_S']]
    in_specs = [pl.BlockSpec((nb, L, D_MODEL), lambda b, c: (b, c, 0)),
                pl.BlockSpec((None, D_MODEL, EVEN_IN), lambda b, c: (j, 0, 0))] + [
        _full((CONV_W, D_HALF)), _full((1, D_HALF)), _full((4, LANES, LANES)), _full((1, D_HALF)),
        _full((4, LANES, LANES)), _full((1, D_HALF)), _full((1, D_HALF)),
        pl.BlockSpec((L, RET_DH), lambda b, c: (c, 0)), pl.BlockSpec((L, RET_DH), lambda b, c: (c, 0)),
        _full((RET_HEADS, L, L)), _full((L, D_HALF)), _full((L, D_HALF)), _full((RET_HEADS, 1, RET_DH)),
        _full((1, D_HALF)), _full((1, D_HALF)),
        perb(1, D_HALF), perb(CONV_W - 1, D_HALF), perb(RET_HEADS, RET_DH, RET_DH)]
    seq = pl.BlockSpec((nb, L, D_HALF), lambda b, c: (b, c, 0))
    ya, yb, h, conv, S = pl.pallas_call(
        functools.partial(_even_prefill_kernel, L=L),
        grid=(B // nb, nc),
        in_specs=in_specs,
        out_specs=[seq, seq, perb(1, D_HALF), perb(CONV_W - 1, D_HALF), perb(RET_HEADS, RET_DH, RET_DH)],
        out_shape=[jax.ShapeDtypeStruct((B, T, D_HALF), F32), jax.ShapeDtypeStruct((B, T, D_HALF), F32),
                   jax.ShapeDtypeStruct((B, 1, D_HALF), F32),
                   jax.ShapeDtypeStruct((B, CONV_W - 1, D_HALF), F32),
                   jax.ShapeDtypeStruct((B, RET_HEADS, RET_DH, RET_DH), F32)],
        scratch_shapes=[pltpu.VMEM((nb, L + SUBLANES, D_HALF), F32), pltpu.VMEM((nb, L, EVEN_IN), F32)],
        compiler_params=_params(("parallel", "arbitrary")),
        name="even_prefill",
    )(*ins)
    return ya, yb, (h.reshape(B, D_HALF), conv, S)


DEC_BB = 8


def _even_decode_kernel(p_ref, cw_ref, cb_ref, wa_ref, ba_ref, wx_ref, bx_ref, lam_ref,
                        cos_ref, sin_ref, dm_ref, qd_ref, kd_ref, cd_ref, gng_ref, gnb_ref,
                        h0_ref, conv0_ref, S0_ref,
                        ya_ref, yb_ref, h_ref, conv_ref, S_ref):
    col = lambda i: p_ref[:, i * D_HALF:(i + 1) * D_HALF]
    xa = col(0)
    xc = cb_ref[...] + cw_ref[CONV_W - 1:CONV_W, :] * xa
    for i in range(CONV_W - 1):
        xc = xc + cw_ref[i:i + 1, :] * conv0_ref[i]
    for i in range(CONV_W - 2):
        conv_ref[i] = conv0_ref[i + 1]
    conv_ref[CONV_W - 2] = xa
    a, u = _lru_gates(xc, wa_ref, ba_ref, wx_ref, bx_ref, lam_ref)
    h = a * h0_ref[...] + u
    h_ref[...] = h
    ya_ref[...] = _gelu_tanh(col(1)) * h

    cosf = cos_ref[...]
    sinf = sin_ref[...]
    row8 = lax.broadcasted_iota(jnp.int32, (SUBLANES, RET_DH), 0)
    q, k, v, gb = col(2), col(3), col(4), col(5)
    for hh in range(RET_HEADS):
        sl = slice(hh * RET_DH, (hh + 1) * RET_DH)
        qh = _rotate(q[:, sl], cosf, sinf)
        kh = _rotate(k[:, sl], cosf, sinf) * (RET_DH ** -0.5)
        vh = v[:, sl]
        qk = jnp.sum(qh * kh, -1, keepdims=True) * dm_ref[:, sl]
        qq = qh * qd_ref[:, sl]
        kk = kh * kd_ref[:, sl]
        rows = []
        for bi in range(DEC_BB):
            S = S0_ref[bi, hh]
            q8 = jnp.broadcast_to(qq[bi:bi + 1, :], (SUBLANES, RET_DH))
            rows.append(_dot_f32(q8, S)[0:1, :])
            k8 = jnp.where(row8 == 0, jnp.broadcast_to(kk[bi:bi + 1, :], (SUBLANES, RET_DH)), 0.0)
            v8 = jnp.broadcast_to(vh[bi:bi + 1, :], (SUBLANES, RET_DH))
            S_ref[bi, hh] = S * cd_ref[hh] + _dot_tn_f32(k8, v8)
        o = qk * vh + jnp.concatenate(rows, axis=0)
        on = _head_norm(o, gng_ref[:, sl], gnb_ref[:, sl], GN_EPS)
        yb_ref[:, sl] = _silu(gb[:, sl]) * on


def _dot_f32(a, b):
    return jnp.dot(a, b, preferred_element_type=F32)


def _dot_tn_f32(a, b):
    return lax.dot_general(a, b, (((0,), (0,)), ((), ())), preferred_element_type=F32)


def _skip_ref(kernel, pos):
    def wrapped(*refs):
        kernel(*refs[:pos], *refs[pos + 1:])
    return wrapped


def _all_layers(kernel, pos_in, pos_out, j, n_layers):
    def wrapped(*refs):
        refs = list(refs)
        s_in, s_out = refs[pos_in], refs[pos_out]
        for other in range(n_layers):
            if other != j:
                s_out[other] = s_in[other]
        refs[pos_in], refs[pos_out] = s_in.at[j], s_out.at[j]
        kernel(*refs)
    return wrapped


def _stacked_state_io(kernel, S_all, S_prev, j, bb, pos_in, n_in, out_idx):
    n_layers, tail = S_all.shape[0], S_all.shape[2:]
    zeros = (0,) * len(tail)
    shape = jax.ShapeDtypeStruct(S_all.shape, F32)
    if S_prev is None:
        spec = pl.BlockSpec((n_layers, bb) + tail, lambda i: (0, i) + zeros)
        return _all_layers(kernel, pos_in, n_in + out_idx, j, n_layers), spec, [], [], {}, shape
    spec = pl.BlockSpec((None, bb) + tail, lambda i: (j, i) + zeros)
    return _skip_ref(kernel, n_in), spec, [S_prev], [pl.BlockSpec(memory_space=pl.ANY)], {n_in: out_idx}, shape


def _even_decode(p2, pos, st, Wl, S_all, j, S_prev):
    B = p2.shape[0]
    bb = DEC_BB
    dmask, qd, kd, cd = _ret_tables(1)
    dm = jnp.repeat(dmask[:, 0, :].T, RET_DH, axis=1)
    cosf, sinf = _rope_tables(pos)
    row2 = lambda a: a.reshape(1, D_HALF)
    rows = lambda n: pl.BlockSpec((bb, n), lambda i: (i, 0))
    convs = pl.BlockSpec((CONV_W - 1, bb, D_HALF), lambda i: (0, i, 0))
    ins = [p2, Wl['lru_conv_w'], row2(Wl['lru_conv_b']), Wl['lru_wa_d'], row2(Wl['lru_ba']),
           Wl['lru_wx_d'], row2(Wl['lru_bx']), row2(Wl['lru_lambda']),
           cosf, sinf, dm, qd, kd, cd, row2(Wl['ret_gn_g']), row2(Wl['ret_gn_b']),
           st['lru_h'], jnp.swapaxes(st['lru_conv'], 0, 1), S_all]
    kern, Ss, extra_in, extra_specs, aliases, S_shape = _stacked_state_io(
        _even_decode_kernel, S_all, S_prev, j, bb, len(ins) - 1, len(ins), 4)
    in_specs = [rows(EVEN_IN), _full((CONV_W, D_HALF)), _full((1, D_HALF)), _full((4, LANES, LANES)),
                _full((1, D_HALF)), _full((4, LANES, LANES)), _full((1, D_HALF)), _full((1, D_HALF)),
                _full((1, RET_DH)), _full((1, RET_DH)), _full((1, D_HALF)), _full((1, D_HALF)),
                _full((1, D_HALF)), _full((RET_HEADS, 1, RET_DH)), _full((1, D_HALF)), _full((1, D_HALF)),
                rows(D_HALF), convs, Ss] + extra_specs
    ya, yb, h, conv, S = pl.pallas_call(
        kern,
        grid=(B // bb,),
        in_specs=in_specs,
        out_specs=[rows(D_HALF), rows(D_HALF), rows(D_HALF), convs, Ss],
        out_shape=[jax.ShapeDtypeStruct((B, D_HALF), F32), jax.ShapeDtypeStruct((B, D_HALF), F32),
                   jax.ShapeDtypeStruct((B, D_HALF), F32),
                   jax.ShapeDtypeStruct((CONV_W - 1, B, D_HALF), F32), S_shape],
        input_output_aliases=aliases,
        compiler_params=_params(("parallel",)),
        name="even_decode",
    )(*ins, *extra_in)
    return ya, yb, (h, jnp.swapaxes(conv, 0, 1), S)


def _mlstm_qkv_gates(xm, xc, wq_ref, wk_ref, wv_ref, wg_ref, bg_ref):
    q = _dot(xc, wq_ref[...])
    k = _dot(xc, wk_ref[...])
    v = _dot(xm, wv_ref[...])
    g_col = (_dot(q, wg_ref[0:D_HALF, :]) + _dot(k, wg_ref[D_HALF:2 * D_HALF, :])
             + _dot(v, wg_ref[2 * D_HALF:3 * D_HALF, :]) + bg_ref[...])
    return q, k, v, g_col


ML_NB = 4


def _skewed(gens):
    live = []
    pending = list(gens)
    while pending or live:
        if pending:
            live.append(pending.pop(0))
        nxt = []
        for g in live:
            try:
                next(g)
                nxt.append(g)
            except StopIteration:
                pass
        live = nxt


def _round_robin(gens):
    gens = list(gens)
    while gens:
        alive = []
        for g in gens:
            try:
                next(g)
                alive.append(g)
            except StopIteration:
                pass
        gens = alive


def _mlstm_prefill_kernel(x_ref, w_ref, cw_ref, cb_ref, wq_ref, wk_ref, wv_ref, wg_ref, bg_ref,
                          wgt_ref, bgt_ref, gng_ref, gnb_ref, skip_ref,
                          conv0_ref, C0_ref, n0_ref, m0_ref,
                          yc_ref, conv_ref, C_ref, n_ref, m_ref, xbuf, ncol, pbuf, *, L):
    @pl.when(pl.program_id(1) == 0)
    def _():
        ones = jnp.ones((ML_DH, LANES), BF16)
        eye = jnp.where(lax.broadcasted_iota(jnp.int32, (ML_DH, LANES), 0)
                        == lax.broadcasted_iota(jnp.int32, (ML_DH, LANES), 1), 1.0, 0.0)
        C_ref[...] = C0_ref[...]
        m_ref[...] = m0_ref[...]
        for bi in range(ML_NB):
            xbuf[bi, SUBLANES - (CONV_W - 1):SUBLANES, :] = conv0_ref[bi]
            for hh in range(ML_HEADS):
                ncol[bi, hh] = _xdot(eye * n0_ref[bi, :, hh * ML_DH:(hh + 1) * ML_DH], ones)

    rows = [_mlstm_prefill_one(x_ref.at[bi], w_ref, pbuf.at[bi], cw_ref, cb_ref, wq_ref, wk_ref, wv_ref, wg_ref,
                               bg_ref, wgt_ref, bgt_ref, gng_ref, gnb_ref, skip_ref,
                               yc_ref.at[bi], conv_ref.at[bi], C_ref.at[bi], n_ref.at[bi], m_ref.at[bi],
                               xbuf.at[bi], ncol.at[bi], L=L) for bi in range(ML_NB)]
    _round_robin(rows)


def _mlstm_prefill_one(x_ref, w_ref, p_ref, cw_ref, cb_ref, wq_ref, wk_ref, wv_ref, wg_ref, bg_ref,
                       wgt_ref, bgt_ref, gng_ref, gnb_ref, skip_ref,
                       yc_ref, conv_ref, C_ref, n_ref, m_ref, xbuf, ncol, *, L):
    p_ref[...] = jnp.dot(x_ref[...].astype(BF16), w_ref[...], preferred_element_type=F32)
    xm_ref, z_ref = _Cols(p_ref, 0), _Cols(p_ref, D_HALF)
    yield
    xm = xm_ref[...]
    xc, tail = _conv_prefill(xm, xbuf, cw_ref, cb_ref, L)
    conv_ref[...] = tail
    xc = _silu(xc)
    yield
    q, k, v, g_col = _mlstm_qkv_gates(xm, xc, wq_ref, wk_ref, wv_ref, wg_ref, bg_ref)
    g_row = (_dot_nt(wgt_ref[:, 0:D_HALF], q) + _dot_nt(wgt_ref[:, D_HALF:2 * D_HALF], k)
             + _dot_nt(wgt_ref[:, 2 * D_HALF:3 * D_HALF], v) + bgt_ref[...])
    yield
    ri = lax.broadcasted_iota(jnp.int32, (L, L), 0)
    ci = lax.broadcasted_iota(jnp.int32, (L, L), 1)
    causal = ri >= ci
    tril = jnp.where(causal, 1.0, 0.0).astype(BF16)
    triu = jnp.where(ci >= ri, 1.0, 0.0).astype(BF16)
    ones = jnp.ones((L, LANES), BF16)
    eye = jnp.where(ri == ci, 1.0, 0.0)

    li_col = g_col
    lf_col = -_softplus(-pltpu.roll(g_col, LANES - ML_HEADS, 1))
    b_col = _xdot_l(tril, lf_col)
    lf_row = -_softplus(-g_row)
    b_row = _xdot(lf_row, triu)
    yield
    c_row = g_row[0:ML_HEADS, :] - b_row[ML_HEADS:2 * ML_HEADS, :]
    row = lax.broadcasted_iota(jnp.int32, (L, LANES), 0)
    pm = li_col - b_col
    sft = 1
    while sft < L:
        pm = jnp.maximum(pm, jnp.where(row >= sft, pltpu.roll(pm, sft, 0), -jnp.inf))
        sft *= 2
    m_prev = m_ref[...]
    u_col = -jnp.maximum(pm, m_prev)
    m_t_col = b_col - u_col
    e_col = jnp.exp(-m_t_col)
    m_new = m_t_col[L - 1:L, :]
    b_last = b_col[L - 1:L, :]
    wk_col = jnp.exp(b_last - b_col + li_col - m_new)
    wC_row = jnp.exp(b_last + m_prev - m_new)
    m_ref[...] = m_new
    yield
    vones = ones
    rep = lambda col, hh: jnp.broadcast_to(col[:, hh:hh + 1], (L, LANES))
    H = range(ML_HEADS)
    sls = [slice(hh * ML_DH, (hh + 1) * ML_DH) for hh in H]
    qh = [q[:, sl].astype(BF16) for sl in sls]
    kh = [k[:, sl] * (ML_DH ** -0.5) for sl in sls]
    vh1 = [jnp.concatenate([v[:, sl].astype(BF16), vones], axis=-1) for sl in sls]
    qk = [_dot_nt(qh[hh], kh[hh]) for hh in H]
    CN = [jnp.concatenate([C_ref[hh], ncol[hh]], axis=-1) for hh in H]
    qc = [jnp.dot(qh[hh], CN[hh].astype(BF16), preferred_element_type=F32) for hh in H]
    kw = [kh[hh] * rep(wk_col, hh) for hh in H]
    upd = [lax.dot_general(kw[hh].astype(BF16), vh1[hh], (((0,), (0,)), ((), ())), preferred_element_type=F32)
           for hh in H]
    yield
    for hh in H:
        w_C =jnp.broadcast_to(wC_row[:, hh:hh + 1], (ML_DH, 2 * ML_DH))
        CNn = w_C * CN[hh] + upd[hh]
        C_ref[hh] = CNn[:, 0:ML_DH]
        ncol[hh] = CNn[:, ML_DH:2 * ML_DH]
        n_ref[:, sls[hh]] = jnp.sum(CNn[:, ML_DH:2 * ML_DH] * eye, axis=0, keepdims=True)
    yield
    u = [rep(u_col, hh) for hh in H]
    s = [(qk[hh] * jnp.exp(jnp.where(causal, u[hh] + c_row[hh:hh + 1, :], -jnp.inf))).astype(BF16) for hh in H]
    sv = [jnp.dot(s[hh], vh1[hh], preferred_element_type=F32) for hh in H]
    yield
    hcell = []
    for hh in H:
        w_inter = jnp.exp(rep(m_prev, hh) + u[hh])
        num = sv[hh][:, 0:ML_DH] + w_inter * qc[hh][:, 0:ML_DH]
        den = sv[hh][:, ML_DH:2 * ML_DH] + w_inter * qc[hh][:, ML_DH:2 * ML_DH]
        hcell.append(num / jnp.maximum(jnp.abs(den), rep(e_col, hh)))
    mu = [jnp.dot(hcell[hh].astype(BF16), ones, preferred_element_type=F32) * (1.0 / ML_DH) for hh in H]
    yield
    oc = [hcell[hh] - mu[hh] for hh in H]
    var = [jnp.dot((oc[hh] * oc[hh]).astype(BF16), ones, preferred_element_type=F32) * (1.0 / ML_DH) for hh in H]
    for hh in H:
        sl = sls[hh]
        hn = oc[hh] * lax.rsqrt(var[hh] + GN_EPS) * gng_ref[:, sl] + gnb_ref[:, sl]
        yc_ref[:, sl] = (hn + skip_ref[:, sl] * xc[:, sl]) * _silu(z_ref[:, sl])


def _ml_dense(w):
    w2 = w.reshape(D_HALF, ML_QKV_BLOCK)
    c = jnp.arange(D_HALF)
    spread = (c[None, :] % ML_QKV_BLOCK == jnp.arange(ML_QKV_BLOCK)[:, None]).astype(w.dtype)
    full = jnp.dot(w2, spread, precision=lax.Precision.HIGHEST)
    same_block = c[:, None] // ML_QKV_BLOCK == c[None, :] // ML_QKV_BLOCK
    return jnp.where(same_block, full, 0.0).astype(BF16)


def _pad_lanes(a, n=LANES):
    return jnp.pad(a, [(0, 0)] * (a.ndim - 1) + [(0, n - a.shape[-1])])


def _mlstm_weights(Wl):
    wg = _pad_lanes(Wl['ml_w_gate']).astype(BF16)
    bg = _pad_lanes(Wl['ml_b_gate'].reshape(1, 2 * ML_HEADS))
    wgt = Wl['ml_w_gate'].T.astype(BF16)
    bgt = jnp.broadcast_to(Wl['ml_b_gate'].reshape(2 * ML_HEADS, 1), (2 * ML_HEADS, LANES))
    return wg, bg, wgt, bgt


def _mlstm_prefill(x3, w_in_all, j, st, Wl):
    B, T, _ = x3.shape
    L = CHUNK
    nc = T // L
    wg, bg, wgt, bgt = _mlstm_weights(Wl)
    nb = ML_NB
    perb = lambda *s: pl.BlockSpec((nb,) + s, lambda b, c: (b,) + (0,) * len(s))
    row2 = lambda a: a.reshape(1, D_HALF)
    ins = [x3, w_in_all, Wl['ml_conv_w'], row2(Wl['ml_conv_b']), Wl['ml_wq_d'], Wl['ml_wk_d'], Wl['ml_wv_d'],
           wg, bg, wgt, bgt, row2(Wl['ml_gn_g']), row2(Wl['ml_gn_b']), row2(Wl['ml_skip']),
           st['ml_conv'], st['ml_C'], st['ml_n'].reshape(B, 1, D_HALF),
           _pad_lanes(st['ml_m']).reshape(B, 1, LANES)]
    in_specs = [pl.BlockSpec((nb, L, D_MODEL), lambda b, c: (b, c, 0)),
                pl.BlockSpec((None, D_MODEL, 2 * D_HALF), lambda b, c: (j, 0, 0)),
                _full((CONV_W, D_HALF)), _full((1, D_HALF)),
                _full((D_HALF, D_HALF)), _full((D_HALF, D_HALF)), _full((D_HALF, D_HALF)),
                _full((3 * D_HALF, LANES)), _full((1, LANES)), _full((2 * ML_HEADS, 3 * D_HALF)),
                _full((2 * ML_HEADS, LANES)), _full((1, D_HALF)), _full((1, D_HALF)), _full((1, D_HALF)),
                perb(CONV_W - 1, D_HALF), perb(ML_HEADS, ML_DH, ML_DH), perb(1, D_HALF), perb(1, LANES)]
    seq = pl.BlockSpec((nb, L, D_HALF), lambda b, c: (b, c, 0))
    yc, conv, C, n, m = pl.pallas_call(
        functools.partial(_mlstm_prefill_kernel, L=L),
        grid=(B // nb, nc),
        in_specs=in_specs,
        out_specs=[seq, perb(CONV_W - 1, D_HALF), perb(ML_HEADS, ML_DH, ML_DH), perb(1, D_HALF), perb(1, LANES)],
        out_shape=[jax.ShapeDtypeStruct((B, T, D_HALF), F32),
                   jax.ShapeDtypeStruct((B, CONV_W - 1, D_HALF), F32),
                   jax.ShapeDtypeStruct((B, ML_HEADS, ML_DH, ML_DH), F32),
                   jax.ShapeDtypeStruct((B, 1, D_HALF), F32),
                   jax.ShapeDtypeStruct((B, 1, LANES), F32)],
        scratch_shapes=[pltpu.VMEM((nb, L + SUBLANES, D_HALF), F32),
                        pltpu.VMEM((nb, ML_HEADS, ML_DH, LANES), F32),
                        pltpu.VMEM((nb, L, 2 * D_HALF), F32)],
        compiler_params=_params(("parallel", "arbitrary")),
        name="mlstm_prefill",
    )(*ins)
    return yc, (C, n.reshape(B, ML_HEADS, ML_DH), m[:, 0, :ML_HEADS], conv)


def _mlstm_decode_kernel(p_ref, cw_ref, cb_ref, wq_ref, wk_ref, wv_ref, wg_ref, bg_ref,
                         gng_ref, gnb_ref, skip_ref, conv0_ref, C0_ref, n0_ref, m0_ref,
                         yc_ref, conv_ref, C_ref, n_ref, m_ref):
    xm = p_ref[:, 0:D_HALF]
    z = p_ref[:, D_HALF:2 * D_HALF]
    xc = cb_ref[...] + cw_ref[CONV_W - 1:CONV_W, :] * xm
    for i in range(CONV_W - 1):
        xc = xc + cw_ref[i:i + 1, :] * conv0_ref[i]
    for i in range(CONV_W - 2):
        conv_ref[i] = conv0_ref[i + 1]
    conv_ref[CONV_W - 2] = xm
    xc = _silu(xc)
    q, k, v, g = _mlstm_qkv_gates(xm, xc, wq_ref, wk_ref, wv_ref, wg_ref, bg_ref)
    lf_all = -_softplus(-g)
    lane = lax.broadcasted_iota(jnp.int32, (1, LANES), 1)
    row8 = lax.broadcasted_iota(jnp.int32, (SUBLANES, ML_DH), 0)
    m_all = m0_ref[...]
    m_out = m_all
    for hh in range(ML_HEADS):
        sl = slice(hh * ML_DH, (hh + 1) * ML_DH)
        qh, vh = q[:, sl], v[:, sl]
        kh = k[:, sl] * (ML_DH ** -0.5)
        li = g[:, hh:hh + 1]
        lf = lf_all[:, ML_HEADS + hh:ML_HEADS + hh + 1]
        m_prev = m_all[:, hh:hh + 1]
        n = n0_ref[:, sl]
        log_inter = lf + m_prev
        m_t = jnp.maximum(li, log_inter)
        s = jnp.sum(qh * kh, -1, keepdims=True) * jnp.exp(li - m_t)
        w_inter = jnp.exp(log_inter - m_t)
        w_k = jnp.exp(li - m_t)
        w_C = jnp.exp(log_inter - m_t)
        kw = kh * w_k
        rows = []
        for bi in range(DEC_BB):
            C = C0_ref[bi, hh]
            q8 = jnp.broadcast_to(qh[bi:bi + 1, :], (SUBLANES, ML_DH))
            rows.append(_dot_f32(q8, C)[0:1, :])
            k8 = jnp.where(row8 == 0, jnp.broadcast_to(kw[bi:bi + 1, :], (SUBLANES, ML_DH)), 0.0)
            v8 = jnp.broadcast_to(vh[bi:bi + 1, :], (SUBLANES, ML_DH))
            C_ref[bi, hh] = w_C[bi:bi + 1, :] * C + _dot_tn_f32(k8, v8)
        qC = jnp.concatenate(rows, axis=0)
        num = s * vh + w_inter * qC
        den = s + w_inter * jnp.sum(qh * n, -1, keepdims=True)
        hcell = num / jnp.maximum(jnp.abs(den), jnp.exp(-m_t))
        n_ref[:, sl] = w_C * n + kw
        m_out = jnp.where(lane == hh, m_t, m_out)
        hn = _head_norm(hcell, gng_ref[:, sl], gnb_ref[:, sl], GN_EPS)
        yc_ref[:, sl] = (hn + skip_ref[:, sl] * xc[:, sl]) * _silu(z[:, sl])
    m_ref[...] = m_out


def _mlstm_decode(p2, st, Wl, C_all, j, C_prev):
    B = p2.shape[0]
    bb = DEC_BB
    wg, bg, _, _ = _mlstm_weights(Wl)
    row2 = lambda a: a.reshape(1, D_HALF)
    rows = lambda n: pl.BlockSpec((bb, n), lambda i: (i, 0))
    convs = pl.BlockSpec((CONV_W - 1, bb, D_HALF), lambda i: (0, i, 0))
    ins = [p2, Wl['ml_conv_w'], row2(Wl['ml_conv_b']), Wl['ml_wq_d'], Wl['ml_wk_d'], Wl['ml_wv_d'], wg, bg,
           row2(Wl['ml_gn_g']), row2(Wl['ml_gn_b']), row2(Wl['ml_skip']),
           jnp.swapaxes(st['ml_conv'], 0, 1), C_all, st['ml_n'].reshape(B, D_HALF), _pad_lanes(st['ml_m'])]
    kern, Cs, extra_in, extra_specs, aliases, C_shape = _stacked_state_io(
        _mlstm_decode_kernel, C_all, C_prev, j, bb, len(ins) - 3, len(ins), 2)
    in_specs = [pl.BlockSpec((bb, 2 * D_HALF), lambda i: (i, 0)), _full((CONV_W, D_HALF)), _full((1, D_HALF)),
                _full((D_HALF, D_HALF)), _full((D_HALF, D_HALF)), _full((D_HALF, D_HALF)),
                _full((3 * D_HALF, LANES)), _full((1, LANES)),
                _full((1, D_HALF)), _full((1, D_HALF)), _full((1, D_HALF)),
                convs, Cs, rows(D_HALF), rows(LANES)] + extra_specs
    yc, conv, C, n, m = pl.pallas_call(
        kern,
        grid=(B // bb,),
        in_specs=in_specs,
        out_specs=[rows(D_HALF), convs, Cs, rows(D_HALF), rows(LANES)],
        out_shape=[jax.ShapeDtypeStruct((B, D_HALF), F32),
                   jax.ShapeDtypeStruct((CONV_W - 1, B, D_HALF), F32),
                   C_shape,
                   jax.ShapeDtypeStruct((B, D_HALF), F32),
                   jax.ShapeDtypeStruct((B, LANES), F32)],
        input_output_aliases=aliases,
        compiler_params=_params(("parallel",)),
        name="mlstm_decode",
    )(*ins, *extra_in)
    return yc, (C, n.reshape(B, ML_HEADS, ML_DH), m[:, :ML_HEADS], jnp.swapaxes(conv, 0, 1))


def _rwkv_pre_body(pr, pr_prev, mu_ref, w0_ref, a0_ref, w2_ref, a2_ref, g2_ref, kkw_ref, kaw_ref, rk_ref,
                   r_ref, d_ref, k_ref, v_ref, a_ref, b_ref, g_ref, bonus_ref):
    pm = pr + (pr_prev - pr) * mu_ref[...]
    r = pm[:, 0:D_HALF]
    kr = pm[:, D_HALF:2 * D_HALF]
    vr = pm[:, 2 * D_HALF:3 * D_HALF]
    lo = pm[:, 3 * D_HALF:RW_COLS_PAD]
    w_log = -_softplus(-(w0_ref[...] + _dot(jnp.tanh(lo), w2_ref[...]))) - 0.5
    a = _sigmoid(a0_ref[...] + _dot(lo, a2_ref[...]))
    g = _dot(_sigmoid(lo), g2_ref[...])
    ones_bd = _group_ones(LANES, RW_DH)
    kk = kr * kkw_ref[...]
    kk = kk / jnp.maximum(jnp.sqrt(_group_sum(kk * kk, ones_bd)), 1e-12)
    kh = kr * (1.0 + (a - 1.0) * kaw_ref[...])
    r_ref[...] = r
    d_ref[...] = jnp.exp(-jnp.exp(w_log))
    k_ref[...] = kh
    v_ref[...] = vr
    a_ref[...] = -kk
    b_ref[...] = kk * a
    g_ref[...] = g
    bonus_ref[...] = _group_sum(r * kh * rk_ref[...], ones_bd) * vr


def _rwkv_pre_prefill_kernel(pr_ref, shift0_ref, *rest, L):
    wrefs, outs, xbuf = rest[:9], rest[9:17], rest[17]
    c = pl.program_id(1)

    @pl.when(c == 0)
    def _():
        xbuf[SUBLANES - 1:SUBLANES, :] = shift0_ref[...]

    pr = pr_ref[...]
    xbuf[SUBLANES:SUBLANES + L, :] = pr
    pr_prev = xbuf[SUBLANES - 1:SUBLANES - 1 + L, :]
    xbuf[SUBLANES - 1:SUBLANES, :] = pr[L - 1:L, :]
    _rwkv_pre_body(pr, pr_prev, *wrefs, *outs)


def _rwkv_pre_decode_kernel(pr_ref, prev_ref, *rest):
    _rwkv_pre_body(pr_ref[...], prev_ref[...], *rest[:9], *rest[9:17])


def _rwkv_pre_weights(Wl):
    row2 = lambda a: a.reshape(1, D_HALF)
    padr = lambda w, o: jnp.pad(w, ((o, RW_LORA_PAD - o - w.shape[0]), (0, 0))).astype(BF16)
    mu = _pad_lanes(Wl['rw_mu'].reshape(1, RW_SHIFT_COLS), RW_COLS_PAD)
    ws = [mu, row2(Wl['rw_w0']), row2(Wl['rw_a0']),
          padr(Wl['rw_w2'], 0), padr(Wl['rw_a2'], RW_DECAY_LORA), padr(Wl['rw_g2'], RW_DECAY_LORA + RW_A_LORA),
          row2(Wl['rw_kk']), row2(Wl['rw_ka']), row2(Wl['rw_rk'])]
    specs = [_full((1, RW_COLS_PAD)), _full((1, D_HALF)), _full((1, D_HALF)),
             _full((RW_LORA_PAD, D_HALF)), _full((RW_LORA_PAD, D_HALF)), _full((RW_LORA_PAD, D_HALF)),
             _full((1, D_HALF)), _full((1, D_HALF)), _full((1, D_HALF))]
    return ws, specs


def _rwkv_pre_prefill(pr3, shift0, Wl):
    B, T, _ = pr3.shape
    L = CHUNK
    ws, wspecs = _rwkv_pre_weights(Wl)
    seq = pl.BlockSpec((None, L, D_HALF), lambda b, c: (b, c, 0))
    outs = pl.pallas_call(
        functools.partial(_rwkv_pre_prefill_kernel, L=L),
        grid=(B, T // L),
        in_specs=[pl.BlockSpec((None, L, RW_COLS_PAD), lambda b, c: (b, c, 0)),
                  pl.BlockSpec((None, 1, RW_COLS_PAD), lambda b, c: (b, 0, 0))] + wspecs,
        out_specs=[seq] * 8,
        out_shape=[jax.ShapeDtypeStruct((B, T, D_HALF), F32)] * 8,
        scratch_shapes=[pltpu.VMEM((L + SUBLANES, RW_COLS_PAD), F32)],
        compiler_params=_params(("parallel", "arbitrary")),
        name="rwkv_pre_prefill",
    )(pr3, _pad_lanes(shift0, RW_COLS_PAD).reshape(B, 1, RW_COLS_PAD), *ws)
    return outs


def _rwkv_pre_decode(pr, shift0, Wl):
    B = pr.shape[0]
    ws, wspecs = _rwkv_pre_weights(Wl)
    full2 = lambda n: pl.BlockSpec((B, n), lambda i: (0, 0))
    outs = pl.pallas_call(
        _rwkv_pre_decode_kernel,
        grid=(1,),
        in_specs=[full2(RW_COLS_PAD), full2(RW_COLS_PAD)] + wspecs,
        out_specs=[full2(D_HALF)] * 8,
        out_shape=[jax.ShapeDtypeStruct((B, D_HALF), F32)] * 8,
        compiler_params=_params(("arbitrary",)),
        name="rwkv_pre_decode",
    )(pr, _pad_lanes(shift0, RW_COLS_PAD), *ws)
    return outs


RW_IP = RW_DH // 2


def _rwkv_rec_kernel(r_ref, d_ref, k_ref, a_ref, b_ref, v_ref, S0_ref, y_ref, S_ref, *, Tc):
    @pl.when(pl.program_id(0) == 0)
    def _():
        S_ref[...] = S0_ref[...]

    lane = lax.broadcasted_iota(jnp.int32, (1, LANES), 1)

    def tiles(t):
        back = (LANES - (t % RW_TB) * RW_BB) % LANES
        out = []
        for ref in (a_ref, d_ref, b_ref, k_ref, r_ref):
            raw = ref[t]
            out.append(jnp.where(lane < LANES // 2, pltpu.roll(raw, back, 1),
                                 pltpu.roll(raw, (back + LANES // 2) % LANES, 1)))
        out.append(pltpu.roll(v_ref[t], back, 1))
        return tuple(out)

    def step(t, carry):
        a, d, b, k, r, vt = carry
        nxt = tiles(jnp.minimum(t + 1, Tc - 1))
        rows = []
        for ip in range(RW_IP):
            S = S_ref[ip]
            sa = jnp.sum(S * a, axis=0, keepdims=True)
            Sn = S * d + sa * b + vt[ip:ip + 1, :] * k
            S_ref[ip] = Sn
            rows.append(jnp.sum(Sn * r, axis=0, keepdims=True))
        y_ref[t] = jnp.concatenate(rows, axis=0)
        return nxt

    lax.fori_loop(0, Tc, step, tiles(0))


def _rwkv_rec_call(r, d, k, a, b, v, S0):
    T = r.shape[0]
    Tc = min(T, 32)
    vec = pl.BlockSpec((Tc, RW_DH, LANES), lambda c: (c, 0, 0))
    vsp = pl.BlockSpec((Tc, RW_IP, LANES), lambda c: (c, 0, 0))
    ssp = _full((RW_IP, RW_DH, LANES))
    return pl.pallas_call(
        functools.partial(_rwkv_rec_kernel, Tc=Tc),
        grid=(T // Tc,),
        in_specs=[vec] * 5 + [vsp, ssp],
        out_specs=[vsp, ssp],
        out_shape=[jax.ShapeDtypeStruct((T, RW_IP, LANES), F32),
                   jax.ShapeDtypeStruct((RW_IP, RW_DH, LANES), F32)],
        compiler_params=_params(("arbitrary",)),
        name="rwkv_recurrence",
    )(r, d, k, a, b, v, S0)


def _rwkv_dec_kernel(r_ref, d_ref, k_ref, a_ref, b_ref, v_ref, S0_ref, y_ref, S_ref):
    a, d, b, k, r = a_ref[...], d_ref[...], b_ref[...], k_ref[...], r_ref[...]
    v = v_ref[...]
    rows = []
    for i in range(RW_DH):
        S = S0_ref[i]
        sa = jnp.sum(S * a, axis=0, keepdims=True)
        Sn = S * d + sa * b + v[i:i + 1, :] * k
        S_ref[i] = Sn
        rows.append(jnp.sum(Sn * r, axis=0, keepdims=True))
    y_ref[...] = jnp.concatenate(rows, axis=0)


def _rwkv_decode_step(r, d, k, v, a, b, S0):
    B = r.shape[0]
    tr = lambda x: x.T.reshape(RW_HEADS, RW_DH, B)
    St = S0.reshape(B, RW_HEADS * RW_DH * RW_DH).T.reshape(RW_HEADS, RW_DH, RW_DH, B)
    vec = pl.BlockSpec((None, RW_DH, B), lambda h: (h, 0, 0))
    ssp = pl.BlockSpec((None, RW_DH, RW_DH, B), lambda h: (h, 0, 0, 0))
    y, S = pl.pallas_call(
        _rwkv_dec_kernel,
        grid=(RW_HEADS,),
        in_specs=[vec] * 6 + [ssp],
        out_specs=[vec, ssp],
        out_shape=[jax.ShapeDtypeStruct((RW_HEADS, RW_DH, B), F32),
                   jax.ShapeDtypeStruct((RW_HEADS, RW_DH, RW_DH, B), F32)],
        compiler_params=_params(("parallel",)),
        name="rwkv_decode_step",
    )(tr(r), tr(d), tr(k), tr(a), tr(b), tr(v), St)
    y = y.reshape(D_HALF, B).T
    S = S.reshape(RW_HEADS * RW_DH * RW_DH, B).T.reshape(B, RW_HEADS, RW_DH, RW_DH)
    return y, S


RW_TB = LANES // RW_BB
RW_NSB = 2


def _head_sum_rows(x):
    x3 = x.reshape(RW_HEADS, RW_DH, x.shape[-1])
    s = jnp.sum(x3, axis=1, keepdims=True)
    return jnp.broadcast_to(s, x3.shape).reshape(x.shape)


def _rwkv_pre_t_kernel(x_ref, shift0_ref, w_ref, mu_ref, w0_ref, a0_ref, w2_ref, a2_ref, g2_ref,
                       kkw_ref, kaw_ref, rk_ref,
                       r_ref, d_ref, k_ref, a_ref, b_ref, v_ref, g_ref, bonus_ref, last_ref, prev_scr):
    @pl.when(pl.program_id(0) == 0)
    def _():
        prev_scr[...] = shift0_ref[...]

    ro = lax.broadcasted_iota(jnp.int32, (LANES, LANES), 0)
    ci = lax.broadcasted_iota(jnp.int32, (LANES, LANES), 1)
    perm = jnp.where(ci == (ro % RW_BB) * RW_TB + ro // RW_BB, 1.0, 0.0).astype(BF16)
    lane = lax.broadcasted_iota(jnp.int32, (1, LANES), 1)
    grp = lane // RW_BB
    ngrp = LANES // RW_BB
    prs = {}

    def scatter(x, o_ref, t0, nrow, npiece):
        rot = [x[q * nrow:(q + 1) * nrow, :] if q == 0 else pltpu.roll(x[q * nrow:(q + 1) * nrow, :], q * RW_BB, 1)
               for q in range(npiece)]
        for t in range(RW_TB):
            m = rot[0]
            for q in range(1, npiece):
                m = jnp.where(grp == (t + q) % ngrp, rot[q], m)
            o_ref[t0 + t] = m

    def block(sb):
        t0 = sb * RW_TB
        xn = x_ref[:, t0:t0 + RW_TB, :].reshape(RW_BB * RW_TB, D_MODEL).astype(BF16)
        xg = jnp.dot(perm, xn, preferred_element_type=F32).astype(BF16)
        pr = lax.dot_general(w_ref[...], xg, (((1,), (1,)), ((), ())), preferred_element_type=F32)
        prs[sb] = pr
        yield
        rolled = pltpu.roll(pr, RW_BB, 1)
        before = prev_scr[...] if sb == 0 else prs[sb - 1]
        prev = jnp.where(lane < RW_BB, before, rolled)
        prs[sb] = rolled
        if sb == RW_NSB - 1:
            prev_scr[...] = rolled
            last_ref[...] = pr
        pm = pr + (prev - pr) * mu_ref[...]
        r = pm[0:D_HALF]
        kr = pm[D_HALF:2 * D_HALF]
        vr = pm[2 * D_HALF:3 * D_HALF]
        lo = pm[3 * D_HALF:RW_COLS_PAD]
        yield
        w_log = -_softplus(-(w0_ref[...] + _dot(w2_ref[...], jnp.tanh(lo)))) - 0.5
        a = _sigmoid(a0_ref[...] + _dot(a2_ref[...], lo))
        g = _dot(g2_ref[...], _sigmoid(lo))
        kk = kr * kkw_ref[...]
        kk = kk / jnp.maximum(jnp.sqrt(_head_sum_rows(kk * kk)), 1e-12)
        kh = kr * (1.0 + (a - 1.0) * kaw_ref[...])
        g_ref[sb] = g
        bonus_ref[sb] = _head_sum_rows(r * kh * rk_ref[...]) * vr
        yield
        scatter(r, r_ref, t0, RW_DH, RW_HEADS)
        yield
        scatter(jnp.exp(-jnp.exp(w_log)), d_ref, t0, RW_DH, RW_HEADS)
        yield
        scatter(kh, k_ref, t0, RW_DH, RW_HEADS)
        yield
        scatter(-kk, a_ref, t0, RW_DH, RW_HEADS)
        yield
        scatter(kk * a, b_ref, t0, RW_DH, RW_HEADS)
        yield
        vv = jnp.concatenate([vr[h * RW_DH + half * RW_IP:h * RW_DH + (half + 1) * RW_IP, :]
                              for half in range(2) for h in range(RW_HEADS)], axis=0)
        scatter(vv, v_ref, t0, RW_IP, ngrp)

    _skewed([block(sb) for sb in range(RW_NSB)])


def _lane_bcast(a, n):
    return jnp.broadcast_to(a.reshape(n, 1), (n, LANES))


def _rwkv_pre_t(x3, shift0, w_rwt, j, Wl):
    B, T, _ = x3.shape
    nblk = T // RW_TB
    padr = lambda w, o: jnp.pad(w, ((o, RW_LORA_PAD - o - w.shape[0]), (0, 0))).astype(BF16).T
    sh = jnp.pad(shift0.T, ((0, RW_COLS_PAD - RW_SHIFT_COLS), (0, LANES - RW_BB)))
    col = lambda a: _lane_bcast(a, D_HALF)
    ins = [x3, sh, w_rwt, _lane_bcast(_pad_lanes(Wl['rw_mu'].reshape(1, -1), RW_COLS_PAD), RW_COLS_PAD),
           col(Wl['rw_w0']), col(Wl['rw_a0']),
           padr(Wl['rw_w2'], 0), padr(Wl['rw_a2'], RW_DECAY_LORA), padr(Wl['rw_g2'], RW_DECAY_LORA + RW_A_LORA),
           col(Wl['rw_kk']), col(Wl['rw_ka']), col(Wl['rw_rk'])]
    tb = RW_TB * RW_NSB
    in_specs = [pl.BlockSpec((B, tb, D_MODEL), lambda c: (0, c, 0)), _full((RW_COLS_PAD, LANES)),
                pl.BlockSpec((None, RW_COLS_PAD, D_MODEL), lambda c: (j, 0, 0)), _full((RW_COLS_PAD, LANES)),
                _full((D_HALF, LANES)), _full((D_HALF, LANES)),
                _full((D_HALF, RW_LORA_PAD)), _full((D_HALF, RW_LORA_PAD)), _full((D_HALF, RW_LORA_PAD)),
                _full((D_HALF, LANES)), _full((D_HALF, LANES)), _full((D_HALF, LANES))]
    blk = pl.BlockSpec((RW_NSB, D_HALF, LANES), lambda c: (c, 0, 0))
    ktile = pl.BlockSpec((tb, RW_DH, LANES), lambda c: (c, 0, 0))
    vtile = pl.BlockSpec((tb, RW_IP, LANES), lambda c: (c, 0, 0))
    outs = pl.pallas_call(
        _rwkv_pre_t_kernel,
        grid=(T // tb,),
        in_specs=in_specs,
        out_specs=[ktile] * 5 + [vtile, blk, blk, _full((RW_COLS_PAD, LANES))],
        out_shape=[jax.ShapeDtypeStruct((T, RW_DH, LANES), F32)] * 5
                  + [jax.ShapeDtypeStruct((T, RW_IP, LANES), F32)]
                  + [jax.ShapeDtypeStruct((nblk, D_HALF, LANES), F32)] * 2
                  + [jax.ShapeDtypeStruct((RW_COLS_PAD, LANES), F32)],
        scratch_shapes=[pltpu.VMEM((RW_COLS_PAD, LANES), F32)],
        compiler_params=_params(("arbitrary",)),
        name="rwkv_pre_t",
    )(*ins)
    shift_new = outs[8][:RW_SHIFT_COLS, LANES - RW_BB:].T
    return outs[:8], shift_new


RW_NPB = 4


def _rwkv_post_t_kernel(y_ref, g_ref, bonus_ref, gng_ref, gnb_ref, yd_ref):
    _round_robin([_rwkv_post_t_block(y_ref, g_ref, bonus_ref, gng_ref, gnb_ref, yd_ref, pb)
                  for pb in range(RW_NPB)])


def _rwkv_post_t_block(y_ref, g_ref, bonus_ref, gng_ref, gnb_ref, yd_ref, pb):
    t0 = pb * RW_TB
    lane = lax.broadcasted_iota(jnp.int32, (1, LANES), 1)
    grp = lane // RW_BB
    ngrp = LANES // RW_BB
    ys = [y_ref[t0 + t] for t in range(RW_TB)]
    rolled = []
    for s in range(ngrp):
        m = ys[s % RW_TB]
        for q in range(1, ngrp):
            m = jnp.where(grp == q, ys[(q + s) % RW_TB], m)
        rolled.append(pltpu.roll(m, s * RW_BB, 1) if s else m)
    yield
    pieces = {}
    for q in range(ngrp):
        m = rolled[(-q) % ngrp]
        for t in range(1, RW_TB):
            m = jnp.where(grp == t, rolled[(t - q) % ngrp], m)
        pieces[divmod(q, RW_HEADS)] = m
    y = jnp.concatenate([pieces[(half, h)] for h in range(RW_HEADS) for half in range(2)], axis=0)
    yield
    mu = _head_sum_rows(y) * (1.0 / RW_DH)
    yc = y - mu
    var = _head_sum_rows(yc * yc) * (1.0 / RW_DH)
    hn = yc * lax.rsqrt(var + RW_GN_EPS) * gng_ref[...] + gnb_ref[...]
    yd = ((hn + bonus_ref[pb]) * g_ref[pb]).T
    yield
    for t in range(RW_TB):
        yd_ref[:, t0 + t, :] = yd[t * RW_BB:(t + 1) * RW_BB, :]


def _rwkv_rec_t(vecs, S0, Wl, T):
    r, d, k, a, b, v, g, bonus = vecs
    nblk = T // RW_TB
    B = RW_BB
    Sr = S0.reshape(B, RW_HEADS, 2, RW_IP, RW_DH).transpose(3, 4, 2, 1, 0).reshape(RW_IP, RW_DH, LANES)
    y, S = _rwkv_rec_call(r, d, k, a, b, v, Sr)
    tb = RW_TB * RW_NPB
    blk = pl.BlockSpec((RW_NPB, D_HALF, LANES), lambda c: (c, 0, 0))
    col = lambda a_: _lane_bcast(a_, D_HALF)
    yd = pl.pallas_call(
        _rwkv_post_t_kernel,
        grid=(nblk // RW_NPB,),
        in_specs=[pl.BlockSpec((tb, RW_IP, LANES), lambda c: (c, 0, 0)), blk, blk,
                  _full((D_HALF, LANES)), _full((D_HALF, LANES))],
        out_specs=pl.BlockSpec((B, tb, D_HALF), lambda c: (0, c, 0)),
        out_shape=jax.ShapeDtypeStruct((B, T, D_HALF), F32),
        compiler_params=_params(("parallel",)),
        name="rwkv_post_t",
    )(y, g, bonus, col(Wl['rw_gn_g']), col(Wl['rw_gn_b']))
    S = S.reshape(RW_IP, RW_DH, 2, RW_HEADS, B).transpose(4, 3, 2, 0, 1).reshape(B, RW_HEADS, RW_DH, RW_DH)
    return yd, S


def _rwkv_post_kernel(y_ref, g_ref, bonus_ref, gng_ref, gnb_ref, o_ref):
    ones_bd = _group_ones(LANES, RW_DH)
    y = y_ref[...]
    mu = _group_sum(y, ones_bd) * (1.0 / RW_DH)
    yc = y - mu
    var = _group_sum(yc * yc, ones_bd) * (1.0 / RW_DH)
    hn = yc * lax.rsqrt(var + RW_GN_EPS) * gng_ref[...] + gnb_ref[...]
    o_ref[...] = (hn + bonus_ref[...]) * g_ref[...]


def _rwkv_post(y2, g2, bonus2, Wl, tm):
    M = y2.shape[0]
    row = pl.BlockSpec((tm, D_HALF), lambda i: (i, 0))
    return pl.pallas_call(
        _rwkv_post_kernel,
        grid=(M // tm,),
        in_specs=[row, row, row, _full((1, D_HALF)), _full((1, D_HALF))],
        out_specs=row,
        out_shape=jax.ShapeDtypeStruct((M, D_HALF), F32),
        compiler_params=_params(("parallel",)),
        name="rwkv_post",
    )(y2, g2, bonus2, Wl['rw_gn_g'].reshape(1, D_HALF), Wl['rw_gn_b'].reshape(1, D_HALF))


def _trunk(x, st, pos, W):
    B, T, _ = x.shape
    M = B * T
    decode = T == 1
    tm_proj = min(M, 512)
    tm_post = min(M, 2 * POST_SUB)
    tm_rw = min(M, 512)
    x2 = x.reshape(M, D_MODEL)
    new = {name: [] for name in st}
    stacked = {}
    for l in range(DEPTH):
        j = l // 2
        Wl = {name: v[j] for name, v in W['per_pair'][l % 2].items()}
        stl = {name: v[j] for name, v in st.items()}
        if l % 2 == 0:
            if decode:
                p, = _proj(x2, W['ev_w_in'], j, tm_proj, (EVEN_IN,))
                ya, yb, (h, cb, S) = _even_decode(p, pos, stl, Wl, st['ret_S'], j, stacked.get('ret_S'))
                stacked['ret_S'] = S
            else:
                ya, yb, (h, cb, S) = _even_prefill(x2.reshape(B, T, D_MODEL), W['ev_w_in'], j, pos, stl, Wl)
                new['ret_S'].append(S)
            new['lru_h'].append(h)
            new['lru_conv'].append(cb)
            wout = W['ev_w_out']
        else:
            if decode:
                p_ml, p_rw = _proj(x2, W['od_w_in'], j, tm_proj, (2 * D_HALF, RW_COLS_PAD))
                ya, (C, n, m, cb) = _mlstm_decode(p_ml, stl, Wl, st['ml_C'], j, stacked.get('ml_C'))
                stacked['ml_C'] = C
                r, d, k, v, a, b, g, bonus = _rwkv_pre_decode(p_rw, stl['rw_shift'], Wl)
                y, S = _rwkv_decode_step(r, d, k, v, a, b, stl['rw_S'])
                yb = _rwkv_post(y, g, bonus, Wl, tm_rw)
                shift_new = p_rw[:, :RW_SHIFT_COLS]
            else:
                x3 = x2.reshape(B, T, D_MODEL)
                ya, (C, n, m, cb) = _mlstm_prefill(x3, W['od_w_in'], j, stl, Wl)
                vecs, shift_new = _rwkv_pre_t(x3, stl['rw_shift'], W['od_w_rwt'], j, Wl)
                yb, S = _rwkv_rec_t(vecs, stl['rw_S'], Wl, T)
                new['ml_C'].append(C)
            new['ml_n'].append(n)
            new['ml_m'].append(m)
            new['ml_conv'].append(cb)
            new['rw_S'].append(S)
            new['rw_shift'].append(shift_new)
            wout = W['od_w_out']
        x2 = _post(ya.reshape(M, D_HALF), yb.reshape(M, D_HALF), x2, wout, j, l,
                   W['ln1_g'], W['ln1_b'], W['mlp_w1'], W['mlp_w2'], W['ln2_g'], W['ln2_b'], tm_post)
    out = {name: stacked[name] if name in stacked else jnp.stack(v) for name, v in new.items()}
    return x2.reshape(B, T, D_MODEL), out


def _prepare_weights(w):
    even_names = ('lru_conv_w', 'lru_conv_b', 'lru_ba', 'lru_bx', 'lru_lambda', 'ret_gn_g', 'ret_gn_b')
    odd_names = ('ml_conv_w', 'ml_conv_b', 'ml_w_gate', 'ml_b_gate', 'ml_gn_g', 'ml_gn_b', 'ml_skip',
                 'rw_mu', 'rw_w0', 'rw_w2', 'rw_a0', 'rw_a2', 'rw_g2', 'rw_kk', 'rw_ka', 'rw_rk', 'rw_gn_g', 'rw_gn_b')
    even = {n: w[n] for n in even_names}
    even.update(lru_wa_d=jax.vmap(_lru_dense)(w['lru_wa']), lru_wx_d=jax.vmap(_lru_dense)(w['lru_wx']))
    odd = {n: w[n] for n in odd_names}
    odd.update(ml_wq_d=jax.vmap(_ml_dense)(w['ml_wq']), ml_wk_d=jax.vmap(_ml_dense)(w['ml_wk']),
               ml_wv_d=jax.vmap(_ml_dense)(w['ml_wv']))
    od_w_in = _pad_lanes(w['od_w_in'], ODD_IN_PAD).astype(BF16)
    return dict(per_pair=(even, odd),
                ev_w_in=w['ev_w_in'].astype(BF16), ev_w_out=w['ev_w_out'].astype(BF16),
                od_w_in=od_w_in, od_w_rwt=jnp.swapaxes(od_w_in[:, :, 2 * D_HALF:], 1, 2),
                od_w_out=w['od_w_out'].astype(BF16),
                mlp_w1=w['mlp_w1'].astype(BF16), mlp_w2=w['mlp_w2'].astype(BF16),
                ln1_g=w['ln1_g'], ln1_b=w['ln1_b'], ln2_g=w['ln2_g'], ln2_b=w['ln2_b'])


def _zero_states(batch):
    z = lambda *s: jnp.zeros(s, F32)
    n_even, n_odd = (DEPTH + 1) // 2, DEPTH // 2
    return dict(lru_h=z(n_even, batch, D_HALF), lru_conv=z(n_even, batch, CONV_W - 1, D_HALF),
                ret_S=z(n_even, batch, RET_HEADS, RET_DH, RET_DH),
                ml_C=z(n_odd, batch, ML_HEADS, ML_DH, ML_DH), ml_n=z(n_odd, batch, ML_HEADS, ML_DH),
                ml_m=z(n_odd, batch, ML_HEADS), ml_conv=z(n_odd, batch, CONV_W - 1, D_HALF),
                rw_S=z(n_odd, batch, RW_HEADS, RW_DH, RW_DH), rw_shift=z(n_odd, batch, RW_SHIFT_COLS))


def kernel(x_prompt, x_sample, state_lru_h, state_lru_conv, state_ret, state_mlstm_C, state_mlstm_n, state_mlstm_m, state_mlstm_conv, state_rwkv_S, state_rwkv_shift, ln1_g, ln1_b, ln2_g, ln2_b, mlp_w1, mlp_w2, ev_w_in, ev_w_out, lru_conv_w, lru_conv_b, lru_wa, lru_ba, lru_wx, lru_bx, lru_lambda, ret_gn_g, ret_gn_b, od_w_in, od_w_out, ml_conv_w, ml_conv_b, ml_wq, ml_wk, ml_wv, ml_w_gate, ml_b_gate, ml_gn_g, ml_gn_b, ml_skip, rw_mu, rw_w0, rw_w2, rw_a0, rw_a2, rw_g2, rw_kk, rw_ka, rw_rk, rw_gn_g, rw_gn_b):
    W = _prepare_weights(dict(
        ln1_g=ln1_g, ln1_b=ln1_b, ln2_g=ln2_g, ln2_b=ln2_b, mlp_w1=mlp_w1, mlp_w2=mlp_w2,
        ev_w_in=ev_w_in, ev_w_out=ev_w_out, lru_conv_w=lru_conv_w, lru_conv_b=lru_conv_b,
        lru_wa=lru_wa, lru_ba=lru_ba, lru_wx=lru_wx, lru_bx=lru_bx, lru_lambda=lru_lambda,
        ret_gn_g=ret_gn_g, ret_gn_b=ret_gn_b, od_w_in=od_w_in, od_w_out=od_w_out,
        ml_conv_w=ml_conv_w, ml_conv_b=ml_conv_b, ml_wq=ml_wq, ml_wk=ml_wk, ml_wv=ml_wv,
        ml_w_gate=ml_w_gate, ml_b_gate=ml_b_gate, ml_gn_g=ml_gn_g, ml_gn_b=ml_gn_b, ml_skip=ml_skip,
        rw_mu=rw_mu, rw_w0=rw_w0, rw_w2=rw_w2, rw_a0=rw_a0, rw_a2=rw_a2, rw_g2=rw_g2,
        rw_kk=rw_kk, rw_ka=rw_ka, rw_rk=rw_rk, rw_gn_g=rw_gn_g, rw_gn_b=rw_gn_b))
    st_sample = dict(lru_h=state_lru_h, lru_conv=state_lru_conv, ret_S=state_ret,
                     ml_C=state_mlstm_C, ml_n=state_mlstm_n, ml_m=state_mlstm_m, ml_conv=state_mlstm_conv,
                     rw_S=state_rwkv_S, rw_shift=state_rwkv_shift)
    pos_prompt = jnp.arange(x_prompt.shape[1], dtype=jnp.int32)
    pos_sample = PAST_LEN + jnp.arange(x_sample.shape[1], dtype=jnp.int32)
    y_prompt, sp = _trunk(x_prompt, _zero_states(x_prompt.shape[0]), pos_prompt, W)
    y_sample, ss = _trunk(x_sample, st_sample, pos_sample, W)
    names = ('lru_h', 'lru_conv', 'ret_S', 'ml_C', 'ml_n', 'ml_m', 'ml_conv', 'rw_S', 'rw_shift')
    return (y_prompt, y_sample) + tuple(sp[n] for n in names) + tuple(ss[n] for n in names)
```

```python
import functools

import jax
import jax.numpy as jnp
from jax import lax
from jax.experimental import pallas as pl
from jax.experimental.pallas import tpu as pltpu

F32 = jnp.float32
BF16 = jnp.bfloat16

D_MODEL = 1024
DEPTH = 4
PAST_LEN = 16384
D_HALF = D_MODEL // 2
CONV_W = 4
LRU_BLOCKS = 8
LRU_BLOCK = D_HALF // LRU_BLOCKS
LRU_C = 8.0
RET_HEADS = 4
RET_DH = D_HALF // RET_HEADS
CHUNK = 128
ROPE_BASE = 10000.0
ML_HEADS = 4
ML_DH = D_HALF // ML_HEADS
ML_QKV_BLOCK = 4
ML_NBLK = D_HALF // ML_QKV_BLOCK
RW_HEADS = 8
RW_DH = D_HALF // RW_HEADS
RW_DECAY_LORA = 32
RW_A_LORA = 32
RW_GATE_LORA = 96
RW_LORA = RW_DECAY_LORA + RW_A_LORA + RW_GATE_LORA
RW_SHIFT_COLS = 3 * D_HALF + RW_LORA
D_FF = 4 * D_MODEL
ALPHA = (2.0 * DEPTH) ** 0.25
EVEN_IN = 6 * D_HALF
ODD_IN = 2 * D_HALF + RW_SHIFT_COLS
LN_EPS = 1e-5
GN_EPS = 1e-5
RW_GN_EPS = 64e-5

LANES = 128
SUBLANES = 8
RW_LORA_PAD = 2 * LANES
RW_COLS_PAD = 3 * D_HALF + RW_LORA_PAD
ODD_IN_PAD = 2 * D_HALF + RW_COLS_PAD
RW_BB = 8
VMEM_LIMIT = 56 * 1024 * 1024


def _params(sem):
    return pltpu.CompilerParams(dimension_semantics=sem, vmem_limit_bytes=VMEM_LIMIT)


def _dot(a, b):
    return jnp.dot(a.astype(BF16), b.astype(BF16), preferred_element_type=F32)


def _dot_nt(a, b):
    return lax.dot_general(a.astype(BF16), b.astype(BF16), (((1,), (1,)), ((), ())),
                           preferred_element_type=F32)


def _dot_tn(a, b):
    return lax.dot_general(a.astype(BF16), b.astype(BF16), (((0,), (0,)), ((), ())),
                           preferred_element_type=F32)


def _split3(a):
    hi = a.astype(BF16)
    r1 = a - hi.astype(F32)
    mid = r1.astype(BF16)
    lo = (r1 - mid.astype(F32)).astype(BF16)
    return hi, mid, lo


def _xdot(a, b01):
    hi, mid, lo = _split3(a)
    f = lambda t: jnp.dot(t, b01, preferred_element_type=F32)
    return f(hi) + f(mid) + f(lo)


def _xdot_l(b01, a):
    hi, mid, lo = _split3(a)
    f = lambda t: jnp.dot(b01, t, preferred_element_type=F32)
    return f(hi) + f(mid) + f(lo)


def _sigmoid(x):
    return 1.0 / (1.0 + jnp.exp(-x))


def _silu(x):
    return x * _sigmoid(x)


def _softplus(x):
    return jnp.maximum(x, 0.0) + jnp.log1p(jnp.exp(-jnp.abs(x)))


def _gelu_tanh(x):
    return 0.5 * x * (1.0 + jnp.tanh(0.7978845608028654 * (x + 0.044715 * (x * x * x))))


def _layer_norm(x, g, b, eps):
    mu = jnp.mean(x, -1, keepdims=True)
    xc = x - mu
    var = jnp.mean(xc * xc, -1, keepdims=True)
    return xc * lax.rsqrt(var + eps) * g + b


def _group_ones(n, group):
    r = lax.broadcasted_iota(jnp.int32, (n, n), 0) // group
    c = lax.broadcasted_iota(jnp.int32, (n, n), 1) // group
    return jnp.where(r == c, 1.0, 0.0).astype(BF16)


def _group_sum(x, ones_bd):
    parts = [_xdot(x[:, s * LANES:(s + 1) * LANES], ones_bd) for s in range(x.shape[1] // LANES)]
    return jnp.concatenate(parts, axis=-1)


def _rotate(x, cosf, sinf):
    return x * cosf + pltpu.roll(x, RET_DH // 2, 1) * sinf


def _proj_kernel(x_ref, w_ref, *o_refs, splits):
    xb = x_ref[...].astype(BF16)
    off = 0
    for o_ref, n in zip(o_refs, splits):
        o_ref[...] = jnp.dot(xb, w_ref[:, off:off + n], preferred_element_type=F32)
        off += n


def _proj(x2d, w_all, j, tm, splits):
    M = x2d.shape[0]
    N = w_all.shape[2]
    return pl.pallas_call(
        functools.partial(_proj_kernel, splits=splits),
        grid=(M // tm,),
        in_specs=[pl.BlockSpec((tm, D_MODEL), lambda i: (i, 0)),
                  pl.BlockSpec((None, D_MODEL, N), lambda i: (j, 0, 0))],
        out_specs=[pl.BlockSpec((tm, n), lambda i: (i, 0)) for n in splits],
        out_shape=[jax.ShapeDtypeStruct((M, n), F32) for n in splits],
        compiler_params=_params(("parallel",)),
        name="proj_in",
    )(x2d, w_all)


FF_CHUNK = 1024


POST_SUB = 256


def _post_kernel(ya_ref, yb_ref, x_ref, wo_ref, g1_ref, b1_ref, w1_ref, w2_ref, g2_ref, b2_ref, o_ref):
    tm = x_ref.shape[0]
    sub = min(tm, POST_SUB)

    def tile(r0):
        rows = slice(r0, r0 + sub)
        y = (jnp.dot(ya_ref[rows, :].astype(BF16), wo_ref[0:D_HALF, :], preferred_element_type=F32)
             + jnp.dot(yb_ref[rows, :].astype(BF16), wo_ref[D_HALF:D_MODEL, :], preferred_element_type=F32))
        yield
        x1 = _layer_norm(ALPHA * x_ref[rows, :] + y, g1_ref[...], b1_ref[...], LN_EPS)
        x1b = x1.astype(BF16)
        acc = jnp.zeros(x1.shape, F32)
        for c in range(D_FF // FF_CHUNK):
            yield
            h = jnp.dot(x1b, w1_ref[:, c * FF_CHUNK:(c + 1) * FF_CHUNK], preferred_element_type=F32)
            yield
            h = jnp.square(jnp.maximum(h, 0.0))
            acc = acc + jnp.dot(h.astype(BF16), w2_ref[c * FF_CHUNK:(c + 1) * FF_CHUNK, :],
                                preferred_element_type=F32)
        yield
        o_ref[rows, :] = _layer_norm(ALPHA * x1 + acc, g2_ref[...], b2_ref[...], LN_EPS)

    _skewed([tile(r0) for r0 in range(0, tm, sub)])


def _post(ya, yb, x2d, wout_all, j, l, ln1_g, ln1_b, w1_all, w2_all, ln2_g, ln2_b, tm):
    M = x2d.shape[0]
    row = lambda i: (i, 0)
    vec = pl.BlockSpec((None, 1, D_MODEL), lambda i: (l, 0, 0))
    r3 = lambda a: a.reshape(DEPTH, 1, D_MODEL)
    return pl.pallas_call(
        _post_kernel,
        grid=(M // tm,),
        in_specs=[pl.BlockSpec((tm, D_HALF), row), pl.BlockSpec((tm, D_HALF), row),
                  pl.BlockSpec((tm, D_MODEL), row),
                  pl.BlockSpec((None, D_MODEL, D_MODEL), lambda i: (j, 0, 0)),
                  vec, vec,
                  pl.BlockSpec((None, D_MODEL, D_FF), lambda i: (l, 0, 0)),
                  pl.BlockSpec((None, D_FF, D_MODEL), lambda i: (l, 0, 0)),
                  vec, vec],
        out_specs=pl.BlockSpec((tm, D_MODEL), row),
        out_shape=jax.ShapeDtypeStruct((M, D_MODEL), F32),
        compiler_params=_params(("parallel",)),
        name="post_mlp",
    )(ya, yb, x2d, wout_all, r3(ln1_g), r3(ln1_b), w1_all, w2_all, r3(ln2_g), r3(ln2_b))


def _conv_prefill(x, xbuf, cw_ref, cb_ref, L):
    xbuf[SUBLANES:SUBLANES + L, :] = x
    y = cb_ref[...]
    for i in range(CONV_W):
        y = y + cw_ref[i:i + 1, :] * xbuf[SUBLANES - (CONV_W - 1) + i:SUBLANES - (CONV_W - 1) + i + L, :]
    tail = xbuf[L + SUBLANES - (CONV_W - 1):L + SUBLANES, :]
    xbuf[SUBLANES - (CONV_W - 1):SUBLANES, :] = tail
    return y, tail


def _lru_gates(xc, wa_ref, ba_ref, wx_ref, bx_ref, lam_ref):
    xcb = xc.astype(BF16)
    nslab = D_HALF // LANES
    ra = jnp.concatenate([jnp.dot(xcb[:, s * LANES:(s + 1) * LANES], wa_ref[s], preferred_element_type=F32)
                          for s in range(nslab)], axis=-1)
    rx = jnp.concatenate([jnp.dot(xcb[:, s * LANES:(s + 1) * LANES], wx_ref[s], preferred_element_type=F32)
                          for s in range(nslab)], axis=-1)
    r = _sigmoid(ra + ba_ref[...])
    i = _sigmoid(rx + bx_ref[...])
    log_a = -LRU_C * r * _softplus(-lam_ref[...])
    a = jnp.exp(log_a)
    t = jnp.tanh(log_a)
    u = jnp.sqrt(-2.0 * t / (1.0 - t)) * (i * xc)
    return a, u


def _head_norm(o, g, b, eps):
    mu = jnp.mean(o, -1, keepdims=True)
    oc = o - mu
    var = jnp.mean(oc * oc, -1, keepdims=True)
    return oc * lax.rsqrt(var + eps) * g + b


EV_NB = 4


class _Cols:
    def __init__(self, ref, off):
        self.ref, self.off = ref, off

    def __getitem__(self, idx):
        if idx is Ellipsis:
            return self.ref[:, self.off:self.off + D_HALF]
        rows, cols = idx
        return self.ref[rows, self.off + cols.start:self.off + cols.stop]


def _even_prefill_kernel(x_ref, w_ref,
                         cw_ref, cb_ref, wa_ref, ba_ref, wx_ref, bx_ref, lam_ref,
                         cos_ref, sin_ref, dmask_ref, qd_ref, kd_ref, cd_ref, gng_ref, gnb_ref,
                         h0_ref, conv0_ref, S0_ref,
                         ya_ref, yb_ref, h_ref, conv_ref, S_ref, xbuf, pbuf, *, L):
    @pl.when(pl.program_id(1) == 0)
    def _():
        h_ref[...] = h0_ref[...]
        S_ref[...] = S0_ref[...]
        xbuf[:, SUBLANES - (CONV_W - 1):SUBLANES, :] = conv0_ref[...]

    rows = [_even_prefill_one(x_ref.at[bi], w_ref, pbuf.at[bi],
                              cw_ref, cb_ref, wa_ref, ba_ref, wx_ref, bx_ref, lam_ref,
                              cos_ref, sin_ref, dmask_ref, qd_ref, kd_ref, cd_ref, gng_ref, gnb_ref,
                              ya_ref.at[bi], yb_ref.at[bi], h_ref.at[bi], conv_ref.at[bi], S_ref.at[bi],
                              xbuf.at[bi], L=L) for bi in range(EV_NB)]
    _round_robin(rows)


def _even_prefill_one(x_ref, w_ref, p_ref,
                      cw_ref, cb_ref, wa_ref, ba_ref, wx_ref, bx_ref, lam_ref,
                      cos_ref, sin_ref, dmask_ref, qd_ref, kd_ref, cd_ref, gng_ref, gnb_ref,
                      ya_ref, yb_ref, h_ref, conv_ref, S_ref, xbuf, *, L):
    p_ref[...] = jnp.dot(x_ref[...].astype(BF16), w_ref[...], preferred_element_type=F32)
    xa_ref, ga_ref, q_ref, k_ref, v_ref, gb_ref = (_Cols(p_ref, i * D_HALF) for i in range(6))
    yield
    H = range(RET_HEADS)
    sls = [slice(hh * RET_DH, (hh + 1) * RET_DH) for hh in H]
    cosf = cos_ref[...]
    sinf = sin_ref[...]
    qh = [_rotate(q_ref[:, sl], cosf, sinf) for sl in sls]
    kh = [_rotate(k_ref[:, sl], cosf, sinf) * (RET_DH ** -0.5) for sl in sls]
    vh = [v_ref[:, sl].astype(BF16) for sl in sls]
    qk = [_dot_nt(qh[hh], kh[hh]) for hh in H]
    S = [S_ref[hh] for hh in H]
    qS = [_dot(qh[hh] * qd_ref[:, sls[hh]], S[hh]) for hh in H]
    kv = [_dot_tn(kh[hh] * kd_ref[:, sls[hh]], vh[hh]) for hh in H]
    yield
    xc, tail = _conv_prefill(xa_ref[...], xbuf, cw_ref, cb_ref, L)
    conv_ref[...] = tail
    a, u = _lru_gates(xc, wa_ref, ba_ref, wx_ref, bx_ref, lam_ref)
    yield
    for hh in H:
        S_ref[hh] = S[hh] * cd_ref[hh] + kv[hh]
    sc = [(qk[hh] * dmask_ref[hh]).astype(BF16) for hh in H]
    o = [jnp.dot(sc[hh], vh[hh], preferred_element_type=F32) + qS[hh] for hh in H]
    yield
    row = lax.broadcasted_iota(jnp.int32, (L, D_HALF), 0) % SUBLANES
    s = 1
    while s < SUBLANES:
        keep = row >= s
        a_sh = jnp.where(keep, pltpu.roll(a, s, 0), 1.0)
        u_sh = jnp.where(keep, pltpu.roll(u, s, 0), 0.0)
        u = a * u_sh + u
        a = a * a_sh
        s *= 2
        yield
    carry = h_ref[...]
    groups = []
    for g in range(L // SUBLANES):
        rows8 = slice(g * SUBLANES, (g + 1) * SUBLANES)
        hg = a[rows8, :] * carry + u[rows8, :]
        carry = hg[SUBLANES - 1:SUBLANES, :]
        groups.append(hg)
        if g % 4 == 3:
            yield
    h = jnp.concatenate(groups, axis=0)
    h_ref[...] = carry
    ya_ref[...] = _gelu_tanh(ga_ref[...]) * h
    ones = jnp.ones((RET_DH, LANES), BF16)
    mu = [jnp.dot(o[hh].astype(BF16), ones, preferred_element_type=F32) * (1.0 / RET_DH) for hh in H]
    yield
    oc = [o[hh] - mu[hh] for hh in H]
    var = [jnp.dot((oc[hh] * oc[hh]).astype(BF16), ones, preferred_element_type=F32) * (1.0 / RET_DH) for hh in H]
    yield
    for hh in H:
        sl = sls[hh]
        on = oc[hh] * lax.rsqrt(var[hh] + GN_EPS) * gng_ref[:, sl] + gnb_ref[:, sl]
        yb_ref[:, sl] = _silu(gb_ref[:, sl]) * on


def _ret_tables(L):
    log_gamma = jnp.log1p(-jnp.exp2(-5.0 - jnp.arange(RET_HEADS, dtype=F32)))
    idx = jnp.arange(L, dtype=F32)
    diff = idx[:, None] - idx[None, :]
    dmask = jnp.where(diff >= 0, jnp.exp(log_gamma[:, None, None] * jnp.maximum(diff, 0.0)), 0.0)
    qd = jnp.exp(log_gamma[:, None] * (idx + 1.0))
    kd = jnp.exp(log_gamma[:, None] * (L - 1.0 - idx))
    cd = jnp.exp(log_gamma * L)
    qd_full = jnp.repeat(qd.T, RET_DH, axis=1)
    kd_full = jnp.repeat(kd.T, RET_DH, axis=1)
    cd_full = jnp.broadcast_to(cd[:, None, None], (RET_HEADS, 1, RET_DH))
    return dmask, qd_full, kd_full, cd_full


def _rope_tables(pos):
    half = RET_DH // 2
    inv = ROPE_BASE ** (-jnp.arange(half, dtype=F32) / half)
    ang = pos.astype(F32)[:, None] * inv[None, :]
    cos, sin = jnp.cos(ang), jnp.sin(ang)
    return jnp.concatenate([cos, cos], -1), jnp.concatenate([-sin, sin], -1)


def _lru_dense(w):
    pairs = LANES // LRU_BLOCK
    w4 = w.reshape(LRU_BLOCKS // pairs, pairs, LRU_BLOCK, LRU_BLOCK)
    eye = jnp.eye(pairs, dtype=w.dtype)
    d = w4[:, :, :, None, :] * eye[None, :, None, :, None]
    return d.reshape(LRU_BLOCKS // pairs, LANES, LANES).astype(BF16)


def _full(shape):
    n = len(shape)
    return pl.BlockSpec(shape, lambda *_: (0,) * n)


def _even_prefill(x3, w_in_all, j, pos, st, Wl):
    B, T, _ = x3.shape
    L = CHUNK
    nc = T // L
    dmask, qd, kd, cd = _ret_tables(L)
    cosf, sinf = _rope_tables(pos)
    nb = EV_NB
    perb = lambda *s: pl.BlockSpec((nb,) + s, lambda b, c: (b,) + (0,) * len(s))
    row2 = lambda a: a.reshape(1, D_HALF)
    ins = [x3, w_in_all, Wl['lru_conv_w'], row2(Wl['lru_conv_b']), Wl['lru_wa_d'], row2(Wl['lru_ba']),
           Wl['lru_wx_d'], row2(Wl['lru_bx']), row2(Wl['lru_lambda']),
           cosf, sinf, dmask, qd, kd, cd, row2(Wl['ret_gn_g']), row2(Wl['ret_gn_b']),
           st['lru_h'].reshape(B, 1, D_HALF), st['lru_conv'], st['ret_S']]
    in_specs = [pl.BlockSpec((nb, L, D_MODEL), lambda b, c: (b, c, 0)),
                pl.BlockSpec((None, D_MODEL, EVEN_IN), lambda b, c: (j, 0, 0))] + [
        _full((CONV_W, D_HALF)), _full((1, D_HALF)), _full((4, LANES, LANES)), _full((1, D_HALF)),
        _full((4, LANES, LANES)), _full((1, D_HALF)), _full((1, D_HALF)),
        pl.BlockSpec((L, RET_DH), lambda b, c: (c, 0)), pl.BlockSpec((L, RET_DH), lambda b, c: (c, 0)),
        _full((RET_HEADS, L, L)), _full((L, D_HALF)), _full((L, D_HALF)), _full((RET_HEADS, 1, RET_DH)),
        _full((1, D_HALF)), _full((1, D_HALF)),
        perb(1, D_HALF), perb(CONV_W - 1, D_HALF), perb(RET_HEADS, RET_DH, RET_DH)]
    seq = pl.BlockSpec((nb, L, D_HALF), lambda b, c: (b, c, 0))
    ya, yb, h, conv, S = pl.pallas_call(
        functools.partial(_even_prefill_kernel, L=L),
        grid=(B // nb, nc),
        in_specs=in_specs,
        out_specs=[seq, seq, perb(1, D_HALF), perb(CONV_W - 1, D_HALF), perb(RET_HEADS, RET_DH, RET_DH)],
        out_shape=[jax.ShapeDtypeStruct((B, T, D_HALF), F32), jax.ShapeDtypeStruct((B, T, D_HALF), F32),
                   jax.ShapeDtypeStruct((B, 1, D_HALF), F32),
                   jax.ShapeDtypeStruct((B, CONV_W - 1, D_HALF), F32),
                   jax.ShapeDtypeStruct((B, RET_HEADS, RET_DH, RET_DH), F32)],
        scratch_shapes=[pltpu.VMEM((nb, L + SUBLANES, D_HALF), F32), pltpu.VMEM((nb, L, EVEN_IN), F32)],
        compiler_params=_params(("parallel", "arbitrary")),
        name="even_prefill",
    )(*ins)
    return ya, yb, (h.reshape(B, D_HALF), conv, S)


DEC_BB = 8


def _even_decode_kernel(p_ref, cw_ref, cb_ref, wa_ref, ba_ref, wx_ref, bx_ref, lam_ref,
                        cos_ref, sin_ref, dm_ref, qd_ref, kd_ref, cd_ref, gng_ref, gnb_ref,
                        h0_ref, conv0_ref, S0_ref,
                        ya_ref, yb_ref, h_ref, conv_ref, S_ref):
    col = lambda i: p_ref[:, i * D_HALF:(i + 1) * D_HALF]
    xa = col(0)
    xc = cb_ref[...] + cw_ref[CONV_W - 1:CONV_W, :] * xa
    for i in range(CONV_W - 1):
        xc = xc + cw_ref[i:i + 1, :] * conv0_ref[i]
    for i in range(CONV_W - 2):
        conv_ref[i] = conv0_ref[i + 1]
    conv_ref[CONV_W - 2] = xa
    a, u = _lru_gates(xc, wa_ref, ba_ref, wx_ref, bx_ref, lam_ref)
    h = a * h0_ref[...] + u
    h_ref[...] = h
    ya_ref[...] = _gelu_tanh(col(1)) * h

    cosf = cos_ref[...]
    sinf = sin_ref[...]
    row8 = lax.broadcasted_iota(jnp.int32, (SUBLANES, RET_DH), 0)
    q, k, v, gb = col(2), col(3), col(4), col(5)
    for hh in range(RET_HEADS):
        sl = slice(hh * RET_DH, (hh + 1) * RET_DH)
        qh = _rotate(q[:, sl], cosf, sinf)
        kh = _rotate(k[:, sl], cosf, sinf) * (RET_DH ** -0.5)
        vh = v[:, sl]
        qk = jnp.sum(qh * kh, -1, keepdims=True) * dm_ref[:, sl]
        qq = qh * qd_ref[:, sl]
        kk = kh * kd_ref[:, sl]
        rows = []
        for bi in range(DEC_BB):
            S = S0_ref[bi, hh]
            q8 = jnp.broadcast_to(qq[bi:bi + 1, :], (SUBLANES, RET_DH))
            rows.append(_dot_f32(q8, S)[0:1, :])
            k8 = jnp.where(row8 == 0, jnp.broadcast_to(kk[bi:bi + 1, :], (SUBLANES, RET_DH)), 0.0)
            v8 = jnp.broadcast_to(vh[bi:bi + 1, :], (SUBLANES, RET_DH))
            S_ref[bi, hh] = S * cd_ref[hh] + _dot_tn_f32(k8, v8)
        o = qk * vh + jnp.concatenate(rows, axis=0)
        on = _head_norm(o, gng_ref[:, sl], gnb_ref[:, sl], GN_EPS)
        yb_ref[:, sl] = _silu(gb[:, sl]) * on


def _dot_f32(a, b):
    return jnp.dot(a, b, preferred_element_type=F32)


def _dot_tn_f32(a, b):
    return lax.dot_general(a, b, (((0,), (0,)), ((), ())), preferred_element_type=F32)


def _skip_ref(kernel, pos):
    def wrapped(*refs):
        kernel(*refs[:pos], *refs[pos + 1:])
    return wrapped


def _all_layers(kernel, pos_in, pos_out, j, n_layers):
    def wrapped(*refs):
        refs = list(refs)
        s_in, s_out = refs[pos_in], refs[pos_out]
        for other in range(n_layers):
            if other != j:
                s_out[other] = s_in[other]
        refs[pos_in], refs[pos_out] = s_in.at[j], s_out.at[j]
        kernel(*refs)
    return wrapped


def _stacked_state_io(kernel, S_all, S_prev, j, bb, pos_in, n_in, out_idx):
    n_layers, tail = S_all.shape[0], S_all.shape[2:]
    zeros = (0,) * len(tail)
    shape = jax.ShapeDtypeStruct(S_all.shape, F32)
    if S_prev is None:
        spec = pl.BlockSpec((n_layers, bb) + tail, lambda i: (0, i) + zeros)
        return _all_layers(kernel, pos_in, n_in + out_idx, j, n_layers), spec, [], [], {}, shape
    spec = pl.BlockSpec((None, bb) + tail, lambda i: (j, i) + zeros)
    return _skip_ref(kernel, n_in), spec, [S_prev], [pl.BlockSpec(memory_space=pl.ANY)], {n_in: out_idx}, shape


def _even_decode(p2, pos, st, Wl, S_all, j, S_prev):
    B = p2.shape[0]
    bb = DEC_BB
    dmask, qd, kd, cd = _ret_tables(1)
    dm = jnp.repeat(dmask[:, 0, :].T, RET_DH, axis=1)
    cosf, sinf = _rope_tables(pos)
    row2 = lambda a: a.reshape(1, D_HALF)
    rows = lambda n: pl.BlockSpec((bb, n), lambda i: (i, 0))
    convs = pl.BlockSpec((CONV_W - 1, bb, D_HALF), lambda i: (0, i, 0))
    ins = [p2, Wl['lru_conv_w'], row2(Wl['lru_conv_b']), Wl['lru_wa_d'], row2(Wl['lru_ba']),
           Wl['lru_wx_d'], row2(Wl['lru_bx']), row2(Wl['lru_lambda']),
           cosf, sinf, dm, qd, kd, cd, row2(Wl['ret_gn_g']), row2(Wl['ret_gn_b']),
           st['lru_h'], jnp.swapaxes(st['lru_conv'], 0, 1), S_all]
    kern, Ss, extra_in, extra_specs, aliases, S_shape = _stacked_state_io(
        _even_decode_kernel, S_all, S_prev, j, bb, len(ins) - 1, len(ins), 4)
    in_specs = [rows(EVEN_IN), _full((CONV_W, D_HALF)), _full((1, D_HALF)), _full((4, LANES, LANES)),
                _full((1, D_HALF)), _full((4, LANES, LANES)), _full((1, D_HALF)), _full((1, D_HALF)),
                _full((1, RET_DH)), _full((1, RET_DH)), _full((1, D_HALF)), _full((1, D_HALF)),
                _full((1, D_HALF)), _full((RET_HEADS, 1, RET_DH)), _full((1, D_HALF)), _full((1, D_HALF)),
                rows(D_HALF), convs, Ss] + extra_specs
    ya, yb, h, conv, S = pl.pallas_call(
        kern,
        grid=(B // bb,),
        in_specs=in_specs,
        out_specs=[rows(D_HALF), rows(D_HALF), rows(D_HALF), convs, Ss],
        out_shape=[jax.ShapeDtypeStruct((B, D_HALF), F32), jax.ShapeDtypeStruct((B, D_HALF), F32),
                   jax.ShapeDtypeStruct((B, D_HALF), F32),
                   jax.ShapeDtypeStruct((CONV_W - 1, B, D_HALF), F32), S_shape],
        input_output_aliases=aliases,
        compiler_params=_params(("parallel",)),
        name="even_decode",
    )(*ins, *extra_in)
    return ya, yb, (h, jnp.swapaxes(conv, 0, 1), S)


def _mlstm_qkv_gates(xm, xc, wq_ref, wk_ref, wv_ref, wg_ref, bg_ref):
    q = _dot(xc, wq_ref[...])
    k = _dot(xc, wk_ref[...])
    v = _dot(xm, wv_ref[...])
    g_col = (_dot(q, wg_ref[0:D_HALF, :]) + _dot(k, wg_ref[D_HALF:2 * D_HALF, :])
             + _dot(v, wg_ref[2 * D_HALF:3 * D_HALF, :]) + bg_ref[...])
    return q, k, v, g_col


ML_NB = 4


def _skewed(gens, lag=1):
    live = []
    pending = list(gens)
    tick = 0
    while pending or live:
        if pending and tick % lag == 0:
            live.append(pending.pop(0))
        tick += 1
        nxt = []
        for g in live:
            try:
                next(g)
                nxt.append(g)
            except StopIteration:
                pass
        live = nxt


def _round_robin(gens):
    gens = list(gens)
    while gens:
        alive = []
        for g in gens:
            try:
                next(g)
                alive.append(g)
            except StopIteration:
                pass
        gens = alive


def _mlstm_prefill_kernel(x_ref, w_ref, cw_ref, cb_ref, wq_ref, wk_ref, wv_ref, wg_ref, bg_ref,
                          wgt_ref, bgt_ref, gng_ref, gnb_ref, skip_ref,
                          conv0_ref, C0_ref, n0_ref, m0_ref,
                          yc_ref, conv_ref, C_ref, n_ref, m_ref, xbuf, ncol, pbuf, *, L):
    @pl.when(pl.program_id(1) == 0)
    def _():
        ones = jnp.ones((ML_DH, LANES), BF16)
        eye = jnp.where(lax.broadcasted_iota(jnp.int32, (ML_DH, LANES), 0)
                        == lax.broadcasted_iota(jnp.int32, (ML_DH, LANES), 1), 1.0, 0.0)
        C_ref[...] = C0_ref[...]
        m_ref[...] = m0_ref[...]
        for bi in range(ML_NB):
            xbuf[bi, SUBLANES - (CONV_W - 1):SUBLANES, :] = conv0_ref[bi]
            for hh in range(ML_HEADS):
                ncol[bi, hh] = _xdot(eye * n0_ref[bi, :, hh * ML_DH:(hh + 1) * ML_DH], ones)

    rows = [_mlstm_prefill_one(x_ref.at[bi], w_ref, pbuf.at[bi], cw_ref, cb_ref, wq_ref, wk_ref, wv_ref, wg_ref,
                               bg_ref, wgt_ref, bgt_ref, gng_ref, gnb_ref, skip_ref,
                               yc_ref.at[bi], conv_ref.at[bi], C_ref.at[bi], n_ref.at[bi], m_ref.at[bi],
                               xbuf.at[bi], ncol.at[bi], L=L) for bi in range(ML_NB)]
    _round_robin(rows)


def _mlstm_prefill_one(x_ref, w_ref, p_ref, cw_ref, cb_ref, wq_ref, wk_ref, wv_ref, wg_ref, bg_ref,
                       wgt_ref, bgt_ref, gng_ref, gnb_ref, skip_ref,
                       yc_ref, conv_ref, C_ref, n_ref, m_ref, xbuf, ncol, *, L):
    p_ref[...] = jnp.dot(x_ref[...].astype(BF16), w_ref[...], preferred_element_type=F32)
    xm_ref, z_ref = _Cols(p_ref, 0), _Cols(p_ref, D_HALF)
    yield
    xm = xm_ref[...]
    xc, tail = _conv_prefill(xm, xbuf, cw_ref, cb_ref, L)
    conv_ref[...] = tail
    xc = _silu(xc)
    yield
    q, k, v, g_col = _mlstm_qkv_gates(xm, xc, wq_ref, wk_ref, wv_ref, wg_ref, bg_ref)
    g_row = (_dot_nt(wgt_ref[:, 0:D_HALF], q) + _dot_nt(wgt_ref[:, D_HALF:2 * D_HALF], k)
             + _dot_nt(wgt_ref[:, 2 * D_HALF:3 * D_HALF], v) + bgt_ref[...])
    yield
    ri = lax.broadcasted_iota(jnp.int32, (L, L), 0)
    ci = lax.broadcasted_iota(jnp.int32, (L, L), 1)
    causal = ri >= ci
    tril = jnp.where(causal, 1.0, 0.0).astype(BF16)
    triu = jnp.where(ci >= ri, 1.0, 0.0).astype(BF16)
    ones = jnp.ones((L, LANES), BF16)
    eye = jnp.where(ri == ci, 1.0, 0.0)

    li_col = g_col
    lf_col = -_softplus(-pltpu.roll(g_col, LANES - ML_HEADS, 1))
    b_col = _xdot_l(tril, lf_col)
    lf_row = -_softplus(-g_row)
    b_row = _xdot(lf_row, triu)
    yield
    c_row = g_row[0:ML_HEADS, :] - b_row[ML_HEADS:2 * ML_HEADS, :]
    row = lax.broadcasted_iota(jnp.int32, (L, LANES), 0)
    pm = li_col - b_col
    sft = 1
    while sft < L:
        pm = jnp.maximum(pm, jnp.where(row >= sft, pltpu.roll(pm, sft, 0), -jnp.inf))
        sft *= 2
    m_prev = m_ref[...]
    u_col = -jnp.maximum(pm, m_prev)
    m_t_col = b_col - u_col
    e_col = jnp.exp(-m_t_col)
    m_new = m_t_col[L - 1:L, :]
    b_last = b_col[L - 1:L, :]
    wk_col = jnp.exp(b_last - b_col + li_col - m_new)
    wC_row = jnp.exp(b_last + m_prev - m_new)
    m_ref[...] = m_new
    yield
    vones = ones
    rep = lambda col, hh: jnp.broadcast_to(col[:, hh:hh + 1], (L, LANES))
    H = range(ML_HEADS)
    sls = [slice(hh * ML_DH, (hh + 1) * ML_DH) for hh in H]
    qh = [q[:, sl].astype(BF16) for sl in sls]
    kh = [k[:, sl] * (ML_DH ** -0.5) for sl in sls]
    vh1 = [jnp.concatenate([v[:, sl].astype(BF16), vones], axis=-1) for sl in sls]
    qk = [_dot_nt(qh[hh], kh[hh]) for hh in H]
    CN = [jnp.concatenate([C_ref[hh], ncol[hh]], axis=-1) for hh in H]
    qc = [jnp.dot(qh[hh], CN[hh].astype(BF16), preferred_element_type=F32) for hh in H]
    kw = [kh[hh] * rep(wk_col, hh) for hh in H]
    upd = [lax.dot_general(kw[hh].astype(BF16), vh1[hh], (((0,), (0,)), ((), ())), preferred_element_type=F32)
           for hh in H]
    yield
    for hh in H:
        w_C =jnp.broadcast_to(wC_row[:, hh:hh + 1], (ML_DH, 2 * ML_DH))
        CNn = w_C * CN[hh] + upd[hh]
        C_ref[hh] = CNn[:, 0:ML_DH]
        ncol[hh] = CNn[:, ML_DH:2 * ML_DH]
        n_ref[:, sls[hh]] = jnp.sum(CNn[:, ML_DH:2 * ML_DH] * eye, axis=0, keepdims=True)
    yield
    u = [rep(u_col, hh) for hh in H]
    s = [(qk[hh] * jnp.exp(jnp.where(causal, u[hh] + c_row[hh:hh + 1, :], -jnp.inf))).astype(BF16) for hh in H]
    sv = [jnp.dot(s[hh], vh1[hh], preferred_element_type=F32) for hh in H]
    yield
    hcell = []
    for hh in H:
        w_inter = jnp.exp(rep(m_prev, hh) + u[hh])
        num = sv[hh][:, 0:ML_DH] + w_inter * qc[hh][:, 0:ML_DH]
        den = sv[hh][:, ML_DH:2 * ML_DH] + w_inter * qc[hh][:, ML_DH:2 * ML_DH]
        hcell.append(num / jnp.maximum(jnp.abs(den), rep(e_col, hh)))
    mu = [jnp.dot(hcell[hh].astype(BF16), ones, preferred_element_type=F32) * (1.0 / ML_DH) for hh in H]
    yield
    oc = [hcell[hh] - mu[hh] for hh in H]
    var = [jnp.dot((oc[hh] * oc[hh]).astype(BF16), ones, preferred_element_type=F32) * (1.0 / ML_DH) for hh in H]
    for hh in H:
        sl = sls[hh]
        hn = oc[hh] * lax.rsqrt(var[hh] + GN_EPS) * gng_ref[:, sl] + gnb_ref[:, sl]
        yc_ref[:, sl] = (hn + skip_ref[:, sl] * xc[:, sl]) * _silu(z_ref[:, sl])


def _ml_dense(w):
    w2 = w.reshape(D_HALF, ML_QKV_BLOCK)
    c = jnp.arange(D_HALF)
    spread = (c[None, :] % ML_QKV_BLOCK == jnp.arange(ML_QKV_BLOCK)[:, None]).astype(w.dtype)
    full = jnp.dot(w2, spread, precision=lax.Precision.HIGHEST)
    same_block = c[:, None] // ML_QKV_BLOCK == c[None, :] // ML_QKV_BLOCK
    return jnp.where(same_block, full, 0.0).astype(BF16)


def _pad_lanes(a, n=LANES):
    return jnp.pad(a, [(0, 0)] * (a.ndim - 1) + [(0, n - a.shape[-1])])


def _mlstm_weights(Wl):
    wg = _pad_lanes(Wl['ml_w_gate']).astype(BF16)
    bg = _pad_lanes(Wl['ml_b_gate'].reshape(1, 2 * ML_HEADS))
    wgt = Wl['ml_w_gate'].T.astype(BF16)
    bgt = jnp.broadcast_to(Wl['ml_b_gate'].reshape(2 * ML_HEADS, 1), (2 * ML_HEADS, LANES))
    return wg, bg, wgt, bgt


def _mlstm_prefill(x3, w_in_all, j, st, Wl):
    B, T, _ = x3.shape
    L = CHUNK
    nc = T // L
    wg, bg, wgt, bgt = _mlstm_weights(Wl)
    nb = ML_NB
    perb = lambda *s: pl.BlockSpec((nb,) + s, lambda b, c: (b,) + (0,) * len(s))
    row2 = lambda a: a.reshape(1, D_HALF)
    ins = [x3, w_in_all, Wl['ml_conv_w'], row2(Wl['ml_conv_b']), Wl['ml_wq_d'], Wl['ml_wk_d'], Wl['ml_wv_d'],
           wg, bg, wgt, bgt, row2(Wl['ml_gn_g']), row2(Wl['ml_gn_b']), row2(Wl['ml_skip']),
           st['ml_conv'], st['ml_C'], st['ml_n'].reshape(B, 1, D_HALF),
           _pad_lanes(st['ml_m']).reshape(B, 1, LANES)]
    in_specs = [pl.BlockSpec((nb, L, D_MODEL), lambda b, c: (b, c, 0)),
                pl.BlockSpec((None, D_MODEL, 2 * D_HALF), lambda b, c: (j, 0, 0)),
                _full((CONV_W, D_HALF)), _full((1, D_HALF)),
                _full((D_HALF, D_HALF)), _full((D_HALF, D_HALF)), _full((D_HALF, D_HALF)),
                _full((3 * D_HALF, LANES)), _full((1, LANES)), _full((2 * ML_HEADS, 3 * D_HALF)),
                _full((2 * ML_HEADS, LANES)), _full((1, D_HALF)), _full((1, D_HALF)), _full((1, D_HALF)),
                perb(CONV_W - 1, D_HALF), perb(ML_HEADS, ML_DH, ML_DH), perb(1, D_HALF), perb(1, LANES)]
    seq = pl.BlockSpec((nb, L, D_HALF), lambda b, c: (b, c, 0))
    yc, conv, C, n, m = pl.pallas_call(
        functools.partial(_mlstm_prefill_kernel, L=L),
        grid=(B // nb, nc),
        in_specs=in_specs,
        out_specs=[seq, perb(CONV_W - 1, D_HALF), perb(ML_HEADS, ML_DH, ML_DH), perb(1, D_HALF), perb(1, LANES)],
        out_shape=[jax.ShapeDtypeStruct((B, T, D_HALF), F32),
                   jax.ShapeDtypeStruct((B, CONV_W - 1, D_HALF), F32),
                   jax.ShapeDtypeStruct((B, ML_HEADS, ML_DH, ML_DH), F32),
                   jax.ShapeDtypeStruct((B, 1, D_HALF), F32),
                   jax.ShapeDtypeStruct((B, 1, LANES), F32)],
        scratch_shapes=[pltpu.VMEM((nb, L + SUBLANES, D_HALF), F32),
                        pltpu.VMEM((nb, ML_HEADS, ML_DH, LANES), F32),
                        pltpu.VMEM((nb, L, 2 * D_HALF), F32)],
        compiler_params=_params(("parallel", "arbitrary")),
        name="mlstm_prefill",
    )(*ins)
    return yc, (C, n.reshape(B, ML_HEADS, ML_DH), m[:, 0, :ML_HEADS], conv)


def _mlstm_decode_kernel(p_ref, cw_ref, cb_ref, wq_ref, wk_ref, wv_ref, wg_ref, bg_ref,
                         gng_ref, gnb_ref, skip_ref, conv0_ref, C0_ref, n0_ref, m0_ref,
                         yc_ref, conv_ref, C_ref, n_ref, m_ref):
    xm = p_ref[:, 0:D_HALF]
    z = p_ref[:, D_HALF:2 * D_HALF]
    xc = cb_ref[...] + cw_ref[CONV_W - 1:CONV_W, :] * xm
    for i in range(CONV_W - 1):
        xc = xc + cw_ref[i:i + 1, :] * conv0_ref[i]
    for i in range(CONV_W - 2):
        conv_ref[i] = conv0_ref[i + 1]
    conv_ref[CONV_W - 2] = xm
    xc = _silu(xc)
    q, k, v, g = _mlstm_qkv_gates(xm, xc, wq_ref, wk_ref, wv_ref, wg_ref, bg_ref)
    lf_all = -_softplus(-g)
    lane = lax.broadcasted_iota(jnp.int32, (1, LANES), 1)
    row8 = lax.broadcasted_iota(jnp.int32, (SUBLANES, ML_DH), 0)
    m_all = m0_ref[...]
    m_out = m_all
    for hh in range(ML_HEADS):
        sl = slice(hh * ML_DH, (hh + 1) * ML_DH)
        qh, vh = q[:, sl], v[:, sl]
        kh = k[:, sl] * (ML_DH ** -0.5)
        li = g[:, hh:hh + 1]
        lf = lf_all[:, ML_HEADS + hh:ML_HEADS + hh + 1]
        m_prev = m_all[:, hh:hh + 1]
        n = n0_ref[:, sl]
        log_inter = lf + m_prev
        m_t = jnp.maximum(li, log_inter)
        s = jnp.sum(qh * kh, -1, keepdims=True) * jnp.exp(li - m_t)
        w_inter = jnp.exp(log_inter - m_t)
        w_k = jnp.exp(li - m_t)
        w_C = jnp.exp(log_inter - m_t)
        kw = kh * w_k
        rows = []
        for bi in range(DEC_BB):
            C = C0_ref[bi, hh]
            q8 = jnp.broadcast_to(qh[bi:bi + 1, :], (SUBLANES, ML_DH))
            rows.append(_dot_f32(q8, C)[0:1, :])
            k8 = jnp.where(row8 == 0, jnp.broadcast_to(kw[bi:bi + 1, :], (SUBLANES, ML_DH)), 0.0)
            v8 = jnp.broadcast_to(vh[bi:bi + 1, :], (SUBLANES, ML_DH))
            C_ref[bi, hh] = w_C[bi:bi + 1, :] * C + _dot_tn_f32(k8, v8)
        qC = jnp.concatenate(rows, axis=0)
        num = s * vh + w_inter * qC
        den = s + w_inter * jnp.sum(qh * n, -1, keepdims=True)
        hcell = num / jnp.maximum(jnp.abs(den), jnp.exp(-m_t))
        n_ref[:, sl] = w_C * n + kw
        m_out = jnp.where(lane == hh, m_t, m_out)
        hn = _head_norm(hcell, gng_ref[:, sl], gnb_ref[:, sl], GN_EPS)
        yc_ref[:, sl] = (hn + skip_ref[:, sl] * xc[:, sl]) * _silu(z[:, sl])
    m_ref[...] = m_out


def _mlstm_decode(p2, st, Wl, C_all, j, C_prev):
    B = p2.shape[0]
    bb = DEC_BB
    wg, bg, _, _ = _mlstm_weights(Wl)
    row2 = lambda a: a.reshape(1, D_HALF)
    rows = lambda n: pl.BlockSpec((bb, n), lambda i: (i, 0))
    convs = pl.BlockSpec((CONV_W - 1, bb, D_HALF), lambda i: (0, i, 0))
    ins = [p2, Wl['ml_conv_w'], row2(Wl['ml_conv_b']), Wl['ml_wq_d'], Wl['ml_wk_d'], Wl['ml_wv_d'], wg, bg,
           row2(Wl['ml_gn_g']), row2(Wl['ml_gn_b']), row2(Wl['ml_skip']),
           jnp.swapaxes(st['ml_conv'], 0, 1), C_all, st['ml_n'].reshape(B, D_HALF), _pad_lanes(st['ml_m'])]
    kern, Cs, extra_in, extra_specs, aliases, C_shape = _stacked_state_io(
        _mlstm_decode_kernel, C_all, C_prev, j, bb, len(ins) - 3, len(ins), 2)
    in_specs = [pl.BlockSpec((bb, 2 * D_HALF), lambda i: (i, 0)), _full((CONV_W, D_HALF)), _full((1, D_HALF)),
                _full((D_HALF, D_HALF)), _full((D_HALF, D_HALF)), _full((D_HALF, D_HALF)),
                _full((3 * D_HALF, LANES)), _full((1, LANES)),
                _full((1, D_HALF)), _full((1, D_HALF)), _full((1, D_HALF)),
                convs, Cs, rows(D_HALF), rows(LANES)] + extra_specs
    yc, conv, C, n, m = pl.pallas_call(
        kern,
        grid=(B // bb,),
        in_specs=in_specs,
        out_specs=[rows(D_HALF), convs, Cs, rows(D_HALF), rows(LANES)],
        out_shape=[jax.ShapeDtypeStruct((B, D_HALF), F32),
                   jax.ShapeDtypeStruct((CONV_W - 1, B, D_HALF), F32),
                   C_shape,
                   jax.ShapeDtypeStruct((B, D_HALF), F32),
                   jax.ShapeDtypeStruct((B, LANES), F32)],
        input_output_aliases=aliases,
        compiler_params=_params(("parallel",)),
        name="mlstm_decode",
    )(*ins, *extra_in)
    return yc, (C, n.reshape(B, ML_HEADS, ML_DH), m[:, :ML_HEADS], jnp.swapaxes(conv, 0, 1))


def _rwkv_pre_body(pr, pr_prev, mu_ref, w0_ref, a0_ref, w2_ref, a2_ref, g2_ref, kkw_ref, kaw_ref, rk_ref,
                   r_ref, d_ref, k_ref, v_ref, a_ref, b_ref, g_ref, bonus_ref):
    pm = pr + (pr_prev - pr) * mu_ref[...]
    r = pm[:, 0:D_HALF]
    kr = pm[:, D_HALF:2 * D_HALF]
    vr = pm[:, 2 * D_HALF:3 * D_HALF]
    lo = pm[:, 3 * D_HALF:RW_COLS_PAD]
    w_log = -_softplus(-(w0_ref[...] + _dot(jnp.tanh(lo), w2_ref[...]))) - 0.5
    a = _sigmoid(a0_ref[...] + _dot(lo, a2_ref[...]))
    g = _dot(_sigmoid(lo), g2_ref[...])
    ones_bd = _group_ones(LANES, RW_DH)
    kk = kr * kkw_ref[...]
    kk = kk / jnp.maximum(jnp.sqrt(_group_sum(kk * kk, ones_bd)), 1e-12)
    kh = kr * (1.0 + (a - 1.0) * kaw_ref[...])
    r_ref[...] = r
    d_ref[...] = jnp.exp(-jnp.exp(w_log))
    k_ref[...] = kh
    v_ref[...] = vr
    a_ref[...] = -kk
    b_ref[...] = kk * a
    g_ref[...] = g
    bonus_ref[...] = _group_sum(r * kh * rk_ref[...], ones_bd) * vr


def _rwkv_pre_prefill_kernel(pr_ref, shift0_ref, *rest, L):
    wrefs, outs, xbuf = rest[:9], rest[9:17], rest[17]
    c = pl.program_id(1)

    @pl.when(c == 0)
    def _():
        xbuf[SUBLANES - 1:SUBLANES, :] = shift0_ref[...]

    pr = pr_ref[...]
    xbuf[SUBLANES:SUBLANES + L, :] = pr
    pr_prev = xbuf[SUBLANES - 1:SUBLANES - 1 + L, :]
    xbuf[SUBLANES - 1:SUBLANES, :] = pr[L - 1:L, :]
    _rwkv_pre_body(pr, pr_prev, *wrefs, *outs)


def _rwkv_pre_decode_kernel(pr_ref, prev_ref, *rest):
    _rwkv_pre_body(pr_ref[...], prev_ref[...], *rest[:9], *rest[9:17])


def _rwkv_pre_weights(Wl):
    row2 = lambda a: a.reshape(1, D_HALF)
    padr = lambda w, o: jnp.pad(w, ((o, RW_LORA_PAD - o - w.shape[0]), (0, 0))).astype(BF16)
    mu = _pad_lanes(Wl['rw_mu'].reshape(1, RW_SHIFT_COLS), RW_COLS_PAD)
    ws = [mu, row2(Wl['rw_w0']), row2(Wl['rw_a0']),
          padr(Wl['rw_w2'], 0), padr(Wl['rw_a2'], RW_DECAY_LORA), padr(Wl['rw_g2'], RW_DECAY_LORA + RW_A_LORA),
          row2(Wl['rw_kk']), row2(Wl['rw_ka']), row2(Wl['rw_rk'])]
    specs = [_full((1, RW_COLS_PAD)), _full((1, D_HALF)), _full((1, D_HALF)),
             _full((RW_LORA_PAD, D_HALF)), _full((RW_LORA_PAD, D_HALF)), _full((RW_LORA_PAD, D_HALF)),
             _full((1, D_HALF)), _full((1, D_HALF)), _full((1, D_HALF))]
    return ws, specs


def _rwkv_pre_prefill(pr3, shift0, Wl):
    B, T, _ = pr3.shape
    L = CHUNK
    ws, wspecs = _rwkv_pre_weights(Wl)
    seq = pl.BlockSpec((None, L, D_HALF), lambda b, c: (b, c, 0))
    outs = pl.pallas_call(
        functools.partial(_rwkv_pre_prefill_kernel, L=L),
        grid=(B, T // L),
        in_specs=[pl.BlockSpec((None, L, RW_COLS_PAD), lambda b, c: (b, c, 0)),
                  pl.BlockSpec((None, 1, RW_COLS_PAD), lambda b, c: (b, 0, 0))] + wspecs,
        out_specs=[seq] * 8,
        out_shape=[jax.ShapeDtypeStruct((B, T, D_HALF), F32)] * 8,
        scratch_shapes=[pltpu.VMEM((L + SUBLANES, RW_COLS_PAD), F32)],
        compiler_params=_params(("parallel", "arbitrary")),
        name="rwkv_pre_prefill",
    )(pr3, _pad_lanes(shift0, RW_COLS_PAD).reshape(B, 1, RW_COLS_PAD), *ws)
    return outs


def _rwkv_pre_decode(pr, shift0, Wl):
    B = pr.shape[0]
    ws, wspecs = _rwkv_pre_weights(Wl)
    full2 = lambda n: pl.BlockSpec((B, n), lambda i: (0, 0))
    outs = pl.pallas_call(
        _rwkv_pre_decode_kernel,
        grid=(1,),
        in_specs=[full2(RW_COLS_PAD), full2(RW_COLS_PAD)] + wspecs,
        out_specs=[full2(D_HALF)] * 8,
        out_shape=[jax.ShapeDtypeStruct((B, D_HALF), F32)] * 8,
        compiler_params=_params(("arbitrary",)),
        name="rwkv_pre_decode",
    )(pr, _pad_lanes(shift0, RW_COLS_PAD), *ws)
    return outs


RW_IP = RW_DH // 2


def _rwkv_rec_kernel(r_ref, d_ref, k_ref, a_ref, b_ref, v_ref, S0_ref, y_ref, S_ref, *, Tc):
    @pl.when(pl.program_id(0) == 0)
    def _():
        S_ref[...] = S0_ref[...]

    lane = lax.broadcasted_iota(jnp.int32, (1, LANES), 1)

    def tiles(t):
        back = (LANES - (t % RW_TB) * RW_BB) % LANES
        out = []
        for ref in (a_ref, d_ref, b_ref, k_ref, r_ref):
            raw = ref[t]
            out.append(jnp.where(lane < LANES // 2, pltpu.roll(raw, back, 1),
                                 pltpu.roll(raw, (back + LANES // 2) % LANES, 1)))
        out.append(pltpu.roll(v_ref[t], back, 1))
        return tuple(out)

    def step(t, carry):
        a, d, b, k, r, vt = carry
        nxt = tiles(jnp.minimum(t + 1, Tc - 1))
        rows = []
        for ip in range(RW_IP):
            S = S_ref[ip]
            sa = jnp.sum(S * a, axis=0, keepdims=True)
            Sn = S * d + sa * b + vt[ip:ip + 1, :] * k
            S_ref[ip] = Sn
            rows.append(jnp.sum(Sn * r, axis=0, keepdims=True))
        y_ref[t] = jnp.concatenate(rows, axis=0)
        return nxt

    lax.fori_loop(0, Tc, step, tiles(0))


def _rwkv_rec_call(r, d, k, a, b, v, S0):
    T = r.shape[0]
    Tc = min(T, 32)
    vec = pl.BlockSpec((Tc, RW_DH, LANES), lambda c: (c, 0, 0))
    vsp = pl.BlockSpec((Tc, RW_IP, LANES), lambda c: (c, 0, 0))
    ssp = _full((RW_IP, RW_DH, LANES))
    return pl.pallas_call(
        functools.partial(_rwkv_rec_kernel, Tc=Tc),
        grid=(T // Tc,),
        in_specs=[vec] * 5 + [vsp, ssp],
        out_specs=[vsp, ssp],
        out_shape=[jax.ShapeDtypeStruct((T, RW_IP, LANES), F32),
                   jax.ShapeDtypeStruct((RW_IP, RW_DH, LANES), F32)],
        compiler_params=_params(("arbitrary",)),
        name="rwkv_recurrence",
    )(r, d, k, a, b, v, S0)


def _rwkv_dec_kernel(r_ref, d_ref, k_ref, a_ref, b_ref, v_ref, S0_ref, y_ref, S_ref):
    a, d, b, k, r = a_ref[...], d_ref[...], b_ref[...], k_ref[...], r_ref[...]
    v = v_ref[...]
    rows = []
    for i in range(RW_DH):
        S = S0_ref[i]
        sa = jnp.sum(S * a, axis=0, keepdims=True)
        Sn = S * d + sa * b + v[i:i + 1, :] * k
        S_ref[i] = Sn
        rows.append(jnp.sum(Sn * r, axis=0, keepdims=True))
    y_ref[...] = jnp.concatenate(rows, axis=0)


def _rwkv_decode_step(r, d, k, v, a, b, S0):
    B = r.shape[0]
    tr = lambda x: x.T.reshape(RW_HEADS, RW_DH, B)
    St = S0.reshape(B, RW_HEADS * RW_DH * RW_DH).T.reshape(RW_HEADS, RW_DH, RW_DH, B)
    vec = pl.BlockSpec((None, RW_DH, B), lambda h: (h, 0, 0))
    ssp = pl.BlockSpec((None, RW_DH, RW_DH, B), lambda h: (h, 0, 0, 0))
    y, S = pl.pallas_call(
        _rwkv_dec_kernel,
        grid=(RW_HEADS,),
        in_specs=[vec] * 6 + [ssp],
        out_specs=[vec, ssp],
        out_shape=[jax.ShapeDtypeStruct((RW_HEADS, RW_DH, B), F32),
                   jax.ShapeDtypeStruct((RW_HEADS, RW_DH, RW_DH, B), F32)],
        compiler_params=_params(("parallel",)),
        name="rwkv_decode_step",
    )(tr(r), tr(d), tr(k), tr(a), tr(b), tr(v), St)
    y = y.reshape(D_HALF, B).T
    S = S.reshape(RW_HEADS * RW_DH * RW_DH, B).T.reshape(B, RW_HEADS, RW_DH, RW_DH)
    return y, S


RW_TB = LANES // RW_BB
RW_NSB = 2
RW_MM_ROWS = 256


def _head_sum_rows(x):
    x3 = x.reshape(RW_HEADS, RW_DH, x.shape[-1])
    s = jnp.sum(x3, axis=1, keepdims=True)
    return jnp.broadcast_to(s, x3.shape).reshape(x.shape)


def _rwkv_pre_t_kernel(x_ref, shift0_ref, w_ref, mu_ref, w0_ref, a0_ref, w2_ref, a2_ref, g2_ref,
                       kkw_ref, kaw_ref, rk_ref,
                       r_ref, d_ref, k_ref, a_ref, b_ref, v_ref, g_ref, bonus_ref, last_ref, prev_scr):
    @pl.when(pl.program_id(0) == 0)
    def _():
        prev_scr[...] = shift0_ref[...]

    ro = lax.broadcasted_iota(jnp.int32, (LANES, LANES), 0)
    ci = lax.broadcasted_iota(jnp.int32, (LANES, LANES), 1)
    perm = jnp.where(ci == (ro % RW_BB) * RW_TB + ro // RW_BB, 1.0, 0.0).astype(BF16)
    lane = lax.broadcasted_iota(jnp.int32, (1, LANES), 1)
    grp = lane // RW_BB
    ngrp = LANES // RW_BB
    prs = {}

    def scatter(x, o_ref, t0, nrow, npiece):
        rot = [x[q * nrow:(q + 1) * nrow, :] if q == 0 else pltpu.roll(x[q * nrow:(q + 1) * nrow, :], q * RW_BB, 1)
               for q in range(npiece)]
        for t in range(RW_TB):
            m = rot[0]
            for q in range(1, npiece):
                m = jnp.where(grp == (t + q) % ngrp, rot[q], m)
            o_ref[t0 + t] = m

    def block(sb):
        t0 = sb * RW_TB
        xn = x_ref[:, t0:t0 + RW_TB, :].reshape(RW_BB * RW_TB, D_MODEL).astype(BF16)
        xg = jnp.dot(perm, xn, preferred_element_type=F32).astype(BF16)
        parts = []
        for r0 in range(0, RW_COLS_PAD, RW_MM_ROWS):
            parts.append(lax.dot_general(w_ref[r0:r0 + RW_MM_ROWS, :], xg, (((1,), (1,)), ((), ())),
                                         preferred_element_type=F32))
            yield
        pr = jnp.concatenate(parts, axis=0)
        prs[sb] = pr
        rolled = pltpu.roll(pr, RW_BB, 1)
        before = prev_scr[...] if sb == 0 else prs[sb - 1]
        prev = jnp.where(lane < RW_BB, before, rolled)
        prs[sb] = rolled
        if sb == RW_NSB - 1:
            prev_scr[...] = rolled
            last_ref[...] = pr
        pm = pr + (prev - pr) * mu_ref[...]
        r = pm[0:D_HALF]
        kr = pm[D_HALF:2 * D_HALF]
        vr = pm[2 * D_HALF:3 * D_HALF]
        lo = pm[3 * D_HALF:RW_COLS_PAD]
        yield
        w_log = -_softplus(-(w0_ref[...] + _dot(w2_ref[...], jnp.tanh(lo)))) - 0.5
        a = _sigmoid(a0_ref[...] + _dot(a2_ref[...], lo))
        g = _dot(g2_ref[...], _sigmoid(lo))
        kk = kr * kkw_ref[...]
        kk = kk / jnp.maximum(jnp.sqrt(_head_sum_rows(kk * kk)), 1e-12)
        kh = kr * (1.0 + (a - 1.0) * kaw_ref[...])
        g_ref[sb] = g
        bonus_ref[sb] = _head_sum_rows(r * kh * rk_ref[...]) * vr
        yield
        scatter(r, r_ref, t0, RW_DH, RW_HEADS)
        yield
        scatter(jnp.exp(-jnp.exp(w_log)), d_ref, t0, RW_DH, RW_HEADS)
        yield
        scatter(kh, k_ref, t0, RW_DH, RW_HEADS)
        yield
        scatter(-kk, a_ref, t0, RW_DH, RW_HEADS)
        yield
        scatter(kk * a, b_ref, t0, RW_DH, RW_HEADS)
        yield
        vv = jnp.concatenate([vr[h * RW_DH + half * RW_IP:h * RW_DH + (half + 1) * RW_IP, :]
                              for half in range(2) for h in range(RW_HEADS)], axis=0)
        scatter(vv, v_ref, t0, RW_IP, ngrp)

    _skewed([block(sb) for sb in range(RW_NSB)], lag=RW_COLS_PAD // RW_MM_ROWS)


def _lane_bcast(a, n):
    return jnp.broadcast_to(a.reshape(n, 1), (n, LANES))


def _rwkv_pre_t(x3, shift0, w_rwt, j, Wl):
    B, T, _ = x3.shape
    nblk = T // RW_TB
    padr = lambda w, o: jnp.pad(w, ((o, RW_LORA_PAD - o - w.shape[0]), (0, 0))).astype(BF16).T
    sh = jnp.pad(shift0.T, ((0, RW_COLS_PAD - RW_SHIFT_COLS), (0, LANES - RW_BB)))
    col = lambda a: _lane_bcast(a, D_HALF)
    ins = [x3, sh, w_rwt, _lane_bcast(_pad_lanes(Wl['rw_mu'].reshape(1, -1), RW_COLS_PAD), RW_COLS_PAD),
           col(Wl['rw_w0']), col(Wl['rw_a0']),
           padr(Wl['rw_w2'], 0), padr(Wl['rw_a2'], RW_DECAY_LORA), padr(Wl['rw_g2'], RW_DECAY_LORA + RW_A_LORA),
           col(Wl['rw_kk']), col(Wl['rw_ka']), col(Wl['rw_rk'])]
    tb = RW_TB * RW_NSB
    in_specs = [pl.BlockSpec((B, tb, D_MODEL), lambda c: (0, c, 0)), _full((RW_COLS_PAD, LANES)),
                pl.BlockSpec((None, RW_COLS_PAD, D_MODEL), lambda c: (j, 0, 0)), _full((RW_COLS_PAD, LANES)),
                _full((D_HALF, LANES)), _full((D_HALF, LANES)),
                _full((D_HALF, RW_LORA_PAD)), _full((D_HALF, RW_LORA_PAD)), _full((D_HALF, RW_LORA_PAD)),
                _full((D_HALF, LANES)), _full((D_HALF, LANES)), _full((D_HALF, LANES))]
    blk = pl.BlockSpec((RW_NSB, D_HALF, LANES), lambda c: (c, 0, 0))
    ktile = pl.BlockSpec((tb, RW_DH, LANES), lambda c: (c, 0, 0))
    vtile = pl.BlockSpec((tb, RW_IP, LANES), lambda c: (c, 0, 0))
    outs = pl.pallas_call(
        _rwkv_pre_t_kernel,
        grid=(T // tb,),
        in_specs=in_specs,
        out_specs=[ktile] * 5 + [vtile, blk, blk, _full((RW_COLS_PAD, LANES))],
        out_shape=[jax.ShapeDtypeStruct((T, RW_DH, LANES), F32)] * 5
                  + [jax.ShapeDtypeStruct((T, RW_IP, LANES), F32)]
                  + [jax.ShapeDtypeStruct((nblk, D_HALF, LANES), F32)] * 2
                  + [jax.ShapeDtypeStruct((RW_COLS_PAD, LANES), F32)],
        scratch_shapes=[pltpu.VMEM((RW_COLS_PAD, LANES), F32)],
        compiler_params=_params(("arbitrary",)),
        name="rwkv_pre_t",
    )(*ins)
    shift_new = outs[8][:RW_SHIFT_COLS, LANES - RW_BB:].T
    return outs[:8], shift_new


RW_NPB = 4


def _rwkv_post_t_kernel(y_ref, g_ref, bonus_ref, gng_ref, gnb_ref, yd_ref):
    _round_robin([_rwkv_post_t_block(y_ref, g_ref, bonus_ref, gng_ref, gnb_ref, yd_ref, pb)
                  for pb in range(RW_NPB)])


def _rwkv_post_t_block(y_ref, g_ref, bonus_ref, gng_ref, gnb_ref, yd_ref, pb):
    t0 = pb * RW_TB
    lane = lax.broadcasted_iota(jnp.int32, (1, LANES), 1)
    grp = lane // RW_BB
    ngrp = LANES // RW_BB
    ys = [y_ref[t0 + t] for t in range(RW_TB)]
    rolled = []
    for s in range(ngrp):
        m = ys[s % RW_TB]
        for q in range(1, ngrp):
            m = jnp.where(grp == q, ys[(q + s) % RW_TB], m)
        rolled.append(pltpu.roll(m, s * RW_BB, 1) if s else m)
    yield
    pieces = {}
    for q in range(ngrp):
        m = rolled[(-q) % ngrp]
        for t in range(1, RW_TB):
            m = jnp.where(grp == t, rolled[(t - q) % ngrp], m)
        pieces[divmod(q, RW_HEADS)] = m
    y = jnp.concatenate([pieces[(half, h)] for h in range(RW_HEADS) for half in range(2)], axis=0)
    yield
    mu = _head_sum_rows(y) * (1.0 / RW_DH)
    yc = y - mu
    var = _head_sum_rows(yc * yc) * (1.0 / RW_DH)
    hn = yc * lax.rsqrt(var + RW_GN_EPS) * gng_ref[...] + gnb_ref[...]
    yd = ((hn + bonus_ref[pb]) * g_ref[pb]).T
    yield
    for t in range(RW_TB):
        yd_ref[:, t0 + t, :] = yd[t * RW_BB:(t + 1) * RW_BB, :]


def _rwkv_rec_t(vecs, S0, Wl, T):
    r, d, k, a, b, v, g, bonus = vecs
    nblk = T // RW_TB
    B = RW_BB
    Sr = S0.reshape(B, RW_HEADS, 2, RW_IP, RW_DH).transpose(3, 4, 2, 1, 0).reshape(RW_IP, RW_DH, LANES)
    y, S = _rwkv_rec_call(r, d, k, a, b, v, Sr)
    tb = RW_TB * RW_NPB
    blk = pl.BlockSpec((RW_NPB, D_HALF, LANES), lambda c: (c, 0, 0))
    col = lambda a_: _lane_bcast(a_, D_HALF)
    yd = pl.pallas_call(
        _rwkv_post_t_kernel,
        grid=(nblk // RW_NPB,),
        in_specs=[pl.BlockSpec((tb, RW_IP, LANES), lambda c: (c, 0, 0)), blk, blk,
                  _full((D_HALF, LANES)), _full((D_HALF, LANES))],
        out_specs=pl.BlockSpec((B, tb, D_HALF), lambda c: (0, c, 0)),
        out_shape=jax.ShapeDtypeStruct((B, T, D_HALF), F32),
        compiler_params=_params(("parallel",)),
        name="rwkv_post_t",
    )(y, g, bonus, col(Wl['rw_gn_g']), col(Wl['rw_gn_b']))
    S = S.reshape(RW_IP, RW_DH, 2, RW_HEADS, B).transpose(4, 3, 2, 0, 1).reshape(B, RW_HEADS, RW_DH, RW_DH)
    return yd, S


def _rwkv_post_kernel(y_ref, g_ref, bonus_ref, gng_ref, gnb_ref, o_ref):
    ones_bd = _group_ones(LANES, RW_DH)
    y = y_ref[...]
    mu = _group_sum(y, ones_bd) * (1.0 / RW_DH)
    yc = y - mu
    var = _group_sum(yc * yc, ones_bd) * (1.0 / RW_DH)
    hn = yc * lax.rsqrt(var + RW_GN_EPS) * gng_ref[...] + gnb_ref[...]
    o_ref[...] = (hn + bonus_ref[...]) * g_ref[...]


def _rwkv_post(y2, g2, bonus2, Wl, tm):
    M = y2.shape[0]
    row = pl.BlockSpec((tm, D_HALF), lambda i: (i, 0))
    return pl.pallas_call(
        _rwkv_post_kernel,
        grid=(M // tm,),
        in_specs=[row, row, row, _full((1, D_HALF)), _full((1, D_HALF))],
        out_specs=row,
        out_shape=jax.ShapeDtypeStruct((M, D_HALF), F32),
        compiler_params=_params(("parallel",)),
        name="rwkv_post",
    )(y2, g2, bonus2, Wl['rw_gn_g'].reshape(1, D_HALF), Wl['rw_gn_b'].reshape(1, D_HALF))


def _trunk(x, st, pos, W):
    B, T, _ = x.shape
    M = B * T
    decode = T == 1
    tm_proj = min(M, 512)
    tm_post = min(M, 2 * POST_SUB)
    tm_rw = min(M, 512)
    x2 = x.reshape(M, D_MODEL)
    new = {name: [] for name in st}
    stacked = {}
    for l in range(DEPTH):
        j = l // 2
        Wl = {name: v[j] for name, v in W['per_pair'][l % 2].items()}
        stl = {name: v[j] for name, v in st.items()}
        if l % 2 == 0:
            if decode:
                p, = _proj(x2, W['ev_w_in'], j, tm_proj, (EVEN_IN,))
                ya, yb, (h, cb, S) = _even_decode(p, pos, stl, Wl, st['ret_S'], j, stacked.get('ret_S'))
                stacked['ret_S'] = S
            else:
                ya, yb, (h, cb, S) = _even_prefill(x2.reshape(B, T, D_MODEL), W['ev_w_in'], j, pos, stl, Wl)
                new['ret_S'].append(S)
            new['lru_h'].append(h)
            new['lru_conv'].append(cb)
            wout = W['ev_w_out']
        else:
            if decode:
                p_ml, p_rw = _proj(x2, W['od_w_in'], j, tm_proj, (2 * D_HALF, RW_COLS_PAD))
                ya, (C, n, m, cb) = _mlstm_decode(p_ml, stl, Wl, st['ml_C'], j, stacked.get('ml_C'))
                stacked['ml_C'] = C
                r, d, k, v, a, b, g, bonus = _rwkv_pre_decode(p_rw, stl['rw_shift'], Wl)
                y, S = _rwkv_decode_step(r, d, k, v, a, b, stl['rw_S'])
                yb = _rwkv_post(y, g, bonus, Wl, tm_rw)
                shift_new = p_rw[:, :RW_SHIFT_COLS]
            else:
                x3 = x2.reshape(B, T, D_MODEL)
                ya, (C, n, m, cb) = _mlstm_prefill(x3, W['od_w_in'], j, stl, Wl)
                vecs, shift_new = _rwkv_pre_t(x3, stl['rw_shift'], W['od_w_rwt'], j, Wl)
                yb, S = _rwkv_rec_t(vecs, stl['rw_S'], Wl, T)
                new['ml_C'].append(C)
            new['ml_n'].append(n)
            new['ml_m'].append(m)
            new['ml_conv'].append(cb)
            new['rw_S'].append(S)
            new['rw_shift'].append(shift_new)
            wout = W['od_w_out']
        x2 = _post(ya.reshape(M, D_HALF), yb.reshape(M, D_HALF), x2, wout, j, l,
                   W['ln1_g'], W['ln1_b'], W['mlp_w1'], W['mlp_w2'], W['ln2_g'], W['ln2_b'], tm_post)
    out = {name: stacked[name] if name in stacked else jnp.stack(v) for name, v in new.items()}
    return x2.reshape(B, T, D_MODEL), out


def _prepare_weights(w):
    even_names = ('lru_conv_w', 'lru_conv_b', 'lru_ba', 'lru_bx', 'lru_lambda', 'ret_gn_g', 'ret_gn_b')
    odd_names = ('ml_conv_w', 'ml_conv_b', 'ml_w_gate', 'ml_b_gate', 'ml_gn_g', 'ml_gn_b', 'ml_skip',
                 'rw_mu', 'rw_w0', 'rw_w2', 'rw_a0', 'rw_a2', 'rw_g2', 'rw_kk', 'rw_ka', 'rw_rk', 'rw_gn_g', 'rw_gn_b')
    even = {n: w[n] for n in even_names}
    even.update(lru_wa_d=jax.vmap(_lru_dense)(w['lru_wa']), lru_wx_d=jax.vmap(_lru_dense)(w['lru_wx']))
    odd = {n: w[n] for n in odd_names}
    odd.update(ml_wq_d=jax.vmap(_ml_dense)(w['ml_wq']), ml_wk_d=jax.vmap(_ml_dense)(w['ml_wk']),
               ml_wv_d=jax.vmap(_ml_dense)(w['ml_wv']))
    od_w_in = _pad_lanes(w['od_w_in'], ODD_IN_PAD).astype(BF16)
    return dict(per_pair=(even, odd),
                ev_w_in=w['ev_w_in'].astype(BF16), ev_w_out=w['ev_w_out'].astype(BF16),
                od_w_in=od_w_in, od_w_rwt=jnp.swapaxes(od_w_in[:, :, 2 * D_HALF:], 1, 2),
                od_w_out=w['od_w_out'].astype(BF16),
                mlp_w1=w['mlp_w1'].astype(BF16), mlp_w2=w['mlp_w2'].astype(BF16),
                ln1_g=w['ln1_g'], ln1_b=w['ln1_b'], ln2_g=w['ln2_g'], ln2_b=w['ln2_b'])


def _zero_states(batch):
    z = lambda *s: jnp.zeros(s, F32)
    n_even, n_odd = (DEPTH + 1) // 2, DEPTH // 2
    return dict(lru_h=z(n_even, batch, D_HALF), lru_conv=z(n_even, batch, CONV_W - 1, D_HALF),
                ret_S=z(n_even, batch, RET_HEADS, RET_DH, RET_DH),
                ml_C=z(n_odd, batch, ML_HEADS, ML_DH, ML_DH), ml_n=z(n_odd, batch, ML_HEADS, ML_DH),
                ml_m=z(n_odd, batch, ML_HEADS), ml_conv=z(n_odd, batch, CONV_W - 1, D_HALF),
                rw_S=z(n_odd, batch, RW_HEADS, RW_DH, RW_DH), rw_shift=z(n_odd, batch, RW_SHIFT_COLS))


def kernel(x_prompt, x_sample, state_lru_h, state_lru_conv, state_ret, state_mlstm_C, state_mlstm_n, state_mlstm_m, state_mlstm_conv, state_rwkv_S, state_rwkv_shift, ln1_g, ln1_b, ln2_g, ln2_b, mlp_w1, mlp_w2, ev_w_in, ev_w_out, lru_conv_w, lru_conv_b, lru_wa, lru_ba, lru_wx, lru_bx, lru_lambda, ret_gn_g, ret_gn_b, od_w_in, od_w_out, ml_conv_w, ml_conv_b, ml_wq, ml_wk, ml_wv, ml_w_gate, ml_b_gate, ml_gn_g, ml_gn_b, ml_skip, rw_mu, rw_w0, rw_w2, rw_a0, rw_a2, rw_g2, rw_kk, rw_ka, rw_rk, rw_gn_g, rw_gn_b):
    W = _prepare_weights(dict(
        ln1_g=ln1_g, ln1_b=ln1_b, ln2_g=ln2_g, ln2_b=ln2_b, mlp_w1=mlp_w1, mlp_w2=mlp_w2,
        ev_w_in=ev_w_in, ev_w_out=ev_w_out, lru_conv_w=lru_conv_w, lru_conv_b=lru_conv_b,
        lru_wa=lru_wa, lru_ba=lru_ba, lru_wx=lru_wx, lru_bx=lru_bx, lru_lambda=lru_lambda,
        ret_gn_g=ret_gn_g, ret_gn_b=ret_gn_b, od_w_in=od_w_in, od_w_out=od_w_out,
        ml_conv_w=ml_conv_w, ml_conv_b=ml_conv_b, ml_wq=ml_wq, ml_wk=ml_wk, ml_wv=ml_wv,
        ml_w_gate=ml_w_gate, ml_b_gate=ml_b_gate, ml_gn_g=ml_gn_g, ml_gn_b=ml_gn_b, ml_skip=ml_skip,
        rw_mu=rw_mu, rw_w0=rw_w0, rw_w2=rw_w2, rw_a0=rw_a0, rw_a2=rw_a2, rw_g2=rw_g2,
        rw_kk=rw_kk, rw_ka=rw_ka, rw_rk=rw_rk, rw_gn_g=rw_gn_g, rw_gn_b=rw_gn_b))
    st_sample = dict(lru_h=state_lru_h, lru_conv=state_lru_conv, ret_S=state_ret,
                     ml_C=state_mlstm_C, ml_n=state_mlstm_n, ml_m=state_mlstm_m, ml_conv=state_mlstm_conv,
                     rw_S=state_rwkv_S, rw_shift=state_rwkv_shift)
    pos_prompt = jnp.arange(x_prompt.shape[1], dtype=jnp.int32)
    pos_sample = PAST_LEN + jnp.arange(x_sample.shape[1], dtype=jnp.int32)
    y_prompt, sp = _trunk(x_prompt, _zero_states(x_prompt.shape[0]), pos_prompt, W)
    y_sample, ss = _trunk(x_sample, st_sample, pos_sample, W)
    names = ('lru_h', 'lru_conv', 'ret_S', 'ml_C', 'ml_n', 'ml_m', 'ml_conv', 'rw_S', 'rw_shift')
    return (y_prompt, y_sample) + tuple(sp[n] for n in names) + tuple(ss[n] for n in names)
```

```python
import functools

import jax
import jax.numpy as jnp
from jax import lax
from jax.experimental import pallas as pl
from jax.experimental.pallas import tpu as pltpu

F32 = jnp.float32
BF16 = jnp.bfloat16

D_MODEL = 1024
DEPTH = 4
PAST_LEN = 16384
D_HALF = D_MODEL // 2
CONV_W = 4
LRU_BLOCKS = 8
LRU_BLOCK = D_HALF // LRU_BLOCKS
LRU_C = 8.0
RET_HEADS = 4
RET_DH = D_HALF // RET_HEADS
CHUNK = 128
ROPE_BASE = 10000.0
ML_HEADS = 4
ML_DH = D_HALF // ML_HEADS
ML_QKV_BLOCK = 4
ML_NBLK = D_HALF // ML_QKV_BLOCK
RW_HEADS = 8
RW_DH = D_HALF // RW_HEADS
RW_DECAY_LORA = 32
RW_A_LORA = 32
RW_GATE_LORA = 96
RW_LORA = RW_DECAY_LORA + RW_A_LORA + RW_GATE_LORA
RW_SHIFT_COLS = 3 * D_HALF + RW_LORA
D_FF = 4 * D_MODEL
ALPHA = (2.0 * DEPTH) ** 0.25
EVEN_IN = 6 * D_HALF
ODD_IN = 2 * D_HALF + RW_SHIFT_COLS
LN_EPS = 1e-5
GN_EPS = 1e-5
RW_GN_EPS = 64e-5

LANES = 128
SUBLANES = 8
RW_LORA_PAD = 2 * LANES
RW_COLS_PAD = 3 * D_HALF + RW_LORA_PAD
ODD_IN_PAD = 2 * D_HALF + RW_COLS_PAD
RW_BB = 8
VMEM_LIMIT = 56 * 1024 * 1024


def _params(sem):
    return pltpu.CompilerParams(dimension_semantics=sem, vmem_limit_bytes=VMEM_LIMIT)


def _dot(a, b):
    return jnp.dot(a.astype(BF16), b.astype(BF16), preferred_element_type=F32)


def _dot_nt(a, b):
    return lax.dot_general(a.astype(BF16), b.astype(BF16), (((1,), (1,)), ((), ())),
                           preferred_element_type=F32)


def _dot_tn(a, b):
    return lax.dot_general(a.astype(BF16), b.astype(BF16), (((0,), (0,)), ((), ())),
                           preferred_element_type=F32)


def _split3(a):
    hi = a.astype(BF16)
    r1 = a - hi.astype(F32)
    mid = r1.astype(BF16)
    lo = (r1 - mid.astype(F32)).astype(BF16)
    return hi, mid, lo


def _xdot(a, b01):
    hi, mid, lo = _split3(a)
    f = lambda t: jnp.dot(t, b01, preferred_element_type=F32)
    return f(hi) + f(mid) + f(lo)


def _xdot_l(b01, a):
    hi, mid, lo = _split3(a)
    f = lambda t: jnp.dot(b01, t, preferred_element_type=F32)
    return f(hi) + f(mid) + f(lo)


def _sigmoid(x):
    return 1.0 / (1.0 + jnp.exp(-x))


def _silu(x):
    return x * _sigmoid(x)


def _softplus(x):
    return jnp.maximum(x, 0.0) + jnp.log1p(jnp.exp(-jnp.abs(x)))


def _gelu_tanh(x):
    return 0.5 * x * (1.0 + jnp.tanh(0.7978845608028654 * (x + 0.044715 * (x * x * x))))


def _layer_norm(x, g, b, eps):
    mu = jnp.mean(x, -1, keepdims=True)
    xc = x - mu
    var = jnp.mean(xc * xc, -1, keepdims=True)
    return xc * lax.rsqrt(var + eps) * g + b


def _group_ones(n, group):
    r = lax.broadcasted_iota(jnp.int32, (n, n), 0) // group
    c = lax.broadcasted_iota(jnp.int32, (n, n), 1) // group
    return jnp.where(r == c, 1.0, 0.0).astype(BF16)


def _group_sum(x, ones_bd):
    parts = [_xdot(x[:, s * LANES:(s + 1) * LANES], ones_bd) for s in range(x.shape[1] // LANES)]
    return jnp.concatenate(parts, axis=-1)


def _rotate(x, cosf, sinf):
    return x * cosf + pltpu.roll(x, RET_DH // 2, 1) * sinf


def _proj_kernel(x_ref, w_ref, *o_refs, splits):
    xb = x_ref[...].astype(BF16)
    off = 0
    for o_ref, n in zip(o_refs, splits):
        o_ref[...] = jnp.dot(xb, w_ref[:, off:off + n], preferred_element_type=F32)
        off += n


def _proj(x2d, w_all, j, tm, splits):
    M = x2d.shape[0]
    N = w_all.shape[2]
    return pl.pallas_call(
        functools.partial(_proj_kernel, splits=splits),
        grid=(M // tm,),
        in_specs=[pl.BlockSpec((tm, D_MODEL), lambda i: (i, 0)),
                  pl.BlockSpec((None, D_MODEL, N), lambda i: (j, 0, 0))],
        out_specs=[pl.BlockSpec((tm, n), lambda i: (i, 0)) for n in splits],
        out_shape=[jax.ShapeDtypeStruct((M, n), F32) for n in splits],
        compiler_params=_params(("parallel",)),
        name="proj_in",
    )(x2d, w_all)


FF_CHUNK = 1024


POST_SUB = 256


def _post_kernel(ya_ref, yb_ref, x_ref, wo_ref, g1_ref, b1_ref, w1_ref, w2_ref, g2_ref, b2_ref, o_ref):
    tm = x_ref.shape[0]
    sub = min(tm, POST_SUB)

    def tile(r0):
        rows = slice(r0, r0 + sub)
        y = (jnp.dot(ya_ref[rows, :].astype(BF16), wo_ref[0:D_HALF, :], preferred_element_type=F32)
             + jnp.dot(yb_ref[rows, :].astype(BF16), wo_ref[D_HALF:D_MODEL, :], preferred_element_type=F32))
        yield
        x1 = _layer_norm(ALPHA * x_ref[rows, :] + y, g1_ref[...], b1_ref[...], LN_EPS)
        x1b = x1.astype(BF16)
        acc = jnp.zeros(x1.shape, F32)
        for c in range(D_FF // FF_CHUNK):
            yield
            h = jnp.dot(x1b, w1_ref[:, c * FF_CHUNK:(c + 1) * FF_CHUNK], preferred_element_type=F32)
            yield
            h = jnp.square(jnp.maximum(h, 0.0))
            acc = acc + jnp.dot(h.astype(BF16), w2_ref[c * FF_CHUNK:(c + 1) * FF_CHUNK, :],
                                preferred_element_type=F32)
        yield
        o_ref[rows, :] = _layer_norm(ALPHA * x1 + acc, g2_ref[...], b2_ref[...], LN_EPS)

    _skewed([tile(r0) for r0 in range(0, tm, sub)])


def _post(ya, yb, x2d, wout_all, j, l, ln1_g, ln1_b, w1_all, w2_all, ln2_g, ln2_b, tm):
    M = x2d.shape[0]
    row = lambda i: (i, 0)
    vec = pl.BlockSpec((None, 1, D_MODEL), lambda i: (l, 0, 0))
    r3 = lambda a: a.reshape(DEPTH, 1, D_MODEL)
    return pl.pallas_call(
        _post_kernel,
        grid=(M // tm,),
        in_specs=[pl.BlockSpec((tm, D_HALF), row), pl.BlockSpec((tm, D_HALF), row),
                  pl.BlockSpec((tm, D_MODEL), row),
                  pl.BlockSpec((None, D_MODEL, D_MODEL), lambda i: (j, 0, 0)),
                  vec, vec,
                  pl.BlockSpec((None, D_MODEL, D_FF), lambda i: (l, 0, 0)),
                  pl.BlockSpec((None, D_FF, D_MODEL), lambda i: (l, 0, 0)),
                  vec, vec],
        out_specs=pl.BlockSpec((tm, D_MODEL), row),
        out_shape=jax.ShapeDtypeStruct((M, D_MODEL), F32),
        compiler_params=_params(("parallel",)),
        name="post_mlp",
    )(ya, yb, x2d, wout_all, r3(ln1_g), r3(ln1_b), w1_all, w2_all, r3(ln2_g), r3(ln2_b))


def _conv_prefill(x, xbuf, cw_ref, cb_ref, L):
    xbuf[SUBLANES:SUBLANES + L, :] = x
    y = cb_ref[...]
    for i in range(CONV_W):
        y = y + cw_ref[i:i + 1, :] * xbuf[SUBLANES - (CONV_W - 1) + i:SUBLANES - (CONV_W - 1) + i + L, :]
    tail = xbuf[L + SUBLANES - (CONV_W - 1):L + SUBLANES, :]
    xbuf[SUBLANES - (CONV_W - 1):SUBLANES, :] = tail
    return y, tail


def _lru_gates(xc, wa_ref, ba_ref, wx_ref, bx_ref, lam_ref):
    xcb = xc.astype(BF16)
    nslab = D_HALF // LANES
    ra = jnp.concatenate([jnp.dot(xcb[:, s * LANES:(s + 1) * LANES], wa_ref[s], preferred_element_type=F32)
                          for s in range(nslab)], axis=-1)
    rx = jnp.concatenate([jnp.dot(xcb[:, s * LANES:(s + 1) * LANES], wx_ref[s], preferred_element_type=F32)
                          for s in range(nslab)], axis=-1)
    r = _sigmoid(ra + ba_ref[...])
    i = _sigmoid(rx + bx_ref[...])
    log_a = -LRU_C * r * _softplus(-lam_ref[...])
    a = jnp.exp(log_a)
    t = jnp.tanh(log_a)
    u = jnp.sqrt(-2.0 * t / (1.0 - t)) * (i * xc)
    return a, u


def _head_norm(o, g, b, eps):
    mu = jnp.mean(o, -1, keepdims=True)
    oc = o - mu
    var = jnp.mean(oc * oc, -1, keepdims=True)
    return oc * lax.rsqrt(var + eps) * g + b


EV_NB = 4


class _Cols:
    def __init__(self, ref, off):
        self.ref, self.off = ref, off

    def __getitem__(self, idx):
        if idx is Ellipsis:
            return self.ref[:, self.off:self.off + D_HALF]
        rows, cols = idx
        return self.ref[rows, self.off + cols.start:self.off + cols.stop]


def _even_prefill_kernel(x_ref, w_ref,
                         cw_ref, cb_ref, wa_ref, ba_ref, wx_ref, bx_ref, lam_ref,
                         cos_ref, sin_ref, dmask_ref, qd_ref, kd_ref, cd_ref, gng_ref, gnb_ref,
                         h0_ref, conv0_ref, S0_ref,
                         ya_ref, yb_ref, h_ref, conv_ref, S_ref, xbuf, pbuf, *, L):
    @pl.when(pl.program_id(1) == 0)
    def _():
        h_ref[...] = h0_ref[...]
        S_ref[...] = S0_ref[...]
        xbuf[:, SUBLANES - (CONV_W - 1):SUBLANES, :] = conv0_ref[...]

    rows = [_even_prefill_one(x_ref.at[bi], w_ref, pbuf.at[bi],
                              cw_ref, cb_ref, wa_ref, ba_ref, wx_ref, bx_ref, lam_ref,
                              cos_ref, sin_ref, dmask_ref, qd_ref, kd_ref, cd_ref, gng_ref, gnb_ref,
                              ya_ref.at[bi], yb_ref.at[bi], h_ref.at[bi], conv_ref.at[bi], S_ref.at[bi],
                              xbuf.at[bi], L=L) for bi in range(EV_NB)]
    _round_robin(rows)


def _even_prefill_one(x_ref, w_ref, p_ref,
                      cw_ref, cb_ref, wa_ref, ba_ref, wx_ref, bx_ref, lam_ref,
                      cos_ref, sin_ref, dmask_ref, qd_ref, kd_ref, cd_ref, gng_ref, gnb_ref,
                      ya_ref, yb_ref, h_ref, conv_ref, S_ref, xbuf, *, L):
    p_ref[...] = jnp.dot(x_ref[...].astype(BF16), w_ref[...], preferred_element_type=F32)
    xa_ref, ga_ref, q_ref, k_ref, v_ref, gb_ref = (_Cols(p_ref, i * D_HALF) for i in range(6))
    yield
    H = range(RET_HEADS)
    sls = [slice(hh * RET_DH, (hh + 1) * RET_DH) for hh in H]
    cosf = cos_ref[...]
    sinf = sin_ref[...]
    qh = [_rotate(q_ref[:, sl], cosf, sinf) for sl in sls]
    kh = [_rotate(k_ref[:, sl], cosf, sinf) * (RET_DH ** -0.5) for sl in sls]
    vh = [v_ref[:, sl].astype(BF16) for sl in sls]
    qk = [_dot_nt(qh[hh], kh[hh]) for hh in H]
    S = [S_ref[hh] for hh in H]
    qS = [_dot(qh[hh] * qd_ref[:, sls[hh]], S[hh]) for hh in H]
    kv = [_dot_tn(kh[hh] * kd_ref[:, sls[hh]], vh[hh]) for hh in H]
    yield
    xc, tail = _conv_prefill(xa_ref[...], xbuf, cw_ref, cb_ref, L)
    conv_ref[...] = tail
    a, u = _lru_gates(xc, wa_ref, ba_ref, wx_ref, bx_ref, lam_ref)
    yield
    for hh in H:
        S_ref[hh] = S[hh] * cd_ref[hh] + kv[hh]
    sc = [(qk[hh] * dmask_ref[hh]).astype(BF16) for hh in H]
    o = [jnp.dot(sc[hh], vh[hh], preferred_element_type=F32) + qS[hh] for hh in H]
    yield
    row = lax.broadcasted_iota(jnp.int32, (L, D_HALF), 0) % SUBLANES
    s = 1
    while s < SUBLANES:
        keep = row >= s
        a_sh = jnp.where(keep, pltpu.roll(a, s, 0), 1.0)
        u_sh = jnp.where(keep, pltpu.roll(u, s, 0), 0.0)
        u = a * u_sh + u
        a = a * a_sh
        s *= 2
        yield
    carry = h_ref[...]
    groups = []
    for g in range(L // SUBLANES):
        rows8 = slice(g * SUBLANES, (g + 1) * SUBLANES)
        hg = a[rows8, :] * carry + u[rows8, :]
        carry = hg[SUBLANES - 1:SUBLANES, :]
        groups.append(hg)
        if g % 4 == 3:
            yield
    h = jnp.concatenate(groups, axis=0)
    h_ref[...] = carry
    ya_ref[...] = _gelu_tanh(ga_ref[...]) * h
    ones = jnp.ones((RET_DH, LANES), BF16)
    mu = [jnp.dot(o[hh].astype(BF16), ones, preferred_element_type=F32) * (1.0 / RET_DH) for hh in H]
    yield
    oc = [o[hh] - mu[hh] for hh in H]
    var = [jnp.dot((oc[hh] * oc[hh]).astype(BF16), ones, preferred_element_type=F32) * (1.0 / RET_DH) for hh in H]
    yield
    for hh in H:
        sl = sls[hh]
        on = oc[hh] * lax.rsqrt(var[hh] + GN_EPS) * gng_ref[:, sl] + gnb_ref[:, sl]
        yb_ref[:, sl] = _silu(gb_ref[:, sl]) * on


def _ret_tables(L):
    log_gamma = jnp.log1p(-jnp.exp2(-5.0 - jnp.arange(RET_HEADS, dtype=F32)))
    idx = jnp.arange(L, dtype=F32)
    diff = idx[:, None] - idx[None, :]
    dmask = jnp.where(diff >= 0, jnp.exp(log_gamma[:, None, None] * jnp.maximum(diff, 0.0)), 0.0)
    qd = jnp.exp(log_gamma[:, None] * (idx + 1.0))
    kd = jnp.exp(log_gamma[:, None] * (L - 1.0 - idx))
    cd = jnp.exp(log_gamma * L)
    qd_full = jnp.repeat(qd.T, RET_DH, axis=1)
    kd_full = jnp.repeat(kd.T, RET_DH, axis=1)
    cd_full = jnp.broadcast_to(cd[:, None, None], (RET_HEADS, 1, RET_DH))
    return dmask, qd_full, kd_full, cd_full


def _rope_tables(pos):
    half = RET_DH // 2
    inv = ROPE_BASE ** (-jnp.arange(half, dtype=F32) / half)
    ang = pos.astype(F32)[:, None] * inv[None, :]
    cos, sin = jnp.cos(ang), jnp.sin(ang)
    return jnp.concatenate([cos, cos], -1), jnp.concatenate([-sin, sin], -1)


def _lru_dense(w):
    pairs = LANES // LRU_BLOCK
    w4 = w.reshape(LRU_BLOCKS // pairs, pairs, LRU_BLOCK, LRU_BLOCK)
    eye = jnp.eye(pairs, dtype=w.dtype)
    d = w4[:, :, :, None, :] * eye[None, :, None, :, None]
    return d.reshape(LRU_BLOCKS // pairs, LANES, LANES).astype(BF16)


def _full(shape):
    n = len(shape)
    return pl.BlockSpec(shape, lambda *_: (0,) * n)


def _even_prefill(x3, w_in_all, j, pos, st, Wl):
    B, T, _ = x3.shape
    L = CHUNK
    nc = T // L
    dmask, qd, kd, cd = _ret_tables(L)
    cosf, sinf = _rope_tables(pos)
    nb = EV_NB
    perb = lambda *s: pl.BlockSpec((nb,) + s, lambda b, c: (b,) + (0,) * len(s))
    row2 = lambda a: a.reshape(1, D_HALF)
    ins = [x3, w_in_all, Wl['lru_conv_w'], row2(Wl['lru_conv_b']), Wl['lru_wa_d'], row2(Wl['lru_ba']),
           Wl['lru_wx_d'], row2(Wl['lru_bx']), row2(Wl['lru_lambda']),
           cosf, sinf, dmask, qd, kd, cd, row2(Wl['ret_gn_g']), row2(Wl['ret_gn_b']),
           st['lru_h'].reshape(B, 1, D_HALF), st['lru_conv'], st['ret_S']]
    in_specs = [pl.BlockSpec((nb, L, D_MODEL), lambda b, c: (b, c, 0)),
                pl.BlockSpec((None, D_MODEL, EVEN_IN), lambda b, c: (j, 0, 0))] + [
        _full((CONV_W, D_HALF)), _full((1, D_HALF)), _full((4, LANES, LANES)), _full((1, D_HALF)),
        _full((4, LANES, LANES)), _full((1, D_HALF)), _full((1, D_HALF)),
        pl.BlockSpec((L, RET_DH), lambda b, c: (c, 0)), pl.BlockSpec((L, RET_DH), lambda b, c: (c, 0)),
        _full((RET_HEADS, L, L)), _full((L, D_HALF)), _full((L, D_HALF)), _full((RET_HEADS, 1, RET_DH)),
        _full((1, D_HALF)), _full((1, D_HALF)),
        perb(1, D_HALF), perb(CONV_W - 1, D_HALF), perb(RET_HEADS, RET_DH, RET_DH)]
    seq = pl.BlockSpec((nb, L, D_HALF), lambda b, c: (b, c, 0))
    ya, yb, h, conv, S = pl.pallas_call(
        functools.partial(_even_prefill_kernel, L=L),
        grid=(B // nb, nc),
        in_specs=in_specs,
        out_specs=[seq, seq, perb(1, D_HALF), perb(CONV_W - 1, D_HALF), perb(RET_HEADS, RET_DH, RET_DH)],
        out_shape=[jax.ShapeDtypeStruct((B, T, D_HALF), F32), jax.ShapeDtypeStruct((B, T, D_HALF), F32),
                   jax.ShapeDtypeStruct((B, 1, D_HALF), F32),
                   jax.ShapeDtypeStruct((B, CONV_W - 1, D_HALF), F32),
                   jax.ShapeDtypeStruct((B, RET_HEADS, RET_DH, RET_DH), F32)],
        scratch_shapes=[pltpu.VMEM((nb, L + SUBLANES, D_HALF), F32), pltpu.VMEM((nb, L, EVEN_IN), F32)],
        compiler_params=_params(("parallel", "arbitrary")),
        name="even_prefill",
    )(*ins)
    return ya, yb, (h.reshape(B, D_HALF), conv, S)


DEC_BB = 8


def _even_decode_kernel(p_ref, cw_ref, cb_ref, wa_ref, ba_ref, wx_ref, bx_ref, lam_ref,
                        cos_ref, sin_ref, dm_ref, qd_ref, kd_ref, cd_ref, gng_ref, gnb_ref,
                        h0_ref, conv0_ref, S0_ref,
                        ya_ref, yb_ref, h_ref, conv_ref, S_ref):
    col = lambda i: p_ref[:, i * D_HALF:(i + 1) * D_HALF]
    xa = col(0)
    xc = cb_ref[...] + cw_ref[CONV_W - 1:CONV_W, :] * xa
    for i in range(CONV_W - 1):
        xc = xc + cw_ref[i:i + 1, :] * conv0_ref[i]
    for i in range(CONV_W - 2):
        conv_ref[i] = conv0_ref[i + 1]
    conv_ref[CONV_W - 2] = xa
    a, u = _lru_gates(xc, wa_ref, ba_ref, wx_ref, bx_ref, lam_ref)
    h = a * h0_ref[...] + u
    h_ref[...] = h
    ya_ref[...] = _gelu_tanh(col(1)) * h

    cosf = cos_ref[...]
    sinf = sin_ref[...]
    row8 = lax.broadcasted_iota(jnp.int32, (SUBLANES, RET_DH), 0)
    q, k, v, gb = col(2), col(3), col(4), col(5)
    for hh in range(RET_HEADS):
        sl = slice(hh * RET_DH, (hh + 1) * RET_DH)
        qh = _rotate(q[:, sl], cosf, sinf)
        kh = _rotate(k[:, sl], cosf, sinf) * (RET_DH ** -0.5)
        vh = v[:, sl]
        qk = jnp.sum(qh * kh, -1, keepdims=True) * dm_ref[:, sl]
        qq = qh * qd_ref[:, sl]
        kk = kh * kd_ref[:, sl]
        rows = []
        for bi in range(DEC_BB):
            S = S0_ref[bi, hh]
            q8 = jnp.broadcast_to(qq[bi:bi + 1, :], (SUBLANES, RET_DH))
            rows.append(_dot_f32(q8, S)[0:1, :])
            k8 = jnp.where(row8 == 0, jnp.broadcast_to(kk[bi:bi + 1, :], (SUBLANES, RET_DH)), 0.0)
            v8 = jnp.broadcast_to(vh[bi:bi + 1, :], (SUBLANES, RET_DH))
            S_ref[bi, hh] = S * cd_ref[hh] + _dot_tn_f32(k8, v8)
        o = qk * vh + jnp.concatenate(rows, axis=0)
        on = _head_norm(o, gng_ref[:, sl], gnb_ref[:, sl], GN_EPS)
        yb_ref[:, sl] = _silu(gb[:, sl]) * on


def _dot_f32(a, b):
    return jnp.dot(a, b, preferred_element_type=F32)


def _dot_tn_f32(a, b):
    return lax.dot_general(a, b, (((0,), (0,)), ((), ())), preferred_element_type=F32)


def _skip_ref(kernel, pos):
    def wrapped(*refs):
        kernel(*refs[:pos], *refs[pos + 1:])
    return wrapped


def _all_layers(kernel, pos_in, pos_out, j, n_layers):
    def wrapped(*refs):
        refs = list(refs)
        s_in, s_out = refs[pos_in], refs[pos_out]
        for other in range(n_layers):
            if other != j:
                s_out[other] = s_in[other]
        refs[pos_in], refs[pos_out] = s_in.at[j], s_out.at[j]
        kernel(*refs)
    return wrapped


def _stacked_state_io(kernel, S_all, S_prev, j, bb, pos_in, n_in, out_idx):
    n_layers, tail = S_all.shape[0], S_all.shape[2:]
    zeros = (0,) * len(tail)
    shape = jax.ShapeDtypeStruct(S_all.shape, F32)
    if S_prev is None:
        spec = pl.BlockSpec((n_layers, bb) + tail, lambda i: (0, i) + zeros)
        return _all_layers(kernel, pos_in, n_in + out_idx, j, n_layers), spec, [], [], {}, shape
    spec = pl.BlockSpec((None, bb) + tail, lambda i: (j, i) + zeros)
    return _skip_ref(kernel, n_in), spec, [S_prev], [pl.BlockSpec(memory_space=pl.ANY)], {n_in: out_idx}, shape


def _even_decode(p2, pos, st, Wl, S_all, j, S_prev):
    B = p2.shape[0]
    bb = DEC_BB
    dmask, qd, kd, cd = _ret_tables(1)
    dm = jnp.repeat(dmask[:, 0, :].T, RET_DH, axis=1)
    cosf, sinf = _rope_tables(pos)
    row2 = lambda a: a.reshape(1, D_HALF)
    rows = lambda n: pl.BlockSpec((bb, n), lambda i: (i, 0))
    convs = pl.BlockSpec((CONV_W - 1, bb, D_HALF), lambda i: (0, i, 0))
    ins = [p2, Wl['lru_conv_w'], row2(Wl['lru_conv_b']), Wl['lru_wa_d'], row2(Wl['lru_ba']),
           Wl['lru_wx_d'], row2(Wl['lru_bx']), row2(Wl['lru_lambda']),
           cosf, sinf, dm, qd, kd, cd, row2(Wl['ret_gn_g']), row2(Wl['ret_gn_b']),
           st['lru_h'], jnp.swapaxes(st['lru_conv'], 0, 1), S_all]
    kern, Ss, extra_in, extra_specs, aliases, S_shape = _stacked_state_io(
        _even_decode_kernel, S_all, S_prev, j, bb, len(ins) - 1, len(ins), 4)
    in_specs = [rows(EVEN_IN), _full((CONV_W, D_HALF)), _full((1, D_HALF)), _full((4, LANES, LANES)),
                _full((1, D_HALF)), _full((4, LANES, LANES)), _full((1, D_HALF)), _full((1, D_HALF)),
                _full((1, RET_DH)), _full((1, RET_DH)), _full((1, D_HALF)), _full((1, D_HALF)),
                _full((1, D_HALF)), _full((RET_HEADS, 1, RET_DH)), _full((1, D_HALF)), _full((1, D_HALF)),
                rows(D_HALF), convs, Ss] + extra_specs
    ya, yb, h, conv, S = pl.pallas_call(
        kern,
        grid=(B // bb,),
        in_specs=in_specs,
        out_specs=[rows(D_HALF), rows(D_HALF), rows(D_HALF), convs, Ss],
        out_shape=[jax.ShapeDtypeStruct((B, D_HALF), F32), jax.ShapeDtypeStruct((B, D_HALF), F32),
                   jax.ShapeDtypeStruct((B, D_HALF), F32),
                   jax.ShapeDtypeStruct((CONV_W - 1, B, D_HALF), F32), S_shape],
        input_output_aliases=aliases,
        compiler_params=_params(("parallel",)),
        name="even_decode",
    )(*ins, *extra_in)
    return ya, yb, (h, jnp.swapaxes(conv, 0, 1), S)


def _mlstm_qkv_gates(xm, xc, wq_ref, wk_ref, wv_ref, wg_ref, bg_ref):
    q = _dot(xc, wq_ref[...])
    k = _dot(xc, wk_ref[...])
    v = _dot(xm, wv_ref[...])
    g_col = (_dot(q, wg_ref[0:D_HALF, :]) + _dot(k, wg_ref[D_HALF:2 * D_HALF, :])
             + _dot(v, wg_ref[2 * D_HALF:3 * D_HALF, :]) + bg_ref[...])
    return q, k, v, g_col


ML_NB = 4


def _skewed(gens, lag=1):
    live = []
    pending = list(gens)
    tick = 0
    while pending or live:
        if pending and tick % lag == 0:
            live.append(pending.pop(0))
        tick += 1
        nxt = []
        for g in live:
            try:
                next(g)
                nxt.append(g)
            except StopIteration:
                pass
        live = nxt


def _round_robin(gens):
    gens = list(gens)
    while gens:
        alive = []
        for g in gens:
            try:
                next(g)
                alive.append(g)
            except StopIteration:
                pass
        gens = alive


def _mlstm_prefill_kernel(x_ref, w_ref, cw_ref, cb_ref, wq_ref, wk_ref, wv_ref, wg_ref, bg_ref,
                          wgt_ref, bgt_ref, gng_ref, gnb_ref, skip_ref,
                          conv0_ref, C0_ref, n0_ref, m0_ref,
                          yc_ref, conv_ref, C_ref, n_ref, m_ref, xbuf, ncol, pbuf, *, L):
    @pl.when(pl.program_id(1) == 0)
    def _():
        ones = jnp.ones((ML_DH, LANES), BF16)
        eye = jnp.where(lax.broadcasted_iota(jnp.int32, (ML_DH, LANES), 0)
                        == lax.broadcasted_iota(jnp.int32, (ML_DH, LANES), 1), 1.0, 0.0)
        C_ref[...] = C0_ref[...]
        m_ref[...] = m0_ref[...]
        for bi in range(ML_NB):
            xbuf[bi, SUBLANES - (CONV_W - 1):SUBLANES, :] = conv0_ref[bi]
            for hh in range(ML_HEADS):
                ncol[bi, hh] = _xdot(eye * n0_ref[bi, :, hh * ML_DH:(hh + 1) * ML_DH], ones)

    rows = [_mlstm_prefill_one(x_ref.at[bi], w_ref, pbuf.at[bi], cw_ref, cb_ref, wq_ref, wk_ref, wv_ref, wg_ref,
                               bg_ref, wgt_ref, bgt_ref, gng_ref, gnb_ref, skip_ref,
                               yc_ref.at[bi], conv_ref.at[bi], C_ref.at[bi], n_ref.at[bi], m_ref.at[bi],
                               xbuf.at[bi], ncol.at[bi], L=L) for bi in range(ML_NB)]
    _round_robin(rows)


def _mlstm_prefill_one(x_ref, w_ref, p_ref, cw_ref, cb_ref, wq_ref, wk_ref, wv_ref, wg_ref, bg_ref,
                       wgt_ref, bgt_ref, gng_ref, gnb_ref, skip_ref,
                       yc_ref, conv_ref, C_ref, n_ref, m_ref, xbuf, ncol, *, L):
    p_ref[...] = jnp.dot(x_ref[...].astype(BF16), w_ref[...], preferred_element_type=F32)
    xm_ref, z_ref = _Cols(p_ref, 0), _Cols(p_ref, D_HALF)
    yield
    xm = xm_ref[...]
    xc, tail = _conv_prefill(xm, xbuf, cw_ref, cb_ref, L)
    conv_ref[...] = tail
    xc = _silu(xc)
    yield
    q, k, v, g_col = _mlstm_qkv_gates(xm, xc, wq_ref, wk_ref, wv_ref, wg_ref, bg_ref)
    g_row = (_dot_nt(wgt_ref[:, 0:D_HALF], q) + _dot_nt(wgt_ref[:, D_HALF:2 * D_HALF], k)
             + _dot_nt(wgt_ref[:, 2 * D_HALF:3 * D_HALF], v) + bgt_ref[...])
    yield
    ri = lax.broadcasted_iota(jnp.int32, (L, L), 0)
    ci = lax.broadcasted_iota(jnp.int32, (L, L), 1)
    causal = ri >= ci
    tril = jnp.where(causal, 1.0, 0.0).astype(BF16)
    triu = jnp.where(ci >= ri, 1.0, 0.0).astype(BF16)
    ones = jnp.ones((L, LANES), BF16)
    eye = jnp.where(ri == ci, 1.0, 0.0)

    li_col = g_col
    lf_col = -_softplus(-pltpu.roll(g_col, LANES - ML_HEADS, 1))
    b_col = _xdot_l(tril, lf_col)
    lf_row = -_softplus(-g_row)
    b_row = _xdot(lf_row, triu)
    yield
    c_row = g_row[0:ML_HEADS, :] - b_row[ML_HEADS:2 * ML_HEADS, :]
    row = lax.broadcasted_iota(jnp.int32, (L, LANES), 0)
    pm = li_col - b_col
    sft = 1
    while sft < L:
        pm = jnp.maximum(pm, jnp.where(row >= sft, pltpu.roll(pm, sft, 0), -jnp.inf))
        sft *= 2
    m_prev = m_ref[...]
    u_col = -jnp.maximum(pm, m_prev)
    m_t_col = b_col - u_col
    e_col = jnp.exp(-m_t_col)
    m_new = m_t_col[L - 1:L, :]
    b_last = b_col[L - 1:L, :]
    wk_col = jnp.exp(b_last - b_col + li_col - m_new)
    wC_row = jnp.exp(b_last + m_prev - m_new)
    m_ref[...] = m_new
    yield
    vones = ones
    rep = lambda col, hh: jnp.broadcast_to(col[:, hh:hh + 1], (L, LANES))
    H = range(ML_HEADS)
    sls = [slice(hh * ML_DH, (hh + 1) * ML_DH) for hh in H]
    qh = [q[:, sl].astype(BF16) for sl in sls]
    kh = [k[:, sl] * (ML_DH ** -0.5) for sl in sls]
    vh1 = [jnp.concatenate([v[:, sl].astype(BF16), vones], axis=-1) for sl in sls]
    qk = [_dot_nt(qh[hh], kh[hh]) for hh in H]
    CN = [jnp.concatenate([C_ref[hh], ncol[hh]], axis=-1) for hh in H]
    qc = [jnp.dot(qh[hh], CN[hh].astype(BF16), preferred_element_type=F32) for hh in H]
    kw = [kh[hh] * rep(wk_col, hh) for hh in H]
    upd = [lax.dot_general(kw[hh].astype(BF16), vh1[hh], (((0,), (0,)), ((), ())), preferred_element_type=F32)
           for hh in H]
    yield
    for hh in H:
        w_C =jnp.broadcast_to(wC_row[:, hh:hh + 1], (ML_DH, 2 * ML_DH))
        CNn = w_C * CN[hh] + upd[hh]
        C_ref[hh] = CNn[:, 0:ML_DH]
        ncol[hh] = CNn[:, ML_DH:2 * ML_DH]
        n_ref[:, sls[hh]] = jnp.sum(CNn[:, ML_DH:2 * ML_DH] * eye, axis=0, keepdims=True)
    yield
    u = [rep(u_col, hh) for hh in H]
    s = [(qk[hh] * jnp.exp(jnp.where(causal, u[hh] + c_row[hh:hh + 1, :], -jnp.inf))).astype(BF16) for hh in H]
    sv = [jnp.dot(s[hh], vh1[hh], preferred_element_type=F32) for hh in H]
    yield
    hcell = []
    for hh in H:
        w_inter = jnp.exp(rep(m_prev, hh) + u[hh])
        num = sv[hh][:, 0:ML_DH] + w_inter * qc[hh][:, 0:ML_DH]
        den = sv[hh][:, ML_DH:2 * ML_DH] + w_inter * qc[hh][:, ML_DH:2 * ML_DH]
        hcell.append(num / jnp.maximum(jnp.abs(den), rep(e_col, hh)))
    mu = [jnp.dot(hcell[hh].astype(BF16), ones, preferred_element_type=F32) * (1.0 / ML_DH) for hh in H]
    yield
    oc = [hcell[hh] - mu[hh] for hh in H]
    var = [jnp.dot((oc[hh] * oc[hh]).astype(BF16), ones, preferred_element_type=F32) * (1.0 / ML_DH) for hh in H]
    for hh in H:
        sl = sls[hh]
        hn = oc[hh] * lax.rsqrt(var[hh] + GN_EPS) * gng_ref[:, sl] + gnb_ref[:, sl]
        yc_ref[:, sl] = (hn + skip_ref[:, sl] * xc[:, sl]) * _silu(z_ref[:, sl])


def _ml_dense(w):
    w2 = w.reshape(D_HALF, ML_QKV_BLOCK)
    c = jnp.arange(D_HALF)
    spread = (c[None, :] % ML_QKV_BLOCK == jnp.arange(ML_QKV_BLOCK)[:, None]).astype(w.dtype)
    full = jnp.dot(w2, spread, precision=lax.Precision.HIGHEST)
    same_block = c[:, None] // ML_QKV_BLOCK == c[None, :] // ML_QKV_BLOCK
    return jnp.where(same_block, full, 0.0).astype(BF16)


def _pad_lanes(a, n=LANES):
    return jnp.pad(a, [(0, 0)] * (a.ndim - 1) + [(0, n - a.shape[-1])])


def _mlstm_weights(Wl):
    wg = _pad_lanes(Wl['ml_w_gate']).astype(BF16)
    bg = _pad_lanes(Wl['ml_b_gate'].reshape(1, 2 * ML_HEADS))
    wgt = Wl['ml_w_gate'].T.astype(BF16)
    bgt = jnp.broadcast_to(Wl['ml_b_gate'].reshape(2 * ML_HEADS, 1), (2 * ML_HEADS, LANES))
    return wg, bg, wgt, bgt


def _mlstm_prefill(x3, w_in_all, j, st, Wl):
    B, T, _ = x3.shape
    L = CHUNK
    nc = T // L
    wg, bg, wgt, bgt = _mlstm_weights(Wl)
    nb = ML_NB
    perb = lambda *s: pl.BlockSpec((nb,) + s, lambda b, c: (b,) + (0,) * len(s))
    row2 = lambda a: a.reshape(1, D_HALF)
    ins = [x3, w_in_all, Wl['ml_conv_w'], row2(Wl['ml_conv_b']), Wl['ml_wq_d'], Wl['ml_wk_d'], Wl['ml_wv_d'],
           wg, bg, wgt, bgt, row2(Wl['ml_gn_g']), row2(Wl['ml_gn_b']), row2(Wl['ml_skip']),
           st['ml_conv'], st['ml_C'], st['ml_n'].reshape(B, 1, D_HALF),
           _pad_lanes(st['ml_m']).reshape(B, 1, LANES)]
    in_specs = [pl.BlockSpec((nb, L, D_MODEL), lambda b, c: (b, c, 0)),
                pl.BlockSpec((None, D_MODEL, 2 * D_HALF), lambda b, c: (j, 0, 0)),
                _full((CONV_W, D_HALF)), _full((1, D_HALF)),
                _full((D_HALF, D_HALF)), _full((D_HALF, D_HALF)), _full((D_HALF, D_HALF)),
                _full((3 * D_HALF, LANES)), _full((1, LANES)), _full((2 * ML_HEADS, 3 * D_HALF)),
                _full((2 * ML_HEADS, LANES)), _full((1, D_HALF)), _full((1, D_HALF)), _full((1, D_HALF)),
                perb(CONV_W - 1, D_HALF), perb(ML_HEADS, ML_DH, ML_DH), perb(1, D_HALF), perb(1, LANES)]
    seq = pl.BlockSpec((nb, L, D_HALF), lambda b, c: (b, c, 0))
    yc, conv, C, n, m = pl.pallas_call(
        functools.partial(_mlstm_prefill_kernel, L=L),
        grid=(B // nb, nc),
        in_specs=in_specs,
        out_specs=[seq, perb(CONV_W - 1, D_HALF), perb(ML_HEADS, ML_DH, ML_DH), perb(1, D_HALF), perb(1, LANES)],
        out_shape=[jax.ShapeDtypeStruct((B, T, D_HALF), F32),
                   jax.ShapeDtypeStruct((B, CONV_W - 1, D_HALF), F32),
                   jax.ShapeDtypeStruct((B, ML_HEADS, ML_DH, ML_DH), F32),
                   jax.ShapeDtypeStruct((B, 1, D_HALF), F32),
                   jax.ShapeDtypeStruct((B, 1, LANES), F32)],
        scratch_shapes=[pltpu.VMEM((nb, L + SUBLANES, D_HALF), F32),
                        pltpu.VMEM((nb, ML_HEADS, ML_DH, LANES), F32),
                        pltpu.VMEM((nb, L, 2 * D_HALF), F32)],
        compiler_params=_params(("parallel", "arbitrary")),
        name="mlstm_prefill",
    )(*ins)
    return yc, (C, n.reshape(B, ML_HEADS, ML_DH), m[:, 0, :ML_HEADS], conv)


def _mlstm_decode_kernel(p_ref, cw_ref, cb_ref, wq_ref, wk_ref, wv_ref, wg_ref, bg_ref,
                         gng_ref, gnb_ref, skip_ref, conv0_ref, C0_ref, n0_ref, m0_ref,
                         yc_ref, conv_ref, C_ref, n_ref, m_ref):
    xm = p_ref[:, 0:D_HALF]
    z = p_ref[:, D_HALF:2 * D_HALF]
    xc = cb_ref[...] + cw_ref[CONV_W - 1:CONV_W, :] * xm
    for i in range(CONV_W - 1):
        xc = xc + cw_ref[i:i + 1, :] * conv0_ref[i]
    for i in range(CONV_W - 2):
        conv_ref[i] = conv0_ref[i + 1]
    conv_ref[CONV_W - 2] = xm
    xc = _silu(xc)
    q, k, v, g = _mlstm_qkv_gates(xm, xc, wq_ref, wk_ref, wv_ref, wg_ref, bg_ref)
    lf_all = -_softplus(-g)
    lane = lax.broadcasted_iota(jnp.int32, (1, LANES), 1)
    row8 = lax.broadcasted_iota(jnp.int32, (SUBLANES, ML_DH), 0)
    m_all = m0_ref[...]
    m_out = m_all
    for hh in range(ML_HEADS):
        sl = slice(hh * ML_DH, (hh + 1) * ML_DH)
        qh, vh = q[:, sl], v[:, sl]
        kh = k[:, sl] * (ML_DH ** -0.5)
        li = g[:, hh:hh + 1]
        lf = lf_all[:, ML_HEADS + hh:ML_HEADS + hh + 1]
        m_prev = m_all[:, hh:hh + 1]
        n = n0_ref[:, sl]
        log_inter = lf + m_prev
        m_t = jnp.maximum(li, log_inter)
        s = jnp.sum(qh * kh, -1, keepdims=True) * jnp.exp(li - m_t)
        w_inter = jnp.exp(log_inter - m_t)
        w_k = jnp.exp(li - m_t)
        w_C = jnp.exp(log_inter - m_t)
        kw = kh * w_k
        rows = []
        for bi in range(DEC_BB):
            C = C0_ref[bi, hh]
            q8 = jnp.broadcast_to(qh[bi:bi + 1, :], (SUBLANES, ML_DH))
            rows.append(_dot_f32(q8, C)[0:1, :])
            k8 = jnp.where(row8 == 0, jnp.broadcast_to(kw[bi:bi + 1, :], (SUBLANES, ML_DH)), 0.0)
            v8 = jnp.broadcast_to(vh[bi:bi + 1, :], (SUBLANES, ML_DH))
            C_ref[bi, hh] = w_C[bi:bi + 1, :] * C + _dot_tn_f32(k8, v8)
        qC = jnp.concatenate(rows, axis=0)
        num = s * vh + w_inter * qC
        den = s + w_inter * jnp.sum(qh * n, -1, keepdims=True)
        hcell = num / jnp.maximum(jnp.abs(den), jnp.exp(-m_t))
        n_ref[:, sl] = w_C * n + kw
        m_out = jnp.where(lane == hh, m_t, m_out)
        hn = _head_norm(hcell, gng_ref[:, sl], gnb_ref[:, sl], GN_EPS)
        yc_ref[:, sl] = (hn + skip_ref[:, sl] * xc[:, sl]) * _silu(z[:, sl])
    m_ref[...] = m_out


def _mlstm_decode(p2, st, Wl, C_all, j, C_prev):
    B = p2.shape[0]
    bb = DEC_BB
    wg, bg, _, _ = _mlstm_weights(Wl)
    row2 = lambda a: a.reshape(1, D_HALF)
    rows = lambda n: pl.BlockSpec((bb, n), lambda i: (i, 0))
    convs = pl.BlockSpec((CONV_W - 1, bb, D_HALF), lambda i: (0, i, 0))
    ins = [p2, Wl['ml_conv_w'], row2(Wl['ml_conv_b']), Wl['ml_wq_d'], Wl['ml_wk_d'], Wl['ml_wv_d'], wg, bg,
           row2(Wl['ml_gn_g']), row2(Wl['ml_gn_b']), row2(Wl['ml_skip']),
           jnp.swapaxes(st['ml_conv'], 0, 1), C_all, st['ml_n'].reshape(B, D_HALF), _pad_lanes(st['ml_m'])]
    kern, Cs, extra_in, extra_specs, aliases, C_shape = _stacked_state_io(
        _mlstm_decode_kernel, C_all, C_prev, j, bb, len(ins) - 3, len(ins), 2)
    in_specs = [pl.BlockSpec((bb, 2 * D_HALF), lambda i: (i, 0)), _full((CONV_W, D_HALF)), _full((1, D_HALF)),
                _full((D_HALF, D_HALF)), _full((D_HALF, D_HALF)), _full((D_HALF, D_HALF)),
                _full((3 * D_HALF, LANES)), _full((1, LANES)),
                _full((1, D_HALF)), _full((1, D_HALF)), _full((1, D_HALF)),
                convs, Cs, rows(D_HALF), rows(LANES)] + extra_specs
    yc, conv, C, n, m = pl.pallas_call(
        kern,
        grid=(B // bb,),
        in_specs=in_specs,
        out_specs=[rows(D_HALF), convs, Cs, rows(D_HALF), rows(LANES)],
        out_shape=[jax.ShapeDtypeStruct((B, D_HALF), F32),
                   jax.ShapeDtypeStruct((CONV_W - 1, B, D_HALF), F32),
                   C_shape,
                   jax.ShapeDtypeStruct((B, D_HALF), F32),
                   jax.ShapeDtypeStruct((B, LANES), F32)],
        input_output_aliases=aliases,
        compiler_params=_params(("parallel",)),
        name="mlstm_decode",
    )(*ins, *extra_in)
    return yc, (C, n.reshape(B, ML_HEADS, ML_DH), m[:, :ML_HEADS], jnp.swapaxes(conv, 0, 1))


def _rwkv_pre_body(pr, pr_prev, mu_ref, w0_ref, a0_ref, w2_ref, a2_ref, g2_ref, kkw_ref, kaw_ref, rk_ref,
                   r_ref, d_ref, k_ref, v_ref, a_ref, b_ref, g_ref, bonus_ref):
    pm = pr + (pr_prev - pr) * mu_ref[...]
    r = pm[:, 0:D_HALF]
    kr = pm[:, D_HALF:2 * D_HALF]
    vr = pm[:, 2 * D_HALF:3 * D_HALF]
    lo = pm[:, 3 * D_HALF:RW_COLS_PAD]
    w_log = -_softplus(-(w0_ref[...] + _dot(jnp.tanh(lo), w2_ref[...]))) - 0.5
    a = _sigmoid(a0_ref[...] + _dot(lo, a2_ref[...]))
    g = _dot(_sigmoid(lo), g2_ref[...])
    ones_bd = _group_ones(LANES, RW_DH)
    kk = kr * kkw_ref[...]
    kk = kk / jnp.maximum(jnp.sqrt(_group_sum(kk * kk, ones_bd)), 1e-12)
    kh = kr * (1.0 + (a - 1.0) * kaw_ref[...])
    r_ref[...] = r
    d_ref[...] = jnp.exp(-jnp.exp(w_log))
    k_ref[...] = kh
    v_ref[...] = vr
    a_ref[...] = -kk
    b_ref[...] = kk * a
    g_ref[...] = g
    bonus_ref[...] = _group_sum(r * kh * rk_ref[...], ones_bd) * vr


def _rwkv_pre_prefill_kernel(pr_ref, shift0_ref, *rest, L):
    wrefs, outs, xbuf = rest[:9], rest[9:17], rest[17]
    c = pl.program_id(1)

    @pl.when(c == 0)
    def _():
        xbuf[SUBLANES - 1:SUBLANES, :] = shift0_ref[...]

    pr = pr_ref[...]
    xbuf[SUBLANES:SUBLANES + L, :] = pr
    pr_prev = xbuf[SUBLANES - 1:SUBLANES - 1 + L, :]
    xbuf[SUBLANES - 1:SUBLANES, :] = pr[L - 1:L, :]
    _rwkv_pre_body(pr, pr_prev, *wrefs, *outs)


def _rwkv_pre_decode_kernel(pr_ref, prev_ref, *rest):
    _rwkv_pre_body(pr_ref[...], prev_ref[...], *rest[:9], *rest[9:17])


def _rwkv_pre_weights(Wl):
    row2 = lambda a: a.reshape(1, D_HALF)
    padr = lambda w, o: jnp.pad(w, ((o, RW_LORA_PAD - o - w.shape[0]), (0, 0))).astype(BF16)
    mu = _pad_lanes(Wl['rw_mu'].reshape(1, RW_SHIFT_COLS), RW_COLS_PAD)
    ws = [mu, row2(Wl['rw_w0']), row2(Wl['rw_a0']),
          padr(Wl['rw_w2'], 0), padr(Wl['rw_a2'], RW_DECAY_LORA), padr(Wl['rw_g2'], RW_DECAY_LORA + RW_A_LORA),
          row2(Wl['rw_kk']), row2(Wl['rw_ka']), row2(Wl['rw_rk'])]
    specs = [_full((1, RW_COLS_PAD)), _full((1, D_HALF)), _full((1, D_HALF)),
             _full((RW_LORA_PAD, D_HALF)), _full((RW_LORA_PAD, D_HALF)), _full((RW_LORA_PAD, D_HALF)),
             _full((1, D_HALF)), _full((1, D_HALF)), _full((1, D_HALF))]
    return ws, specs


def _rwkv_pre_prefill(pr3, shift0, Wl):
    B, T, _ = pr3.shape
    L = CHUNK
    ws, wspecs = _rwkv_pre_weights(Wl)
    seq = pl.BlockSpec((None, L, D_HALF), lambda b, c: (b, c, 0))
    outs = pl.pallas_call(
        functools.partial(_rwkv_pre_prefill_kernel, L=L),
        grid=(B, T // L),
        in_specs=[pl.BlockSpec((None, L, RW_COLS_PAD), lambda b, c: (b, c, 0)),
                  pl.BlockSpec((None, 1, RW_COLS_PAD), lambda b, c: (b, 0, 0))] + wspecs,
        out_specs=[seq] * 8,
        out_shape=[jax.ShapeDtypeStruct((B, T, D_HALF), F32)] * 8,
        scratch_shapes=[pltpu.VMEM((L + SUBLANES, RW_COLS_PAD), F32)],
        compiler_params=_params(("parallel", "arbitrary")),
        name="rwkv_pre_prefill",
    )(pr3, _pad_lanes(shift0, RW_COLS_PAD).reshape(B, 1, RW_COLS_PAD), *ws)
    return outs


def _rwkv_pre_decode(pr, shift0, Wl):
    B = pr.shape[0]
    ws, wspecs = _rwkv_pre_weights(Wl)
    full2 = lambda n: pl.BlockSpec((B, n), lambda i: (0, 0))
    outs = pl.pallas_call(
        _rwkv_pre_decode_kernel,
        grid=(1,),
        in_specs=[full2(RW_COLS_PAD), full2(RW_COLS_PAD)] + wspecs,
        out_specs=[full2(D_HALF)] * 8,
        out_shape=[jax.ShapeDtypeStruct((B, D_HALF), F32)] * 8,
        compiler_params=_params(("arbitrary",)),
        name="rwkv_pre_decode",
    )(pr, _pad_lanes(shift0, RW_COLS_PAD), *ws)
    return outs


RW_IP = RW_DH // 2


def _rwkv_rec_kernel(r_ref, d_ref, k_ref, a_ref, b_ref, v_ref, S0_ref, y_ref, S_ref, *, Tc):
    @pl.when(pl.program_id(0) == 0)
    def _():
        S_ref[...] = S0_ref[...]

    lane = lax.broadcasted_iota(jnp.int32, (1, LANES), 1)

    def tiles(t):
        back = (LANES - (t % RW_TB) * RW_BB) % LANES
        out = []
        for ref in (a_ref, d_ref, b_ref, k_ref, r_ref):
            raw = ref[t]
            out.append(jnp.where(lane < LANES // 2, pltpu.roll(raw, back, 1),
                                 pltpu.roll(raw, (back + LANES // 2) % LANES, 1)))
        out.append(pltpu.roll(v_ref[t], back, 1))
        return tuple(out)

    def step(t, carry):
        a, d, b, k, r, vt = carry
        nxt = tiles(jnp.minimum(t + 1, Tc - 1))
        rows = []
        for ip in range(RW_IP):
            S = S_ref[ip]
            sa = jnp.sum(S * a, axis=0, keepdims=True)
            Sn = S * d + sa * b + vt[ip:ip + 1, :] * k
            S_ref[ip] = Sn
            rows.append(jnp.sum(Sn * r, axis=0, keepdims=True))
        y_ref[t] = jnp.concatenate(rows, axis=0)
        return nxt

    lax.fori_loop(0, Tc, step, tiles(0), unroll=8)


def _rwkv_rec_call(r, d, k, a, b, v, S0):
    T = r.shape[0]
    Tc = min(T, 32)
    vec = pl.BlockSpec((Tc, RW_DH, LANES), lambda c: (c, 0, 0))
    vsp = pl.BlockSpec((Tc, RW_IP, LANES), lambda c: (c, 0, 0))
    ssp = _full((RW_IP, RW_DH, LANES))
    return pl.pallas_call(
        functools.partial(_rwkv_rec_kernel, Tc=Tc),
        grid=(T // Tc,),
        in_specs=[vec] * 5 + [vsp, ssp],
        out_specs=[vsp, ssp],
        out_shape=[jax.ShapeDtypeStruct((T, RW_IP, LANES), F32),
                   jax.ShapeDtypeStruct((RW_IP, RW_DH, LANES), F32)],
        compiler_params=_params(("arbitrary",)),
        name="rwkv_recurrence",
    )(r, d, k, a, b, v, S0)


def _rwkv_dec_kernel(r_ref, d_ref, k_ref, a_ref, b_ref, v_ref, S0_ref, y_ref, S_ref):
    a, d, b, k, r = a_ref[...], d_ref[...], b_ref[...], k_ref[...], r_ref[...]
    v = v_ref[...]
    rows = []
    for i in range(RW_DH):
        S = S0_ref[i]
        sa = jnp.sum(S * a, axis=0, keepdims=True)
        Sn = S * d + sa * b + v[i:i + 1, :] * k
        S_ref[i] = Sn
        rows.append(jnp.sum(Sn * r, axis=0, keepdims=True))
    y_ref[...] = jnp.concatenate(rows, axis=0)


def _rwkv_decode_step(r, d, k, v, a, b, S0):
    B = r.shape[0]
    tr = lambda x: x.T.reshape(RW_HEADS, RW_DH, B)
    St = S0.reshape(B, RW_HEADS * RW_DH * RW_DH).T.reshape(RW_HEADS, RW_DH, RW_DH, B)
    vec = pl.BlockSpec((None, RW_DH, B), lambda h: (h, 0, 0))
    ssp = pl.BlockSpec((None, RW_DH, RW_DH, B), lambda h: (h, 0, 0, 0))
    y, S = pl.pallas_call(
        _rwkv_dec_kernel,
        grid=(RW_HEADS,),
        in_specs=[vec] * 6 + [ssp],
        out_specs=[vec, ssp],
        out_shape=[jax.ShapeDtypeStruct((RW_HEADS, RW_DH, B), F32),
                   jax.ShapeDtypeStruct((RW_HEADS, RW_DH, RW_DH, B), F32)],
        compiler_params=_params(("parallel",)),
        name="rwkv_decode_step",
    )(tr(r), tr(d), tr(k), tr(a), tr(b), tr(v), St)
    y = y.reshape(D_HALF, B).T
    S = S.reshape(RW_HEADS * RW_DH * RW_DH, B).T.reshape(B, RW_HEADS, RW_DH, RW_DH)
    return y, S


RW_TB = LANES // RW_BB
RW_NSB = 2
RW_MM_ROWS = 256


def _head_sum_rows(x):
    x3 = x.reshape(RW_HEADS, RW_DH, x.shape[-1])
    s = jnp.sum(x3, axis=1, keepdims=True)
    return jnp.broadcast_to(s, x3.shape).reshape(x.shape)


def _rwkv_pre_t_kernel(x_ref, shift0_ref, w_ref, mu_ref, w0_ref, a0_ref, w2_ref, a2_ref, g2_ref,
                       kkw_ref, kaw_ref, rk_ref,
                       r_ref, d_ref, k_ref, a_ref, b_ref, v_ref, g_ref, bonus_ref, last_ref, prev_scr):
    @pl.when(pl.program_id(0) == 0)
    def _():
        prev_scr[...] = shift0_ref[...]

    ro = lax.broadcasted_iota(jnp.int32, (LANES, LANES), 0)
    ci = lax.broadcasted_iota(jnp.int32, (LANES, LANES), 1)
    perm = jnp.where(ci == (ro % RW_BB) * RW_TB + ro // RW_BB, 1.0, 0.0).astype(BF16)
    lane = lax.broadcasted_iota(jnp.int32, (1, LANES), 1)
    grp = lane // RW_BB
    ngrp = LANES // RW_BB
    prs = {}

    def scatter(x, o_ref, t0, nrow, npiece):
        rot = [x[q * nrow:(q + 1) * nrow, :] if q == 0 else pltpu.roll(x[q * nrow:(q + 1) * nrow, :], q * RW_BB, 1)
               for q in range(npiece)]
        for t in range(RW_TB):
            m = rot[0]
            for q in range(1, npiece):
                m = jnp.where(grp == (t + q) % ngrp, rot[q], m)
            o_ref[t0 + t] = m

    def block(sb):
        t0 = sb * RW_TB
        xn = x_ref[:, t0:t0 + RW_TB, :].reshape(RW_BB * RW_TB, D_MODEL).astype(BF16)
        xg = jnp.dot(perm, xn, preferred_element_type=F32).astype(BF16)
        parts = []
        for r0 in range(0, RW_COLS_PAD, RW_MM_ROWS):
            parts.append(lax.dot_general(w_ref[r0:r0 + RW_MM_ROWS, :], xg, (((1,), (1,)), ((), ())),
                                         preferred_element_type=F32))
            yield
        pr = jnp.concatenate(parts, axis=0)
        prs[sb] = pr
        rolled = pltpu.roll(pr, RW_BB, 1)
        before = prev_scr[...] if sb == 0 else prs[sb - 1]
        prev = jnp.where(lane < RW_BB, before, rolled)
        prs[sb] = rolled
        if sb == RW_NSB - 1:
            prev_scr[...] = rolled
            last_ref[...] = pr
        pm = pr + (prev - pr) * mu_ref[...]
        r = pm[0:D_HALF]
        kr = pm[D_HALF:2 * D_HALF]
        vr = pm[2 * D_HALF:3 * D_HALF]
        lo = pm[3 * D_HALF:RW_COLS_PAD]
        yield
        w_log = -_softplus(-(w0_ref[...] + _dot(w2_ref[...], jnp.tanh(lo)))) - 0.5
        a = _sigmoid(a0_ref[...] + _dot(a2_ref[...], lo))
        g = _dot(g2_ref[...], _sigmoid(lo))
        kk = kr * kkw_ref[...]
        kk = kk / jnp.maximum(jnp.sqrt(_head_sum_rows(kk * kk)), 1e-12)
        kh = kr * (1.0 + (a - 1.0) * kaw_ref[...])
        g_ref[sb] = g
        bonus_ref[sb] = _head_sum_rows(r * kh * rk_ref[...]) * vr
        yield
        scatter(r, r_ref, t0, RW_DH, RW_HEADS)
        yield
        scatter(jnp.exp(-jnp.exp(w_log)), d_ref, t0, RW_DH, RW_HEADS)
        yield
        scatter(kh, k_ref, t0, RW_DH, RW_HEADS)
        yield
        scatter(-kk, a_ref, t0, RW_DH, RW_HEADS)
        yield
        scatter(kk * a, b_ref, t0, RW_DH, RW_HEADS)
        yield
        vv = jnp.concatenate([vr[h * RW_DH + half * RW_IP:h * RW_DH + (half + 1) * RW_IP, :]
                              for half in range(2) for h in range(RW_HEADS)], axis=0)
        scatter(vv, v_ref, t0, RW_IP, ngrp)

    _skewed([block(sb) for sb in range(RW_NSB)], lag=RW_COLS_PAD // RW_MM_ROWS)


def _lane_bcast(a, n):
    return jnp.broadcast_to(a.reshape(n, 1), (n, LANES))


def _rwkv_pre_t(x3, shift0, w_rwt, j, Wl):
    B, T, _ = x3.shape
    nblk = T // RW_TB
    padr = lambda w, o: jnp.pad(w, ((o, RW_LORA_PAD - o - w.shape[0]), (0, 0))).astype(BF16).T
    sh = jnp.pad(shift0.T, ((0, RW_COLS_PAD - RW_SHIFT_COLS), (0, LANES - RW_BB)))
    col = lambda a: _lane_bcast(a, D_HALF)
    ins = [x3, sh, w_rwt, _lane_bcast(_pad_lanes(Wl['rw_mu'].reshape(1, -1), RW_COLS_PAD), RW_COLS_PAD),
           col(Wl['rw_w0']), col(Wl['rw_a0']),
           padr(Wl['rw_w2'], 0), padr(Wl['rw_a2'], RW_DECAY_LORA), padr(Wl['rw_g2'], RW_DECAY_LORA + RW_A_LORA),
           col(Wl['rw_kk']), col(Wl['rw_ka']), col(Wl['rw_rk'])]
    tb = RW_TB * RW_NSB
    in_specs = [pl.BlockSpec((B, tb, D_MODEL), lambda c: (0, c, 0)), _full((RW_COLS_PAD, LANES)),
                pl.BlockSpec((None, RW_COLS_PAD, D_MODEL), lambda c: (j, 0, 0)), _full((RW_COLS_PAD, LANES)),
                _full((D_HALF, LANES)), _full((D_HALF, LANES)),
                _full((D_HALF, RW_LORA_PAD)), _full((D_HALF, RW_LORA_PAD)), _full((D_HALF, RW_LORA_PAD)),
                _full((D_HALF, LANES)), _full((D_HALF, LANES)), _full((D_HALF, LANES))]
    blk = pl.BlockSpec((RW_NSB, D_HALF, LANES), lambda c: (c, 0, 0))
    ktile = pl.BlockSpec((tb, RW_DH, LANES), lambda c: (c, 0, 0))
    vtile = pl.BlockSpec((tb, RW_IP, LANES), lambda c: (c, 0, 0))
    outs = pl.pallas_call(
        _rwkv_pre_t_kernel,
        grid=(T // tb,),
        in_specs=in_specs,
        out_specs=[ktile] * 5 + [vtile, blk, blk, _full((RW_COLS_PAD, LANES))],
        out_shape=[jax.ShapeDtypeStruct((T, RW_DH, LANES), F32)] * 5
                  + [jax.ShapeDtypeStruct((T, RW_IP, LANES), F32)]
                  + [jax.ShapeDtypeStruct((nblk, D_HALF, LANES), F32)] * 2
                  + [jax.ShapeDtypeStruct((RW_COLS_PAD, LANES), F32)],
        scratch_shapes=[pltpu.VMEM((RW_COLS_PAD, LANES), F32)],
        compiler_params=_params(("arbitrary",)),
        name="rwkv_pre_t",
    )(*ins)
    shift_new = outs[8][:RW_SHIFT_COLS, LANES - RW_BB:].T
    return outs[:8], shift_new


RW_NPB = 4


def _rwkv_post_t_kernel(y_ref, g_ref, bonus_ref, gng_ref, gnb_ref, yd_ref):
    _round_robin([_rwkv_post_t_block(y_ref, g_ref, bonus_ref, gng_ref, gnb_ref, yd_ref, pb)
                  for pb in range(RW_NPB)])


def _rwkv_post_t_block(y_ref, g_ref, bonus_ref, gng_ref, gnb_ref, yd_ref, pb):
    t0 = pb * RW_TB
    lane = lax.broadcasted_iota(jnp.int32, (1, LANES), 1)
    grp = lane // RW_BB
    ngrp = LANES // RW_BB
    ys = [y_ref[t0 + t] for t in range(RW_TB)]
    rolled = []
    for s in range(ngrp):
        m = ys[s % RW_TB]
        for q in range(1, ngrp):
            m = jnp.where(grp == q, ys[(q + s) % RW_TB], m)
        rolled.append(pltpu.roll(m, s * RW_BB, 1) if s else m)
    yield
    pieces = {}
    for q in range(ngrp):
        m = rolled[(-q) % ngrp]
        for t in range(1, RW_TB):
            m = jnp.where(grp == t, rolled[(t - q) % ngrp], m)
        pieces[divmod(q, RW_HEADS)] = m
    y = jnp.concatenate([pieces[(half, h)] for h in range(RW_HEADS) for half in range(2)], axis=0)
    yield
    mu = _head_sum_rows(y) * (1.0 / RW_DH)
    yc = y - mu
    var = _head_sum_rows(yc * yc) * (1.0 / RW_DH)
    hn = yc * lax.rsqrt(var + RW_GN_EPS) * gng_ref[...] + gnb_ref[...]
    yd = ((hn + bonus_ref[pb]) * g_ref[pb]).T
    yield
    for t in range(RW_TB):
        yd_ref[:, t0 + t, :] = yd[t * RW_BB:(t + 1) * RW_BB, :]


def _rwkv_rec_t(vecs, S0, Wl, T):
    r, d, k, a, b, v, g, bonus = vecs
    nblk = T // RW_TB
    B = RW_BB
    Sr = S0.reshape(B, RW_HEADS, 2, RW_IP, RW_DH).transpose(3, 4, 2, 1, 0).reshape(RW_IP, RW_DH, LANES)
    y, S = _rwkv_rec_call(r, d, k, a, b, v, Sr)
    tb = RW_TB * RW_NPB
    blk = pl.BlockSpec((RW_NPB, D_HALF, LANES), lambda c: (c, 0, 0))
    col = lambda a_: _lane_bcast(a_, D_HALF)
    yd = pl.pallas_call(
        _rwkv_post_t_kernel,
        grid=(nblk // RW_NPB,),
        in_specs=[pl.BlockSpec((tb, RW_IP, LANES), lambda c: (c, 0, 0)), blk, blk,
                  _full((D_HALF, LANES)), _full((D_HALF, LANES))],
        out_specs=pl.BlockSpec((B, tb, D_HALF), lambda c: (0, c, 0)),
        out_shape=jax.ShapeDtypeStruct((B, T, D_HALF), F32),
        compiler_params=_params(("parallel",)),
        name="rwkv_post_t",
    )(y, g, bonus, col(Wl['rw_gn_g']), col(Wl['rw_gn_b']))
    S = S.reshape(RW_IP, RW_DH, 2, RW_HEADS, B).transpose(4, 3, 2, 0, 1).reshape(B, RW_HEADS, RW_DH, RW_DH)
    return yd, S


def _rwkv_post_kernel(y_ref, g_ref, bonus_ref, gng_ref, gnb_ref, o_ref):
    ones_bd = _group_ones(LANES, RW_DH)
    y = y_ref[...]
    mu = _group_sum(y, ones_bd) * (1.0 / RW_DH)
    yc = y - mu
    var = _group_sum(yc * yc, ones_bd) * (1.0 / RW_DH)
    hn = yc * lax.rsqrt(var + RW_GN_EPS) * gng_ref[...] + gnb_ref[...]
    o_ref[...] = (hn + bonus_ref[...]) * g_ref[...]


def _rwkv_post(y2, g2, bonus2, Wl, tm):
    M = y2.shape[0]
    row = pl.BlockSpec((tm, D_HALF), lambda i: (i, 0))
    return pl.pallas_call(
        _rwkv_post_kernel,
        grid=(M // tm,),
        in_specs=[row, row, row, _full((1, D_HALF)), _full((1, D_HALF))],
        out_specs=row,
        out_shape=jax.ShapeDtypeStruct((M, D_HALF), F32),
        compiler_params=_params(("parallel",)),
        name="rwkv_post",
    )(y2, g2, bonus2, Wl['rw_gn_g'].reshape(1, D_HALF), Wl['rw_gn_b'].reshape(1, D_HALF))


def _trunk(x, st, pos, W):
    B, T, _ = x.shape
    M = B * T
    decode = T == 1
    tm_proj = min(M, 512)
    tm_post = min(M, 2 * POST_SUB)
    tm_rw = min(M, 512)
    x2 = x.reshape(M, D_MODEL)
    new = {name: [] for name in st}
    stacked = {}
    for l in range(DEPTH):
        j = l // 2
        Wl = {name: v[j] for name, v in W['per_pair'][l % 2].items()}
        stl = {name: v[j] for name, v in st.items()}
        if l % 2 == 0:
            if decode:
                p, = _proj(x2, W['ev_w_in'], j, tm_proj, (EVEN_IN,))
                ya, yb, (h, cb, S) = _even_decode(p, pos, stl, Wl, st['ret_S'], j, stacked.get('ret_S'))
                stacked['ret_S'] = S
            else:
                ya, yb, (h, cb, S) = _even_prefill(x2.reshape(B, T, D_MODEL), W['ev_w_in'], j, pos, stl, Wl)
                new['ret_S'].append(S)
            new['lru_h'].append(h)
            new['lru_conv'].append(cb)
            wout = W['ev_w_out']
        else:
            if decode:
                p_ml, p_rw = _proj(x2, W['od_w_in'], j, tm_proj, (2 * D_HALF, RW_COLS_PAD))
                ya, (C, n, m, cb) = _mlstm_decode(p_ml, stl, Wl, st['ml_C'], j, stacked.get('ml_C'))
                stacked['ml_C'] = C
                r, d, k, v, a, b, g, bonus = _rwkv_pre_decode(p_rw, stl['rw_shift'], Wl)
                y, S = _rwkv_decode_step(r, d, k, v, a, b, stl['rw_S'])
                yb = _rwkv_post(y, g, bonus, Wl, tm_rw)
                shift_new = p_rw[:, :RW_SHIFT_COLS]
            else:
                x3 = x2.reshape(B, T, D_MODEL)
                ya, (C, n, m, cb) = _mlstm_prefill(x3, W['od_w_in'], j, stl, Wl)
                vecs, shift_new = _rwkv_pre_t(x3, stl['rw_shift'], W['od_w_rwt'], j, Wl)
                yb, S = _rwkv_rec_t(vecs, stl['rw_S'], Wl, T)
                new['ml_C'].append(C)
            new['ml_n'].append(n)
            new['ml_m'].append(m)
            new['ml_conv'].append(cb)
            new['rw_S'].append(S)
            new['rw_shift'].append(shift_new)
            wout = W['od_w_out']
        x2 = _post(ya.reshape(M, D_HALF), yb.reshape(M, D_HALF), x2, wout, j, l,
                   W['ln1_g'], W['ln1_b'], W['mlp_w1'], W['mlp_w2'], W['ln2_g'], W['ln2_b'], tm_post)
    out = {name: stacked[name] if name in stacked else jnp.stack(v) for name, v in new.items()}
    return x2.reshape(B, T, D_MODEL), out


def _prepare_weights(w):
    even_names = ('lru_conv_w', 'lru_conv_b', 'lru_ba', 'lru_bx', 'lru_lambda', 'ret_gn_g', 'ret_gn_b')
    odd_names = ('ml_conv_w', 'ml_conv_b', 'ml_w_gate', 'ml_b_gate', 'ml_gn_g', 'ml_gn_b', 'ml_skip',
                 'rw_mu', 'rw_w0', 'rw_w2', 'rw_a0', 'rw_a2', 'rw_g2', 'rw_kk', 'rw_ka', 'rw_rk', 'rw_gn_g', 'rw_gn_b')
    even = {n: w[n] for n in even_names}
    even.update(lru_wa_d=jax.vmap(_lru_dense)(w['lru_wa']), lru_wx_d=jax.vmap(_lru_dense)(w['lru_wx']))
    odd = {n: w[n] for n in odd_names}
    odd.update(ml_wq_d=jax.vmap(_ml_dense)(w['ml_wq']), ml_wk_d=jax.vmap(_ml_dense)(w['ml_wk']),
               ml_wv_d=jax.vmap(_ml_dense)(w['ml_wv']))
    od_w_in = _pad_lanes(w['od_w_in'], ODD_IN_PAD).astype(BF16)
    return dict(per_pair=(even, odd),
                ev_w_in=w['ev_w_in'].astype(BF16), ev_w_out=w['ev_w_out'].astype(BF16),
                od_w_in=od_w_in, od_w_rwt=jnp.swapaxes(od_w_in[:, :, 2 * D_HALF:], 1, 2),
                od_w_out=w['od_w_out'].astype(BF16),
                mlp_w1=w['mlp_w1'].astype(BF16), mlp_w2=w['mlp_w2'].astype(BF16),
                ln1_g=w['ln1_g'], ln1_b=w['ln1_b'], ln2_g=w['ln2_g'], ln2_b=w['ln2_b'])


def _zero_states(batch):
    z = lambda *s: jnp.zeros(s, F32)
    n_even, n_odd = (DEPTH + 1) // 2, DEPTH // 2
    return dict(lru_h=z(n_even, batch, D_HALF), lru_conv=z(n_even, batch, CONV_W - 1, D_HALF),
                ret_S=z(n_even, batch, RET_HEADS, RET_DH, RET_DH),
                ml_C=z(n_odd, batch, ML_HEADS, ML_DH, ML_DH), ml_n=z(n_odd, batch, ML_HEADS, ML_DH),
                ml_m=z(n_odd, batch, ML_HEADS), ml_conv=z(n_odd, batch, CONV_W - 1, D_HALF),
                rw_S=z(n_odd, batch, RW_HEADS, RW_DH, RW_DH), rw_shift=z(n_odd, batch, RW_SHIFT_COLS))


def kernel(x_prompt, x_sample, state_lru_h, state_lru_conv, state_ret, state_mlstm_C, state_mlstm_n, state_mlstm_m, state_mlstm_conv, state_rwkv_S, state_rwkv_shift, ln1_g, ln1_b, ln2_g, ln2_b, mlp_w1, mlp_w2, ev_w_in, ev_w_out, lru_conv_w, lru_conv_b, lru_wa, lru_ba, lru_wx, lru_bx, lru_lambda, ret_gn_g, ret_gn_b, od_w_in, od_w_out, ml_conv_w, ml_conv_b, ml_wq, ml_wk, ml_wv, ml_w_gate, ml_b_gate, ml_gn_g, ml_gn_b, ml_skip, rw_mu, rw_w0, rw_w2, rw_a0, rw_a2, rw_g2, rw_kk, rw_ka, rw_rk, rw_gn_g, rw_gn_b):
    W = _prepare_weights(dict(
        ln1_g=ln1_g, ln1_b=ln1_b, ln2_g=ln2_g, ln2_b=ln2_b, mlp_w1=mlp_w1, mlp_w2=mlp_w2,
        ev_w_in=ev_w_in, ev_w_out=ev_w_out, lru_conv_w=lru_conv_w, lru_conv_b=lru_conv_b,
        lru_wa=lru_wa, lru_ba=lru_ba, lru_wx=lru_wx, lru_bx=lru_bx, lru_lambda=lru_lambda,
        ret_gn_g=ret_gn_g, ret_gn_b=ret_gn_b, od_w_in=od_w_in, od_w_out=od_w_out,
        ml_conv_w=ml_conv_w, ml_conv_b=ml_conv_b, ml_wq=ml_wq, ml_wk=ml_wk, ml_wv=ml_wv,
        ml_w_gate=ml_w_gate, ml_b_gate=ml_b_gate, ml_gn_g=ml_gn_g, ml_gn_b=ml_gn_b, ml_skip=ml_skip,
        rw_mu=rw_mu, rw_w0=rw_w0, rw_w2=rw_w2, rw_a0=rw_a0, rw_a2=rw_a2, rw_g2=rw_g2,
        rw_kk=rw_kk, rw_ka=rw_ka, rw_rk=rw_rk, rw_gn_g=rw_gn_g, rw_gn_b=rw_gn_b))
    st_sample = dict(lru_h=state_lru_h, lru_conv=state_lru_conv, ret_S=state_ret,
                     ml_C=state_mlstm_C, ml_n=state_mlstm_n, ml_m=state_mlstm_m, ml_conv=state_mlstm_conv,
                     rw_S=state_rwkv_S, rw_shift=state_rwkv_shift)
    pos_prompt = jnp.arange(x_prompt.shape[1], dtype=jnp.int32)
    pos_sample = PAST_LEN + jnp.arange(x_sample.shape[1], dtype=jnp.int32)
    y_prompt, sp = _trunk(x_prompt, _zero_states(x_prompt.shape[0]), pos_prompt, W)
    y_sample, ss = _trunk(x_sample, st_sample, pos_sample, W)
    names = ('lru_h', 'lru_conv', 'ret_S', 'ml_C', 'ml_n', 'ml_m', 'ml_conv', 'rw_S', 'rw_shift')
    return (y_prompt, y_sample) + tuple(sp[n] for n in names) + tuple(ss[n] for n in names)
```

```python
import functools

import jax
import jax.numpy as jnp
from jax import lax
from jax.experimental import pallas as pl
from jax.experimental.pallas import tpu as pltpu

F32 = jnp.float32
BF16 = jnp.bfloat16

D_MODEL = 1024
DEPTH = 4
PAST_LEN = 16384
D_HALF = D_MODEL // 2
CONV_W = 4
LRU_BLOCKS = 8
LRU_BLOCK = D_HALF // LRU_BLOCKS
LRU_C = 8.0
RET_HEADS = 4
RET_DH = D_HALF // RET_HEADS
CHUNK = 128
ROPE_BASE = 10000.0
ML_HEADS = 4
ML_DH = D_HALF // ML_HEADS
ML_QKV_BLOCK = 4
ML_NBLK = D_HALF // ML_QKV_BLOCK
RW_HEADS = 8
RW_DH = D_HALF // RW_HEADS
RW_DECAY_LORA = 32
RW_A_LORA = 32
RW_GATE_LORA = 96
RW_LORA = RW_DECAY_LORA + RW_A_LORA + RW_GATE_LORA
RW_SHIFT_COLS = 3 * D_HALF + RW_LORA
D_FF = 4 * D_MODEL
ALPHA = (2.0 * DEPTH) ** 0.25
EVEN_IN = 6 * D_HALF
ODD_IN = 2 * D_HALF + RW_SHIFT_COLS
LN_EPS = 1e-5
GN_EPS = 1e-5
RW_GN_EPS = 64e-5

LANES = 128
SUBLANES = 8
RW_LORA_PAD = 2 * LANES
RW_COLS_PAD = 3 * D_HALF + RW_LORA_PAD
ODD_IN_PAD = 2 * D_HALF + RW_COLS_PAD
RW_BB = 8
VMEM_LIMIT = 56 * 1024 * 1024


def _params(sem):
    return pltpu.CompilerParams(dimension_semantics=sem, vmem_limit_bytes=VMEM_LIMIT)


def _dot(a, b):
    return jnp.dot(a.astype(BF16), b.astype(BF16), preferred_element_type=F32)


def _dot_nt(a, b):
    return lax.dot_general(a.astype(BF16), b.astype(BF16), (((1,), (1,)), ((), ())),
                           preferred_element_type=F32)


def _dot_tn(a, b):
    return lax.dot_general(a.astype(BF16), b.astype(BF16), (((0,), (0,)), ((), ())),
                           preferred_element_type=F32)


def _split3(a):
    hi = a.astype(BF16)
    r1 = a - hi.astype(F32)
    mid = r1.astype(BF16)
    lo = (r1 - mid.astype(F32)).astype(BF16)
    return hi, mid, lo


def _xdot(a, b01):
    hi, mid, lo = _split3(a)
    f = lambda t: jnp.dot(t, b01, preferred_element_type=F32)
    return f(hi) + f(mid) + f(lo)


def _xdot_l(b01, a):
    hi, mid, lo = _split3(a)
    f = lambda t: jnp.dot(b01, t, preferred_element_type=F32)
    return f(hi) + f(mid) + f(lo)


def _sigmoid(x):
    return 1.0 / (1.0 + jnp.exp(-x))


def _silu(x):
    return x * _sigmoid(x)


def _softplus(x):
    return jnp.maximum(x, 0.0) + jnp.log1p(jnp.exp(-jnp.abs(x)))


def _gelu_tanh(x):
    return 0.5 * x * (1.0 + jnp.tanh(0.7978845608028654 * (x + 0.044715 * (x * x * x))))


def _layer_norm(x, g, b, eps):
    mu = jnp.mean(x, -1, keepdims=True)
    xc = x - mu
    var = jnp.mean(xc * xc, -1, keepdims=True)
    return xc * lax.rsqrt(var + eps) * g + b


def _group_ones(n, group):
    r = lax.broadcasted_iota(jnp.int32, (n, n), 0) // group
    c = lax.broadcasted_iota(jnp.int32, (n, n), 1) // group
    return jnp.where(r == c, 1.0, 0.0).astype(BF16)


def _group_sum(x, ones_bd):
    parts = [_xdot(x[:, s * LANES:(s + 1) * LANES], ones_bd) for s in range(x.shape[1] // LANES)]
    return jnp.concatenate(parts, axis=-1)


def _rotate(x, cosf, sinf):
    return x * cosf + pltpu.roll(x, RET_DH // 2, 1) * sinf


def _proj_kernel(x_ref, w_ref, *o_refs, splits):
    xb = x_ref[...].astype(BF16)
    off = 0
    for o_ref, n in zip(o_refs, splits):
        o_ref[...] = jnp.dot(xb, w_ref[:, off:off + n], preferred_element_type=F32)
        off += n


def _proj(x2d, w_all, j, tm, splits):
    M = x2d.shape[0]
    N = w_all.shape[2]
    return pl.pallas_call(
        functools.partial(_proj_kernel, splits=splits),
        grid=(M // tm,),
        in_specs=[pl.BlockSpec((tm, D_MODEL), lambda i: (i, 0)),
                  pl.BlockSpec((None, D_MODEL, N), lambda i: (j, 0, 0))],
        out_specs=[pl.BlockSpec((tm, n), lambda i: (i, 0)) for n in splits],
        out_shape=[jax.ShapeDtypeStruct((M, n), F32) for n in splits],
        compiler_params=_params(("parallel",)),
        name="proj_in",
    )(x2d, w_all)


FF_CHUNK = 1024


POST_SUB = 256


def _post_kernel(ya_ref, yb_ref, x_ref, wo_ref, g1_ref, b1_ref, w1_ref, w2_ref, g2_ref, b2_ref, o_ref):
    tm = x_ref.shape[0]
    sub = min(tm, POST_SUB)

    def tile(r0):
        rows = slice(r0, r0 + sub)
        y = (jnp.dot(ya_ref[rows, :].astype(BF16), wo_ref[0:D_HALF, :], preferred_element_type=F32)
             + jnp.dot(yb_ref[rows, :].astype(BF16), wo_ref[D_HALF:D_MODEL, :], preferred_element_type=F32))
        yield
        x1 = _layer_norm(ALPHA * x_ref[rows, :] + y, g1_ref[...], b1_ref[...], LN_EPS)
        x1b = x1.astype(BF16)
        acc = jnp.zeros(x1.shape, F32)
        for c in range(D_FF // FF_CHUNK):
            yield
            h = jnp.dot(x1b, w1_ref[:, c * FF_CHUNK:(c + 1) * FF_CHUNK], preferred_element_type=F32)
            yield
            h = jnp.square(jnp.maximum(h, 0.0))
            acc = acc + jnp.dot(h.astype(BF16), w2_ref[c * FF_CHUNK:(c + 1) * FF_CHUNK, :],
                                preferred_element_type=F32)
        yield
        o_ref[rows, :] = _layer_norm(ALPHA * x1 + acc, g2_ref[...], b2_ref[...], LN_EPS)

    _skewed([tile(r0) for r0 in range(0, tm, sub)])


def _post(ya, yb, x2d, wout_all, j, l, ln1_g, ln1_b, w1_all, w2_all, ln2_g, ln2_b, tm):
    M = x2d.shape[0]
    row = lambda i: (i, 0)
    vec = pl.BlockSpec((None, 1, D_MODEL), lambda i: (l, 0, 0))
    r3 = lambda a: a.reshape(DEPTH, 1, D_MODEL)
    return pl.pallas_call(
        _post_kernel,
        grid=(M // tm,),
        in_specs=[pl.BlockSpec((tm, D_HALF), row), pl.BlockSpec((tm, D_HALF), row),
                  pl.BlockSpec((tm, D_MODEL), row),
                  pl.BlockSpec((None, D_MODEL, D_MODEL), lambda i: (j, 0, 0)),
                  vec, vec,
                  pl.BlockSpec((None, D_MODEL, D_FF), lambda i: (l, 0, 0)),
                  pl.BlockSpec((None, D_FF, D_MODEL), lambda i: (l, 0, 0)),
                  vec, vec],
        out_specs=pl.BlockSpec((tm, D_MODEL), row),
        out_shape=jax.ShapeDtypeStruct((M, D_MODEL), F32),
        compiler_params=_params(("parallel",)),
        name="post_mlp",
    )(ya, yb, x2d, wout_all, r3(ln1_g), r3(ln1_b), w1_all, w2_all, r3(ln2_g), r3(ln2_b))


def _conv_prefill(x, xbuf, cw_ref, cb_ref, L):
    xbuf[SUBLANES:SUBLANES + L, :] = x
    y = cb_ref[...]
    for i in range(CONV_W):
        y = y + cw_ref[i:i + 1, :] * xbuf[SUBLANES - (CONV_W - 1) + i:SUBLANES - (CONV_W - 1) + i + L, :]
    tail = xbuf[L + SUBLANES - (CONV_W - 1):L + SUBLANES, :]
    xbuf[SUBLANES - (CONV_W - 1):SUBLANES, :] = tail
    return y, tail


def _lru_gates(xc, wa_ref, ba_ref, wx_ref, bx_ref, lam_ref):
    xcb = xc.astype(BF16)
    nslab = D_HALF // LANES
    ra = jnp.concatenate([jnp.dot(xcb[:, s * LANES:(s + 1) * LANES], wa_ref[s], preferred_element_type=F32)
                          for s in range(nslab)], axis=-1)
    rx = jnp.concatenate([jnp.dot(xcb[:, s * LANES:(s + 1) * LANES], wx_ref[s], preferred_element_type=F32)
                          for s in range(nslab)], axis=-1)
    r = _sigmoid(ra + ba_ref[...])
    i = _sigmoid(rx + bx_ref[...])
    log_a = -LRU_C * r * _softplus(-lam_ref[...])
    a = jnp.exp(log_a)
    t = jnp.tanh(log_a)
    u = jnp.sqrt(-2.0 * t / (1.0 - t)) * (i * xc)
    return a, u


def _head_norm(o, g, b, eps):
    mu = jnp.mean(o, -1, keepdims=True)
    oc = o - mu
    var = jnp.mean(oc * oc, -1, keepdims=True)
    return oc * lax.rsqrt(var + eps) * g + b


EV_NB = 4


class _Cols:
    def __init__(self, ref, off):
        self.ref, self.off = ref, off

    def __getitem__(self, idx):
        if idx is Ellipsis:
            return self.ref[:, self.off:self.off + D_HALF]
        rows, cols = idx
        return self.ref[rows, self.off + cols.start:self.off + cols.stop]


def _even_prefill_kernel(x_ref, w_ref,
                         cw_ref, cb_ref, wa_ref, ba_ref, wx_ref, bx_ref, lam_ref,
                         cos_ref, sin_ref, dmask_ref, qd_ref, kd_ref, cd_ref, gng_ref, gnb_ref,
                         h0_ref, conv0_ref, S0_ref,
                         ya_ref, yb_ref, h_ref, conv_ref, S_ref, xbuf, pbuf, *, L):
    @pl.when(pl.program_id(1) == 0)
    def _():
        h_ref[...] = h0_ref[...]
        S_ref[...] = S0_ref[...]
        xbuf[:, SUBLANES - (CONV_W - 1):SUBLANES, :] = conv0_ref[...]

    rows = [_even_prefill_one(x_ref.at[bi], w_ref, pbuf.at[bi],
                              cw_ref, cb_ref, wa_ref, ba_ref, wx_ref, bx_ref, lam_ref,
                              cos_ref, sin_ref, dmask_ref, qd_ref, kd_ref, cd_ref, gng_ref, gnb_ref,
                              ya_ref.at[bi], yb_ref.at[bi], h_ref.at[bi], conv_ref.at[bi], S_ref.at[bi],
                              xbuf.at[bi], L=L) for bi in range(EV_NB)]
    _round_robin(rows)


def _even_prefill_one(x_ref, w_ref, p_ref,
                      cw_ref, cb_ref, wa_ref, ba_ref, wx_ref, bx_ref, lam_ref,
                      cos_ref, sin_ref, dmask_ref, qd_ref, kd_ref, cd_ref, gng_ref, gnb_ref,
                      ya_ref, yb_ref, h_ref, conv_ref, S_ref, xbuf, *, L):
    p_ref[...] = jnp.dot(x_ref[...].astype(BF16), w_ref[...], preferred_element_type=F32)
    xa_ref, ga_ref, q_ref, k_ref, v_ref, gb_ref = (_Cols(p_ref, i * D_HALF) for i in range(6))
    yield
    H = range(RET_HEADS)
    sls = [slice(hh * RET_DH, (hh + 1) * RET_DH) for hh in H]
    cosf = cos_ref[...]
    sinf = sin_ref[...]
    qh = [_rotate(q_ref[:, sl], cosf, sinf) for sl in sls]
    kh = [_rotate(k_ref[:, sl], cosf, sinf) * (RET_DH ** -0.5) for sl in sls]
    vh = [v_ref[:, sl].astype(BF16) for sl in sls]
    qk = [_dot_nt(qh[hh], kh[hh]) for hh in H]
    S = [S_ref[hh] for hh in H]
    qS = [_dot(qh[hh] * qd_ref[:, sls[hh]], S[hh]) for hh in H]
    kv = [_dot_tn(kh[hh] * kd_ref[:, sls[hh]], vh[hh]) for hh in H]
    yield
    xc, tail = _conv_prefill(xa_ref[...], xbuf, cw_ref, cb_ref, L)
    conv_ref[...] = tail
    a, u = _lru_gates(xc, wa_ref, ba_ref, wx_ref, bx_ref, lam_ref)
    yield
    for hh in H:
        S_ref[hh] = S[hh] * cd_ref[hh] + kv[hh]
    sc = [(qk[hh] * dmask_ref[hh]).astype(BF16) for hh in H]
    o = [jnp.dot(sc[hh], vh[hh], preferred_element_type=F32) + qS[hh] for hh in H]
    yield
    row = lax.broadcasted_iota(jnp.int32, (L, D_HALF), 0) % SUBLANES
    s = 1
    while s < SUBLANES:
        keep = row >= s
        a_sh = jnp.where(keep, pltpu.roll(a, s, 0), 1.0)
        u_sh = jnp.where(keep, pltpu.roll(u, s, 0), 0.0)
        u = a * u_sh + u
        a = a * a_sh
        s *= 2
        yield
    carry = h_ref[...]
    groups = []
    for g in range(L // SUBLANES):
        rows8 = slice(g * SUBLANES, (g + 1) * SUBLANES)
        hg = a[rows8, :] * carry + u[rows8, :]
        carry = hg[SUBLANES - 1:SUBLANES, :]
        groups.append(hg)
        if g % 4 == 3:
            yield
    h = jnp.concatenate(groups, axis=0)
    h_ref[...] = carry
    ya_ref[...] = _gelu_tanh(ga_ref[...]) * h
    ones = jnp.ones((RET_DH, LANES), BF16)
    mu = [jnp.dot(o[hh].astype(BF16), ones, preferred_element_type=F32) * (1.0 / RET_DH) for hh in H]
    yield
    oc = [o[hh] - mu[hh] for hh in H]
    var = [jnp.dot((oc[hh] * oc[hh]).astype(BF16), ones, preferred_element_type=F32) * (1.0 / RET_DH) for hh in H]
    yield
    for hh in H:
        sl = sls[hh]
        on = oc[hh] * lax.rsqrt(var[hh] + GN_EPS) * gng_ref[:, sl] + gnb_ref[:, sl]
        yb_ref[:, sl] = _silu(gb_ref[:, sl]) * on


def _ret_tables(L):
    log_gamma = jnp.log1p(-jnp.exp2(-5.0 - jnp.arange(RET_HEADS, dtype=F32)))
    idx = jnp.arange(L, dtype=F32)
    diff = idx[:, None] - idx[None, :]
    dmask = jnp.where(diff >= 0, jnp.exp(log_gamma[:, None, None] * jnp.maximum(diff, 0.0)), 0.0)
    qd = jnp.exp(log_gamma[:, None] * (idx + 1.0))
    kd = jnp.exp(log_gamma[:, None] * (L - 1.0 - idx))
    cd = jnp.exp(log_gamma * L)
    qd_full = jnp.repeat(qd.T, RET_DH, axis=1)
    kd_full = jnp.repeat(kd.T, RET_DH, axis=1)
    cd_full = jnp.broadcast_to(cd[:, None, None], (RET_HEADS, 1, RET_DH))
    return dmask, qd_full, kd_full, cd_full


def _rope_tables(pos):
    half = RET_DH // 2
    inv = ROPE_BASE ** (-jnp.arange(half, dtype=F32) / half)
    ang = pos.astype(F32)[:, None] * inv[None, :]
    cos, sin = jnp.cos(ang), jnp.sin(ang)
    return jnp.concatenate([cos, cos], -1), jnp.concatenate([-sin, sin], -1)


def _lru_dense(w):
    pairs = LANES // LRU_BLOCK
    w4 = w.reshape(LRU_BLOCKS // pairs, pairs, LRU_BLOCK, LRU_BLOCK)
    eye = jnp.eye(pairs, dtype=w.dtype)
    d = w4[:, :, :, None, :] * eye[None, :, None, :, None]
    return d.reshape(LRU_BLOCKS // pairs, LANES, LANES).astype(BF16)


def _full(shape):
    n = len(shape)
    return pl.BlockSpec(shape, lambda *_: (0,) * n)


def _even_prefill(x3, w_in_all, j, pos, st, Wl):
    B, T, _ = x3.shape
    L = CHUNK
    nc = T // L
    dmask, qd, kd, cd = _ret_tables(L)
    cosf, sinf = _rope_tables(pos)
    nb = EV_NB
    perb = lambda *s: pl.BlockSpec((nb,) + s, lambda b, c: (b,) + (0,) * len(s))
    row2 = lambda a: a.reshape(1, D_HALF)
    ins = [x3, w_in_all, Wl['lru_conv_w'], row2(Wl['lru_conv_b']), Wl['lru_wa_d'], row2(Wl['lru_ba']),
           Wl['lru_wx_d'], row2(Wl['lru_bx']), row2(Wl['lru_lambda']),
           cosf, sinf, dmask, qd, kd, cd, row2(Wl['ret_gn_g']), row2(Wl['ret_gn_b']),
           st['lru_h'].reshape(B, 1, D_HALF), st['lru_conv'], st['ret_S']]
    in_specs = [pl.BlockSpec((nb, L, D_MODEL), lambda b, c: (b, c, 0)),
                pl.BlockSpec((None, D_MODEL, EVEN_IN), lambda b, c: (j, 0, 0))] + [
        _full((CONV_W, D_HALF)), _full((1, D_HALF)), _full((4, LANES, LANES)), _full((1, D_HALF)),
        _full((4, LANES, LANES)), _full((1, D_HALF)), _full((1, D_HALF)),
        pl.BlockSpec((L, RET_DH), lambda b, c: (c, 0)), pl.BlockSpec((L, RET_DH), lambda b, c: (c, 0)),
        _full((RET_HEADS, L, L)), _full((L, D_HALF)), _full((L, D_HALF)), _full((RET_HEADS, 1, RET_DH)),
        _full((1, D_HALF)), _full((1, D_HALF)),
        perb(1, D_HALF), perb(CONV_W - 1, D_HALF), perb(RET_HEADS, RET_DH, RET_DH)]
    seq = pl.BlockSpec((nb, L, D_HALF), lambda b, c: (b, c, 0))
    ya, yb, h, conv, S = pl.pallas_call(
        functools.partial(_even_prefill_kernel, L=L),
        grid=(B // nb, nc),
        in_specs=in_specs,
        out_specs=[seq, seq, perb(1, D_HALF), perb(CONV_W - 1, D_HALF), perb(RET_HEADS, RET_DH, RET_DH)],
        out_shape=[jax.ShapeDtypeStruct((B, T, D_HALF), F32), jax.ShapeDtypeStruct((B, T, D_HALF), F32),
                   jax.ShapeDtypeStruct((B, 1, D_HALF), F32),
                   jax.ShapeDtypeStruct((B, CONV_W - 1, D_HALF), F32),
                   jax.ShapeDtypeStruct((B, RET_HEADS, RET_DH, RET_DH), F32)],
        scratch_shapes=[pltpu.VMEM((nb, L + SUBLANES, D_HALF), F32), pltpu.VMEM((nb, L, EVEN_IN), F32)],
        compiler_params=_params(("parallel", "arbitrary")),
        name="even_prefill",
    )(*ins)
    return ya, yb, (h.reshape(B, D_HALF), conv, S)


DEC_BB = 8


def _even_decode_kernel(p_ref, cw_ref, cb_ref, wa_ref, ba_ref, wx_ref, bx_ref, lam_ref,
                        cos_ref, sin_ref, dm_ref, qd_ref, kd_ref, cd_ref, gng_ref, gnb_ref,
                        h0_ref, conv0_ref, S0_ref,
                        ya_ref, yb_ref, h_ref, conv_ref, S_ref):
    col = lambda i: p_ref[:, i * D_HALF:(i + 1) * D_HALF]
    xa = col(0)
    xc = cb_ref[...] + cw_ref[CONV_W - 1:CONV_W, :] * xa
    for i in range(CONV_W - 1):
        xc = xc + cw_ref[i:i + 1, :] * conv0_ref[i]
    for i in range(CONV_W - 2):
        conv_ref[i] = conv0_ref[i + 1]
    conv_ref[CONV_W - 2] = xa
    a, u = _lru_gates(xc, wa_ref, ba_ref, wx_ref, bx_ref, lam_ref)
    h = a * h0_ref[...] + u
    h_ref[...] = h
    ya_ref[...] = _gelu_tanh(col(1)) * h

    cosf = cos_ref[...]
    sinf = sin_ref[...]
    row8 = lax.broadcasted_iota(jnp.int32, (SUBLANES, RET_DH), 0)
    q, k, v, gb = col(2), col(3), col(4), col(5)
    for hh in range(RET_HEADS):
        sl = slice(hh * RET_DH, (hh + 1) * RET_DH)
        qh = _rotate(q[:, sl], cosf, sinf)
        kh = _rotate(k[:, sl], cosf, sinf) * (RET_DH ** -0.5)
        vh = v[:, sl]
        qk = jnp.sum(qh * kh, -1, keepdims=True) * dm_ref[:, sl]
        qq = qh * qd_ref[:, sl]
        kk = kh * kd_ref[:, sl]
        rows = []
        for bi in range(DEC_BB):
            S = S0_ref[bi, hh]
            q8 = jnp.broadcast_to(qq[bi:bi + 1, :], (SUBLANES, RET_DH))
            rows.append(_dot_f32(q8, S)[0:1, :])
            k8 = jnp.where(row8 == 0, jnp.broadcast_to(kk[bi:bi + 1, :], (SUBLANES, RET_DH)), 0.0)
            v8 = jnp.broadcast_to(vh[bi:bi + 1, :], (SUBLANES, RET_DH))
            S_ref[bi, hh] = S * cd_ref[hh] + _dot_tn_f32(k8, v8)
        o = qk * vh + jnp.concatenate(rows, axis=0)
        on = _head_norm(o, gng_ref[:, sl], gnb_ref[:, sl], GN_EPS)
        yb_ref[:, sl] = _silu(gb[:, sl]) * on


def _dot_f32(a, b):
    return jnp.dot(a, b, preferred_element_type=F32)


def _dot_tn_f32(a, b):
    return lax.dot_general(a, b, (((0,), (0,)), ((), ())), preferred_element_type=F32)


def _skip_ref(kernel, pos):
    def wrapped(*refs):
        kernel(*refs[:pos], *refs[pos + 1:])
    return wrapped


def _all_layers(kernel, pos_in, pos_out, j, n_layers):
    def wrapped(*refs):
        refs = list(refs)
        s_in, s_out = refs[pos_in], refs[pos_out]
        for other in range(n_layers):
            if other != j:
                s_out[other] = s_in[other]
        refs[pos_in], refs[pos_out] = s_in.at[j], s_out.at[j]
        kernel(*refs)
    return wrapped


def _stacked_state_io(kernel, S_all, S_prev, j, bb, pos_in, n_in, out_idx):
    n_layers, tail = S_all.shape[0], S_all.shape[2:]
    zeros = (0,) * len(tail)
    shape = jax.ShapeDtypeStruct(S_all.shape, F32)
    if S_prev is None:
        spec = pl.BlockSpec((n_layers, bb) + tail, lambda i: (0, i) + zeros)
        return _all_layers(kernel, pos_in, n_in + out_idx, j, n_layers), spec, [], [], {}, shape
    spec = pl.BlockSpec((None, bb) + tail, lambda i: (j, i) + zeros)
    return _skip_ref(kernel, n_in), spec, [S_prev], [pl.BlockSpec(memory_space=pl.ANY)], {n_in: out_idx}, shape


def _even_decode(p2, pos, st, Wl, S_all, j, S_prev):
    B = p2.shape[0]
    bb = DEC_BB
    dmask, qd, kd, cd = _ret_tables(1)
    dm = jnp.repeat(dmask[:, 0, :].T, RET_DH, axis=1)
    cosf, sinf = _rope_tables(pos)
    row2 = lambda a: a.reshape(1, D_HALF)
    rows = lambda n: pl.BlockSpec((bb, n), lambda i: (i, 0))
    convs = pl.BlockSpec((CONV_W - 1, bb, D_HALF), lambda i: (0, i, 0))
    ins = [p2, Wl['lru_conv_w'], row2(Wl['lru_conv_b']), Wl['lru_wa_d'], row2(Wl['lru_ba']),
           Wl['lru_wx_d'], row2(Wl['lru_bx']), row2(Wl['lru_lambda']),
           cosf, sinf, dm, qd, kd, cd, row2(Wl['ret_gn_g']), row2(Wl['ret_gn_b']),
           st['lru_h'], jnp.swapaxes(st['lru_conv'], 0, 1), S_all]
    kern, Ss, extra_in, extra_specs, aliases, S_shape = _stacked_state_io(
        _even_decode_kernel, S_all, S_prev, j, bb, len(ins) - 1, len(ins), 4)
    in_specs = [rows(EVEN_IN), _full((CONV_W, D_HALF)), _full((1, D_HALF)), _full((4, LANES, LANES)),
                _full((1, D_HALF)), _full((4, LANES, LANES)), _full((1, D_HALF)), _full((1, D_HALF)),
                _full((1, RET_DH)), _full((1, RET_DH)), _full((1, D_HALF)), _full((1, D_HALF)),
                _full((1, D_HALF)), _full((RET_HEADS, 1, RET_DH)), _full((1, D_HALF)), _full((1, D_HALF)),
                rows(D_HALF), convs, Ss] + extra_specs
    ya, yb, h, conv, S = pl.pallas_call(
        kern,
        grid=(B // bb,),
        in_specs=in_specs,
        out_specs=[rows(D_HALF), rows(D_HALF), rows(D_HALF), convs, Ss],
        out_shape=[jax.ShapeDtypeStruct((B, D_HALF), F32), jax.ShapeDtypeStruct((B, D_HALF), F32),
                   jax.ShapeDtypeStruct((B, D_HALF), F32),
                   jax.ShapeDtypeStruct((CONV_W - 1, B, D_HALF), F32), S_shape],
        input_output_aliases=aliases,
        compiler_params=_params(("parallel",)),
        name="even_decode",
    )(*ins, *extra_in)
    return ya, yb, (h, jnp.swapaxes(conv, 0, 1), S)


def _mlstm_qkv_gates(xm, xc, wq_ref, wk_ref, wv_ref, wg_ref, bg_ref):
    q = _dot(xc, wq_ref[...])
    k = _dot(xc, wk_ref[...])
    v = _dot(xm, wv_ref[...])
    g_col = (_dot(q, wg_ref[0:D_HALF, :]) + _dot(k, wg_ref[D_HALF:2 * D_HALF, :])
             + _dot(v, wg_ref[2 * D_HALF:3 * D_HALF, :]) + bg_ref[...])
    return q, k, v, g_col


ML_NB = 4


def _skewed(gens, lag=1):
    live = []
    pending = list(gens)
    tick = 0
    while pending or live:
        if pending and tick % lag == 0:
            live.append(pending.pop(0))
        tick += 1
        nxt = []
        for g in live:
            try:
                next(g)
                nxt.append(g)
            except StopIteration:
                pass
        live = nxt


def _round_robin(gens):
    gens = list(gens)
    while gens:
        alive = []
        for g in gens:
            try:
                next(g)
                alive.append(g)
            except StopIteration:
                pass
        gens = alive


def _mlstm_prefill_kernel(x_ref, w_ref, cw_ref, cb_ref, wq_ref, wk_ref, wv_ref, wg_ref, bg_ref,
                          wgt_ref, bgt_ref, gng_ref, gnb_ref, skip_ref,
                          conv0_ref, C0_ref, n0_ref, m0_ref,
                          yc_ref, conv_ref, C_ref, n_ref, m_ref, xbuf, ncol, pbuf, *, L):
    @pl.when(pl.program_id(1) == 0)
    def _():
        ones = jnp.ones((ML_DH, LANES), BF16)
        eye = jnp.where(lax.broadcasted_iota(jnp.int32, (ML_DH, LANES), 0)
                        == lax.broadcasted_iota(jnp.int32, (ML_DH, LANES), 1), 1.0, 0.0)
        C_ref[...] = C0_ref[...]
        m_ref[...] = m0_ref[...]
        for bi in range(ML_NB):
            xbuf[bi, SUBLANES - (CONV_W - 1):SUBLANES, :] = conv0_ref[bi]
            for hh in range(ML_HEADS):
                ncol[bi, hh] = _xdot(eye * n0_ref[bi, :, hh * ML_DH:(hh + 1) * ML_DH], ones)

    rows = [_mlstm_prefill_one(x_ref.at[bi], w_ref, pbuf.at[bi], cw_ref, cb_ref, wq_ref, wk_ref, wv_ref, wg_ref,
                               bg_ref, wgt_ref, bgt_ref, gng_ref, gnb_ref, skip_ref,
                               yc_ref.at[bi], conv_ref.at[bi], C_ref.at[bi], n_ref.at[bi], m_ref.at[bi],
                               xbuf.at[bi], ncol.at[bi], L=L) for bi in range(ML_NB)]
    _round_robin(rows)


def _mlstm_prefill_one(x_ref, w_ref, p_ref, cw_ref, cb_ref, wq_ref, wk_ref, wv_ref, wg_ref, bg_ref,
                       wgt_ref, bgt_ref, gng_ref, gnb_ref, skip_ref,
                       yc_ref, conv_ref, C_ref, n_ref, m_ref, xbuf, ncol, *, L):
    p_ref[...] = jnp.dot(x_ref[...].astype(BF16), w_ref[...], preferred_element_type=F32)
    xm_ref, z_ref = _Cols(p_ref, 0), _Cols(p_ref, D_HALF)
    yield
    xm = xm_ref[...]
    xc, tail = _conv_prefill(xm, xbuf, cw_ref, cb_ref, L)
    conv_ref[...] = tail
    xc = _silu(xc)
    yield
    q, k, v, g_col = _mlstm_qkv_gates(xm, xc, wq_ref, wk_ref, wv_ref, wg_ref, bg_ref)
    g_row = (_dot_nt(wgt_ref[:, 0:D_HALF], q) + _dot_nt(wgt_ref[:, D_HALF:2 * D_HALF], k)
             + _dot_nt(wgt_ref[:, 2 * D_HALF:3 * D_HALF], v) + bgt_ref[...])
    yield
    ri = lax.broadcasted_iota(jnp.int32, (L, L), 0)
    ci = lax.broadcasted_iota(jnp.int32, (L, L), 1)
    causal = ri >= ci
    tril = jnp.where(causal, 1.0, 0.0).astype(BF16)
    triu = jnp.where(ci >= ri, 1.0, 0.0).astype(BF16)
    ones = jnp.ones((L, LANES), BF16)
    eye = jnp.where(ri == ci, 1.0, 0.0)

    li_col = g_col
    lf_col = -_softplus(-pltpu.roll(g_col, LANES - ML_HEADS, 1))
    b_col = _xdot_l(tril, lf_col)
    lf_row = -_softplus(-g_row)
    b_row = _xdot(lf_row, triu)
    yield
    c_row = g_row[0:ML_HEADS, :] - b_row[ML_HEADS:2 * ML_HEADS, :]
    row = lax.broadcasted_iota(jnp.int32, (L, LANES), 0)
    pm = li_col - b_col
    sft = 1
    while sft < L:
        pm = jnp.maximum(pm, jnp.where(row >= sft, pltpu.roll(pm, sft, 0), -jnp.inf))
        sft *= 2
    m_prev = m_ref[...]
    u_col = -jnp.maximum(pm, m_prev)
    m_t_col = b_col - u_col
    e_col = jnp.exp(-m_t_col)
    m_new = m_t_col[L - 1:L, :]
    b_last = b_col[L - 1:L, :]
    wk_col = jnp.exp(b_last - b_col + li_col - m_new)
    wC_row = jnp.exp(b_last + m_prev - m_new)
    m_ref[...] = m_new
    yield
    vones = ones
    rep = lambda col, hh: jnp.broadcast_to(col[:, hh:hh + 1], (L, LANES))
    H = range(ML_HEADS)
    sls = [slice(hh * ML_DH, (hh + 1) * ML_DH) for hh in H]
    qh = [q[:, sl].astype(BF16) for sl in sls]
    kh = [k[:, sl] * (ML_DH ** -0.5) for sl in sls]
    vh1 = [jnp.concatenate([v[:, sl].astype(BF16), vones], axis=-1) for sl in sls]
    qk = [_dot_nt(qh[hh], kh[hh]) for hh in H]
    CN = [jnp.concatenate([C_ref[hh], ncol[hh]], axis=-1) for hh in H]
    qc = [jnp.dot(qh[hh], CN[hh].astype(BF16), preferred_element_type=F32) for hh in H]
    kw = [kh[hh] * rep(wk_col, hh) for hh in H]
    upd = [lax.dot_general(kw[hh].astype(BF16), vh1[hh], (((0,), (0,)), ((), ())), preferred_element_type=F32)
           for hh in H]
    yield
    for hh in H:
        w_C =jnp.broadcast_to(wC_row[:, hh:hh + 1], (ML_DH, 2 * ML_DH))
        CNn = w_C * CN[hh] + upd[hh]
        C_ref[hh] = CNn[:, 0:ML_DH]
        ncol[hh] = CNn[:, ML_DH:2 * ML_DH]
        n_ref[:, sls[hh]] = jnp.sum(CNn[:, ML_DH:2 * ML_DH] * eye, axis=0, keepdims=True)
    yield
    u = [rep(u_col, hh) for hh in H]
    s = [(qk[hh] * jnp.exp(jnp.where(causal, u[hh] + c_row[hh:hh + 1, :], -jnp.inf))).astype(BF16) for hh in H]
    sv = [jnp.dot(s[hh], vh1[hh], preferred_element_type=F32) for hh in H]
    yield
    hcell = []
    for hh in H:
        w_inter = jnp.exp(rep(m_prev, hh) + u[hh])
        num = sv[hh][:, 0:ML_DH] + w_inter * qc[hh][:, 0:ML_DH]
        den = sv[hh][:, ML_DH:2 * ML_DH] + w_inter * qc[hh][:, ML_DH:2 * ML_DH]
        hcell.append(num / jnp.maximum(jnp.abs(den), rep(e_col, hh)))
    mu = [jnp.dot(hcell[hh].astype(BF16), ones, preferred_element_type=F32) * (1.0 / ML_DH) for hh in H]
    yield
    oc = [hcell[hh] - mu[hh] for hh in H]
    var = [jnp.dot((oc[hh] * oc[hh]).astype(BF16), ones, preferred_element_type=F32) * (1.0 / ML_DH) for hh in H]
    for hh in H:
        sl = sls[hh]
        hn = oc[hh] * lax.rsqrt(var[hh] + GN_EPS) * gng_ref[:, sl] + gnb_ref[:, sl]
        yc_ref[:, sl] = (hn + skip_ref[:, sl] * xc[:, sl]) * _silu(z_ref[:, sl])


def _ml_dense(w):
    w2 = w.reshape(D_HALF, ML_QKV_BLOCK)
    c = jnp.arange(D_HALF)
    spread = (c[None, :] % ML_QKV_BLOCK == jnp.arange(ML_QKV_BLOCK)[:, None]).astype(w.dtype)
    full = jnp.dot(w2, spread, precision=lax.Precision.HIGHEST)
    same_block = c[:, None] // ML_QKV_BLOCK == c[None, :] // ML_QKV_BLOCK
    return jnp.where(same_block, full, 0.0).astype(BF16)


def _pad_lanes(a, n=LANES):
    return jnp.pad(a, [(0, 0)] * (a.ndim - 1) + [(0, n - a.shape[-1])])


def _mlstm_weights(Wl):
    wg = _pad_lanes(Wl['ml_w_gate']).astype(BF16)
    bg = _pad_lanes(Wl['ml_b_gate'].reshape(1, 2 * ML_HEADS))
    wgt = Wl['ml_w_gate'].T.astype(BF16)
    bgt = jnp.broadcast_to(Wl['ml_b_gate'].reshape(2 * ML_HEADS, 1), (2 * ML_HEADS, LANES))
    return wg, bg, wgt, bgt


def _mlstm_prefill(x3, w_in_all, j, st, Wl):
    B, T, _ = x3.shape
    L = CHUNK
    nc = T // L
    wg, bg, wgt, bgt = _mlstm_weights(Wl)
    nb = ML_NB
    perb = lambda *s: pl.BlockSpec((nb,) + s, lambda b, c: (b,) + (0,) * len(s))
    row2 = lambda a: a.reshape(1, D_HALF)
    ins = [x3, w_in_all, Wl['ml_conv_w'], row2(Wl['ml_conv_b']), Wl['ml_wq_d'], Wl['ml_wk_d'], Wl['ml_wv_d'],
           wg, bg, wgt, bgt, row2(Wl['ml_gn_g']), row2(Wl['ml_gn_b']), row2(Wl['ml_skip']),
           st['ml_conv'], st['ml_C'], st['ml_n'].reshape(B, 1, D_HALF),
           _pad_lanes(st['ml_m']).reshape(B, 1, LANES)]
    in_specs = [pl.BlockSpec((nb, L, D_MODEL), lambda b, c: (b, c, 0)),
                pl.BlockSpec((None, D_MODEL, 2 * D_HALF), lambda b, c: (j, 0, 0)),
                _full((CONV_W, D_HALF)), _full((1, D_HALF)),
                _full((D_HALF, D_HALF)), _full((D_HALF, D_HALF)), _full((D_HALF, D_HALF)),
                _full((3 * D_HALF, LANES)), _full((1, LANES)), _full((2 * ML_HEADS, 3 * D_HALF)),
                _full((2 * ML_HEADS, LANES)), _full((1, D_HALF)), _full((1, D_HALF)), _full((1, D_HALF)),
                perb(CONV_W - 1, D_HALF), perb(ML_HEADS, ML_DH, ML_DH), perb(1, D_HALF), perb(1, LANES)]
    seq = pl.BlockSpec((nb, L, D_HALF), lambda b, c: (b, c, 0))
    yc, conv, C, n, m = pl.pallas_call(
        functools.partial(_mlstm_prefill_kernel, L=L),
        grid=(B // nb, nc),
        in_specs=in_specs,
        out_specs=[seq, perb(CONV_W - 1, D_HALF), perb(ML_HEADS, ML_DH, ML_DH), perb(1, D_HALF), perb(1, LANES)],
        out_shape=[jax.ShapeDtypeStruct((B, T, D_HALF), F32),
                   jax.ShapeDtypeStruct((B, CONV_W - 1, D_HALF), F32),
                   jax.ShapeDtypeStruct((B, ML_HEADS, ML_DH, ML_DH), F32),
                   jax.ShapeDtypeStruct((B, 1, D_HALF), F32),
                   jax.ShapeDtypeStruct((B, 1, LANES), F32)],
        scratch_shapes=[pltpu.VMEM((nb, L + SUBLANES, D_HALF), F32),
                        pltpu.VMEM((nb, ML_HEADS, ML_DH, LANES), F32),
                        pltpu.VMEM((nb, L, 2 * D_HALF), F32)],
        compiler_params=_params(("parallel", "arbitrary")),
        name="mlstm_prefill",
    )(*ins)
    return yc, (C, n.reshape(B, ML_HEADS, ML_DH), m[:, 0, :ML_HEADS], conv)


def _mlstm_decode_kernel(p_ref, cw_ref, cb_ref, wq_ref, wk_ref, wv_ref, wg_ref, bg_ref,
                         gng_ref, gnb_ref, skip_ref, conv0_ref, C0_ref, n0_ref, m0_ref,
                         yc_ref, conv_ref, C_ref, n_ref, m_ref):
    xm = p_ref[:, 0:D_HALF]
    z = p_ref[:, D_HALF:2 * D_HALF]
    xc = cb_ref[...] + cw_ref[CONV_W - 1:CONV_W, :] * xm
    for i in range(CONV_W - 1):
        xc = xc + cw_ref[i:i + 1, :] * conv0_ref[i]
    for i in range(CONV_W - 2):
        conv_ref[i] = conv0_ref[i + 1]
    conv_ref[CONV_W - 2] = xm
    xc = _silu(xc)
    q, k, v, g = _mlstm_qkv_gates(xm, xc, wq_ref, wk_ref, wv_ref, wg_ref, bg_ref)
    lf_all = -_softplus(-g)
    lane = lax.broadcasted_iota(jnp.int32, (1, LANES), 1)
    row8 = lax.broadcasted_iota(jnp.int32, (SUBLANES, ML_DH), 0)
    m_all = m0_ref[...]
    m_out = m_all
    for hh in range(ML_HEADS):
        sl = slice(hh * ML_DH, (hh + 1) * ML_DH)
        qh, vh = q[:, sl], v[:, sl]
        kh = k[:, sl] * (ML_DH ** -0.5)
        li = g[:, hh:hh + 1]
        lf = lf_all[:, ML_HEADS + hh:ML_HEADS + hh + 1]
        m_prev = m_all[:, hh:hh + 1]
        n = n0_ref[:, sl]
        log_inter = lf + m_prev
        m_t = jnp.maximum(li, log_inter)
        s = jnp.sum(qh * kh, -1, keepdims=True) * jnp.exp(li - m_t)
        w_inter = jnp.exp(log_inter - m_t)
        w_k = jnp.exp(li - m_t)
        w_C = jnp.exp(log_inter - m_t)
        kw = kh * w_k
        rows = []
        for bi in range(DEC_BB):
            C = C0_ref[bi, hh]
            q8 = jnp.broadcast_to(qh[bi:bi + 1, :], (SUBLANES, ML_DH))
            rows.append(_dot_f32(q8, C)[0:1, :])
            k8 = jnp.where(row8 == 0, jnp.broadcast_to(kw[bi:bi + 1, :], (SUBLANES, ML_DH)), 0.0)
            v8 = jnp.broadcast_to(vh[bi:bi + 1, :], (SUBLANES, ML_DH))
            C_ref[bi, hh] = w_C[bi:bi + 1, :] * C + _dot_tn_f32(k8, v8)
        qC = jnp.concatenate(rows, axis=0)
        num = s * vh + w_inter * qC
        den = s + w_inter * jnp.sum(qh * n, -1, keepdims=True)
        hcell = num / jnp.maximum(jnp.abs(den), jnp.exp(-m_t))
        n_ref[:, sl] = w_C * n + kw
        m_out = jnp.where(lane == hh, m_t, m_out)
        hn = _head_norm(hcell, gng_ref[:, sl], gnb_ref[:, sl], GN_EPS)
        yc_ref[:, sl] = (hn + skip_ref[:, sl] * xc[:, sl]) * _silu(z[:, sl])
    m_ref[...] = m_out


def _mlstm_decode(p2, st, Wl, C_all, j, C_prev):
    B = p2.shape[0]
    bb = DEC_BB
    wg, bg, _, _ = _mlstm_weights(Wl)
    row2 = lambda a: a.reshape(1, D_HALF)
    rows = lambda n: pl.BlockSpec((bb, n), lambda i: (i, 0))
    convs = pl.BlockSpec((CONV_W - 1, bb, D_HALF), lambda i: (0, i, 0))
    ins = [p2, Wl['ml_conv_w'], row2(Wl['ml_conv_b']), Wl['ml_wq_d'], Wl['ml_wk_d'], Wl['ml_wv_d'], wg, bg,
           row2(Wl['ml_gn_g']), row2(Wl['ml_gn_b']), row2(Wl['ml_skip']),
           jnp.swapaxes(st['ml_conv'], 0, 1), C_all, st['ml_n'].reshape(B, D_HALF), _pad_lanes(st['ml_m'])]
    kern, Cs, extra_in, extra_specs, aliases, C_shape = _stacked_state_io(
        _mlstm_decode_kernel, C_all, C_prev, j, bb, len(ins) - 3, len(ins), 2)
    in_specs = [pl.BlockSpec((bb, 2 * D_HALF), lambda i: (i, 0)), _full((CONV_W, D_HALF)), _full((1, D_HALF)),
                _full((D_HALF, D_HALF)), _full((D_HALF, D_HALF)), _full((D_HALF, D_HALF)),
                _full((3 * D_HALF, LANES)), _full((1, LANES)),
                _full((1, D_HALF)), _full((1, D_HALF)), _full((1, D_HALF)),
                convs, Cs, rows(D_HALF), rows(LANES)] + extra_specs
    yc, conv, C, n, m = pl.pallas_call(
        kern,
        grid=(B // bb,),
        in_specs=in_specs,
        out_specs=[rows(D_HALF), convs, Cs, rows(D_HALF), rows(LANES)],
        out_shape=[jax.ShapeDtypeStruct((B, D_HALF), F32),
                   jax.ShapeDtypeStruct((CONV_W - 1, B, D_HALF), F32),
                   C_shape,
                   jax.ShapeDtypeStruct((B, D_HALF), F32),
                   jax.ShapeDtypeStruct((B, LANES), F32)],
        input_output_aliases=aliases,
        compiler_params=_params(("parallel",)),
        name="mlstm_decode",
    )(*ins, *extra_in)
    return yc, (C, n.reshape(B, ML_HEADS, ML_DH), m[:, :ML_HEADS], jnp.swapaxes(conv, 0, 1))


def _rwkv_pre_body(pr, pr_prev, mu_ref, w0_ref, a0_ref, w2_ref, a2_ref, g2_ref, kkw_ref, kaw_ref, rk_ref,
                   r_ref, d_ref, k_ref, v_ref, a_ref, b_ref, g_ref, bonus_ref):
    pm = pr + (pr_prev - pr) * mu_ref[...]
    r = pm[:, 0:D_HALF]
    kr = pm[:, D_HALF:2 * D_HALF]
    vr = pm[:, 2 * D_HALF:3 * D_HALF]
    lo = pm[:, 3 * D_HALF:RW_COLS_PAD]
    w_log = -_softplus(-(w0_ref[...] + _dot(jnp.tanh(lo), w2_ref[...]))) - 0.5
    a = _sigmoid(a0_ref[...] + _dot(lo, a2_ref[...]))
    g = _dot(_sigmoid(lo), g2_ref[...])
    ones_bd = _group_ones(LANES, RW_DH)
    kk = kr * kkw_ref[...]
    kk = kk / jnp.maximum(jnp.sqrt(_group_sum(kk * kk, ones_bd)), 1e-12)
    kh = kr * (1.0 + (a - 1.0) * kaw_ref[...])
    r_ref[...] = r
    d_ref[...] = jnp.exp(-jnp.exp(w_log))
    k_ref[...] = kh
    v_ref[...] = vr
    a_ref[...] = -kk
    b_ref[...] = kk * a
    g_ref[...] = g
    bonus_ref[...] = _group_sum(r * kh * rk_ref[...], ones_bd) * vr


def _rwkv_pre_prefill_kernel(pr_ref, shift0_ref, *rest, L):
    wrefs, outs, xbuf = rest[:9], rest[9:17], rest[17]
    c = pl.program_id(1)

    @pl.when(c == 0)
    def _():
        xbuf[SUBLANES - 1:SUBLANES, :] = shift0_ref[...]

    pr = pr_ref[...]
    xbuf[SUBLANES:SUBLANES + L, :] = pr
    pr_prev = xbuf[SUBLANES - 1:SUBLANES - 1 + L, :]
    xbuf[SUBLANES - 1:SUBLANES, :] = pr[L - 1:L, :]
    _rwkv_pre_body(pr, pr_prev, *wrefs, *outs)


def _rwkv_pre_decode_kernel(pr_ref, prev_ref, *rest):
    _rwkv_pre_body(pr_ref[...], prev_ref[...], *rest[:9], *rest[9:17])


def _rwkv_pre_weights(Wl):
    row2 = lambda a: a.reshape(1, D_HALF)
    padr = lambda w, o: jnp.pad(w, ((o, RW_LORA_PAD - o - w.shape[0]), (0, 0))).astype(BF16)
    mu = _pad_lanes(Wl['rw_mu'].reshape(1, RW_SHIFT_COLS), RW_COLS_PAD)
    ws = [mu, row2(Wl['rw_w0']), row2(Wl['rw_a0']),
          padr(Wl['rw_w2'], 0), padr(Wl['rw_a2'], RW_DECAY_LORA), padr(Wl['rw_g2'], RW_DECAY_LORA + RW_A_LORA),
          row2(Wl['rw_kk']), row2(Wl['rw_ka']), row2(Wl['rw_rk'])]
    specs = [_full((1, RW_COLS_PAD)), _full((1, D_HALF)), _full((1, D_HALF)),
             _full((RW_LORA_PAD, D_HALF)), _full((RW_LORA_PAD, D_HALF)), _full((RW_LORA_PAD, D_HALF)),
             _full((1, D_HALF)), _full((1, D_HALF)), _full((1, D_HALF))]
    return ws, specs


def _rwkv_pre_prefill(pr3, shift0, Wl):
    B, T, _ = pr3.shape
    L = CHUNK
    ws, wspecs = _rwkv_pre_weights(Wl)
    seq = pl.BlockSpec((None, L, D_HALF), lambda b, c: (b, c, 0))
    outs = pl.pallas_call(
        functools.partial(_rwkv_pre_prefill_kernel, L=L),
        grid=(B, T // L),
        in_specs=[pl.BlockSpec((None, L, RW_COLS_PAD), lambda b, c: (b, c, 0)),
                  pl.BlockSpec((None, 1, RW_COLS_PAD), lambda b, c: (b, 0, 0))] + wspecs,
        out_specs=[seq] * 8,
        out_shape=[jax.ShapeDtypeStruct((B, T, D_HALF), F32)] * 8,
        scratch_shapes=[pltpu.VMEM((L + SUBLANES, RW_COLS_PAD), F32)],
        compiler_params=_params(("parallel", "arbitrary")),
        name="rwkv_pre_prefill",
    )(pr3, _pad_lanes(shift0, RW_COLS_PAD).reshape(B, 1, RW_COLS_PAD), *ws)
    return outs


def _rwkv_pre_decode(pr, shift0, Wl):
    B = pr.shape[0]
    ws, wspecs = _rwkv_pre_weights(Wl)
    full2 = lambda n: pl.BlockSpec((B, n), lambda i: (0, 0))
    outs = pl.pallas_call(
        _rwkv_pre_decode_kernel,
        grid=(1,),
        in_specs=[full2(RW_COLS_PAD), full2(RW_COLS_PAD)] + wspecs,
        out_specs=[full2(D_HALF)] * 8,
        out_shape=[jax.ShapeDtypeStruct((B, D_HALF), F32)] * 8,
        compiler_params=_params(("arbitrary",)),
        name="rwkv_pre_decode",
    )(pr, _pad_lanes(shift0, RW_COLS_PAD), *ws)
    return outs


RW_IP = RW_DH // 2


def _rwkv_rec_kernel(r_ref, d_ref, k_ref, a_ref, b_ref, v_ref, S0_ref, y_ref, S_ref, *, Tc):
    @pl.when(pl.program_id(0) == 0)
    def _():
        S_ref[...] = S0_ref[...]

    lane = lax.broadcasted_iota(jnp.int32, (1, LANES), 1)

    def tiles(t):
        back = (LANES - (t % RW_TB) * RW_BB) % LANES
        out = []
        for ref in (a_ref, d_ref, b_ref, k_ref, r_ref):
            raw = ref[t]
            out.append(jnp.where(lane < LANES // 2, pltpu.roll(raw, back, 1),
                                 pltpu.roll(raw, (back + LANES // 2) % LANES, 1)))
        out.append(pltpu.roll(v_ref[t], back, 1))
        return tuple(out)

    def step(t, carry):
        a, d, b, k, r, vt = carry
        nxt = tiles(jnp.minimum(t + 1, Tc - 1))
        rows = []
        for ip in range(RW_IP):
            S = S_ref[ip]
            sa = jnp.sum(S * a, axis=0, keepdims=True)
            Sn = S * d + sa * b + vt[ip:ip + 1, :] * k
            S_ref[ip] = Sn
            rows.append(jnp.sum(Sn * r, axis=0, keepdims=True))
        y_ref[t] = jnp.concatenate(rows, axis=0)
        return nxt

    lax.fori_loop(0, Tc, step, tiles(0), unroll=8)


def _rwkv_rec_call(r, d, k, a, b, v, S0):
    T = r.shape[0]
    Tc = min(T, 64)
    vec = pl.BlockSpec((Tc, RW_DH, LANES), lambda c: (c, 0, 0))
    vsp = pl.BlockSpec((Tc, RW_IP, LANES), lambda c: (c, 0, 0))
    ssp = _full((RW_IP, RW_DH, LANES))
    return pl.pallas_call(
        functools.partial(_rwkv_rec_kernel, Tc=Tc),
        grid=(T // Tc,),
        in_specs=[vec] * 5 + [vsp, ssp],
        out_specs=[vsp, ssp],
        out_shape=[jax.ShapeDtypeStruct((T, RW_IP, LANES), F32),
                   jax.ShapeDtypeStruct((RW_IP, RW_DH, LANES), F32)],
        compiler_params=_params(("arbitrary",)),
        name="rwkv_recurrence",
    )(r, d, k, a, b, v, S0)


def _rwkv_dec_kernel(r_ref, d_ref, k_ref, a_ref, b_ref, v_ref, S0_ref, y_ref, S_ref):
    a, d, b, k, r = a_ref[...], d_ref[...], b_ref[...], k_ref[...], r_ref[...]
    v = v_ref[...]
    rows = []
    for i in range(RW_DH):
        S = S0_ref[i]
        sa = jnp.sum(S * a, axis=0, keepdims=True)
        Sn = S * d + sa * b + v[i:i + 1, :] * k
        S_ref[i] = Sn
        rows.append(jnp.sum(Sn * r, axis=0, keepdims=True))
    y_ref[...] = jnp.concatenate(rows, axis=0)


def _rwkv_decode_step(r, d, k, v, a, b, S0):
    B = r.shape[0]
    tr = lambda x: x.T.reshape(RW_HEADS, RW_DH, B)
    St = S0.reshape(B, RW_HEADS * RW_DH * RW_DH).T.reshape(RW_HEADS, RW_DH, RW_DH, B)
    vec = pl.BlockSpec((None, RW_DH, B), lambda h: (h, 0, 0))
    ssp = pl.BlockSpec((None, RW_DH, RW_DH, B), lambda h: (h, 0, 0, 0))
    y, S = pl.pallas_call(
        _rwkv_dec_kernel,
        grid=(RW_HEADS,),
        in_specs=[vec] * 6 + [ssp],
        out_specs=[vec, ssp],
        out_shape=[jax.ShapeDtypeStruct((RW_HEADS, RW_DH, B), F32),
                   jax.ShapeDtypeStruct((RW_HEADS, RW_DH, RW_DH, B), F32)],
        compiler_params=_params(("parallel",)),
        name="rwkv_decode_step",
    )(tr(r), tr(d), tr(k), tr(a), tr(b), tr(v), St)
    y = y.reshape(D_HALF, B).T
    S = S.reshape(RW_HEADS * RW_DH * RW_DH, B).T.reshape(B, RW_HEADS, RW_DH, RW_DH)
    return y, S


RW_TB = LANES // RW_BB
RW_NSB = 2
RW_MM_ROWS = 256


def _head_sum_rows(x):
    x3 = x.reshape(RW_HEADS, RW_DH, x.shape[-1])
    s = jnp.sum(x3, axis=1, keepdims=True)
    return jnp.broadcast_to(s, x3.shape).reshape(x.shape)


def _rwkv_pre_t_kernel(x_ref, shift0_ref, w_ref, mu_ref, w0_ref, a0_ref, w2_ref, a2_ref, g2_ref,
                       kkw_ref, kaw_ref, rk_ref,
                       r_ref, d_ref, k_ref, a_ref, b_ref, v_ref, g_ref, bonus_ref, last_ref, prev_scr):
    @pl.when(pl.program_id(0) == 0)
    def _():
        prev_scr[...] = shift0_ref[...]

    ro = lax.broadcasted_iota(jnp.int32, (LANES, LANES), 0)
    ci = lax.broadcasted_iota(jnp.int32, (LANES, LANES), 1)
    perm = jnp.where(ci == (ro % RW_BB) * RW_TB + ro // RW_BB, 1.0, 0.0).astype(BF16)
    lane = lax.broadcasted_iota(jnp.int32, (1, LANES), 1)
    grp = lane // RW_BB
    ngrp = LANES // RW_BB
    prs = {}

    def scatter(x, o_ref, t0, nrow, npiece):
        rot = [x[q * nrow:(q + 1) * nrow, :] if q == 0 else pltpu.roll(x[q * nrow:(q + 1) * nrow, :], q * RW_BB, 1)
               for q in range(npiece)]
        for t in range(RW_TB):
            m = rot[0]
            for q in range(1, npiece):
                m = jnp.where(grp == (t + q) % ngrp, rot[q], m)
            o_ref[t0 + t] = m

    def block(sb):
        t0 = sb * RW_TB
        xn = x_ref[:, t0:t0 + RW_TB, :].reshape(RW_BB * RW_TB, D_MODEL).astype(BF16)
        xg = jnp.dot(perm, xn, preferred_element_type=F32).astype(BF16)
        parts = []
        for r0 in range(0, RW_COLS_PAD, RW_MM_ROWS):
            parts.append(lax.dot_general(w_ref[r0:r0 + RW_MM_ROWS, :], xg, (((1,), (1,)), ((), ())),
                                         preferred_element_type=F32))
            yield
        pr = jnp.concatenate(parts, axis=0)
        prs[sb] = pr
        rolled = pltpu.roll(pr, RW_BB, 1)
        before = prev_scr[...] if sb == 0 else prs[sb - 1]
        prev = jnp.where(lane < RW_BB, before, rolled)
        prs[sb] = rolled
        if sb == RW_NSB - 1:
            prev_scr[...] = rolled
            last_ref[...] = pr
        pm = pr + (prev - pr) * mu_ref[...]
        r = pm[0:D_HALF]
        kr = pm[D_HALF:2 * D_HALF]
        vr = pm[2 * D_HALF:3 * D_HALF]
        lo = pm[3 * D_HALF:RW_COLS_PAD]
        yield
        w_log = -_softplus(-(w0_ref[...] + _dot(w2_ref[...], jnp.tanh(lo)))) - 0.5
        a = _sigmoid(a0_ref[...] + _dot(a2_ref[...], lo))
        g = _dot(g2_ref[...], _sigmoid(lo))
        kk = kr * kkw_ref[...]
        kk = kk / jnp.maximum(jnp.sqrt(_head_sum_rows(kk * kk)), 1e-12)
        kh = kr * (1.0 + (a - 1.0) * kaw_ref[...])
        g_ref[sb] = g
        bonus_ref[sb] = _head_sum_rows(r * kh * rk_ref[...]) * vr
        yield
        scatter(r, r_ref, t0, RW_DH, RW_HEADS)
        yield
        scatter(jnp.exp(-jnp.exp(w_log)), d_ref, t0, RW_DH, RW_HEADS)
        yield
        scatter(kh, k_ref, t0, RW_DH, RW_HEADS)
        yield
        scatter(-kk, a_ref, t0, RW_DH, RW_HEADS)
        yield
        scatter(kk * a, b_ref, t0, RW_DH, RW_HEADS)
        yield
        vv = jnp.concatenate([vr[h * RW_DH + half * RW_IP:h * RW_DH + (half + 1) * RW_IP, :]
                              for half in range(2) for h in range(RW_HEADS)], axis=0)
        scatter(vv, v_ref, t0, RW_IP, ngrp)

    _skewed([block(sb) for sb in range(RW_NSB)], lag=RW_COLS_PAD // RW_MM_ROWS)


def _lane_bcast(a, n):
    return jnp.broadcast_to(a.reshape(n, 1), (n, LANES))


def _rwkv_pre_t(x3, shift0, w_rwt, j, Wl):
    B, T, _ = x3.shape
    nblk = T // RW_TB
    padr = lambda w, o: jnp.pad(w, ((o, RW_LORA_PAD - o - w.shape[0]), (0, 0))).astype(BF16).T
    sh = jnp.pad(shift0.T, ((0, RW_COLS_PAD - RW_SHIFT_COLS), (0, LANES - RW_BB)))
    col = lambda a: _lane_bcast(a, D_HALF)
    ins = [x3, sh, w_rwt, _lane_bcast(_pad_lanes(Wl['rw_mu'].reshape(1, -1), RW_COLS_PAD), RW_COLS_PAD),
           col(Wl['rw_w0']), col(Wl['rw_a0']),
           padr(Wl['rw_w2'], 0), padr(Wl['rw_a2'], RW_DECAY_LORA), padr(Wl['rw_g2'], RW_DECAY_LORA + RW_A_LORA),
           col(Wl['rw_kk']), col(Wl['rw_ka']), col(Wl['rw_rk'])]
    tb = RW_TB * RW_NSB
    in_specs = [pl.BlockSpec((B, tb, D_MODEL), lambda c: (0, c, 0)), _full((RW_COLS_PAD, LANES)),
                pl.BlockSpec((None, RW_COLS_PAD, D_MODEL), lambda c: (j, 0, 0)), _full((RW_COLS_PAD, LANES)),
                _full((D_HALF, LANES)), _full((D_HALF, LANES)),
                _full((D_HALF, RW_LORA_PAD)), _full((D_HALF, RW_LORA_PAD)), _full((D_HALF, RW_LORA_PAD)),
                _full((D_HALF, LANES)), _full((D_HALF, LANES)), _full((D_HALF, LANES))]
    blk = pl.BlockSpec((RW_NSB, D_HALF, LANES), lambda c: (c, 0, 0))
    ktile = pl.BlockSpec((tb, RW_DH, LANES), lambda c: (c, 0, 0))
    vtile = pl.BlockSpec((tb, RW_IP, LANES), lambda c: (c, 0, 0))
    outs = pl.pallas_call(
        _rwkv_pre_t_kernel,
        grid=(T // tb,),
        in_specs=in_specs,
        out_specs=[ktile] * 5 + [vtile, blk, blk, _full((RW_COLS_PAD, LANES))],
        out_shape=[jax.ShapeDtypeStruct((T, RW_DH, LANES), F32)] * 5
                  + [jax.ShapeDtypeStruct((T, RW_IP, LANES), F32)]
                  + [jax.ShapeDtypeStruct((nblk, D_HALF, LANES), F32)] * 2
                  + [jax.ShapeDtypeStruct((RW_COLS_PAD, LANES), F32)],
        scratch_shapes=[pltpu.VMEM((RW_COLS_PAD, LANES), F32)],
        compiler_params=_params(("arbitrary",)),
        name="rwkv_pre_t",
    )(*ins)
    shift_new = outs[8][:RW_SHIFT_COLS, LANES - RW_BB:].T
    return outs[:8], shift_new


RW_NPB = 8


def _rwkv_post_t_kernel(y_ref, g_ref, bonus_ref, gng_ref, gnb_ref, yd_ref):
    _round_robin([_rwkv_post_t_block(y_ref, g_ref, bonus_ref, gng_ref, gnb_ref, yd_ref, pb)
                  for pb in range(RW_NPB)])


def _rwkv_post_t_block(y_ref, g_ref, bonus_ref, gng_ref, gnb_ref, yd_ref, pb):
    t0 = pb * RW_TB
    lane = lax.broadcasted_iota(jnp.int32, (1, LANES), 1)
    grp = lane // RW_BB
    ngrp = LANES // RW_BB
    ys = [y_ref[t0 + t] for t in range(RW_TB)]
    rolled = []
    for s in range(ngrp):
        m = ys[s % RW_TB]
        for q in range(1, ngrp):
            m = jnp.where(grp == q, ys[(q + s) % RW_TB], m)
        rolled.append(pltpu.roll(m, s * RW_BB, 1) if s else m)
    yield
    pieces = {}
    for q in range(ngrp):
        m = rolled[(-q) % ngrp]
        for t in range(1, RW_TB):
            m = jnp.where(grp == t, rolled[(t - q) % ngrp], m)
        pieces[divmod(q, RW_HEADS)] = m
    y = jnp.concatenate([pieces[(half, h)] for h in range(RW_HEADS) for half in range(2)], axis=0)
    yield
    mu = _head_sum_rows(y) * (1.0 / RW_DH)
    yc = y - mu
    var = _head_sum_rows(yc * yc) * (1.0 / RW_DH)
    hn = yc * lax.rsqrt(var + RW_GN_EPS) * gng_ref[...] + gnb_ref[...]
    yd = ((hn + bonus_ref[pb]) * g_ref[pb]).T
    yield
    for t in range(RW_TB):
        yd_ref[:, t0 + t, :] = yd[t * RW_BB:(t + 1) * RW_BB, :]


def _rwkv_rec_t(vecs, S0, Wl, T):
    r, d, k, a, b, v, g, bonus = vecs
    nblk = T // RW_TB
    B = RW_BB
    Sr = S0.reshape(B, RW_HEADS, 2, RW_IP, RW_DH).transpose(3, 4, 2, 1, 0).reshape(RW_IP, RW_DH, LANES)
    y, S = _rwkv_rec_call(r, d, k, a, b, v, Sr)
    tb = RW_TB * RW_NPB
    blk = pl.BlockSpec((RW_NPB, D_HALF, LANES), lambda c: (c, 0, 0))
    col = lambda a_: _lane_bcast(a_, D_HALF)
    yd = pl.pallas_call(
        _rwkv_post_t_kernel,
        grid=(nblk // RW_NPB,),
        in_specs=[pl.BlockSpec((tb, RW_IP, LANES), lambda c: (c, 0, 0)), blk, blk,
                  _full((D_HALF, LANES)), _full((D_HALF, LANES))],
        out_specs=pl.BlockSpec((B, tb, D_HALF), lambda c: (0, c, 0)),
        out_shape=jax.ShapeDtypeStruct((B, T, D_HALF), F32),
        compiler_params=_params(("parallel",)),
        name="rwkv_post_t",
    )(y, g, bonus, col(Wl['rw_gn_g']), col(Wl['rw_gn_b']))
    S = S.reshape(RW_IP, RW_DH, 2, RW_HEADS, B).transpose(4, 3, 2, 0, 1).reshape(B, RW_HEADS, RW_DH, RW_DH)
    return yd, S


def _rwkv_post_kernel(y_ref, g_ref, bonus_ref, gng_ref, gnb_ref, o_ref):
    ones_bd = _group_ones(LANES, RW_DH)
    y = y_ref[...]
    mu = _group_sum(y, ones_bd) * (1.0 / RW_DH)
    yc = y - mu
    var = _group_sum(yc * yc, ones_bd) * (1.0 / RW_DH)
    hn = yc * lax.rsqrt(var + RW_GN_EPS) * gng_ref[...] + gnb_ref[...]
    o_ref[...] = (hn + bonus_ref[...]) * g_ref[...]


def _rwkv_post(y2, g2, bonus2, Wl, tm):
    M = y2.shape[0]
    row = pl.BlockSpec((tm, D_HALF), lambda i: (i, 0))
    return pl.pallas_call(
        _rwkv_post_kernel,
        grid=(M // tm,),
        in_specs=[row, row, row, _full((1, D_HALF)), _full((1, D_HALF))],
        out_specs=row,
        out_shape=jax.ShapeDtypeStruct((M, D_HALF), F32),
        compiler_params=_params(("parallel",)),
        name="rwkv_post",
    )(y2, g2, bonus2, Wl['rw_gn_g'].reshape(1, D_HALF), Wl['rw_gn_b'].reshape(1, D_HALF))


def _trunk(x, st, pos, W):
    B, T, _ = x.shape
    M = B * T
    decode = T == 1
    tm_proj = min(M, 512)
    tm_post = min(M, 2 * POST_SUB)
    tm_rw = min(M, 512)
    x2 = x.reshape(M, D_MODEL)
    new = {name: [] for name in st}
    stacked = {}
    for l in range(DEPTH):
        j = l // 2
        Wl = {name: v[j] for name, v in W['per_pair'][l % 2].items()}
        stl = {name: v[j] for name, v in st.items()}
        if l % 2 == 0:
            if decode:
                p, = _proj(x2, W['ev_w_in'], j, tm_proj, (EVEN_IN,))
                ya, yb, (h, cb, S) = _even_decode(p, pos, stl, Wl, st['ret_S'], j, stacked.get('ret_S'))
                stacked['ret_S'] = S
            else:
                ya, yb, (h, cb, S) = _even_prefill(x2.reshape(B, T, D_MODEL), W['ev_w_in'], j, pos, stl, Wl)
                new['ret_S'].append(S)
            new['lru_h'].append(h)
            new['lru_conv'].append(cb)
            wout = W['ev_w_out']
        else:
            if decode:
                p_ml, p_rw = _proj(x2, W['od_w_in'], j, tm_proj, (2 * D_HALF, RW_COLS_PAD))
                ya, (C, n, m, cb) = _mlstm_decode(p_ml, stl, Wl, st['ml_C'], j, stacked.get('ml_C'))
                stacked['ml_C'] = C
                r, d, k, v, a, b, g, bonus = _rwkv_pre_decode(p_rw, stl['rw_shift'], Wl)
                y, S = _rwkv_decode_step(r, d, k, v, a, b, stl['rw_S'])
                yb = _rwkv_post(y, g, bonus, Wl, tm_rw)
                shift_new = p_rw[:, :RW_SHIFT_COLS]
            else:
                x3 = x2.reshape(B, T, D_MODEL)
                ya, (C, n, m, cb) = _mlstm_prefill(x3, W['od_w_in'], j, stl, Wl)
                vecs, shift_new = _rwkv_pre_t(x3, stl['rw_shift'], W['od_w_rwt'], j, Wl)
                yb, S = _rwkv_rec_t(vecs, stl['rw_S'], Wl, T)
                new['ml_C'].append(C)
            new['ml_n'].append(n)
            new['ml_m'].append(m)
            new['ml_conv'].append(cb)
            new['rw_S'].append(S)
            new['rw_shift'].append(shift_new)
            wout = W['od_w_out']
        x2 = _post(ya.reshape(M, D_HALF), yb.reshape(M, D_HALF), x2, wout, j, l,
                   W['ln1_g'], W['ln1_b'], W['mlp_w1'], W['mlp_w2'], W['ln2_g'], W['ln2_b'], tm_post)
    out = {name: stacked[name] if name in stacked else jnp.stack(v) for name, v in new.items()}
    return x2.reshape(B, T, D_MODEL), out


def _prepare_weights(w):
    even_names = ('lru_conv_w', 'lru_conv_b', 'lru_ba', 'lru_bx', 'lru_lambda', 'ret_gn_g', 'ret_gn_b')
    odd_names = ('ml_conv_w', 'ml_conv_b', 'ml_w_gate', 'ml_b_gate', 'ml_gn_g', 'ml_gn_b', 'ml_skip',
                 'rw_mu', 'rw_w0', 'rw_w2', 'rw_a0', 'rw_a2', 'rw_g2', 'rw_kk', 'rw_ka', 'rw_rk', 'rw_gn_g', 'rw_gn_b')
    even = {n: w[n] for n in even_names}
    even.update(lru_wa_d=jax.vmap(_lru_dense)(w['lru_wa']), lru_wx_d=jax.vmap(_lru_dense)(w['lru_wx']))
    odd = {n: w[n] for n in odd_names}
    odd.update(ml_wq_d=jax.vmap(_ml_dense)(w['ml_wq']), ml_wk_d=jax.vmap(_ml_dense)(w['ml_wk']),
               ml_wv_d=jax.vmap(_ml_dense)(w['ml_wv']))
    od_w_in = _pad_lanes(w['od_w_in'], ODD_IN_PAD).astype(BF16)
    return dict(per_pair=(even, odd),
                ev_w_in=w['ev_w_in'].astype(BF16), ev_w_out=w['ev_w_out'].astype(BF16),
                od_w_in=od_w_in, od_w_rwt=jnp.swapaxes(od_w_in[:, :, 2 * D_HALF:], 1, 2),
                od_w_out=w['od_w_out'].astype(BF16),
                mlp_w1=w['mlp_w1'].astype(BF16), mlp_w2=w['mlp_w2'].astype(BF16),
                ln1_g=w['ln1_g'], ln1_b=w['ln1_b'], ln2_g=w['ln2_g'], ln2_b=w['ln2_b'])


def _zero_states(batch):
    z = lambda *s: jnp.zeros(s, F32)
    n_even, n_odd = (DEPTH + 1) // 2, DEPTH // 2
    return dict(lru_h=z(n_even, batch, D_HALF), lru_conv=z(n_even, batch, CONV_W - 1, D_HALF),
                ret_S=z(n_even, batch, RET_HEADS, RET_DH, RET_DH),
                ml_C=z(n_odd, batch, ML_HEADS, ML_DH, ML_DH), ml_n=z(n_odd, batch, ML_HEADS, ML_DH),
                ml_m=z(n_odd, batch, ML_HEADS), ml_conv=z(n_odd, batch, CONV_W - 1, D_HALF),
                rw_S=z(n_odd, batch, RW_HEADS, RW_DH, RW_DH), rw_shift=z(n_odd, batch, RW_SHIFT_COLS))


def kernel(x_prompt, x_sample, state_lru_h, state_lru_conv, state_ret, state_mlstm_C, state_mlstm_n, state_mlstm_m, state_mlstm_conv, state_rwkv_S, state_rwkv_shift, ln1_g, ln1_b, ln2_g, ln2_b, mlp_w1, mlp_w2, ev_w_in, ev_w_out, lru_conv_w, lru_conv_b, lru_wa, lru_ba, lru_wx, lru_bx, lru_lambda, ret_gn_g, ret_gn_b, od_w_in, od_w_out, ml_conv_w, ml_conv_b, ml_wq, ml_wk, ml_wv, ml_w_gate, ml_b_gate, ml_gn_g, ml_gn_b, ml_skip, rw_mu, rw_w0, rw_w2, rw_a0, rw_a2, rw_g2, rw_kk, rw_ka, rw_rk, rw_gn_g, rw_gn_b):
    W = _prepare_weights(dict(
        ln1_g=ln1_g, ln1_b=ln1_b, ln2_g=ln2_g, ln2_b=ln2_b, mlp_w1=mlp_w1, mlp_w2=mlp_w2,
        ev_w_in=ev_w_in, ev_w_out=ev_w_out, lru_conv_w=lru_conv_w, lru_conv_b=lru_conv_b,
        lru_wa=lru_wa, lru_ba=lru_ba, lru_wx=lru_wx, lru_bx=lru_bx, lru_lambda=lru_lambda,
        ret_gn_g=ret_gn_g, ret_gn_b=ret_gn_b, od_w_in=od_w_in, od_w_out=od_w_out,
        ml_conv_w=ml_conv_w, ml_conv_b=ml_conv_b, ml_wq=ml_wq, ml_wk=ml_wk, ml_wv=ml_wv,
        ml_w_gate=ml_w_gate, ml_b_gate=ml_b_gate, ml_gn_g=ml_gn_g, ml_gn_b=ml_gn_b, ml_skip=ml_skip,
        rw_mu=rw_mu, rw_w0=rw_w0, rw_w2=rw_w2, rw_a0=rw_a0, rw_a2=rw_a2, rw_g2=rw_g2,
        rw_kk=rw_kk, rw_ka=rw_ka, rw_rk=rw_rk, rw_gn_g=rw_gn_g, rw_gn_b=rw_gn_b))
    st_sample = dict(lru_h=state_lru_h, lru_conv=state_lru_conv, ret_S=state_ret,
                     ml_C=state_mlstm_C, ml_n=state_mlstm_n, ml_m=state_mlstm_m, ml_conv=state_mlstm_conv,
                     rw_S=state_rwkv_S, rw_shift=state_rwkv_shift)
    pos_prompt = jnp.arange(x_prompt.shape[1], dtype=jnp.int32)
    pos_sample = PAST_LEN + jnp.arange(x_sample.shape[1], dtype=jnp.int32)
    y_prompt, sp = _trunk(x_prompt, _zero_states(x_prompt.shape[0]), pos_prompt, W)
    y_sample, ss = _trunk(x_sample, st_sample, pos_sample, W)
    names = ('lru_h', 'lru_conv', 'ret_S', 'ml_C', 'ml_n', 'ml_m', 'ml_conv', 'rw_S', 'rw_shift')
    return (y_prompt, y_sample) + tuple(sp[n] for n in names) + tuple(ss[n] for n in names)
```

```python
import functools

import jax
import jax.numpy as jnp
from jax import lax
from jax.experimental import pallas as pl
from jax.experimental.pallas import tpu as pltpu

F32 = jnp.float32
BF16 = jnp.bfloat16

D_MODEL = 1024
DEPTH = 4
PAST_LEN = 16384
D_HALF = D_MODEL // 2
CONV_W = 4
LRU_BLOCKS = 8
LRU_BLOCK = D_HALF // LRU_BLOCKS
LRU_C = 8.0
RET_HEADS = 4
RET_DH = D_HALF // RET_HEADS
CHUNK = 128
ROPE_BASE = 10000.0
ML_HEADS = 4
ML_DH = D_HALF // ML_HEADS
ML_QKV_BLOCK = 4
ML_NBLK = D_HALF // ML_QKV_BLOCK
RW_HEADS = 8
RW_DH = D_HALF // RW_HEADS
RW_DECAY_LORA = 32
RW_A_LORA = 32
RW_GATE_LORA = 96
RW_LORA = RW_DECAY_LORA + RW_A_LORA + RW_GATE_LORA
RW_SHIFT_COLS = 3 * D_HALF + RW_LORA
D_FF = 4 * D_MODEL
ALPHA = (2.0 * DEPTH) ** 0.25
EVEN_IN = 6 * D_HALF
ODD_IN = 2 * D_HALF + RW_SHIFT_COLS
LN_EPS = 1e-5
GN_EPS = 1e-5
RW_GN_EPS = 64e-5

LANES = 128
SUBLANES = 8
RW_LORA_PAD = 2 * LANES
RW_COLS_PAD = 3 * D_HALF + RW_LORA_PAD
ODD_IN_PAD = 2 * D_HALF + RW_COLS_PAD
RW_BB = 8
VMEM_LIMIT = 56 * 1024 * 1024


def _params(sem):
    return pltpu.CompilerParams(dimension_semantics=sem, vmem_limit_bytes=VMEM_LIMIT)


def _dot(a, b):
    return jnp.dot(a.astype(BF16), b.astype(BF16), preferred_element_type=F32)


def _dot_nt(a, b):
    return lax.dot_general(a.astype(BF16), b.astype(BF16), (((1,), (1,)), ((), ())),
                           preferred_element_type=F32)


def _dot_tn(a, b):
    return lax.dot_general(a.astype(BF16), b.astype(BF16), (((0,), (0,)), ((), ())),
                           preferred_element_type=F32)


def _split3(a):
    hi = a.astype(BF16)
    r1 = a - hi.astype(F32)
    mid = r1.astype(BF16)
    lo = (r1 - mid.astype(F32)).astype(BF16)
    return hi, mid, lo


def _xdot(a, b01):
    hi, mid, lo = _split3(a)
    f = lambda t: jnp.dot(t, b01, preferred_element_type=F32)
    return f(hi) + f(mid) + f(lo)


def _xdot_l(b01, a):
    hi, mid, lo = _split3(a)
    f = lambda t: jnp.dot(b01, t, preferred_element_type=F32)
    return f(hi) + f(mid) + f(lo)


def _sigmoid(x):
    return 1.0 / (1.0 + jnp.exp(-x))


def _silu(x):
    return x * _sigmoid(x)


def _softplus(x):
    return jnp.maximum(x, 0.0) + jnp.log1p(jnp.exp(-jnp.abs(x)))


def _gelu_tanh(x):
    return 0.5 * x * (1.0 + jnp.tanh(0.7978845608028654 * (x + 0.044715 * (x * x * x))))


def _layer_norm(x, g, b, eps):
    mu = jnp.mean(x, -1, keepdims=True)
    xc = x - mu
    var = jnp.mean(xc * xc, -1, keepdims=True)
    return xc * lax.rsqrt(var + eps) * g + b


def _group_ones(n, group):
    r = lax.broadcasted_iota(jnp.int32, (n, n), 0) // group
    c = lax.broadcasted_iota(jnp.int32, (n, n), 1) // group
    return jnp.where(r == c, 1.0, 0.0).astype(BF16)


def _group_sum(x, ones_bd):
    parts = [_xdot(x[:, s * LANES:(s + 1) * LANES], ones_bd) for s in range(x.shape[1] // LANES)]
    return jnp.concatenate(parts, axis=-1)


def _rotate(x, cosf, sinf):
    return x * cosf + pltpu.roll(x, RET_DH // 2, 1) * sinf


def _proj_kernel(x_ref, w_ref, *o_refs, splits):
    xb = x_ref[...].astype(BF16)
    off = 0
    for o_ref, n in zip(o_refs, splits):
        o_ref[...] = jnp.dot(xb, w_ref[:, off:off + n], preferred_element_type=F32)
        off += n


def _proj(x2d, w_all, j, tm, splits):
    M = x2d.shape[0]
    N = w_all.shape[2]
    return pl.pallas_call(
        functools.partial(_proj_kernel, splits=splits),
        grid=(M // tm,),
        in_specs=[pl.BlockSpec((tm, D_MODEL), lambda i: (i, 0)),
                  pl.BlockSpec((None, D_MODEL, N), lambda i: (j, 0, 0))],
        out_specs=[pl.BlockSpec((tm, n), lambda i: (i, 0)) for n in splits],
        out_shape=[jax.ShapeDtypeStruct((M, n), F32) for n in splits],
        compiler_params=_params(("parallel",)),
        name="proj_in",
    )(x2d, w_all)


FF_CHUNK = 1024


POST_SUB = 256


def _post_kernel(ya_ref, yb_ref, x_ref, wo_ref, g1_ref, b1_ref, w1_ref, w2_ref, g2_ref, b2_ref, o_ref):
    tm = x_ref.shape[0]
    sub = min(tm, POST_SUB)

    def tile(r0):
        rows = slice(r0, r0 + sub)
        y = (jnp.dot(ya_ref[rows, :].astype(BF16), wo_ref[0:D_HALF, :], preferred_element_type=F32)
             + jnp.dot(yb_ref[rows, :].astype(BF16), wo_ref[D_HALF:D_MODEL, :], preferred_element_type=F32))
        yield
        x1 = _layer_norm(ALPHA * x_ref[rows, :] + y, g1_ref[...], b1_ref[...], LN_EPS)
        x1b = x1.astype(BF16)
        acc = jnp.zeros(x1.shape, F32)
        for c in range(D_FF // FF_CHUNK):
            yield
            h = jnp.dot(x1b, w1_ref[:, c * FF_CHUNK:(c + 1) * FF_CHUNK], preferred_element_type=F32)
            yield
            h = jnp.square(jnp.maximum(h, 0.0))
            acc = acc + jnp.dot(h.astype(BF16), w2_ref[c * FF_CHUNK:(c + 1) * FF_CHUNK, :],
                                preferred_element_type=F32)
        yield
        o_ref[rows, :] = _layer_norm(ALPHA * x1 + acc, g2_ref[...], b2_ref[...], LN_EPS)

    _skewed([tile(r0) for r0 in range(0, tm, sub)])


def _post(ya, yb, x2d, wout_all, j, l, ln1_g, ln1_b, w1_all, w2_all, ln2_g, ln2_b, tm):
    M = x2d.shape[0]
    row = lambda i: (i, 0)
    vec = pl.BlockSpec((None, 1, D_MODEL), lambda i: (l, 0, 0))
    r3 = lambda a: a.reshape(DEPTH, 1, D_MODEL)
    return pl.pallas_call(
        _post_kernel,
        grid=(M // tm,),
        in_specs=[pl.BlockSpec((tm, D_HALF), row), pl.BlockSpec((tm, D_HALF), row),
                  pl.BlockSpec((tm, D_MODEL), row),
                  pl.BlockSpec((None, D_MODEL, D_MODEL), lambda i: (j, 0, 0)),
                  vec, vec,
                  pl.BlockSpec((None, D_MODEL, D_FF), lambda i: (l, 0, 0)),
                  pl.BlockSpec((None, D_FF, D_MODEL), lambda i: (l, 0, 0)),
                  vec, vec],
        out_specs=pl.BlockSpec((tm, D_MODEL), row),
        out_shape=jax.ShapeDtypeStruct((M, D_MODEL), F32),
        compiler_params=_params(("parallel",)),
        name="post_mlp",
    )(ya, yb, x2d, wout_all, r3(ln1_g), r3(ln1_b), w1_all, w2_all, r3(ln2_g), r3(ln2_b))


def _conv_prefill(x, xbuf, cw_ref, cb_ref, L):
    xbuf[SUBLANES:SUBLANES + L, :] = x
    y = cb_ref[...]
    for i in range(CONV_W):
        y = y + cw_ref[i:i + 1, :] * xbuf[SUBLANES - (CONV_W - 1) + i:SUBLANES - (CONV_W - 1) + i + L, :]
    tail = xbuf[L + SUBLANES - (CONV_W - 1):L + SUBLANES, :]
    xbuf[SUBLANES - (CONV_W - 1):SUBLANES, :] = tail
    return y, tail


def _lru_gates(xc, wa_ref, ba_ref, wx_ref, bx_ref, lam_ref):
    xcb = xc.astype(BF16)
    nslab = D_HALF // LANES
    ra = jnp.concatenate([jnp.dot(xcb[:, s * LANES:(s + 1) * LANES], wa_ref[s], preferred_element_type=F32)
                          for s in range(nslab)], axis=-1)
    rx = jnp.concatenate([jnp.dot(xcb[:, s * LANES:(s + 1) * LANES], wx_ref[s], preferred_element_type=F32)
                          for s in range(nslab)], axis=-1)
    r = _sigmoid(ra + ba_ref[...])
    i = _sigmoid(rx + bx_ref[...])
    log_a = -LRU_C * r * _softplus(-lam_ref[...])
    a = jnp.exp(log_a)
    t = jnp.tanh(log_a)
    u = jnp.sqrt(-2.0 * t / (1.0 - t)) * (i * xc)
    return a, u


def _head_norm(o, g, b, eps):
    mu = jnp.mean(o, -1, keepdims=True)
    oc = o - mu
    var = jnp.mean(oc * oc, -1, keepdims=True)
    return oc * lax.rsqrt(var + eps) * g + b


EV_NB = 4


class _Cols:
    def __init__(self, ref, off):
        self.ref, self.off = ref, off

    def __getitem__(self, idx):
        if idx is Ellipsis:
            return self.ref[:, self.off:self.off + D_HALF]
        rows, cols = idx
        return self.ref[rows, self.off + cols.start:self.off + cols.stop]


def _even_prefill_kernel(x_ref, w_ref,
                         cw_ref, cb_ref, wa_ref, ba_ref, wx_ref, bx_ref, lam_ref,
                         cos_ref, sin_ref, dmask_ref, qd_ref, kd_ref, cd_ref, gng_ref, gnb_ref,
                         h0_ref, conv0_ref, S0_ref,
                         ya_ref, yb_ref, h_ref, conv_ref, S_ref, xbuf, pbuf, *, L):
    @pl.when(pl.program_id(1) == 0)
    def _():
        h_ref[...] = h0_ref[...]
        S_ref[...] = S0_ref[...]
        xbuf[:, SUBLANES - (CONV_W - 1):SUBLANES, :] = conv0_ref[...]

    rows = [_even_prefill_one(x_ref.at[bi], w_ref, pbuf.at[bi],
                              cw_ref, cb_ref, wa_ref, ba_ref, wx_ref, bx_ref, lam_ref,
                              cos_ref, sin_ref, dmask_ref, qd_ref, kd_ref, cd_ref, gng_ref, gnb_ref,
                              ya_ref.at[bi], yb_ref.at[bi], h_ref.at[bi], conv_ref.at[bi], S_ref.at[bi],
                              xbuf.at[bi], L=L) for bi in range(EV_NB)]
    _round_robin(rows)


def _even_prefill_one(x_ref, w_ref, p_ref,
                      cw_ref, cb_ref, wa_ref, ba_ref, wx_ref, bx_ref, lam_ref,
                      cos_ref, sin_ref, dmask_ref, qd_ref, kd_ref, cd_ref, gng_ref, gnb_ref,
                      ya_ref, yb_ref, h_ref, conv_ref, S_ref, xbuf, *, L):
    p_ref[...] = jnp.dot(x_ref[...].astype(BF16), w_ref[...], preferred_element_type=F32)
    xa_ref, ga_ref, q_ref, k_ref, v_ref, gb_ref = (_Cols(p_ref, i * D_HALF) for i in range(6))
    yield
    H = range(RET_HEADS)
    sls = [slice(hh * RET_DH, (hh + 1) * RET_DH) for hh in H]
    cosf = cos_ref[...]
    sinf = sin_ref[...]
    qh = [_rotate(q_ref[:, sl], cosf, sinf) for sl in sls]
    kh = [_rotate(k_ref[:, sl], cosf, sinf) * (RET_DH ** -0.5) for sl in sls]
    vh = [v_ref[:, sl].astype(BF16) for sl in sls]
    qk = [_dot_nt(qh[hh], kh[hh]) for hh in H]
    S = [S_ref[hh] for hh in H]
    qS = [_dot(qh[hh] * qd_ref[:, sls[hh]], S[hh]) for hh in H]
    kv = [_dot_tn(kh[hh] * kd_ref[:, sls[hh]], vh[hh]) for hh in H]
    yield
    xc, tail = _conv_prefill(xa_ref[...], xbuf, cw_ref, cb_ref, L)
    conv_ref[...] = tail
    a, u = _lru_gates(xc, wa_ref, ba_ref, wx_ref, bx_ref, lam_ref)
    yield
    for hh in H:
        S_ref[hh] = S[hh] * cd_ref[hh] + kv[hh]
    sc = [(qk[hh] * dmask_ref[hh]).astype(BF16) for hh in H]
    o = [jnp.dot(sc[hh], vh[hh], preferred_element_type=F32) + qS[hh] for hh in H]
    yield
    row = lax.broadcasted_iota(jnp.int32, (L, D_HALF), 0) % SUBLANES
    s = 1
    while s < SUBLANES:
        keep = row >= s
        a_sh = jnp.where(keep, pltpu.roll(a, s, 0), 1.0)
        u_sh = jnp.where(keep, pltpu.roll(u, s, 0), 0.0)
        u = a * u_sh + u
        a = a * a_sh
        s *= 2
        yield
    carry = h_ref[...]
    groups = []
    for g in range(L // SUBLANES):
        rows8 = slice(g * SUBLANES, (g + 1) * SUBLANES)
        hg = a[rows8, :] * carry + u[rows8, :]
        carry = hg[SUBLANES - 1:SUBLANES, :]
        groups.append(hg)
        if g % 4 == 3:
            yield
    h = jnp.concatenate(groups, axis=0)
    h_ref[...] = carry
    ya_ref[...] = _gelu_tanh(ga_ref[...]) * h
    ones = jnp.ones((RET_DH, LANES), BF16)
    mu = [jnp.dot(o[hh].astype(BF16), ones, preferred_element_type=F32) * (1.0 / RET_DH) for hh in H]
    yield
    oc = [o[hh] - mu[hh] for hh in H]
    var = [jnp.dot((oc[hh] * oc[hh]).astype(BF16), ones, preferred_element_type=F32) * (1.0 / RET_DH) for hh in H]
    yield
    for hh in H:
        sl = sls[hh]
        on = oc[hh] * lax.rsqrt(var[hh] + GN_EPS) * gng_ref[:, sl] + gnb_ref[:, sl]
        yb_ref[:, sl] = _silu(gb_ref[:, sl]) * on


def _ret_tables(L):
    log_gamma = jnp.log1p(-jnp.exp2(-5.0 - jnp.arange(RET_HEADS, dtype=F32)))
    idx = jnp.arange(L, dtype=F32)
    diff = idx[:, None] - idx[None, :]
    dmask = jnp.where(diff >= 0, jnp.exp(log_gamma[:, None, None] * jnp.maximum(diff, 0.0)), 0.0)
    qd = jnp.exp(log_gamma[:, None] * (idx + 1.0))
    kd = jnp.exp(log_gamma[:, None] * (L - 1.0 - idx))
    cd = jnp.exp(log_gamma * L)
    qd_full = jnp.repeat(qd.T, RET_DH, axis=1)
    kd_full = jnp.repeat(kd.T, RET_DH, axis=1)
    cd_full = jnp.broadcast_to(cd[:, None, None], (RET_HEADS, 1, RET_DH))
    return dmask, qd_full, kd_full, cd_full


def _rope_tables(pos):
    half = RET_DH // 2
    inv = ROPE_BASE ** (-jnp.arange(half, dtype=F32) / half)
    ang = pos.astype(F32)[:, None] * inv[None, :]
    cos, sin = jnp.cos(ang), jnp.sin(ang)
    return jnp.concatenate([cos, cos], -1), jnp.concatenate([-sin, sin], -1)


def _lru_dense(w):
    pairs = LANES // LRU_BLOCK
    w4 = w.reshape(LRU_BLOCKS // pairs, pairs, LRU_BLOCK, LRU_BLOCK)
    eye = jnp.eye(pairs, dtype=w.dtype)
    d = w4[:, :, :, None, :] * eye[None, :, None, :, None]
    return d.reshape(LRU_BLOCKS // pairs, LANES, LANES).astype(BF16)


def _full(shape):
    n = len(shape)
    return pl.BlockSpec(shape, lambda *_: (0,) * n)


def _even_prefill(x3, w_in_all, j, pos, st, Wl):
    B, T, _ = x3.shape
    L = CHUNK
    nc = T // L
    dmask, qd, kd, cd = _ret_tables(L)
    cosf, sinf = _rope_tables(pos)
    nb = EV_NB
    perb = lambda *s: pl.BlockSpec((nb,) + s, lambda b, c: (b,) + (0,) * len(s))
    row2 = lambda a: a.reshape(1, D_HALF)
    ins = [x3, w_in_all, Wl['lru_conv_w'], row2(Wl['lru_conv_b']), Wl['lru_wa_d'], row2(Wl['lru_ba']),
           Wl['lru_wx_d'], row2(Wl['lru_bx']), row2(Wl['lru_lambda']),
           cosf, sinf, dmask, qd, kd, cd, row2(Wl['ret_gn_g']), row2(Wl['ret_gn_b']),
           st['lru_h'].reshape(B, 1, D_HALF), st['lru_conv'], st['ret_S']]
    in_specs = [pl.BlockSpec((nb, L, D_MODEL), lambda b, c: (b, c, 0)),
                pl.BlockSpec((None, D_MODEL, EVEN_IN), lambda b, c: (j, 0, 0))] + [
        _full((CONV_W, D_HALF)), _full((1, D_HALF)), _full((4, LANES, LANES)), _full((1, D_HALF)),
        _full((4, LANES, LANES)), _full((1, D_HALF)), _full((1, D_HALF)),
        pl.BlockSpec((L, RET_DH), lambda b, c: (c, 0)), pl.BlockSpec((L, RET_DH), lambda b, c: (c, 0)),
        _full((RET_HEADS, L, L)), _full((L, D_HALF)), _full((L, D_HALF)), _full((RET_HEADS, 1, RET_DH)),
        _full((1, D_HALF)), _full((1, D_HALF)),
        perb(1, D_HALF), perb(CONV_W - 1, D_HALF), perb(RET_HEADS, RET_DH, RET_DH)]
    seq = pl.BlockSpec((nb, L, D_HALF), lambda b, c: (b, c, 0))
    ya, yb, h, conv, S = pl.pallas_call(
        functools.partial(_even_prefill_kernel, L=L),
        grid=(B // nb, nc),
        in_specs=in_specs,
        out_specs=[seq, seq, perb(1, D_HALF), perb(CONV_W - 1, D_HALF), perb(RET_HEADS, RET_DH, RET_DH)],
        out_shape=[jax.ShapeDtypeStruct((B, T, D_HALF), F32), jax.ShapeDtypeStruct((B, T, D_HALF), F32),
                   jax.ShapeDtypeStruct((B, 1, D_HALF), F32),
                   jax.ShapeDtypeStruct((B, CONV_W - 1, D_HALF), F32),
                   jax.ShapeDtypeStruct((B, RET_HEADS, RET_DH, RET_DH), F32)],
        scratch_shapes=[pltpu.VMEM((nb, L + SUBLANES, D_HALF), F32), pltpu.VMEM((nb, L, EVEN_IN), F32)],
        compiler_params=_params(("parallel", "arbitrary")),
        name="even_prefill",
    )(*ins)
    return ya, yb, (h.reshape(B, D_HALF), conv, S)


DEC_BB = 8


def _even_decode_kernel(p_ref, cw_ref, cb_ref, wa_ref, ba_ref, wx_ref, bx_ref, lam_ref,
                        cos_ref, sin_ref, dm_ref, qd_ref, kd_ref, cd_ref, gng_ref, gnb_ref,
                        h0_ref, conv0_ref, S0_ref,
                        ya_ref, yb_ref, h_ref, conv_ref, S_ref):
    col = lambda i: p_ref[:, i * D_HALF:(i + 1) * D_HALF]
    xa = col(0)
    xc = cb_ref[...] + cw_ref[CONV_W - 1:CONV_W, :] * xa
    for i in range(CONV_W - 1):
        xc = xc + cw_ref[i:i + 1, :] * conv0_ref[i]
    for i in range(CONV_W - 2):
        conv_ref[i] = conv0_ref[i + 1]
    conv_ref[CONV_W - 2] = xa
    a, u = _lru_gates(xc, wa_ref, ba_ref, wx_ref, bx_ref, lam_ref)
    h = a * h0_ref[...] + u
    h_ref[...] = h
    ya_ref[...] = _gelu_tanh(col(1)) * h

    cosf = cos_ref[...]
    sinf = sin_ref[...]
    row8 = lax.broadcasted_iota(jnp.int32, (SUBLANES, RET_DH), 0)
    q, k, v, gb = col(2), col(3), col(4), col(5)
    for hh in range(RET_HEADS):
        sl = slice(hh * RET_DH, (hh + 1) * RET_DH)
        qh = _rotate(q[:, sl], cosf, sinf)
        kh = _rotate(k[:, sl], cosf, sinf) * (RET_DH ** -0.5)
        vh = v[:, sl]
        qk = jnp.sum(qh * kh, -1, keepdims=True) * dm_ref[:, sl]
        qq = qh * qd_ref[:, sl]
        kk = kh * kd_ref[:, sl]
        rows = []
        for bi in range(DEC_BB):
            S = S0_ref[bi, hh]
            q8 = jnp.broadcast_to(qq[bi:bi + 1, :], (SUBLANES, RET_DH))
            rows.append(_dot_f32(q8, S)[0:1, :])
            k8 = jnp.where(row8 == 0, jnp.broadcast_to(kk[bi:bi + 1, :], (SUBLANES, RET_DH)), 0.0)
            v8 = jnp.broadcast_to(vh[bi:bi + 1, :], (SUBLANES, RET_DH))
            S_ref[bi, hh] = S * cd_ref[hh] + _dot_tn_f32(k8, v8)
        o = qk * vh + jnp.concatenate(rows, axis=0)
        on = _head_norm(o, gng_ref[:, sl], gnb_ref[:, sl], GN_EPS)
        yb_ref[:, sl] = _silu(gb[:, sl]) * on


def _dot_f32(a, b):
    return jnp.dot(a, b, preferred_element_type=F32)


def _dot_tn_f32(a, b):
    return lax.dot_general(a, b, (((0,), (0,)), ((), ())), preferred_element_type=F32)


def _skip_ref(kernel, pos):
    def wrapped(*refs):
        kernel(*refs[:pos], *refs[pos + 1:])
    return wrapped


def _all_layers(kernel, pos_in, pos_out, j, n_layers):
    def wrapped(*refs):
        refs = list(refs)
        s_in, s_out = refs[pos_in], refs[pos_out]
        for other in range(n_layers):
            if other != j:
                s_out[other] = s_in[other]
        refs[pos_in], refs[pos_out] = s_in.at[j], s_out.at[j]
        kernel(*refs)
    return wrapped


def _stacked_state_io(kernel, S_all, S_prev, j, bb, pos_in, n_in, out_idx):
    n_layers, tail = S_all.shape[0], S_all.shape[2:]
    zeros = (0,) * len(tail)
    shape = jax.ShapeDtypeStruct(S_all.shape, F32)
    if S_prev is None:
        spec = pl.BlockSpec((n_layers, bb) + tail, lambda i: (0, i) + zeros)
        return _all_layers(kernel, pos_in, n_in + out_idx, j, n_layers), spec, [], [], {}, shape
    spec = pl.BlockSpec((None, bb) + tail, lambda i: (j, i) + zeros)
    return _skip_ref(kernel, n_in), spec, [S_prev], [pl.BlockSpec(memory_space=pl.ANY)], {n_in: out_idx}, shape


def _even_decode(p2, pos, st, Wl, S_all, j, S_prev):
    B = p2.shape[0]
    bb = DEC_BB
    dmask, qd, kd, cd = _ret_tables(1)
    dm = jnp.repeat(dmask[:, 0, :].T, RET_DH, axis=1)
    cosf, sinf = _rope_tables(pos)
    row2 = lambda a: a.reshape(1, D_HALF)
    rows = lambda n: pl.BlockSpec((bb, n), lambda i: (i, 0))
    convs = pl.BlockSpec((CONV_W - 1, bb, D_HALF), lambda i: (0, i, 0))
    ins = [p2, Wl['lru_conv_w'], row2(Wl['lru_conv_b']), Wl['lru_wa_d'], row2(Wl['lru_ba']),
           Wl['lru_wx_d'], row2(Wl['lru_bx']), row2(Wl['lru_lambda']),
           cosf, sinf, dm, qd, kd, cd, row2(Wl['ret_gn_g']), row2(Wl['ret_gn_b']),
           st['lru_h'], jnp.swapaxes(st['lru_conv'], 0, 1), S_all]
    kern, Ss, extra_in, extra_specs, aliases, S_shape = _stacked_state_io(
        _even_decode_kernel, S_all, S_prev, j, bb, len(ins) - 1, len(ins), 4)
    in_specs = [rows(EVEN_IN), _full((CONV_W, D_HALF)), _full((1, D_HALF)), _full((4, LANES, LANES)),
                _full((1, D_HALF)), _full((4, LANES, LANES)), _full((1, D_HALF)), _full((1, D_HALF)),
                _full((1, RET_DH)), _full((1, RET_DH)), _full((1, D_HALF)), _full((1, D_HALF)),
                _full((1, D_HALF)), _full((RET_HEADS, 1, RET_DH)), _full((1, D_HALF)), _full((1, D_HALF)),
                rows(D_HALF), convs, Ss] + extra_specs
    ya, yb, h, conv, S = pl.pallas_call(
        kern,
        grid=(B // bb,),
        in_specs=in_specs,
        out_specs=[rows(D_HALF), rows(D_HALF), rows(D_HALF), convs, Ss],
        out_shape=[jax.ShapeDtypeStruct((B, D_HALF), F32), jax.ShapeDtypeStruct((B, D_HALF), F32),
                   jax.ShapeDtypeStruct((B, D_HALF), F32),
                   jax.ShapeDtypeStruct((CONV_W - 1, B, D_HALF), F32), S_shape],
        input_output_aliases=aliases,
        compiler_params=_params(("parallel",)),
        name="even_decode",
    )(*ins, *extra_in)
    return ya, yb, (h, jnp.swapaxes(conv, 0, 1), S)


def _mlstm_qkv_gates(xm, xc, wq_ref, wk_ref, wv_ref, wg_ref, bg_ref):
    q = _dot(xc, wq_ref[...])
    k = _dot(xc, wk_ref[...])
    v = _dot(xm, wv_ref[...])
    g_col = (_dot(q, wg_ref[0:D_HALF, :]) + _dot(k, wg_ref[D_HALF:2 * D_HALF, :])
             + _dot(v, wg_ref[2 * D_HALF:3 * D_HALF, :]) + bg_ref[...])
    return q, k, v, g_col


ML_NB = 4


def _skewed(gens, lag=1):
    live = []
    pending = list(gens)
    tick = 0
    while pending or live:
        if pending and tick % lag == 0:
            live.append(pending.pop(0))
        tick += 1
        nxt = []
        for g in live:
            try:
                next(g)
                nxt.append(g)
            except StopIteration:
                pass
        live = nxt


def _round_robin(gens):
    gens = list(gens)
    while gens:
        alive = []
        for g in gens:
            try:
                next(g)
                alive.append(g)
            except StopIteration:
                pass
        gens = alive


def _mlstm_prefill_kernel(x_ref, w_ref, cw_ref, cb_ref, wq_ref, wk_ref, wv_ref, wg_ref, bg_ref,
                          wgt_ref, bgt_ref, gng_ref, gnb_ref, skip_ref,
                          conv0_ref, C0_ref, n0_ref, m0_ref,
                          yc_ref, conv_ref, C_ref, n_ref, m_ref, xbuf, ncol, pbuf, *, L):
    @pl.when(pl.program_id(1) == 0)
    def _():
        ones = jnp.ones((ML_DH, LANES), BF16)
        eye = jnp.where(lax.broadcasted_iota(jnp.int32, (ML_DH, LANES), 0)
                        == lax.broadcasted_iota(jnp.int32, (ML_DH, LANES), 1), 1.0, 0.0)
        C_ref[...] = C0_ref[...]
        m_ref[...] = m0_ref[...]
        for bi in range(ML_NB):
            xbuf[bi, SUBLANES - (CONV_W - 1):SUBLANES, :] = conv0_ref[bi]
            for hh in range(ML_HEADS):
                ncol[bi, hh] = _xdot(eye * n0_ref[bi, :, hh * ML_DH:(hh + 1) * ML_DH], ones)

    rows = [_mlstm_prefill_one(x_ref.at[bi], w_ref, pbuf.at[bi], cw_ref, cb_ref, wq_ref, wk_ref, wv_ref, wg_ref,
                               bg_ref, wgt_ref, bgt_ref, gng_ref, gnb_ref, skip_ref,
                               yc_ref.at[bi], conv_ref.at[bi], C_ref.at[bi], n_ref.at[bi], m_ref.at[bi],
                               xbuf.at[bi], ncol.at[bi], L=L) for bi in range(ML_NB)]
    _round_robin(rows)


def _mlstm_prefill_one(x_ref, w_ref, p_ref, cw_ref, cb_ref, wq_ref, wk_ref, wv_ref, wg_ref, bg_ref,
                       wgt_ref, bgt_ref, gng_ref, gnb_ref, skip_ref,
                       yc_ref, conv_ref, C_ref, n_ref, m_ref, xbuf, ncol, *, L):
    p_ref[...] = jnp.dot(x_ref[...].astype(BF16), w_ref[...], preferred_element_type=F32)
    xm_ref, z_ref = _Cols(p_ref, 0), _Cols(p_ref, D_HALF)
    yield
    xm = xm_ref[...]
    xc, tail = _conv_prefill(xm, xbuf, cw_ref, cb_ref, L)
    conv_ref[...] = tail
    xc = _silu(xc)
    yield
    q, k, v, g_col = _mlstm_qkv_gates(xm, xc, wq_ref, wk_ref, wv_ref, wg_ref, bg_ref)
    g_row = (_dot_nt(wgt_ref[:, 0:D_HALF], q) + _dot_nt(wgt_ref[:, D_HALF:2 * D_HALF], k)
             + _dot_nt(wgt_ref[:, 2 * D_HALF:3 * D_HALF], v) + bgt_ref[...])
    yield
    ri = lax.broadcasted_iota(jnp.int32, (L, L), 0)
    ci = lax.broadcasted_iota(jnp.int32, (L, L), 1)
    causal = ri >= ci
    tril = jnp.where(causal, 1.0, 0.0).astype(BF16)
    triu = jnp.where(ci >= ri, 1.0, 0.0).astype(BF16)
    ones = jnp.ones((L, LANES), BF16)
    eye = jnp.where(ri == ci, 1.0, 0.0)

    li_col = g_col
    lf_col = -_softplus(-pltpu.roll(g_col, LANES - ML_HEADS, 1))
    b_col = _xdot_l(tril, lf_col)
    lf_row = -_softplus(-g_row)
    b_row = _xdot(lf_row, triu)
    yield
    c_row = g_row[0:ML_HEADS, :] - b_row[ML_HEADS:2 * ML_HEADS, :]
    row = lax.broadcasted_iota(jnp.int32, (L, LANES), 0)
    pm = li_col - b_col
    sft = 1
    while sft < L:
        pm = jnp.maximum(pm, jnp.where(row >= sft, pltpu.roll(pm, sft, 0), -jnp.inf))
        sft *= 2
    m_prev = m_ref[...]
    u_col = -jnp.maximum(pm, m_prev)
    m_t_col = b_col - u_col
    e_col = jnp.exp(-m_t_col)
    m_new = m_t_col[L - 1:L, :]
    b_last = b_col[L - 1:L, :]
    wk_col = jnp.exp(b_last - b_col + li_col - m_new)
    wC_row = jnp.exp(b_last + m_prev - m_new)
    m_ref[...] = m_new
    yield
    vones = ones
    rep = lambda col, hh: jnp.broadcast_to(col[:, hh:hh + 1], (L, LANES))
    H = range(ML_HEADS)
    sls = [slice(hh * ML_DH, (hh + 1) * ML_DH) for hh in H]
    qh = [q[:, sl].astype(BF16) for sl in sls]
    kh = [k[:, sl] * (ML_DH ** -0.5) for sl in sls]
    vh1 = [jnp.concatenate([v[:, sl].astype(BF16), vones], axis=-1) for sl in sls]
    qk = [_dot_nt(qh[hh], kh[hh]) for hh in H]
    CN = [jnp.concatenate([C_ref[hh], ncol[hh]], axis=-1) for hh in H]
    qc = [jnp.dot(qh[hh], CN[hh].astype(BF16), preferred_element_type=F32) for hh in H]
    kw = [kh[hh] * rep(wk_col, hh) for hh in H]
    upd = [lax.dot_general(kw[hh].astype(BF16), vh1[hh], (((0,), (0,)), ((), ())), preferred_element_type=F32)
           for hh in H]
    yield
    for hh in H:
        w_C =jnp.broadcast_to(wC_row[:, hh:hh + 1], (ML_DH, 2 * ML_DH))
        CNn = w_C * CN[hh] + upd[hh]
        C_ref[hh] = CNn[:, 0:ML_DH]
        ncol[hh] = CNn[:, ML_DH:2 * ML_DH]
        n_ref[:, sls[hh]] = jnp.sum(CNn[:, ML_DH:2 * ML_DH] * eye, axis=0, keepdims=True)
    yield
    u = [rep(u_col, hh) for hh in H]
    s = [(qk[hh] * jnp.exp(jnp.where(causal, u[hh] + c_row[hh:hh + 1, :], -jnp.inf))).astype(BF16) for hh in H]
    sv = [jnp.dot(s[hh], vh1[hh], preferred_element_type=F32) for hh in H]
    yield
    hcell = []
    for hh in H:
        w_inter = jnp.exp(rep(m_prev, hh) + u[hh])
        num = sv[hh][:, 0:ML_DH] + w_inter * qc[hh][:, 0:ML_DH]
        den = sv[hh][:, ML_DH:2 * ML_DH] + w_inter * qc[hh][:, ML_DH:2 * ML_DH]
        hcell.append(num / jnp.maximum(jnp.abs(den), rep(e_col, hh)))
    mu = [jnp.dot(hcell[hh].astype(BF16), ones, preferred_element_type=F32) * (1.0 / ML_DH) for hh in H]
    yield
    oc = [hcell[hh] - mu[hh] for hh in H]
    var = [jnp.dot((oc[hh] * oc[hh]).astype(BF16), ones, preferred_element_type=F32) * (1.0 / ML_DH) for hh in H]
    for hh in H:
        sl = sls[hh]
        hn = oc[hh] * lax.rsqrt(var[hh] + GN_EPS) * gng_ref[:, sl] + gnb_ref[:, sl]
        yc_ref[:, sl] = (hn + skip_ref[:, sl] * xc[:, sl]) * _silu(z_ref[:, sl])


def _ml_dense(w):
    w2 = w.reshape(D_HALF, ML_QKV_BLOCK)
    c = jnp.arange(D_HALF)
    spread = (c[None, :] % ML_QKV_BLOCK == jnp.arange(ML_QKV_BLOCK)[:, None]).astype(w.dtype)
    full = jnp.dot(w2, spread, precision=lax.Precision.HIGHEST)
    same_block = c[:, None] // ML_QKV_BLOCK == c[None, :] // ML_QKV_BLOCK
    return jnp.where(same_block, full, 0.0).astype(BF16)


def _pad_lanes(a, n=LANES):
    return jnp.pad(a, [(0, 0)] * (a.ndim - 1) + [(0, n - a.shape[-1])])


def _mlstm_weights(Wl):
    wg = _pad_lanes(Wl['ml_w_gate']).astype(BF16)
    bg = _pad_lanes(Wl['ml_b_gate'].reshape(1, 2 * ML_HEADS))
    wgt = Wl['ml_w_gate'].T.astype(BF16)
    bgt = jnp.broadcast_to(Wl['ml_b_gate'].reshape(2 * ML_HEADS, 1), (2 * ML_HEADS, LANES))
    return wg, bg, wgt, bgt


def _mlstm_prefill(x3, w_in_all, j, st, Wl):
    B, T, _ = x3.shape
    L = CHUNK
    nc = T // L
    wg, bg, wgt, bgt = _mlstm_weights(Wl)
    nb = ML_NB
    perb = lambda *s: pl.BlockSpec((nb,) + s, lambda b, c: (b,) + (0,) * len(s))
    row2 = lambda a: a.reshape(1, D_HALF)
    ins = [x3, w_in_all, Wl['ml_conv_w'], row2(Wl['ml_conv_b']), Wl['ml_wq_d'], Wl['ml_wk_d'], Wl['ml_wv_d'],
           wg, bg, wgt, bgt, row2(Wl['ml_gn_g']), row2(Wl['ml_gn_b']), row2(Wl['ml_skip']),
           st['ml_conv'], st['ml_C'], st['ml_n'].reshape(B, 1, D_HALF),
           _pad_lanes(st['ml_m']).reshape(B, 1, LANES)]
    in_specs = [pl.BlockSpec((nb, L, D_MODEL), lambda b, c: (b, c, 0)),
                pl.BlockSpec((None, D_MODEL, 2 * D_HALF), lambda b, c: (j, 0, 0)),
                _full((CONV_W, D_HALF)), _full((1, D_HALF)),
                _full((D_HALF, D_HALF)), _full((D_HALF, D_HALF)), _full((D_HALF, D_HALF)),
                _full((3 * D_HALF, LANES)), _full((1, LANES)), _full((2 * ML_HEADS, 3 * D_HALF)),
                _full((2 * ML_HEADS, LANES)), _full((1, D_HALF)), _full((1, D_HALF)), _full((1, D_HALF)),
                perb(CONV_W - 1, D_HALF), perb(ML_HEADS, ML_DH, ML_DH), perb(1, D_HALF), perb(1, LANES)]
    seq = pl.BlockSpec((nb, L, D_HALF), lambda b, c: (b, c, 0))
    yc, conv, C, n, m = pl.pallas_call(
        functools.partial(_mlstm_prefill_kernel, L=L),
        grid=(B // nb, nc),
        in_specs=in_specs,
        out_specs=[seq, perb(CONV_W - 1, D_HALF), perb(ML_HEADS, ML_DH, ML_DH), perb(1, D_HALF), perb(1, LANES)],
        out_shape=[jax.ShapeDtypeStruct((B, T, D_HALF), F32),
                   jax.ShapeDtypeStruct((B, CONV_W - 1, D_HALF), F32),
                   jax.ShapeDtypeStruct((B, ML_HEADS, ML_DH, ML_DH), F32),
                   jax.ShapeDtypeStruct((B, 1, D_HALF), F32),
                   jax.ShapeDtypeStruct((B, 1, LANES), F32)],
        scratch_shapes=[pltpu.VMEM((nb, L + SUBLANES, D_HALF), F32),
                        pltpu.VMEM((nb, ML_HEADS, ML_DH, LANES), F32),
                        pltpu.VMEM((nb, L, 2 * D_HALF), F32)],
        compiler_params=_params(("parallel", "arbitrary")),
        name="mlstm_prefill",
    )(*ins)
    return yc, (C, n.reshape(B, ML_HEADS, ML_DH), m[:, 0, :ML_HEADS], conv)


def _mlstm_decode_kernel(p_ref, cw_ref, cb_ref, wq_ref, wk_ref, wv_ref, wg_ref, bg_ref,
                         gng_ref, gnb_ref, skip_ref, conv0_ref, C0_ref, n0_ref, m0_ref,
                         yc_ref, conv_ref, C_ref, n_ref, m_ref):
    xm = p_ref[:, 0:D_HALF]
    z = p_ref[:, D_HALF:2 * D_HALF]
    xc = cb_ref[...] + cw_ref[CONV_W - 1:CONV_W, :] * xm
    for i in range(CONV_W - 1):
        xc = xc + cw_ref[i:i + 1, :] * conv0_ref[i]
    for i in range(CONV_W - 2):
        conv_ref[i] = conv0_ref[i + 1]
    conv_ref[CONV_W - 2] = xm
    xc = _silu(xc)
    q, k, v, g = _mlstm_qkv_gates(xm, xc, wq_ref, wk_ref, wv_ref, wg_ref, bg_ref)
    lf_all = -_softplus(-g)
    lane = lax.broadcasted_iota(jnp.int32, (1, LANES), 1)
    row8 = lax.broadcasted_iota(jnp.int32, (SUBLANES, ML_DH), 0)
    m_all = m0_ref[...]
    m_out = m_all
    for hh in range(ML_HEADS):
        sl = slice(hh * ML_DH, (hh + 1) * ML_DH)
        qh, vh = q[:, sl], v[:, sl]
        kh = k[:, sl] * (ML_DH ** -0.5)
        li = g[:, hh:hh + 1]
        lf = lf_all[:, ML_HEADS + hh:ML_HEADS + hh + 1]
        m_prev = m_all[:, hh:hh + 1]
        n = n0_ref[:, sl]
        log_inter = lf + m_prev
        m_t = jnp.maximum(li, log_inter)
        s = jnp.sum(qh * kh, -1, keepdims=True) * jnp.exp(li - m_t)
        w_inter = jnp.exp(log_inter - m_t)
        w_k = jnp.exp(li - m_t)
        w_C = jnp.exp(log_inter - m_t)
        kw = kh * w_k
        rows = []
        for bi in range(DEC_BB):
            C = C0_ref[bi, hh]
            q8 = jnp.broadcast_to(qh[bi:bi + 1, :], (SUBLANES, ML_DH))
            rows.append(_dot_f32(q8, C)[0:1, :])
            k8 = jnp.where(row8 == 0, jnp.broadcast_to(kw[bi:bi + 1, :], (SUBLANES, ML_DH)), 0.0)
            v8 = jnp.broadcast_to(vh[bi:bi + 1, :], (SUBLANES, ML_DH))
            C_ref[bi, hh] = w_C[bi:bi + 1, :] * C + _dot_tn_f32(k8, v8)
        qC = jnp.concatenate(rows, axis=0)
        num = s * vh + w_inter * qC
        den = s + w_inter * jnp.sum(qh * n, -1, keepdims=True)
        hcell = num / jnp.maximum(jnp.abs(den), jnp.exp(-m_t))
        n_ref[:, sl] = w_C * n + kw
        m_out = jnp.where(lane == hh, m_t, m_out)
        hn = _head_norm(hcell, gng_ref[:, sl], gnb_ref[:, sl], GN_EPS)
        yc_ref[:, sl] = (hn + skip_ref[:, sl] * xc[:, sl]) * _silu(z[:, sl])
    m_ref[...] = m_out


def _mlstm_decode(p2, st, Wl, C_all, j, C_prev):
    B = p2.shape[0]
    bb = DEC_BB
    wg, bg, _, _ = _mlstm_weights(Wl)
    row2 = lambda a: a.reshape(1, D_HALF)
    rows = lambda n: pl.BlockSpec((bb, n), lambda i: (i, 0))
    convs = pl.BlockSpec((CONV_W - 1, bb, D_HALF), lambda i: (0, i, 0))
    ins = [p2, Wl['ml_conv_w'], row2(Wl['ml_conv_b']), Wl['ml_wq_d'], Wl['ml_wk_d'], Wl['ml_wv_d'], wg, bg,
           row2(Wl['ml_gn_g']), row2(Wl['ml_gn_b']), row2(Wl['ml_skip']),
           jnp.swapaxes(st['ml_conv'], 0, 1), C_all, st['ml_n'].reshape(B, D_HALF), _pad_lanes(st['ml_m'])]
    kern, Cs, extra_in, extra_specs, aliases, C_shape = _stacked_state_io(
        _mlstm_decode_kernel, C_all, C_prev, j, bb, len(ins) - 3, len(ins), 2)
    in_specs = [pl.BlockSpec((bb, 2 * D_HALF), lambda i: (i, 0)), _full((CONV_W, D_HALF)), _full((1, D_HALF)),
                _full((D_HALF, D_HALF)), _full((D_HALF, D_HALF)), _full((D_HALF, D_HALF)),
                _full((3 * D_HALF, LANES)), _full((1, LANES)),
                _full((1, D_HALF)), _full((1, D_HALF)), _full((1, D_HALF)),
                convs, Cs, rows(D_HALF), rows(LANES)] + extra_specs
    yc, conv, C, n, m = pl.pallas_call(
        kern,
        grid=(B // bb,),
        in_specs=in_specs,
        out_specs=[rows(D_HALF), convs, Cs, rows(D_HALF), rows(LANES)],
        out_shape=[jax.ShapeDtypeStruct((B, D_HALF), F32),
                   jax.ShapeDtypeStruct((CONV_W - 1, B, D_HALF), F32),
                   C_shape,
                   jax.ShapeDtypeStruct((B, D_HALF), F32),
                   jax.ShapeDtypeStruct((B, LANES), F32)],
        input_output_aliases=aliases,
        compiler_params=_params(("parallel",)),
        name="mlstm_decode",
    )(*ins, *extra_in)
    return yc, (C, n.reshape(B, ML_HEADS, ML_DH), m[:, :ML_HEADS], jnp.swapaxes(conv, 0, 1))


def _rwkv_pre_body(pr, pr_prev, mu_ref, w0_ref, a0_ref, w2_ref, a2_ref, g2_ref, kkw_ref, kaw_ref, rk_ref,
                   r_ref, d_ref, k_ref, v_ref, a_ref, b_ref, g_ref, bonus_ref):
    pm = pr + (pr_prev - pr) * mu_ref[...]
    r = pm[:, 0:D_HALF]
    kr = pm[:, D_HALF:2 * D_HALF]
    vr = pm[:, 2 * D_HALF:3 * D_HALF]
    lo = pm[:, 3 * D_HALF:RW_COLS_PAD]
    w_log = -_softplus(-(w0_ref[...] + _dot(jnp.tanh(lo), w2_ref[...]))) - 0.5
    a = _sigmoid(a0_ref[...] + _dot(lo, a2_ref[...]))
    g = _dot(_sigmoid(lo), g2_ref[...])
    ones_bd = _group_ones(LANES, RW_DH)
    kk = kr * kkw_ref[...]
    kk = kk / jnp.maximum(jnp.sqrt(_group_sum(kk * kk, ones_bd)), 1e-12)
    kh = kr * (1.0 + (a - 1.0) * kaw_ref[...])
    r_ref[...] = r
    d_ref[...] = jnp.exp(-jnp.exp(w_log))
    k_ref[...] = kh
    v_ref[...] = vr
    a_ref[...] = -kk
    b_ref[...] = kk * a
    g_ref[...] = g
    bonus_ref[...] = _group_sum(r * kh * rk_ref[...], ones_bd) * vr


def _rwkv_pre_decode_kernel(pr_ref, prev_ref, *rest):
    _rwkv_pre_body(pr_ref[...], prev_ref[...], *rest[:9], *rest[9:17])


def _rwkv_pre_weights(Wl):
    row2 = lambda a: a.reshape(1, D_HALF)
    padr = lambda w, o: jnp.pad(w, ((o, RW_LORA_PAD - o - w.shape[0]), (0, 0))).astype(BF16)
    mu = _pad_lanes(Wl['rw_mu'].reshape(1, RW_SHIFT_COLS), RW_COLS_PAD)
    ws = [mu, row2(Wl['rw_w0']), row2(Wl['rw_a0']),
          padr(Wl['rw_w2'], 0), padr(Wl['rw_a2'], RW_DECAY_LORA), padr(Wl['rw_g2'], RW_DECAY_LORA + RW_A_LORA),
          row2(Wl['rw_kk']), row2(Wl['rw_ka']), row2(Wl['rw_rk'])]
    specs = [_full((1, RW_COLS_PAD)), _full((1, D_HALF)), _full((1, D_HALF)),
             _full((RW_LORA_PAD, D_HALF)), _full((RW_LORA_PAD, D_HALF)), _full((RW_LORA_PAD, D_HALF)),
             _full((1, D_HALF)), _full((1, D_HALF)), _full((1, D_HALF))]
    return ws, specs


def _rwkv_pre_decode(pr, shift0, Wl):
    B = pr.shape[0]
    ws, wspecs = _rwkv_pre_weights(Wl)
    full2 = lambda n: pl.BlockSpec((B, n), lambda i: (0, 0))
    outs = pl.pallas_call(
        _rwkv_pre_decode_kernel,
        grid=(1,),
        in_specs=[full2(RW_COLS_PAD), full2(RW_COLS_PAD)] + wspecs,
        out_specs=[full2(D_HALF)] * 8,
        out_shape=[jax.ShapeDtypeStruct((B, D_HALF), F32)] * 8,
        compiler_params=_params(("arbitrary",)),
        name="rwkv_pre_decode",
    )(pr, _pad_lanes(shift0, RW_COLS_PAD), *ws)
    return outs


RW_IP = RW_DH // 2
RW_TC = 64
RW_UNROLL = 8


def _rwkv_rec_kernel(r_ref, d_ref, k_ref, a_ref, b_ref, v_ref, S0_ref, y_ref, S_ref, *, Tc):
    @pl.when(pl.program_id(0) == 0)
    def _():
        S_ref[...] = S0_ref[...]

    lane = lax.broadcasted_iota(jnp.int32, (1, LANES), 1)

    def tiles(t):
        back = (LANES - (t % RW_TB) * RW_BB) % LANES
        out = []
        for ref in (a_ref, d_ref, b_ref, k_ref, r_ref):
            raw = ref[t]
            out.append(jnp.where(lane < LANES // 2, pltpu.roll(raw, back, 1),
                                 pltpu.roll(raw, (back + LANES // 2) % LANES, 1)))
        out.append(pltpu.roll(v_ref[t], back, 1))
        return tuple(out)

    def step(t, carry):
        a, d, b, k, r, vt = carry
        nxt = tiles(jnp.minimum(t + 1, Tc - 1))
        rows = []
        for ip in range(RW_IP):
            S = S_ref[ip]
            sa = jnp.sum(S * a, axis=0, keepdims=True)
            Sn = S * d + sa * b + vt[ip:ip + 1, :] * k
            S_ref[ip] = Sn
            rows.append(jnp.sum(Sn * r, axis=0, keepdims=True))
        y_ref[t] = jnp.concatenate(rows, axis=0)
        return nxt

    lax.fori_loop(0, Tc, step, tiles(0), unroll=RW_UNROLL)


def _rwkv_rec_call(r, d, k, a, b, v, S0):
    T = r.shape[0]
    Tc = min(T, RW_TC)
    vec = pl.BlockSpec((Tc, RW_DH, LANES), lambda c: (c, 0, 0))
    vsp = pl.BlockSpec((Tc, RW_IP, LANES), lambda c: (c, 0, 0))
    ssp = _full((RW_IP, RW_DH, LANES))
    return pl.pallas_call(
        functools.partial(_rwkv_rec_kernel, Tc=Tc),
        grid=(T // Tc,),
        in_specs=[vec] * 5 + [vsp, ssp],
        out_specs=[vsp, ssp],
        out_shape=[jax.ShapeDtypeStruct((T, RW_IP, LANES), F32),
                   jax.ShapeDtypeStruct((RW_IP, RW_DH, LANES), F32)],
        compiler_params=_params(("arbitrary",)),
        name="rwkv_recurrence",
    )(r, d, k, a, b, v, S0)


def _rwkv_dec_kernel(r_ref, d_ref, k_ref, a_ref, b_ref, v_ref, S0_ref, y_ref, S_ref):
    a, d, b, k, r = a_ref[...], d_ref[...], b_ref[...], k_ref[...], r_ref[...]
    v = v_ref[...]
    rows = []
    for i in range(RW_DH):
        S = S0_ref[i]
        sa = jnp.sum(S * a, axis=0, keepdims=True)
        Sn = S * d + sa * b + v[i:i + 1, :] * k
        S_ref[i] = Sn
        rows.append(jnp.sum(Sn * r, axis=0, keepdims=True))
    y_ref[...] = jnp.concatenate(rows, axis=0)


def _rwkv_decode_step(r, d, k, v, a, b, S0):
    B = r.shape[0]
    tr = lambda x: x.T.reshape(RW_HEADS, RW_DH, B)
    St = S0.reshape(B, RW_HEADS * RW_DH * RW_DH).T.reshape(RW_HEADS, RW_DH, RW_DH, B)
    vec = pl.BlockSpec((None, RW_DH, B), lambda h: (h, 0, 0))
    ssp = pl.BlockSpec((None, RW_DH, RW_DH, B), lambda h: (h, 0, 0, 0))
    y, S = pl.pallas_call(
        _rwkv_dec_kernel,
        grid=(RW_HEADS,),
        in_specs=[vec] * 6 + [ssp],
        out_specs=[vec, ssp],
        out_shape=[jax.ShapeDtypeStruct((RW_HEADS, RW_DH, B), F32),
                   jax.ShapeDtypeStruct((RW_HEADS, RW_DH, RW_DH, B), F32)],
        compiler_params=_params(("parallel",)),
        name="rwkv_decode_step",
    )(tr(r), tr(d), tr(k), tr(a), tr(b), tr(v), St)
    y = y.reshape(D_HALF, B).T
    S = S.reshape(RW_HEADS * RW_DH * RW_DH, B).T.reshape(B, RW_HEADS, RW_DH, RW_DH)
    return y, S


RW_TB = LANES // RW_BB
RW_NSB = 2
RW_MM_ROWS = 256


def _head_sum_rows(x):
    x3 = x.reshape(RW_HEADS, RW_DH, x.shape[-1])
    s = jnp.sum(x3, axis=1, keepdims=True)
    return jnp.broadcast_to(s, x3.shape).reshape(x.shape)


def _rwkv_pre_t_kernel(x_ref, shift0_ref, w_ref, mu_ref, w0_ref, a0_ref, w2_ref, a2_ref, g2_ref,
                       kkw_ref, kaw_ref, rk_ref,
                       r_ref, d_ref, k_ref, a_ref, b_ref, v_ref, g_ref, bonus_ref, last_ref, prev_scr):
    @pl.when(pl.program_id(0) == 0)
    def _():
        prev_scr[...] = shift0_ref[...]

    ro = lax.broadcasted_iota(jnp.int32, (LANES, LANES), 0)
    ci = lax.broadcasted_iota(jnp.int32, (LANES, LANES), 1)
    perm = jnp.where(ci == (ro % RW_BB) * RW_TB + ro // RW_BB, 1.0, 0.0).astype(BF16)
    lane = lax.broadcasted_iota(jnp.int32, (1, LANES), 1)
    grp = lane // RW_BB
    ngrp = LANES // RW_BB
    prs = {}

    def scatter(x, o_ref, t0, nrow, npiece):
        rot = [x[q * nrow:(q + 1) * nrow, :] if q == 0 else pltpu.roll(x[q * nrow:(q + 1) * nrow, :], q * RW_BB, 1)
               for q in range(npiece)]
        for t in range(RW_TB):
            m = rot[0]
            for q in range(1, npiece):
                m = jnp.where(grp == (t + q) % ngrp, rot[q], m)
            o_ref[t0 + t] = m

    def block(sb):
        t0 = sb * RW_TB
        xn = x_ref[:, t0:t0 + RW_TB, :].reshape(RW_BB * RW_TB, D_MODEL).astype(BF16)
        xg = jnp.dot(perm, xn, preferred_element_type=F32).astype(BF16)
        parts = []
        for r0 in range(0, RW_COLS_PAD, RW_MM_ROWS):
            parts.append(lax.dot_general(w_ref[r0:r0 + RW_MM_ROWS, :], xg, (((1,), (1,)), ((), ())),
                                         preferred_element_type=F32))
            yield
        pr = jnp.concatenate(parts, axis=0)
        prs[sb] = pr
        rolled = pltpu.roll(pr, RW_BB, 1)
        before = prev_scr[...] if sb == 0 else prs[sb - 1]
        prev = jnp.where(lane < RW_BB, before, rolled)
        prs[sb] = rolled
        if sb == RW_NSB - 1:
            prev_scr[...] = rolled
            last_ref[...] = pr
        pm = pr + (prev - pr) * mu_ref[...]
        r = pm[0:D_HALF]
        kr = pm[D_HALF:2 * D_HALF]
        vr = pm[2 * D_HALF:3 * D_HALF]
        lo = pm[3 * D_HALF:RW_COLS_PAD]
        yield
        w_log = -_softplus(-(w0_ref[...] + _dot(w2_ref[...], jnp.tanh(lo)))) - 0.5
        a = _sigmoid(a0_ref[...] + _dot(a2_ref[...], lo))
        g = _dot(g2_ref[...], _sigmoid(lo))
        kk = kr * kkw_ref[...]
        kk = kk / jnp.maximum(jnp.sqrt(_head_sum_rows(kk * kk)), 1e-12)
        kh = kr * (1.0 + (a - 1.0) * kaw_ref[...])
        g_ref[sb] = g
        bonus_ref[sb] = _head_sum_rows(r * kh * rk_ref[...]) * vr
        yield
        scatter(r, r_ref, t0, RW_DH, RW_HEADS)
        yield
        scatter(jnp.exp(-jnp.exp(w_log)), d_ref, t0, RW_DH, RW_HEADS)
        yield
        scatter(kh, k_ref, t0, RW_DH, RW_HEADS)
        yield
        scatter(-kk, a_ref, t0, RW_DH, RW_HEADS)
        yield
        scatter(kk * a, b_ref, t0, RW_DH, RW_HEADS)
        yield
        vv = jnp.concatenate([vr[h * RW_DH + half * RW_IP:h * RW_DH + (half + 1) * RW_IP, :]
                              for half in range(2) for h in range(RW_HEADS)], axis=0)
        scatter(vv, v_ref, t0, RW_IP, ngrp)

    _skewed([block(sb) for sb in range(RW_NSB)], lag=RW_COLS_PAD // RW_MM_ROWS)


def _lane_bcast(a, n):
    return jnp.broadcast_to(a.reshape(n, 1), (n, LANES))


def _rwkv_pre_t(x3, shift0, w_rwt, j, Wl):
    B, T, _ = x3.shape
    nblk = T // RW_TB
    padr = lambda w, o: jnp.pad(w, ((o, RW_LORA_PAD - o - w.shape[0]), (0, 0))).astype(BF16).T
    sh = jnp.pad(shift0.T, ((0, RW_COLS_PAD - RW_SHIFT_COLS), (0, LANES - RW_BB)))
    col = lambda a: _lane_bcast(a, D_HALF)
    ins = [x3, sh, w_rwt, _lane_bcast(_pad_lanes(Wl['rw_mu'].reshape(1, -1), RW_COLS_PAD), RW_COLS_PAD),
           col(Wl['rw_w0']), col(Wl['rw_a0']),
           padr(Wl['rw_w2'], 0), padr(Wl['rw_a2'], RW_DECAY_LORA), padr(Wl['rw_g2'], RW_DECAY_LORA + RW_A_LORA),
           col(Wl['rw_kk']), col(Wl['rw_ka']), col(Wl['rw_rk'])]
    tb = RW_TB * RW_NSB
    in_specs = [pl.BlockSpec((B, tb, D_MODEL), lambda c: (0, c, 0)), _full((RW_COLS_PAD, LANES)),
                pl.BlockSpec((None, RW_COLS_PAD, D_MODEL), lambda c: (j, 0, 0)), _full((RW_COLS_PAD, LANES)),
                _full((D_HALF, LANES)), _full((D_HALF, LANES)),
                _full((D_HALF, RW_LORA_PAD)), _full((D_HALF, RW_LORA_PAD)), _full((D_HALF, RW_LORA_PAD)),
                _full((D_HALF, LANES)), _full((D_HALF, LANES)), _full((D_HALF, LANES))]
    blk = pl.BlockSpec((RW_NSB, D_HALF, LANES), lambda c: (c, 0, 0))
    ktile = pl.BlockSpec((tb, RW_DH, LANES), lambda c: (c, 0, 0))
    vtile = pl.BlockSpec((tb, RW_IP, LANES), lambda c: (c, 0, 0))
    outs = pl.pallas_call(
        _rwkv_pre_t_kernel,
        grid=(T // tb,),
        in_specs=in_specs,
        out_specs=[ktile] * 5 + [vtile, blk, blk, _full((RW_COLS_PAD, LANES))],
        out_shape=[jax.ShapeDtypeStruct((T, RW_DH, LANES), F32)] * 5
                  + [jax.ShapeDtypeStruct((T, RW_IP, LANES), F32)]
                  + [jax.ShapeDtypeStruct((nblk, D_HALF, LANES), F32)] * 2
                  + [jax.ShapeDtypeStruct((RW_COLS_PAD, LANES), F32)],
        scratch_shapes=[pltpu.VMEM((RW_COLS_PAD, LANES), F32)],
        compiler_params=_params(("arbitrary",)),
        name="rwkv_pre_t",
    )(*ins)
    shift_new = outs[8][:RW_SHIFT_COLS, LANES - RW_BB:].T
    return outs[:8], shift_new


RW_NPB = 8


def _rwkv_post_t_kernel(y_ref, g_ref, bonus_ref, gng_ref, gnb_ref, yd_ref):
    _round_robin([_rwkv_post_t_block(y_ref, g_ref, bonus_ref, gng_ref, gnb_ref, yd_ref, pb)
                  for pb in range(RW_NPB)])


def _rwkv_post_t_block(y_ref, g_ref, bonus_ref, gng_ref, gnb_ref, yd_ref, pb):
    t0 = pb * RW_TB
    lane = lax.broadcasted_iota(jnp.int32, (1, LANES), 1)
    grp = lane // RW_BB
    ngrp = LANES // RW_BB
    ys = [y_ref[t0 + t] for t in range(RW_TB)]
    rolled = []
    for s in range(ngrp):
        m = ys[s % RW_TB]
        for q in range(1, ngrp):
            m = jnp.where(grp == q, ys[(q + s) % RW_TB], m)
        rolled.append(pltpu.roll(m, s * RW_BB, 1) if s else m)
    yield
    pieces = {}
    for q in range(ngrp):
        m = rolled[(-q) % ngrp]
        for t in range(1, RW_TB):
            m = jnp.where(grp == t, rolled[(t - q) % ngrp], m)
        pieces[divmod(q, RW_HEADS)] = m
    y = jnp.concatenate([pieces[(half, h)] for h in range(RW_HEADS) for half in range(2)], axis=0)
    yield
    mu = _head_sum_rows(y) * (1.0 / RW_DH)
    yc = y - mu
    var = _head_sum_rows(yc * yc) * (1.0 / RW_DH)
    hn = yc * lax.rsqrt(var + RW_GN_EPS) * gng_ref[...] + gnb_ref[...]
    yd = ((hn + bonus_ref[pb]) * g_ref[pb]).T
    yield
    for t in range(RW_TB):
        yd_ref[:, t0 + t, :] = yd[t * RW_BB:(t + 1) * RW_BB, :]


def _rwkv_rec_t(vecs, S0, Wl, T):
    r, d, k, a, b, v, g, bonus = vecs
    nblk = T // RW_TB
    B = RW_BB
    Sr = S0.reshape(B, RW_HEADS, 2, RW_IP, RW_DH).transpose(3, 4, 2, 1, 0).reshape(RW_IP, RW_DH, LANES)
    y, S = _rwkv_rec_call(r, d, k, a, b, v, Sr)
    tb = RW_TB * RW_NPB
    blk = pl.BlockSpec((RW_NPB, D_HALF, LANES), lambda c: (c, 0, 0))
    col = lambda a_: _lane_bcast(a_, D_HALF)
    yd = pl.pallas_call(
        _rwkv_post_t_kernel,
        grid=(nblk // RW_NPB,),
        in_specs=[pl.BlockSpec((tb, RW_IP, LANES), lambda c: (c, 0, 0)), blk, blk,
                  _full((D_HALF, LANES)), _full((D_HALF, LANES))],
        out_specs=pl.BlockSpec((B, tb, D_HALF), lambda c: (0, c, 0)),
        out_shape=jax.ShapeDtypeStruct((B, T, D_HALF), F32),
        compiler_params=_params(("parallel",)),
        name="rwkv_post_t",
    )(y, g, bonus, col(Wl['rw_gn_g']), col(Wl['rw_gn_b']))
    S = S.reshape(RW_IP, RW_DH, 2, RW_HEADS, B).transpose(4, 3, 2, 0, 1).reshape(B, RW_HEADS, RW_DH, RW_DH)
    return yd, S


def _rwkv_post_kernel(y_ref, g_ref, bonus_ref, gng_ref, gnb_ref, o_ref):
    ones_bd = _group_ones(LANES, RW_DH)
    y = y_ref[...]
    mu = _group_sum(y, ones_bd) * (1.0 / RW_DH)
    yc = y - mu
    var = _group_sum(yc * yc, ones_bd) * (1.0 / RW_DH)
    hn = yc * lax.rsqrt(var + RW_GN_EPS) * gng_ref[...] + gnb_ref[...]
    o_ref[...] = (hn + bonus_ref[...]) * g_ref[...]


def _rwkv_post(y2, g2, bonus2, Wl, tm):
    M = y2.shape[0]
    row = pl.BlockSpec((tm, D_HALF), lambda i: (i, 0))
    return pl.pallas_call(
        _rwkv_post_kernel,
        grid=(M // tm,),
        in_specs=[row, row, row, _full((1, D_HALF)), _full((1, D_HALF))],
        out_specs=row,
        out_shape=jax.ShapeDtypeStruct((M, D_HALF), F32),
        compiler_params=_params(("parallel",)),
        name="rwkv_post",
    )(y2, g2, bonus2, Wl['rw_gn_g'].reshape(1, D_HALF), Wl['rw_gn_b'].reshape(1, D_HALF))


def _trunk(x, st, pos, W):
    B, T, _ = x.shape
    M = B * T
    decode = T == 1
    tm_proj = min(M, 512)
    tm_post = min(M, 2 * POST_SUB)
    tm_rw = min(M, 512)
    x2 = x.reshape(M, D_MODEL)
    new = {name: [] for name in st}
    stacked = {}
    for l in range(DEPTH):
        j = l // 2
        Wl = {name: v[j] for name, v in W['per_pair'][l % 2].items()}
        stl = {name: v[j] for name, v in st.items()}
        if l % 2 == 0:
            if decode:
                p, = _proj(x2, W['ev_w_in'], j, tm_proj, (EVEN_IN,))
                ya, yb, (h, cb, S) = _even_decode(p, pos, stl, Wl, st['ret_S'], j, stacked.get('ret_S'))
                stacked['ret_S'] = S
            else:
                ya, yb, (h, cb, S) = _even_prefill(x2.reshape(B, T, D_MODEL), W['ev_w_in'], j, pos, stl, Wl)
                new['ret_S'].append(S)
            new['lru_h'].append(h)
            new['lru_conv'].append(cb)
            wout = W['ev_w_out']
        else:
            if decode:
                p_ml, p_rw = _proj(x2, W['od_w_in'], j, tm_proj, (2 * D_HALF, RW_COLS_PAD))
                ya, (C, n, m, cb) = _mlstm_decode(p_ml, stl, Wl, st['ml_C'], j, stacked.get('ml_C'))
                stacked['ml_C'] = C
                r, d, k, v, a, b, g, bonus = _rwkv_pre_decode(p_rw, stl['rw_shift'], Wl)
                y, S = _rwkv_decode_step(r, d, k, v, a, b, stl['rw_S'])
                yb = _rwkv_post(y, g, bonus, Wl, tm_rw)
                shift_new = p_rw[:, :RW_SHIFT_COLS]
            else:
                x3 = x2.reshape(B, T, D_MODEL)
                ya, (C, n, m, cb) = _mlstm_prefill(x3, W['od_w_in'], j, stl, Wl)
                vecs, shift_new = _rwkv_pre_t(x3, stl['rw_shift'], W['od_w_rwt'], j, Wl)
                yb, S = _rwkv_rec_t(vecs, stl['rw_S'], Wl, T)
                new['ml_C'].append(C)
            new['ml_n'].append(n)
            new['ml_m'].append(m)
            new['ml_conv'].append(cb)
            new['rw_S'].append(S)
            new['rw_shift'].append(shift_new)
            wout = W['od_w_out']
        x2 = _post(ya.reshape(M, D_HALF), yb.reshape(M, D_HALF), x2, wout, j, l,
                   W['ln1_g'], W['ln1_b'], W['mlp_w1'], W['mlp_w2'], W['ln2_g'], W['ln2_b'], tm_post)
    out = {name: stacked[name] if name in stacked else jnp.stack(v) for name, v in new.items()}
    return x2.reshape(B, T, D_MODEL), out


def _prepare_weights(w):
    even_names = ('lru_conv_w', 'lru_conv_b', 'lru_ba', 'lru_bx', 'lru_lambda', 'ret_gn_g', 'ret_gn_b')
    odd_names = ('ml_conv_w', 'ml_conv_b', 'ml_w_gate', 'ml_b_gate', 'ml_gn_g', 'ml_gn_b', 'ml_skip',
                 'rw_mu', 'rw_w0', 'rw_w2', 'rw_a0', 'rw_a2', 'rw_g2', 'rw_kk', 'rw_ka', 'rw_rk', 'rw_gn_g', 'rw_gn_b')
    even = {n: w[n] for n in even_names}
    even.update(lru_wa_d=jax.vmap(_lru_dense)(w['lru_wa']), lru_wx_d=jax.vmap(_lru_dense)(w['lru_wx']))
    odd = {n: w[n] for n in odd_names}
    odd.update(ml_wq_d=jax.vmap(_ml_dense)(w['ml_wq']), ml_wk_d=jax.vmap(_ml_dense)(w['ml_wk']),
               ml_wv_d=jax.vmap(_ml_dense)(w['ml_wv']))
    od_w_in = _pad_lanes(w['od_w_in'], ODD_IN_PAD).astype(BF16)
    return dict(per_pair=(even, odd),
                ev_w_in=w['ev_w_in'].astype(BF16), ev_w_out=w['ev_w_out'].astype(BF16),
                od_w_in=od_w_in, od_w_rwt=jnp.swapaxes(od_w_in[:, :, 2 * D_HALF:], 1, 2),
                od_w_out=w['od_w_out'].astype(BF16),
                mlp_w1=w['mlp_w1'].astype(BF16), mlp_w2=w['mlp_w2'].astype(BF16),
                ln1_g=w['ln1_g'], ln1_b=w['ln1_b'], ln2_g=w['ln2_g'], ln2_b=w['ln2_b'])


def _zero_states(batch):
    z = lambda *s: jnp.zeros(s, F32)
    n_even, n_odd = (DEPTH + 1) // 2, DEPTH // 2
    return dict(lru_h=z(n_even, batch, D_HALF), lru_conv=z(n_even, batch, CONV_W - 1, D_HALF),
                ret_S=z(n_even, batch, RET_HEADS, RET_DH, RET_DH),
                ml_C=z(n_odd, batch, ML_HEADS, ML_DH, ML_DH), ml_n=z(n_odd, batch, ML_HEADS, ML_DH),
                ml_m=z(n_odd, batch, ML_HEADS), ml_conv=z(n_odd, batch, CONV_W - 1, D_HALF),
                rw_S=z(n_odd, batch, RW_HEADS, RW_DH, RW_DH), rw_shift=z(n_odd, batch, RW_SHIFT_COLS))


def kernel(x_prompt, x_sample, state_lru_h, state_lru_conv, state_ret, state_mlstm_C, state_mlstm_n, state_mlstm_m, state_mlstm_conv, state_rwkv_S, state_rwkv_shift, ln1_g, ln1_b, ln2_g, ln2_b, mlp_w1, mlp_w2, ev_w_in, ev_w_out, lru_conv_w, lru_conv_b, lru_wa, lru_ba, lru_wx, lru_bx, lru_lambda, ret_gn_g, ret_gn_b, od_w_in, od_w_out, ml_conv_w, ml_conv_b, ml_wq, ml_wk, ml_wv, ml_w_gate, ml_b_gate, ml_gn_g, ml_gn_b, ml_skip, rw_mu, rw_w0, rw_w2, rw_a0, rw_a2, rw_g2, rw_kk, rw_ka, rw_rk, rw_gn_g, rw_gn_b):
    W = _prepare_weights(dict(
        ln1_g=ln1_g, ln1_b=ln1_b, ln2_g=ln2_g, ln2_b=ln2_b, mlp_w1=mlp_w1, mlp_w2=mlp_w2,
        ev_w_in=ev_w_in, ev_w_out=ev_w_out, lru_conv_w=lru_conv_w, lru_conv_b=lru_conv_b,
        lru_wa=lru_wa, lru_ba=lru_ba, lru_wx=lru_wx, lru_bx=lru_bx, lru_lambda=lru_lambda,
        ret_gn_g=ret_gn_g, ret_gn_b=ret_gn_b, od_w_in=od_w_in, od_w_out=od_w_out,
        ml_conv_w=ml_conv_w, ml_conv_b=ml_conv_b, ml_wq=ml_wq, ml_wk=ml_wk, ml_wv=ml_wv,
        ml_w_gate=ml_w_gate, ml_b_gate=ml_b_gate, ml_gn_g=ml_gn_g, ml_gn_b=ml_gn_b, ml_skip=ml_skip,
        rw_mu=rw_mu, rw_w0=rw_w0, rw_w2=rw_w2, rw_a0=rw_a0, rw_a2=rw_a2, rw_g2=rw_g2,
        rw_kk=rw_kk, rw_ka=rw_ka, rw_rk=rw_rk, rw_gn_g=rw_gn_g, rw_gn_b=rw_gn_b))
    st_sample = dict(lru_h=state_lru_h, lru_conv=state_lru_conv, ret_S=state_ret,
                     ml_C=state_mlstm_C, ml_n=state_mlstm_n, ml_m=state_mlstm_m, ml_conv=state_mlstm_conv,
                     rw_S=state_rwkv_S, rw_shift=state_rwkv_shift)
    pos_prompt = jnp.arange(x_prompt.shape[1], dtype=jnp.int32)
    pos_sample = PAST_LEN + jnp.arange(x_sample.shape[1], dtype=jnp.int32)
    y_prompt, sp = _trunk(x_prompt, _zero_states(x_prompt.shape[0]), pos_prompt, W)
    y_sample, ss = _trunk(x_sample, st_sample, pos_sample, W)
    names = ('lru_h', 'lru_conv', 'ret_S', 'ml_C', 'ml_n', 'ml_m', 'ml_conv', 'rw_S', 'rw_shift')
    return (y_prompt, y_sample) + tuple(sp[n] for n in names) + tuple(ss[n] for n in names)
```

```python
import functools

import jax
import jax.numpy as jnp
from jax import lax
from jax.experimental import pallas as pl
from jax.experimental.pallas import tpu as pltpu

F32 = jnp.float32
BF16 = jnp.bfloat16

D_MODEL = 1024
DEPTH = 4
PAST_LEN = 16384
D_HALF = D_MODEL // 2
CONV_W = 4
LRU_BLOCKS = 8
LRU_BLOCK = D_HALF // LRU_BLOCKS
LRU_C = 8.0
RET_HEADS = 4
RET_DH = D_HALF // RET_HEADS
CHUNK = 128
ROPE_BASE = 10000.0
ML_HEADS = 4
ML_DH = D_HALF // ML_HEADS
ML_QKV_BLOCK = 4
ML_NBLK = D_HALF // ML_QKV_BLOCK
RW_HEADS = 8
RW_DH = D_HALF // RW_HEADS
RW_DECAY_LORA = 32
RW_A_LORA = 32
RW_GATE_LORA = 96
RW_LORA = RW_DECAY_LORA + RW_A_LORA + RW_GATE_LORA
RW_SHIFT_COLS = 3 * D_HALF + RW_LORA
D_FF = 4 * D_MODEL
ALPHA = (2.0 * DEPTH) ** 0.25
EVEN_IN = 6 * D_HALF
ODD_IN = 2 * D_HALF + RW_SHIFT_COLS
LN_EPS = 1e-5
GN_EPS = 1e-5
RW_GN_EPS = 64e-5

LANES = 128
SUBLANES = 8
RW_LORA_PAD = 2 * LANES
RW_COLS_PAD = 3 * D_HALF + RW_LORA_PAD
ODD_IN_PAD = 2 * D_HALF + RW_COLS_PAD
RW_BB = 8
VMEM_LIMIT = 56 * 1024 * 1024


def _params(sem):
    return pltpu.CompilerParams(dimension_semantics=sem, vmem_limit_bytes=VMEM_LIMIT)


def _dot(a, b):
    return jnp.dot(a.astype(BF16), b.astype(BF16), preferred_element_type=F32)


def _dot_nt(a, b):
    return lax.dot_general(a.astype(BF16), b.astype(BF16), (((1,), (1,)), ((), ())),
                           preferred_element_type=F32)


def _dot_tn(a, b):
    return lax.dot_general(a.astype(BF16), b.astype(BF16), (((0,), (0,)), ((), ())),
                           preferred_element_type=F32)


def _split3(a):
    hi = a.astype(BF16)
    r1 = a - hi.astype(F32)
    mid = r1.astype(BF16)
    lo = (r1 - mid.astype(F32)).astype(BF16)
    return hi, mid, lo


def _xdot(a, b01):
    hi, mid, lo = _split3(a)
    f = lambda t: jnp.dot(t, b01, preferred_element_type=F32)
    return f(hi) + f(mid) + f(lo)


def _xdot_l(b01, a):
    hi, mid, lo = _split3(a)
    f = lambda t: jnp.dot(b01, t, preferred_element_type=F32)
    return f(hi) + f(mid) + f(lo)


def _sigmoid(x):
    return 1.0 / (1.0 + jnp.exp(-x))


def _silu(x):
    return x * _sigmoid(x)


def _softplus(x):
    return jnp.maximum(x, 0.0) + jnp.log1p(jnp.exp(-jnp.abs(x)))


def _gelu_tanh(x):
    return 0.5 * x * (1.0 + jnp.tanh(0.7978845608028654 * (x + 0.044715 * (x * x * x))))


def _layer_norm(x, g, b, eps):
    mu = jnp.mean(x, -1, keepdims=True)
    xc = x - mu
    var = jnp.mean(xc * xc, -1, keepdims=True)
    return xc * lax.rsqrt(var + eps) * g + b


def _group_ones(n, group):
    r = lax.broadcasted_iota(jnp.int32, (n, n), 0) // group
    c = lax.broadcasted_iota(jnp.int32, (n, n), 1) // group
    return jnp.where(r == c, 1.0, 0.0).astype(BF16)


def _group_sum(x, ones_bd):
    parts = [_xdot(x[:, s * LANES:(s + 1) * LANES], ones_bd) for s in range(x.shape[1] // LANES)]
    return jnp.concatenate(parts, axis=-1)


def _rotate(x, cosf, sinf):
    return x * cosf + pltpu.roll(x, RET_DH // 2, 1) * sinf


def _proj_kernel(x_ref, w_ref, *o_refs, splits):
    xb = x_ref[...].astype(BF16)
    off = 0
    for o_ref, n in zip(o_refs, splits):
        o_ref[...] = jnp.dot(xb, w_ref[:, off:off + n], preferred_element_type=F32)
        off += n


def _proj(x2d, w_all, j, tm, splits):
    M = x2d.shape[0]
    N = w_all.shape[2]
    return pl.pallas_call(
        functools.partial(_proj_kernel, splits=splits),
        grid=(M // tm,),
        in_specs=[pl.BlockSpec((tm, D_MODEL), lambda i: (i, 0)),
                  pl.BlockSpec((None, D_MODEL, N), lambda i: (j, 0, 0))],
        out_specs=[pl.BlockSpec((tm, n), lambda i: (i, 0)) for n in splits],
        out_shape=[jax.ShapeDtypeStruct((M, n), F32) for n in splits],
        compiler_params=_params(("parallel",)),
        name="proj_in",
    )(x2d, w_all)


FF_CHUNK = 1024


POST_SUB = 256


def _post_kernel(ya_ref, yb_ref, x_ref, wo_ref, g1_ref, b1_ref, w1_ref, w2_ref, g2_ref, b2_ref, o_ref):
    tm = x_ref.shape[0]
    sub = min(tm, POST_SUB)

    def tile(r0):
        rows = slice(r0, r0 + sub)
        y = (jnp.dot(ya_ref[rows, :].astype(BF16), wo_ref[0:D_HALF, :], preferred_element_type=F32)
             + jnp.dot(yb_ref[rows, :].astype(BF16), wo_ref[D_HALF:D_MODEL, :], preferred_element_type=F32))
        yield
        x1 = _layer_norm(ALPHA * x_ref[rows, :] + y, g1_ref[...], b1_ref[...], LN_EPS)
        x1b = x1.astype(BF16)
        acc = jnp.zeros(x1.shape, F32)
        for c in range(D_FF // FF_CHUNK):
            yield
            h = jnp.dot(x1b, w1_ref[:, c * FF_CHUNK:(c + 1) * FF_CHUNK], preferred_element_type=F32)
            yield
            h = jnp.square(jnp.maximum(h, 0.0))
            acc = acc + jnp.dot(h.astype(BF16), w2_ref[c * FF_CHUNK:(c + 1) * FF_CHUNK, :],
                                preferred_element_type=F32)
        yield
        o_ref[rows, :] = _layer_norm(ALPHA * x1 + acc, g2_ref[...], b2_ref[...], LN_EPS)

    _skewed([tile(r0) for r0 in range(0, tm, sub)])


def _post(ya, yb, x2d, wout_all, j, l, ln1_g, ln1_b, w1_all, w2_all, ln2_g, ln2_b, tm):
    M = x2d.shape[0]
    row = lambda i: (i, 0)
    vec = pl.BlockSpec((None, 1, D_MODEL), lambda i: (l, 0, 0))
    r3 = lambda a: a.reshape(DEPTH, 1, D_MODEL)
    return pl.pallas_call(
        _post_kernel,
        grid=(M // tm,),
        in_specs=[pl.BlockSpec((tm, D_HALF), row), pl.BlockSpec((tm, D_HALF), row),
                  pl.BlockSpec((tm, D_MODEL), row),
                  pl.BlockSpec((None, D_MODEL, D_MODEL), lambda i: (j, 0, 0)),
                  vec, vec,
                  pl.BlockSpec((None, D_MODEL, D_FF), lambda i: (l, 0, 0)),
                  pl.BlockSpec((None, D_FF, D_MODEL), lambda i: (l, 0, 0)),
                  vec, vec],
        out_specs=pl.BlockSpec((tm, D_MODEL), row),
        out_shape=jax.ShapeDtypeStruct((M, D_MODEL), F32),
        compiler_params=_params(("parallel",)),
        name="post_mlp",
    )(ya, yb, x2d, wout_all, r3(ln1_g), r3(ln1_b), w1_all, w2_all, r3(ln2_g), r3(ln2_b))


def _conv_prefill(x, xbuf, cw_ref, cb_ref, L):
    xbuf[SUBLANES:SUBLANES + L, :] = x
    y = cb_ref[...]
    for i in range(CONV_W):
        y = y + cw_ref[i:i + 1, :] * xbuf[SUBLANES - (CONV_W - 1) + i:SUBLANES - (CONV_W - 1) + i + L, :]
    tail = xbuf[L + SUBLANES - (CONV_W - 1):L + SUBLANES, :]
    xbuf[SUBLANES - (CONV_W - 1):SUBLANES, :] = tail
    return y, tail


def _lru_gates(xc, wa_ref, ba_ref, wx_ref, bx_ref, lam_ref):
    xcb = xc.astype(BF16)
    nslab = D_HALF // LANES
    ra = jnp.concatenate([jnp.dot(xcb[:, s * LANES:(s + 1) * LANES], wa_ref[s], preferred_element_type=F32)
                          for s in range(nslab)], axis=-1)
    rx = jnp.concatenate([jnp.dot(xcb[:, s * LANES:(s + 1) * LANES], wx_ref[s], preferred_element_type=F32)
                          for s in range(nslab)], axis=-1)
    r = _sigmoid(ra + ba_ref[...])
    i = _sigmoid(rx + bx_ref[...])
    log_a = -LRU_C * r * _softplus(-lam_ref[...])
    a = jnp.exp(log_a)
    t = jnp.tanh(log_a)
    u = jnp.sqrt(-2.0 * t / (1.0 - t)) * (i * xc)
    return a, u


def _head_norm(o, g, b, eps):
    mu = jnp.mean(o, -1, keepdims=True)
    oc = o - mu
    var = jnp.mean(oc * oc, -1, keepdims=True)
    return oc * lax.rsqrt(var + eps) * g + b


EV_NB = 4


class _Cols:
    def __init__(self, ref, off):
        self.ref, self.off = ref, off

    def __getitem__(self, idx):
        if idx is Ellipsis:
            return self.ref[:, self.off:self.off + D_HALF]
        rows, cols = idx
        return self.ref[rows, self.off + cols.start:self.off + cols.stop]


def _even_prefill_kernel(x_ref, w_ref,
                         cw_ref, cb_ref, wa_ref, ba_ref, wx_ref, bx_ref, lam_ref,
                         cos_ref, sin_ref, dmask_ref, qd_ref, kd_ref, cd_ref, gng_ref, gnb_ref,
                         h0_ref, conv0_ref, S0_ref,
                         ya_ref, yb_ref, h_ref, conv_ref, S_ref, xbuf, pbuf, *, L):
    @pl.when(pl.program_id(1) == 0)
    def _():
        h_ref[...] = h0_ref[...]
        S_ref[...] = S0_ref[...]
        xbuf[:, SUBLANES - (CONV_W - 1):SUBLANES, :] = conv0_ref[...]

    rows = [_even_prefill_one(x_ref.at[bi], w_ref, pbuf.at[bi],
                              cw_ref, cb_ref, wa_ref, ba_ref, wx_ref, bx_ref, lam_ref,
                              cos_ref, sin_ref, dmask_ref, qd_ref, kd_ref, cd_ref, gng_ref, gnb_ref,
                              ya_ref.at[bi], yb_ref.at[bi], h_ref.at[bi], conv_ref.at[bi], S_ref.at[bi],
                              xbuf.at[bi], L=L) for bi in range(EV_NB)]
    _round_robin(rows)


def _even_prefill_one(x_ref, w_ref, p_ref,
                      cw_ref, cb_ref, wa_ref, ba_ref, wx_ref, bx_ref, lam_ref,
                      cos_ref, sin_ref, dmask_ref, qd_ref, kd_ref, cd_ref, gng_ref, gnb_ref,
                      ya_ref, yb_ref, h_ref, conv_ref, S_ref, xbuf, *, L):
    p_ref[...] = jnp.dot(x_ref[...].astype(BF16), w_ref[...], preferred_element_type=F32)
    xa_ref, ga_ref, q_ref, k_ref, v_ref, gb_ref = (_Cols(p_ref, i * D_HALF) for i in range(6))
    yield
    H = range(RET_HEADS)
    sls = [slice(hh * RET_DH, (hh + 1) * RET_DH) for hh in H]
    cosf = cos_ref[...]
    sinf = sin_ref[...]
    qh = [_rotate(q_ref[:, sl], cosf, sinf) for sl in sls]
    kh = [_rotate(k_ref[:, sl], cosf, sinf) * (RET_DH ** -0.5) for sl in sls]
    vh = [v_ref[:, sl].astype(BF16) for sl in sls]
    qk = [_dot_nt(qh[hh], kh[hh]) for hh in H]
    S = [S_ref[hh] for hh in H]
    qS = [_dot(qh[hh] * qd_ref[:, sls[hh]], S[hh]) for hh in H]
    kv = [_dot_tn(kh[hh] * kd_ref[:, sls[hh]], vh[hh]) for hh in H]
    yield
    xc, tail = _conv_prefill(xa_ref[...], xbuf, cw_ref, cb_ref, L)
    conv_ref[...] = tail
    a, u = _lru_gates(xc, wa_ref, ba_ref, wx_ref, bx_ref, lam_ref)
    yield
    for hh in H:
        S_ref[hh] = S[hh] * cd_ref[hh] + kv[hh]
    sc = [(qk[hh] * dmask_ref[hh]).astype(BF16) for hh in H]
    o = [jnp.dot(sc[hh], vh[hh], preferred_element_type=F32) + qS[hh] for hh in H]
    yield
    row = lax.broadcasted_iota(jnp.int32, (L, D_HALF), 0) % SUBLANES
    s = 1
    while s < SUBLANES:
        keep = row >= s
        a_sh = jnp.where(keep, pltpu.roll(a, s, 0), 1.0)
        u_sh = jnp.where(keep, pltpu.roll(u, s, 0), 0.0)
        u = a * u_sh + u
        a = a * a_sh
        s *= 2
        yield
    carry = h_ref[...]
    groups = []
    for g in range(L // SUBLANES):
        rows8 = slice(g * SUBLANES, (g + 1) * SUBLANES)
        hg = a[rows8, :] * carry + u[rows8, :]
        carry = hg[SUBLANES - 1:SUBLANES, :]
        groups.append(hg)
        if g % 4 == 3:
            yield
    h = jnp.concatenate(groups, axis=0)
    h_ref[...] = carry
    ya_ref[...] = _gelu_tanh(ga_ref[...]) * h
    ones = jnp.ones((RET_DH, LANES), BF16)
    mu = [jnp.dot(o[hh].astype(BF16), ones, preferred_element_type=F32) * (1.0 / RET_DH) for hh in H]
    yield
    oc = [o[hh] - mu[hh] for hh in H]
    var = [jnp.dot((oc[hh] * oc[hh]).astype(BF16), ones, preferred_element_type=F32) * (1.0 / RET_DH) for hh in H]
    yield
    for hh in H:
        sl = sls[hh]
        on = oc[hh] * lax.rsqrt(var[hh] + GN_EPS) * gng_ref[:, sl] + gnb_ref[:, sl]
        yb_ref[:, sl] = _silu(gb_ref[:, sl]) * on


def _ret_tables(L):
    log_gamma = jnp.log1p(-jnp.exp2(-5.0 - jnp.arange(RET_HEADS, dtype=F32)))
    idx = jnp.arange(L, dtype=F32)
    diff = idx[:, None] - idx[None, :]
    dmask = jnp.where(diff >= 0, jnp.exp(log_gamma[:, None, None] * jnp.maximum(diff, 0.0)), 0.0)
    qd = jnp.exp(log_gamma[:, None] * (idx + 1.0))
    kd = jnp.exp(log_gamma[:, None] * (L - 1.0 - idx))
    cd = jnp.exp(log_gamma * L)
    qd_full = jnp.repeat(qd.T, RET_DH, axis=1)
    kd_full = jnp.repeat(kd.T, RET_DH, axis=1)
    cd_full = jnp.broadcast_to(cd[:, None, None], (RET_HEADS, 1, RET_DH))
    return dmask, qd_full, kd_full, cd_full


def _rope_tables(pos):
    half = RET_DH // 2
    inv = ROPE_BASE ** (-jnp.arange(half, dtype=F32) / half)
    ang = pos.astype(F32)[:, None] * inv[None, :]
    cos, sin = jnp.cos(ang), jnp.sin(ang)
    return jnp.concatenate([cos, cos], -1), jnp.concatenate([-sin, sin], -1)


def _lru_dense(w):
    pairs = LANES // LRU_BLOCK
    w4 = w.reshape(LRU_BLOCKS // pairs, pairs, LRU_BLOCK, LRU_BLOCK)
    eye = jnp.eye(pairs, dtype=w.dtype)
    d = w4[:, :, :, None, :] * eye[None, :, None, :, None]
    return d.reshape(LRU_BLOCKS // pairs, LANES, LANES).astype(BF16)


def _full(shape):
    n = len(shape)
    return pl.BlockSpec(shape, lambda *_: (0,) * n)


def _even_prefill(x3, w_in_all, j, pos, st, Wl):
    B, T, _ = x3.shape
    L = CHUNK
    nc = T // L
    dmask, qd, kd, cd = _ret_tables(L)
    cosf, sinf = _rope_tables(pos)
    nb = EV_NB
    perb = lambda *s: pl.BlockSpec((nb,) + s, lambda b, c: (b,) + (0,) * len(s))
    row2 = lambda a: a.reshape(1, D_HALF)
    ins = [x3, w_in_all, Wl['lru_conv_w'], row2(Wl['lru_conv_b']), Wl['lru_wa_d'], row2(Wl['lru_ba']),
           Wl['lru_wx_d'], row2(Wl['lru_bx']), row2(Wl['lru_lambda']),
           cosf, sinf, dmask, qd, kd, cd, row2(Wl['ret_gn_g']), row2(Wl['ret_gn_b']),
           st['lru_h'].reshape(B, 1, D_HALF), st['lru_conv'], st['ret_S']]
    in_specs = [pl.BlockSpec((nb, L, D_MODEL), lambda b, c: (b, c, 0)),
                pl.BlockSpec((None, D_MODEL, EVEN_IN), lambda b, c: (j, 0, 0))] + [
        _full((CONV_W, D_HALF)), _full((1, D_HALF)), _full((4, LANES, LANES)), _full((1, D_HALF)),
        _full((4, LANES, LANES)), _full((1, D_HALF)), _full((1, D_HALF)),
        pl.BlockSpec((L, RET_DH), lambda b, c: (c, 0)), pl.BlockSpec((L, RET_DH), lambda b, c: (c, 0)),
        _full((RET_HEADS, L, L)), _full((L, D_HALF)), _full((L, D_HALF)), _full((RET_HEADS, 1, RET_DH)),
        _full((1, D_HALF)), _full((1, D_HALF)),
        perb(1, D_HALF), perb(CONV_W - 1, D_HALF), perb(RET_HEADS, RET_DH, RET_DH)]
    seq = pl.BlockSpec((nb, L, D_HALF), lambda b, c: (b, c, 0))
    ya, yb, h, conv, S = pl.pallas_call(
        functools.partial(_even_prefill_kernel, L=L),
        grid=(B // nb, nc),
        in_specs=in_specs,
        out_specs=[seq, seq, perb(1, D_HALF), perb(CONV_W - 1, D_HALF), perb(RET_HEADS, RET_DH, RET_DH)],
        out_shape=[jax.ShapeDtypeStruct((B, T, D_HALF), F32), jax.ShapeDtypeStruct((B, T, D_HALF), F32),
                   jax.ShapeDtypeStruct((B, 1, D_HALF), F32),
                   jax.ShapeDtypeStruct((B, CONV_W - 1, D_HALF), F32),
                   jax.ShapeDtypeStruct((B, RET_HEADS, RET_DH, RET_DH), F32)],
        scratch_shapes=[pltpu.VMEM((nb, L + SUBLANES, D_HALF), F32), pltpu.VMEM((nb, L, EVEN_IN), F32)],
        compiler_params=_params(("parallel", "arbitrary")),
        name="even_prefill",
    )(*ins)
    return ya, yb, (h.reshape(B, D_HALF), conv, S)


DEC_BB = 8


def _even_decode_kernel(p_ref, cw_ref, cb_ref, wa_ref, ba_ref, wx_ref, bx_ref, lam_ref,
                        cos_ref, sin_ref, dm_ref, qd_ref, kd_ref, cd_ref, gng_ref, gnb_ref,
                        h0_ref, conv0_ref, S0_ref,
                        ya_ref, yb_ref, h_ref, conv_ref, S_ref):
    col = lambda i: p_ref[:, i * D_HALF:(i + 1) * D_HALF]
    xa = col(0)
    xc = cb_ref[...] + cw_ref[CONV_W - 1:CONV_W, :] * xa
    for i in range(CONV_W - 1):
        xc = xc + cw_ref[i:i + 1, :] * conv0_ref[i]
    for i in range(CONV_W - 2):
        conv_ref[i] = conv0_ref[i + 1]
    conv_ref[CONV_W - 2] = xa
    a, u = _lru_gates(xc, wa_ref, ba_ref, wx_ref, bx_ref, lam_ref)
    h = a * h0_ref[...] + u
    h_ref[...] = h
    ya_ref[...] = _gelu_tanh(col(1)) * h

    cosf = cos_ref[...]
    sinf = sin_ref[...]
    row8 = lax.broadcasted_iota(jnp.int32, (SUBLANES, RET_DH), 0)
    q, k, v, gb = col(2), col(3), col(4), col(5)
    for hh in range(RET_HEADS):
        sl = slice(hh * RET_DH, (hh + 1) * RET_DH)
        qh = _rotate(q[:, sl], cosf, sinf)
        kh = _rotate(k[:, sl], cosf, sinf) * (RET_DH ** -0.5)
        vh = v[:, sl]
        qk = jnp.sum(qh * kh, -1, keepdims=True) * dm_ref[:, sl]
        qq = qh * qd_ref[:, sl]
        kk = kh * kd_ref[:, sl]
        rows = []
        for bi in range(DEC_BB):
            S = S0_ref[bi, hh]
            q8 = jnp.broadcast_to(qq[bi:bi + 1, :], (SUBLANES, RET_DH))
            rows.append(_dot_f32(q8, S)[0:1, :])
            k8 = jnp.where(row8 == 0, jnp.broadcast_to(kk[bi:bi + 1, :], (SUBLANES, RET_DH)), 0.0)
            v8 = jnp.broadcast_to(vh[bi:bi + 1, :], (SUBLANES, RET_DH))
            S_ref[bi, hh] = S * cd_ref[hh] + _dot_tn_f32(k8, v8)
        o = qk * vh + jnp.concatenate(rows, axis=0)
        on = _head_norm(o, gng_ref[:, sl], gnb_ref[:, sl], GN_EPS)
        yb_ref[:, sl] = _silu(gb[:, sl]) * on


def _dot_f32(a, b):
    return jnp.dot(a, b, preferred_element_type=F32)


def _dot_tn_f32(a, b):
    return lax.dot_general(a, b, (((0,), (0,)), ((), ())), preferred_element_type=F32)


def _skip_ref(kernel, pos):
    def wrapped(*refs):
        kernel(*refs[:pos], *refs[pos + 1:])
    return wrapped


def _all_layers(kernel, pos_in, pos_out, j, n_layers):
    def wrapped(*refs):
        refs = list(refs)
        s_in, s_out = refs[pos_in], refs[pos_out]
        for other in range(n_layers):
            if other != j:
                s_out[other] = s_in[other]
        refs[pos_in], refs[pos_out] = s_in.at[j], s_out.at[j]
        kernel(*refs)
    return wrapped


def _stacked_state_io(kernel, S_all, S_prev, j, bb, pos_in, n_in, out_idx):
    n_layers, tail = S_all.shape[0], S_all.shape[2:]
    zeros = (0,) * len(tail)
    shape = jax.ShapeDtypeStruct(S_all.shape, F32)
    if S_prev is None:
        spec = pl.BlockSpec((n_layers, bb) + tail, lambda i: (0, i) + zeros)
        return _all_layers(kernel, pos_in, n_in + out_idx, j, n_layers), spec, [], [], {}, shape
    spec = pl.BlockSpec((None, bb) + tail, lambda i: (j, i) + zeros)
    return _skip_ref(kernel, n_in), spec, [S_prev], [pl.BlockSpec(memory_space=pl.ANY)], {n_in: out_idx}, shape


def _even_decode(p2, pos, st, Wl, S_all, j, S_prev):
    B = p2.shape[0]
    bb = DEC_BB
    dmask, qd, kd, cd = _ret_tables(1)
    dm = jnp.repeat(dmask[:, 0, :].T, RET_DH, axis=1)
    cosf, sinf = _rope_tables(pos)
    row2 = lambda a: a.reshape(1, D_HALF)
    rows = lambda n: pl.BlockSpec((bb, n), lambda i: (i, 0))
    convs = pl.BlockSpec((CONV_W - 1, bb, D_HALF), lambda i: (0, i, 0))
    ins = [p2, Wl['lru_conv_w'], row2(Wl['lru_conv_b']), Wl['lru_wa_d'], row2(Wl['lru_ba']),
           Wl['lru_wx_d'], row2(Wl['lru_bx']), row2(Wl['lru_lambda']),
           cosf, sinf, dm, qd, kd, cd, row2(Wl['ret_gn_g']), row2(Wl['ret_gn_b']),
           st['lru_h'], jnp.swapaxes(st['lru_conv'], 0, 1), S_all]
    kern, Ss, extra_in, extra_specs, aliases, S_shape = _stacked_state_io(
        _even_decode_kernel, S_all, S_prev, j, bb, len(ins) - 1, len(ins), 4)
    in_specs = [rows(EVEN_IN), _full((CONV_W, D_HALF)), _full((1, D_HALF)), _full((4, LANES, LANES)),
                _full((1, D_HALF)), _full((4, LANES, LANES)), _full((1, D_HALF)), _full((1, D_HALF)),
                _full((1, RET_DH)), _full((1, RET_DH)), _full((1, D_HALF)), _full((1, D_HALF)),
                _full((1, D_HALF)), _full((RET_HEADS, 1, RET_DH)), _full((1, D_HALF)), _full((1, D_HALF)),
                rows(D_HALF), convs, Ss] + extra_specs
    ya, yb, h, conv, S = pl.pallas_call(
        kern,
        grid=(B // bb,),
        in_specs=in_specs,
        out_specs=[rows(D_HALF), rows(D_HALF), rows(D_HALF), convs, Ss],
        out_shape=[jax.ShapeDtypeStruct((B, D_HALF), F32), jax.ShapeDtypeStruct((B, D_HALF), F32),
                   jax.ShapeDtypeStruct((B, D_HALF), F32),
                   jax.ShapeDtypeStruct((CONV_W - 1, B, D_HALF), F32), S_shape],
        input_output_aliases=aliases,
        compiler_params=_params(("parallel",)),
        name="even_decode",
    )(*ins, *extra_in)
    return ya, yb, (h, jnp.swapaxes(conv, 0, 1), S)


def _mlstm_qkv_gates(xm, xc, wq_ref, wk_ref, wv_ref, wg_ref, bg_ref):
    q = _dot(xc, wq_ref[...])
    k = _dot(xc, wk_ref[...])
    v = _dot(xm, wv_ref[...])
    g_col = (_dot(q, wg_ref[0:D_HALF, :]) + _dot(k, wg_ref[D_HALF:2 * D_HALF, :])
             + _dot(v, wg_ref[2 * D_HALF:3 * D_HALF, :]) + bg_ref[...])
    return q, k, v, g_col


ML_NB = 4


def _skewed(gens, lag=1):
    live = []
    pending = list(gens)
    tick = 0
    while pending or live:
        if pending and tick % lag == 0:
            live.append(pending.pop(0))
        tick += 1
        nxt = []
        for g in live:
            try:
                next(g)
                nxt.append(g)
            except StopIteration:
                pass
        live = nxt


def _round_robin(gens):
    gens = list(gens)
    while gens:
        alive = []
        for g in gens:
            try:
                next(g)
                alive.append(g)
            except StopIteration:
                pass
        gens = alive


def _mlstm_prefill_kernel(x_ref, w_ref, cw_ref, cb_ref, wq_ref, wk_ref, wv_ref, wg_ref, bg_ref,
                          wgt_ref, bgt_ref, gng_ref, gnb_ref, skip_ref,
                          conv0_ref, C0_ref, n0_ref, m0_ref,
                          yc_ref, conv_ref, C_ref, n_ref, m_ref, xbuf, ncol, pbuf, *, L):
    @pl.when(pl.program_id(1) == 0)
    def _():
        ones = jnp.ones((ML_DH, LANES), BF16)
        eye = jnp.where(lax.broadcasted_iota(jnp.int32, (ML_DH, LANES), 0)
                        == lax.broadcasted_iota(jnp.int32, (ML_DH, LANES), 1), 1.0, 0.0)
        C_ref[...] = C0_ref[...]
        m_ref[...] = m0_ref[...]
        for bi in range(ML_NB):
            xbuf[bi, SUBLANES - (CONV_W - 1):SUBLANES, :] = conv0_ref[bi]
            for hh in range(ML_HEADS):
                ncol[bi, hh] = _xdot(eye * n0_ref[bi, :, hh * ML_DH:(hh + 1) * ML_DH], ones)

    rows = [_mlstm_prefill_one(x_ref.at[bi], w_ref, pbuf.at[bi], cw_ref, cb_ref, wq_ref, wk_ref, wv_ref, wg_ref,
                               bg_ref, wgt_ref, bgt_ref, gng_ref, gnb_ref, skip_ref,
                               yc_ref.at[bi], conv_ref.at[bi], C_ref.at[bi], n_ref.at[bi], m_ref.at[bi],
                               xbuf.at[bi], ncol.at[bi], L=L) for bi in range(ML_NB)]
    _round_robin(rows)


def _mlstm_prefill_one(x_ref, w_ref, p_ref, cw_ref, cb_ref, wq_ref, wk_ref, wv_ref, wg_ref, bg_ref,
                       wgt_ref, bgt_ref, gng_ref, gnb_ref, skip_ref,
                       yc_ref, conv_ref, C_ref, n_ref, m_ref, xbuf, ncol, *, L):
    p_ref[...] = jnp.dot(x_ref[...].astype(BF16), w_ref[...], preferred_element_type=F32)
    xm_ref, z_ref = _Cols(p_ref, 0), _Cols(p_ref, D_HALF)
    yield
    xm = xm_ref[...]
    xc, tail = _conv_prefill(xm, xbuf, cw_ref, cb_ref, L)
    conv_ref[...] = tail
    xc = _silu(xc)
    yield
    q, k, v, g_col = _mlstm_qkv_gates(xm, xc, wq_ref, wk_ref, wv_ref, wg_ref, bg_ref)
    g_row = (_dot_nt(wgt_ref[:, 0:D_HALF], q) + _dot_nt(wgt_ref[:, D_HALF:2 * D_HALF], k)
             + _dot_nt(wgt_ref[:, 2 * D_HALF:3 * D_HALF], v) + bgt_ref[...])
    yield
    ri = lax.broadcasted_iota(jnp.int32, (L, L), 0)
    ci = lax.broadcasted_iota(jnp.int32, (L, L), 1)
    causal = ri >= ci
    tril = jnp.where(causal, 1.0, 0.0).astype(BF16)
    triu = jnp.where(ci >= ri, 1.0, 0.0).astype(BF16)
    ones = jnp.ones((L, LANES), BF16)
    eye = jnp.where(ri == ci, 1.0, 0.0)

    li_col = g_col
    lf_col = -_softplus(-pltpu.roll(g_col, LANES - ML_HEADS, 1))
    b_col = _xdot_l(tril, lf_col)
    lf_row = -_softplus(-g_row)
    b_row = _xdot(lf_row, triu)
    yield
    c_row = g_row[0:ML_HEADS, :] - b_row[ML_HEADS:2 * ML_HEADS, :]
    row = lax.broadcasted_iota(jnp.int32, (L, LANES), 0)
    pm = li_col - b_col
    sft = 1
    while sft < L:
        pm = jnp.maximum(pm, jnp.where(row >= sft, pltpu.roll(pm, sft, 0), -jnp.inf))
        sft *= 2
    m_prev = m_ref[...]
    u_col = -jnp.maximum(pm, m_prev)
    m_t_col = b_col - u_col
    e_col = jnp.exp(-m_t_col)
    m_new = m_t_col[L - 1:L, :]
    b_last = b_col[L - 1:L, :]
    wk_col = jnp.exp(b_last - b_col + li_col - m_new)
    wC_row = jnp.exp(b_last + m_prev - m_new)
    m_ref[...] = m_new
    yield
    vones = ones
    rep = lambda col, hh: jnp.broadcast_to(col[:, hh:hh + 1], (L, LANES))
    H = range(ML_HEADS)
    sls = [slice(hh * ML_DH, (hh + 1) * ML_DH) for hh in H]
    qh = [q[:, sl].astype(BF16) for sl in sls]
    kh = [k[:, sl] * (ML_DH ** -0.5) for sl in sls]
    vh1 = [jnp.concatenate([v[:, sl].astype(BF16), vones], axis=-1) for sl in sls]
    qk = [_dot_nt(qh[hh], kh[hh]) for hh in H]
    CN = [jnp.concatenate([C_ref[hh], ncol[hh]], axis=-1) for hh in H]
    qc = [jnp.dot(qh[hh], CN[hh].astype(BF16), preferred_element_type=F32) for hh in H]
    kw = [kh[hh] * rep(wk_col, hh) for hh in H]
    upd = [lax.dot_general(kw[hh].astype(BF16), vh1[hh], (((0,), (0,)), ((), ())), preferred_element_type=F32)
           for hh in H]
    yield
    for hh in H:
        w_C =jnp.broadcast_to(wC_row[:, hh:hh + 1], (ML_DH, 2 * ML_DH))
        CNn = w_C * CN[hh] + upd[hh]
        C_ref[hh] = CNn[:, 0:ML_DH]
        ncol[hh] = CNn[:, ML_DH:2 * ML_DH]
        n_ref[:, sls[hh]] = jnp.sum(CNn[:, ML_DH:2 * ML_DH] * eye, axis=0, keepdims=True)
    yield
    u = [rep(u_col, hh) for hh in H]
    s = [(qk[hh] * jnp.exp(jnp.where(causal, u[hh] + c_row[hh:hh + 1, :], -jnp.inf))).astype(BF16) for hh in H]
    sv = [jnp.dot(s[hh], vh1[hh], preferred_element_type=F32) for hh in H]
    yield
    hcell = []
    for hh in H:
        w_inter = jnp.exp(rep(m_prev, hh) + u[hh])
        num = sv[hh][:, 0:ML_DH] + w_inter * qc[hh][:, 0:ML_DH]
        den = sv[hh][:, ML_DH:2 * ML_DH] + w_inter * qc[hh][:, ML_DH:2 * ML_DH]
        hcell.append(num / jnp.maximum(jnp.abs(den), rep(e_col, hh)))
    mu = [jnp.dot(hcell[hh].astype(BF16), ones, preferred_element_type=F32) * (1.0 / ML_DH) for hh in H]
    yield
    oc = [hcell[hh] - mu[hh] for hh in H]
    var = [jnp.dot((oc[hh] * oc[hh]).astype(BF16), ones, preferred_element_type=F32) * (1.0 / ML_DH) for hh in H]
    for hh in H:
        sl = sls[hh]
        hn = oc[hh] * lax.rsqrt(var[hh] + GN_EPS) * gng_ref[:, sl] + gnb_ref[:, sl]
        yc_ref[:, sl] = (hn + skip_ref[:, sl] * xc[:, sl]) * _silu(z_ref[:, sl])


def _ml_dense(w):
    w2 = w.reshape(D_HALF, ML_QKV_BLOCK)
    c = jnp.arange(D_HALF)
    spread = (c[None, :] % ML_QKV_BLOCK == jnp.arange(ML_QKV_BLOCK)[:, None]).astype(w.dtype)
    full = jnp.dot(w2, spread, precision=lax.Precision.HIGHEST)
    same_block = c[:, None] // ML_QKV_BLOCK == c[None, :] // ML_QKV_BLOCK
    return jnp.where(same_block, full, 0.0).astype(BF16)


def _pad_lanes(a, n=LANES):
    return jnp.pad(a, [(0, 0)] * (a.ndim - 1) + [(0, n - a.shape[-1])])


def _mlstm_weights(Wl):
    wg = _pad_lanes(Wl['ml_w_gate']).astype(BF16)
    bg = _pad_lanes(Wl['ml_b_gate'].reshape(1, 2 * ML_HEADS))
    wgt = Wl['ml_w_gate'].T.astype(BF16)
    bgt = jnp.broadcast_to(Wl['ml_b_gate'].reshape(2 * ML_HEADS, 1), (2 * ML_HEADS, LANES))
    return wg, bg, wgt, bgt


def _mlstm_prefill(x3, w_in_all, j, st, Wl):
    B, T, _ = x3.shape
    L = CHUNK
    nc = T // L
    wg, bg, wgt, bgt = _mlstm_weights(Wl)
    nb = ML_NB
    perb = lambda *s: pl.BlockSpec((nb,) + s, lambda b, c: (b,) + (0,) * len(s))
    row2 = lambda a: a.reshape(1, D_HALF)
    ins = [x3, w_in_all, Wl['ml_conv_w'], row2(Wl['ml_conv_b']), Wl['ml_wq_d'], Wl['ml_wk_d'], Wl['ml_wv_d'],
           wg, bg, wgt, bgt, row2(Wl['ml_gn_g']), row2(Wl['ml_gn_b']), row2(Wl['ml_skip']),
           st['ml_conv'], st['ml_C'], st['ml_n'].reshape(B, 1, D_HALF),
           _pad_lanes(st['ml_m']).reshape(B, 1, LANES)]
    in_specs = [pl.BlockSpec((nb, L, D_MODEL), lambda b, c: (b, c, 0)),
                pl.BlockSpec((None, D_MODEL, 2 * D_HALF), lambda b, c: (j, 0, 0)),
                _full((CONV_W, D_HALF)), _full((1, D_HALF)),
                _full((D_HALF, D_HALF)), _full((D_HALF, D_HALF)), _full((D_HALF, D_HALF)),
                _full((3 * D_HALF, LANES)), _full((1, LANES)), _full((2 * ML_HEADS, 3 * D_HALF)),
                _full((2 * ML_HEADS, LANES)), _full((1, D_HALF)), _full((1, D_HALF)), _full((1, D_HALF)),
                perb(CONV_W - 1, D_HALF), perb(ML_HEADS, ML_DH, ML_DH), perb(1, D_HALF), perb(1, LANES)]
    seq = pl.BlockSpec((nb, L, D_HALF), lambda b, c: (b, c, 0))
    yc, conv, C, n, m = pl.pallas_call(
        functools.partial(_mlstm_prefill_kernel, L=L),
        grid=(B // nb, nc),
        in_specs=in_specs,
        out_specs=[seq, perb(CONV_W - 1, D_HALF), perb(ML_HEADS, ML_DH, ML_DH), perb(1, D_HALF), perb(1, LANES)],
        out_shape=[jax.ShapeDtypeStruct((B, T, D_HALF), F32),
                   jax.ShapeDtypeStruct((B, CONV_W - 1, D_HALF), F32),
                   jax.ShapeDtypeStruct((B, ML_HEADS, ML_DH, ML_DH), F32),
                   jax.ShapeDtypeStruct((B, 1, D_HALF), F32),
                   jax.ShapeDtypeStruct((B, 1, LANES), F32)],
        scratch_shapes=[pltpu.VMEM((nb, L + SUBLANES, D_HALF), F32),
                        pltpu.VMEM((nb, ML_HEADS, ML_DH, LANES), F32),
                        pltpu.VMEM((nb, L, 2 * D_HALF), F32)],
        compiler_params=_params(("parallel", "arbitrary")),
        name="mlstm_prefill",
    )(*ins)
    return yc, (C, n.reshape(B, ML_HEADS, ML_DH), m[:, 0, :ML_HEADS], conv)


def _mlstm_decode_kernel(p_ref, cw_ref, cb_ref, wq_ref, wk_ref, wv_ref, wg_ref, bg_ref,
                         gng_ref, gnb_ref, skip_ref, conv0_ref, C0_ref, n0_ref, m0_ref,
                         yc_ref, conv_ref, C_ref, n_ref, m_ref):
    xm = p_ref[:, 0:D_HALF]
    z = p_ref[:, D_HALF:2 * D_HALF]
    xc = cb_ref[...] + cw_ref[CONV_W - 1:CONV_W, :] * xm
    for i in range(CONV_W - 1):
        xc = xc + cw_ref[i:i + 1, :] * conv0_ref[i]
    for i in range(CONV_W - 2):
        conv_ref[i] = conv0_ref[i + 1]
    conv_ref[CONV_W - 2] = xm
    xc = _silu(xc)
    q, k, v, g = _mlstm_qkv_gates(xm, xc, wq_ref, wk_ref, wv_ref, wg_ref, bg_ref)
    lf_all = -_softplus(-g)
    lane = lax.broadcasted_iota(jnp.int32, (1, LANES), 1)
    row8 = lax.broadcasted_iota(jnp.int32, (SUBLANES, ML_DH), 0)
    m_all = m0_ref[...]
    m_out = m_all
    for hh in range(ML_HEADS):
        sl = slice(hh * ML_DH, (hh + 1) * ML_DH)
        qh, vh = q[:, sl], v[:, sl]
        kh = k[:, sl] * (ML_DH ** -0.5)
        li = g[:, hh:hh + 1]
        lf = lf_all[:, ML_HEADS + hh:ML_HEADS + hh + 1]
        m_prev = m_all[:, hh:hh + 1]
        n = n0_ref[:, sl]
        log_inter = lf + m_prev
        m_t = jnp.maximum(li, log_inter)
        s = jnp.sum(qh * kh, -1, keepdims=True) * jnp.exp(li - m_t)
        w_inter = jnp.exp(log_inter - m_t)
        w_k = jnp.exp(li - m_t)
        w_C = jnp.exp(log_inter - m_t)
        kw = kh * w_k
        rows = []
        for bi in range(DEC_BB):
            C = C0_ref[bi, hh]
            q8 = jnp.broadcast_to(qh[bi:bi + 1, :], (SUBLANES, ML_DH))
            rows.append(_dot_f32(q8, C)[0:1, :])
            k8 = jnp.where(row8 == 0, jnp.broadcast_to(kw[bi:bi + 1, :], (SUBLANES, ML_DH)), 0.0)
            v8 = jnp.broadcast_to(vh[bi:bi + 1, :], (SUBLANES, ML_DH))
            C_ref[bi, hh] = w_C[bi:bi + 1, :] * C + _dot_tn_f32(k8, v8)
        qC = jnp.concatenate(rows, axis=0)
        num = s * vh + w_inter * qC
        den = s + w_inter * jnp.sum(qh * n, -1, keepdims=True)
        hcell = num / jnp.maximum(jnp.abs(den), jnp.exp(-m_t))
        n_ref[:, sl] = w_C * n + kw
        m_out = jnp.where(lane == hh, m_t, m_out)
        hn = _head_norm(hcell, gng_ref[:, sl], gnb_ref[:, sl], GN_EPS)
        yc_ref[:, sl] = (hn + skip_ref[:, sl] * xc[:, sl]) * _silu(z[:, sl])
    m_ref[...] = m_out


def _mlstm_decode(p2, st, Wl, C_all, j, C_prev):
    B = p2.shape[0]
    bb = DEC_BB
    wg, bg, _, _ = _mlstm_weights(Wl)
    row2 = lambda a: a.reshape(1, D_HALF)
    rows = lambda n: pl.BlockSpec((bb, n), lambda i: (i, 0))
    convs = pl.BlockSpec((CONV_W - 1, bb, D_HALF), lambda i: (0, i, 0))
    ins = [p2, Wl['ml_conv_w'], row2(Wl['ml_conv_b']), Wl['ml_wq_d'], Wl['ml_wk_d'], Wl['ml_wv_d'], wg, bg,
           row2(Wl['ml_gn_g']), row2(Wl['ml_gn_b']), row2(Wl['ml_skip']),
           jnp.swapaxes(st['ml_conv'], 0, 1), C_all, st['ml_n'].reshape(B, D_HALF), _pad_lanes(st['ml_m'])]
    kern, Cs, extra_in, extra_specs, aliases, C_shape = _stacked_state_io(
        _mlstm_decode_kernel, C_all, C_prev, j, bb, len(ins) - 3, len(ins), 2)
    in_specs = [pl.BlockSpec((bb, 2 * D_HALF), lambda i: (i, 0)), _full((CONV_W, D_HALF)), _full((1, D_HALF)),
                _full((D_HALF, D_HALF)), _full((D_HALF, D_HALF)), _full((D_HALF, D_HALF)),
                _full((3 * D_HALF, LANES)), _full((1, LANES)),
                _full((1, D_HALF)), _full((1, D_HALF)), _full((1, D_HALF)),
                convs, Cs, rows(D_HALF), rows(LANES)] + extra_specs
    yc, conv, C, n, m = pl.pallas_call(
        kern,
        grid=(B // bb,),
        in_specs=in_specs,
        out_specs=[rows(D_HALF), convs, Cs, rows(D_HALF), rows(LANES)],
        out_shape=[jax.ShapeDtypeStruct((B, D_HALF), F32),
                   jax.ShapeDtypeStruct((CONV_W - 1, B, D_HALF), F32),
                   C_shape,
                   jax.ShapeDtypeStruct((B, D_HALF), F32),
                   jax.ShapeDtypeStruct((B, LANES), F32)],
        input_output_aliases=aliases,
        compiler_params=_params(("parallel",)),
        name="mlstm_decode",
    )(*ins, *extra_in)
    return yc, (C, n.reshape(B, ML_HEADS, ML_DH), m[:, :ML_HEADS], jnp.swapaxes(conv, 0, 1))


def _rwkv_pre_body(pr, pr_prev, mu_ref, w0_ref, a0_ref, w2_ref, a2_ref, g2_ref, kkw_ref, kaw_ref, rk_ref,
                   r_ref, d_ref, k_ref, v_ref, a_ref, b_ref, g_ref, bonus_ref):
    pm = pr + (pr_prev - pr) * mu_ref[...]
    r = pm[:, 0:D_HALF]
    kr = pm[:, D_HALF:2 * D_HALF]
    vr = pm[:, 2 * D_HALF:3 * D_HALF]
    lo = pm[:, 3 * D_HALF:RW_COLS_PAD]
    w_log = -_softplus(-(w0_ref[...] + _dot(jnp.tanh(lo), w2_ref[...]))) - 0.5
    a = _sigmoid(a0_ref[...] + _dot(lo, a2_ref[...]))
    g = _dot(_sigmoid(lo), g2_ref[...])
    ones_bd = _group_ones(LANES, RW_DH)
    kk = kr * kkw_ref[...]
    kk = kk / jnp.maximum(jnp.sqrt(_group_sum(kk * kk, ones_bd)), 1e-12)
    kh = kr * (1.0 + (a - 1.0) * kaw_ref[...])
    r_ref[...] = r
    d_ref[...] = jnp.exp(-jnp.exp(w_log))
    k_ref[...] = kh
    v_ref[...] = vr
    a_ref[...] = -kk
    b_ref[...] = kk * a
    g_ref[...] = g
    bonus_ref[...] = _group_sum(r * kh * rk_ref[...], ones_bd) * vr


def _rwkv_pre_decode_kernel(pr_ref, prev_ref, *rest):
    _rwkv_pre_body(pr_ref[...], prev_ref[...], *rest[:9], *rest[9:17])


def _rwkv_pre_weights(Wl):
    row2 = lambda a: a.reshape(1, D_HALF)
    padr = lambda w, o: jnp.pad(w, ((o, RW_LORA_PAD - o - w.shape[0]), (0, 0))).astype(BF16)
    mu = _pad_lanes(Wl['rw_mu'].reshape(1, RW_SHIFT_COLS), RW_COLS_PAD)
    ws = [mu, row2(Wl['rw_w0']), row2(Wl['rw_a0']),
          padr(Wl['rw_w2'], 0), padr(Wl['rw_a2'], RW_DECAY_LORA), padr(Wl['rw_g2'], RW_DECAY_LORA + RW_A_LORA),
          row2(Wl['rw_kk']), row2(Wl['rw_ka']), row2(Wl['rw_rk'])]
    specs = [_full((1, RW_COLS_PAD)), _full((1, D_HALF)), _full((1, D_HALF)),
             _full((RW_LORA_PAD, D_HALF)), _full((RW_LORA_PAD, D_HALF)), _full((RW_LORA_PAD, D_HALF)),
             _full((1, D_HALF)), _full((1, D_HALF)), _full((1, D_HALF))]
    return ws, specs


def _rwkv_pre_decode(pr, shift0, Wl):
    B = pr.shape[0]
    ws, wspecs = _rwkv_pre_weights(Wl)
    full2 = lambda n: pl.BlockSpec((B, n), lambda i: (0, 0))
    outs = pl.pallas_call(
        _rwkv_pre_decode_kernel,
        grid=(1,),
        in_specs=[full2(RW_COLS_PAD), full2(RW_COLS_PAD)] + wspecs,
        out_specs=[full2(D_HALF)] * 8,
        out_shape=[jax.ShapeDtypeStruct((B, D_HALF), F32)] * 8,
        compiler_params=_params(("arbitrary",)),
        name="rwkv_pre_decode",
    )(pr, _pad_lanes(shift0, RW_COLS_PAD), *ws)
    return outs


RW_IP = RW_DH // 2
RW_TC = 64
RW_UNROLL = 8


def _rwkv_rec_kernel(r_ref, d_ref, k_ref, a_ref, b_ref, v_ref, S0_ref, y_ref, S_ref, *, Tc):
    @pl.when(pl.program_id(0) == 0)
    def _():
        S_ref[...] = S0_ref[...]

    lane = lax.broadcasted_iota(jnp.int32, (1, LANES), 1)

    def tiles(t):
        back = (LANES - (t % RW_TB) * RW_BB) % LANES
        out = []
        for ref in (a_ref, d_ref, b_ref, k_ref, r_ref):
            raw = ref[t]
            out.append(jnp.where(lane < LANES // 2, pltpu.roll(raw, back, 1),
                                 pltpu.roll(raw, (back + LANES // 2) % LANES, 1)))
        out.append(pltpu.roll(v_ref[t], back, 1))
        return tuple(out)

    def step(t, carry):
        a, d, b, k, r, vt = carry
        nxt = tiles(jnp.minimum(t + 1, Tc - 1))
        sas = [jnp.sum(S_ref[ip] * a, axis=0, keepdims=True) for ip in range(RW_IP)]
        rows = []
        for ip in range(RW_IP):
            S = S_ref[ip]
            sa = sas[ip]
            Sn = S * d + sa * b + vt[ip:ip + 1, :] * k
            S_ref[ip] = Sn
            rows.append(jnp.sum(Sn * r, axis=0, keepdims=True))
        y_ref[t] = jnp.concatenate(rows, axis=0)
        return nxt

    lax.fori_loop(0, Tc, step, tiles(0), unroll=RW_UNROLL)


def _rwkv_rec_call(r, d, k, a, b, v, S0):
    T = r.shape[0]
    Tc = min(T, RW_TC)
    vec = pl.BlockSpec((Tc, RW_DH, LANES), lambda c: (c, 0, 0))
    vsp = pl.BlockSpec((Tc, RW_IP, LANES), lambda c: (c, 0, 0))
    ssp = _full((RW_IP, RW_DH, LANES))
    return pl.pallas_call(
        functools.partial(_rwkv_rec_kernel, Tc=Tc),
        grid=(T // Tc,),
        in_specs=[vec] * 5 + [vsp, ssp],
        out_specs=[vsp, ssp],
        out_shape=[jax.ShapeDtypeStruct((T, RW_IP, LANES), F32),
                   jax.ShapeDtypeStruct((RW_IP, RW_DH, LANES), F32)],
        compiler_params=_params(("arbitrary",)),
        name="rwkv_recurrence",
    )(r, d, k, a, b, v, S0)


def _rwkv_dec_kernel(r_ref, d_ref, k_ref, a_ref, b_ref, v_ref, S0_ref, y_ref, S_ref):
    a, d, b, k, r = a_ref[...], d_ref[...], b_ref[...], k_ref[...], r_ref[...]
    v = v_ref[...]
    rows = []
    for i in range(RW_DH):
        S = S0_ref[i]
        sa = jnp.sum(S * a, axis=0, keepdims=True)
        Sn = S * d + sa * b + v[i:i + 1, :] * k
        S_ref[i] = Sn
        rows.append(jnp.sum(Sn * r, axis=0, keepdims=True))
    y_ref[...] = jnp.concatenate(rows, axis=0)


def _rwkv_decode_step(r, d, k, v, a, b, S0):
    B = r.shape[0]
    tr = lambda x: x.T.reshape(RW_HEADS, RW_DH, B)
    St = S0.reshape(B, RW_HEADS * RW_DH * RW_DH).T.reshape(RW_HEADS, RW_DH, RW_DH, B)
    vec = pl.BlockSpec((None, RW_DH, B), lambda h: (h, 0, 0))
    ssp = pl.BlockSpec((None, RW_DH, RW_DH, B), lambda h: (h, 0, 0, 0))
    y, S = pl.pallas_call(
        _rwkv_dec_kernel,
        grid=(RW_HEADS,),
        in_specs=[vec] * 6 + [ssp],
        out_specs=[vec, ssp],
        out_shape=[jax.ShapeDtypeStruct((RW_HEADS, RW_DH, B), F32),
                   jax.ShapeDtypeStruct((RW_HEADS, RW_DH, RW_DH, B), F32)],
        compiler_params=_params(("parallel",)),
        name="rwkv_decode_step",
    )(tr(r), tr(d), tr(k), tr(a), tr(b), tr(v), St)
    y = y.reshape(D_HALF, B).T
    S = S.reshape(RW_HEADS * RW_DH * RW_DH, B).T.reshape(B, RW_HEADS, RW_DH, RW_DH)
    return y, S


RW_TB = LANES // RW_BB
RW_NSB = 2
RW_MM_ROWS = 256


def _head_sum_rows(x):
    x3 = x.reshape(RW_HEADS, RW_DH, x.shape[-1])
    s = jnp.sum(x3, axis=1, keepdims=True)
    return jnp.broadcast_to(s, x3.shape).reshape(x.shape)


def _rwkv_pre_t_kernel(x_ref, shift0_ref, w_ref, mu_ref, w0_ref, a0_ref, w2_ref, a2_ref, g2_ref,
                       kkw_ref, kaw_ref, rk_ref,
                       r_ref, d_ref, k_ref, a_ref, b_ref, v_ref, g_ref, bonus_ref, last_ref, prev_scr):
    @pl.when(pl.program_id(0) == 0)
    def _():
        prev_scr[...] = shift0_ref[...]

    ro = lax.broadcasted_iota(jnp.int32, (LANES, LANES), 0)
    ci = lax.broadcasted_iota(jnp.int32, (LANES, LANES), 1)
    perm = jnp.where(ci == (ro % RW_BB) * RW_TB + ro // RW_BB, 1.0, 0.0).astype(BF16)
    lane = lax.broadcasted_iota(jnp.int32, (1, LANES), 1)
    grp = lane // RW_BB
    ngrp = LANES // RW_BB
    prs = {}

    def scatter(x, o_ref, t0, nrow, npiece):
        rot = [x[q * nrow:(q + 1) * nrow, :] if q == 0 else pltpu.roll(x[q * nrow:(q + 1) * nrow, :], q * RW_BB, 1)
               for q in range(npiece)]
        for t in range(RW_TB):
            m = rot[0]
            for q in range(1, npiece):
                m = jnp.where(grp == (t + q) % ngrp, rot[q], m)
            o_ref[t0 + t] = m

    def block(sb):
        t0 = sb * RW_TB
        xn = x_ref[:, t0:t0 + RW_TB, :].reshape(RW_BB * RW_TB, D_MODEL).astype(BF16)
        xg = jnp.dot(perm, xn, preferred_element_type=F32).astype(BF16)
        parts = []
        for r0 in range(0, RW_COLS_PAD, RW_MM_ROWS):
            parts.append(lax.dot_general(w_ref[r0:r0 + RW_MM_ROWS, :], xg, (((1,), (1,)), ((), ())),
                                         preferred_element_type=F32))
            yield
        pr = jnp.concatenate(parts, axis=0)
        prs[sb] = pr
        rolled = pltpu.roll(pr, RW_BB, 1)
        before = prev_scr[...] if sb == 0 else prs[sb - 1]
        prev = jnp.where(lane < RW_BB, before, rolled)
        prs[sb] = rolled
        if sb == RW_NSB - 1:
            prev_scr[...] = rolled
            last_ref[...] = pr
        pm = pr + (prev - pr) * mu_ref[...]
        r = pm[0:D_HALF]
        kr = pm[D_HALF:2 * D_HALF]
        vr = pm[2 * D_HALF:3 * D_HALF]
        lo = pm[3 * D_HALF:RW_COLS_PAD]
        yield
        w_log = -_softplus(-(w0_ref[...] + _dot(w2_ref[...], jnp.tanh(lo)))) - 0.5
        a = _sigmoid(a0_ref[...] + _dot(a2_ref[...], lo))
        g = _dot(g2_ref[...], _sigmoid(lo))
        kk = kr * kkw_ref[...]
        kk = kk / jnp.maximum(jnp.sqrt(_head_sum_rows(kk * kk)), 1e-12)
        kh = kr * (1.0 + (a - 1.0) * kaw_ref[...])
        g_ref[sb] = g
        bonus_ref[sb] = _head_sum_rows(r * kh * rk_ref[...]) * vr
        yield
        scatter(r, r_ref, t0, RW_DH, RW_HEADS)
        yield
        scatter(jnp.exp(-jnp.exp(w_log)), d_ref, t0, RW_DH, RW_HEADS)
        yield
        scatter(kh, k_ref, t0, RW_DH, RW_HEADS)
        yield
        scatter(-kk, a_ref, t0, RW_DH, RW_HEADS)
        yield
        scatter(kk * a, b_ref, t0, RW_DH, RW_HEADS)
        yield
        vv = jnp.concatenate([vr[h * RW_DH + half * RW_IP:h * RW_DH + (half + 1) * RW_IP, :]
                              for half in range(2) for h in range(RW_HEADS)], axis=0)
        scatter(vv, v_ref, t0, RW_IP, ngrp)

    _skewed([block(sb) for sb in range(RW_NSB)], lag=RW_COLS_PAD // RW_MM_ROWS)


def _lane_bcast(a, n):
    return jnp.broadcast_to(a.reshape(n, 1), (n, LANES))


def _rwkv_pre_t(x3, shift0, w_rwt, j, Wl):
    B, T, _ = x3.shape
    nblk = T // RW_TB
    padr = lambda w, o: jnp.pad(w, ((o, RW_LORA_PAD - o - w.shape[0]), (0, 0))).astype(BF16).T
    sh = jnp.pad(shift0.T, ((0, RW_COLS_PAD - RW_SHIFT_COLS), (0, LANES - RW_BB)))
    col = lambda a: _lane_bcast(a, D_HALF)
    ins = [x3, sh, w_rwt, _lane_bcast(_pad_lanes(Wl['rw_mu'].reshape(1, -1), RW_COLS_PAD), RW_COLS_PAD),
           col(Wl['rw_w0']), col(Wl['rw_a0']),
           padr(Wl['rw_w2'], 0), padr(Wl['rw_a2'], RW_DECAY_LORA), padr(Wl['rw_g2'], RW_DECAY_LORA + RW_A_LORA),
           col(Wl['rw_kk']), col(Wl['rw_ka']), col(Wl['rw_rk'])]
    tb = RW_TB * RW_NSB
    in_specs = [pl.BlockSpec((B, tb, D_MODEL), lambda c: (0, c, 0)), _full((RW_COLS_PAD, LANES)),
                pl.BlockSpec((None, RW_COLS_PAD, D_MODEL), lambda c: (j, 0, 0)), _full((RW_COLS_PAD, LANES)),
                _full((D_HALF, LANES)), _full((D_HALF, LANES)),
                _full((D_HALF, RW_LORA_PAD)), _full((D_HALF, RW_LORA_PAD)), _full((D_HALF, RW_LORA_PAD)),
                _full((D_HALF, LANES)), _full((D_HALF, LANES)), _full((D_HALF, LANES))]
    blk = pl.BlockSpec((RW_NSB, D_HALF, LANES), lambda c: (c, 0, 0))
    ktile = pl.BlockSpec((tb, RW_DH, LANES), lambda c: (c, 0, 0))
    vtile = pl.BlockSpec((tb, RW_IP, LANES), lambda c: (c, 0, 0))
    outs = pl.pallas_call(
        _rwkv_pre_t_kernel,
        grid=(T // tb,),
        in_specs=in_specs,
        out_specs=[ktile] * 5 + [vtile, blk, blk, _full((RW_COLS_PAD, LANES))],
        out_shape=[jax.ShapeDtypeStruct((T, RW_DH, LANES), F32)] * 5
                  + [jax.ShapeDtypeStruct((T, RW_IP, LANES), F32)]
                  + [jax.ShapeDtypeStruct((nblk, D_HALF, LANES), F32)] * 2
                  + [jax.ShapeDtypeStruct((RW_COLS_PAD, LANES), F32)],
        scratch_shapes=[pltpu.VMEM((RW_COLS_PAD, LANES), F32)],
        compiler_params=_params(("arbitrary",)),
        name="rwkv_pre_t",
    )(*ins)
    shift_new = outs[8][:RW_SHIFT_COLS, LANES - RW_BB:].T
    return outs[:8], shift_new


RW_NPB = 8


def _rwkv_post_t_kernel(y_ref, g_ref, bonus_ref, gng_ref, gnb_ref, yd_ref):
    _round_robin([_rwkv_post_t_block(y_ref, g_ref, bonus_ref, gng_ref, gnb_ref, yd_ref, pb)
                  for pb in range(RW_NPB)])


def _rwkv_post_t_block(y_ref, g_ref, bonus_ref, gng_ref, gnb_ref, yd_ref, pb):
    t0 = pb * RW_TB
    lane = lax.broadcasted_iota(jnp.int32, (1, LANES), 1)
    grp = lane // RW_BB
    ngrp = LANES // RW_BB
    ys = [y_ref[t0 + t] for t in range(RW_TB)]
    rolled = []
    for s in range(ngrp):
        m = ys[s % RW_TB]
        for q in range(1, ngrp):
            m = jnp.where(grp == q, ys[(q + s) % RW_TB], m)
        rolled.append(pltpu.roll(m, s * RW_BB, 1) if s else m)
    yield
    pieces = {}
    for q in range(ngrp):
        m = rolled[(-q) % ngrp]
        for t in range(1, RW_TB):
            m = jnp.where(grp == t, rolled[(t - q) % ngrp], m)
        pieces[divmod(q, RW_HEADS)] = m
    y = jnp.concatenate([pieces[(half, h)] for h in range(RW_HEADS) for half in range(2)], axis=0)
    yield
    mu = _head_sum_rows(y) * (1.0 / RW_DH)
    yc = y - mu
    var = _head_sum_rows(yc * yc) * (1.0 / RW_DH)
    hn = yc * lax.rsqrt(var + RW_GN_EPS) * gng_ref[...] + gnb_ref[...]
    yd = ((hn + bonus_ref[pb]) * g_ref[pb]).T
    yield
    for t in range(RW_TB):
        yd_ref[:, t0 + t, :] = yd[t * RW_BB:(t + 1) * RW_BB, :]


def _rwkv_rec_t(vecs, S0, Wl, T):
    r, d, k, a, b, v, g, bonus = vecs
    nblk = T // RW_TB
    B = RW_BB
    Sr = S0.reshape(B, RW_HEADS, 2, RW_IP, RW_DH).transpose(3, 4, 2, 1, 0).reshape(RW_IP, RW_DH, LANES)
    y, S = _rwkv_rec_call(r, d, k, a, b, v, Sr)
    tb = RW_TB * RW_NPB
    blk = pl.BlockSpec((RW_NPB, D_HALF, LANES), lambda c: (c, 0, 0))
    col = lambda a_: _lane_bcast(a_, D_HALF)
    yd = pl.pallas_call(
        _rwkv_post_t_kernel,
        grid=(nblk // RW_NPB,),
        in_specs=[pl.BlockSpec((tb, RW_IP, LANES), lambda c: (c, 0, 0)), blk, blk,
                  _full((D_HALF, LANES)), _full((D_HALF, LANES))],
        out_specs=pl.BlockSpec((B, tb, D_HALF), lambda c: (0, c, 0)),
        out_shape=jax.ShapeDtypeStruct((B, T, D_HALF), F32),
        compiler_params=_params(("parallel",)),
        name="rwkv_post_t",
    )(y, g, bonus, col(Wl['rw_gn_g']), col(Wl['rw_gn_b']))
    S = S.reshape(RW_IP, RW_DH, 2, RW_HEADS, B).transpose(4, 3, 2, 0, 1).reshape(B, RW_HEADS, RW_DH, RW_DH)
    return yd, S


def _rwkv_post_kernel(y_ref, g_ref, bonus_ref, gng_ref, gnb_ref, o_ref):
    ones_bd = _group_ones(LANES, RW_DH)
    y = y_ref[...]
    mu = _group_sum(y, ones_bd) * (1.0 / RW_DH)
    yc = y - mu
    var = _group_sum(yc * yc, ones_bd) * (1.0 / RW_DH)
    hn = yc * lax.rsqrt(var + RW_GN_EPS) * gng_ref[...] + gnb_ref[...]
    o_ref[...] = (hn + bonus_ref[...]) * g_ref[...]


def _rwkv_post(y2, g2, bonus2, Wl, tm):
    M = y2.shape[0]
    row = pl.BlockSpec((tm, D_HALF), lambda i: (i, 0))
    return pl.pallas_call(
        _rwkv_post_kernel,
        grid=(M // tm,),
        in_specs=[row, row, row, _full((1, D_HALF)), _full((1, D_HALF))],
        out_specs=row,
        out_shape=jax.ShapeDtypeStruct((M, D_HALF), F32),
        compiler_params=_params(("parallel",)),
        name="rwkv_post",
    )(y2, g2, bonus2, Wl['rw_gn_g'].reshape(1, D_HALF), Wl['rw_gn_b'].reshape(1, D_HALF))


def _trunk(x, st, pos, W):
    B, T, _ = x.shape
    M = B * T
    decode = T == 1
    tm_proj = min(M, 512)
    tm_post = min(M, 2 * POST_SUB)
    tm_rw = min(M, 512)
    x2 = x.reshape(M, D_MODEL)
    new = {name: [] for name in st}
    stacked = {}
    for l in range(DEPTH):
        j = l // 2
        Wl = {name: v[j] for name, v in W['per_pair'][l % 2].items()}
        stl = {name: v[j] for name, v in st.items()}
        if l % 2 == 0:
            if decode:
                p, = _proj(x2, W['ev_w_in'], j, tm_proj, (EVEN_IN,))
                ya, yb, (h, cb, S) = _even_decode(p, pos, stl, Wl, st['ret_S'], j, stacked.get('ret_S'))
                stacked['ret_S'] = S
            else:
                ya, yb, (h, cb, S) = _even_prefill(x2.reshape(B, T, D_MODEL), W['ev_w_in'], j, pos, stl, Wl)
                new['ret_S'].append(S)
            new['lru_h'].append(h)
            new['lru_conv'].append(cb)
            wout = W['ev_w_out']
        else:
            if decode:
                p_ml, p_rw = _proj(x2, W['od_w_in'], j, tm_proj, (2 * D_HALF, RW_COLS_PAD))
                ya, (C, n, m, cb) = _mlstm_decode(p_ml, stl, Wl, st['ml_C'], j, stacked.get('ml_C'))
                stacked['ml_C'] = C
                r, d, k, v, a, b, g, bonus = _rwkv_pre_decode(p_rw, stl['rw_shift'], Wl)
                y, S = _rwkv_decode_step(r, d, k, v, a, b, stl['rw_S'])
                yb = _rwkv_post(y, g, bonus, Wl, tm_rw)
                shift_new = p_rw[:, :RW_SHIFT_COLS]
            else:
                x3 = x2.reshape(B, T, D_MODEL)
                ya, (C, n, m, cb) = _mlstm_prefill(x3, W['od_w_in'], j, stl, Wl)
                vecs, shift_new = _rwkv_pre_t(x3, stl['rw_shift'], W['od_w_rwt'], j, Wl)
                yb, S = _rwkv_rec_t(vecs, stl['rw_S'], Wl, T)
                new['ml_C'].append(C)
            new['ml_n'].append(n)
            new['ml_m'].append(m)
            new['ml_conv'].append(cb)
            new['rw_S'].append(S)
            new['rw_shift'].append(shift_new)
            wout = W['od_w_out']
        x2 = _post(ya.reshape(M, D_HALF), yb.reshape(M, D_HALF), x2, wout, j, l,
                   W['ln1_g'], W['ln1_b'], W['mlp_w1'], W['mlp_w2'], W['ln2_g'], W['ln2_b'], tm_post)
    out = {name: stacked[name] if name in stacked else jnp.stack(v) for name, v in new.items()}
    return x2.reshape(B, T, D_MODEL), out


def _prepare_weights(w):
    even_names = ('lru_conv_w', 'lru_conv_b', 'lru_ba', 'lru_bx', 'lru_lambda', 'ret_gn_g', 'ret_gn_b')
    odd_names = ('ml_conv_w', 'ml_conv_b', 'ml_w_gate', 'ml_b_gate', 'ml_gn_g', 'ml_gn_b', 'ml_skip',
                 'rw_mu', 'rw_w0', 'rw_w2', 'rw_a0', 'rw_a2', 'rw_g2', 'rw_kk', 'rw_ka', 'rw_rk', 'rw_gn_g', 'rw_gn_b')
    even = {n: w[n] for n in even_names}
    even.update(lru_wa_d=jax.vmap(_lru_dense)(w['lru_wa']), lru_wx_d=jax.vmap(_lru_dense)(w['lru_wx']))
    odd = {n: w[n] for n in odd_names}
    odd.update(ml_wq_d=jax.vmap(_ml_dense)(w['ml_wq']), ml_wk_d=jax.vmap(_ml_dense)(w['ml_wk']),
               ml_wv_d=jax.vmap(_ml_dense)(w['ml_wv']))
    od_w_in = _pad_lanes(w['od_w_in'], ODD_IN_PAD).astype(BF16)
    return dict(per_pair=(even, odd),
                ev_w_in=w['ev_w_in'].astype(BF16), ev_w_out=w['ev_w_out'].astype(BF16),
                od_w_in=od_w_in, od_w_rwt=jnp.swapaxes(od_w_in[:, :, 2 * D_HALF:], 1, 2),
                od_w_out=w['od_w_out'].astype(BF16),
                mlp_w1=w['mlp_w1'].astype(BF16), mlp_w2=w['mlp_w2'].astype(BF16),
                ln1_g=w['ln1_g'], ln1_b=w['ln1_b'], ln2_g=w['ln2_g'], ln2_b=w['ln2_b'])


def _zero_states(batch):
    z = lambda *s: jnp.zeros(s, F32)
    n_even, n_odd = (DEPTH + 1) // 2, DEPTH // 2
    return dict(lru_h=z(n_even, batch, D_HALF), lru_conv=z(n_even, batch, CONV_W - 1, D_HALF),
                ret_S=z(n_even, batch, RET_HEADS, RET_DH, RET_DH),
                ml_C=z(n_odd, batch, ML_HEADS, ML_DH, ML_DH), ml_n=z(n_odd, batch, ML_HEADS, ML_DH),
                ml_m=z(n_odd, batch, ML_HEADS), ml_conv=z(n_odd, batch, CONV_W - 1, D_HALF),
                rw_S=z(n_odd, batch, RW_HEADS, RW_DH, RW_DH), rw_shift=z(n_odd, batch, RW_SHIFT_COLS))


def kernel(x_prompt, x_sample, state_lru_h, state_lru_conv, state_ret, state_mlstm_C, state_mlstm_n, state_mlstm_m, state_mlstm_conv, state_rwkv_S, state_rwkv_shift, ln1_g, ln1_b, ln2_g, ln2_b, mlp_w1, mlp_w2, ev_w_in, ev_w_out, lru_conv_w, lru_conv_b, lru_wa, lru_ba, lru_wx, lru_bx, lru_lambda, ret_gn_g, ret_gn_b, od_w_in, od_w_out, ml_conv_w, ml_conv_b, ml_wq, ml_wk, ml_wv, ml_w_gate, ml_b_gate, ml_gn_g, ml_gn_b, ml_skip, rw_mu, rw_w0, rw_w2, rw_a0, rw_a2, rw_g2, rw_kk, rw_ka, rw_rk, rw_gn_g, rw_gn_b):
    W = _prepare_weights(dict(
        ln1_g=ln1_g, ln1_b=ln1_b, ln2_g=ln2_g, ln2_b=ln2_b, mlp_w1=mlp_w1, mlp_w2=mlp_w2,
        ev_w_in=ev_w_in, ev_w_out=ev_w_out, lru_conv_w=lru_conv_w, lru_conv_b=lru_conv_b,
        lru_wa=lru_wa, lru_ba=lru_ba, lru_wx=lru_wx, lru_bx=lru_bx, lru_lambda=lru_lambda,
        ret_gn_g=ret_gn_g, ret_gn_b=ret_gn_b, od_w_in=od_w_in, od_w_out=od_w_out,
        ml_conv_w=ml_conv_w, ml_conv_b=ml_conv_b, ml_wq=ml_wq, ml_wk=ml_wk, ml_wv=ml_wv,
        ml_w_gate=ml_w_gate, ml_b_gate=ml_b_gate, ml_gn_g=ml_gn_g, ml_gn_b=ml_gn_b, ml_skip=ml_skip,
        rw_mu=rw_mu, rw_w0=rw_w0, rw_w2=rw_w2, rw_a0=rw_a0, rw_a2=rw_a2, rw_g2=rw_g2,
        rw_kk=rw_kk, rw_ka=rw_ka, rw_rk=rw_rk, rw_gn_g=rw_gn_g, rw_gn_b=rw_gn_b))
    st_sample = dict(lru_h=state_lru_h, lru_conv=state_lru_conv, ret_S=state_ret,
                     ml_C=state_mlstm_C, ml_n=state_mlstm_n, ml_m=state_mlstm_m, ml_conv=state_mlstm_conv,
                     rw_S=state_rwkv_S, rw_shift=state_rwkv_shift)
    pos_prompt = jnp.arange(x_prompt.shape[1], dtype=jnp.int32)
    pos_sample = PAST_LEN + jnp.arange(x_sample.shape[1], dtype=jnp.int32)
    y_prompt, sp = _trunk(x_prompt, _zero_states(x_prompt.shape[0]), pos_prompt, W)
    y_sample, ss = _trunk(x_sample, st_sample, pos_sample, W)
    names = ('lru_h', 'lru_conv', 'ret_S', 'ml_C', 'ml_n', 'ml_m', 'ml_conv', 'rw_S', 'rw_shift')
    return (y_prompt, y_sample) + tuple(sp[n] for n in names) + tuple(ss[n] for n in names)
```
